```python
import jax, jax.numpy as jnp
from jax import lax
import numpy as np

D_MODEL = 1024
BATCH = 8
SEQ = 8192
DEPTH = 4

N_MIXERS = 4
N_META = 16
Q_BLOCK = 128
EPS = 1e-6
POOL_WINDOWS = (2, 4, 8, 16)
N_POOL_GROUPS = len(POOL_WINDOWS)
POOL_GROUP = D_MODEL // N_POOL_GROUPS
N_HEADS = 16
HEAD_DIM = D_MODEL // N_HEADS
MLA_HEADS = 16
MLA_Q_RANK = 384
MLA_KV_RANK = 256
MLA_NOPE = 64
MLA_ROPE = 32
MLA_V = 64
ROPE_THETA = 10000.0
D_FF = ((-(-8 * D_MODEL // 3) + 255) // 256) * 256

kernel_name = "hybrid_pool_sb_mla_fox_trunk"


def _n_layers_of(m):
    return len(range(m, DEPTH, N_MIXERS))


def rmsnorm(x, g):
    xf = x.astype(jnp.float32)
    y = xf * lax.rsqrt(jnp.mean(xf * xf, axis=-1, keepdims=True) + EPS)
    return (y * g.astype(jnp.float32)).astype(x.dtype)


def swiglu(h, w_gate, w_up, w_down):
    return (jax.nn.silu(h @ w_gate) * (h @ w_up)) @ w_down


def sweep_queries(attend, q_parts, kv_parts):
    L = q_parts[0].shape[1]
    pos = jnp.arange(L)
    meta_out = attend(tuple(a[:, :N_META] for a in q_parts), pos[:N_META],
                      tuple(a[:, :N_META] for a in kv_parts), pos[:N_META])
    n_blk = (L - N_META) // Q_BLOCK

    def body(i):
        start = N_META + i * Q_BLOCK
        qs = tuple(lax.dynamic_slice_in_dim(a, start, Q_BLOCK, axis=1) for a in q_parts)
        return attend(qs, start + jnp.arange(Q_BLOCK), kv_parts, pos)

    out = lax.map(body, jnp.arange(n_blk))
    B = out.shape[1]
    out = jnp.moveaxis(out, 0, 1).reshape((B, n_blk * Q_BLOCK) + out.shape[3:])
    return jnp.concatenate([meta_out, out], axis=1)


def softmax_block(q, k, v, qpos, kpos, scale, q_decay=None, k_decay=None):
    s = jnp.einsum('bqhd,bkhd->bhqk', q, k).astype(jnp.float32) * scale
    if q_decay is not None:
        s = s + (jnp.transpose(q_decay, (0, 2, 1))[:, :, :, None]
                 - jnp.transpose(k_decay, (0, 2, 1))[:, :, None, :]).astype(jnp.float32)
    mask = kpos[None, :] <= qpos[:, None]
    s = jnp.where(mask, s, jnp.finfo(jnp.float32).min)
    p = jax.nn.softmax(s, axis=-1)
    return jnp.einsum('bhqk,bkhd->bqhd', p.astype(v.dtype), v)


def pool_mixer(h, w, scale):
    B, L, _ = h.shape
    hf = h.astype(jnp.float32)
    pos = jnp.arange(L)
    outs = []
    for g, win in enumerate(POOL_WINDOWS):
        xg = hf[..., g * POOL_GROUP:(g + 1) * POOL_GROUP]
        cs = jnp.cumsum(xg, axis=1)
        lag = jnp.pad(cs[:, :-win], ((0, 0), (win, 0), (0, 0)))
        cnt = jnp.minimum(pos + 1, win).astype(jnp.float32)[None, :, None]
        outs.append((cs - lag) / cnt - xg)
    pooled = jnp.stack(outs, axis=2).astype(h.dtype)
    mixed = jnp.einsum('blgc,gcd->blgd', pooled, w).reshape(B, L, D_MODEL)
    return mixed * scale


def _sb_attend(qs, qpos, kvs, kpos):
    (q,) = qs
    k, v = kvs
    z = jnp.einsum('bqhd,bkhd->bhqk', q, k).astype(jnp.float32) * (HEAD_DIM ** -0.5)
    mask = kpos[None, :] < qpos[:, None]
    log_keep = jnp.where(mask, jax.nn.log_sigmoid(-z), 0.0)
    later = lax.cumsum(log_keep, axis=3, reverse=True) - log_keep
    a = jnp.where(mask, jnp.exp(jax.nn.log_sigmoid(z) + later), 0.0)
    return jnp.einsum('bhqk,bkhd->bqhd', a.astype(v.dtype), v)


def sb_mixer(h, w_qkv, w_o):
    B, L, _ = h.shape
    qkv = (h @ w_qkv).reshape(B, L, 3, N_HEADS, HEAD_DIM)
    q, k, v = qkv[:, :, 0], qkv[:, :, 1], qkv[:, :, 2]
    o = sweep_queries(_sb_attend, (q,), (k, v))
    return o.reshape(B, L, N_HEADS * HEAD_DIM) @ w_o


def _rope(x, cos, sin):
    xf = x.astype(jnp.float32)
    half = xf.shape[-1] // 2
    x1, x2 = xf[..., :half], xf[..., half:]
    return jnp.concatenate([x1 * cos - x2 * sin, x2 * cos + x1 * sin], axis=-1).astype(x.dtype)


def _mla_attend(qs, qpos, kvs, kpos):
    (q,) = qs
    k, v = kvs
    return softmax_block(q, k, v, qpos, kpos, (MLA_NOPE + MLA_ROPE) ** -0.5)


def mla_mixer(h, w_down, q_norm, kv_norm, w_uq, w_ukv, w_o):
    B, L, _ = h.shape
    down = h @ w_down
    c_q = rmsnorm(down[..., :MLA_Q_RANK], q_norm)
    c_kv = rmsnorm(down[..., MLA_Q_RANK:MLA_Q_RANK + MLA_KV_RANK], kv_norm)
    k_rope = down[..., MLA_Q_RANK + MLA_KV_RANK:]
    q = (c_q @ w_uq).reshape(B, L, MLA_HEADS, MLA_NOPE + MLA_ROPE)
    kv = (c_kv @ w_ukv).reshape(B, L, MLA_HEADS, MLA_NOPE + MLA_V)
    q_nope, q_rope = q[..., :MLA_NOPE], q[..., MLA_NOPE:]
    k_nope, v = kv[..., :MLA_NOPE], kv[..., MLA_NOPE:]
    inv = ROPE_THETA ** (-jnp.arange(0, MLA_ROPE, 2, dtype=jnp.float32) / MLA_ROPE)
    ang = jnp.arange(L, dtype=jnp.float32)[:, None] * inv[None, :]
    cos, sin = jnp.cos(ang), jnp.sin(ang)
    q_rope = _rope(q_rope, cos[:, None, :], sin[:, None, :])
    k_rope = _rope(k_rope, cos, sin)
    q = jnp.concatenate([q_nope, q_rope], axis=-1)
    k = jnp.concatenate([k_nope, jnp.broadcast_to(k_rope[:, :, None, :], (B, L, MLA_HEADS, MLA_ROPE))], axis=-1)
    o = sweep_queries(_mla_attend, (q,), (k, v))
    return o.reshape(B, L, MLA_HEADS * MLA_V) @ w_o


def _fox_attend(qs, qpos, kvs, kpos):
    q, fq = qs
    k, v, fk = kvs
    return softmax_block(q, k, v, qpos, kpos, HEAD_DIM ** -0.5, fq, fk)


def fox_mixer(h, w_qkvf, b_f, w_o):
    B, L, _ = h.shape
    proj = h @ w_qkvf
    qkv = proj[..., :3 * N_HEADS * HEAD_DIM].reshape(B, L, 3, N_HEADS, HEAD_DIM)
    q, k, v = qkv[:, :, 0], qkv[:, :, 1], qkv[:, :, 2]
    f_logit = proj[..., 3 * N_HEADS * HEAD_DIM:].astype(jnp.float32) + b_f.astype(jnp.float32)
    F = jnp.cumsum(jax.nn.log_sigmoid(f_logit), axis=1)
    o = sweep_queries(_fox_attend, (q, F), (k, v, F))
    return o.reshape(B, L, N_HEADS * HEAD_DIM) @ w_o


def _fwd_setup_inputs(seed: int = 0) -> dict:
    key = jax.random.key(seed)
    ks = jax.random.split(key, 24)
    f32 = jnp.float32

    def w(k, shape, fan_in):
        return jax.random.normal(k, shape, f32) * (fan_in ** -0.5)

    def gain(k, shape):
        return 1.0 + 0.02 * jax.random.normal(k, shape, f32)

    nA, nB, nC, nD = (_n_layers_of(m) for m in range(N_MIXERS))
    D = D_MODEL
    return {
        "x": jax.random.normal(ks[0], (BATCH, SEQ, D), f32),
        "meta": jax.random.normal(ks[1], (N_META, D), f32),
        "norm_mix": gain(ks[2], (DEPTH, D)),
        "norm_ffn": gain(ks[3], (DEPTH, D)),
        "pool_w": w(ks[4], (nA, N_POOL_GROUPS, POOL_GROUP, POOL_GROUP), POOL_GROUP),
        "pool_scale": gain(ks[5], (nA, D)),
        "sb_w_qkv": w(ks[6], (nB, D, 3 * N_HEADS * HEAD_DIM), D),
        "sb_w_o": w(ks[7], (nB, N_HEADS * HEAD_DIM, D), N_HEADS * HEAD_DIM),
        "mla_w_down": w(ks[8], (nC, D, MLA_Q_RANK + MLA_KV_RANK + MLA_ROPE), D),
        "mla_q_norm": gain(ks[9], (nC, MLA_Q_RANK)),
        "mla_kv_norm": gain(ks[10], (nC, MLA_KV_RANK)),
        "mla_w_uq": w(ks[11], (nC, MLA_Q_RANK, MLA_HEADS * (MLA_NOPE + MLA_ROPE)), MLA_Q_RANK),
        "mla_w_ukv": w(ks[12], (nC, MLA_KV_RANK, MLA_HEADS * (MLA_NOPE + MLA_V)), MLA_KV_RANK),
        "mla_w_o": w(ks[13], (nC, MLA_HEADS * MLA_V, D), MLA_HEADS * MLA_V),
        "fox_w_qkvf": w(ks[14], (nD, D, 3 * N_HEADS * HEAD_DIM + N_HEADS), D),
        "fox_b_f": 2.0 + 0.5 * jax.random.normal(ks[15], (nD, N_HEADS), f32),
        "fox_w_o": w(ks[16], (nD, N_HEADS * HEAD_DIM, D), N_HEADS * HEAD_DIM),
        "ffn_w_gate": w(ks[17], (DEPTH, D, D_FF), D),
        "ffn_w_up": w(ks[18], (DEPTH, D, D_FF), D),
        "ffn_w_down": w(ks[19], (DEPTH, D_FF, D), D_FF),
        "final_norm": gain(ks[20], (D,)),
    }


def _fwd_reference(x, meta, norm_mix, norm_ffn, pool_w, pool_scale, sb_w_qkv, sb_w_o,
              mla_w_down, mla_q_norm, mla_kv_norm, mla_w_uq, mla_w_ukv, mla_w_o,
              fox_w_qkvf, fox_b_f, fox_w_o, ffn_w_gate, ffn_w_up, ffn_w_down, final_norm):
    B = x.shape[0]
    meta_b = jnp.broadcast_to(meta[None].astype(x.dtype), (B, N_META, D_MODEL))
    h = jnp.concatenate([meta_b, x], axis=1)
    for i in range(DEPTH):
        m, j = i % N_MIXERS, i // N_MIXERS
        a = rmsnorm(h, norm_mix[i])
        if m == 0:
            mix = pool_mixer(a, pool_w[j], pool_scale[j])
        elif m == 1:
            mix = sb_mixer(a, sb_w_qkv[j], sb_w_o[j])
        elif m == 2:
            mix = mla_mixer(a, mla_w_down[j], mla_q_norm[j], mla_kv_norm[j],
                            mla_w_uq[j], mla_w_ukv[j], mla_w_o[j])
        else:
            mix = fox_mixer(a, fox_w_qkvf[j], fox_b_f[j], fox_w_o[j])
        h = h + mix
        h = h + swiglu(rmsnorm(h, norm_ffn[i]), ffn_w_gate[i], ffn_w_up[i], ffn_w_down[i])
    h = rmsnorm(h, final_norm)
    return h[:, N_META:]


import jax as _jax
import jax.numpy as _jnp

TWIN_FORMAT = 'train_step'
FWD_PARAMS = ['x', 'meta', 'norm_mix', 'norm_ffn', 'pool_w', 'pool_scale', 'sb_w_qkv', 'sb_w_o', 'mla_w_down', 'mla_q_norm', 'mla_kv_norm', 'mla_w_uq', 'mla_w_ukv', 'mla_w_o', 'fox_w_qkvf', 'fox_b_f', 'fox_w_o', 'ffn_w_gate', 'ffn_w_up', 'ffn_w_down', 'final_norm']
TWIN_WEIGHTS = ['meta', 'norm_mix', 'norm_ffn', 'pool_w', 'pool_scale', 'sb_w_qkv', 'sb_w_o', 'mla_w_down', 'mla_q_norm', 'mla_kv_norm', 'mla_w_uq', 'mla_w_ukv', 'mla_w_o', 'fox_w_qkvf', 'fox_b_f', 'fox_w_o', 'ffn_w_gate', 'ffn_w_up', 'ffn_w_down', 'final_norm']
TWIN_DIFF_INPUT = 'x'
TWIN_INPUTS = ['x', 'meta', 'norm_mix', 'norm_ffn', 'pool_w', 'pool_scale', 'sb_w_qkv', 'sb_w_o', 'mla_w_down', 'mla_q_norm', 'mla_kv_norm', 'mla_w_uq', 'mla_w_ukv', 'mla_w_o', 'fox_w_qkvf', 'fox_b_f', 'fox_w_o', 'ffn_w_gate', 'ffn_w_up', 'ffn_w_down', 'final_norm', 'loss_target', 'm_meta', 'm_norm_mix', 'm_norm_ffn', 'm_pool_w', 'm_pool_scale', 'm_sb_w_qkv', 'm_sb_w_o', 'm_mla_w_down', 'm_mla_q_norm', 'm_mla_kv_norm', 'm_mla_w_uq', 'm_mla_w_ukv', 'm_mla_w_o', 'm_fox_w_qkvf', 'm_fox_b_f', 'm_fox_w_o', 'm_ffn_w_gate', 'm_ffn_w_up', 'm_ffn_w_down', 'm_final_norm', 'v_meta', 'v_norm_mix', 'v_norm_ffn', 'v_pool_w', 'v_pool_scale', 'v_sb_w_qkv', 'v_sb_w_o', 'v_mla_w_down', 'v_mla_q_norm', 'v_mla_kv_norm', 'v_mla_w_uq', 'v_mla_w_ukv', 'v_mla_w_o', 'v_fox_w_qkvf', 'v_fox_b_f', 'v_fox_w_o', 'v_ffn_w_gate', 'v_ffn_w_up', 'v_ffn_w_down', 'v_final_norm']
TWIN_OUTPUTS = ['loss', 'grad_x', 'grad_meta', 'grad_norm_mix', 'grad_norm_ffn', 'grad_pool_w', 'grad_pool_scale', 'grad_sb_w_qkv', 'grad_sb_w_o', 'grad_mla_w_down', 'grad_mla_q_norm', 'grad_mla_kv_norm', 'grad_mla_w_uq', 'grad_mla_w_ukv', 'grad_mla_w_o', 'grad_fox_w_qkvf', 'grad_fox_b_f', 'grad_fox_w_o', 'grad_ffn_w_gate', 'grad_ffn_w_up', 'grad_ffn_w_down', 'grad_final_norm', 'delta_meta', 'delta_norm_mix', 'delta_norm_ffn', 'delta_pool_w', 'delta_pool_scale', 'delta_sb_w_qkv', 'delta_sb_w_o', 'delta_mla_w_down', 'delta_mla_q_norm', 'delta_mla_kv_norm', 'delta_mla_w_uq', 'delta_mla_w_ukv', 'delta_mla_w_o', 'delta_fox_w_qkvf', 'delta_fox_b_f', 'delta_fox_w_o', 'delta_ffn_w_gate', 'delta_ffn_w_up', 'delta_ffn_w_down', 'delta_final_norm', 'new_m_meta', 'new_m_norm_mix', 'new_m_norm_ffn', 'new_m_pool_w', 'new_m_pool_scale', 'new_m_sb_w_qkv', 'new_m_sb_w_o', 'new_m_mla_w_down', 'new_m_mla_q_norm', 'new_m_mla_kv_norm', 'new_m_mla_w_uq', 'new_m_mla_w_ukv', 'new_m_mla_w_o', 'new_m_fox_w_qkvf', 'new_m_fox_b_f', 'new_m_fox_w_o', 'new_m_ffn_w_gate', 'new_m_ffn_w_up', 'new_m_ffn_w_down', 'new_m_final_norm', 'new_v_meta', 'new_v_norm_mix', 'new_v_norm_ffn', 'new_v_pool_w', 'new_v_pool_scale', 'new_v_sb_w_qkv', 'new_v_sb_w_o', 'new_v_mla_w_down', 'new_v_mla_q_norm', 'new_v_mla_kv_norm', 'new_v_mla_w_uq', 'new_v_mla_w_ukv', 'new_v_mla_w_o', 'new_v_fox_w_qkvf', 'new_v_fox_b_f', 'new_v_fox_w_o', 'new_v_ffn_w_gate', 'new_v_ffn_w_up', 'new_v_ffn_w_down', 'new_v_final_norm']
TWIN_LEAF_KINDS = {'loss': 'loss', 'grad_x': 'grad_x', 'grad_meta': 'grad_w', 'grad_norm_mix': 'grad_w', 'grad_norm_ffn': 'grad_w', 'grad_pool_w': 'grad_w', 'grad_pool_scale': 'grad_w', 'grad_sb_w_qkv': 'grad_w', 'grad_sb_w_o': 'grad_w', 'grad_mla_w_down': 'grad_w', 'grad_mla_q_norm': 'grad_w', 'grad_mla_kv_norm': 'grad_w', 'grad_mla_w_uq': 'grad_w', 'grad_mla_w_ukv': 'grad_w', 'grad_mla_w_o': 'grad_w', 'grad_fox_w_qkvf': 'grad_w', 'grad_fox_b_f': 'grad_w', 'grad_fox_w_o': 'grad_w', 'grad_ffn_w_gate': 'grad_w', 'grad_ffn_w_up': 'grad_w', 'grad_ffn_w_down': 'grad_w', 'grad_final_norm': 'grad_w', 'delta_meta': 'delta_w', 'delta_norm_mix': 'delta_w', 'delta_norm_ffn': 'delta_w', 'delta_pool_w': 'delta_w', 'delta_pool_scale': 'delta_w', 'delta_sb_w_qkv': 'delta_w', 'delta_sb_w_o': 'delta_w', 'delta_mla_w_down': 'delta_w', 'delta_mla_q_norm': 'delta_w', 'delta_mla_kv_norm': 'delta_w', 'delta_mla_w_uq': 'delta_w', 'delta_mla_w_ukv': 'delta_w', 'delta_mla_w_o': 'delta_w', 'delta_fox_w_qkvf': 'delta_w', 'delta_fox_b_f': 'delta_w', 'delta_fox_w_o': 'delta_w', 'delta_ffn_w_gate': 'delta_w', 'delta_ffn_w_up': 'delta_w', 'delta_ffn_w_down': 'delta_w', 'delta_final_norm': 'delta_w', 'new_m_meta': 'new_m', 'new_m_norm_mix': 'new_m', 'new_m_norm_ffn': 'new_m', 'new_m_pool_w': 'new_m', 'new_m_pool_scale': 'new_m', 'new_m_sb_w_qkv': 'new_m', 'new_m_sb_w_o': 'new_m', 'new_m_mla_w_down': 'new_m', 'new_m_mla_q_norm': 'new_m', 'new_m_mla_kv_norm': 'new_m', 'new_m_mla_w_uq': 'new_m', 'new_m_mla_w_ukv': 'new_m', 'new_m_mla_w_o': 'new_m', 'new_m_fox_w_qkvf': 'new_m', 'new_m_fox_b_f': 'new_m', 'new_m_fox_w_o': 'new_m', 'new_m_ffn_w_gate': 'new_m', 'new_m_ffn_w_up': 'new_m', 'new_m_ffn_w_down': 'new_m', 'new_m_final_norm': 'new_m', 'new_v_meta': 'new_v', 'new_v_norm_mix': 'new_v', 'new_v_norm_ffn': 'new_v', 'new_v_pool_w': 'new_v', 'new_v_pool_scale': 'new_v', 'new_v_sb_w_qkv': 'new_v', 'new_v_sb_w_o': 'new_v', 'new_v_mla_w_down': 'new_v', 'new_v_mla_q_norm': 'new_v', 'new_v_mla_kv_norm': 'new_v', 'new_v_mla_w_uq': 'new_v', 'new_v_mla_w_ukv': 'new_v', 'new_v_mla_w_o': 'new_v', 'new_v_fox_w_qkvf': 'new_v', 'new_v_fox_b_f': 'new_v', 'new_v_fox_w_o': 'new_v', 'new_v_ffn_w_gate': 'new_v', 'new_v_ffn_w_up': 'new_v', 'new_v_ffn_w_down': 'new_v', 'new_v_final_norm': 'new_v'}


def _forward(args):
    return _fwd_reference(*[args[k] for k in FWD_PARAMS])


def _output_shape():
    def fwd():
        inp = _fwd_setup_inputs(0)
        return _fwd_reference(*[inp[k] for k in FWD_PARAMS])
    out = _jax.eval_shape(fwd)
    return out.shape, out.dtype

N_MICROBATCH = 1
ADAM_LR = 0.001
ADAM_B1 = 0.9
ADAM_B2 = 0.999
ADAM_EPS = 1e-08
ADAM_WD = 0.01
ADAM_STEP = 10
PER_EXAMPLE_BATCH_AXIS = {'x': 0, 'loss_target': 0}
SHARED_INPUTS = []
_WEIGHT_DTYPES = {'meta': _jnp.float32, 'norm_mix': _jnp.float32, 'norm_ffn': _jnp.float32, 'pool_w': _jnp.float32, 'pool_scale': _jnp.float32, 'sb_w_qkv': _jnp.float32, 'sb_w_o': _jnp.float32, 'mla_w_down': _jnp.float32, 'mla_q_norm': _jnp.float32, 'mla_kv_norm': _jnp.float32, 'mla_w_uq': _jnp.float32, 'mla_w_ukv': _jnp.float32, 'mla_w_o': _jnp.float32, 'fox_w_qkvf': _jnp.float32, 'fox_b_f': _jnp.float32, 'fox_w_o': _jnp.float32, 'ffn_w_gate': _jnp.float32, 'ffn_w_up': _jnp.float32, 'ffn_w_down': _jnp.float32, 'final_norm': _jnp.float32}
MOMENT_SCALE = {'meta': 7.770540e-03, 'norm_mix': 1.445248e-01, 'norm_ffn': 1.474060e-01, 'pool_w': 2.258723e-01, 'pool_scale': 3.586330e-01, 'sb_w_qkv': 8.551800e-02, 'sb_w_o': 1.222888e-01, 'mla_w_down': 5.828667e-02, 'mla_q_norm': 4.379040e-02, 'mla_kv_norm': 7.773925e-02, 'mla_w_uq': 2.226113e-02, 'mla_w_ukv': 2.642581e-02, 'mla_w_o': 2.932386e-02, 'fox_w_qkvf': 4.875269e-02, 'fox_b_f': 2.221855e-01, 'fox_w_o': 5.613110e-02, 'ffn_w_gate': 6.323362e-02, 'ffn_w_up': 6.120534e-02, 'ffn_w_down': 1.013922e-01, 'final_norm': 6.417353e+01}


def _to_microbatches(a, axis):
    t = _jnp.moveaxis(a, axis, 0)
    t = t.reshape((N_MICROBATCH, t.shape[0] // N_MICROBATCH) + t.shape[1:])
    return _jnp.moveaxis(t, 1, axis + 1)


def setup_inputs(seed: int = 0) -> dict:
    inp = _fwd_setup_inputs(seed)
    key = _jax.random.fold_in(_jax.random.key(seed), 7919)
    shape, _ = _output_shape()
    out = dict(inp)
    out["loss_target"] = _jax.random.normal(_jax.random.fold_in(key, 0), shape, _jnp.float32)
    for i, name in enumerate(TWIN_WEIGHTS):
        w = inp[name].astype(_jnp.float32)
        if MOMENT_SCALE is None:
            s = _jnp.sqrt(_jnp.mean(_jnp.square(w)) + 1e-30)
        else:
            s = MOMENT_SCALE[name]
        km, kv = _jax.random.split(_jax.random.fold_in(key, i + 1))
        out[name] = w
        out["m_" + name] = s * _jax.random.normal(km, w.shape, _jnp.float32)
        out["v_" + name] = (s * s) * _jax.random.uniform(kv, w.shape, _jnp.float32, 0.5, 1.5)
    if N_MICROBATCH > 1:
        for name, axis in PER_EXAMPLE_BATCH_AXIS.items():
            out[name] = _to_microbatches(out[name], axis)
    return {'x': out['x'], 'meta': out['meta'], 'norm_mix': out['norm_mix'], 'norm_ffn': out['norm_ffn'], 'pool_w': out['pool_w'], 'pool_scale': out['pool_scale'], 'sb_w_qkv': out['sb_w_qkv'], 'sb_w_o': out['sb_w_o'], 'mla_w_down': out['mla_w_down'], 'mla_q_norm': out['mla_q_norm'], 'mla_kv_norm': out['mla_kv_norm'], 'mla_w_uq': out['mla_w_uq'], 'mla_w_ukv': out['mla_w_ukv'], 'mla_w_o': out['mla_w_o'], 'fox_w_qkvf': out['fox_w_qkvf'], 'fox_b_f': out['fox_b_f'], 'fox_w_o': out['fox_w_o'], 'ffn_w_gate': out['ffn_w_gate'], 'ffn_w_up': out['ffn_w_up'], 'ffn_w_down': out['ffn_w_down'], 'final_norm': out['final_norm'], 'loss_target': out['loss_target'], 'm_meta': out['m_meta'], 'm_norm_mix': out['m_norm_mix'], 'm_norm_ffn': out['m_norm_ffn'], 'm_pool_w': out['m_pool_w'], 'm_pool_scale': out['m_pool_scale'], 'm_sb_w_qkv': out['m_sb_w_qkv'], 'm_sb_w_o': out['m_sb_w_o'], 'm_mla_w_down': out['m_mla_w_down'], 'm_mla_q_norm': out['m_mla_q_norm'], 'm_mla_kv_norm': out['m_mla_kv_norm'], 'm_mla_w_uq': out['m_mla_w_uq'], 'm_mla_w_ukv': out['m_mla_w_ukv'], 'm_mla_w_o': out['m_mla_w_o'], 'm_fox_w_qkvf': out['m_fox_w_qkvf'], 'm_fox_b_f': out['m_fox_b_f'], 'm_fox_w_o': out['m_fox_w_o'], 'm_ffn_w_gate': out['m_ffn_w_gate'], 'm_ffn_w_up': out['m_ffn_w_up'], 'm_ffn_w_down': out['m_ffn_w_down'], 'm_final_norm': out['m_final_norm'], 'v_meta': out['v_meta'], 'v_norm_mix': out['v_norm_mix'], 'v_norm_ffn': out['v_norm_ffn'], 'v_pool_w': out['v_pool_w'], 'v_pool_scale': out['v_pool_scale'], 'v_sb_w_qkv': out['v_sb_w_qkv'], 'v_sb_w_o': out['v_sb_w_o'], 'v_mla_w_down': out['v_mla_w_down'], 'v_mla_q_norm': out['v_mla_q_norm'], 'v_mla_kv_norm': out['v_mla_kv_norm'], 'v_mla_w_uq': out['v_mla_w_uq'], 'v_mla_w_ukv': out['v_mla_w_ukv'], 'v_mla_w_o': out['v_mla_w_o'], 'v_fox_w_qkvf': out['v_fox_w_qkvf'], 'v_fox_b_f': out['v_fox_b_f'], 'v_fox_w_o': out['v_fox_w_o'], 'v_ffn_w_gate': out['v_ffn_w_gate'], 'v_ffn_w_up': out['v_ffn_w_up'], 'v_ffn_w_down': out['v_ffn_w_down'], 'v_final_norm': out['v_final_norm']}


def _loss(weights, diff, rest, loss_target):
    with _jax.named_scope("forward"):
        args = {**rest, TWIN_DIFF_INPUT: diff, **{k: w.astype(_WEIGHT_DTYPES[k]) for k, w in weights.items()}}
        y = _forward(args)
    with _jax.named_scope("loss_head"):
        err = _jnp.square(y.astype(_jnp.float32) - loss_target)
        return 0.5 * _jnp.sum(_jnp.mean(err, axis=-1)) if err.ndim else 0.5 * err


def _adamw(w, g, m, v):
    m = ADAM_B1 * m + (1.0 - ADAM_B1) * g
    v = ADAM_B2 * v + (1.0 - ADAM_B2) * _jnp.square(g)
    m_hat = m / (1.0 - ADAM_B1 ** ADAM_STEP)
    v_hat = v / (1.0 - ADAM_B2 ** ADAM_STEP)
    delta = -ADAM_LR * (m_hat / (_jnp.sqrt(v_hat) + ADAM_EPS) + ADAM_WD * w)
    return delta, m, v


def reference(x, meta, norm_mix, norm_ffn, pool_w, pool_scale, sb_w_qkv, sb_w_o, mla_w_down, mla_q_norm, mla_kv_norm, mla_w_uq, mla_w_ukv, mla_w_o, fox_w_qkvf, fox_b_f, fox_w_o, ffn_w_gate, ffn_w_up, ffn_w_down, final_norm, loss_target, m_meta, m_norm_mix, m_norm_ffn, m_pool_w, m_pool_scale, m_sb_w_qkv, m_sb_w_o, m_mla_w_down, m_mla_q_norm, m_mla_kv_norm, m_mla_w_uq, m_mla_w_ukv, m_mla_w_o, m_fox_w_qkvf, m_fox_b_f, m_fox_w_o, m_ffn_w_gate, m_ffn_w_up, m_ffn_w_down, m_final_norm, v_meta, v_norm_mix, v_norm_ffn, v_pool_w, v_pool_scale, v_sb_w_qkv, v_sb_w_o, v_mla_w_down, v_mla_q_norm, v_mla_kv_norm, v_mla_w_uq, v_mla_w_ukv, v_mla_w_o, v_fox_w_qkvf, v_fox_b_f, v_fox_w_o, v_ffn_w_gate, v_ffn_w_up, v_ffn_w_down, v_final_norm):
    given = dict(x=x, meta=meta, norm_mix=norm_mix, norm_ffn=norm_ffn, pool_w=pool_w, pool_scale=pool_scale, sb_w_qkv=sb_w_qkv, sb_w_o=sb_w_o, mla_w_down=mla_w_down, mla_q_norm=mla_q_norm, mla_kv_norm=mla_kv_norm, mla_w_uq=mla_w_uq, mla_w_ukv=mla_w_ukv, mla_w_o=mla_w_o, fox_w_qkvf=fox_w_qkvf, fox_b_f=fox_b_f, fox_w_o=fox_w_o, ffn_w_gate=ffn_w_gate, ffn_w_up=ffn_w_up, ffn_w_down=ffn_w_down, final_norm=final_norm, loss_target=loss_target, m_meta=m_meta, m_norm_mix=m_norm_mix, m_norm_ffn=m_norm_ffn, m_pool_w=m_pool_w, m_pool_scale=m_pool_scale, m_sb_w_qkv=m_sb_w_qkv, m_sb_w_o=m_sb_w_o, m_mla_w_down=m_mla_w_down, m_mla_q_norm=m_mla_q_norm, m_mla_kv_norm=m_mla_kv_norm, m_mla_w_uq=m_mla_w_uq, m_mla_w_ukv=m_mla_w_ukv, m_mla_w_o=m_mla_w_o, m_fox_w_qkvf=m_fox_w_qkvf, m_fox_b_f=m_fox_b_f, m_fox_w_o=m_fox_w_o, m_ffn_w_gate=m_ffn_w_gate, m_ffn_w_up=m_ffn_w_up, m_ffn_w_down=m_ffn_w_down, m_final_norm=m_final_norm, v_meta=v_meta, v_norm_mix=v_norm_mix, v_norm_ffn=v_norm_ffn, v_pool_w=v_pool_w, v_pool_scale=v_pool_scale, v_sb_w_qkv=v_sb_w_qkv, v_sb_w_o=v_sb_w_o, v_mla_w_down=v_mla_w_down, v_mla_q_norm=v_mla_q_norm, v_mla_kv_norm=v_mla_kv_norm, v_mla_w_uq=v_mla_w_uq, v_mla_w_ukv=v_mla_w_ukv, v_mla_w_o=v_mla_w_o, v_fox_w_qkvf=v_fox_w_qkvf, v_fox_b_f=v_fox_b_f, v_fox_w_o=v_fox_w_o, v_ffn_w_gate=v_ffn_w_gate, v_ffn_w_up=v_ffn_w_up, v_ffn_w_down=v_ffn_w_down, v_final_norm=v_final_norm)
    weights = {n: given[n] for n in TWIN_WEIGHTS}
    shared = {n: given[n] for n in SHARED_INPUTS}
    per_example = {n: given[n] for n in ['x']}
    grad_fn = _jax.value_and_grad(_loss, argnums=(0, 1))

    def one_microbatch(ex, loss_target):
        ex = dict(ex)
        diff = ex.pop(TWIN_DIFF_INPUT)
        return grad_fn(weights, diff, {**shared, **ex}, loss_target)

    if N_MICROBATCH == 1:
        loss, (grad_w, grad_x) = one_microbatch(per_example, given["loss_target"])
    else:
        def body(carry, xs):
            loss_sum, grad_sum = carry
            l_k, (gw_k, gx_k) = one_microbatch(xs[0], xs[1])
            with _jax.named_scope("update"):
                return (loss_sum + l_k, _jax.tree.map(_jnp.add, grad_sum, gw_k)), gx_k

        init = (_jnp.zeros((), _jnp.float32), _jax.tree.map(_jnp.zeros_like, weights))
        (loss, grad_w), grad_x = _jax.lax.scan(body, init, (per_example, given["loss_target"]))
    with _jax.named_scope("update"):
        delta_w, new_m, new_v = {}, {}, {}
        for n in TWIN_WEIGHTS:
            delta_w[n], new_m[n], new_v[n] = _adamw(weights[n], grad_w[n], given["m_" + n], given["v_" + n])
    return (loss, grad_x, *[grad_w[n] for n in TWIN_WEIGHTS], *[delta_w[n] for n in TWIN_WEIGHTS],
            *[new_m[n] for n in TWIN_WEIGHTS], *[new_v[n] for n in TWIN_WEIGHTS])
```

```python
import functools

import numpy as np
import jax
import jax.numpy as jnp
from jax import lax
from jax.experimental import pallas as pl
from jax.experimental.pallas import tpu as pltpu

F32 = jnp.float32
BF16 = jnp.bfloat16

N_DEV = 8
D_MODEL = 1024
N_META = 16
PAD = 240
ROW0 = PAD + N_META
EPS = 1e-6
POOL_WINDOWS = (2, 4, 8, 16)
POOL_GROUP = 256
HALO = 128
N_HEADS = 16
HEAD_DIM = 64
N_PAIRS = N_HEADS // 2
MLA_Q_RANK = 384
MLA_KV_RANK = 256
MLA_NOPE = 64
MLA_ROPE = 32
ROPE_THETA = 10000.0
D_FF = 2816
DEPTH = 4
ATTN_TILE = 256
NEG = -1e30
VMEM_LIMIT = 56 * 2**20

ADAM_LR = 0.001
ADAM_B1 = 0.9
ADAM_B2 = 0.999
ADAM_EPS = 1e-08
ADAM_WD = 0.01
ADAM_STEP = 10

MESH = pl.DeviceIdType.MESH


def _params(sem=None):
    return pltpu.CompilerParams(dimension_semantics=sem, vmem_limit_bytes=VMEM_LIMIT)


def _pick(n, cands):
    for c in cands:
        if n % c == 0:
            return c
    return n


def _col_tile(n, cap=1536):
    best = None
    for t in range(128, min(n, cap) + 1, 128):
        if n % t == 0:
            best = t
    return best if best is not None else n


def _dot(a, b):
    return jnp.dot(a, b, preferred_element_type=F32)


def _dot_nt(a, b):
    return lax.dot_general(a, b, (((1,), (1,)), ((), ())), preferred_element_type=F32)


def _dot_tn(a, b):
    return lax.dot_general(a, b, (((0,), (0,)), ((), ())), preferred_element_type=F32)


def _mm_nn(a, b, out_dtype, name, res=None):
    M, K = a.shape
    N = b.shape[1]
    tm = _pick(M, (768, 512, 256, 128))
    tn = _col_tile(N)

    def body(*refs):
        if res is None:
            a_ref, b_ref, o_ref = refs
        else:
            a_ref, b_ref, r_ref, o_ref = refs
        acc = _dot(a_ref[...].astype(BF16), b_ref[...])
        if res is not None:
            acc = acc + r_ref[...]
        o_ref[...] = acc.astype(o_ref.dtype)

    in_specs = [pl.BlockSpec((tm, K), lambda n, m: (m, 0)), pl.BlockSpec((K, tn), lambda n, m: (0, n))]
    args = [a, b]
    if res is not None:
        in_specs.append(pl.BlockSpec((tm, tn), lambda n, m: (m, n)))
        args.append(res)
    return pl.pallas_call(
        body, name=name, grid=(N // tn, M // tm), in_specs=in_specs,
        out_specs=pl.BlockSpec((tm, tn), lambda n, m: (m, n)),
        out_shape=jax.ShapeDtypeStruct((M, N), out_dtype),
        compiler_params=_params(("parallel", "parallel")))(*args)


def _mm_nt(a, w, out_dtype, name):
    M, N = a.shape
    K = w.shape[0]
    tm = _pick(M, (768, 512, 256, 128)) if N <= 3200 else _pick(M, (256, 128))
    tk = _col_tile(K, 1024)

    def body(a_ref, w_ref, o_ref):
        o_ref[...] = _dot_nt(a_ref[...].astype(BF16), w_ref[...]).astype(o_ref.dtype)

    return pl.pallas_call(
        body, name=name, grid=(K // tk, M // tm),
        in_specs=[pl.BlockSpec((tm, N), lambda k, m: (m, 0)), pl.BlockSpec((tk, N), lambda k, m: (k, 0))],
        out_specs=pl.BlockSpec((tm, tk), lambda k, m: (m, k)),
        out_shape=jax.ShapeDtypeStruct((M, K), out_dtype),
        compiler_params=_params(("parallel", "parallel")))(a, w)


def _mm_tn(a, b, name):
    M, K = a.shape
    N = b.shape[1]
    tm = _pick(M, (768, 512, 256, 128))
    tk = _col_tile(K, 1408)
    tn = _col_tile(N, 1408)

    def body(a_ref, b_ref, o_ref):
        @pl.when(pl.program_id(2) == 0)
        def _():
            o_ref[...] = jnp.zeros_like(o_ref)
        o_ref[...] += _dot_tn(a_ref[...].astype(BF16), b_ref[...].astype(BF16))

    return pl.pallas_call(
        body, name=name, grid=(K // tk, N // tn, M // tm),
        in_specs=[pl.BlockSpec((tm, tk), lambda k, n, m: (m, k)), pl.BlockSpec((tm, tn), lambda k, n, m: (m, n))],
        out_specs=pl.BlockSpec((tk, tn), lambda k, n, m: (k, n)),
        out_shape=jax.ShapeDtypeStruct((K, N), F32),
        compiler_params=_params(("parallel", "parallel", "arbitrary")))(a, b)


def _norm_fwd(h, gain, out_dtype, name):
    M, C = h.shape
    tm = _pick(M, (768, 512, 256, 128))

    def body(h_ref, g_ref, a_ref):
        x = h_ref[...]
        r = lax.rsqrt(jnp.mean(x * x, axis=-1, keepdims=True) + EPS)
        a_ref[...] = ((x * r) * g_ref[...]).astype(a_ref.dtype)

    return pl.pallas_call(
        body, name=name, grid=(M // tm,),
        in_specs=[pl.BlockSpec((tm, C), lambda m: (m, 0)), pl.BlockSpec((1, C), lambda m: (0, 0))],
        out_specs=pl.BlockSpec((tm, C), lambda m: (m, 0)),
        out_shape=jax.ShapeDtypeStruct((M, C), out_dtype),
        compiler_params=_params(("parallel",)))(h, gain)


def _norm_bwd(h, gain, da, dres, name):
    M, C = h.shape
    tm = _pick(M, (768, 512, 256, 128))

    def body(*refs):
        if dres is None:
            h_ref, g_ref, da_ref, dh_ref, dg_ref = refs
        else:
            h_ref, g_ref, da_ref, dr_ref, dh_ref, dg_ref = refs
        x = h_ref[...]
        r = lax.rsqrt(jnp.mean(x * x, axis=-1, keepdims=True) + EPS)
        y = x * r
        dav = da_ref[...].astype(F32)
        dy = dav * g_ref[...]
        dh = r * (dy - y * jnp.mean(dy * y, axis=-1, keepdims=True))
        if dres is not None:
            dh = dh + dr_ref[...]
        dh_ref[...] = dh

        @pl.when(pl.program_id(0) == 0)
        def _():
            dg_ref[...] = jnp.zeros_like(dg_ref)
        dg_ref[...] += jnp.sum(dav * y, axis=0, keepdims=True)

    row = pl.BlockSpec((tm, C), lambda m: (m, 0))
    vec = pl.BlockSpec((1, C), lambda m: (0, 0))
    in_specs = [row, vec, row] + ([row] if dres is not None else [])
    args = [h, gain, da] + ([dres] if dres is not None else [])
    return pl.pallas_call(
        body, name=name, grid=(M // tm,), in_specs=in_specs, out_specs=[row, vec],
        out_shape=[jax.ShapeDtypeStruct((M, C), F32), jax.ShapeDtypeStruct((1, C), F32)],
        compiler_params=_params(("arbitrary",)))(*args)


def _ffn_up(b, w_gu, name):
    M, K = b.shape
    F = w_gu.shape[1] // 2
    tm = _pick(M, (768, 512, 256, 128))
    tn = _col_tile(F, 1408)
    nb = F // tn

    def body(b_ref, wg_ref, wu_ref, g_ref, u_ref, act_ref):
        x = b_ref[...]
        g = _dot(x, wg_ref[...])
        u = _dot(x, wu_ref[...])
        g_ref[...] = g
        u_ref[...] = u
        act_ref[...] = ((g * jax.nn.sigmoid(g)) * u).astype(act_ref.dtype)

    blk = pl.BlockSpec((tm, tn), lambda n, m: (m, n))
    return pl.pallas_call(
        body, name=name, grid=(nb, M // tm),
        in_specs=[pl.BlockSpec((tm, K), lambda n, m: (m, 0)),
                  pl.BlockSpec((K, tn), lambda n, m: (0, n)),
                  pl.BlockSpec((K, tn), lambda n, m: (0, n + nb))],
        out_specs=[blk, blk, blk],
        out_shape=[jax.ShapeDtypeStruct((M, F), F32), jax.ShapeDtypeStruct((M, F), F32),
                   jax.ShapeDtypeStruct((M, F), BF16)],
        compiler_params=_params(("parallel", "parallel")))(b, w_gu, w_gu)


def _ffn_dact(dy, w_d, g, u, name):
    M, K = dy.shape
    F = w_d.shape[0]
    tm = _pick(M, (768, 512, 256, 128))
    tn = _col_tile(F, 1408)
    nb = F // tn

    def body(dy_ref, wd_ref, g_ref, u_ref, dg_ref, du_ref):
        dact = _dot_nt(dy_ref[...].astype(BF16), wd_ref[...])
        gv = g_ref[...]
        s = jax.nn.sigmoid(gv)
        silu = gv * s
        dg_ref[...] = (dact * u_ref[...] * (s * (1.0 + gv * (1.0 - s)))).astype(dg_ref.dtype)
        du_ref[...] = (dact * silu).astype(du_ref.dtype)

    blk = pl.BlockSpec((tm, tn), lambda n, m: (m, n))
    dg, du = pl.pallas_call(
        body, name=name, grid=(nb, M // tm),
        in_specs=[pl.BlockSpec((tm, K), lambda n, m: (m, 0)), pl.BlockSpec((tn, K), lambda n, m: (n, 0)), blk, blk],
        out_specs=[blk, blk],
        out_shape=[jax.ShapeDtypeStruct((M, F), BF16), jax.ShapeDtypeStruct((M, F), BF16)],
        compiler_params=_params(("parallel", "parallel")))(dy, w_d, g, u)
    return jnp.concatenate([dg, du], axis=1)


def _loss_head(h, gain, target, name):
    M, C = h.shape
    tm = ROW0
    assert M % tm == 0 and target.shape[0] == M - ROW0

    def body(h_ref, g_ref, t_ref, sq_ref, dh_ref, dg_ref):
        i = pl.program_id(0)

        @pl.when(i == 0)
        def _():
            sq_ref[...] = jnp.zeros_like(sq_ref)
            dg_ref[...] = jnp.zeros_like(dg_ref)
            dh_ref[...] = jnp.zeros_like(dh_ref)

        @pl.when(i > 0)
        def _():
            x = h_ref[...]
            r = lax.rsqrt(jnp.mean(x * x, axis=-1, keepdims=True) + EPS)
            y = x * r
            err = y * g_ref[...] - t_ref[...]
            sq_ref[...] += jnp.sum(err * err, axis=0, keepdims=True)
            da = err * (1.0 / C)
            dy = da * g_ref[...]
            dh_ref[...] = r * (dy - y * jnp.mean(dy * y, axis=-1, keepdims=True))
            dg_ref[...] += jnp.sum(da * y, axis=0, keepdims=True)

    row = pl.BlockSpec((tm, C), lambda m: (m, 0))
    vec = pl.BlockSpec((1, C), lambda m: (0, 0))
    return pl.pallas_call(
        body, name=name, grid=(M // tm,),
        in_specs=[row, vec, pl.BlockSpec((tm, C), lambda m: (jnp.maximum(m - 1, 0), 0))],
        out_specs=[vec, row, vec],
        out_shape=[jax.ShapeDtypeStruct((1, C), F32), jax.ShapeDtypeStruct((M, C), F32),
                   jax.ShapeDtypeStruct((1, C), F32)],
        compiler_params=_params(("arbitrary",)))(h, gain, target)


def _pool_counts(row0, tm):
    pos = row0 + lax.broadcasted_iota(jnp.int32, (tm, 1), 0) - PAD
    return pos


def _pool_fwd(h, a, w, scale, name):
    M, C = a.shape
    tm = 256
    hb = tm // HALO

    def body(h_ref, a_ref, halo_ref, w_ref, s_ref, o_ref, p_ref):
        i = pl.program_id(0)
        row0 = i * tm
        ext = jnp.concatenate([halo_ref[...], a_ref[...]], axis=0)
        src = row0 - HALO + lax.broadcasted_iota(jnp.int32, (tm + HALO, 1), 0)
        ext = jnp.where(src >= PAD, ext, 0.0)
        r = lax.broadcasted_iota(jnp.int32, (tm, tm + HALO), 0)
        c = lax.broadcasted_iota(jnp.int32, (tm, tm + HALO), 1)
        pos = _pool_counts(row0, tm)
        for g, win in enumerate(POOL_WINDOWS):
            band = ((c <= r + HALO) & (c > r + HALO - win)).astype(F32)
            cols = slice(g * POOL_GROUP, (g + 1) * POOL_GROUP)
            xg = ext[:, cols]
            tot = jnp.dot(band, xg, precision=lax.Precision.HIGHEST, preferred_element_type=F32)
            cnt = jnp.clip(pos + 1, 1, win).astype(F32)
            pooled = (tot / cnt - xg[HALO:]).astype(BF16)
            p_ref[:, cols] = pooled
            mixed = _dot(pooled, w_ref[g])
            o_ref[:, cols] = h_ref[:, cols] + mixed * s_ref[:, cols]

    row = pl.BlockSpec((tm, C), lambda m: (m, 0))
    return pl.pallas_call(
        body, name=name, grid=(M // tm,),
        in_specs=[row, row, pl.BlockSpec((HALO, C), lambda m: (jnp.maximum(m * hb - 1, 0), 0)),
                  pl.BlockSpec((4, POOL_GROUP, POOL_GROUP), lambda m: (0, 0, 0)),
                  pl.BlockSpec((1, C), lambda m: (0, 0))],
        out_specs=[row, row],
        out_shape=[jax.ShapeDtypeStruct((M, C), F32), jax.ShapeDtypeStruct((M, C), BF16)],
        compiler_params=_params(("parallel",)))(h, a, a, w, scale)


def _pool_bwd_mix(dout, pooled, w, scale, name):
    M, C = dout.shape
    tm = 256

    def body(do_ref, p_ref, w_ref, s_ref, dpc_ref, dw_ref, ds_ref):
        i = pl.program_id(0)

        @pl.when(i == 0)
        def _():
            dw_ref[...] = jnp.zeros_like(dw_ref)
            ds_ref[...] = jnp.zeros_like(ds_ref)

        pos = _pool_counts(i * tm, tm)
        for g, win in enumerate(POOL_WINDOWS):
            cols = slice(g * POOL_GROUP, (g + 1) * POOL_GROUP)
            do = do_ref[:, cols]
            pooled = p_ref[:, cols]
            mixed = _dot(pooled, w_ref[g])
            ds_ref[:, cols] += jnp.sum(do * mixed, axis=0, keepdims=True)
            dmix = (do * s_ref[:, cols]).astype(BF16)
            dw_ref[g] += _dot_tn(pooled, dmix)
            dp = _dot_nt(dmix, w_ref[g])
            cnt = jnp.clip(pos + 1, 1, win).astype(F32)
            dpc_ref[:, cols] = dp / cnt

    row = pl.BlockSpec((tm, C), lambda m: (m, 0))
    wspec = pl.BlockSpec((4, POOL_GROUP, POOL_GROUP), lambda m: (0, 0, 0))
    vec = pl.BlockSpec((1, C), lambda m: (0, 0))
    return pl.pallas_call(
        body, name=name, grid=(M // tm,),
        in_specs=[row, row, wspec, vec], out_specs=[row, wspec, vec],
        out_shape=[jax.ShapeDtypeStruct((M, C), F32), jax.ShapeDtypeStruct((4, POOL_GROUP, POOL_GROUP), F32),
                   jax.ShapeDtypeStruct((1, C), F32)],
        compiler_params=_params(("arbitrary",)))(dout, pooled, w, scale)


def _pool_bwd_window(dpc, name):
    M, C = dpc.shape
    tm = 256
    hb = tm // HALO
    last = M // HALO - 1

    def body(d_ref, halo_ref, da_ref):
        i = pl.program_id(0)
        row0 = i * tm
        ext = jnp.concatenate([d_ref[...], halo_ref[...]], axis=0)
        src = row0 + lax.broadcasted_iota(jnp.int32, (tm + HALO, 1), 0)
        ext = jnp.where(src < M, ext, 0.0)
        r = lax.broadcasted_iota(jnp.int32, (tm, tm + HALO), 0)
        c = lax.broadcasted_iota(jnp.int32, (tm, tm + HALO), 1)
        pos = _pool_counts(row0, tm)
        for g, win in enumerate(POOL_WINDOWS):
            band = ((c >= r) & (c < r + win)).astype(F32)
            cols = slice(g * POOL_GROUP, (g + 1) * POOL_GROUP)
            xg = ext[:, cols]
            tot = jnp.dot(band, xg, precision=lax.Precision.HIGHEST, preferred_element_type=F32)
            cnt = jnp.clip(pos + 1, 1, win).astype(F32)
            da_ref[:, cols] = jnp.where(pos >= 0, tot - xg[:tm] * cnt, 0.0)

    row = pl.BlockSpec((tm, C), lambda m: (m, 0))
    return pl.pallas_call(
        body, name=name, grid=(M // tm,),
        in_specs=[row, pl.BlockSpec((HALO, C), lambda m: (jnp.minimum((m + 1) * hb, last), 0))],
        out_specs=row, out_shape=jax.ShapeDtypeStruct((M, C), F32),
        compiler_params=_params(("parallel",)))(dpc, dpc)


def _head_masks():
    lane = lax.broadcasted_iota(jnp.int32, (1, 128), 1)
    return lane < HEAD_DIM, lane


def _split_heads(x, first):
    z = jnp.zeros_like(x)
    return jnp.where(first, x, z), jnp.where(first, z, x)


def _split_rope(x, lane):
    z = jnp.zeros_like(x)
    return jnp.where(lane < MLA_ROPE, x, z), jnp.where((lane >= MLA_ROPE) & (lane < 2 * MLA_ROPE), x, z)


def _softmax_fwd(qkv, cols, scale, name, rope=None, decay=None):
    q_arr, k_arr, v_arr = qkv
    cq, ck, cv = cols
    M = q_arr.shape[0]
    t = ATTN_TILE
    has_rope, has_decay = rope is not None, decay is not None

    def body(*refs):
        refs = list(refs)
        q_ref, k_ref, v_ref = refs[:3]
        pos = 3
        if has_rope:
            qr_ref, kr_ref = refs[pos:pos + 2]
            pos += 2
        if has_decay:
            fc_ref, fr_ref = refs[pos:pos + 2]
            pos += 2
        if has_decay:
            o_ref, lse_ref, ox_ref, m_s, l_s, acc_s, accx_s = refs[pos:]
        else:
            o_ref, lse_ref, m_s, l_s, acc_s = refs[pos:]
        i = pl.program_id(1)
        first, lane = _head_masks()
        qs = _split_heads(q_ref[...], first)
        if has_rope:
            qrs = _split_rope(qr_ref[...], lane)
        m_s[...] = jnp.full_like(m_s, NEG)
        l_s[...] = jnp.zeros_like(l_s)
        acc_s[...] = jnp.zeros_like(acc_s)
        if has_decay:
            accx_s[...] = jnp.zeros_like(accx_s)
        qpos = i * t + lax.broadcasted_iota(jnp.int32, (t, t), 0)
        kidx = lax.broadcasted_iota(jnp.int32, (t, t), 1)

        def step(kb, carry):
            k0 = pl.multiple_of(kb * t, t)
            k = k_ref[pl.ds(k0, t), :]
            vs = _split_heads(v_ref[pl.ds(k0, t), :], first)
            kpos = k0 + kidx
            valid = (kpos <= qpos) & (kpos >= PAD)
            pv = pvx = None
            alphas = []
            for hh in range(2):
                s = _dot_nt(qs[hh], k)
                if has_rope:
                    s = s + _dot_nt(qrs[hh], kr_ref[pl.ds(k0, t), :])
                s = s * scale
                if has_decay:
                    s = s + (fc_ref[:, hh:hh + 1] - fr_ref[hh:hh + 1, pl.ds(k0, t)])
                s = jnp.where(valid, s, NEG)
                m_old = m_s[hh]
                m_new = jnp.maximum(m_old, jnp.max(s, axis=1, keepdims=True))
                p = jnp.exp(s - m_new)
                alpha = jnp.exp(m_old - m_new)
                l_s[hh] = alpha * l_s[hh] + jnp.sum(p, axis=1, keepdims=True)
                m_s[hh] = m_new
                pb = p.astype(BF16)
                d = _dot(pb, vs[hh])
                pv = d if pv is None else pv + d
                if has_decay:
                    dx = _dot((p - pb.astype(F32)).astype(BF16), vs[hh])
                    pvx = dx if pvx is None else pvx + dx
                alphas.append(alpha)
            alpha2 = jnp.where(first, alphas[0], alphas[1])
            acc_s[...] = acc_s[...] * alpha2 + pv
            if has_decay:
                accx_s[...] = accx_s[...] * alpha2 + pvx
            return carry

        lax.fori_loop(0, i + 1, step, 0)
        linv = jnp.where(first, 1.0 / l_s[0], 1.0 / l_s[1])
        o_ref[...] = (acc_s[...] * linv).astype(o_ref.dtype)
        if has_decay:
            ox_ref[...] = (acc_s[...] + accx_s[...]) * linv
        lse_ref[:, 0:1] = m_s[0] + jnp.log(l_s[0])
        lse_ref[:, 1:2] = m_s[1] + jnp.log(l_s[1])

    whole = lambda c0: pl.BlockSpec((M, 128), lambda j, i: (0, c0 + j))
    in_specs = [pl.BlockSpec((t, 128), lambda j, i: (i, cq + j)), whole(ck), whole(cv)]
    args = [q_arr, k_arr, v_arr]
    if has_rope:
        in_specs += [pl.BlockSpec((t, 128), lambda j, i: (i, j)), pl.BlockSpec((M, 128), lambda j, i: (0, 0))]
        args += list(rope)
    if has_decay:
        in_specs += [pl.BlockSpec((None, t, 2), lambda j, i: (j, i, 0)), pl.BlockSpec((None, 2, M), lambda j, i: (j, 0, 0))]
        args += list(decay)
    blk = pl.BlockSpec((t, 128), lambda j, i: (i, j))
    out_specs = [blk, pl.BlockSpec((None, t, 2), lambda j, i: (j, i, 0))]
    out_shape = [jax.ShapeDtypeStruct((M, N_PAIRS * 128), BF16), jax.ShapeDtypeStruct((N_PAIRS, M, 2), F32)]
    scratch = [pltpu.VMEM((2, t, 1), F32), pltpu.VMEM((2, t, 1), F32), pltpu.VMEM((t, 128), F32)]
    if has_decay:
        out_specs.append(blk)
        out_shape.append(jax.ShapeDtypeStruct((M, N_PAIRS * 128), F32))
        scratch.append(pltpu.VMEM((t, 128), F32))
    return pl.pallas_call(
        body, name=name, grid=(N_PAIRS, M // t), in_specs=in_specs, out_specs=out_specs, out_shape=out_shape,
        scratch_shapes=scratch, compiler_params=_params(("parallel", "arbitrary")))(*args)


def _softmax_bwd(qkv, cols, o, do, lse, scale, name, rope=None, decay=None):
    q_arr, k_arr, v_arr = qkv
    cq, ck, cv = cols
    M = q_arr.shape[0]
    t = ATTN_TILE
    has_rope, has_decay = rope is not None, decay is not None
    n_in = 6 + 2 * has_rope + 2 * has_decay

    def body(*refs):
        refs = list(refs)
        q_ref, k_ref, v_ref, o_ref, do_ref, lse_ref = refs[:6]
        pos = 6
        if has_rope:
            qr_ref, kr_ref = refs[pos:pos + 2]
            pos += 2
        if has_decay:
            fc_ref, fr_ref = refs[pos:pos + 2]
            pos += 2
        outs = refs[n_in:]
        dq_ref, dk_ref, dv_ref = outs[:3]
        pos = 3
        if has_rope:
            dqr_ref, dkr_ref = outs[pos:pos + 2]
            pos += 2
        if has_decay:
            dfk_ref = outs[pos]
            pos += 1
        dq_s = outs[pos]
        dqr_s = outs[pos + 1] if has_rope else None
        i = pl.program_id(1)
        first, lane = _head_masks()

        @pl.when(i == 0)
        def _():
            dk_ref[...] = jnp.zeros_like(dk_ref)
            dv_ref[...] = jnp.zeros_like(dv_ref)
            if has_rope:
                dkr_ref[...] = jnp.zeros_like(dkr_ref)
            if has_decay:
                dfk_ref[...] = jnp.zeros_like(dfk_ref)

        qs = _split_heads(q_ref[...], first)
        dov = do_ref[...]
        dos = _split_heads(dov, first)
        prod = dov.astype(F32) * o_ref[...].astype(F32)
        deltas = (jnp.sum(jnp.where(first, prod, 0.0), axis=1, keepdims=True),
                  jnp.sum(jnp.where(first, 0.0, prod), axis=1, keepdims=True))
        if has_rope:
            qrs = _split_rope(qr_ref[...], lane)
            dqr_s[...] = jnp.zeros_like(dqr_s)
        dq_s[...] = jnp.zeros_like(dq_s)
        qpos = i * t + lax.broadcasted_iota(jnp.int32, (t, t), 0)
        kidx = lax.broadcasted_iota(jnp.int32, (t, t), 1)

        def step(kb, carry):
            k0 = pl.multiple_of(kb * t, t)
            rows = pl.ds(k0, t)
            k = k_ref[rows, :]
            v = v_ref[rows, :]
            ks = _split_heads(k, first)
            kpos = k0 + kidx
            valid = (kpos <= qpos) & (kpos >= PAD)
            if has_rope:
                kr = kr_ref[rows, :]
                krs = _split_rope(kr, lane)
            dq = dk = dv = dqr = dkr = None
            for hh in range(2):
                s = _dot_nt(qs[hh], k)
                if has_rope:
                    s = s + _dot_nt(qrs[hh], kr)
                s = s * scale
                if has_decay:
                    s = s + (fc_ref[:, hh:hh + 1] - fr_ref[hh:hh + 1, rows])
                s = jnp.where(valid, s, NEG)
                p = jnp.exp(s - lse_ref[:, hh:hh + 1])
                dp = _dot_nt(dos[hh], v)
                ds = p * (dp - deltas[hh])
                if has_decay:
                    dfk_ref[hh:hh + 1, rows] += -jnp.sum(ds, axis=0, keepdims=True)
                dsb = (ds * scale).astype(BF16)
                a = _dot(dsb, ks[hh])
                b = _dot_tn(dsb, qs[hh])
                c = _dot_tn(p.astype(BF16), dos[hh])
                dq = a if dq is None else dq + a
                dk = b if dk is None else dk + b
                dv = c if dv is None else dv + c
                if has_rope:
                    e = _dot(dsb, krs[hh])
                    f = _dot_tn(dsb, qrs[hh])
                    dqr = e if dqr is None else dqr + e
                    dkr = f if dkr is None else dkr + f
            dq_s[...] += dq
            dk_ref[rows, :] += dk
            dv_ref[rows, :] += dv
            if has_rope:
                dqr_s[...] += dqr
                dkr_ref[rows, :] += dkr
            return carry

        lax.fori_loop(0, i + 1, step, 0)
        dq_ref[...] = dq_s[...].astype(dq_ref.dtype)
        if has_rope:
            dqr_ref[...] = dqr_s[...].astype(dqr_ref.dtype)

    whole = lambda c0: pl.BlockSpec((M, 128), lambda j, i: (0, c0 + j))
    blk = pl.BlockSpec((t, 128), lambda j, i: (i, j))
    col = pl.BlockSpec((M, 128), lambda j, i: (0, j))
    in_specs = [pl.BlockSpec((t, 128), lambda j, i: (i, cq + j)), whole(ck), whole(cv), blk, blk,
                pl.BlockSpec((None, t, 2), lambda j, i: (j, i, 0))]
    args = [q_arr, k_arr, v_arr, o, do, lse]
    out_specs = [blk, col, col]
    out_shape = [jax.ShapeDtypeStruct((M, N_PAIRS * 128), BF16), jax.ShapeDtypeStruct((M, N_PAIRS * 128), F32),
                 jax.ShapeDtypeStruct((M, N_PAIRS * 128), F32)]
    scratch = [pltpu.VMEM((t, 128), F32)]
    if has_rope:
        in_specs += [blk, pl.BlockSpec((M, 128), lambda j, i: (0, 0))]
        args += list(rope)
        out_specs += [blk, pl.BlockSpec((None, M, 128), lambda j, i: (j, 0, 0))]
        out_shape += [jax.ShapeDtypeStruct((M, N_PAIRS * 128), BF16), jax.ShapeDtypeStruct((N_PAIRS, M, 128), F32)]
        scratch += [pltpu.VMEM((t, 128), F32)]
    if has_decay:
        in_specs += [pl.BlockSpec((None, t, 2), lambda j, i: (j, i, 0)), pl.BlockSpec((None, 2, M), lambda j, i: (j, 0, 0))]
        args += list(decay)
        out_specs += [pl.BlockSpec((None, 2, M), lambda j, i: (j, 0, 0))]
        out_shape += [jax.ShapeDtypeStruct((N_PAIRS, 2, M), F32)]
    return pl.pallas_call(
        body, name=name, grid=(N_PAIRS, M // t), in_specs=in_specs, out_specs=out_specs, out_shape=out_shape,
        scratch_shapes=scratch, compiler_params=_params(("parallel", "arbitrary")))(*args)


def _tri(t, rel):
    j = lax.broadcasted_iota(jnp.int32, (t, t), 0)
    k = lax.broadcasted_iota(jnp.int32, (t, t), 1)
    m = {"gt": j > k, "le": j <= k, "lt": j < k}[rel]
    return m.astype(BF16)


def _lane_cumsum(x, tri):
    hi = x.astype(BF16)
    lo = (x - hi.astype(F32)).astype(BF16)
    return _dot(hi, tri) + _dot(lo, tri)


def _log_sigmoids(z):
    sp = jnp.log(1.0 + jnp.exp(-jnp.abs(z)))
    return jnp.minimum(z, 0.0) - sp, jnp.minimum(-z, 0.0) - sp


def _sb_fwd(qkv, scale, name):
    M = qkv.shape[0]
    t = ATTN_TILE
    ck, cv = N_PAIRS, 2 * N_PAIRS

    def body(q_ref, k_ref, v_ref, o_ref, tot_ref, c_s, acc_s):
        i = pl.program_id(1)
        first, _ = _head_masks()
        qs = _split_heads(q_ref[...], first)
        c_s[...] = jnp.zeros_like(c_s)
        acc_s[...] = jnp.zeros_like(acc_s)
        tri = _tri(t, "gt")
        qpos = i * t + lax.broadcasted_iota(jnp.int32, (t, t), 0)
        kidx = lax.broadcasted_iota(jnp.int32, (t, t), 1)

        def step(it, carry):
            k0 = pl.multiple_of((i - it) * t, t)
            k = k_ref[pl.ds(k0, t), :]
            vs = _split_heads(v_ref[pl.ds(k0, t), :], first)
            kpos = k0 + kidx
            valid = (kpos < qpos) & (kpos >= PAD)
            pv = None
            for hh in range(2):
                z = _dot_nt(qs[hh], k) * scale
                lb, lkr = _log_sigmoids(z)
                lk = jnp.where(valid, lkr, 0.0)
                later = c_s[hh] + _lane_cumsum(lk, tri)
                a = jnp.where(valid, jnp.exp(lb + later), 0.0)
                c_s[hh] = c_s[hh] + jnp.sum(lk, axis=1, keepdims=True)
                d = _dot(a.astype(BF16), vs[hh])
                pv = d if pv is None else pv + d
            acc_s[...] += pv
            return carry

        lax.fori_loop(0, i + 1, step, 0)
        o_ref[...] = acc_s[...].astype(o_ref.dtype)
        tot_ref[:, 0:1] = c_s[0]
        tot_ref[:, 1:2] = c_s[1]

    whole = lambda c0: pl.BlockSpec((M, 128), lambda j, i: (0, c0 + j))
    return pl.pallas_call(
        body, name=name, grid=(N_PAIRS, M // t),
        in_specs=[pl.BlockSpec((t, 128), lambda j, i: (i, j)), whole(ck), whole(cv)],
        out_specs=[pl.BlockSpec((t, 128), lambda j, i: (i, j)), pl.BlockSpec((None, t, 2), lambda j, i: (j, i, 0))],
        out_shape=[jax.ShapeDtypeStruct((M, N_PAIRS * 128), BF16), jax.ShapeDtypeStruct((N_PAIRS, M, 2), F32)],
        scratch_shapes=[pltpu.VMEM((2, t, 1), F32), pltpu.VMEM((t, 128), F32)],
        compiler_params=_params(("parallel", "arbitrary")))(qkv, qkv, qkv)


def _sb_bwd(qkv, do, tot, scale, name):
    M = qkv.shape[0]
    t = ATTN_TILE
    ck, cv = N_PAIRS, 2 * N_PAIRS

    def body(q_ref, k_ref, v_ref, do_ref, tot_ref, dq_ref, dk_ref, dv_ref, pc_s, dc_s, dq_s):
        i = pl.program_id(1)
        first, _ = _head_masks()

        @pl.when(i == 0)
        def _():
            dk_ref[...] = jnp.zeros_like(dk_ref)
            dv_ref[...] = jnp.zeros_like(dv_ref)

        qs = _split_heads(q_ref[...], first)
        dos = _split_heads(do_ref[...], first)
        pc_s[...] = jnp.zeros_like(pc_s)
        dc_s[...] = jnp.zeros_like(dc_s)
        dq_s[...] = jnp.zeros_like(dq_s)
        tri_le = _tri(t, "le")
        tri_lt = _tri(t, "lt")
        qpos = i * t + lax.broadcasted_iota(jnp.int32, (t, t), 0)
        kidx = lax.broadcasted_iota(jnp.int32, (t, t), 1)

        def step(kb, carry):
            k0 = pl.multiple_of(kb * t, t)
            rows = pl.ds(k0, t)
            k = k_ref[rows, :]
            v = v_ref[rows, :]
            ks = _split_heads(k, first)
            kpos = k0 + kidx
            valid = (kpos < qpos) & (kpos >= PAD)
            dq = dk = dv = None
            for hh in range(2):
                z = _dot_nt(qs[hh], k) * scale
                lb, lkr = _log_sigmoids(z)
                lk = jnp.where(valid, lkr, 0.0)
                later = tot_ref[:, hh:hh + 1] - (pc_s[hh] + _lane_cumsum(lk, tri_le))
                a = jnp.where(valid, jnp.exp(lb + later), 0.0)
                dl = a * _dot_nt(dos[hh], v)
                early = dc_s[hh] + _lane_cumsum(dl, tri_lt)
                sg = jnp.exp(lb)
                dz = jnp.where(valid, dl * (1.0 - sg) - early * sg, 0.0) * scale
                pc_s[hh] = pc_s[hh] + jnp.sum(lk, axis=1, keepdims=True)
                dc_s[hh] = dc_s[hh] + jnp.sum(dl, axis=1, keepdims=True)
                dzb = dz.astype(BF16)
                x = _dot(dzb, ks[hh])
                y = _dot_tn(dzb, qs[hh])
                w = _dot_tn(a.astype(BF16), dos[hh])
                dq = x if dq is None else dq + x
                dk = y if dk is None else dk + y
                dv = w if dv is None else dv + w
            dq_s[...] += dq
            dk_ref[rows, :] += dk
            dv_ref[rows, :] += dv
            return carry

        lax.fori_loop(0, i + 1, step, 0)
        dq_ref[...] = dq_s[...].astype(dq_ref.dtype)

    whole = lambda c0: pl.BlockSpec((M, 128), lambda j, i: (0, c0 + j))
    blk = pl.BlockSpec((t, 128), lambda j, i: (i, j))
    col = pl.BlockSpec((M, 128), lambda j, i: (0, j))
    return pl.pallas_call(
        body, name=name, grid=(N_PAIRS, M // t),
        in_specs=[blk, whole(ck), whole(cv), blk, pl.BlockSpec((None, t, 2), lambda j, i: (j, i, 0))],
        out_specs=[blk, col, col],
        out_shape=[jax.ShapeDtypeStruct((M, N_PAIRS * 128), BF16), jax.ShapeDtypeStruct((M, N_PAIRS * 128), F32),
                   jax.ShapeDtypeStruct((M, N_PAIRS * 128), F32)],
        scratch_shapes=[pltpu.VMEM((2, t, 1), F32), pltpu.VMEM((2, t, 1), F32), pltpu.VMEM((t, 128), F32)],
        compiler_params=_params(("parallel", "arbitrary")))(qkv, qkv, qkv, do, tot)


def _rope_tables(M):
    pos = (jnp.arange(M, dtype=jnp.int32) - PAD).astype(F32)
    inv = ROPE_THETA ** (-jnp.arange(0, MLA_ROPE, 2, dtype=F32) / MLA_ROPE)
    ang = pos[:, None] * inv[None, :]
    cos, sin = jnp.cos(ang), jnp.sin(ang)
    z = jnp.zeros((M, 64), F32)
    cos_t = jnp.concatenate([cos, cos, cos, cos, z], axis=1)
    sin_t = jnp.concatenate([-sin, sin, -sin, sin, z], axis=1)
    return cos_t, sin_t


def _rope(x, cos_t, sin_t, out_dtype, name, inverse=False, lead=0):
    M, C = x.shape
    tm = _pick(M, (768, 512, 256, 128))
    nblk = (C - lead) // 128
    sign = -1.0 if inverse else 1.0

    def body(x_ref, c_ref, s_ref, o_ref):
        lane = lax.broadcasted_iota(jnp.int32, (1, 128), 1)
        low = (lane % MLA_ROPE) < (MLA_ROPE // 2)
        cos = c_ref[...]
        sin = s_ref[...] * sign
        if lead:
            o_ref[:, :lead] = x_ref[:, :lead].astype(o_ref.dtype)
        for b in range(nblk):
            cols = slice(lead + b * 128, lead + (b + 1) * 128)
            v = x_ref[:, cols].astype(F32)
            up = pltpu.roll(v, 128 - MLA_ROPE // 2, 1)
            down = pltpu.roll(v, MLA_ROPE // 2, 1)
            o_ref[:, cols] = (v * cos + jnp.where(low, up, down) * sin).astype(o_ref.dtype)

    row = pl.BlockSpec((tm, C), lambda m: (m, 0))
    tab = pl.BlockSpec((tm, 128), lambda m: (m, 0))
    return pl.pallas_call(
        body, name=name, grid=(M // tm,), in_specs=[row, tab, tab], out_specs=row,
        out_shape=jax.ShapeDtypeStruct((M, C), out_dtype),
        compiler_params=_params(("parallel",)))(x, cos_t, sin_t)


def _forget_cumsum(f_logit, bias, name):
    M = f_logit.shape[0]
    tm = 256

    def body(f_ref, b_ref, o_ref, c_s):
        i = pl.program_id(0)

        @pl.when(i == 0)
        def _():
            c_s[...] = jnp.zeros_like(c_s)
        ls, _ = _log_sigmoids(f_ref[...] + b_ref[...])
        rows = i * tm + lax.broadcasted_iota(jnp.int32, (tm, 1), 0)
        ls = jnp.where(rows >= PAD, ls, 0.0)
        r = lax.broadcasted_iota(jnp.int32, (tm, tm), 0)
        c = lax.broadcasted_iota(jnp.int32, (tm, tm), 1)
        tri = (c <= r).astype(F32)
        cum = jnp.dot(tri, ls, precision=lax.Precision.HIGHEST, preferred_element_type=F32) + c_s[...]
        o_ref[...] = cum
        c_s[...] = cum[tm - 1:tm, :]

    row = pl.BlockSpec((tm, 128), lambda m: (m, 0))
    return pl.pallas_call(
        body, name=name, grid=(M // tm,),
        in_specs=[row, pl.BlockSpec((1, 128), lambda m: (0, 0))], out_specs=row,
        out_shape=jax.ShapeDtypeStruct((M, 128), F32), scratch_shapes=[pltpu.VMEM((1, 128), F32)],
        compiler_params=_params(("arbitrary",)))(f_logit, bias)


def _forget_cumsum_bwd(f_logit, bias, dF, name):
    M = f_logit.shape[0]
    tm = 256
    nb = M // tm

    def body(f_ref, b_ref, d_ref, o_ref, db_ref, c_s):
        i = pl.program_id(0)

        @pl.when(i == 0)
        def _():
            c_s[...] = jnp.zeros_like(c_s)
            db_ref[...] = jnp.zeros_like(db_ref)
        r = lax.broadcasted_iota(jnp.int32, (tm, tm), 0)
        c = lax.broadcasted_iota(jnp.int32, (tm, tm), 1)
        tri = (c >= r).astype(F32)
        cum = jnp.dot(tri, d_ref[...], precision=lax.Precision.HIGHEST, preferred_element_type=F32) + c_s[...]
        c_s[...] = cum[0:1, :]
        _, lsn = _log_sigmoids(f_ref[...] + b_ref[...])
        rows = (nb - 1 - i) * tm + lax.broadcasted_iota(jnp.int32, (tm, 1), 0)
        dl = jnp.where(rows >= PAD, cum * jnp.exp(lsn), 0.0)
        o_ref[...] = dl
        db_ref[...] += jnp.sum(dl, axis=0, keepdims=True)

    row = pl.BlockSpec((tm, 128), lambda m: (nb - 1 - m, 0))
    vec = pl.BlockSpec((1, 128), lambda m: (0, 0))
    return pl.pallas_call(
        body, name=name, grid=(nb,), in_specs=[row, vec, row], out_specs=[row, vec],
        out_shape=[jax.ShapeDtypeStruct((M, 128), F32), jax.ShapeDtypeStruct((1, 128), F32)],
        scratch_shapes=[pltpu.VMEM((1, 128), F32)],
        compiler_params=_params(("arbitrary",)))(f_logit, bias, dF)


def _adamw(w, slots, m, v, name):
    R, C = w.shape
    tr = _pick(R, (128, 64, 32, 16, 8))
    c1 = 1.0 - ADAM_B1 ** ADAM_STEP
    c2 = 1.0 - ADAM_B2 ** ADAM_STEP

    def body(w_ref, s_ref, m_ref, v_ref, g_ref, d_ref, mo_ref, vo_ref):
        g = s_ref[0].astype(F32)
        for k in range(1, N_DEV):
            g = g + s_ref[k].astype(F32)
        mn = ADAM_B1 * m_ref[...] + (1.0 - ADAM_B1) * g
        vn = ADAM_B2 * v_ref[...] + (1.0 - ADAM_B2) * (g * g)
        m_hat = mn / c1
        v_hat = vn / c2
        g_ref[...] = g
        d_ref[...] = -ADAM_LR * (m_hat / (jnp.sqrt(v_hat) + ADAM_EPS) + ADAM_WD * w_ref[...])
        mo_ref[...] = mn
        vo_ref[...] = vn

    row = pl.BlockSpec((tr, C), lambda r: (r, 0))
    shp = jax.ShapeDtypeStruct((R, C), F32)
    return pl.pallas_call(
        body, name=name, grid=(R // tr,),
        in_specs=[row, pl.BlockSpec((N_DEV, tr, C), lambda r: (0, r, 0)), row, row],
        out_specs=[row, row, row, row], out_shape=[shp, shp, shp, shp],
        compiler_params=_params(("parallel",)))(w, slots, m, v)


def _position():
    x, y, c = lax.axis_index("x"), lax.axis_index("y"), lax.axis_index("c")
    return x, y, c


def _all_gather(shard, name):
    R, C = shard.shape

    def body(x_ref, out_ref, send_sems, recv_sems, local_sem):
        x, y, c = _position()
        me, sibling = (x, y, c), (x, y, 1 - c)
        chips = [(1 - x, y), (x, 1 - y), (1 - x, 1 - y)]

        def slot(px, py, pc):
            return out_ref.at[4 * px + 2 * py + pc]

        def copy(k, block, to, src=None):
            return pltpu.make_async_remote_copy(
                src_ref=slot(*block) if src is None else src, dst_ref=slot(*block),
                send_sem=send_sems.at[k], recv_sem=recv_sems.at[k], device_id=to, device_id_type=MESH)

        mine = pltpu.make_async_copy(x_ref, slot(*me), local_sem)
        mine.start()
        first = [copy(0, me, sibling, src=x_ref)]
        first += [copy(1 + j, me, (*chip, c), src=x_ref) for j, chip in enumerate(chips)]
        for cp in first:
            cp.start()
        passed = [copy(4 + j, (*chip, c), sibling) for j, chip in enumerate(chips)]
        for j, chip in enumerate(chips):
            copy(1 + j, (*chip, c), me).wait_recv()
            passed[j].start()
        copy(0, sibling, me).wait_recv()
        for j, chip in enumerate(chips):
            copy(4 + j, (*chip, 1 - c), me).wait_recv()
        for cp in first + passed:
            cp.wait_send()
        mine.wait()

    return pl.pallas_call(
        body, name=name, out_shape=jax.ShapeDtypeStruct((N_DEV, R, C), shard.dtype),
        in_specs=[pl.BlockSpec(memory_space=pl.ANY)], out_specs=pl.BlockSpec(memory_space=pl.ANY),
        scratch_shapes=[pltpu.SemaphoreType.DMA((7,)), pltpu.SemaphoreType.DMA((7,)), pltpu.SemaphoreType.DMA],
    )(shard)


def _exchange_slots(slots, name):
    _, R, C = slots.shape

    def body(g_ref, land_ref, send_sems, recv_sems, local_sem):
        x, y, c = _position()
        me = 4 * x + 2 * y + c
        mine = pltpu.make_async_copy(g_ref.at[me], land_ref.at[me], local_sem)
        mine.start()
        copies = []
        for k in range(1, N_DEV):
            px = 1 - x if k & 4 else x
            py = 1 - y if k & 2 else y
            pc = 1 - c if k & 1 else c
            peer = 4 * px + 2 * py + pc
            send = pltpu.make_async_remote_copy(
                src_ref=g_ref.at[peer], dst_ref=land_ref.at[me], send_sem=send_sems.at[k - 1],
                recv_sem=recv_sems.at[k - 1], device_id=(px, py, pc), device_id_type=MESH)
            recv = pltpu.make_async_remote_copy(
                src_ref=g_ref.at[me], dst_ref=land_ref.at[peer], send_sem=send_sems.at[k - 1],
                recv_sem=recv_sems.at[k - 1], device_id=(px, py, pc), device_id_type=MESH)
            send.start()
            copies.append((send, recv))
        for send, recv in copies:
            recv.wait_recv()
        for send, recv in copies:
            send.wait_send()
        mine.wait()

    return pl.pallas_call(
        body, name=name, out_shape=jax.ShapeDtypeStruct(slots.shape, slots.dtype),
        in_specs=[pl.BlockSpec(memory_space=pl.ANY)], out_specs=pl.BlockSpec(memory_space=pl.ANY),
        scratch_shapes=[pltpu.SemaphoreType.DMA((7,)), pltpu.SemaphoreType.DMA((7,)), pltpu.SemaphoreType.DMA],
    )(slots)


SHARDED = (("sb_w_qkv", 2), ("sb_w_o", 1), ("mla_w_down", 1), ("mla_w_uq", 2), ("mla_w_ukv", 2), ("mla_w_o", 1),
           ("fox_w_qkvf", 2), ("fox_w_o", 1), ("ffn_w_gate", 2), ("ffn_w_up", 2), ("ffn_w_down", 1),
           ("pool_w", 2), ("meta", 1), ("mla_q_norm", 1), ("mla_kv_norm", 1))
SMALL_F32 = ("meta", "mla_q_norm", "mla_kv_norm")
REPLICATED = ("norm_mix", "norm_ffn", "pool_scale", "fox_b_f", "final_norm")
SECTION_ROWS = 16
LANES = 1024


def _sections(shapes, names):
    table, row = {}, 0
    for n in names:
        size = int(np.prod(shapes[n]))
        rows = -(-size // (LANES * SECTION_ROWS)) * SECTION_ROWS
        table[n] = (row, rows, size)
        row += rows
    total = -(-row // 128) * 128 if row > 128 else -(-row // 16) * 16
    return table, total


def _pack(arrays, table, total, names):
    parts, row = [], 0
    for n in names:
        r0, rows, size = table[n]
        flat = arrays[n].reshape(-1).astype(F32)
        parts.append(jnp.pad(flat, (0, rows * LANES - size)))
        row = r0 + rows
    if total > row:
        parts.append(jnp.zeros(((total - row) * LANES,), F32))
    return jnp.concatenate(parts).reshape(total, LANES)


def _unpack(buf, table, shapes, names):
    out = {}
    for n in names:
        r0, rows, size = table[n]
        out[n] = buf[r0:r0 + rows].reshape(-1)[:size].reshape(shapes[n])
    return out


def _full_from_gathered(g, axis):
    g = jnp.moveaxis(g, 0, axis)
    shp = g.shape
    return g.reshape(shp[:axis] + (shp[axis] * shp[axis + 1],) + shp[axis + 2:])


def _slots_from_full(full, axis):
    shp = full.shape
    g = full.reshape(shp[:axis] + (N_DEV, shp[axis] // N_DEV) + shp[axis + 1:])
    return jnp.moveaxis(g, axis, 0)


def _pairs_col(f16):
    M = f16.shape[0]
    return jnp.transpose(f16.reshape(M, N_PAIRS, 2), (1, 0, 2))


def _pairs_row(f16):
    M = f16.shape[0]
    return jnp.transpose(f16.reshape(M, N_PAIRS, 2), (1, 2, 0))


def _local_step(x, target, W, P):
    S = x.shape[0]
    M = S + ROW0
    G = {}
    gain = lambda name, i: P[name][i][None, :]
    h0 = jnp.concatenate([jnp.zeros((PAD, D_MODEL), F32), P["meta"], x], axis=0)

    def ffn_fwd(h1, i):
        b = _norm_fwd(h1, gain("norm_ffn", i), BF16, f"ffn{i}_norm")
        g, u, act = _ffn_up(b, W["ffn_w_gu"][i], f"ffn{i}_up")
        h2 = _mm_nn(act, W["ffn_w_down"][i], F32, f"ffn{i}_down", res=h1)
        return h2, (h1, b, g, u, act)

    def ffn_bwd(dh2, saved, i):
        h1, b, g, u, act = saved
        dgu = _ffn_dact(dh2, W["ffn_w_down"][i], g, u, f"ffn{i}_dact")
        G.setdefault("ffn_w_down", {})[i] = _mm_tn(act, dh2, f"ffn{i}_dwd")
        db = _mm_nt(dgu, W["ffn_w_gu"][i], F32, f"ffn{i}_db")
        G.setdefault("ffn_w_gu", {})[i] = _mm_tn(b, dgu, f"ffn{i}_dwgu")
        dh1, dgain = _norm_bwd(h1, gain("norm_ffn", i), db, dh2, f"ffn{i}_dnorm")
        G.setdefault("norm_ffn", {})[i] = dgain
        return dh1

    a0 = _norm_fwd(h0, gain("norm_mix", 0), F32, "mix0_norm")
    h1_0, pooled = _pool_fwd(h0, a0, W["pool_w"], P["pool_scale"], "pool_fwd")
    h_1, ffn0 = ffn_fwd(h1_0, 0)

    sb_scale = HEAD_DIM ** -0.5
    a1 = _norm_fwd(h_1, gain("norm_mix", 1), BF16, "mix1_norm")
    sb_qkv = _mm_nn(a1, W["sb_w_qkv"], BF16, "sb_qkv")
    sb_o, sb_tot = _sb_fwd(sb_qkv, sb_scale, "sb_fwd")
    h1_1 = _mm_nn(sb_o, W["sb_w_o"], F32, "sb_out", res=h_1)
    h_2, ffn1 = ffn_fwd(h1_1, 1)

    mla_scale = (MLA_NOPE + MLA_ROPE) ** -0.5
    cos_t, sin_t = _rope_tables(M)
    a2 = _norm_fwd(h_2, gain("norm_mix", 2), BF16, "mix2_norm")
    down = _mm_nn(a2, W["mla_w_down"], F32, "mla_down")
    dq_raw = down[:, :MLA_Q_RANK]
    dkv_raw = down[:, MLA_Q_RANK:MLA_Q_RANK + MLA_KV_RANK]
    kr_raw = down[:, MLA_Q_RANK + MLA_KV_RANK:]
    c_q = _norm_fwd(dq_raw, P["mla_q_norm"], BF16, "mla_qnorm")
    c_kv = _norm_fwd(dkv_raw, P["mla_kv_norm"], BF16, "mla_kvnorm")
    q_lin = _mm_nn(c_q, W["mla_w_uq"], F32, "mla_uq")
    q_all = _rope(q_lin, cos_t, sin_t, BF16, "mla_qrope", lead=D_MODEL)
    kv_all = _mm_nn(c_kv, W["mla_w_ukv"], BF16, "mla_ukv")
    kr_in = jnp.concatenate([kr_raw, kr_raw, jnp.zeros((M, 64), F32)], axis=1)
    kr = _rope(kr_in, cos_t, sin_t, BF16, "mla_krope")
    q_rope = q_all[:, D_MODEL:]
    mla_qkv = (q_all, kv_all, kv_all)
    mla_cols = (0, 0, N_PAIRS)
    mla_o, mla_lse = _softmax_fwd(mla_qkv, mla_cols, mla_scale, "mla_fwd", rope=(q_rope, kr))
    h1_2 = _mm_nn(mla_o, W["mla_w_o"], F32, "mla_out", res=h_2)
    h_3, ffn2 = ffn_fwd(h1_2, 2)

    fox_scale = HEAD_DIM ** -0.5
    a3 = _norm_fwd(h_3, gain("norm_mix", 3), BF16, "mix3_norm")
    fox_qkv = _mm_nn(a3, W["fox_w_qkv"], BF16, "fox_qkv")
    f_logit = _mm_nn(a3, W["fox_w_f"], F32, "fox_f")
    b_f = jnp.pad(P["fox_b_f"], ((0, 0), (0, 128 - N_HEADS)))
    Fc = _forget_cumsum(f_logit, b_f, "fox_cumsum")
    decay = (_pairs_col(Fc[:, :N_HEADS]), _pairs_row(Fc[:, :N_HEADS]))
    fox_qkv3 = (fox_qkv, fox_qkv, fox_qkv)
    fox_cols = (0, N_PAIRS, 2 * N_PAIRS)
    fox_o, fox_lse, fox_ox = _softmax_fwd(fox_qkv3, fox_cols, fox_scale, "fox_fwd", decay=decay)
    h1_3 = _mm_nn(fox_o, W["fox_w_o"], F32, "fox_out", res=h_3)
    h_4, ffn3 = ffn_fwd(h1_3, 3)

    sq, dh, dgain = _loss_head(h_4, P["final_norm"][None, :], target, "loss_head")
    G["final_norm"] = dgain[0]

    dh = ffn_bwd(dh, ffn3, 3)
    do = _mm_nt(dh, W["fox_w_o"], BF16, "fox_do")
    G["fox_w_o"] = _mm_tn(fox_o, dh, "fox_dwo")
    dq, dk, dv, dfk = _softmax_bwd(fox_qkv3, fox_cols, fox_ox, do, fox_lse, fox_scale, "fox_bwd", decay=decay)
    dF = jnp.transpose(dfk, (2, 0, 1)).reshape(M, N_HEADS)
    dF = jnp.pad(dF, ((0, 0), (0, 128 - N_HEADS)))
    dlogit, db_f = _forget_cumsum_bwd(f_logit, b_f, dF, "fox_dcumsum")
    G["fox_b_f"] = db_f[:, :N_HEADS]
    dproj = jnp.concatenate([dq, dk.astype(BF16), dv.astype(BF16), dlogit.astype(BF16)], axis=1)
    da = _mm_nt(dproj, W["fox_w_qkvf"], F32, "fox_da")
    G["fox_w_qkvf"] = _mm_tn(a3, dproj, "fox_dwqkvf")
    dh, dgain = _norm_bwd(h_3, gain("norm_mix", 3), da, dh, "mix3_dnorm")
    G.setdefault("norm_mix", {})[3] = dgain

    dh = ffn_bwd(dh, ffn2, 2)
    do = _mm_nt(dh, W["mla_w_o"], BF16, "mla_do")
    G["mla_w_o"] = _mm_tn(mla_o, dh, "mla_dwo")
    dq, dk, dv, dqr, dkr = _softmax_bwd(mla_qkv, mla_cols, mla_o, do, mla_lse, mla_scale, "mla_bwd",
                                        rope=(q_rope, kr))
    dqr = _rope(dqr, cos_t, sin_t, BF16, "mla_dqrope", inverse=True)
    dq_all = jnp.concatenate([dq, dqr], axis=1)
    dkr_sum = _rope(jnp.sum(dkr, axis=0), cos_t, sin_t, F32, "mla_dkrope", inverse=True)
    dkr_raw = dkr_sum[:, :MLA_ROPE] + dkr_sum[:, MLA_ROPE:2 * MLA_ROPE]
    dkv_all = jnp.concatenate([dk.astype(BF16), dv.astype(BF16)], axis=1)
    dc_q = _mm_nt(dq_all, W["mla_w_uq"], F32, "mla_dcq")
    G["mla_w_uq"] = _mm_tn(c_q, dq_all, "mla_dwuq")
    dc_kv = _mm_nt(dkv_all, W["mla_w_ukv"], F32, "mla_dckv")
    G["mla_w_ukv"] = _mm_tn(c_kv, dkv_all, "mla_dwukv")
    ddq_raw, G["mla_q_norm"] = _norm_bwd(dq_raw, P["mla_q_norm"], dc_q, None, "mla_dqnorm")
    ddkv_raw, G["mla_kv_norm"] = _norm_bwd(dkv_raw, P["mla_kv_norm"], dc_kv, None, "mla_dkvnorm")
    ddown = jnp.concatenate([ddq_raw, ddkv_raw, dkr_raw], axis=1).astype(BF16)
    da = _mm_nt(ddown, W["mla_w_down"], F32, "mla_da")
    G["mla_w_down"] = _mm_tn(a2, ddown, "mla_dwdown")
    dh, dgain = _norm_bwd(h_2, gain("norm_mix", 2), da, dh, "mix2_dnorm")
    G["norm_mix"][2] = dgain

    dh = ffn_bwd(dh, ffn1, 1)
    do = _mm_nt(dh, W["sb_w_o"], BF16, "sb_do")
    G["sb_w_o"] = _mm_tn(sb_o, dh, "sb_dwo")
    dq, dk, dv = _sb_bwd(sb_qkv, do, sb_tot, sb_scale, "sb_bwd")
    dqkv = jnp.concatenate([dq, dk.astype(BF16), dv.astype(BF16)], axis=1)
    da = _mm_nt(dqkv, W["sb_w_qkv"], F32, "sb_da")
    G["sb_w_qkv"] = _mm_tn(a1, dqkv, "sb_dwqkv")
    dh, dgain = _norm_bwd(h_1, gain("norm_mix", 1), da, dh, "mix1_dnorm")
    G["norm_mix"][1] = dgain

    dh = ffn_bwd(dh, ffn0, 0)
    dpc, G["pool_w"], G["pool_scale"] = _pool_bwd_mix(dh, pooled, W["pool_w"], P["pool_scale"], "pool_dmix")
    da = _pool_bwd_window(dpc, "pool_dwindow")
    dh, dgain = _norm_bwd(h0, gain("norm_mix", 0), da, dh, "mix0_dnorm")
    G["norm_mix"][0] = dgain

    G["norm_mix"] = jnp.concatenate([G["norm_mix"][i] for i in range(DEPTH)], axis=0)
    G["norm_ffn"] = jnp.concatenate([G["norm_ffn"][i] for i in range(DEPTH)], axis=0)
    G["ffn_w_down"] = jnp.stack([G["ffn_w_down"][i] for i in range(DEPTH)])
    G["ffn_w_gu"] = jnp.stack([G["ffn_w_gu"][i] for i in range(DEPTH)])
    return sq, dh, G


def _kernel_weights(full):
    W = {}
    W["pool_w"] = full["pool_w"][0]
    W["sb_w_qkv"] = full["sb_w_qkv"][0]
    W["sb_w_o"] = full["sb_w_o"][0]
    W["mla_w_down"] = full["mla_w_down"][0]
    uq = full["mla_w_uq"][0].reshape(MLA_Q_RANK, N_HEADS, MLA_NOPE + MLA_ROPE)
    nope = uq[:, :, :MLA_NOPE].reshape(MLA_Q_RANK, N_HEADS * MLA_NOPE)
    rope = uq[:, :, MLA_NOPE:].reshape(MLA_Q_RANK, N_PAIRS, 2 * MLA_ROPE)
    rope = jnp.pad(rope, ((0, 0), (0, 0), (0, 128 - 2 * MLA_ROPE))).reshape(MLA_Q_RANK, N_PAIRS * 128)
    W["mla_w_uq"] = jnp.concatenate([nope, rope], axis=1)
    ukv = full["mla_w_ukv"][0].reshape(MLA_KV_RANK, N_HEADS, 2, HEAD_DIM)
    W["mla_w_ukv"] = jnp.transpose(ukv, (0, 2, 1, 3)).reshape(MLA_KV_RANK, 2 * N_HEADS * HEAD_DIM)
    W["mla_w_o"] = full["mla_w_o"][0]
    qkvf = full["fox_w_qkvf"][0]
    n_qkv = 3 * N_HEADS * HEAD_DIM
    W["fox_w_qkv"] = qkvf[:, :n_qkv]
    W["fox_w_f"] = jnp.pad(qkvf[:, n_qkv:], ((0, 0), (0, 128 - N_HEADS)))
    W["fox_w_qkvf"] = jnp.concatenate([W["fox_w_qkv"], W["fox_w_f"]], axis=1)
    W["fox_w_o"] = full["fox_w_o"][0]
    W["ffn_w_gu"] = jnp.concatenate([full["ffn_w_gate"], full["ffn_w_up"]], axis=2)
    W["ffn_w_down"] = full["ffn_w_down"]
    return W


def _reference_grads(G):
    out = {}
    out["pool_w"] = G["pool_w"][None]
    for n in ("sb_w_qkv", "sb_w_o", "mla_w_down", "mla_w_o", "fox_w_o"):
        out[n] = G[n][None]
    duq = G["mla_w_uq"]
    nope = duq[:, :N_HEADS * MLA_NOPE].reshape(MLA_Q_RANK, N_HEADS, MLA_NOPE)
    rope = duq[:, N_HEADS * MLA_NOPE:].reshape(MLA_Q_RANK, N_PAIRS, 128)[:, :, :2 * MLA_ROPE]
    rope = rope.reshape(MLA_Q_RANK, N_HEADS, MLA_ROPE)
    out["mla_w_uq"] = jnp.concatenate([nope, rope], axis=2).reshape(1, MLA_Q_RANK, -1)
    dukv = G["mla_w_ukv"].reshape(MLA_KV_RANK, 2, N_HEADS, HEAD_DIM)
    out["mla_w_ukv"] = jnp.transpose(dukv, (0, 2, 1, 3)).reshape(1, MLA_KV_RANK, -1)
    out["fox_w_qkvf"] = G["fox_w_qkvf"][None, :, :3 * N_HEADS * HEAD_DIM + N_HEADS]
    out["ffn_w_gate"] = G["ffn_w_gu"][:, :, :D_FF]
    out["ffn_w_up"] = G["ffn_w_gu"][:, :, D_FF:]
    out["ffn_w_down"] = G["ffn_w_down"]
    out["mla_q_norm"] = G["mla_q_norm"]
    out["mla_kv_norm"] = G["mla_kv_norm"]
    return out


WEIGHT_NAMES = ("meta", "norm_mix", "norm_ffn", "pool_w", "pool_scale", "sb_w_qkv", "sb_w_o", "mla_w_down",
                "mla_q_norm", "mla_kv_norm", "mla_w_uq", "mla_w_ukv", "mla_w_o", "fox_w_qkvf", "fox_b_f",
                "fox_w_o", "ffn_w_gate", "ffn_w_up", "ffn_w_down", "final_norm")


def kernel(x, meta, norm_mix, norm_ffn, pool_w, pool_scale, sb_w_qkv, sb_w_o, mla_w_down, mla_q_norm, mla_kv_norm, mla_w_uq, mla_w_ukv, mla_w_o, fox_w_qkvf, fox_b_f, fox_w_o, ffn_w_gate, ffn_w_up, ffn_w_down, final_norm, loss_target, m_meta, m_norm_mix, m_norm_ffn, m_pool_w, m_pool_scale, m_sb_w_qkv, m_sb_w_o, m_mla_w_down, m_mla_q_norm, m_mla_kv_norm, m_mla_w_uq, m_mla_w_ukv, m_mla_w_o, m_fox_w_qkvf, m_fox_b_f, m_fox_w_o, m_ffn_w_gate, m_ffn_w_up, m_ffn_w_down, m_final_norm, v_meta, v_norm_mix, v_norm_ffn, v_pool_w, v_pool_scale, v_sb_w_qkv, v_sb_w_o, v_mla_w_down, v_mla_q_norm, v_mla_kv_norm, v_mla_w_uq, v_mla_w_ukv, v_mla_w_o, v_fox_w_qkvf, v_fox_b_f, v_fox_w_o, v_ffn_w_gate, v_ffn_w_up, v_ffn_w_down, v_final_norm):
    w = dict(meta=meta, norm_mix=norm_mix, norm_ffn=norm_ffn, pool_w=pool_w, pool_scale=pool_scale,
             sb_w_qkv=sb_w_qkv, sb_w_o=sb_w_o, mla_w_down=mla_w_down, mla_q_norm=mla_q_norm,
             mla_kv_norm=mla_kv_norm, mla_w_uq=mla_w_uq, mla_w_ukv=mla_w_ukv, mla_w_o=mla_w_o,
             fox_w_qkvf=fox_w_qkvf, fox_b_f=fox_b_f, fox_w_o=fox_w_o, ffn_w_gate=ffn_w_gate, ffn_w_up=ffn_w_up,
             ffn_w_down=ffn_w_down, final_norm=final_norm)
    m = dict(meta=m_meta, norm_mix=m_norm_mix, norm_ffn=m_norm_ffn, pool_w=m_pool_w, pool_scale=m_pool_scale,
             sb_w_qkv=m_sb_w_qkv, sb_w_o=m_sb_w_o, mla_w_down=m_mla_w_down, mla_q_norm=m_mla_q_norm,
             mla_kv_norm=m_mla_kv_norm, mla_w_uq=m_mla_w_uq, mla_w_ukv=m_mla_w_ukv, mla_w_o=m_mla_w_o,
             fox_w_qkvf=m_fox_w_qkvf, fox_b_f=m_fox_b_f, fox_w_o=m_fox_w_o, ffn_w_gate=m_ffn_w_gate,
             ffn_w_up=m_ffn_w_up, ffn_w_down=m_ffn_w_down, final_norm=m_final_norm)
    v = dict(meta=v_meta, norm_mix=v_norm_mix, norm_ffn=v_norm_ffn, pool_w=v_pool_w, pool_scale=v_pool_scale,
             sb_w_qkv=v_sb_w_qkv, sb_w_o=v_sb_w_o, mla_w_down=v_mla_w_down, mla_q_norm=v_mla_q_norm,
             mla_kv_norm=v_mla_kv_norm, mla_w_uq=v_mla_w_uq, mla_w_ukv=v_mla_w_ukv, mla_w_o=v_mla_w_o,
             fox_w_qkvf=v_fox_w_qkvf, fox_b_f=v_fox_b_f, fox_w_o=v_fox_w_o, ffn_w_gate=v_ffn_w_gate,
             ffn_w_up=v_ffn_w_up, ffn_w_down=v_ffn_w_down, final_norm=v_final_norm)

    sh_names = tuple(n for n, _ in SHARDED)
    sh_axis = dict(SHARDED)
    shapes = {n: w[n].shape for n in WEIGHT_NAMES}
    table, total = _sections(shapes, sh_names)
    stable, stotal = _sections(shapes, SMALL_F32)
    rtable, rtotal = _sections(shapes, REPLICATED)

    w_pack = _pack(w, table, total, sh_names)
    gathered = _all_gather(w_pack.astype(BF16), "gather_weights")
    small = _all_gather(_pack(w, stable, stotal, SMALL_F32), "gather_small")
    full = {}
    for n in sh_names:
        if n in SMALL_F32:
            r0, rows, size = stable[n]
            g = small[:, r0:r0 + rows].reshape(N_DEV, -1)[:, :size].reshape((N_DEV,) + shapes[n])
        else:
            r0, rows, size = table[n]
            g = gathered[:, r0:r0 + rows].reshape(N_DEV, -1)[:, :size].reshape((N_DEV,) + shapes[n])
        full[n] = _full_from_gathered(g, sh_axis[n])
    W = _kernel_weights(full)
    P = dict(meta=full["meta"], mla_q_norm=full["mla_q_norm"], mla_kv_norm=full["mla_kv_norm"],
             norm_mix=norm_mix, norm_ffn=norm_ffn, pool_scale=pool_scale, fox_b_f=fox_b_f, final_norm=final_norm)

    sq, dh0, G = _local_step(x[0], loss_target[0], W, P)
    loss = lax.psum(0.5 * jnp.sum(sq) / D_MODEL, ("x", "y", "c"))
    grad_x = dh0[ROW0:][None]

    gw = _reference_grads(G)
    gw["meta"] = dh0[PAD:ROW0]
    slots = {n: _slots_from_full(gw[n], sh_axis[n] ) for n in sh_names}
    slot_shapes = {n: (N_DEV,) + shapes[n] for n in sh_names}
    parts = []
    row = 0
    for n in sh_names:
        r0, rows, size = table[n]
        flat = slots[n].reshape(N_DEV, -1)
        parts.append(jnp.pad(flat, ((0, 0), (0, rows * LANES - size))))
        row = r0 + rows
    if total > row:
        parts.append(jnp.zeros((N_DEV, (total - row) * LANES), F32))
    slot_pack = jnp.concatenate(parts, axis=1).reshape(N_DEV, total, LANES)
    landed = _exchange_slots(slot_pack, "exchange_grads")
    g_s, d_s, m_s, v_s = _adamw(w_pack, landed, _pack(m, table, total, sh_names), _pack(v, table, total, sh_names),
                                "adamw_sharded")

    rep_g = dict(norm_mix=G["norm_mix"], norm_ffn=G["norm_ffn"], pool_scale=G["pool_scale"], fox_b_f=G["fox_b_f"],
                 final_norm=G["final_norm"])
    rep_all = _all_gather(_pack(rep_g, rtable, rtotal, REPLICATED), "gather_replicated_grads")
    g_r, d_r, m_r, v_r = _adamw(_pack(w, rtable, rtotal, REPLICATED), rep_all, _pack(m, rtable, rtotal, REPLICATED),
                                _pack(v, rtable, rtotal, REPLICATED), "adamw_replicated")

    outs = []
    for packed_s, packed_r in ((g_s, g_r), (d_s, d_r), (m_s, m_r), (v_s, v_r)):
        got = _unpack(packed_s, table, shapes, sh_names)
        got.update(_unpack(packed_r, rtable, shapes, REPLICATED))
        outs.extend(got[n] for n in WEIGHT_NAMES)
    return (loss, grad_x, *outs)
```

```python
import numpy as np
import jax
import jax.numpy as jnp
from jax import lax
from jax.experimental import pallas as pl
from jax.experimental.pallas import tpu as pltpu

F32 = jnp.float32
BF16 = jnp.bfloat16

N_DEV = 8
D_MODEL = 1024
N_META = 16
PAD = 240
ROW0 = PAD + N_META
EPS = 1e-6
POOL_WINDOWS = (2, 4, 8, 16)
POOL_GROUP = 256
HALO = 128
N_HEADS = 16
HEAD_DIM = 64
N_PAIRS = N_HEADS // 2
MLA_Q_RANK = 384
MLA_KV_RANK = 256
MLA_NOPE = 64
MLA_ROPE = 32
ROPE_THETA = 10000.0
D_FF = 2816
DEPTH = 4
ATTN_TILE = 256
NEG = -1e30
SB_EXIT = -110.0
VMEM_LIMIT = 56 * 2**20
ADAM_TILE_ELEMS = 192 * 1024

ADAM_LR = 0.001
ADAM_B1 = 0.9
ADAM_B2 = 0.999
ADAM_EPS = 1e-08
ADAM_WD = 0.01
ADAM_STEP = 10

MESH = pl.DeviceIdType.MESH


def _params(sem=None):
    return pltpu.CompilerParams(dimension_semantics=sem, vmem_limit_bytes=VMEM_LIMIT)


def _pick(n, cands):
    for c in cands:
        if n % c == 0:
            return c
    return n


def _col_tile(n, cap=1536):
    best = None
    for t in range(128, min(n, cap) + 1, 128):
        if n % t == 0:
            best = t
    return best if best is not None else n


def _dot(a, b):
    return jnp.dot(a, b, preferred_element_type=F32)


def _dot_nt(a, b):
    return lax.dot_general(a, b, (((1,), (1,)), ((), ())), preferred_element_type=F32)


def _dot_tn(a, b):
    return lax.dot_general(a, b, (((0,), (0,)), ((), ())), preferred_element_type=F32)


def _mm_nn(a, b, out_dtype, name, res=None):
    M, K = a.shape
    N = b.shape[1]
    tm = _pick(M, (768, 512, 256, 128))
    tn = _col_tile(N)

    def body(*refs):
        if res is None:
            a_ref, b_ref, o_ref = refs
        else:
            a_ref, b_ref, r_ref, o_ref = refs
        acc = _dot(a_ref[...].astype(BF16), b_ref[...])
        if res is not None:
            acc = acc + r_ref[...]
        o_ref[...] = acc.astype(o_ref.dtype)

    in_specs = [pl.BlockSpec((tm, K), lambda n, m: (m, 0)), pl.BlockSpec((K, tn), lambda n, m: (0, n))]
    args = [a, b]
    if res is not None:
        in_specs.append(pl.BlockSpec((tm, tn), lambda n, m: (m, n)))
        args.append(res)
    return pl.pallas_call(
        body, name=name, grid=(N // tn, M // tm), in_specs=in_specs,
        out_specs=pl.BlockSpec((tm, tn), lambda n, m: (m, n)),
        out_shape=jax.ShapeDtypeStruct((M, N), out_dtype),
        compiler_params=_params(("parallel", "parallel")))(*args)


def _mm_nt(a, w, out_dtype, name):
    M, N = a.shape
    K = w.shape[0]
    tm = _pick(M, (768, 512, 256, 128)) if N <= 3200 else _pick(M, (256, 128))
    tk = _col_tile(K, 1024)

    def body(a_ref, w_ref, o_ref):
        o_ref[...] = _dot_nt(a_ref[...].astype(BF16), w_ref[...]).astype(o_ref.dtype)

    return pl.pallas_call(
        body, name=name, grid=(K // tk, M // tm),
        in_specs=[pl.BlockSpec((tm, N), lambda k, m: (m, 0)), pl.BlockSpec((tk, N), lambda k, m: (k, 0))],
        out_specs=pl.BlockSpec((tm, tk), lambda k, m: (m, k)),
        out_shape=jax.ShapeDtypeStruct((M, K), out_dtype),
        compiler_params=_params(("parallel", "parallel")))(a, w)


def _mm_tn(a, b, name):
    M, K = a.shape
    N = b.shape[1]
    tm = _pick(M, (768, 512, 256, 128))
    tk = _col_tile(K, 1408)
    tn = _col_tile(N, 1408)

    def body(a_ref, b_ref, o_ref):
        @pl.when(pl.program_id(2) == 0)
        def _():
            o_ref[...] = jnp.zeros_like(o_ref)
        o_ref[...] += _dot_tn(a_ref[...].astype(BF16), b_ref[...].astype(BF16))

    return pl.pallas_call(
        body, name=name, grid=(K // tk, N // tn, M // tm),
        in_specs=[pl.BlockSpec((tm, tk), lambda k, n, m: (m, k)), pl.BlockSpec((tm, tn), lambda k, n, m: (m, n))],
        out_specs=pl.BlockSpec((tk, tn), lambda k, n, m: (k, n)),
        out_shape=jax.ShapeDtypeStruct((K, N), F32),
        compiler_params=_params(("parallel", "parallel", "arbitrary")))(a, b)


def _norm_fwd(h, gain, out_dtype, name):
    M, C = h.shape
    tm = _pick(M, (768, 512, 256, 128))

    def body(h_ref, g_ref, a_ref):
        x = h_ref[...]
        r = lax.rsqrt(jnp.mean(x * x, axis=-1, keepdims=True) + EPS)
        a_ref[...] = ((x * r) * g_ref[...]).astype(a_ref.dtype)

    return pl.pallas_call(
        body, name=name, grid=(M // tm,),
        in_specs=[pl.BlockSpec((tm, C), lambda m: (m, 0)), pl.BlockSpec((1, C), lambda m: (0, 0))],
        out_specs=pl.BlockSpec((tm, C), lambda m: (m, 0)),
        out_shape=jax.ShapeDtypeStruct((M, C), out_dtype),
        compiler_params=_params(("parallel",)))(h, gain)


def _norm_bwd(h, gain, da, dres, name):
    M, C = h.shape
    tm = _pick(M, (768, 512, 256, 128))

    def body(*refs):
        if dres is None:
            h_ref, g_ref, da_ref, dh_ref, dg_ref = refs
        else:
            h_ref, g_ref, da_ref, dr_ref, dh_ref, dg_ref = refs
        x = h_ref[...]
        r = lax.rsqrt(jnp.mean(x * x, axis=-1, keepdims=True) + EPS)
        y = x * r
        dav = da_ref[...].astype(F32)
        dy = dav * g_ref[...]
        dh = r * (dy - y * jnp.mean(dy * y, axis=-1, keepdims=True))
        if dres is not None:
            dh = dh + dr_ref[...]
        dh_ref[...] = dh

        @pl.when(pl.program_id(0) == 0)
        def _():
            dg_ref[...] = jnp.zeros_like(dg_ref)
        dg_ref[...] += jnp.sum(dav * y, axis=0, keepdims=True)

    row = pl.BlockSpec((tm, C), lambda m: (m, 0))
    vec = pl.BlockSpec((1, C), lambda m: (0, 0))
    in_specs = [row, vec, row] + ([row] if dres is not None else [])
    args = [h, gain, da] + ([dres] if dres is not None else [])
    return pl.pallas_call(
        body, name=name, grid=(M // tm,), in_specs=in_specs, out_specs=[row, vec],
        out_shape=[jax.ShapeDtypeStruct((M, C), F32), jax.ShapeDtypeStruct((1, C), F32)],
        compiler_params=_params(("arbitrary",)))(*args)


def _ffn_up(b, w_gu, name):
    M, K = b.shape
    F = w_gu.shape[1] // 2
    tm = _pick(M, (768, 512, 256, 128))
    tn = _col_tile(F, 1408)
    nb = F // tn

    def body(b_ref, wg_ref, wu_ref, g_ref, u_ref, act_ref):
        x = b_ref[...]
        g = _dot(x, wg_ref[...])
        u = _dot(x, wu_ref[...])
        g_ref[...] = g
        u_ref[...] = u
        act_ref[...] = ((g * jax.nn.sigmoid(g)) * u).astype(act_ref.dtype)

    blk = pl.BlockSpec((tm, tn), lambda n, m: (m, n))
    return pl.pallas_call(
        body, name=name, grid=(nb, M // tm),
        in_specs=[pl.BlockSpec((tm, K), lambda n, m: (m, 0)),
                  pl.BlockSpec((K, tn), lambda n, m: (0, n)),
                  pl.BlockSpec((K, tn), lambda n, m: (0, n + nb))],
        out_specs=[blk, blk, blk],
        out_shape=[jax.ShapeDtypeStruct((M, F), F32), jax.ShapeDtypeStruct((M, F), F32),
                   jax.ShapeDtypeStruct((M, F), BF16)],
        compiler_params=_params(("parallel", "parallel")))(b, w_gu, w_gu)


def _ffn_dact(dy, w_d, g, u, name):
    M, K = dy.shape
    F = w_d.shape[0]
    tm = _pick(M, (768, 512, 256, 128))
    tn = _col_tile(F, 1408)
    nb = F // tn

    def body(dy_ref, wd_ref, g_ref, u_ref, dg_ref, du_ref):
        dact = _dot_nt(dy_ref[...].astype(BF16), wd_ref[...])
        gv = g_ref[...]
        s = jax.nn.sigmoid(gv)
        silu = gv * s
        dg_ref[...] = (dact * u_ref[...] * (s * (1.0 + gv * (1.0 - s)))).astype(dg_ref.dtype)
        du_ref[...] = (dact * silu).astype(du_ref.dtype)

    blk = pl.BlockSpec((tm, tn), lambda n, m: (m, n))
    dg, du = pl.pallas_call(
        body, name=name, grid=(nb, M // tm),
        in_specs=[pl.BlockSpec((tm, K), lambda n, m: (m, 0)), pl.BlockSpec((tn, K), lambda n, m: (n, 0)), blk, blk],
        out_specs=[blk, blk],
        out_shape=[jax.ShapeDtypeStruct((M, F), BF16), jax.ShapeDtypeStruct((M, F), BF16)],
        compiler_params=_params(("parallel", "parallel")))(dy, w_d, g, u)
    return jnp.concatenate([dg, du], axis=1)


def _loss_head(h, gain, target, name):
    M, C = h.shape
    tm = ROW0
    assert M % tm == 0 and target.shape[0] == M - ROW0

    def body(h_ref, g_ref, t_ref, sq_ref, dh_ref, dg_ref):
        i = pl.program_id(0)

        @pl.when(i == 0)
        def _():
            sq_ref[...] = jnp.zeros_like(sq_ref)
            dg_ref[...] = jnp.zeros_like(dg_ref)
            dh_ref[...] = jnp.zeros_like(dh_ref)

        @pl.when(i > 0)
        def _():
            x = h_ref[...]
            r = lax.rsqrt(jnp.mean(x * x, axis=-1, keepdims=True) + EPS)
            y = x * r
            err = y * g_ref[...] - t_ref[...]
            sq_ref[...] += jnp.sum(err * err, axis=0, keepdims=True)
            da = err * (1.0 / C)
            dy = da * g_ref[...]
            dh_ref[...] = r * (dy - y * jnp.mean(dy * y, axis=-1, keepdims=True))
            dg_ref[...] += jnp.sum(da * y, axis=0, keepdims=True)

    row = pl.BlockSpec((tm, C), lambda m: (m, 0))
    vec = pl.BlockSpec((1, C), lambda m: (0, 0))
    return pl.pallas_call(
        body, name=name, grid=(M // tm,),
        in_specs=[row, vec, pl.BlockSpec((tm, C), lambda m: (jnp.maximum(m - 1, 0), 0))],
        out_specs=[vec, row, vec],
        out_shape=[jax.ShapeDtypeStruct((1, C), F32), jax.ShapeDtypeStruct((M, C), F32),
                   jax.ShapeDtypeStruct((1, C), F32)],
        compiler_params=_params(("arbitrary",)))(h, gain, target)


def _pool_pos(row0, tm):
    return row0 + lax.broadcasted_iota(jnp.int32, (tm, 1), 0) - PAD


def _pool_fwd(h, a, w, scale, name):
    M, C = a.shape
    tm = 256
    hb = tm // HALO

    def body(h_ref, a_ref, halo_ref, w_ref, s_ref, o_ref, p_ref):
        i = pl.program_id(0)
        row0 = i * tm
        ext = jnp.concatenate([halo_ref[...], a_ref[...]], axis=0)
        src = row0 - HALO + lax.broadcasted_iota(jnp.int32, (tm + HALO, 1), 0)
        ext = jnp.where(src >= PAD, ext, 0.0)
        r = lax.broadcasted_iota(jnp.int32, (tm, tm + HALO), 0)
        c = lax.broadcasted_iota(jnp.int32, (tm, tm + HALO), 1)
        pos = _pool_pos(row0, tm)
        for g, win in enumerate(POOL_WINDOWS):
            band = ((c <= r + HALO) & (c > r + HALO - win)).astype(F32)
            cols = slice(g * POOL_GROUP, (g + 1) * POOL_GROUP)
            xg = ext[:, cols]
            tot = jnp.dot(band, xg, precision=lax.Precision.HIGHEST, preferred_element_type=F32)
            cnt = jnp.clip(pos + 1, 1, win).astype(F32)
            pooled = (tot / cnt - xg[HALO:]).astype(BF16)
            p_ref[:, cols] = pooled
            mixed = _dot(pooled, w_ref[g])
            o_ref[:, cols] = h_ref[:, cols] + mixed * s_ref[:, cols]

    row = pl.BlockSpec((tm, C), lambda m: (m, 0))
    return pl.pallas_call(
        body, name=name, grid=(M // tm,),
        in_specs=[row, row, pl.BlockSpec((HALO, C), lambda m: (jnp.maximum(m * hb - 1, 0), 0)),
                  pl.BlockSpec((4, POOL_GROUP, POOL_GROUP), lambda m: (0, 0, 0)),
                  pl.BlockSpec((1, C), lambda m: (0, 0))],
        out_specs=[row, row],
        out_shape=[jax.ShapeDtypeStruct((M, C), F32), jax.ShapeDtypeStruct((M, C), BF16)],
        compiler_params=_params(("parallel",)))(h, a, a, w, scale)


def _pool_bwd_mix(dout, pooled, w, scale, name):
    M, C = dout.shape
    tm = 256

    def body(do_ref, p_ref, w_ref, s_ref, dpc_ref, dw_ref, ds_ref):
        i = pl.program_id(0)

        @pl.when(i == 0)
        def _():
            dw_ref[...] = jnp.zeros_like(dw_ref)
            ds_ref[...] = jnp.zeros_like(ds_ref)

        pos = _pool_pos(i * tm, tm)
        for g, win in enumerate(POOL_WINDOWS):
            cols = slice(g * POOL_GROUP, (g + 1) * POOL_GROUP)
            do = do_ref[:, cols]
            pooled = p_ref[:, cols]
            mixed = _dot(pooled, w_ref[g])
            ds_ref[:, cols] += jnp.sum(do * mixed, axis=0, keepdims=True)
            dmix = (do * s_ref[:, cols]).astype(BF16)
            dw_ref[g] += _dot_tn(pooled, dmix)
            dp = _dot_nt(dmix, w_ref[g])
            cnt = jnp.clip(pos + 1, 1, win).astype(F32)
            dpc_ref[:, cols] = dp / cnt

    row = pl.BlockSpec((tm, C), lambda m: (m, 0))
    wspec = pl.BlockSpec((4, POOL_GROUP, POOL_GROUP), lambda m: (0, 0, 0))
    vec = pl.BlockSpec((1, C), lambda m: (0, 0))
    return pl.pallas_call(
        body, name=name, grid=(M // tm,),
        in_specs=[row, row, wspec, vec], out_specs=[row, wspec, vec],
        out_shape=[jax.ShapeDtypeStruct((M, C), F32), jax.ShapeDtypeStruct((4, POOL_GROUP, POOL_GROUP), F32),
                   jax.ShapeDtypeStruct((1, C), F32)],
        compiler_params=_params(("arbitrary",)))(dout, pooled, w, scale)


def _pool_bwd_window(dpc, name):
    M, C = dpc.shape
    tm = 256
    hb = tm // HALO
    last = M // HALO - 1

    def body(d_ref, halo_ref, da_ref):
        i = pl.program_id(0)
        row0 = i * tm
        ext = jnp.concatenate([d_ref[...], halo_ref[...]], axis=0)
        src = row0 + lax.broadcasted_iota(jnp.int32, (tm + HALO, 1), 0)
        ext = jnp.where(src < M, ext, 0.0)
        r = lax.broadcasted_iota(jnp.int32, (tm, tm + HALO), 0)
        c = lax.broadcasted_iota(jnp.int32, (tm, tm + HALO), 1)
        pos = _pool_pos(row0, tm)
        for g, win in enumerate(POOL_WINDOWS):
            band = ((c >= r) & (c < r + win)).astype(F32)
            cols = slice(g * POOL_GROUP, (g + 1) * POOL_GROUP)
            xg = ext[:, cols]
            tot = jnp.dot(band, xg, precision=lax.Precision.HIGHEST, preferred_element_type=F32)
            cnt = jnp.clip(pos + 1, 1, win).astype(F32)
            da_ref[:, cols] = jnp.where(pos >= 0, tot - xg[:tm] * cnt, 0.0)

    row = pl.BlockSpec((tm, C), lambda m: (m, 0))
    return pl.pallas_call(
        body, name=name, grid=(M // tm,),
        in_specs=[row, pl.BlockSpec((HALO, C), lambda m: (jnp.minimum((m + 1) * hb, last), 0))],
        out_specs=row, out_shape=jax.ShapeDtypeStruct((M, C), F32),
        compiler_params=_params(("parallel",)))(dpc, dpc)


def _head_masks():
    lane = lax.broadcasted_iota(jnp.int32, (1, 128), 1)
    return lane < HEAD_DIM, lane


def _split_heads(x, first):
    z = jnp.zeros_like(x)
    return jnp.where(first, x, z), jnp.where(first, z, x)


def _split_rope(x, lane):
    z = jnp.zeros_like(x)
    return jnp.where(lane < MLA_ROPE, x, z), jnp.where((lane >= MLA_ROPE) & (lane < 2 * MLA_ROPE), x, z)


def _mla_fwd(q_all, kv_all, qr, kr, scale, name):
    M = q_all.shape[0]
    t = ATTN_TILE

    def body(q_ref, k_ref, v_ref, qr_ref, kr_ref, o_ref, lse_ref, m_s, l_s, acc_s):
        i = pl.program_id(1)
        first, lane = _head_masks()
        qs = _split_heads(q_ref[...], first)
        qrs = _split_rope(qr_ref[...], lane)
        m_s[...] = jnp.full_like(m_s, NEG)
        l_s[...] = jnp.zeros_like(l_s)
        acc_s[...] = jnp.zeros_like(acc_s)
        qpos = i * t + lax.broadcasted_iota(jnp.int32, (t, t), 0)
        kidx = lax.broadcasted_iota(jnp.int32, (t, t), 1)

        def step(kb, carry):
            k0 = pl.multiple_of(kb * t, t)
            k = k_ref[pl.ds(k0, t), :]
            kr = kr_ref[pl.ds(k0, t), :]
            vs = _split_heads(v_ref[pl.ds(k0, t), :], first)
            kpos = k0 + kidx
            valid = (kpos <= qpos) & (kpos >= PAD)
            pv = None
            alphas = []
            for hh in range(2):
                s = (_dot_nt(qs[hh], k) + _dot_nt(qrs[hh], kr)) * scale
                s = jnp.where(valid, s, NEG)
                m_old = m_s[hh]
                m_new = jnp.maximum(m_old, jnp.max(s, axis=1, keepdims=True))
                p = jnp.exp(s - m_new)
                alpha = jnp.exp(m_old - m_new)
                l_s[hh] = alpha * l_s[hh] + jnp.sum(p, axis=1, keepdims=True)
                m_s[hh] = m_new
                d = _dot(p.astype(BF16), vs[hh])
                pv = d if pv is None else pv + d
                alphas.append(alpha)
            acc_s[...] = acc_s[...] * jnp.where(first, alphas[0], alphas[1]) + pv
            return carry

        lax.fori_loop(0, i + 1, step, 0)
        o_ref[...] = (acc_s[...] * jnp.where(first, 1.0 / l_s[0], 1.0 / l_s[1])).astype(o_ref.dtype)
        lse_ref[:, 0:1] = m_s[0] + jnp.log(l_s[0])
        lse_ref[:, 1:2] = m_s[1] + jnp.log(l_s[1])

    blk = pl.BlockSpec((t, 128), lambda j, i: (i, j))
    return pl.pallas_call(
        body, name=name, grid=(N_PAIRS, M // t),
        in_specs=[blk, pl.BlockSpec((M, 128), lambda j, i: (0, j)), pl.BlockSpec((M, 128), lambda j, i: (0, N_PAIRS + j)),
                  blk, pl.BlockSpec((M, 128), lambda j, i: (0, 0))],
        out_specs=[blk, pl.BlockSpec((None, t, 2), lambda j, i: (j, i, 0))],
        out_shape=[jax.ShapeDtypeStruct((M, N_PAIRS * 128), BF16), jax.ShapeDtypeStruct((N_PAIRS, M, 2), F32)],
        scratch_shapes=[pltpu.VMEM((2, t, 1), F32), pltpu.VMEM((2, t, 1), F32), pltpu.VMEM((t, 128), F32)],
        compiler_params=_params(("parallel", "arbitrary")))(q_all, kv_all, kv_all, qr, kr)


def _mla_bwd(q_all, kv_all, qr, kr, o, do, lse, scale, name):
    M = q_all.shape[0]
    t = ATTN_TILE

    def body(q_ref, k_ref, v_ref, qr_ref, kr_ref, o_ref, do_ref, lse_ref,
             dq_ref, dk_ref, dv_ref, dqr_ref, dkr_ref, dq_s, dqr_s):
        i = pl.program_id(1)
        first, lane = _head_masks()

        @pl.when(i == 0)
        def _():
            dk_ref[...] = jnp.zeros_like(dk_ref)
            dv_ref[...] = jnp.zeros_like(dv_ref)
            dkr_ref[...] = jnp.zeros_like(dkr_ref)

        qs = _split_heads(q_ref[...], first)
        qrs = _split_rope(qr_ref[...], lane)
        dov = do_ref[...]
        dos = _split_heads(dov, first)
        prod = dov.astype(F32) * o_ref[...].astype(F32)
        deltas = (jnp.sum(jnp.where(first, prod, 0.0), axis=1, keepdims=True),
                  jnp.sum(jnp.where(first, 0.0, prod), axis=1, keepdims=True))
        dq_s[...] = jnp.zeros_like(dq_s)
        dqr_s[...] = jnp.zeros_like(dqr_s)
        qpos = i * t + lax.broadcasted_iota(jnp.int32, (t, t), 0)
        kidx = lax.broadcasted_iota(jnp.int32, (t, t), 1)

        def step(kb, carry):
            k0 = pl.multiple_of(kb * t, t)
            rows = pl.ds(k0, t)
            k = k_ref[rows, :]
            v = v_ref[rows, :]
            kr = kr_ref[rows, :]
            ks = _split_heads(k, first)
            krs = _split_rope(kr, lane)
            kpos = k0 + kidx
            valid = (kpos <= qpos) & (kpos >= PAD)
            dq = dk = dv = dqr = dkr = None
            for hh in range(2):
                s = (_dot_nt(qs[hh], k) + _dot_nt(qrs[hh], kr)) * scale
                s = jnp.where(valid, s, NEG)
                p = jnp.exp(s - lse_ref[:, hh:hh + 1])
                ds = p * (_dot_nt(dos[hh], v) - deltas[hh])
                dsb = (ds * scale).astype(BF16)
                a = _dot(dsb, ks[hh])
                b = _dot_tn(dsb, qs[hh])
                c = _dot_tn(p.astype(BF16), dos[hh])
                e = _dot(dsb, krs[hh])
                f = _dot_tn(dsb, qrs[hh])
                dq = a if dq is None else dq + a
                dk = b if dk is None else dk + b
                dv = c if dv is None else dv + c
                dqr = e if dqr is None else dqr + e
                dkr = f if dkr is None else dkr + f
            dq_s[...] += dq
            dqr_s[...] += dqr
            dk_ref[rows, :] += dk
            dv_ref[rows, :] += dv
            dkr_ref[rows, :] += dkr
            return carry

        lax.fori_loop(0, i + 1, step, 0)
        dq_ref[...] = dq_s[...].astype(dq_ref.dtype)
        dqr_ref[...] = dqr_s[...].astype(dqr_ref.dtype)

    blk = pl.BlockSpec((t, 128), lambda j, i: (i, j))
    col = pl.BlockSpec((M, 128), lambda j, i: (0, j))
    wide = jax.ShapeDtypeStruct((M, N_PAIRS * 128), F32)
    return pl.pallas_call(
        body, name=name, grid=(N_PAIRS, M // t),
        in_specs=[blk, col, pl.BlockSpec((M, 128), lambda j, i: (0, N_PAIRS + j)), blk,
                  pl.BlockSpec((M, 128), lambda j, i: (0, 0)), blk, blk,
                  pl.BlockSpec((None, t, 2), lambda j, i: (j, i, 0))],
        out_specs=[blk, col, col, blk, pl.BlockSpec((None, M, 128), lambda j, i: (j, 0, 0))],
        out_shape=[jax.ShapeDtypeStruct((M, N_PAIRS * 128), BF16), wide, wide,
                   jax.ShapeDtypeStruct((M, N_PAIRS * 128), BF16), jax.ShapeDtypeStruct((N_PAIRS, M, 128), F32)],
        scratch_shapes=[pltpu.VMEM((t, 128), F32), pltpu.VMEM((t, 128), F32)],
        compiler_params=_params(("parallel", "arbitrary")))(q_all, kv_all, kv_all, qr, kr, o, do, lse)


def _tri(t, rel):
    j = lax.broadcasted_iota(jnp.int32, (t, t), 0)
    k = lax.broadcasted_iota(jnp.int32, (t, t), 1)
    m = {"gt": j > k, "le": j <= k, "lt": j < k}[rel]
    return m.astype(BF16)


def _lane_cumsum(x, tri):
    hi = x.astype(BF16)
    lo = (x - hi.astype(F32)).astype(BF16)
    return _dot(hi, tri) + _dot(lo, tri)


def _log_sigmoids(z):
    sp = jnp.log(1.0 + jnp.exp(-jnp.abs(z)))
    return jnp.minimum(z, 0.0) - sp, jnp.minimum(-z, 0.0) - sp


def _sb_fwd(qkv, scale, name):
    M = qkv.shape[0]
    t = ATTN_TILE
    ck, cv = N_PAIRS, 2 * N_PAIRS

    def body(q_ref, k_ref, v_ref, o_ref, tot_ref, c_s, acc_s):
        i = pl.program_id(1)
        first, _ = _head_masks()
        qs = _split_heads(q_ref[...], first)
        c_s[...] = jnp.zeros_like(c_s)
        acc_s[...] = jnp.zeros_like(acc_s)
        tri = _tri(t, "gt")
        qpos = i * t + lax.broadcasted_iota(jnp.int32, (t, t), 0)
        kidx = lax.broadcasted_iota(jnp.int32, (t, t), 1)

        def step(it):
            k0 = pl.multiple_of((i - it) * t, t)
            k = k_ref[pl.ds(k0, t), :]
            vs = _split_heads(v_ref[pl.ds(k0, t), :], first)
            kpos = k0 + kidx
            valid = (kpos < qpos) & (kpos >= PAD)
            pv = None
            for hh in range(2):
                z = _dot_nt(qs[hh], k) * scale
                lb, lkr = _log_sigmoids(z)
                lk = jnp.where(valid, lkr, 0.0)
                later = c_s[hh] + _lane_cumsum(lk, tri)
                a = jnp.where(valid, jnp.exp(lb + later), 0.0)
                c_s[hh] = c_s[hh] + jnp.sum(lk, axis=1, keepdims=True)
                d = _dot(a.astype(BF16), vs[hh])
                pv = d if pv is None else pv + d
            acc_s[...] += pv

        def cond(carry):
            it, go = carry
            return (it <= i) & go

        def walk(carry):
            it, _ = carry
            step(it)
            return it + 1, jnp.max(jnp.maximum(c_s[0], c_s[1])) > SB_EXIT

        walked, _ = lax.while_loop(cond, walk, (jnp.int32(0), True))
        o_ref[...] = acc_s[...].astype(o_ref.dtype)
        tot_ref[:, 0:1] = c_s[0]
        tot_ref[:, 1:2] = c_s[1]
        tot_ref[:, 2:3] = jnp.full((t, 1), walked.astype(F32))

    whole = lambda c0: pl.BlockSpec((M, 128), lambda j, i: (0, c0 + j))
    return pl.pallas_call(
        body, name=name, grid=(N_PAIRS, M // t),
        in_specs=[pl.BlockSpec((t, 128), lambda j, i: (i, j)), whole(ck), whole(cv)],
        out_specs=[pl.BlockSpec((t, 128), lambda j, i: (i, j)), pl.BlockSpec((None, t, 3), lambda j, i: (j, i, 0))],
        out_shape=[jax.ShapeDtypeStruct((M, N_PAIRS * 128), BF16), jax.ShapeDtypeStruct((N_PAIRS, M, 3), F32)],
        scratch_shapes=[pltpu.VMEM((2, t, 1), F32), pltpu.VMEM((t, 128), F32)],
        compiler_params=_params(("parallel", "arbitrary")))(qkv, qkv, qkv)


def _sb_bwd(qkv, do, tot, scale, name):
    M = qkv.shape[0]
    t = ATTN_TILE
    ck, cv = N_PAIRS, 2 * N_PAIRS

    def body(q_ref, k_ref, v_ref, do_ref, tot_ref, dq_ref, dk_ref, dv_ref, pc_s, dc_s, dq_s):
        i = pl.program_id(1)
        first, _ = _head_masks()

        @pl.when(i == 0)
        def _():
            dk_ref[...] = jnp.zeros_like(dk_ref)
            dv_ref[...] = jnp.zeros_like(dv_ref)

        qs = _split_heads(q_ref[...], first)
        dos = _split_heads(do_ref[...], first)
        pc_s[...] = jnp.zeros_like(pc_s)
        dc_s[...] = jnp.zeros_like(dc_s)
        dq_s[...] = jnp.zeros_like(dq_s)
        tri_le = _tri(t, "le")
        tri_lt = _tri(t, "lt")
        qpos = i * t + lax.broadcasted_iota(jnp.int32, (t, t), 0)
        kidx = lax.broadcasted_iota(jnp.int32, (t, t), 1)

        def step(kb, carry):
            k0 = pl.multiple_of(kb * t, t)
            rows = pl.ds(k0, t)
            k = k_ref[rows, :]
            v = v_ref[rows, :]
            ks = _split_heads(k, first)
            kpos = k0 + kidx
            valid = (kpos < qpos) & (kpos >= PAD)
            dq = dk = dv = None
            for hh in range(2):
                z = _dot_nt(qs[hh], k) * scale
                lb, lkr = _log_sigmoids(z)
                lk = jnp.where(valid, lkr, 0.0)
                later = tot_ref[:, hh:hh + 1] - (pc_s[hh] + _lane_cumsum(lk, tri_le))
                a = jnp.where(valid, jnp.exp(lb + later), 0.0)
                dl = a * _dot_nt(dos[hh], v)
                early = dc_s[hh] + _lane_cumsum(dl, tri_lt)
                sg = jnp.exp(lb)
                dz = jnp.where(valid, dl * (1.0 - sg) - early * sg, 0.0) * scale
                pc_s[hh] = pc_s[hh] + jnp.sum(lk, axis=1, keepdims=True)
                dc_s[hh] = dc_s[hh] + jnp.sum(dl, axis=1, keepdims=True)
                dzb = dz.astype(BF16)
                x = _dot(dzb, ks[hh])
                y = _dot_tn(dzb, qs[hh])
                w = _dot_tn(a.astype(BF16), dos[hh])
                dq = x if dq is None else dq + x
                dk = y if dk is None else dk + y
                dv = w if dv is None else dv + w
            dq_s[...] += dq
            dk_ref[rows, :] += dk
            dv_ref[rows, :] += dv
            return carry

        walked = jnp.max(tot_ref[:, 2:3]).astype(jnp.int32)
        lax.fori_loop(i + 1 - walked, i + 1, step, 0)
        dq_ref[...] = dq_s[...].astype(dq_ref.dtype)

    whole = lambda c0: pl.BlockSpec((M, 128), lambda j, i: (0, c0 + j))
    blk = pl.BlockSpec((t, 128), lambda j, i: (i, j))
    col = pl.BlockSpec((M, 128), lambda j, i: (0, j))
    return pl.pallas_call(
        body, name=name, grid=(N_PAIRS, M // t),
        in_specs=[blk, whole(ck), whole(cv), blk, pl.BlockSpec((None, t, 3), lambda j, i: (j, i, 0))],
        out_specs=[blk, col, col],
        out_shape=[jax.ShapeDtypeStruct((M, N_PAIRS * 128), BF16), jax.ShapeDtypeStruct((M, N_PAIRS * 128), F32),
                   jax.ShapeDtypeStruct((M, N_PAIRS * 128), F32)],
        scratch_shapes=[pltpu.VMEM((2, t, 1), F32), pltpu.VMEM((2, t, 1), F32), pltpu.VMEM((t, 128), F32)],
        compiler_params=_params(("parallel", "arbitrary")))(qkv, qkv, qkv, do, tot)


def _rows_between(lo, hi):
    r = lax.broadcasted_iota(jnp.int32, (128, 1), 0)
    return (r >= lo) & (r < hi)


def _lanes_between(lo, hi):
    c = lax.broadcasted_iota(jnp.int32, (1, 128), 1)
    return (c >= lo) & (c < hi)


def _keep(x, mask):
    return jnp.where(mask, x, jnp.zeros_like(x))


def _walk_key_blocks(i, step):
    def mid(kb, carry):
        step(kb, False)
        return carry

    step(0, True)
    lax.fori_loop(1, i, mid, 0)

    @pl.when(i > 0)
    def _():
        step(i, True)


def _valid_mask(i, kb, t):
    kpos = kb * t + lax.broadcasted_iota(jnp.int32, (t, t), 0)
    qpos = i * t + lax.broadcasted_iota(jnp.int32, (t, t), 1)
    return (kpos <= qpos) & (kpos >= PAD)


def _fox_fwd(qkv, qkv_t, f_rows, f_cols, scale, name):
    M = qkv.shape[0]
    t = ATTN_TILE
    ck, cv = N_PAIRS, 2 * N_PAIRS

    def body(qt_ref, k_ref, vt_ref, fq_ref, fk_ref, o_ref, lse_ref, ox_ref, m_s, l_s, acc_s, accx_s):
        i = pl.program_id(1)
        qt = qt_ref[...]
        qts = (_keep(qt, _rows_between(0, 64)), _keep(qt, _rows_between(64, 128)))
        m_s[...] = jnp.full_like(m_s, NEG)
        l_s[...] = jnp.zeros_like(l_s)
        acc_s[...] = jnp.zeros_like(acc_s)
        accx_s[...] = jnp.zeros_like(accx_s)

        def step(kb, masked):
            k0 = pl.multiple_of(kb * t, t)
            rows = pl.ds(k0, t)
            k = k_ref[rows, :]
            if masked:
                valid = _valid_mask(i, kb, t)
            for hh in range(2):
                s = _dot(k, qts[hh]) * scale + (fq_ref[hh:hh + 1, :] - fk_ref[rows, hh:hh + 1])
                if masked:
                    s = jnp.where(valid, s, NEG)
                m_old = m_s[hh]
                m_new = jnp.maximum(m_old, jnp.max(s, axis=0, keepdims=True))
                p = jnp.exp(s - m_new)
                alpha = jnp.exp(m_old - m_new)
                l_s[hh] = alpha * l_s[hh] + jnp.sum(p, axis=0, keepdims=True)
                m_s[hh] = m_new
                pb = p.astype(BF16)
                hr = slice(HEAD_DIM * hh, HEAD_DIM * (hh + 1))
                vt = vt_ref[hr, rows]
                acc_s[hr, :] = acc_s[hr, :] * alpha + _dot(vt, pb)
                accx_s[hr, :] = accx_s[hr, :] * alpha + _dot(vt, (p - pb.astype(F32)).astype(BF16))

        _walk_key_blocks(i, step)
        for hh in range(2):
            hr = slice(HEAD_DIM * hh, HEAD_DIM * (hh + 1))
            inv = 1.0 / l_s[hh]
            o_ref[hr, :] = (acc_s[hr, :] * inv).astype(o_ref.dtype)
            ox_ref[hr, :] = (acc_s[hr, :] + accx_s[hr, :]) * inv
            lse_ref[hh:hh + 1, :] = m_s[hh] + jnp.log(l_s[hh])

    blk = pl.BlockSpec((128, t), lambda j, i: (j, i))
    stat = pl.BlockSpec((None, 2, t), lambda j, i: (j, 0, i))
    return pl.pallas_call(
        body, name=name, grid=(N_PAIRS, M // t),
        in_specs=[blk, pl.BlockSpec((M, 128), lambda j, i: (0, ck + j)), pl.BlockSpec((128, M), lambda j, i: (cv + j, 0)),
                  stat, pl.BlockSpec((None, M, 2), lambda j, i: (j, 0, 0))],
        out_specs=[blk, stat, blk],
        out_shape=[jax.ShapeDtypeStruct((N_PAIRS * 128, M), BF16), jax.ShapeDtypeStruct((N_PAIRS, 2, M), F32),
                   jax.ShapeDtypeStruct((N_PAIRS * 128, M), F32)],
        scratch_shapes=[pltpu.VMEM((2, 1, t), F32), pltpu.VMEM((2, 1, t), F32), pltpu.VMEM((128, t), F32),
                        pltpu.VMEM((128, t), F32)],
        compiler_params=_params(("parallel", "arbitrary")))(qkv_t, qkv, qkv_t, f_rows, f_cols)


def _fox_bwd(qkv, qkv_t, o_t, do, do_t, lse, f_rows, f_cols, scale, name):
    M = qkv.shape[0]
    t = ATTN_TILE
    ck, cv = N_PAIRS, 2 * N_PAIRS

    def body(q_ref, qt_ref, k_ref, kt_ref, v_ref, ot_ref, do_ref, dot_ref, lse_ref, fq_ref, fk_ref,
             dq_ref, dk_ref, dv_ref, cs_ref, dq_s):
        i = pl.program_id(1)

        @pl.when(i == 0)
        def _():
            dk_ref[...] = jnp.zeros_like(dk_ref)
            dv_ref[...] = jnp.zeros_like(dv_ref)
            cs_ref[...] = jnp.zeros_like(cs_ref)

        heads_l = (_lanes_between(0, 64), _lanes_between(64, 128))
        heads_r = (_rows_between(0, 64), _rows_between(64, 128))
        q = q_ref[...]
        qt = qt_ref[...]
        do = do_ref[...]
        dot = dot_ref[...]
        qs = tuple(_keep(q, m) for m in heads_l)
        qts = tuple(_keep(qt, m) for m in heads_r)
        dos = tuple(_keep(do, m) for m in heads_l)
        dots = tuple(_keep(dot, m) for m in heads_r)
        prod = dot.astype(F32) * ot_ref[...]
        deltas = tuple(jnp.sum(prod[HEAD_DIM * hh:HEAD_DIM * (hh + 1)], axis=0, keepdims=True) for hh in range(2))
        ones = tuple(m.astype(BF16) * jnp.ones((t, 128), BF16) for m in heads_l)
        dq_s[...] = jnp.zeros_like(dq_s)

        def step(kb, masked):
            k0 = pl.multiple_of(kb * t, t)
            rows = pl.ds(k0, t)
            k = k_ref[rows, :]
            v = v_ref[rows, :]
            if masked:
                valid = _valid_mask(i, kb, t)
            dk = dv = cs = None
            for hh in range(2):
                s = _dot(k, qts[hh]) * scale + (fq_ref[hh:hh + 1, :] - fk_ref[rows, hh:hh + 1])
                if masked:
                    s = jnp.where(valid, s, NEG)
                p = jnp.exp(s - lse_ref[hh:hh + 1, :])
                ds = p * (_dot(v, dots[hh]) - deltas[hh])
                hi = ds.astype(BF16)
                lo = (ds - hi.astype(F32)).astype(BF16)
                c = _dot(hi, ones[hh]) + _dot(lo, ones[hh])
                dsb = (ds * scale).astype(BF16)
                hr = slice(HEAD_DIM * hh, HEAD_DIM * (hh + 1))
                dq_s[hr, :] += _dot(kt_ref[hr, rows], dsb)
                a = _dot(dsb, qs[hh])
                b = _dot(p.astype(BF16), dos[hh])
                dk = a if dk is None else dk + a
                dv = b if dv is None else dv + b
                cs = c if cs is None else cs + c
            dk_ref[rows, :] += dk
            dv_ref[rows, :] += dv
            cs_ref[rows, :] += cs

        _walk_key_blocks(i, step)
        dq_ref[...] = dq_s[...].astype(dq_ref.dtype)

    rblk = pl.BlockSpec((t, 128), lambda j, i: (i, j))
    tblk = pl.BlockSpec((128, t), lambda j, i: (j, i))
    stat = pl.BlockSpec((None, 2, t), lambda j, i: (j, 0, i))
    col = pl.BlockSpec((M, 128), lambda j, i: (0, j))
    wide = jax.ShapeDtypeStruct((M, N_PAIRS * 128), F32)
    return pl.pallas_call(
        body, name=name, grid=(N_PAIRS, M // t),
        in_specs=[rblk, tblk, pl.BlockSpec((M, 128), lambda j, i: (0, ck + j)),
                  pl.BlockSpec((128, M), lambda j, i: (ck + j, 0)), pl.BlockSpec((M, 128), lambda j, i: (0, cv + j)),
                  tblk, rblk, tblk, stat, stat, pl.BlockSpec((None, M, 2), lambda j, i: (j, 0, 0))],
        out_specs=[tblk, col, col, pl.BlockSpec((None, M, 128), lambda j, i: (j, 0, 0))],
        out_shape=[jax.ShapeDtypeStruct((N_PAIRS * 128, M), BF16), wide, wide,
                   jax.ShapeDtypeStruct((N_PAIRS, M, 128), F32)],
        scratch_shapes=[pltpu.VMEM((128, t), F32)],
        compiler_params=_params(("parallel", "arbitrary")))(qkv, qkv_t, qkv, qkv_t, qkv, o_t, do, do_t, lse,
                                                            f_rows, f_cols)


def _rope_tables(M):
    pos = (jnp.arange(M, dtype=jnp.int32) - PAD).astype(F32)
    inv = ROPE_THETA ** (-jnp.arange(0, MLA_ROPE, 2, dtype=F32) / MLA_ROPE)
    ang = pos[:, None] * inv[None, :]
    cos, sin = jnp.cos(ang), jnp.sin(ang)
    z = jnp.zeros((M, 64), F32)
    cos_t = jnp.concatenate([cos, cos, cos, cos, z], axis=1)
    sin_t = jnp.concatenate([-sin, sin, -sin, sin, z], axis=1)
    return cos_t, sin_t


def _rope(x, cos_t, sin_t, out_dtype, name, inverse=False, lead=0):
    M, C = x.shape
    tm = _pick(M, (768, 512, 256, 128))
    nblk = (C - lead) // 128
    sign = -1.0 if inverse else 1.0

    def body(x_ref, c_ref, s_ref, o_ref):
        lane = lax.broadcasted_iota(jnp.int32, (1, 128), 1)
        low = (lane % MLA_ROPE) < (MLA_ROPE // 2)
        cos = c_ref[...]
        sin = s_ref[...] * sign
        if lead:
            o_ref[:, :lead] = x_ref[:, :lead].astype(o_ref.dtype)
        for b in range(nblk):
            cols = slice(lead + b * 128, lead + (b + 1) * 128)
            v = x_ref[:, cols].astype(F32)
            up = pltpu.roll(v, 128 - MLA_ROPE // 2, 1)
            down = pltpu.roll(v, MLA_ROPE // 2, 1)
            o_ref[:, cols] = (v * cos + jnp.where(low, up, down) * sin).astype(o_ref.dtype)

    row = pl.BlockSpec((tm, C), lambda m: (m, 0))
    tab = pl.BlockSpec((tm, 128), lambda m: (m, 0))
    return pl.pallas_call(
        body, name=name, grid=(M // tm,), in_specs=[row, tab, tab], out_specs=row,
        out_shape=jax.ShapeDtypeStruct((M, C), out_dtype),
        compiler_params=_params(("parallel",)))(x, cos_t, sin_t)


def _forget_cumsum(f_logit, bias, name):
    M = f_logit.shape[0]
    tm = 256

    def body(f_ref, b_ref, o_ref, c_s):
        i = pl.program_id(0)

        @pl.when(i == 0)
        def _():
            c_s[...] = jnp.zeros_like(c_s)
        ls, _ = _log_sigmoids(f_ref[...] + b_ref[...])
        rows = i * tm + lax.broadcasted_iota(jnp.int32, (tm, 1), 0)
        ls = jnp.where(rows >= PAD, ls, 0.0)
        r = lax.broadcasted_iota(jnp.int32, (tm, tm), 0)
        c = lax.broadcasted_iota(jnp.int32, (tm, tm), 1)
        tri = (c <= r).astype(F32)
        cum = jnp.dot(tri, ls, precision=lax.Precision.HIGHEST, preferred_element_type=F32) + c_s[...]
        o_ref[...] = cum
        c_s[...] = cum[tm - 1:tm, :]

    row = pl.BlockSpec((tm, 128), lambda m: (m, 0))
    return pl.pallas_call(
        body, name=name, grid=(M // tm,),
        in_specs=[row, pl.BlockSpec((1, 128), lambda m: (0, 0))], out_specs=row,
        out_shape=jax.ShapeDtypeStruct((M, 128), F32), scratch_shapes=[pltpu.VMEM((1, 128), F32)],
        compiler_params=_params(("arbitrary",)))(f_logit, bias)


def _forget_cumsum_bwd(f_logit, bias, dF, name):
    M = f_logit.shape[0]
    tm = 256
    nb = M // tm

    def body(f_ref, b_ref, d_ref, o_ref, db_ref, c_s):
        i = pl.program_id(0)

        @pl.when(i == 0)
        def _():
            c_s[...] = jnp.zeros_like(c_s)
            db_ref[...] = jnp.zeros_like(db_ref)
        r = lax.broadcasted_iota(jnp.int32, (tm, tm), 0)
        c = lax.broadcasted_iota(jnp.int32, (tm, tm), 1)
        tri = (c >= r).astype(F32)
        cum = jnp.dot(tri, d_ref[...], precision=lax.Precision.HIGHEST, preferred_element_type=F32) + c_s[...]
        c_s[...] = cum[0:1, :]
        _, lsn = _log_sigmoids(f_ref[...] + b_ref[...])
        rows = (nb - 1 - i) * tm + lax.broadcasted_iota(jnp.int32, (tm, 1), 0)
        dl = jnp.where(rows >= PAD, cum * jnp.exp(lsn), 0.0)
        o_ref[...] = dl
        db_ref[...] += jnp.sum(dl, axis=0, keepdims=True)

    row = pl.BlockSpec((tm, 128), lambda m: (nb - 1 - m, 0))
    vec = pl.BlockSpec((1, 128), lambda m: (0, 0))
    return pl.pallas_call(
        body, name=name, grid=(nb,), in_specs=[row, vec, row], out_specs=[row, vec],
        out_shape=[jax.ShapeDtypeStruct((M, 128), F32), jax.ShapeDtypeStruct((1, 128), F32)],
        scratch_shapes=[pltpu.VMEM((1, 128), F32)],
        compiler_params=_params(("arbitrary",)))(f_logit, bias, dF)


def _adamw(w, parts, m, v, name):
    R, C = w.shape
    tr = R
    for d in range(8, R, 8):
        if R % d == 0 and d * C <= ADAM_TILE_ELEMS:
            tr = d
    c1 = 1.0 - ADAM_B1 ** ADAM_STEP
    c2 = 1.0 - ADAM_B2 ** ADAM_STEP

    def body(w_ref, s_ref, m_ref, v_ref, g_ref, d_ref, mo_ref, vo_ref):
        g = s_ref[0].astype(F32)
        for k in range(1, N_DEV):
            g = g + s_ref[k].astype(F32)
        mn = ADAM_B1 * m_ref[...] + (1.0 - ADAM_B1) * g
        vn = ADAM_B2 * v_ref[...] + (1.0 - ADAM_B2) * (g * g)
        m_hat = mn / c1
        v_hat = vn / c2
        g_ref[...] = g
        d_ref[...] = -ADAM_LR * (m_hat / (jnp.sqrt(v_hat) + ADAM_EPS) + ADAM_WD * w_ref[...])
        mo_ref[...] = mn
        vo_ref[...] = vn

    row = pl.BlockSpec((tr, C), lambda r: (r, 0))
    shp = jax.ShapeDtypeStruct((R, C), F32)
    return pl.pallas_call(
        body, name=name, grid=(R // tr,),
        in_specs=[row, pl.BlockSpec((N_DEV, tr, C), lambda r: (0, r, 0)), row, row],
        out_specs=[row, row, row, row], out_shape=[shp, shp, shp, shp],
        compiler_params=_params(("parallel",)))(w, parts, m, v)


def _position():
    return lax.axis_index("x"), lax.axis_index("y"), lax.axis_index("c")


def _all_gather(blocks, name):
    n = len(blocks)

    def body(*refs):
        x_refs, out_refs = refs[:n], refs[n:2 * n]
        send_sems, recv_sems, local_sems = refs[2 * n:]
        x, y, c = _position()
        me, sibling = (x, y, c), (x, y, 1 - c)
        chips = [(1 - x, y), (x, 1 - y), (1 - x, 1 - y)]

        def copies(k, block, to, own=False):
            slot = 4 * block[0] + 2 * block[1] + block[2]
            return [pltpu.make_async_remote_copy(
                src_ref=x_refs[p] if own else out_refs[p].at[slot], dst_ref=out_refs[p].at[slot],
                send_sem=send_sems.at[k, p], recv_sem=recv_sems.at[k, p], device_id=to, device_id_type=MESH)
                for p in range(n)]

        mine = [pltpu.make_async_copy(x_refs[p], out_refs[p].at[4 * x + 2 * y + c], local_sems.at[p]) for p in range(n)]
        for cp in mine:
            cp.start()
        first = copies(0, me, sibling, own=True)
        for j, chip in enumerate(chips):
            first += copies(1 + j, me, (*chip, c), own=True)
        for cp in first:
            cp.start()
        passed = []
        for j, chip in enumerate(chips):
            for cp in copies(1 + j, (*chip, c), me):
                cp.wait_recv()
            onward = copies(4 + j, (*chip, c), sibling)
            for cp in onward:
                cp.start()
            passed += onward
        for cp in copies(0, sibling, me):
            cp.wait_recv()
        for j, chip in enumerate(chips):
            for cp in copies(4 + j, (*chip, 1 - c), me):
                cp.wait_recv()
        for cp in first + passed:
            cp.wait_send()
        for cp in mine:
            cp.wait()

    any_spec = pl.BlockSpec(memory_space=pl.ANY)
    return pl.pallas_call(
        body, name=name, out_shape=[jax.ShapeDtypeStruct((N_DEV,) + b.shape, b.dtype) for b in blocks],
        in_specs=[any_spec] * n, out_specs=[any_spec] * n,
        scratch_shapes=[pltpu.SemaphoreType.DMA((7, n)), pltpu.SemaphoreType.DMA((7, n)), pltpu.SemaphoreType.DMA((n,))],
    )(*blocks)


def _exchange(parts, name):
    n = len(parts)

    def body(*refs):
        g_refs, land_refs = refs[:n], refs[n:2 * n]
        send_sems, recv_sems, local_sems = refs[2 * n:]
        x, y, c = _position()
        me = 4 * x + 2 * y + c
        mine = [pltpu.make_async_copy(g_refs[p].at[me], land_refs[p].at[me], local_sems.at[p]) for p in range(n)]
        for cp in mine:
            cp.start()
        sends, recvs = [], []
        for k in range(1, N_DEV):
            px = 1 - x if k & 4 else x
            py = 1 - y if k & 2 else y
            pc = 1 - c if k & 1 else c
            peer = 4 * px + 2 * py + pc
            for p in range(n):
                sends.append(pltpu.make_async_remote_copy(
                    src_ref=g_refs[p].at[peer], dst_ref=land_refs[p].at[me], send_sem=send_sems.at[k - 1, p],
                    recv_sem=recv_sems.at[k - 1, p], device_id=(px, py, pc), device_id_type=MESH))
                recvs.append(pltpu.make_async_remote_copy(
                    src_ref=g_refs[p].at[me], dst_ref=land_refs[p].at[peer], send_sem=send_sems.at[k - 1, p],
                    recv_sem=recv_sems.at[k - 1, p], device_id=(px, py, pc), device_id_type=MESH))
        for cp in sends:
            cp.start()
        for cp in recvs:
            cp.wait_recv()
        for cp in sends:
            cp.wait_send()
        for cp in mine:
            cp.wait()

    any_spec = pl.BlockSpec(memory_space=pl.ANY)
    return pl.pallas_call(
        body, name=name, out_shape=[jax.ShapeDtypeStruct(p.shape, p.dtype) for p in parts],
        in_specs=[any_spec] * n, out_specs=[any_spec] * n,
        scratch_shapes=[pltpu.SemaphoreType.DMA((7, n)), pltpu.SemaphoreType.DMA((7, n)), pltpu.SemaphoreType.DMA((n,))],
    )(*parts)


SHARDED = (("sb_w_qkv", 2), ("sb_w_o", 1), ("mla_w_down", 1), ("mla_w_uq", 2), ("mla_w_ukv", 2), ("mla_w_o", 1),
           ("fox_w_qkvf", 2), ("fox_w_o", 1), ("ffn_w_gate", 2), ("ffn_w_up", 2), ("ffn_w_down", 1),
           ("pool_w", 2), ("meta", 1), ("mla_q_norm", 1), ("mla_kv_norm", 1))
KEPT_F32 = ("meta", "mla_q_norm", "mla_kv_norm")
REPLICATED = ("norm_mix", "norm_ffn", "pool_scale", "fox_b_f", "final_norm")
WEIGHT_NAMES = ("meta", "norm_mix", "norm_ffn", "pool_w", "pool_scale", "sb_w_qkv", "sb_w_o", "mla_w_down",
                "mla_q_norm", "mla_kv_norm", "mla_w_uq", "mla_w_ukv", "mla_w_o", "fox_w_qkvf", "fox_b_f",
                "fox_w_o", "ffn_w_gate", "ffn_w_up", "ffn_w_down", "final_norm")
LANES = 1024


def _pack_rows(arrays, names):
    parts = []
    for n in names:
        flat = arrays[n].reshape(-1).astype(F32)
        rows = -(-flat.shape[0] // LANES)
        parts.append(jnp.pad(flat, (0, rows * LANES - flat.shape[0])).reshape(rows, LANES))
    rows = sum(p.shape[0] for p in parts)
    parts.append(jnp.zeros((-(-rows // 8) * 8 - rows, LANES), F32))
    return jnp.concatenate(parts, axis=0)


def _unpack_rows(buf, shapes, names):
    out, row = {}, 0
    for n in names:
        size = int(np.prod(shapes[n]))
        rows = -(-size // LANES)
        out[n] = buf[row:row + rows].reshape(-1)[:size].reshape(shapes[n])
        row += rows
    return out


def _whole_from_gathered(g, axis):
    g = jnp.moveaxis(g, 0, axis)
    shp = g.shape
    return g.reshape(shp[:axis] + (shp[axis] * shp[axis + 1],) + shp[axis + 2:])


def _parts_from_whole(whole, axis):
    shp = whole.shape
    g = whole.reshape(shp[:axis] + (N_DEV, shp[axis] // N_DEV) + shp[axis + 1:])
    return jnp.moveaxis(g, axis, 0)


def _kernel_weights(full):
    W = {}
    W["pool_w"] = full["pool_w"][0]
    W["sb_w_qkv"] = full["sb_w_qkv"][0]
    W["sb_w_o"] = full["sb_w_o"][0]
    W["mla_w_down"] = full["mla_w_down"][0]
    uq = full["mla_w_uq"][0].reshape(MLA_Q_RANK, N_HEADS, MLA_NOPE + MLA_ROPE)
    nope = uq[:, :, :MLA_NOPE].reshape(MLA_Q_RANK, N_HEADS * MLA_NOPE)
    rope = uq[:, :, MLA_NOPE:].reshape(MLA_Q_RANK, N_PAIRS, 2 * MLA_ROPE)
    rope = jnp.pad(rope, ((0, 0), (0, 0), (0, 128 - 2 * MLA_ROPE))).reshape(MLA_Q_RANK, N_PAIRS * 128)
    W["mla_w_uq"] = jnp.concatenate([nope, rope], axis=1)
    ukv = full["mla_w_ukv"][0].reshape(MLA_KV_RANK, N_HEADS, 2, HEAD_DIM)
    W["mla_w_ukv"] = jnp.transpose(ukv, (0, 2, 1, 3)).reshape(MLA_KV_RANK, 2 * N_HEADS * HEAD_DIM)
    W["mla_w_o"] = full["mla_w_o"][0]
    qkvf = full["fox_w_qkvf"][0]
    n_qkv = 3 * N_HEADS * HEAD_DIM
    W["fox_w_qkv"] = qkvf[:, :n_qkv]
    W["fox_w_f"] = jnp.pad(qkvf[:, n_qkv:], ((0, 0), (0, 128 - N_HEADS)))
    W["fox_w_qkvf"] = jnp.concatenate([W["fox_w_qkv"], W["fox_w_f"]], axis=1)
    W["fox_w_o"] = full["fox_w_o"][0]
    W["ffn_w_gu"] = jnp.concatenate([full["ffn_w_gate"], full["ffn_w_up"]], axis=2)
    W["ffn_w_down"] = full["ffn_w_down"]
    return W


def _reference_grads(G):
    out = {}
    out["pool_w"] = G["pool_w"][None]
    for n in ("sb_w_qkv", "sb_w_o", "mla_w_down", "mla_w_o", "fox_w_o"):
        out[n] = G[n][None]
    duq = G["mla_w_uq"]
    nope = duq[:, :N_HEADS * MLA_NOPE].reshape(MLA_Q_RANK, N_HEADS, MLA_NOPE)
    rope = duq[:, N_HEADS * MLA_NOPE:].reshape(MLA_Q_RANK, N_PAIRS, 128)[:, :, :2 * MLA_ROPE]
    rope = rope.reshape(MLA_Q_RANK, N_HEADS, MLA_ROPE)
    out["mla_w_uq"] = jnp.concatenate([nope, rope], axis=2).reshape(1, MLA_Q_RANK, -1)
    dukv = G["mla_w_ukv"].reshape(MLA_KV_RANK, 2, N_HEADS, HEAD_DIM)
    out["mla_w_ukv"] = jnp.transpose(dukv, (0, 2, 1, 3)).reshape(1, MLA_KV_RANK, -1)
    out["fox_w_qkvf"] = G["fox_w_qkvf"][None, :, :3 * N_HEADS * HEAD_DIM + N_HEADS]
    out["ffn_w_gate"] = G["ffn_w_gu"][:, :, :D_FF]
    out["ffn_w_up"] = G["ffn_w_gu"][:, :, D_FF:]
    out["ffn_w_down"] = G["ffn_w_down"]
    out["mla_q_norm"] = G["mla_q_norm"]
    out["mla_kv_norm"] = G["mla_kv_norm"]
    return out


def _pairs_col(f16):
    M = f16.shape[0]
    return jnp.transpose(f16.reshape(M, N_PAIRS, 2), (1, 0, 2))


def _pairs_row(f16):
    M = f16.shape[0]
    return jnp.transpose(f16.reshape(M, N_PAIRS, 2), (1, 2, 0))


def _local_step(x, target, W, P):
    S = x.shape[0]
    M = S + ROW0
    G = {}
    gain = lambda name, i: P[name][i][None, :]
    h0 = jnp.concatenate([jnp.zeros((PAD, D_MODEL), F32), P["meta"], x], axis=0)

    def ffn_fwd(h1, i):
        b = _norm_fwd(h1, gain("norm_ffn", i), BF16, f"ffn{i}_norm")
        g, u, act = _ffn_up(b, W["ffn_w_gu"][i], f"ffn{i}_up")
        h2 = _mm_nn(act, W["ffn_w_down"][i], F32, f"ffn{i}_down", res=h1)
        return h2, (h1, b, g, u, act)

    def ffn_bwd(dh2, saved, i):
        h1, b, g, u, act = saved
        dgu = _ffn_dact(dh2, W["ffn_w_down"][i], g, u, f"ffn{i}_dact")
        G.setdefault("ffn_w_down", {})[i] = _mm_tn(act, dh2, f"ffn{i}_dwd")
        db = _mm_nt(dgu, W["ffn_w_gu"][i], F32, f"ffn{i}_db")
        G.setdefault("ffn_w_gu", {})[i] = _mm_tn(b, dgu, f"ffn{i}_dwgu")
        dh1, dgain = _norm_bwd(h1, gain("norm_ffn", i), db, dh2, f"ffn{i}_dnorm")
        G.setdefault("norm_ffn", {})[i] = dgain
        return dh1

    a0 = _norm_fwd(h0, gain("norm_mix", 0), F32, "mix0_norm")
    h1_0, pooled = _pool_fwd(h0, a0, W["pool_w"], P["pool_scale"], "pool_fwd")
    h_1, ffn0 = ffn_fwd(h1_0, 0)

    sb_scale = HEAD_DIM ** -0.5
    a1 = _norm_fwd(h_1, gain("norm_mix", 1), BF16, "mix1_norm")
    sb_qkv = _mm_nn(a1, W["sb_w_qkv"], BF16, "sb_qkv")
    sb_o, sb_tot = _sb_fwd(sb_qkv, sb_scale, "sb_fwd")
    h1_1 = _mm_nn(sb_o, W["sb_w_o"], F32, "sb_out", res=h_1)
    h_2, ffn1 = ffn_fwd(h1_1, 1)

    mla_scale = (MLA_NOPE + MLA_ROPE) ** -0.5
    cos_t, sin_t = _rope_tables(M)
    a2 = _norm_fwd(h_2, gain("norm_mix", 2), BF16, "mix2_norm")
    down = _mm_nn(a2, W["mla_w_down"], F32, "mla_down")
    dq_raw = down[:, :MLA_Q_RANK]
    dkv_raw = down[:, MLA_Q_RANK:MLA_Q_RANK + MLA_KV_RANK]
    kr_raw = down[:, MLA_Q_RANK + MLA_KV_RANK:]
    c_q = _norm_fwd(dq_raw, P["mla_q_norm"], BF16, "mla_qnorm")
    c_kv = _norm_fwd(dkv_raw, P["mla_kv_norm"], BF16, "mla_kvnorm")
    q_lin = _mm_nn(c_q, W["mla_w_uq"], F32, "mla_uq")
    q_all = _rope(q_lin, cos_t, sin_t, BF16, "mla_qrope", lead=D_MODEL)
    kv_all = _mm_nn(c_kv, W["mla_w_ukv"], BF16, "mla_ukv")
    kr_in = jnp.concatenate([kr_raw, kr_raw, jnp.zeros((M, 64), F32)], axis=1)
    kr = _rope(kr_in, cos_t, sin_t, BF16, "mla_krope")
    q_rope = q_all[:, D_MODEL:]
    mla_o, mla_lse = _mla_fwd(q_all, kv_all, q_rope, kr, mla_scale, "mla_fwd")
    h1_2 = _mm_nn(mla_o, W["mla_w_o"], F32, "mla_out", res=h_2)
    h_3, ffn2 = ffn_fwd(h1_2, 2)

    fox_scale = HEAD_DIM ** -0.5
    a3 = _norm_fwd(h_3, gain("norm_mix", 3), BF16, "mix3_norm")
    fox_qkv = _mm_nn(a3, W["fox_w_qkv"], BF16, "fox_qkv")
    f_logit = _mm_nn(a3, W["fox_w_f"], F32, "fox_f")
    b_f = jnp.pad(P["fox_b_f"], ((0, 0), (0, 128 - N_HEADS)))
    Fc = _forget_cumsum(f_logit, b_f, "fox_cumsum")
    f_rows, f_cols = _pairs_row(Fc[:, :N_HEADS]), _pairs_col(Fc[:, :N_HEADS])
    fox_qkv_t = fox_qkv.T
    fox_o_t, fox_lse, fox_ox_t = _fox_fwd(fox_qkv, fox_qkv_t, f_rows, f_cols, fox_scale, "fox_fwd")
    fox_o = fox_o_t.T
    h1_3 = _mm_nn(fox_o, W["fox_w_o"], F32, "fox_out", res=h_3)
    h_4, ffn3 = ffn_fwd(h1_3, 3)

    sq, dh, dgain = _loss_head(h_4, P["final_norm"][None, :], target, "loss_head")
    G["final_norm"] = dgain[0]

    dh = ffn_bwd(dh, ffn3, 3)
    do = _mm_nt(dh, W["fox_w_o"], BF16, "fox_do")
    G["fox_w_o"] = _mm_tn(fox_o, dh, "fox_dwo")
    dq_t, dk, dv, colsum = _fox_bwd(fox_qkv, fox_qkv_t, fox_ox_t, do, do.T, fox_lse, f_rows, f_cols, fox_scale,
                                    "fox_bwd")
    dF = -jnp.transpose(colsum[:, :, ::HEAD_DIM], (1, 0, 2)).reshape(M, N_HEADS)
    dF = jnp.pad(dF, ((0, 0), (0, 128 - N_HEADS)))
    dlogit, db_f = _forget_cumsum_bwd(f_logit, b_f, dF, "fox_dcumsum")
    G["fox_b_f"] = db_f[:, :N_HEADS]
    dproj = jnp.concatenate([dq_t.T, dk.astype(BF16), dv.astype(BF16), dlogit.astype(BF16)], axis=1)
    da = _mm_nt(dproj, W["fox_w_qkvf"], F32, "fox_da")
    G["fox_w_qkvf"] = _mm_tn(a3, dproj, "fox_dwqkvf")
    dh, dgain = _norm_bwd(h_3, gain("norm_mix", 3), da, dh, "mix3_dnorm")
    G.setdefault("norm_mix", {})[3] = dgain

    dh = ffn_bwd(dh, ffn2, 2)
    do = _mm_nt(dh, W["mla_w_o"], BF16, "mla_do")
    G["mla_w_o"] = _mm_tn(mla_o, dh, "mla_dwo")
    dq, dk, dv, dqr, dkr = _mla_bwd(q_all, kv_all, q_rope, kr, mla_o, do, mla_lse, mla_scale, "mla_bwd")
    dqr = _rope(dqr, cos_t, sin_t, BF16, "mla_dqrope", inverse=True)
    dq_all = jnp.concatenate([dq, dqr], axis=1)
    dkr_sum = _rope(jnp.sum(dkr, axis=0), cos_t, sin_t, F32, "mla_dkrope", inverse=True)
    dkr_raw = dkr_sum[:, :MLA_ROPE] + dkr_sum[:, MLA_ROPE:2 * MLA_ROPE]
    dkv_all = jnp.concatenate([dk.astype(BF16), dv.astype(BF16)], axis=1)
    dc_q = _mm_nt(dq_all, W["mla_w_uq"], F32, "mla_dcq")
    G["mla_w_uq"] = _mm_tn(c_q, dq_all, "mla_dwuq")
    dc_kv = _mm_nt(dkv_all, W["mla_w_ukv"], F32, "mla_dckv")
    G["mla_w_ukv"] = _mm_tn(c_kv, dkv_all, "mla_dwukv")
    ddq_raw, G["mla_q_norm"] = _norm_bwd(dq_raw, P["mla_q_norm"], dc_q, None, "mla_dqnorm")
    ddkv_raw, G["mla_kv_norm"] = _norm_bwd(dkv_raw, P["mla_kv_norm"], dc_kv, None, "mla_dkvnorm")
    ddown = jnp.concatenate([ddq_raw, ddkv_raw, dkr_raw], axis=1).astype(BF16)
    da = _mm_nt(ddown, W["mla_w_down"], F32, "mla_da")
    G["mla_w_down"] = _mm_tn(a2, ddown, "mla_dwdown")
    dh, dgain = _norm_bwd(h_2, gain("norm_mix", 2), da, dh, "mix2_dnorm")
    G["norm_mix"][2] = dgain

    dh = ffn_bwd(dh, ffn1, 1)
    do = _mm_nt(dh, W["sb_w_o"], BF16, "sb_do")
    G["sb_w_o"] = _mm_tn(sb_o, dh, "sb_dwo")
    dq, dk, dv = _sb_bwd(sb_qkv, do, sb_tot, sb_scale, "sb_bwd")
    dqkv = jnp.concatenate([dq, dk.astype(BF16), dv.astype(BF16)], axis=1)
    da = _mm_nt(dqkv, W["sb_w_qkv"], F32, "sb_da")
    G["sb_w_qkv"] = _mm_tn(a1, dqkv, "sb_dwqkv")
    dh, dgain = _norm_bwd(h_1, gain("norm_mix", 1), da, dh, "mix1_dnorm")
    G["norm_mix"][1] = dgain

    dh = ffn_bwd(dh, ffn0, 0)
    dpc, G["pool_w"], G["pool_scale"] = _pool_bwd_mix(dh, pooled, W["pool_w"], P["pool_scale"], "pool_dmix")
    da = _pool_bwd_window(dpc, "pool_dwindow")
    dh, dgain = _norm_bwd(h0, gain("norm_mix", 0), da, dh, "mix0_dnorm")
    G["norm_mix"][0] = dgain

    G["norm_mix"] = jnp.concatenate([G["norm_mix"][i] for i in range(DEPTH)], axis=0)
    G["norm_ffn"] = jnp.concatenate([G["norm_ffn"][i] for i in range(DEPTH)], axis=0)
    G["ffn_w_down"] = jnp.stack([G["ffn_w_down"][i] for i in range(DEPTH)])
    G["ffn_w_gu"] = jnp.stack([G["ffn_w_gu"][i] for i in range(DEPTH)])
    return sq, dh, G


def kernel(x, meta, norm_mix, norm_ffn, pool_w, pool_scale, sb_w_qkv, sb_w_o, mla_w_down, mla_q_norm, mla_kv_norm, mla_w_uq, mla_w_ukv, mla_w_o, fox_w_qkvf, fox_b_f, fox_w_o, ffn_w_gate, ffn_w_up, ffn_w_down, final_norm, loss_target, m_meta, m_norm_mix, m_norm_ffn, m_pool_w, m_pool_scale, m_sb_w_qkv, m_sb_w_o, m_mla_w_down, m_mla_q_norm, m_mla_kv_norm, m_mla_w_uq, m_mla_w_ukv, m_mla_w_o, m_fox_w_qkvf, m_fox_b_f, m_fox_w_o, m_ffn_w_gate, m_ffn_w_up, m_ffn_w_down, m_final_norm, v_meta, v_norm_mix, v_norm_ffn, v_pool_w, v_pool_scale, v_sb_w_qkv, v_sb_w_o, v_mla_w_down, v_mla_q_norm, v_mla_kv_norm, v_mla_w_uq, v_mla_w_ukv, v_mla_w_o, v_fox_w_qkvf, v_fox_b_f, v_fox_w_o, v_ffn_w_gate, v_ffn_w_up, v_ffn_w_down, v_final_norm):
    w = dict(meta=meta, norm_mix=norm_mix, norm_ffn=norm_ffn, pool_w=pool_w, pool_scale=pool_scale,
             sb_w_qkv=sb_w_qkv, sb_w_o=sb_w_o, mla_w_down=mla_w_down, mla_q_norm=mla_q_norm,
             mla_kv_norm=mla_kv_norm, mla_w_uq=mla_w_uq, mla_w_ukv=mla_w_ukv, mla_w_o=mla_w_o,
             fox_w_qkvf=fox_w_qkvf, fox_b_f=fox_b_f, fox_w_o=fox_w_o, ffn_w_gate=ffn_w_gate, ffn_w_up=ffn_w_up,
             ffn_w_down=ffn_w_down, final_norm=final_norm)
    m = dict(meta=m_meta, norm_mix=m_norm_mix, norm_ffn=m_norm_ffn, pool_w=m_pool_w, pool_scale=m_pool_scale,
             sb_w_qkv=m_sb_w_qkv, sb_w_o=m_sb_w_o, mla_w_down=m_mla_w_down, mla_q_norm=m_mla_q_norm,
             mla_kv_norm=m_mla_kv_norm, mla_w_uq=m_mla_w_uq, mla_w_ukv=m_mla_w_ukv, mla_w_o=m_mla_w_o,
             fox_w_qkvf=m_fox_w_qkvf, fox_b_f=m_fox_b_f, fox_w_o=m_fox_w_o, ffn_w_gate=m_ffn_w_gate,
             ffn_w_up=m_ffn_w_up, ffn_w_down=m_ffn_w_down, final_norm=m_final_norm)
    v = dict(meta=v_meta, norm_mix=v_norm_mix, norm_ffn=v_norm_ffn, pool_w=v_pool_w, pool_scale=v_pool_scale,
             sb_w_qkv=v_sb_w_qkv, sb_w_o=v_sb_w_o, mla_w_down=v_mla_w_down, mla_q_norm=v_mla_q_norm,
             mla_kv_norm=v_mla_kv_norm, mla_w_uq=v_mla_w_uq, mla_w_ukv=v_mla_w_ukv, mla_w_o=v_mla_w_o,
             fox_w_qkvf=v_fox_w_qkvf, fox_b_f=v_fox_b_f, fox_w_o=v_fox_w_o, ffn_w_gate=v_ffn_w_gate,
             ffn_w_up=v_ffn_w_up, ffn_w_down=v_ffn_w_down, final_norm=v_final_norm)

    sh_names = tuple(n for n, _ in SHARDED)
    sh_axis = dict(SHARDED)
    shapes = {n: w[n].shape for n in WEIGHT_NAMES}
    wire = lambda n: F32 if n in KEPT_F32 else BF16

    gathered = _all_gather([w[n].astype(wire(n)) for n in sh_names], "gather_weights")
    full = {n: _whole_from_gathered(g, sh_axis[n]) for n, g in zip(sh_names, gathered)}
    W = _kernel_weights(full)
    P = dict(meta=full["meta"], mla_q_norm=full["mla_q_norm"], mla_kv_norm=full["mla_kv_norm"],
             norm_mix=norm_mix, norm_ffn=norm_ffn, pool_scale=pool_scale, fox_b_f=fox_b_f, final_norm=final_norm)

    sq, dh0, G = _local_step(x[0], loss_target[0], W, P)
    loss = lax.psum(0.5 * jnp.sum(sq) / D_MODEL, ("x", "y", "c"))
    grad_x = dh0[ROW0:][None]

    gw = _reference_grads(G)
    gw["meta"] = dh0[PAD:ROW0]
    parts = [_parts_from_whole(gw[n], sh_axis[n]).astype(wire(n)) for n in sh_names]
    landed = _exchange(parts, "exchange_grads")
    results = {}
    for n, got in zip(sh_names, landed):
        rc = (int(np.prod(shapes[n][:-1])), shapes[n][-1])
        outs = _adamw(w[n].reshape(rc), got.reshape((N_DEV,) + rc), m[n].reshape(rc), v[n].reshape(rc), f"adamw_{n}")
        results[n] = [o.reshape(shapes[n]) for o in outs]

    rep_g = dict(norm_mix=G["norm_mix"], norm_ffn=G["norm_ffn"], pool_scale=G["pool_scale"], fox_b_f=G["fox_b_f"],
                 final_norm=G["final_norm"])
    (rep_all,) = _all_gather([_pack_rows(rep_g, REPLICATED)], "gather_replicated_grads")
    rep_out = _adamw(_pack_rows(w, REPLICATED), rep_all, _pack_rows(m, REPLICATED), _pack_rows(v, REPLICATED),
                     "adamw_replicated")
    rep = [_unpack_rows(o, shapes, REPLICATED) for o in rep_out]
    for n in REPLICATED:
        results[n] = [r[n] for r in rep]

    outs = [results[n][k] for k in range(4) for n in WEIGHT_NAMES]
    return (loss, grad_x, *outs)
```

```python
import numpy as np
import jax
import jax.numpy as jnp
from jax import lax
from jax.experimental import pallas as pl
from jax.experimental.pallas import tpu as pltpu

F32 = jnp.float32
BF16 = jnp.bfloat16

N_DEV = 8
D_MODEL = 1024
N_META = 16
PAD = 240
ROW0 = PAD + N_META
EPS = 1e-6
POOL_WINDOWS = (2, 4, 8, 16)
POOL_GROUP = 256
HALO = 128
N_HEADS = 16
HEAD_DIM = 64
N_PAIRS = N_HEADS // 2
MLA_Q_RANK = 384
MLA_KV_RANK = 256
MLA_NOPE = 64
MLA_ROPE = 32
ROPE_THETA = 10000.0
D_FF = 2816
DEPTH = 4
ATTN_TILE = 256
NEG = -1e30
EXP_ZERO = -110.0
VMEM_LIMIT = 56 * 2**20
ADAM_TILE_ELEMS = 192 * 1024

ADAM_LR = 0.001
ADAM_B1 = 0.9
ADAM_B2 = 0.999
ADAM_EPS = 1e-08
ADAM_WD = 0.01
ADAM_STEP = 10

MESH = pl.DeviceIdType.MESH


def _params(sem=None):
    return pltpu.CompilerParams(dimension_semantics=sem, vmem_limit_bytes=VMEM_LIMIT)


def _pick(n, cands):
    for c in cands:
        if n % c == 0:
            return c
    return n


def _col_tile(n, cap=1536):
    best = None
    for t in range(128, min(n, cap) + 1, 128):
        if n % t == 0:
            best = t
    return best if best is not None else n


def _dot(a, b):
    return jnp.dot(a, b, preferred_element_type=F32)


def _dot_nt(a, b):
    return lax.dot_general(a, b, (((1,), (1,)), ((), ())), preferred_element_type=F32)


def _dot_tn(a, b):
    return lax.dot_general(a, b, (((0,), (0,)), ((), ())), preferred_element_type=F32)


def _mm_nn(a, b, out_dtype, name, res=None):
    M, K = a.shape
    N = b.shape[1]
    tm = _pick(M, (768, 512, 256, 128))
    tn = _col_tile(N)

    def body(*refs):
        if res is None:
            a_ref, b_ref, o_ref = refs
        else:
            a_ref, b_ref, r_ref, o_ref = refs
        acc = _dot(a_ref[...].astype(BF16), b_ref[...])
        if res is not None:
            acc = acc + r_ref[...]
        o_ref[...] = acc.astype(o_ref.dtype)

    in_specs = [pl.BlockSpec((tm, K), lambda n, m: (m, 0)), pl.BlockSpec((K, tn), lambda n, m: (0, n))]
    args = [a, b]
    if res is not None:
        in_specs.append(pl.BlockSpec((tm, tn), lambda n, m: (m, n)))
        args.append(res)
    return pl.pallas_call(
        body, name=name, grid=(N // tn, M // tm), in_specs=in_specs,
        out_specs=pl.BlockSpec((tm, tn), lambda n, m: (m, n)),
        out_shape=jax.ShapeDtypeStruct((M, N), out_dtype),
        compiler_params=_params(("parallel", "parallel")))(*args)


def _mm_nt(a, w, out_dtype, name):
    M, N = a.shape
    K = w.shape[0]
    tm = _pick(M, (768, 512, 256, 128)) if N <= 3200 else _pick(M, (256, 128))
    tk = _col_tile(K, 1024)

    def body(a_ref, w_ref, o_ref):
        o_ref[...] = _dot_nt(a_ref[...].astype(BF16), w_ref[...]).astype(o_ref.dtype)

    return pl.pallas_call(
        body, name=name, grid=(K // tk, M // tm),
        in_specs=[pl.BlockSpec((tm, N), lambda k, m: (m, 0)), pl.BlockSpec((tk, N), lambda k, m: (k, 0))],
        out_specs=pl.BlockSpec((tm, tk), lambda k, m: (m, k)),
        out_shape=jax.ShapeDtypeStruct((M, K), out_dtype),
        compiler_params=_params(("parallel", "parallel")))(a, w)


def _mm_tn(a, b, name):
    M, K = a.shape
    N = b.shape[1]
    tm = _pick(M, (768, 512, 256, 128))
    tk = _col_tile(K, 1408)
    tn = _col_tile(N, 1408)

    def body(a_ref, b_ref, o_ref):
        @pl.when(pl.program_id(2) == 0)
        def _():
            o_ref[...] = jnp.zeros_like(o_ref)
        o_ref[...] += _dot_tn(a_ref[...].astype(BF16), b_ref[...].astype(BF16))

    return pl.pallas_call(
        body, name=name, grid=(K // tk, N // tn, M // tm),
        in_specs=[pl.BlockSpec((tm, tk), lambda k, n, m: (m, k)), pl.BlockSpec((tm, tn), lambda k, n, m: (m, n))],
        out_specs=pl.BlockSpec((tk, tn), lambda k, n, m: (k, n)),
        out_shape=jax.ShapeDtypeStruct((K, N), F32),
        compiler_params=_params(("parallel", "parallel", "arbitrary")))(a, b)


def _norm_fwd(h, gain, out_dtype, name):
    M, C = h.shape
    tm = _pick(M, (768, 512, 256, 128))

    def body(h_ref, g_ref, a_ref):
        x = h_ref[...]
        r = lax.rsqrt(jnp.mean(x * x, axis=-1, keepdims=True) + EPS)
        a_ref[...] = ((x * r) * g_ref[...]).astype(a_ref.dtype)

    return pl.pallas_call(
        body, name=name, grid=(M // tm,),
        in_specs=[pl.BlockSpec((tm, C), lambda m: (m, 0)), pl.BlockSpec((1, C), lambda m: (0, 0))],
        out_specs=pl.BlockSpec((tm, C), lambda m: (m, 0)),
        out_shape=jax.ShapeDtypeStruct((M, C), out_dtype),
        compiler_params=_params(("parallel",)))(h, gain)


def _norm_bwd(h, gain, da, dres, name):
    M, C = h.shape
    tm = _pick(M, (768, 512, 256, 128))

    def body(*refs):
        if dres is None:
            h_ref, g_ref, da_ref, dh_ref, dg_ref = refs
        else:
            h_ref, g_ref, da_ref, dr_ref, dh_ref, dg_ref = refs
        x = h_ref[...]
        r = lax.rsqrt(jnp.mean(x * x, axis=-1, keepdims=True) + EPS)
        y = x * r
        dav = da_ref[...].astype(F32)
        dy = dav * g_ref[...]
        dh = r * (dy - y * jnp.mean(dy * y, axis=-1, keepdims=True))
        if dres is not None:
            dh = dh + dr_ref[...]
        dh_ref[...] = dh

        @pl.when(pl.program_id(0) == 0)
        def _():
            dg_ref[...] = jnp.zeros_like(dg_ref)
        dg_ref[...] += jnp.sum(dav * y, axis=0, keepdims=True)

    row = pl.BlockSpec((tm, C), lambda m: (m, 0))
    vec = pl.BlockSpec((1, C), lambda m: (0, 0))
    in_specs = [row, vec, row] + ([row] if dres is not None else [])
    args = [h, gain, da] + ([dres] if dres is not None else [])
    return pl.pallas_call(
        body, name=name, grid=(M // tm,), in_specs=in_specs, out_specs=[row, vec],
        out_shape=[jax.ShapeDtypeStruct((M, C), F32), jax.ShapeDtypeStruct((1, C), F32)],
        compiler_params=_params(("arbitrary",)))(*args)


def _ffn_up(b, w_gu, name):
    M, K = b.shape
    F = w_gu.shape[1] // 2
    tm = _pick(M, (768, 512, 256, 128))
    tn = _col_tile(F, 1408)
    nb = F // tn

    def body(b_ref, wg_ref, wu_ref, g_ref, u_ref, act_ref):
        x = b_ref[...]
        g = _dot(x, wg_ref[...])
        u = _dot(x, wu_ref[...])
        g_ref[...] = g
        u_ref[...] = u
        act_ref[...] = ((g * jax.nn.sigmoid(g)) * u).astype(act_ref.dtype)

    blk = pl.BlockSpec((tm, tn), lambda n, m: (m, n))
    return pl.pallas_call(
        body, name=name, grid=(nb, M // tm),
        in_specs=[pl.BlockSpec((tm, K), lambda n, m: (m, 0)),
                  pl.BlockSpec((K, tn), lambda n, m: (0, n)),
                  pl.BlockSpec((K, tn), lambda n, m: (0, n + nb))],
        out_specs=[blk, blk, blk],
        out_shape=[jax.ShapeDtypeStruct((M, F), F32), jax.ShapeDtypeStruct((M, F), F32),
                   jax.ShapeDtypeStruct((M, F), BF16)],
        compiler_params=_params(("parallel", "parallel")))(b, w_gu, w_gu)


def _ffn_dact(dy, w_d, g, u, name):
    M, K = dy.shape
    F = w_d.shape[0]
    tm = _pick(M, (768, 512, 256, 128))
    tn = _col_tile(F, 1408)
    nb = F // tn

    def body(dy_ref, wd_ref, g_ref, u_ref, dg_ref, du_ref):
        dact = _dot_nt(dy_ref[...].astype(BF16), wd_ref[...])
        gv = g_ref[...]
        s = jax.nn.sigmoid(gv)
        silu = gv * s
        dg_ref[...] = (dact * u_ref[...] * (s * (1.0 + gv * (1.0 - s)))).astype(dg_ref.dtype)
        du_ref[...] = (dact * silu).astype(du_ref.dtype)

    blk = pl.BlockSpec((tm, tn), lambda n, m: (m, n))
    dg, du = pl.pallas_call(
        body, name=name, grid=(nb, M // tm),
        in_specs=[pl.BlockSpec((tm, K), lambda n, m: (m, 0)), pl.BlockSpec((tn, K), lambda n, m: (n, 0)), blk, blk],
        out_specs=[blk, blk],
        out_shape=[jax.ShapeDtypeStruct((M, F), BF16), jax.ShapeDtypeStruct((M, F), BF16)],
        compiler_params=_params(("parallel", "parallel")))(dy, w_d, g, u)
    return jnp.concatenate([dg, du], axis=1)


def _loss_head(h, gain, target, name):
    M, C = h.shape
    tm = ROW0
    assert M % tm == 0 and target.shape[0] == M - ROW0

    def body(h_ref, g_ref, t_ref, sq_ref, dh_ref, dg_ref):
        i = pl.program_id(0)

        @pl.when(i == 0)
        def _():
            sq_ref[...] = jnp.zeros_like(sq_ref)
            dg_ref[...] = jnp.zeros_like(dg_ref)
            dh_ref[...] = jnp.zeros_like(dh_ref)

        @pl.when(i > 0)
        def _():
            x = h_ref[...]
            r = lax.rsqrt(jnp.mean(x * x, axis=-1, keepdims=True) + EPS)
            y = x * r
            err = y * g_ref[...] - t_ref[...]
            sq_ref[...] += jnp.sum(err * err, axis=0, keepdims=True)
            da = err * (1.0 / C)
            dy = da * g_ref[...]
            dh_ref[...] = r * (dy - y * jnp.mean(dy * y, axis=-1, keepdims=True))
            dg_ref[...] += jnp.sum(da * y, axis=0, keepdims=True)

    row = pl.BlockSpec((tm, C), lambda m: (m, 0))
    vec = pl.BlockSpec((1, C), lambda m: (0, 0))
    return pl.pallas_call(
        body, name=name, grid=(M // tm,),
        in_specs=[row, vec, pl.BlockSpec((tm, C), lambda m: (jnp.maximum(m - 1, 0), 0))],
        out_specs=[vec, row, vec],
        out_shape=[jax.ShapeDtypeStruct((1, C), F32), jax.ShapeDtypeStruct((M, C), F32),
                   jax.ShapeDtypeStruct((1, C), F32)],
        compiler_params=_params(("arbitrary",)))(h, gain, target)


def _pool_pos(row0, tm):
    return row0 + lax.broadcasted_iota(jnp.int32, (tm, 1), 0) - PAD


def _pool_fwd(h, a, w, scale, name):
    M, C = a.shape
    tm = 256
    hb = tm // HALO

    def body(h_ref, a_ref, halo_ref, w_ref, s_ref, o_ref, p_ref):
        i = pl.program_id(0)
        row0 = i * tm
        ext = jnp.concatenate([halo_ref[...], a_ref[...]], axis=0)
        src = row0 - HALO + lax.broadcasted_iota(jnp.int32, (tm + HALO, 1), 0)
        ext = jnp.where(src >= PAD, ext, 0.0)
        r = lax.broadcasted_iota(jnp.int32, (tm, tm + HALO), 0)
        c = lax.broadcasted_iota(jnp.int32, (tm, tm + HALO), 1)
        pos = _pool_pos(row0, tm)
        for g, win in enumerate(POOL_WINDOWS):
            band = ((c <= r + HALO) & (c > r + HALO - win)).astype(F32)
            cols = slice(g * POOL_GROUP, (g + 1) * POOL_GROUP)
            xg = ext[:, cols]
            tot = jnp.dot(band, xg, precision=lax.Precision.HIGHEST, preferred_element_type=F32)
            cnt = jnp.clip(pos + 1, 1, win).astype(F32)
            pooled = (tot / cnt - xg[HALO:]).astype(BF16)
            p_ref[:, cols] = pooled
            mixed = _dot(pooled, w_ref[g])
            o_ref[:, cols] = h_ref[:, cols] + mixed * s_ref[:, cols]

    row = pl.BlockSpec((tm, C), lambda m: (m, 0))
    return pl.pallas_call(
        body, name=name, grid=(M // tm,),
        in_specs=[row, row, pl.BlockSpec((HALO, C), lambda m: (jnp.maximum(m * hb - 1, 0), 0)),
                  pl.BlockSpec((4, POOL_GROUP, POOL_GROUP), lambda m: (0, 0, 0)),
                  pl.BlockSpec((1, C), lambda m: (0, 0))],
        out_specs=[row, row],
        out_shape=[jax.ShapeDtypeStruct((M, C), F32), jax.ShapeDtypeStruct((M, C), BF16)],
        compiler_params=_params(("parallel",)))(h, a, a, w, scale)


def _pool_bwd_mix(dout, pooled, w, scale, name):
    M, C = dout.shape
    tm = 256

    def body(do_ref, p_ref, w_ref, s_ref, dpc_ref, dw_ref, ds_ref):
        i = pl.program_id(0)

        @pl.when(i == 0)
        def _():
            dw_ref[...] = jnp.zeros_like(dw_ref)
            ds_ref[...] = jnp.zeros_like(ds_ref)

        pos = _pool_pos(i * tm, tm)
        for g, win in enumerate(POOL_WINDOWS):
            cols = slice(g * POOL_GROUP, (g + 1) * POOL_GROUP)
            do = do_ref[:, cols]
            pooled = p_ref[:, cols]
            mixed = _dot(pooled, w_ref[g])
            ds_ref[:, cols] += jnp.sum(do * mixed, axis=0, keepdims=True)
            dmix = (do * s_ref[:, cols]).astype(BF16)
            dw_ref[g] += _dot_tn(pooled, dmix)
            dp = _dot_nt(dmix, w_ref[g])
            cnt = jnp.clip(pos + 1, 1, win).astype(F32)
            dpc_ref[:, cols] = dp / cnt

    row = pl.BlockSpec((tm, C), lambda m: (m, 0))
    wspec = pl.BlockSpec((4, POOL_GROUP, POOL_GROUP), lambda m: (0, 0, 0))
    vec = pl.BlockSpec((1, C), lambda m: (0, 0))
    return pl.pallas_call(
        body, name=name, grid=(M // tm,),
        in_specs=[row, row, wspec, vec], out_specs=[row, wspec, vec],
        out_shape=[jax.ShapeDtypeStruct((M, C), F32), jax.ShapeDtypeStruct((4, POOL_GROUP, POOL_GROUP), F32),
                   jax.ShapeDtypeStruct((1, C), F32)],
        compiler_params=_params(("arbitrary",)))(dout, pooled, w, scale)


def _pool_bwd_window(dpc, name):
    M, C = dpc.shape
    tm = 256
    hb = tm // HALO
    last = M // HALO - 1

    def body(d_ref, halo_ref, da_ref):
        i = pl.program_id(0)
        row0 = i * tm
        ext = jnp.concatenate([d_ref[...], halo_ref[...]], axis=0)
        src = row0 + lax.broadcasted_iota(jnp.int32, (tm + HALO, 1), 0)
        ext = jnp.where(src < M, ext, 0.0)
        r = lax.broadcasted_iota(jnp.int32, (tm, tm + HALO), 0)
        c = lax.broadcasted_iota(jnp.int32, (tm, tm + HALO), 1)
        pos = _pool_pos(row0, tm)
        for g, win in enumerate(POOL_WINDOWS):
            band = ((c >= r) & (c < r + win)).astype(F32)
            cols = slice(g * POOL_GROUP, (g + 1) * POOL_GROUP)
            xg = ext[:, cols]
            tot = jnp.dot(band, xg, precision=lax.Precision.HIGHEST, preferred_element_type=F32)
            cnt = jnp.clip(pos + 1, 1, win).astype(F32)
            da_ref[:, cols] = jnp.where(pos >= 0, tot - xg[:tm] * cnt, 0.0)

    row = pl.BlockSpec((tm, C), lambda m: (m, 0))
    return pl.pallas_call(
        body, name=name, grid=(M // tm,),
        in_specs=[row, pl.BlockSpec((HALO, C), lambda m: (jnp.minimum((m + 1) * hb, last), 0))],
        out_specs=row, out_shape=jax.ShapeDtypeStruct((M, C), F32),
        compiler_params=_params(("parallel",)))(dpc, dpc)


def _head_masks():
    lane = lax.broadcasted_iota(jnp.int32, (1, 128), 1)
    return lane < HEAD_DIM, lane


def _split_heads(x, first):
    z = jnp.zeros_like(x)
    return jnp.where(first, x, z), jnp.where(first, z, x)


def _split_rope(x, lane):
    z = jnp.zeros_like(x)
    return jnp.where(lane < MLA_ROPE, x, z), jnp.where((lane >= MLA_ROPE) & (lane < 2 * MLA_ROPE), x, z)


def _mla_fwd(q_all, kv_all, qr, kr, scale, name):
    M = q_all.shape[0]
    t = ATTN_TILE

    def body(q_ref, k_ref, v_ref, qr_ref, kr_ref, o_ref, lse_ref, m_s, l_s, acc_s):
        i = pl.program_id(1)
        first, lane = _head_masks()
        qs = _split_heads(q_ref[...], first)
        qrs = _split_rope(qr_ref[...], lane)
        qcat = tuple(jnp.concatenate([qs[hh], qrs[hh]], axis=1) for hh in range(2))
        m_s[...] = jnp.full_like(m_s, NEG)
        l_s[...] = jnp.zeros_like(l_s)
        acc_s[...] = jnp.zeros_like(acc_s)
        qpos = i * t + lax.broadcasted_iota(jnp.int32, (t, t), 0)
        kidx = lax.broadcasted_iota(jnp.int32, (t, t), 1)

        def step(kb, carry):
            k0 = pl.multiple_of(kb * t, t)
            kcat = jnp.concatenate([k_ref[pl.ds(k0, t), :], kr_ref[pl.ds(k0, t), :]], axis=1)
            vs = _split_heads(v_ref[pl.ds(k0, t), :], first)
            kpos = k0 + kidx
            valid = (kpos <= qpos) & (kpos >= PAD)
            pv = None
            alphas = []
            for hh in range(2):
                s = _dot_nt(qcat[hh], kcat) * scale
                s = jnp.where(valid, s, NEG)
                m_old = m_s[hh]
                m_new = jnp.maximum(m_old, jnp.max(s, axis=1, keepdims=True))
                p = jnp.exp(s - m_new)
                alpha = jnp.exp(m_old - m_new)
                l_s[hh] = alpha * l_s[hh] + jnp.sum(p, axis=1, keepdims=True)
                m_s[hh] = m_new
                d = _dot(p.astype(BF16), vs[hh])
                pv = d if pv is None else pv + d
                alphas.append(alpha)
            acc_s[...] = acc_s[...] * jnp.where(first, alphas[0], alphas[1]) + pv
            return carry

        lax.fori_loop(0, i + 1, step, 0)
        o_ref[...] = (acc_s[...] * jnp.where(first, 1.0 / l_s[0], 1.0 / l_s[1])).astype(o_ref.dtype)
        lse_ref[:, 0:1] = m_s[0] + jnp.log(l_s[0])
        lse_ref[:, 1:2] = m_s[1] + jnp.log(l_s[1])

    blk = pl.BlockSpec((t, 128), lambda j, i: (i, j))
    return pl.pallas_call(
        body, name=name, grid=(N_PAIRS, M // t),
        in_specs=[blk, pl.BlockSpec((M, 128), lambda j, i: (0, j)), pl.BlockSpec((M, 128), lambda j, i: (0, N_PAIRS + j)),
                  blk, pl.BlockSpec((M, 128), lambda j, i: (0, 0))],
        out_specs=[blk, pl.BlockSpec((None, t, 2), lambda j, i: (j, i, 0))],
        out_shape=[jax.ShapeDtypeStruct((M, N_PAIRS * 128), BF16), jax.ShapeDtypeStruct((N_PAIRS, M, 2), F32)],
        scratch_shapes=[pltpu.VMEM((2, t, 1), F32), pltpu.VMEM((2, t, 1), F32), pltpu.VMEM((t, 128), F32)],
        compiler_params=_params(("parallel", "arbitrary")))(q_all, kv_all, kv_all, qr, kr)


def _mla_bwd(q_all, kv_all, qr, kr, o, do, lse, scale, name):
    M = q_all.shape[0]
    t = ATTN_TILE

    def body(q_ref, k_ref, v_ref, qr_ref, kr_ref, o_ref, do_ref, lse_ref,
             dq_ref, dk_ref, dv_ref, dqr_ref, dkr_ref, dq_s):
        i = pl.program_id(1)
        first, lane = _head_masks()

        @pl.when(i == 0)
        def _():
            dk_ref[...] = jnp.zeros_like(dk_ref)
            dv_ref[...] = jnp.zeros_like(dv_ref)
            dkr_ref[...] = jnp.zeros_like(dkr_ref)

        qs = _split_heads(q_ref[...], first)
        qrs = _split_rope(qr_ref[...], lane)
        qcat = tuple(jnp.concatenate([qs[hh], qrs[hh]], axis=1) for hh in range(2))
        dov = do_ref[...]
        dos = _split_heads(dov, first)
        prod = dov.astype(F32) * o_ref[...].astype(F32)
        deltas = (jnp.sum(jnp.where(first, prod, 0.0), axis=1, keepdims=True),
                  jnp.sum(jnp.where(first, 0.0, prod), axis=1, keepdims=True))
        dq_s[...] = jnp.zeros_like(dq_s)
        qpos = i * t + lax.broadcasted_iota(jnp.int32, (t, t), 0)
        kidx = lax.broadcasted_iota(jnp.int32, (t, t), 1)

        def step(kb, carry):
            k0 = pl.multiple_of(kb * t, t)
            rows = pl.ds(k0, t)
            k = k_ref[rows, :]
            v = v_ref[rows, :]
            kr = kr_ref[rows, :]
            kcat = jnp.concatenate([k, kr], axis=1)
            ks = _split_heads(k, first)
            krs = _split_rope(kr, lane)
            kpos = k0 + kidx
            valid = (kpos <= qpos) & (kpos >= PAD)
            dq = dk = dv = None
            for hh in range(2):
                s = _dot_nt(qcat[hh], kcat) * scale
                s = jnp.where(valid, s, NEG)
                p = jnp.exp(s - lse_ref[:, hh:hh + 1])
                ds = p * (_dot_nt(dos[hh], v) - deltas[hh])
                dsb = (ds * scale).astype(BF16)
                a = _dot(dsb, jnp.concatenate([ks[hh], krs[hh]], axis=1))
                b = _dot_tn(dsb, qcat[hh])
                c = _dot_tn(p.astype(BF16), dos[hh])
                dq = a if dq is None else dq + a
                dk = b if dk is None else dk + b
                dv = c if dv is None else dv + c
            dq_s[...] += dq
            dk_ref[rows, :] += dk[:, :128]
            dkr_ref[rows, :] += dk[:, 128:]
            dv_ref[rows, :] += dv
            return carry

        lax.fori_loop(0, i + 1, step, 0)
        dq_ref[...] = dq_s[:, :128].astype(dq_ref.dtype)
        dqr_ref[...] = dq_s[:, 128:].astype(dqr_ref.dtype)

    blk = pl.BlockSpec((t, 128), lambda j, i: (i, j))
    col = pl.BlockSpec((M, 128), lambda j, i: (0, j))
    wide = jax.ShapeDtypeStruct((M, N_PAIRS * 128), F32)
    return pl.pallas_call(
        body, name=name, grid=(N_PAIRS, M // t),
        in_specs=[blk, col, pl.BlockSpec((M, 128), lambda j, i: (0, N_PAIRS + j)), blk,
                  pl.BlockSpec((M, 128), lambda j, i: (0, 0)), blk, blk,
                  pl.BlockSpec((None, t, 2), lambda j, i: (j, i, 0))],
        out_specs=[blk, col, col, blk, pl.BlockSpec((None, M, 128), lambda j, i: (j, 0, 0))],
        out_shape=[jax.ShapeDtypeStruct((M, N_PAIRS * 128), BF16), wide, wide,
                   jax.ShapeDtypeStruct((M, N_PAIRS * 128), BF16), jax.ShapeDtypeStruct((N_PAIRS, M, 128), F32)],
        scratch_shapes=[pltpu.VMEM((t, 256), F32)],
        compiler_params=_params(("parallel", "arbitrary")))(q_all, kv_all, kv_all, qr, kr, o, do, lse)


def _tri(t, rel):
    j = lax.broadcasted_iota(jnp.int32, (t, t), 0)
    k = lax.broadcasted_iota(jnp.int32, (t, t), 1)
    m = {"gt": j > k, "le": j <= k, "lt": j < k}[rel]
    return m.astype(BF16)


def _lane_cumsum(x, tri):
    hi = x.astype(BF16)
    lo = (x - hi.astype(F32)).astype(BF16)
    return _dot(hi, tri) + _dot(lo, tri)


def _log_sigmoids(z):
    sp = jnp.log(1.0 + jnp.exp(-jnp.abs(z)))
    return jnp.minimum(z, 0.0) - sp, jnp.minimum(-z, 0.0) - sp


def _sb_fwd(qkv, scale, name):
    M = qkv.shape[0]
    t = ATTN_TILE
    ck, cv = N_PAIRS, 2 * N_PAIRS

    def body(q_ref, k_ref, v_ref, o_ref, tot_ref, c_s, acc_s):
        i = pl.program_id(1)
        first, _ = _head_masks()
        qs = _split_heads(q_ref[...], first)
        c_s[...] = jnp.zeros_like(c_s)
        acc_s[...] = jnp.zeros_like(acc_s)
        tri = _tri(t, "gt")
        qpos = i * t + lax.broadcasted_iota(jnp.int32, (t, t), 0)
        kidx = lax.broadcasted_iota(jnp.int32, (t, t), 1)

        def step(it):
            k0 = pl.multiple_of((i - it) * t, t)
            k = k_ref[pl.ds(k0, t), :]
            vs = _split_heads(v_ref[pl.ds(k0, t), :], first)
            kpos = k0 + kidx
            valid = (kpos < qpos) & (kpos >= PAD)
            pv = None
            for hh in range(2):
                z = _dot_nt(qs[hh], k) * scale
                lb, lkr = _log_sigmoids(z)
                lk = jnp.where(valid, lkr, 0.0)
                later = c_s[hh] + _lane_cumsum(lk, tri)
                a = jnp.where(valid, jnp.exp(lb + later), 0.0)
                c_s[hh] = c_s[hh] + jnp.sum(lk, axis=1, keepdims=True)
                d = _dot(a.astype(BF16), vs[hh])
                pv = d if pv is None else pv + d
            acc_s[...] += pv

        def cond(carry):
            it, go = carry
            return (it <= i) & go

        def walk(carry):
            it, _ = carry
            step(it)
            return it + 1, jnp.max(jnp.maximum(c_s[0], c_s[1])) > EXP_ZERO

        walked, _ = lax.while_loop(cond, walk, (jnp.int32(0), True))
        o_ref[...] = acc_s[...].astype(o_ref.dtype)
        tot_ref[:, 0:1] = c_s[0]
        tot_ref[:, 1:2] = c_s[1]
        tot_ref[:, 2:3] = jnp.full((t, 1), walked.astype(F32))

    whole = lambda c0: pl.BlockSpec((M, 128), lambda j, i: (0, c0 + j))
    return pl.pallas_call(
        body, name=name, grid=(N_PAIRS, M // t),
        in_specs=[pl.BlockSpec((t, 128), lambda j, i: (i, j)), whole(ck), whole(cv)],
        out_specs=[pl.BlockSpec((t, 128), lambda j, i: (i, j)), pl.BlockSpec((None, t, 3), lambda j, i: (j, i, 0))],
        out_shape=[jax.ShapeDtypeStruct((M, N_PAIRS * 128), BF16), jax.ShapeDtypeStruct((N_PAIRS, M, 3), F32)],
        scratch_shapes=[pltpu.VMEM((2, t, 1), F32), pltpu.VMEM((t, 128), F32)],
        compiler_params=_params(("parallel", "arbitrary")))(qkv, qkv, qkv)


def _sb_bwd(qkv, do, tot, scale, name):
    M = qkv.shape[0]
    t = ATTN_TILE
    ck, cv = N_PAIRS, 2 * N_PAIRS

    def body(q_ref, k_ref, v_ref, do_ref, tot_ref, dq_ref, dk_ref, dv_ref, pc_s, dc_s, dq_s):
        i = pl.program_id(1)
        first, _ = _head_masks()

        @pl.when(i == 0)
        def _():
            dk_ref[...] = jnp.zeros_like(dk_ref)
            dv_ref[...] = jnp.zeros_like(dv_ref)

        qs = _split_heads(q_ref[...], first)
        dos = _split_heads(do_ref[...], first)
        pc_s[...] = jnp.zeros_like(pc_s)
        dc_s[...] = jnp.zeros_like(dc_s)
        dq_s[...] = jnp.zeros_like(dq_s)
        tri_le = _tri(t, "le")
        tri_lt = _tri(t, "lt")
        qpos = i * t + lax.broadcasted_iota(jnp.int32, (t, t), 0)
        kidx = lax.broadcasted_iota(jnp.int32, (t, t), 1)

        def step(kb, carry):
            k0 = pl.multiple_of(kb * t, t)
            rows = pl.ds(k0, t)
            k = k_ref[rows, :]
            v = v_ref[rows, :]
            ks = _split_heads(k, first)
            kpos = k0 + kidx
            valid = (kpos < qpos) & (kpos >= PAD)
            dq = dk = dv = None
            for hh in range(2):
                z = _dot_nt(qs[hh], k) * scale
                lb, lkr = _log_sigmoids(z)
                lk = jnp.where(valid, lkr, 0.0)
                later = tot_ref[:, hh:hh + 1] - (pc_s[hh] + _lane_cumsum(lk, tri_le))
                a = jnp.where(valid, jnp.exp(lb + later), 0.0)
                dl = a * _dot_nt(dos[hh], v)
                early = dc_s[hh] + _lane_cumsum(dl, tri_lt)
                sg = jnp.exp(lb)
                dz = jnp.where(valid, dl * (1.0 - sg) - early * sg, 0.0) * scale
                pc_s[hh] = pc_s[hh] + jnp.sum(lk, axis=1, keepdims=True)
                dc_s[hh] = dc_s[hh] + jnp.sum(dl, axis=1, keepdims=True)
                dzb = dz.astype(BF16)
                x = _dot(dzb, ks[hh])
                y = _dot_tn(dzb, qs[hh])
                w = _dot_tn(a.astype(BF16), dos[hh])
                dq = x if dq is None else dq + x
                dk = y if dk is None else dk + y
                dv = w if dv is None else dv + w
            dq_s[...] += dq
            dk_ref[rows, :] += dk
            dv_ref[rows, :] += dv
            return carry

        walked = jnp.max(tot_ref[:, 2:3]).astype(jnp.int32)
        lax.fori_loop(i + 1 - walked, i + 1, step, 0)
        dq_ref[...] = dq_s[...].astype(dq_ref.dtype)

    whole = lambda c0: pl.BlockSpec((M, 128), lambda j, i: (0, c0 + j))
    blk = pl.BlockSpec((t, 128), lambda j, i: (i, j))
    col = pl.BlockSpec((M, 128), lambda j, i: (0, j))
    return pl.pallas_call(
        body, name=name, grid=(N_PAIRS, M // t),
        in_specs=[blk, whole(ck), whole(cv), blk, pl.BlockSpec((None, t, 3), lambda j, i: (j, i, 0))],
        out_specs=[blk, col, col],
        out_shape=[jax.ShapeDtypeStruct((M, N_PAIRS * 128), BF16), jax.ShapeDtypeStruct((M, N_PAIRS * 128), F32),
                   jax.ShapeDtypeStruct((M, N_PAIRS * 128), F32)],
        scratch_shapes=[pltpu.VMEM((2, t, 1), F32), pltpu.VMEM((2, t, 1), F32), pltpu.VMEM((t, 128), F32)],
        compiler_params=_params(("parallel", "arbitrary")))(qkv, qkv, qkv, do, tot)


def _rows_between(lo, hi):
    r = lax.broadcasted_iota(jnp.int32, (128, 1), 0)
    return (r >= lo) & (r < hi)


def _lanes_between(lo, hi):
    c = lax.broadcasted_iota(jnp.int32, (1, 128), 1)
    return (c >= lo) & (c < hi)


def _keep(x, mask):
    return jnp.where(mask, x, jnp.zeros_like(x))


def _valid_mask(i, kb, t):
    kpos = kb * t + lax.broadcasted_iota(jnp.int32, (t, t), 0)
    qpos = i * t + lax.broadcasted_iota(jnp.int32, (t, t), 1)
    return (kpos <= qpos) & (kpos >= PAD)


def _fox_fwd(qkv, qkv_t, f_rows, f_cols, scale, name):
    M = qkv.shape[0]
    t = ATTN_TILE
    ck, cv = N_PAIRS, 2 * N_PAIRS

    def body(qt_ref, k_ref, vt_ref, fq_ref, fk_ref, o_ref, lse_ref, ox_ref, m_s, l_s, acc_s, accx_s, kmax_s):
        i = pl.program_id(1)

        @pl.when(i == 0)
        def _():
            first = _lanes_between(0, 64)

            def block_max(kb, carry):
                kk = k_ref[pl.ds(pl.multiple_of(kb * t, t), t), :].astype(F32)
                kk = kk * kk
                a = jnp.max(jnp.sum(jnp.where(first, kk, 0.0), axis=1, keepdims=True))
                b = jnp.max(jnp.sum(jnp.where(first, 0.0, kk), axis=1, keepdims=True))
                return jnp.maximum(carry[0], a), jnp.maximum(carry[1], b)

            a, b = lax.fori_loop(0, M // t, block_max, (jnp.float32(0.0), jnp.float32(0.0)))
            kmax_s[0] = a
            kmax_s[1] = b

        qt = qt_ref[...]
        qts = (_keep(qt, _rows_between(0, 64)), _keep(qt, _rows_between(64, 128)))
        qf = qt.astype(F32)
        qf = qf * qf
        qbound = tuple(
            (1.001 * scale) * jnp.sqrt(jnp.sum(qf[HEAD_DIM * hh:HEAD_DIM * (hh + 1)], axis=0, keepdims=True) * kmax_s[hh])
            for hh in range(2))
        m_s[...] = jnp.full_like(m_s, NEG)
        l_s[...] = jnp.zeros_like(l_s)
        acc_s[...] = jnp.zeros_like(acc_s)
        accx_s[...] = jnp.zeros_like(accx_s)

        def step(kb, masked):
            k0 = pl.multiple_of(kb * t, t)
            rows = pl.ds(k0, t)
            k = k_ref[rows, :]
            if masked:
                valid = _valid_mask(i, kb, t)
            for hh in range(2):
                s = _dot(k, qts[hh]) * scale + (fq_ref[hh:hh + 1, :] - fk_ref[rows, hh:hh + 1])
                if masked:
                    s = jnp.where(valid, s, NEG)
                m_old = m_s[hh]
                m_new = jnp.maximum(m_old, jnp.max(s, axis=0, keepdims=True))
                p = jnp.exp(s - m_new)
                alpha = jnp.exp(m_old - m_new)
                l_s[hh] = alpha * l_s[hh] + jnp.sum(p, axis=0, keepdims=True)
                m_s[hh] = m_new
                pb = p.astype(BF16)
                hr = slice(HEAD_DIM * hh, HEAD_DIM * (hh + 1))
                vt = vt_ref[hr, rows]
                acc_s[hr, :] = acc_s[hr, :] * alpha + _dot(vt, pb)
                accx_s[hr, :] = accx_s[hr, :] * alpha + _dot(vt, (p - pb.astype(F32)).astype(BF16))

        def keep_going(kb):
            k0 = pl.multiple_of(kb * t, t)
            worst = None
            for hh in range(2):
                f0 = jnp.max(fk_ref[pl.ds(k0, 8), hh:hh + 1])
                w = jnp.max(qbound[hh] + (fq_ref[hh:hh + 1, :] - f0) - m_s[hh])
                worst = w if worst is None else jnp.maximum(worst, w)
            return worst > EXP_ZERO

        def cond(carry):
            kb, go, _ = carry
            return (kb >= 1) & go

        def walk(carry):
            kb, _, n = carry
            step(kb, False)
            return kb - 1, keep_going(kb), n + 1

        step(i, True)
        _, go, n = lax.while_loop(cond, walk, (i - 1, keep_going(i), jnp.int32(1)))
        first_too = go & (i > 0)

        @pl.when(first_too)
        def _():
            step(0, True)

        walked = n + first_too.astype(jnp.int32)
        for hh in range(2):
            hr = slice(HEAD_DIM * hh, HEAD_DIM * (hh + 1))
            inv = 1.0 / l_s[hh]
            o_ref[hr, :] = (acc_s[hr, :] * inv).astype(o_ref.dtype)
            ox_ref[hr, :] = (acc_s[hr, :] + accx_s[hr, :]) * inv
            lse_ref[hh:hh + 1, :] = m_s[hh] + jnp.log(l_s[hh])
        lse_ref[2:3, :] = jnp.full((1, t), walked.astype(F32))

    blk = pl.BlockSpec((128, t), lambda j, i: (j, i))
    stat = pl.BlockSpec((None, 2, t), lambda j, i: (j, 0, i))
    return pl.pallas_call(
        body, name=name, grid=(N_PAIRS, M // t),
        in_specs=[blk, pl.BlockSpec((M, 128), lambda j, i: (0, ck + j)), pl.BlockSpec((128, M), lambda j, i: (cv + j, 0)),
                  stat, pl.BlockSpec((None, M, 2), lambda j, i: (j, 0, 0))],
        out_specs=[blk, pl.BlockSpec((None, 3, t), lambda j, i: (j, 0, i)), blk],
        out_shape=[jax.ShapeDtypeStruct((N_PAIRS * 128, M), BF16), jax.ShapeDtypeStruct((N_PAIRS, 3, M), F32),
                   jax.ShapeDtypeStruct((N_PAIRS * 128, M), F32)],
        scratch_shapes=[pltpu.VMEM((2, 1, t), F32), pltpu.VMEM((2, 1, t), F32), pltpu.VMEM((128, t), F32),
                        pltpu.VMEM((128, t), F32), pltpu.SMEM((2,), F32)],
        compiler_params=_params(("parallel", "arbitrary")))(qkv_t, qkv, qkv_t, f_rows, f_cols)


def _fox_bwd(qkv, qkv_t, o_t, do, do_t, lse, f_rows, f_cols, scale, name):
    M = qkv.shape[0]
    t = ATTN_TILE
    ck, cv = N_PAIRS, 2 * N_PAIRS

    def body(q_ref, qt_ref, k_ref, kt_ref, v_ref, ot_ref, do_ref, dot_ref, lse_ref, fq_ref, fk_ref,
             dq_ref, dk_ref, dv_ref, cs_ref, dq_s):
        i = pl.program_id(1)

        @pl.when(i == 0)
        def _():
            dk_ref[...] = jnp.zeros_like(dk_ref)
            dv_ref[...] = jnp.zeros_like(dv_ref)
            cs_ref[...] = jnp.zeros_like(cs_ref)

        heads_l = (_lanes_between(0, 64), _lanes_between(64, 128))
        heads_r = (_rows_between(0, 64), _rows_between(64, 128))
        q = q_ref[...]
        qt = qt_ref[...]
        do = do_ref[...]
        dot = dot_ref[...]
        qs = tuple(_keep(q, m) for m in heads_l)
        qts = tuple(_keep(qt, m) for m in heads_r)
        dos = tuple(_keep(do, m) for m in heads_l)
        dots = tuple(_keep(dot, m) for m in heads_r)
        prod = dot.astype(F32) * ot_ref[...]
        deltas = tuple(jnp.sum(prod[HEAD_DIM * hh:HEAD_DIM * (hh + 1)], axis=0, keepdims=True) for hh in range(2))
        ones = tuple(m.astype(BF16) * jnp.ones((t, 128), BF16) for m in heads_l)
        dq_s[...] = jnp.zeros_like(dq_s)

        def step(kb, masked):
            k0 = pl.multiple_of(kb * t, t)
            rows = pl.ds(k0, t)
            k = k_ref[rows, :]
            v = v_ref[rows, :]
            if masked:
                valid = _valid_mask(i, kb, t)
            dk = dv = cs = None
            for hh in range(2):
                s = _dot(k, qts[hh]) * scale + (fq_ref[hh:hh + 1, :] - fk_ref[rows, hh:hh + 1])
                if masked:
                    s = jnp.where(valid, s, NEG)
                p = jnp.exp(s - lse_ref[hh:hh + 1, :])
                ds = p * (_dot(v, dots[hh]) - deltas[hh])
                hi = ds.astype(BF16)
                lo = (ds - hi.astype(F32)).astype(BF16)
                c = _dot(hi, ones[hh]) + _dot(lo, ones[hh])
                dsb = (ds * scale).astype(BF16)
                hr = slice(HEAD_DIM * hh, HEAD_DIM * (hh + 1))
                dq_s[hr, :] += _dot(kt_ref[hr, rows], dsb)
                a = _dot(dsb, qs[hh])
                b = _dot(p.astype(BF16), dos[hh])
                dk = a if dk is None else dk + a
                dv = b if dv is None else dv + b
                cs = c if cs is None else cs + c
            dk_ref[rows, :] += dk
            dv_ref[rows, :] += dv
            cs_ref[rows, :] += cs

        first_walked = i + 1 - jnp.max(lse_ref[2:3, :]).astype(jnp.int32)

        def mid(kb, carry):
            step(kb, False)
            return carry

        @pl.when(first_walked == 0)
        def _():
            step(0, True)

        lax.fori_loop(jnp.maximum(first_walked, 1), i, mid, 0)

        @pl.when(i > 0)
        def _():
            step(i, True)

        dq_ref[...] = dq_s[...].astype(dq_ref.dtype)

    rblk = pl.BlockSpec((t, 128), lambda j, i: (i, j))
    tblk = pl.BlockSpec((128, t), lambda j, i: (j, i))
    stat = pl.BlockSpec((None, 2, t), lambda j, i: (j, 0, i))
    stat3 = pl.BlockSpec((None, 3, t), lambda j, i: (j, 0, i))
    col = pl.BlockSpec((M, 128), lambda j, i: (0, j))
    wide = jax.ShapeDtypeStruct((M, N_PAIRS * 128), F32)
    return pl.pallas_call(
        body, name=name, grid=(N_PAIRS, M // t),
        in_specs=[rblk, tblk, pl.BlockSpec((M, 128), lambda j, i: (0, ck + j)),
                  pl.BlockSpec((128, M), lambda j, i: (ck + j, 0)), pl.BlockSpec((M, 128), lambda j, i: (0, cv + j)),
                  tblk, rblk, tblk, stat3, stat, pl.BlockSpec((None, M, 2), lambda j, i: (j, 0, 0))],
        out_specs=[tblk, col, col, pl.BlockSpec((None, M, 128), lambda j, i: (j, 0, 0))],
        out_shape=[jax.ShapeDtypeStruct((N_PAIRS * 128, M), BF16), wide, wide,
                   jax.ShapeDtypeStruct((N_PAIRS, M, 128), F32)],
        scratch_shapes=[pltpu.VMEM((128, t), F32)],
        compiler_params=_params(("parallel", "arbitrary")))(qkv, qkv_t, qkv, qkv_t, qkv, o_t, do, do_t, lse,
                                                            f_rows, f_cols)


def _rope_tables(M):
    pos = (jnp.arange(M, dtype=jnp.int32) - PAD).astype(F32)
    inv = ROPE_THETA ** (-jnp.arange(0, MLA_ROPE, 2, dtype=F32) / MLA_ROPE)
    ang = pos[:, None] * inv[None, :]
    cos, sin = jnp.cos(ang), jnp.sin(ang)
    z = jnp.zeros((M, 64), F32)
    cos_t = jnp.concatenate([cos, cos, cos, cos, z], axis=1)
    sin_t = jnp.concatenate([-sin, sin, -sin, sin, z], axis=1)
    return cos_t, sin_t


def _rope(x, cos_t, sin_t, out_dtype, name, inverse=False, lead=0):
    M, C = x.shape
    tm = _pick(M, (768, 512, 256, 128))
    nblk = (C - lead) // 128
    sign = -1.0 if inverse else 1.0

    def body(x_ref, c_ref, s_ref, o_ref):
        lane = lax.broadcasted_iota(jnp.int32, (1, 128), 1)
        low = (lane % MLA_ROPE) < (MLA_ROPE // 2)
        cos = c_ref[...]
        sin = s_ref[...] * sign
        if lead:
            o_ref[:, :lead] = x_ref[:, :lead].astype(o_ref.dtype)
        for b in range(nblk):
            cols = slice(lead + b * 128, lead + (b + 1) * 128)
            v = x_ref[:, cols].astype(F32)
            up = pltpu.roll(v, 128 - MLA_ROPE // 2, 1)
            down = pltpu.roll(v, MLA_ROPE // 2, 1)
            o_ref[:, cols] = (v * cos + jnp.where(low, up, down) * sin).astype(o_ref.dtype)

    row = pl.BlockSpec((tm, C), lambda m: (m, 0))
    tab = pl.BlockSpec((tm, 128), lambda m: (m, 0))
    return pl.pallas_call(
        body, name=name, grid=(M // tm,), in_specs=[row, tab, tab], out_specs=row,
        out_shape=jax.ShapeDtypeStruct((M, C), out_dtype),
        compiler_params=_params(("parallel",)))(x, cos_t, sin_t)


def _forget_cumsum(f_logit, bias, name):
    M = f_logit.shape[0]
    tm = 256

    def body(f_ref, b_ref, o_ref, c_s):
        i = pl.program_id(0)

        @pl.when(i == 0)
        def _():
            c_s[...] = jnp.zeros_like(c_s)
        ls, _ = _log_sigmoids(f_ref[...] + b_ref[...])
        rows = i * tm + lax.broadcasted_iota(jnp.int32, (tm, 1), 0)
        ls = jnp.where(rows >= PAD, ls, 0.0)
        r = lax.broadcasted_iota(jnp.int32, (tm, tm), 0)
        c = lax.broadcasted_iota(jnp.int32, (tm, tm), 1)
        tri = (c <= r).astype(F32)
        cum = jnp.dot(tri, ls, precision=lax.Precision.HIGHEST, preferred_element_type=F32) + c_s[...]
        o_ref[...] = cum
        c_s[...] = cum[tm - 1:tm, :]

    row = pl.BlockSpec((tm, 128), lambda m: (m, 0))
    return pl.pallas_call(
        body, name=name, grid=(M // tm,),
        in_specs=[row, pl.BlockSpec((1, 128), lambda m: (0, 0))], out_specs=row,
        out_shape=jax.ShapeDtypeStruct((M, 128), F32), scratch_shapes=[pltpu.VMEM((1, 128), F32)],
        compiler_params=_params(("arbitrary",)))(f_logit, bias)


def _forget_cumsum_bwd(f_logit, bias, dF, name):
    M = f_logit.shape[0]
    tm = 256
    nb = M // tm

    def body(f_ref, b_ref, d_ref, o_ref, db_ref, c_s):
        i = pl.program_id(0)

        @pl.when(i == 0)
        def _():
            c_s[...] = jnp.zeros_like(c_s)
            db_ref[...] = jnp.zeros_like(db_ref)
        r = lax.broadcasted_iota(jnp.int32, (tm, tm), 0)
        c = lax.broadcasted_iota(jnp.int32, (tm, tm), 1)
        tri = (c >= r).astype(F32)
        cum = jnp.dot(tri, d_ref[...], precision=lax.Precision.HIGHEST, preferred_element_type=F32) + c_s[...]
        c_s[...] = cum[0:1, :]
        _, lsn = _log_sigmoids(f_ref[...] + b_ref[...])
        rows = (nb - 1 - i) * tm + lax.broadcasted_iota(jnp.int32, (tm, 1), 0)
        dl = jnp.where(rows >= PAD, cum * jnp.exp(lsn), 0.0)
        o_ref[...] = dl
        db_ref[...] += jnp.sum(dl, axis=0, keepdims=True)

    row = pl.BlockSpec((tm, 128), lambda m: (nb - 1 - m, 0))
    vec = pl.BlockSpec((1, 128), lambda m: (0, 0))
    return pl.pallas_call(
        body, name=name, grid=(nb,), in_specs=[row, vec, row], out_specs=[row, vec],
        out_shape=[jax.ShapeDtypeStruct((M, 128), F32), jax.ShapeDtypeStruct((1, 128), F32)],
        scratch_shapes=[pltpu.VMEM((1, 128), F32)],
        compiler_params=_params(("arbitrary",)))(f_logit, bias, dF)


def _adamw(w, parts, m, v, name):
    R, C = w.shape
    tr = R
    for d in range(8, R, 8):
        if R % d == 0 and d * C <= ADAM_TILE_ELEMS:
            tr = d
    c1 = 1.0 - ADAM_B1 ** ADAM_STEP
    c2 = 1.0 - ADAM_B2 ** ADAM_STEP

    def body(w_ref, s_ref, m_ref, v_ref, g_ref, d_ref, mo_ref, vo_ref):
        g = s_ref[0].astype(F32)
        for k in range(1, N_DEV):
            g = g + s_ref[k].astype(F32)
        mn = ADAM_B1 * m_ref[...] + (1.0 - ADAM_B1) * g
        vn = ADAM_B2 * v_ref[...] + (1.0 - ADAM_B2) * (g * g)
        m_hat = mn / c1
        v_hat = vn / c2
        g_ref[...] = g
        d_ref[...] = -ADAM_LR * (m_hat / (jnp.sqrt(v_hat) + ADAM_EPS) + ADAM_WD * w_ref[...])
        mo_ref[...] = mn
        vo_ref[...] = vn

    row = pl.BlockSpec((tr, C), lambda r: (r, 0))
    shp = jax.ShapeDtypeStruct((R, C), F32)
    return pl.pallas_call(
        body, name=name, grid=(R // tr,),
        in_specs=[row, pl.BlockSpec((N_DEV, tr, C), lambda r: (0, r, 0)), row, row],
        out_specs=[row, row, row, row], out_shape=[shp, shp, shp, shp],
        compiler_params=_params(("parallel",)))(w, parts, m, v)


def _position():
    return lax.axis_index("x"), lax.axis_index("y"), lax.axis_index("c")


def _all_gather(blocks, name):
    n = len(blocks)

    def body(*refs):
        x_refs, out_refs = refs[:n], refs[n:2 * n]
        send_sems, recv_sems, local_sems = refs[2 * n:]
        x, y, c = _position()
        me, sibling = (x, y, c), (x, y, 1 - c)
        chips = [(1 - x, y), (x, 1 - y), (1 - x, 1 - y)]

        def copies(k, block, to, own=False):
            slot = 4 * block[0] + 2 * block[1] + block[2]
            return [pltpu.make_async_remote_copy(
                src_ref=x_refs[p] if own else out_refs[p].at[slot], dst_ref=out_refs[p].at[slot],
                send_sem=send_sems.at[k, p], recv_sem=recv_sems.at[k, p], device_id=to, device_id_type=MESH)
                for p in range(n)]

        mine = [pltpu.make_async_copy(x_refs[p], out_refs[p].at[4 * x + 2 * y + c], local_sems.at[p]) for p in range(n)]
        for cp in mine:
            cp.start()
        first = copies(0, me, sibling, own=True)
        for j, chip in enumerate(chips):
            first += copies(1 + j, me, (*chip, c), own=True)
        for cp in first:
            cp.start()
        passed = []
        for j, chip in enumerate(chips):
            for cp in copies(1 + j, (*chip, c), me):
                cp.wait_recv()
            onward = copies(4 + j, (*chip, c), sibling)
            for cp in onward:
                cp.start()
            passed += onward
        for cp in copies(0, sibling, me):
            cp.wait_recv()
        for j, chip in enumerate(chips):
            for cp in copies(4 + j, (*chip, 1 - c), me):
                cp.wait_recv()
        for cp in first + passed:
            cp.wait_send()
        for cp in mine:
            cp.wait()

    any_spec = pl.BlockSpec(memory_space=pl.ANY)
    return pl.pallas_call(
        body, name=name, out_shape=[jax.ShapeDtypeStruct((N_DEV,) + b.shape, b.dtype) for b in blocks],
        in_specs=[any_spec] * n, out_specs=[any_spec] * n,
        scratch_shapes=[pltpu.SemaphoreType.DMA((7, n)), pltpu.SemaphoreType.DMA((7, n)), pltpu.SemaphoreType.DMA((n,))],
    )(*blocks)


def _exchange(parts, name):
    n = len(parts)

    def body(*refs):
        g_refs, land_refs = refs[:n], refs[n:2 * n]
        send_sems, recv_sems, local_sems = refs[2 * n:]
        x, y, c = _position()
        me = 4 * x + 2 * y + c
        mine = [pltpu.make_async_copy(g_refs[p].at[me], land_refs[p].at[me], local_sems.at[p]) for p in range(n)]
        for cp in mine:
            cp.start()
        sends, recvs = [], []
        for k in range(1, N_DEV):
            px = 1 - x if k & 4 else x
            py = 1 - y if k & 2 else y
            pc = 1 - c if k & 1 else c
            peer = 4 * px + 2 * py + pc
            for p in range(n):
                sends.append(pltpu.make_async_remote_copy(
                    src_ref=g_refs[p].at[peer], dst_ref=land_refs[p].at[me], send_sem=send_sems.at[k - 1, p],
                    recv_sem=recv_sems.at[k - 1, p], device_id=(px, py, pc), device_id_type=MESH))
                recvs.append(pltpu.make_async_remote_copy(
                    src_ref=g_refs[p].at[me], dst_ref=land_refs[p].at[peer], send_sem=send_sems.at[k - 1, p],
                    recv_sem=recv_sems.at[k - 1, p], device_id=(px, py, pc), device_id_type=MESH))
        for cp in sends:
            cp.start()
        for cp in recvs:
            cp.wait_recv()
        for cp in sends:
            cp.wait_send()
        for cp in mine:
            cp.wait()

    any_spec = pl.BlockSpec(memory_space=pl.ANY)
    return pl.pallas_call(
        body, name=name, out_shape=[jax.ShapeDtypeStruct(p.shape, p.dtype) for p in parts],
        in_specs=[any_spec] * n, out_specs=[any_spec] * n,
        scratch_shapes=[pltpu.SemaphoreType.DMA((7, n)), pltpu.SemaphoreType.DMA((7, n)), pltpu.SemaphoreType.DMA((n,))],
    )(*parts)


SHARDED = (("sb_w_qkv", 2), ("sb_w_o", 1), ("mla_w_down", 1), ("mla_w_uq", 2), ("mla_w_ukv", 2), ("mla_w_o", 1),
           ("fox_w_qkvf", 2), ("fox_w_o", 1), ("ffn_w_gate", 2), ("ffn_w_up", 2), ("ffn_w_down", 1),
           ("pool_w", 2), ("meta", 1), ("mla_q_norm", 1), ("mla_kv_norm", 1))
KEPT_F32 = ("meta", "mla_q_norm", "mla_kv_norm")
REPLICATED = ("norm_mix", "norm_ffn", "pool_scale", "fox_b_f", "final_norm")
WEIGHT_NAMES = ("meta", "norm_mix", "norm_ffn", "pool_w", "pool_scale", "sb_w_qkv", "sb_w_o", "mla_w_down",
                "mla_q_norm", "mla_kv_norm", "mla_w_uq", "mla_w_ukv", "mla_w_o", "fox_w_qkvf", "fox_b_f",
                "fox_w_o", "ffn_w_gate", "ffn_w_up", "ffn_w_down", "final_norm")
LANES = 1024


def _pack_rows(arrays, names):
    parts = []
    for n in names:
        flat = arrays[n].reshape(-1).astype(F32)
        rows = -(-flat.shape[0] // LANES)
        parts.append(jnp.pad(flat, (0, rows * LANES - flat.shape[0])).reshape(rows, LANES))
    rows = sum(p.shape[0] for p in parts)
    parts.append(jnp.zeros((-(-rows // 8) * 8 - rows, LANES), F32))
    return jnp.concatenate(parts, axis=0)


def _unpack_rows(buf, shapes, names):
    out, row = {}, 0
    for n in names:
        size = int(np.prod(shapes[n]))
        rows = -(-size // LANES)
        out[n] = buf[row:row + rows].reshape(-1)[:size].reshape(shapes[n])
        row += rows
    return out


def _whole_from_gathered(g, axis):
    g = jnp.moveaxis(g, 0, axis)
    shp = g.shape
    return g.reshape(shp[:axis] + (shp[axis] * shp[axis + 1],) + shp[axis + 2:])


def _parts_from_whole(whole, axis):
    shp = whole.shape
    g = whole.reshape(shp[:axis] + (N_DEV, shp[axis] // N_DEV) + shp[axis + 1:])
    return jnp.moveaxis(g, axis, 0)


def _kernel_weights(full):
    W = {}
    W["pool_w"] = full["pool_w"][0]
    W["sb_w_qkv"] = full["sb_w_qkv"][0]
    W["sb_w_o"] = full["sb_w_o"][0]
    W["mla_w_down"] = full["mla_w_down"][0]
    uq = full["mla_w_uq"][0].reshape(MLA_Q_RANK, N_HEADS, MLA_NOPE + MLA_ROPE)
    nope = uq[:, :, :MLA_NOPE].reshape(MLA_Q_RANK, N_HEADS * MLA_NOPE)
    rope = uq[:, :, MLA_NOPE:].reshape(MLA_Q_RANK, N_PAIRS, 2 * MLA_ROPE)
    rope = jnp.pad(rope, ((0, 0), (0, 0), (0, 128 - 2 * MLA_ROPE))).reshape(MLA_Q_RANK, N_PAIRS * 128)
    W["mla_w_uq"] = jnp.concatenate([nope, rope], axis=1)
    ukv = full["mla_w_ukv"][0].reshape(MLA_KV_RANK, N_HEADS, 2, HEAD_DIM)
    W["mla_w_ukv"] = jnp.transpose(ukv, (0, 2, 1, 3)).reshape(MLA_KV_RANK, 2 * N_HEADS * HEAD_DIM)
    W["mla_w_o"] = full["mla_w_o"][0]
    qkvf = full["fox_w_qkvf"][0]
    n_qkv = 3 * N_HEADS * HEAD_DIM
    W["fox_w_qkv"] = qkvf[:, :n_qkv]
    W["fox_w_f"] = jnp.pad(qkvf[:, n_qkv:], ((0, 0), (0, 128 - N_HEADS)))
    W["fox_w_qkvf"] = jnp.concatenate([W["fox_w_qkv"], W["fox_w_f"]], axis=1)
    W["fox_w_o"] = full["fox_w_o"][0]
    W["ffn_w_gu"] = jnp.concatenate([full["ffn_w_gate"], full["ffn_w_up"]], axis=2)
    W["ffn_w_down"] = full["ffn_w_down"]
    return W


def _reference_grads(G):
    out = {}
    out["pool_w"] = G["pool_w"][None]
    for n in ("sb_w_qkv", "sb_w_o", "mla_w_down", "mla_w_o", "fox_w_o"):
        out[n] = G[n][None]
    duq = G["mla_w_uq"]
    nope = duq[:, :N_HEADS * MLA_NOPE].reshape(MLA_Q_RANK, N_HEADS, MLA_NOPE)
    rope = duq[:, N_HEADS * MLA_NOPE:].reshape(MLA_Q_RANK, N_PAIRS, 128)[:, :, :2 * MLA_ROPE]
    rope = rope.reshape(MLA_Q_RANK, N_HEADS, MLA_ROPE)
    out["mla_w_uq"] = jnp.concatenate([nope, rope], axis=2).reshape(1, MLA_Q_RANK, -1)
    dukv = G["mla_w_ukv"].reshape(MLA_KV_RANK, 2, N_HEADS, HEAD_DIM)
    out["mla_w_ukv"] = jnp.transpose(dukv, (0, 2, 1, 3)).reshape(1, MLA_KV_RANK, -1)
    out["fox_w_qkvf"] = G["fox_w_qkvf"][None, :, :3 * N_HEADS * HEAD_DIM + N_HEADS]
    out["ffn_w_gate"] = G["ffn_w_gu"][:, :, :D_FF]
    out["ffn_w_up"] = G["ffn_w_gu"][:, :, D_FF:]
    out["ffn_w_down"] = G["ffn_w_down"]
    out["mla_q_norm"] = G["mla_q_norm"]
    out["mla_kv_norm"] = G["mla_kv_norm"]
    return out


def _pairs_col(f16):
    M = f16.shape[0]
    return jnp.transpose(f16.reshape(M, N_PAIRS, 2), (1, 0, 2))


def _pairs_row(f16):
    M = f16.shape[0]
    return jnp.transpose(f16.reshape(M, N_PAIRS, 2), (1, 2, 0))


def _local_step(x, target, W, P):
    S = x.shape[0]
    M = S + ROW0
    G = {}
    gain = lambda name, i: P[name][i][None, :]
    h0 = jnp.concatenate([jnp.zeros((PAD, D_MODEL), F32), P["meta"], x], axis=0)

    def ffn_fwd(h1, i):
        b = _norm_fwd(h1, gain("norm_ffn", i), BF16, f"ffn{i}_norm")
        g, u, act = _ffn_up(b, W["ffn_w_gu"][i], f"ffn{i}_up")
        h2 = _mm_nn(act, W["ffn_w_down"][i], F32, f"ffn{i}_down", res=h1)
        return h2, (h1, b, g, u, act)

    def ffn_bwd(dh2, saved, i):
        h1, b, g, u, act = saved
        dgu = _ffn_dact(dh2, W["ffn_w_down"][i], g, u, f"ffn{i}_dact")
        G.setdefault("ffn_w_down", {})[i] = _mm_tn(act, dh2, f"ffn{i}_dwd")
        db = _mm_nt(dgu, W["ffn_w_gu"][i], F32, f"ffn{i}_db")
        G.setdefault("ffn_w_gu", {})[i] = _mm_tn(b, dgu, f"ffn{i}_dwgu")
        dh1, dgain = _norm_bwd(h1, gain("norm_ffn", i), db, dh2, f"ffn{i}_dnorm")
        G.setdefault("norm_ffn", {})[i] = dgain
        return dh1

    a0 = _norm_fwd(h0, gain("norm_mix", 0), F32, "mix0_norm")
    h1_0, pooled = _pool_fwd(h0, a0, W["pool_w"], P["pool_scale"], "pool_fwd")
    h_1, ffn0 = ffn_fwd(h1_0, 0)

    sb_scale = HEAD_DIM ** -0.5
    a1 = _norm_fwd(h_1, gain("norm_mix", 1), BF16, "mix1_norm")
    sb_qkv = _mm_nn(a1, W["sb_w_qkv"], BF16, "sb_qkv")
    sb_o, sb_tot = _sb_fwd(sb_qkv, sb_scale, "sb_fwd")
    h1_1 = _mm_nn(sb_o, W["sb_w_o"], F32, "sb_out", res=h_1)
    h_2, ffn1 = ffn_fwd(h1_1, 1)

    mla_scale = (MLA_NOPE + MLA_ROPE) ** -0.5
    cos_t, sin_t = _rope_tables(M)
    a2 = _norm_fwd(h_2, gain("norm_mix", 2), BF16, "mix2_norm")
    down = _mm_nn(a2, W["mla_w_down"], F32, "mla_down")
    dq_raw = down[:, :MLA_Q_RANK]
    dkv_raw = down[:, MLA_Q_RANK:MLA_Q_RANK + MLA_KV_RANK]
    kr_raw = down[:, MLA_Q_RANK + MLA_KV_RANK:]
    c_q = _norm_fwd(dq_raw, P["mla_q_norm"], BF16, "mla_qnorm")
    c_kv = _norm_fwd(dkv_raw, P["mla_kv_norm"], BF16, "mla_kvnorm")
    q_lin = _mm_nn(c_q, W["mla_w_uq"], F32, "mla_uq")
    q_all = _rope(q_lin, cos_t, sin_t, BF16, "mla_qrope", lead=D_MODEL)
    kv_all = _mm_nn(c_kv, W["mla_w_ukv"], BF16, "mla_ukv")
    kr_in = jnp.concatenate([kr_raw, kr_raw, jnp.zeros((M, 64), F32)], axis=1)
    kr = _rope(kr_in, cos_t, sin_t, BF16, "mla_krope")
    q_rope = q_all[:, D_MODEL:]
    mla_o, mla_lse = _mla_fwd(q_all, kv_all, q_rope, kr, mla_scale, "mla_fwd")
    h1_2 = _mm_nn(mla_o, W["mla_w_o"], F32, "mla_out", res=h_2)
    h_3, ffn2 = ffn_fwd(h1_2, 2)

    fox_scale = HEAD_DIM ** -0.5
    a3 = _norm_fwd(h_3, gain("norm_mix", 3), BF16, "mix3_norm")
    fox_qkv = _mm_nn(a3, W["fox_w_qkv"], BF16, "fox_qkv")
    f_logit = _mm_nn(a3, W["fox_w_f"], F32, "fox_f")
    b_f = jnp.pad(P["fox_b_f"], ((0, 0), (0, 128 - N_HEADS)))
    Fc = _forget_cumsum(f_logit, b_f, "fox_cumsum")
    f_rows, f_cols = _pairs_row(Fc[:, :N_HEADS]), _pairs_col(Fc[:, :N_HEADS])
    fox_qkv_t = fox_qkv.T
    fox_o_t, fox_lse, fox_ox_t = _fox_fwd(fox_qkv, fox_qkv_t, f_rows, f_cols, fox_scale, "fox_fwd")
    fox_o = fox_o_t.T
    h1_3 = _mm_nn(fox_o, W["fox_w_o"], F32, "fox_out", res=h_3)
    h_4, ffn3 = ffn_fwd(h1_3, 3)

    sq, dh, dgain = _loss_head(h_4, P["final_norm"][None, :], target, "loss_head")
    G["final_norm"] = dgain[0]

    dh = ffn_bwd(dh, ffn3, 3)
    do = _mm_nt(dh, W["fox_w_o"], BF16, "fox_do")
    G["fox_w_o"] = _mm_tn(fox_o, dh, "fox_dwo")
    dq_t, dk, dv, colsum = _fox_bwd(fox_qkv, fox_qkv_t, fox_ox_t, do, do.T, fox_lse, f_rows, f_cols, fox_scale,
                                    "fox_bwd")
    dF = -jnp.transpose(colsum[:, :, ::HEAD_DIM], (1, 0, 2)).reshape(M, N_HEADS)
    dF = jnp.pad(dF, ((0, 0), (0, 128 - N_HEADS)))
    dlogit, db_f = _forget_cumsum_bwd(f_logit, b_f, dF, "fox_dcumsum")
    G["fox_b_f"] = db_f[:, :N_HEADS]
    dproj = jnp.concatenate([dq_t.T, dk.astype(BF16), dv.astype(BF16), dlogit.astype(BF16)], axis=1)
    da = _mm_nt(dproj, W["fox_w_qkvf"], F32, "fox_da")
    G["fox_w_qkvf"] = _mm_tn(a3, dproj, "fox_dwqkvf")
    dh, dgain = _norm_bwd(h_3, gain("norm_mix", 3), da, dh, "mix3_dnorm")
    G.setdefault("norm_mix", {})[3] = dgain

    dh = ffn_bwd(dh, ffn2, 2)
    do = _mm_nt(dh, W["mla_w_o"], BF16, "mla_do")
    G["mla_w_o"] = _mm_tn(mla_o, dh, "mla_dwo")
    dq, dk, dv, dqr, dkr = _mla_bwd(q_all, kv_all, q_rope, kr, mla_o, do, mla_lse, mla_scale, "mla_bwd")
    dqr = _rope(dqr, cos_t, sin_t, BF16, "mla_dqrope", inverse=True)
    dq_all = jnp.concatenate([dq, dqr], axis=1)
    dkr_sum = _rope(jnp.sum(dkr, axis=0), cos_t, sin_t, F32, "mla_dkrope", inverse=True)
    dkr_raw = dkr_sum[:, :MLA_ROPE] + dkr_sum[:, MLA_ROPE:2 * MLA_ROPE]
    dkv_all = jnp.concatenate([dk.astype(BF16), dv.astype(BF16)], axis=1)
    dc_q = _mm_nt(dq_all, W["mla_w_uq"], F32, "mla_dcq")
    G["mla_w_uq"] = _mm_tn(c_q, dq_all, "mla_dwuq")
    dc_kv = _mm_nt(dkv_all, W["mla_w_ukv"], F32, "mla_dckv")
    G["mla_w_ukv"] = _mm_tn(c_kv, dkv_all, "mla_dwukv")
    ddq_raw, G["mla_q_norm"] = _norm_bwd(dq_raw, P["mla_q_norm"], dc_q, None, "mla_dqnorm")
    ddkv_raw, G["mla_kv_norm"] = _norm_bwd(dkv_raw, P["mla_kv_norm"], dc_kv, None, "mla_dkvnorm")
    ddown = jnp.concatenate([ddq_raw, ddkv_raw, dkr_raw], axis=1).astype(BF16)
    da = _mm_nt(ddown, W["mla_w_down"], F32, "mla_da")
    G["mla_w_down"] = _mm_tn(a2, ddown, "mla_dwdown")
    dh, dgain = _norm_bwd(h_2, gain("norm_mix", 2), da, dh, "mix2_dnorm")
    G["norm_mix"][2] = dgain

    dh = ffn_bwd(dh, ffn1, 1)
    do = _mm_nt(dh, W["sb_w_o"], BF16, "sb_do")
    G["sb_w_o"] = _mm_tn(sb_o, dh, "sb_dwo")
    dq, dk, dv = _sb_bwd(sb_qkv, do, sb_tot, sb_scale, "sb_bwd")
    dqkv = jnp.concatenate([dq, dk.astype(BF16), dv.astype(BF16)], axis=1)
    da = _mm_nt(dqkv, W["sb_w_qkv"], F32, "sb_da")
    G["sb_w_qkv"] = _mm_tn(a1, dqkv, "sb_dwqkv")
    dh, dgain = _norm_bwd(h_1, gain("norm_mix", 1), da, dh, "mix1_dnorm")
    G["norm_mix"][1] = dgain

    dh = ffn_bwd(dh, ffn0, 0)
    dpc, G["pool_w"], G["pool_scale"] = _pool_bwd_mix(dh, pooled, W["pool_w"], P["pool_scale"], "pool_dmix")
    da = _pool_bwd_window(dpc, "pool_dwindow")
    dh, dgain = _norm_bwd(h0, gain("norm_mix", 0), da, dh, "mix0_dnorm")
    G["norm_mix"][0] = dgain

    G["norm_mix"] = jnp.concatenate([G["norm_mix"][i] for i in range(DEPTH)], axis=0)
    G["norm_ffn"] = jnp.concatenate([G["norm_ffn"][i] for i in range(DEPTH)], axis=0)
    G["ffn_w_down"] = jnp.stack([G["ffn_w_down"][i] for i in range(DEPTH)])
    G["ffn_w_gu"] = jnp.stack([G["ffn_w_gu"][i] for i in range(DEPTH)])
    return sq, dh, G


def kernel(x, meta, norm_mix, norm_ffn, pool_w, pool_scale, sb_w_qkv, sb_w_o, mla_w_down, mla_q_norm, mla_kv_norm, mla_w_uq, mla_w_ukv, mla_w_o, fox_w_qkvf, fox_b_f, fox_w_o, ffn_w_gate, ffn_w_up, ffn_w_down, final_norm, loss_target, m_meta, m_norm_mix, m_norm_ffn, m_pool_w, m_pool_scale, m_sb_w_qkv, m_sb_w_o, m_mla_w_down, m_mla_q_norm, m_mla_kv_norm, m_mla_w_uq, m_mla_w_ukv, m_mla_w_o, m_fox_w_qkvf, m_fox_b_f, m_fox_w_o, m_ffn_w_gate, m_ffn_w_up, m_ffn_w_down, m_final_norm, v_meta, v_norm_mix, v_norm_ffn, v_pool_w, v_pool_scale, v_sb_w_qkv, v_sb_w_o, v_mla_w_down, v_mla_q_norm, v_mla_kv_norm, v_mla_w_uq, v_mla_w_ukv, v_mla_w_o, v_fox_w_qkvf, v_fox_b_f, v_fox_w_o, v_ffn_w_gate, v_ffn_w_up, v_ffn_w_down, v_final_norm):
    w = dict(meta=meta, norm_mix=norm_mix, norm_ffn=norm_ffn, pool_w=pool_w, pool_scale=pool_scale,
             sb_w_qkv=sb_w_qkv, sb_w_o=sb_w_o, mla_w_down=mla_w_down, mla_q_norm=mla_q_norm,
             mla_kv_norm=mla_kv_norm, mla_w_uq=mla_w_uq, mla_w_ukv=mla_w_ukv, mla_w_o=mla_w_o,
             fox_w_qkvf=fox_w_qkvf, fox_b_f=fox_b_f, fox_w_o=fox_w_o, ffn_w_gate=ffn_w_gate, ffn_w_up=ffn_w_up,
             ffn_w_down=ffn_w_down, final_norm=final_norm)
    m = dict(meta=m_meta, norm_mix=m_norm_mix, norm_ffn=m_norm_ffn, pool_w=m_pool_w, pool_scale=m_pool_scale,
             sb_w_qkv=m_sb_w_qkv, sb_w_o=m_sb_w_o, mla_w_down=m_mla_w_down, mla_q_norm=m_mla_q_norm,
             mla_kv_norm=m_mla_kv_norm, mla_w_uq=m_mla_w_uq, mla_w_ukv=m_mla_w_ukv, mla_w_o=m_mla_w_o,
             fox_w_qkvf=m_fox_w_qkvf, fox_b_f=m_fox_b_f, fox_w_o=m_fox_w_o, ffn_w_gate=m_ffn_w_gate,
             ffn_w_up=m_ffn_w_up, ffn_w_down=m_ffn_w_down, final_norm=m_final_norm)
    v = dict(meta=v_meta, norm_mix=v_norm_mix, norm_ffn=v_norm_ffn, pool_w=v_pool_w, pool_scale=v_pool_scale,
             sb_w_qkv=v_sb_w_qkv, sb_w_o=v_sb_w_o, mla_w_down=v_mla_w_down, mla_q_norm=v_mla_q_norm,
             mla_kv_norm=v_mla_kv_norm, mla_w_uq=v_mla_w_uq, mla_w_ukv=v_mla_w_ukv, mla_w_o=v_mla_w_o,
             fox_w_qkvf=v_fox_w_qkvf, fox_b_f=v_fox_b_f, fox_w_o=v_fox_w_o, ffn_w_gate=v_ffn_w_gate,
             ffn_w_up=v_ffn_w_up, ffn_w_down=v_ffn_w_down, final_norm=v_final_norm)

    sh_names = tuple(n for n, _ in SHARDED)
    sh_axis = dict(SHARDED)
    shapes = {n: w[n].shape for n in WEIGHT_NAMES}
    wire = lambda n: F32 if n in KEPT_F32 else BF16

    gathered = _all_gather([w[n].astype(wire(n)) for n in sh_names], "gather_weights")
    full = {n: _whole_from_gathered(g, sh_axis[n]) for n, g in zip(sh_names, gathered)}
    W = _kernel_weights(full)
    P = dict(meta=full["meta"], mla_q_norm=full["mla_q_norm"], mla_kv_norm=full["mla_kv_norm"],
             norm_mix=norm_mix, norm_ffn=norm_ffn, pool_scale=pool_scale, fox_b_f=fox_b_f, final_norm=final_norm)

    sq, dh0, G = _local_step(x[0], loss_target[0], W, P)
    loss = lax.psum(0.5 * jnp.sum(sq) / D_MODEL, ("x", "y", "c"))
    grad_x = dh0[ROW0:][None]

    gw = _reference_grads(G)
    gw["meta"] = dh0[PAD:ROW0]
    parts = [_parts_from_whole(gw[n], sh_axis[n]).astype(wire(n)) for n in sh_names]
    landed = _exchange(parts, "exchange_grads")
    results = {}
    for n, got in zip(sh_names, landed):
        rc = (int(np.prod(shapes[n][:-1])), shapes[n][-1])
        outs = _adamw(w[n].reshape(rc), got.reshape((N_DEV,) + rc), m[n].reshape(rc), v[n].reshape(rc), f"adamw_{n}")
        results[n] = [o.reshape(shapes[n]) for o in outs]

    rep_g = dict(norm_mix=G["norm_mix"], norm_ffn=G["norm_ffn"], pool_scale=G["pool_scale"], fox_b_f=G["fox_b_f"],
                 final_norm=G["final_norm"])
    (rep_all,) = _all_gather([_pack_rows(rep_g, REPLICATED)], "gather_replicated_grads")
    rep_out = _adamw(_pack_rows(w, REPLICATED), rep_all, _pack_rows(m, REPLICATED), _pack_rows(v, REPLICATED),
                     "adamw_replicated")
    rep = [_unpack_rows(o, shapes, REPLICATED) for o in rep_out]
    for n in REPLICATED:
        results[n] = [r[n] for r in rep]

    outs = [results[n][k] for k in range(4) for n in WEIGHT_NAMES]
    return (loss, grad_x, *outs)
```

```python
import numpy as np
import jax
import jax.numpy as jnp
from jax import lax
from jax.experimental import pallas as pl
from jax.experimental.pallas import tpu as pltpu

F32 = jnp.float32
BF16 = jnp.bfloat16

N_DEV = 8
D_MODEL = 1024
N_META = 16
PAD = 240
ROW0 = PAD + N_META
EPS = 1e-6
POOL_WINDOWS = (2, 4, 8, 16)
POOL_GROUP = 256
HALO = 128
N_HEADS = 16
HEAD_DIM = 64
N_PAIRS = N_HEADS // 2
MLA_Q_RANK = 384
MLA_KV_RANK = 256
MLA_NOPE = 64
MLA_ROPE = 32
ROPE_THETA = 10000.0
D_FF = 2816
DEPTH = 4
ATTN_TILE = 256
WALK_TILE = 128
NEG = -1e30
EXP_ZERO = -110.0
VMEM_LIMIT = 56 * 2**20
ADAM_TILE_ELEMS = 192 * 1024

ADAM_LR = 0.001
ADAM_B1 = 0.9
ADAM_B2 = 0.999
ADAM_EPS = 1e-08
ADAM_WD = 0.01
ADAM_STEP = 10

MESH = pl.DeviceIdType.MESH


def _params(sem=None):
    return pltpu.CompilerParams(dimension_semantics=sem, vmem_limit_bytes=VMEM_LIMIT)


def _pick(n, cands):
    for c in cands:
        if n % c == 0:
            return c
    return n


def _col_tile(n, cap=1536):
    best = None
    for t in range(128, min(n, cap) + 1, 128):
        if n % t == 0:
            best = t
    return best if best is not None else n


def _dot(a, b):
    return jnp.dot(a, b, preferred_element_type=F32)


def _dot_nt(a, b):
    return lax.dot_general(a, b, (((1,), (1,)), ((), ())), preferred_element_type=F32)


def _dot_tn(a, b):
    return lax.dot_general(a, b, (((0,), (0,)), ((), ())), preferred_element_type=F32)


def _mm_nn(a, b, out_dtype, name, res=None):
    M, K = a.shape
    N = b.shape[1]
    tm = _pick(M, (768, 512, 256, 128))
    tn = _col_tile(N)

    def body(*refs):
        if res is None:
            a_ref, b_ref, o_ref = refs
        else:
            a_ref, b_ref, r_ref, o_ref = refs
        acc = _dot(a_ref[...].astype(BF16), b_ref[...])
        if res is not None:
            acc = acc + r_ref[...]
        o_ref[...] = acc.astype(o_ref.dtype)

    in_specs = [pl.BlockSpec((tm, K), lambda n, m: (m, 0)), pl.BlockSpec((K, tn), lambda n, m: (0, n))]
    args = [a, b]
    if res is not None:
        in_specs.append(pl.BlockSpec((tm, tn), lambda n, m: (m, n)))
        args.append(res)
    return pl.pallas_call(
        body, name=name, grid=(N // tn, M // tm), in_specs=in_specs,
        out_specs=pl.BlockSpec((tm, tn), lambda n, m: (m, n)),
        out_shape=jax.ShapeDtypeStruct((M, N), out_dtype),
        compiler_params=_params(("parallel", "parallel")))(*args)


def _mm_nt(a, w, out_dtype, name):
    M, N = a.shape
    K = w.shape[0]
    tm = _pick(M, (768, 512, 256, 128)) if N <= 3200 else _pick(M, (256, 128))
    tk = _col_tile(K, 1024)

    def body(a_ref, w_ref, o_ref):
        o_ref[...] = _dot_nt(a_ref[...].astype(BF16), w_ref[...]).astype(o_ref.dtype)

    return pl.pallas_call(
        body, name=name, grid=(K // tk, M // tm),
        in_specs=[pl.BlockSpec((tm, N), lambda k, m: (m, 0)), pl.BlockSpec((tk, N), lambda k, m: (k, 0))],
        out_specs=pl.BlockSpec((tm, tk), lambda k, m: (m, k)),
        out_shape=jax.ShapeDtypeStruct((M, K), out_dtype),
        compiler_params=_params(("parallel", "parallel")))(a, w)


def _mm_tn(a, b, name):
    M, K = a.shape
    N = b.shape[1]
    tm = _pick(M, (768, 512, 256, 128))
    tk = _col_tile(K, 1408)
    tn = _col_tile(N, 1408)

    def body(a_ref, b_ref, o_ref):
        @pl.when(pl.program_id(2) == 0)
        def _():
            o_ref[...] = jnp.zeros_like(o_ref)
        o_ref[...] += _dot_tn(a_ref[...].astype(BF16), b_ref[...].astype(BF16))

    return pl.pallas_call(
        body, name=name, grid=(K // tk, N // tn, M // tm),
        in_specs=[pl.BlockSpec((tm, tk), lambda k, n, m: (m, k)), pl.BlockSpec((tm, tn), lambda k, n, m: (m, n))],
        out_specs=pl.BlockSpec((tk, tn), lambda k, n, m: (k, n)),
        out_shape=jax.ShapeDtypeStruct((K, N), F32),
        compiler_params=_params(("parallel", "parallel", "arbitrary")))(a, b)


def _norm_fwd(h, gain, out_dtype, name):
    M, C = h.shape
    tm = _pick(M, (768, 512, 256, 128))

    def body(h_ref, g_ref, a_ref):
        x = h_ref[...]
        r = lax.rsqrt(jnp.mean(x * x, axis=-1, keepdims=True) + EPS)
        a_ref[...] = ((x * r) * g_ref[...]).astype(a_ref.dtype)

    return pl.pallas_call(
        body, name=name, grid=(M // tm,),
        in_specs=[pl.BlockSpec((tm, C), lambda m: (m, 0)), pl.BlockSpec((1, C), lambda m: (0, 0))],
        out_specs=pl.BlockSpec((tm, C), lambda m: (m, 0)),
        out_shape=jax.ShapeDtypeStruct((M, C), out_dtype),
        compiler_params=_params(("parallel",)))(h, gain)


def _norm_bwd(h, gain, da, dres, name, token_rows=False):
    M, C = h.shape
    tm = ROW0 if token_rows else _pick(M, (768, 512, 256, 128))

    def body(*refs):
        refs = list(refs)
        dx_ref = refs.pop() if token_rows else None
        if dres is None:
            h_ref, g_ref, da_ref, dh_ref, dg_ref = refs
        else:
            h_ref, g_ref, da_ref, dr_ref, dh_ref, dg_ref = refs
        x = h_ref[...]
        r = lax.rsqrt(jnp.mean(x * x, axis=-1, keepdims=True) + EPS)
        y = x * r
        dav = da_ref[...].astype(F32)
        dy = dav * g_ref[...]
        dh = r * (dy - y * jnp.mean(dy * y, axis=-1, keepdims=True))
        if dres is not None:
            dh = dh + dr_ref[...]
        dh_ref[...] = dh
        if token_rows:
            dx_ref[...] = dh

        @pl.when(pl.program_id(0) == 0)
        def _():
            dg_ref[...] = jnp.zeros_like(dg_ref)
        dg_ref[...] += jnp.sum(dav * y, axis=0, keepdims=True)

    row = pl.BlockSpec((tm, C), lambda m: (m, 0))
    vec = pl.BlockSpec((1, C), lambda m: (0, 0))
    in_specs = [row, vec, row] + ([row] if dres is not None else [])
    args = [h, gain, da] + ([dres] if dres is not None else [])
    out_specs = [row, vec]
    out_shape = [jax.ShapeDtypeStruct((M, C), F32), jax.ShapeDtypeStruct((1, C), F32)]
    if token_rows:
        out_specs.append(pl.BlockSpec((tm, C), lambda m: (jnp.maximum(m - 1, 0), 0)))
        out_shape.append(jax.ShapeDtypeStruct((M - ROW0, C), F32))
    return pl.pallas_call(
        body, name=name, grid=(M // tm,), in_specs=in_specs, out_specs=out_specs, out_shape=out_shape,
        compiler_params=_params(("arbitrary",)))(*args)


def _ffn_up(b, w_gu, name):
    M, K = b.shape
    F = w_gu.shape[1] // 2
    tm = _pick(M, (768, 512, 256, 128))
    tn = _col_tile(F, 1408)
    nb = F // tn

    def body(b_ref, wg_ref, wu_ref, g_ref, u_ref, act_ref):
        x = b_ref[...]
        g = _dot(x, wg_ref[...])
        u = _dot(x, wu_ref[...])
        g_ref[...] = g
        u_ref[...] = u
        act_ref[...] = ((g * jax.nn.sigmoid(g)) * u).astype(act_ref.dtype)

    blk = pl.BlockSpec((tm, tn), lambda n, m: (m, n))
    return pl.pallas_call(
        body, name=name, grid=(nb, M // tm),
        in_specs=[pl.BlockSpec((tm, K), lambda n, m: (m, 0)),
                  pl.BlockSpec((K, tn), lambda n, m: (0, n)),
                  pl.BlockSpec((K, tn), lambda n, m: (0, n + nb))],
        out_specs=[blk, blk, blk],
        out_shape=[jax.ShapeDtypeStruct((M, F), F32), jax.ShapeDtypeStruct((M, F), F32),
                   jax.ShapeDtypeStruct((M, F), BF16)],
        compiler_params=_params(("parallel", "parallel")))(b, w_gu, w_gu)


def _ffn_dact(dy, w_d, g, u, name):
    M, K = dy.shape
    F = w_d.shape[0]
    tm = _pick(M, (768, 512, 256, 128))
    tn = _col_tile(F, 1408)
    nb = F // tn

    def body(dy_ref, wd_ref, g_ref, u_ref, dg_ref, du_ref):
        dact = _dot_nt(dy_ref[...].astype(BF16), wd_ref[...])
        gv = g_ref[...]
        s = jax.nn.sigmoid(gv)
        silu = gv * s
        dg_ref[...] = (dact * u_ref[...] * (s * (1.0 + gv * (1.0 - s)))).astype(dg_ref.dtype)
        du_ref[...] = (dact * silu).astype(du_ref.dtype)

    blk = pl.BlockSpec((tm, tn), lambda n, m: (m, n))
    dg, du = pl.pallas_call(
        body, name=name, grid=(nb, M // tm),
        in_specs=[pl.BlockSpec((tm, K), lambda n, m: (m, 0)), pl.BlockSpec((tn, K), lambda n, m: (n, 0)), blk, blk],
        out_specs=[blk, blk],
        out_shape=[jax.ShapeDtypeStruct((M, F), BF16), jax.ShapeDtypeStruct((M, F), BF16)],
        compiler_params=_params(("parallel", "parallel")))(dy, w_d, g, u)
    return jnp.concatenate([dg, du], axis=1)


def _loss_head(h, gain, target, name):
    M, C = h.shape
    tm = ROW0
    assert M % tm == 0 and target.shape[0] == M - ROW0

    def body(h_ref, g_ref, t_ref, sq_ref, dh_ref, dg_ref):
        i = pl.program_id(0)

        @pl.when(i == 0)
        def _():
            sq_ref[...] = jnp.zeros_like(sq_ref)
            dg_ref[...] = jnp.zeros_like(dg_ref)
            dh_ref[...] = jnp.zeros_like(dh_ref)

        @pl.when(i > 0)
        def _():
            x = h_ref[...]
            r = lax.rsqrt(jnp.mean(x * x, axis=-1, keepdims=True) + EPS)
            y = x * r
            err = y * g_ref[...] - t_ref[...]
            sq_ref[...] += jnp.sum(err * err, axis=0, keepdims=True)
            da = err * (1.0 / C)
            dy = da * g_ref[...]
            dh_ref[...] = r * (dy - y * jnp.mean(dy * y, axis=-1, keepdims=True))
            dg_ref[...] += jnp.sum(da * y, axis=0, keepdims=True)

    row = pl.BlockSpec((tm, C), lambda m: (m, 0))
    vec = pl.BlockSpec((1, C), lambda m: (0, 0))
    return pl.pallas_call(
        body, name=name, grid=(M // tm,),
        in_specs=[row, vec, pl.BlockSpec((tm, C), lambda m: (jnp.maximum(m - 1, 0), 0))],
        out_specs=[vec, row, vec],
        out_shape=[jax.ShapeDtypeStruct((1, C), F32), jax.ShapeDtypeStruct((M, C), F32),
                   jax.ShapeDtypeStruct((1, C), F32)],
        compiler_params=_params(("arbitrary",)))(h, gain, target)


def _pool_pos(row0, tm):
    return row0 + lax.broadcasted_iota(jnp.int32, (tm, 1), 0) - PAD


def _pool_fwd(h, a, w, scale, name):
    M, C = a.shape
    tm = 256
    hb = tm // HALO

    def body(h_ref, a_ref, halo_ref, w_ref, s_ref, o_ref, p_ref):
        i = pl.program_id(0)
        row0 = i * tm
        ext = jnp.concatenate([halo_ref[...], a_ref[...]], axis=0)
        src = row0 - HALO + lax.broadcasted_iota(jnp.int32, (tm + HALO, 1), 0)
        ext = jnp.where(src >= PAD, ext, 0.0)
        r = lax.broadcasted_iota(jnp.int32, (tm, tm + HALO), 0)
        c = lax.broadcasted_iota(jnp.int32, (tm, tm + HALO), 1)
        pos = _pool_pos(row0, tm)
        for g, win in enumerate(POOL_WINDOWS):
            band = ((c <= r + HALO) & (c > r + HALO - win)).astype(F32)
            cols = slice(g * POOL_GROUP, (g + 1) * POOL_GROUP)
            xg = ext[:, cols]
            tot = jnp.dot(band, xg, precision=lax.Precision.HIGHEST, preferred_element_type=F32)
            cnt = jnp.clip(pos + 1, 1, win).astype(F32)
            pooled = (tot / cnt - xg[HALO:]).astype(BF16)
            p_ref[:, cols] = pooled
            mixed = _dot(pooled, w_ref[g])
            o_ref[:, cols] = h_ref[:, cols] + mixed * s_ref[:, cols]

    row = pl.BlockSpec((tm, C), lambda m: (m, 0))
    return pl.pallas_call(
        body, name=name, grid=(M // tm,),
        in_specs=[row, row, pl.BlockSpec((HALO, C), lambda m: (jnp.maximum(m * hb - 1, 0), 0)),
                  pl.BlockSpec((4, POOL_GROUP, POOL_GROUP), lambda m: (0, 0, 0)),
                  pl.BlockSpec((1, C), lambda m: (0, 0))],
        out_specs=[row, row],
        out_shape=[jax.ShapeDtypeStruct((M, C), F32), jax.ShapeDtypeStruct((M, C), BF16)],
        compiler_params=_params(("parallel",)))(h, a, a, w, scale)


def _pool_bwd_mix(dout, pooled, w, scale, name):
    M, C = dout.shape
    tm = 256

    def body(do_ref, p_ref, w_ref, s_ref, dpc_ref, dw_ref, ds_ref):
        i = pl.program_id(0)

        @pl.when(i == 0)
        def _():
            dw_ref[...] = jnp.zeros_like(dw_ref)
            ds_ref[...] = jnp.zeros_like(ds_ref)

        pos = _pool_pos(i * tm, tm)
        for g, win in enumerate(POOL_WINDOWS):
            cols = slice(g * POOL_GROUP, (g + 1) * POOL_GROUP)
            do = do_ref[:, cols]
            pooled = p_ref[:, cols]
            mixed = _dot(pooled, w_ref[g])
            ds_ref[:, cols] += jnp.sum(do * mixed, axis=0, keepdims=True)
            dmix = (do * s_ref[:, cols]).astype(BF16)
            dw_ref[g] += _dot_tn(pooled, dmix)
            dp = _dot_nt(dmix, w_ref[g])
            cnt = jnp.clip(pos + 1, 1, win).astype(F32)
            dpc_ref[:, cols] = dp / cnt

    row = pl.BlockSpec((tm, C), lambda m: (m, 0))
    wspec = pl.BlockSpec((4, POOL_GROUP, POOL_GROUP), lambda m: (0, 0, 0))
    vec = pl.BlockSpec((1, C), lambda m: (0, 0))
    return pl.pallas_call(
        body, name=name, grid=(M // tm,),
        in_specs=[row, row, wspec, vec], out_specs=[row, wspec, vec],
        out_shape=[jax.ShapeDtypeStruct((M, C), F32), jax.ShapeDtypeStruct((4, POOL_GROUP, POOL_GROUP), F32),
                   jax.ShapeDtypeStruct((1, C), F32)],
        compiler_params=_params(("arbitrary",)))(dout, pooled, w, scale)


def _pool_bwd_window(dpc, name):
    M, C = dpc.shape
    tm = 256
    hb = tm // HALO
    last = M // HALO - 1

    def body(d_ref, halo_ref, da_ref):
        i = pl.program_id(0)
        row0 = i * tm
        ext = jnp.concatenate([d_ref[...], halo_ref[...]], axis=0)
        src = row0 + lax.broadcasted_iota(jnp.int32, (tm + HALO, 1), 0)
        ext = jnp.where(src < M, ext, 0.0)
        r = lax.broadcasted_iota(jnp.int32, (tm, tm + HALO), 0)
        c = lax.broadcasted_iota(jnp.int32, (tm, tm + HALO), 1)
        pos = _pool_pos(row0, tm)
        for g, win in enumerate(POOL_WINDOWS):
            band = ((c >= r) & (c < r + win)).astype(F32)
            cols = slice(g * POOL_GROUP, (g + 1) * POOL_GROUP)
            xg = ext[:, cols]
            tot = jnp.dot(band, xg, precision=lax.Precision.HIGHEST, preferred_element_type=F32)
            cnt = jnp.clip(pos + 1, 1, win).astype(F32)
            da_ref[:, cols] = jnp.where(pos >= 0, tot - xg[:tm] * cnt, 0.0)

    row = pl.BlockSpec((tm, C), lambda m: (m, 0))
    return pl.pallas_call(
        body, name=name, grid=(M // tm,),
        in_specs=[row, pl.BlockSpec((HALO, C), lambda m: (jnp.minimum((m + 1) * hb, last), 0))],
        out_specs=row, out_shape=jax.ShapeDtypeStruct((M, C), F32),
        compiler_params=_params(("parallel",)))(dpc, dpc)


def _head_masks():
    lane = lax.broadcasted_iota(jnp.int32, (1, 128), 1)
    return lane < HEAD_DIM, lane


def _split_heads(x, first):
    z = jnp.zeros_like(x)
    return jnp.where(first, x, z), jnp.where(first, z, x)


def _split_rope(x, lane):
    z = jnp.zeros_like(x)
    return jnp.where(lane < MLA_ROPE, x, z), jnp.where((lane >= MLA_ROPE) & (lane < 2 * MLA_ROPE), x, z)


def _mla_fwd(q_all, kv_all, qr, kr, scale, name):
    M = q_all.shape[0]
    t = ATTN_TILE

    def body(q_ref, k_ref, v_ref, qr_ref, kr_ref, o_ref, lse_ref, m_s, l_s, acc_s):
        i = pl.program_id(1)
        first, lane = _head_masks()
        qs = _split_heads(q_ref[...], first)
        qrs = _split_rope(qr_ref[...], lane)
        qcat = tuple(jnp.concatenate([qs[hh], qrs[hh]], axis=1) for hh in range(2))
        ones = jnp.ones((t, 128), BF16)
        m_s[...] = jnp.full_like(m_s, NEG)
        l_s[...] = jnp.zeros_like(l_s)
        acc_s[...] = jnp.zeros_like(acc_s)
        qpos = i * t + lax.broadcasted_iota(jnp.int32, (t, t), 0)
        kidx = lax.broadcasted_iota(jnp.int32, (t, t), 1)

        def step(kb, carry):
            k0 = pl.multiple_of(kb * t, t)
            kcat = jnp.concatenate([k_ref[pl.ds(k0, t), :], kr_ref[pl.ds(k0, t), :]], axis=1)
            vs = _split_heads(v_ref[pl.ds(k0, t), :], first)
            kpos = k0 + kidx
            valid = (kpos <= qpos) & (kpos >= PAD)
            pv = None
            alphas = []
            for hh in range(2):
                s = _dot_nt(qcat[hh], kcat) * scale
                s = jnp.where(valid, s, NEG)
                m_old = m_s[hh]
                m_new = jnp.maximum(m_old, jnp.max(s, axis=1, keepdims=True))
                pb = jnp.exp(s - jnp.tile(m_new, (1, t // 128))).astype(BF16)
                alpha = jnp.exp(m_old - m_new)
                l_s[hh] = alpha * l_s[hh] + _dot(pb, ones)
                m_s[hh] = m_new
                d = _dot(pb, vs[hh])
                pv = d if pv is None else pv + d
                alphas.append(alpha)
            acc_s[...] = acc_s[...] * jnp.where(first, alphas[0], alphas[1]) + pv
            return carry

        lax.fori_loop(0, i + 1, step, 0)
        o_ref[...] = (acc_s[...] * jnp.where(first, 1.0 / l_s[0], 1.0 / l_s[1])).astype(o_ref.dtype)
        lse_ref[:, 0:1] = (m_s[0] + jnp.log(l_s[0]))[:, 0:1]
        lse_ref[:, 1:2] = (m_s[1] + jnp.log(l_s[1]))[:, 0:1]

    blk = pl.BlockSpec((t, 128), lambda j, i: (i, j))
    return pl.pallas_call(
        body, name=name, grid=(N_PAIRS, M // t),
        in_specs=[blk, pl.BlockSpec((M, 128), lambda j, i: (0, j)), pl.BlockSpec((M, 128), lambda j, i: (0, N_PAIRS + j)),
                  blk, pl.BlockSpec((M, 128), lambda j, i: (0, 0))],
        out_specs=[blk, pl.BlockSpec((None, t, 2), lambda j, i: (j, i, 0))],
        out_shape=[jax.ShapeDtypeStruct((M, N_PAIRS * 128), BF16), jax.ShapeDtypeStruct((N_PAIRS, M, 2), F32)],
        scratch_shapes=[pltpu.VMEM((2, t, 128), F32), pltpu.VMEM((2, t, 128), F32), pltpu.VMEM((t, 128), F32)],
        compiler_params=_params(("parallel", "arbitrary")))(q_all, kv_all, kv_all, qr, kr)


def _mla_bwd(q_all, kv_all, qr, kr, o, do, lse, scale, name):
    M = q_all.shape[0]
    t = ATTN_TILE

    def body(q_ref, k_ref, v_ref, qr_ref, kr_ref, o_ref, do_ref, lse_ref,
             dq_ref, dk_ref, dv_ref, dqr_ref, dkr_ref, dq_s, lse_s, delta_s):
        i = pl.program_id(1)
        first, lane = _head_masks()

        @pl.when(i == 0)
        def _():
            dk_ref[...] = jnp.zeros_like(dk_ref)
            dv_ref[...] = jnp.zeros_like(dv_ref)
            dkr_ref[...] = jnp.zeros_like(dkr_ref)

        qs = _split_heads(q_ref[...], first)
        qrs = _split_rope(qr_ref[...], lane)
        qcat = tuple(jnp.concatenate([qs[hh], qrs[hh]], axis=1) for hh in range(2))
        dov = do_ref[...]
        dos = _split_heads(dov, first)
        prod = dov.astype(F32) * o_ref[...].astype(F32)
        deltas = (jnp.sum(jnp.where(first, prod, 0.0), axis=1, keepdims=True),
                  jnp.sum(jnp.where(first, 0.0, prod), axis=1, keepdims=True))
        for hh in range(2):
            lse_s[hh] = jnp.broadcast_to(lse_ref[:, hh:hh + 1], (t, t))
            delta_s[hh] = jnp.broadcast_to(deltas[hh], (t, t))
        dq_s[...] = jnp.zeros_like(dq_s)
        qpos = i * t + lax.broadcasted_iota(jnp.int32, (t, t), 0)
        kidx = lax.broadcasted_iota(jnp.int32, (t, t), 1)

        def step(kb, carry):
            k0 = pl.multiple_of(kb * t, t)
            rows = pl.ds(k0, t)
            k = k_ref[rows, :]
            v = v_ref[rows, :]
            kr = kr_ref[rows, :]
            kcat = jnp.concatenate([k, kr], axis=1)
            ks = _split_heads(k, first)
            krs = _split_rope(kr, lane)
            kpos = k0 + kidx
            valid = (kpos <= qpos) & (kpos >= PAD)
            dq = dk = dv = None
            for hh in range(2):
                s = _dot_nt(qcat[hh], kcat) * scale
                s = jnp.where(valid, s, NEG)
                p = jnp.exp(s - lse_s[hh])
                ds = p * (_dot_nt(dos[hh], v) - delta_s[hh])
                dsb = (ds * scale).astype(BF16)
                a = _dot(dsb, jnp.concatenate([ks[hh], krs[hh]], axis=1))
                b = _dot_tn(dsb, qcat[hh])
                c = _dot_tn(p.astype(BF16), dos[hh])
                dq = a if dq is None else dq + a
                dk = b if dk is None else dk + b
                dv = c if dv is None else dv + c
            dq_s[...] += dq
            dk_ref[rows, :] += dk[:, :128]
            dkr_ref[rows, :] += dk[:, 128:]
            dv_ref[rows, :] += dv
            return carry

        lax.fori_loop(0, i + 1, step, 0)
        dq_ref[...] = dq_s[:, :128].astype(dq_ref.dtype)
        dqr_ref[...] = dq_s[:, 128:].astype(dqr_ref.dtype)

    blk = pl.BlockSpec((t, 128), lambda j, i: (i, j))
    col = pl.BlockSpec((M, 128), lambda j, i: (0, j))
    wide = jax.ShapeDtypeStruct((M, N_PAIRS * 128), F32)
    return pl.pallas_call(
        body, name=name, grid=(N_PAIRS, M // t),
        in_specs=[blk, col, pl.BlockSpec((M, 128), lambda j, i: (0, N_PAIRS + j)), blk,
                  pl.BlockSpec((M, 128), lambda j, i: (0, 0)), blk, blk,
                  pl.BlockSpec((None, t, 2), lambda j, i: (j, i, 0))],
        out_specs=[blk, col, col, blk, pl.BlockSpec((None, M, 128), lambda j, i: (j, 0, 0))],
        out_shape=[jax.ShapeDtypeStruct((M, N_PAIRS * 128), BF16), wide, wide,
                   jax.ShapeDtypeStruct((M, N_PAIRS * 128), BF16), jax.ShapeDtypeStruct((N_PAIRS, M, 128), F32)],
        scratch_shapes=[pltpu.VMEM((t, 256), F32), pltpu.VMEM((2, t, t), F32), pltpu.VMEM((2, t, t), F32)],
        compiler_params=_params(("parallel", "arbitrary")))(q_all, kv_all, kv_all, qr, kr, o, do, lse)


def _tri(t, rel):
    j = lax.broadcasted_iota(jnp.int32, (t, t), 0)
    k = lax.broadcasted_iota(jnp.int32, (t, t), 1)
    m = {"gt": j > k, "le": j <= k, "lt": j < k}[rel]
    return m.astype(BF16)


def _lane_cumsum(x, tri):
    hi = x.astype(BF16)
    lo = (x - hi.astype(F32)).astype(BF16)
    return _dot(hi, tri) + _dot(lo, tri)


def _log_sigmoids(z):
    sp = jnp.log(1.0 + jnp.exp(-jnp.abs(z)))
    return jnp.minimum(z, 0.0) - sp, jnp.minimum(-z, 0.0) - sp


def _sb_fwd(qkv, scale, name):
    M = qkv.shape[0]
    t = WALK_TILE
    ck, cv = N_PAIRS, 2 * N_PAIRS

    def body(q_ref, k_ref, v_ref, o_ref, tot_ref, c_s, acc_s):
        i = pl.program_id(1)
        first, _ = _head_masks()
        qs = _split_heads(q_ref[...], first)
        c_s[...] = jnp.zeros_like(c_s)
        acc_s[...] = jnp.zeros_like(acc_s)
        tri = _tri(t, "gt")
        qpos = i * t + lax.broadcasted_iota(jnp.int32, (t, t), 0)
        kidx = lax.broadcasted_iota(jnp.int32, (t, t), 1)

        def step(it):
            k0 = pl.multiple_of((i - it) * t, t)
            k = k_ref[pl.ds(k0, t), :]
            vs = _split_heads(v_ref[pl.ds(k0, t), :], first)
            kpos = k0 + kidx
            valid = (kpos < qpos) & (kpos >= PAD)
            pv = None
            for hh in range(2):
                z = _dot_nt(qs[hh], k) * scale
                lb, lkr = _log_sigmoids(z)
                lk = jnp.where(valid, lkr, 0.0)
                later = c_s[hh] + _lane_cumsum(lk, tri)
                a = jnp.where(valid, jnp.exp(lb + later), 0.0)
                c_s[hh] = c_s[hh] + jnp.sum(lk, axis=1, keepdims=True)
                d = _dot(a.astype(BF16), vs[hh])
                pv = d if pv is None else pv + d
            acc_s[...] += pv

        def cond(carry):
            it, go = carry
            return (it <= i) & go

        def walk(carry):
            it, _ = carry
            step(it)
            return it + 1, jnp.max(jnp.maximum(c_s[0], c_s[1])) > EXP_ZERO

        walked, _ = lax.while_loop(cond, walk, (jnp.int32(0), True))
        o_ref[...] = acc_s[...].astype(o_ref.dtype)
        tot_ref[:, 0:1] = c_s[0]
        tot_ref[:, 1:2] = c_s[1]
        tot_ref[:, 2:3] = jnp.full((t, 1), walked.astype(F32))

    whole = lambda c0: pl.BlockSpec((M, 128), lambda j, i: (0, c0 + j))
    return pl.pallas_call(
        body, name=name, grid=(N_PAIRS, M // t),
        in_specs=[pl.BlockSpec((t, 128), lambda j, i: (i, j)), whole(ck), whole(cv)],
        out_specs=[pl.BlockSpec((t, 128), lambda j, i: (i, j)), pl.BlockSpec((None, t, 3), lambda j, i: (j, i, 0))],
        out_shape=[jax.ShapeDtypeStruct((M, N_PAIRS * 128), BF16), jax.ShapeDtypeStruct((N_PAIRS, M, 3), F32)],
        scratch_shapes=[pltpu.VMEM((2, t, 1), F32), pltpu.VMEM((t, 128), F32)],
        compiler_params=_params(("parallel", "arbitrary")))(qkv, qkv, qkv)


def _sb_bwd(qkv, do, tot, scale, name):
    M = qkv.shape[0]
    t = WALK_TILE
    ck, cv = N_PAIRS, 2 * N_PAIRS

    def body(q_ref, k_ref, v_ref, do_ref, tot_ref, dq_ref, dk_ref, dv_ref, pc_s, dc_s, dq_s):
        i = pl.program_id(1)
        first, _ = _head_masks()

        @pl.when(i == 0)
        def _():
            dk_ref[...] = jnp.zeros_like(dk_ref)
            dv_ref[...] = jnp.zeros_like(dv_ref)

        qs = _split_heads(q_ref[...], first)
        dos = _split_heads(do_ref[...], first)
        pc_s[...] = jnp.zeros_like(pc_s)
        dc_s[...] = jnp.zeros_like(dc_s)
        dq_s[...] = jnp.zeros_like(dq_s)
        tri_le = _tri(t, "le")
        tri_lt = _tri(t, "lt")
        qpos = i * t + lax.broadcasted_iota(jnp.int32, (t, t), 0)
        kidx = lax.broadcasted_iota(jnp.int32, (t, t), 1)

        def step(kb, carry):
            k0 = pl.multiple_of(kb * t, t)
            rows = pl.ds(k0, t)
            k = k_ref[rows, :]
            v = v_ref[rows, :]
            ks = _split_heads(k, first)
            kpos = k0 + kidx
            valid = (kpos < qpos) & (kpos >= PAD)
            dq = dk = dv = None
            for hh in range(2):
                z = _dot_nt(qs[hh], k) * scale
                lb, lkr = _log_sigmoids(z)
                lk = jnp.where(valid, lkr, 0.0)
                later = tot_ref[:, hh:hh + 1] - (pc_s[hh] + _lane_cumsum(lk, tri_le))
                a = jnp.where(valid, jnp.exp(lb + later), 0.0)
                dl = a * _dot_nt(dos[hh], v)
                early = dc_s[hh] + _lane_cumsum(dl, tri_lt)
                sg = jnp.exp(lb)
                dz = jnp.where(valid, dl * (1.0 - sg) - early * sg, 0.0) * scale
                pc_s[hh] = pc_s[hh] + jnp.sum(lk, axis=1, keepdims=True)
                dc_s[hh] = dc_s[hh] + jnp.sum(dl, axis=1, keepdims=True)
                dzb = dz.astype(BF16)
                x = _dot(dzb, ks[hh])
                y = _dot_tn(dzb, qs[hh])
                w = _dot_tn(a.astype(BF16), dos[hh])
                dq = x if dq is None else dq + x
                dk = y if dk is None else dk + y
                dv = w if dv is None else dv + w
            dq_s[...] += dq
            dk_ref[rows, :] += dk
            dv_ref[rows, :] += dv
            return carry

        walked = jnp.max(tot_ref[:, 2:3]).astype(jnp.int32)
        lax.fori_loop(i + 1 - walked, i + 1, step, 0)
        dq_ref[...] = dq_s[...].astype(dq_ref.dtype)

    whole = lambda c0: pl.BlockSpec((M, 128), lambda j, i: (0, c0 + j))
    blk = pl.BlockSpec((t, 128), lambda j, i: (i, j))
    col = pl.BlockSpec((M, 128), lambda j, i: (0, j))
    return pl.pallas_call(
        body, name=name, grid=(N_PAIRS, M // t),
        in_specs=[blk, whole(ck), whole(cv), blk, pl.BlockSpec((None, t, 3), lambda j, i: (j, i, 0))],
        out_specs=[blk, col, col],
        out_shape=[jax.ShapeDtypeStruct((M, N_PAIRS * 128), BF16), jax.ShapeDtypeStruct((M, N_PAIRS * 128), F32),
                   jax.ShapeDtypeStruct((M, N_PAIRS * 128), F32)],
        scratch_shapes=[pltpu.VMEM((2, t, 1), F32), pltpu.VMEM((2, t, 1), F32), pltpu.VMEM((t, 128), F32)],
        compiler_params=_params(("parallel", "arbitrary")))(qkv, qkv, qkv, do, tot)


def _rows_between(lo, hi):
    r = lax.broadcasted_iota(jnp.int32, (128, 1), 0)
    return (r >= lo) & (r < hi)


def _lanes_between(lo, hi):
    c = lax.broadcasted_iota(jnp.int32, (1, 128), 1)
    return (c >= lo) & (c < hi)


def _keep(x, mask):
    return jnp.where(mask, x, jnp.zeros_like(x))


def _valid_mask(i, kb, t):
    kpos = kb * t + lax.broadcasted_iota(jnp.int32, (t, t), 0)
    qpos = i * t + lax.broadcasted_iota(jnp.int32, (t, t), 1)
    return (kpos <= qpos) & (kpos >= PAD)


def _fox_fwd(qkv, qkv_t, f_rows, f_cols, scale, name):
    M = qkv.shape[0]
    t = WALK_TILE
    first_blk = PAD // t
    ck, cv = N_PAIRS, 2 * N_PAIRS

    def body(qt_ref, k_ref, vt_ref, fq_ref, fk_ref, o_ref, lse_ref, ox_ref, m_s, l_s, acc_s, accx_s, kmax_s):
        i = pl.program_id(1)

        @pl.when(i == 0)
        def _():
            first = _lanes_between(0, 64)

            def block_max(kb, carry):
                kk = k_ref[pl.ds(pl.multiple_of(kb * t, t), t), :].astype(F32)
                kk = kk * kk
                a = jnp.max(jnp.sum(jnp.where(first, kk, 0.0), axis=1, keepdims=True))
                b = jnp.max(jnp.sum(jnp.where(first, 0.0, kk), axis=1, keepdims=True))
                return jnp.maximum(carry[0], a), jnp.maximum(carry[1], b)

            a, b = lax.fori_loop(0, M // t, block_max, (jnp.float32(0.0), jnp.float32(0.0)))
            kmax_s[0] = a
            kmax_s[1] = b

        qt = qt_ref[...]
        qts = (_keep(qt, _rows_between(0, 64)), _keep(qt, _rows_between(64, 128)))
        qf = qt.astype(F32)
        qf = qf * qf
        qbound = tuple(
            (1.001 * scale) * jnp.sqrt(jnp.sum(qf[HEAD_DIM * hh:HEAD_DIM * (hh + 1)], axis=0, keepdims=True) * kmax_s[hh])
            for hh in range(2))
        m_s[...] = jnp.full_like(m_s, NEG)
        l_s[...] = jnp.zeros_like(l_s)
        acc_s[...] = jnp.zeros_like(acc_s)
        accx_s[...] = jnp.zeros_like(accx_s)

        def step(kb, masked):
            k0 = pl.multiple_of(kb * t, t)
            rows = pl.ds(k0, t)
            k = k_ref[rows, :]
            if masked:
                valid = _valid_mask(i, kb, t)
            for hh in range(2):
                s = _dot(k, qts[hh]) * scale + (fq_ref[hh:hh + 1, :] - fk_ref[rows, hh:hh + 1])
                if masked:
                    s = jnp.where(valid, s, NEG)
                m_old = m_s[hh]
                m_new = jnp.maximum(m_old, jnp.max(s, axis=0, keepdims=True))
                p = jnp.exp(s - m_new)
                alpha = jnp.exp(m_old - m_new)
                l_s[hh] = alpha * l_s[hh] + jnp.sum(p, axis=0, keepdims=True)
                m_s[hh] = m_new
                pb = p.astype(BF16)
                hr = slice(HEAD_DIM * hh, HEAD_DIM * (hh + 1))
                vt = vt_ref[hr, rows]
                acc_s[hr, :] = acc_s[hr, :] * alpha + _dot(vt, pb)
                accx_s[hr, :] = accx_s[hr, :] * alpha + _dot(vt, (p - pb.astype(F32)).astype(BF16))

        def keep_going(kb):
            k0 = pl.multiple_of(kb * t, t)
            worst = None
            for hh in range(2):
                f0 = jnp.max(fk_ref[pl.ds(k0, 8), hh:hh + 1])
                w = jnp.max(qbound[hh] + (fq_ref[hh:hh + 1, :] - f0) - m_s[hh])
                worst = w if worst is None else jnp.maximum(worst, w)
            return worst > EXP_ZERO

        def cond(carry):
            kb, go, _ = carry
            return (kb > first_blk) & go

        def walk(carry):
            kb, _, n = carry
            step(kb, False)
            return kb - 1, keep_going(kb), n + 1

        step(i, True)
        _, go, n = lax.while_loop(cond, walk, (i - 1, keep_going(i), jnp.int32(1)))
        first_too = go & (i > first_blk)

        @pl.when(first_too)
        def _():
            step(first_blk, True)

        walked = n + first_too.astype(jnp.int32)
        for hh in range(2):
            hr = slice(HEAD_DIM * hh, HEAD_DIM * (hh + 1))
            inv = 1.0 / l_s[hh]
            o_ref[hr, :] = (acc_s[hr, :] * inv).astype(o_ref.dtype)
            ox_ref[hr, :] = (acc_s[hr, :] + accx_s[hr, :]) * inv
            lse_ref[hh:hh + 1, :] = m_s[hh] + jnp.log(l_s[hh])
        lse_ref[2:3, :] = jnp.full((1, t), walked.astype(F32))

    blk = pl.BlockSpec((128, t), lambda j, i: (j, i))
    stat = pl.BlockSpec((None, 2, t), lambda j, i: (j, 0, i))
    return pl.pallas_call(
        body, name=name, grid=(N_PAIRS, M // t),
        in_specs=[blk, pl.BlockSpec((M, 128), lambda j, i: (0, ck + j)), pl.BlockSpec((128, M), lambda j, i: (cv + j, 0)),
                  stat, pl.BlockSpec((None, M, 2), lambda j, i: (j, 0, 0))],
        out_specs=[blk, pl.BlockSpec((None, 3, t), lambda j, i: (j, 0, i)), blk],
        out_shape=[jax.ShapeDtypeStruct((N_PAIRS * 128, M), BF16), jax.ShapeDtypeStruct((N_PAIRS, 3, M), F32),
                   jax.ShapeDtypeStruct((N_PAIRS * 128, M), F32)],
        scratch_shapes=[pltpu.VMEM((2, 1, t), F32), pltpu.VMEM((2, 1, t), F32), pltpu.VMEM((128, t), F32),
                        pltpu.VMEM((128, t), F32), pltpu.SMEM((2,), F32)],
        compiler_params=_params(("parallel", "arbitrary")))(qkv_t, qkv, qkv_t, f_rows, f_cols)


def _fox_bwd(qkv, qkv_t, o_t, do, do_t, lse, f_rows, f_cols, scale, name):
    M = qkv.shape[0]
    t = WALK_TILE
    first_blk = PAD // t
    ck, cv = N_PAIRS, 2 * N_PAIRS

    def body(q_ref, qt_ref, k_ref, kt_ref, v_ref, ot_ref, do_ref, dot_ref, lse_ref, fq_ref, fk_ref,
             dq_ref, dk_ref, dv_ref, cs_ref, dq_s):
        i = pl.program_id(1)

        @pl.when(i == 0)
        def _():
            dk_ref[...] = jnp.zeros_like(dk_ref)
            dv_ref[...] = jnp.zeros_like(dv_ref)
            cs_ref[...] = jnp.zeros_like(cs_ref)

        heads_l = (_lanes_between(0, 64), _lanes_between(64, 128))
        heads_r = (_rows_between(0, 64), _rows_between(64, 128))
        q = q_ref[...]
        qt = qt_ref[...]
        do = do_ref[...]
        dot = dot_ref[...]
        qs = tuple(_keep(q, m) for m in heads_l)
        qts = tuple(_keep(qt, m) for m in heads_r)
        dos = tuple(_keep(do, m) for m in heads_l)
        dots = tuple(_keep(dot, m) for m in heads_r)
        prod = dot.astype(F32) * ot_ref[...]
        deltas = tuple(jnp.sum(prod[HEAD_DIM * hh:HEAD_DIM * (hh + 1)], axis=0, keepdims=True) for hh in range(2))
        ones = tuple(m.astype(BF16) * jnp.ones((t, 128), BF16) for m in heads_l)
        dq_s[...] = jnp.zeros_like(dq_s)

        def step(kb, masked):
            k0 = pl.multiple_of(kb * t, t)
            rows = pl.ds(k0, t)
            k = k_ref[rows, :]
            v = v_ref[rows, :]
            if masked:
                valid = _valid_mask(i, kb, t)
            dk = dv = cs = None
            for hh in range(2):
                s = _dot(k, qts[hh]) * scale + (fq_ref[hh:hh + 1, :] - fk_ref[rows, hh:hh + 1])
                if masked:
                    s = jnp.where(valid, s, NEG)
                p = jnp.exp(s - lse_ref[hh:hh + 1, :])
                ds = p * (_dot(v, dots[hh]) - deltas[hh])
                hi = ds.astype(BF16)
                lo = (ds - hi.astype(F32)).astype(BF16)
                c = _dot(hi, ones[hh]) + _dot(lo, ones[hh])
                dsb = (ds * scale).astype(BF16)
                hr = slice(HEAD_DIM * hh, HEAD_DIM * (hh + 1))
                dq_s[hr, :] += _dot(kt_ref[hr, rows], dsb)
                a = _dot(dsb, qs[hh])
                b = _dot(p.astype(BF16), dos[hh])
                dk = a if dk is None else dk + a
                dv = b if dv is None else dv + b
                cs = c if cs is None else cs + c
            dk_ref[rows, :] += dk
            dv_ref[rows, :] += dv
            cs_ref[rows, :] += cs

        first_walked = i + 1 - jnp.max(lse_ref[2:3, :]).astype(jnp.int32)

        def mid(kb, carry):
            step(kb, False)
            return carry

        @pl.when((first_walked == first_blk) & (i > first_blk))
        def _():
            step(first_blk, True)

        lax.fori_loop(jnp.maximum(first_walked, first_blk + 1), i, mid, 0)
        step(i, True)
        dq_ref[...] = dq_s[...].astype(dq_ref.dtype)

    rblk = pl.BlockSpec((t, 128), lambda j, i: (i, j))
    tblk = pl.BlockSpec((128, t), lambda j, i: (j, i))
    stat = pl.BlockSpec((None, 2, t), lambda j, i: (j, 0, i))
    stat3 = pl.BlockSpec((None, 3, t), lambda j, i: (j, 0, i))
    col = pl.BlockSpec((M, 128), lambda j, i: (0, j))
    wide = jax.ShapeDtypeStruct((M, N_PAIRS * 128), F32)
    return pl.pallas_call(
        body, name=name, grid=(N_PAIRS, M // t),
        in_specs=[rblk, tblk, pl.BlockSpec((M, 128), lambda j, i: (0, ck + j)),
                  pl.BlockSpec((128, M), lambda j, i: (ck + j, 0)), pl.BlockSpec((M, 128), lambda j, i: (0, cv + j)),
                  tblk, rblk, tblk, stat3, stat, pl.BlockSpec((None, M, 2), lambda j, i: (j, 0, 0))],
        out_specs=[tblk, col, col, pl.BlockSpec((None, M, 128), lambda j, i: (j, 0, 0))],
        out_shape=[jax.ShapeDtypeStruct((N_PAIRS * 128, M), BF16), wide, wide,
                   jax.ShapeDtypeStruct((N_PAIRS, M, 128), F32)],
        scratch_shapes=[pltpu.VMEM((128, t), F32)],
        compiler_params=_params(("parallel", "arbitrary")))(qkv, qkv_t, qkv, qkv_t, qkv, o_t, do, do_t, lse,
                                                            f_rows, f_cols)


def _rope_tables(M):
    pos = (jnp.arange(M, dtype=jnp.int32) - PAD).astype(F32)
    inv = ROPE_THETA ** (-jnp.arange(0, MLA_ROPE, 2, dtype=F32) / MLA_ROPE)
    ang = pos[:, None] * inv[None, :]
    cos, sin = jnp.cos(ang), jnp.sin(ang)
    z = jnp.zeros((M, 64), F32)
    cos_t = jnp.concatenate([cos, cos, cos, cos, z], axis=1)
    sin_t = jnp.concatenate([-sin, sin, -sin, sin, z], axis=1)
    return cos_t, sin_t


def _rope(x, cos_t, sin_t, out_dtype, name, inverse=False, lead=0):
    M, C = x.shape
    tm = _pick(M, (768, 512, 256, 128))
    nblk = (C - lead) // 128
    sign = -1.0 if inverse else 1.0

    def body(x_ref, c_ref, s_ref, o_ref):
        lane = lax.broadcasted_iota(jnp.int32, (1, 128), 1)
        low = (lane % MLA_ROPE) < (MLA_ROPE // 2)
        cos = c_ref[...]
        sin = s_ref[...] * sign
        if lead:
            o_ref[:, :lead] = x_ref[:, :lead].astype(o_ref.dtype)
        for b in range(nblk):
            cols = slice(lead + b * 128, lead + (b + 1) * 128)
            v = x_ref[:, cols].astype(F32)
            up = pltpu.roll(v, 128 - MLA_ROPE // 2, 1)
            down = pltpu.roll(v, MLA_ROPE // 2, 1)
            o_ref[:, cols] = (v * cos + jnp.where(low, up, down) * sin).astype(o_ref.dtype)

    row = pl.BlockSpec((tm, C), lambda m: (m, 0))
    tab = pl.BlockSpec((tm, 128), lambda m: (m, 0))
    return pl.pallas_call(
        body, name=name, grid=(M // tm,), in_specs=[row, tab, tab], out_specs=row,
        out_shape=jax.ShapeDtypeStruct((M, C), out_dtype),
        compiler_params=_params(("parallel",)))(x, cos_t, sin_t)


def _forget_cumsum(f_logit, bias, name):
    M = f_logit.shape[0]
    tm = 256

    def body(f_ref, b_ref, o_ref, c_s):
        i = pl.program_id(0)

        @pl.when(i == 0)
        def _():
            c_s[...] = jnp.zeros_like(c_s)
        ls, _ = _log_sigmoids(f_ref[...] + b_ref[...])
        rows = i * tm + lax.broadcasted_iota(jnp.int32, (tm, 1), 0)
        ls = jnp.where(rows >= PAD, ls, 0.0)
        r = lax.broadcasted_iota(jnp.int32, (tm, tm), 0)
        c = lax.broadcasted_iota(jnp.int32, (tm, tm), 1)
        tri = (c <= r).astype(F32)
        cum = jnp.dot(tri, ls, precision=lax.Precision.HIGHEST, preferred_element_type=F32) + c_s[...]
        o_ref[...] = cum
        c_s[...] = cum[tm - 1:tm, :]

    row = pl.BlockSpec((tm, 128), lambda m: (m, 0))
    return pl.pallas_call(
        body, name=name, grid=(M // tm,),
        in_specs=[row, pl.BlockSpec((1, 128), lambda m: (0, 0))], out_specs=row,
        out_shape=jax.ShapeDtypeStruct((M, 128), F32), scratch_shapes=[pltpu.VMEM((1, 128), F32)],
        compiler_params=_params(("arbitrary",)))(f_logit, bias)


def _forget_cumsum_bwd(f_logit, bias, dF, name):
    M = f_logit.shape[0]
    tm = 256
    nb = M // tm

    def body(f_ref, b_ref, d_ref, o_ref, db_ref, c_s):
        i = pl.program_id(0)

        @pl.when(i == 0)
        def _():
            c_s[...] = jnp.zeros_like(c_s)
            db_ref[...] = jnp.zeros_like(db_ref)
        r = lax.broadcasted_iota(jnp.int32, (tm, tm), 0)
        c = lax.broadcasted_iota(jnp.int32, (tm, tm), 1)
        tri = (c >= r).astype(F32)
        cum = jnp.dot(tri, d_ref[...], precision=lax.Precision.HIGHEST, preferred_element_type=F32) + c_s[...]
        c_s[...] = cum[0:1, :]
        _, lsn = _log_sigmoids(f_ref[...] + b_ref[...])
        rows = (nb - 1 - i) * tm + lax.broadcasted_iota(jnp.int32, (tm, 1), 0)
        dl = jnp.where(rows >= PAD, cum * jnp.exp(lsn), 0.0)
        o_ref[...] = dl
        db_ref[...] += jnp.sum(dl, axis=0, keepdims=True)

    row = pl.BlockSpec((tm, 128), lambda m: (nb - 1 - m, 0))
    vec = pl.BlockSpec((1, 128), lambda m: (0, 0))
    return pl.pallas_call(
        body, name=name, grid=(nb,), in_specs=[row, vec, row], out_specs=[row, vec],
        out_shape=[jax.ShapeDtypeStruct((M, 128), F32), jax.ShapeDtypeStruct((1, 128), F32)],
        scratch_shapes=[pltpu.VMEM((1, 128), F32)],
        compiler_params=_params(("arbitrary",)))(f_logit, bias, dF)


def _adamw(w, parts, m, v, name):
    R, C = w.shape
    tr = R
    for d in range(8, R, 8):
        if R % d == 0 and d * C <= ADAM_TILE_ELEMS:
            tr = d
    c1 = 1.0 - ADAM_B1 ** ADAM_STEP
    c2 = 1.0 - ADAM_B2 ** ADAM_STEP

    def body(w_ref, s_ref, m_ref, v_ref, g_ref, d_ref, mo_ref, vo_ref):
        g = s_ref[0].astype(F32)
        for k in range(1, N_DEV):
            g = g + s_ref[k].astype(F32)
        mn = ADAM_B1 * m_ref[...] + (1.0 - ADAM_B1) * g
        vn = ADAM_B2 * v_ref[...] + (1.0 - ADAM_B2) * (g * g)
        m_hat = mn / c1
        v_hat = vn / c2
        g_ref[...] = g
        d_ref[...] = -ADAM_LR * (m_hat / (jnp.sqrt(v_hat) + ADAM_EPS) + ADAM_WD * w_ref[...])
        mo_ref[...] = mn
        vo_ref[...] = vn

    row = pl.BlockSpec((tr, C), lambda r: (r, 0))
    shp = jax.ShapeDtypeStruct((R, C), F32)
    return pl.pallas_call(
        body, name=name, grid=(R // tr,),
        in_specs=[row, pl.BlockSpec((N_DEV, tr, C), lambda r: (0, r, 0)), row, row],
        out_specs=[row, row, row, row], out_shape=[shp, shp, shp, shp],
        compiler_params=_params(("parallel",)))(w, parts, m, v)


def _position():
    return lax.axis_index("x"), lax.axis_index("y"), lax.axis_index("c")


def _all_gather(blocks, name):
    n = len(blocks)

    def body(*refs):
        x_refs, out_refs = refs[:n], refs[n:2 * n]
        send_sems, recv_sems, local_sems = refs[2 * n:]
        x, y, c = _position()
        me, sibling = (x, y, c), (x, y, 1 - c)
        chips = [(1 - x, y), (x, 1 - y), (1 - x, 1 - y)]

        def copies(k, block, to, own=False):
            slot = 4 * block[0] + 2 * block[1] + block[2]
            return [pltpu.make_async_remote_copy(
                src_ref=x_refs[p] if own else out_refs[p].at[slot], dst_ref=out_refs[p].at[slot],
                send_sem=send_sems.at[k, p], recv_sem=recv_sems.at[k, p], device_id=to, device_id_type=MESH)
                for p in range(n)]

        mine = [pltpu.make_async_copy(x_refs[p], out_refs[p].at[4 * x + 2 * y + c], local_sems.at[p]) for p in range(n)]
        for cp in mine:
            cp.start()
        first = copies(0, me, sibling, own=True)
        for j, chip in enumerate(chips):
            first += copies(1 + j, me, (*chip, c), own=True)
        for cp in first:
            cp.start()
        passed = []
        for j, chip in enumerate(chips):
            for cp in copies(1 + j, (*chip, c), me):
                cp.wait_recv()
            onward = copies(4 + j, (*chip, c), sibling)
            for cp in onward:
                cp.start()
            passed += onward
        for cp in copies(0, sibling, me):
            cp.wait_recv()
        for j, chip in enumerate(chips):
            for cp in copies(4 + j, (*chip, 1 - c), me):
                cp.wait_recv()
        for cp in first + passed:
            cp.wait_send()
        for cp in mine:
            cp.wait()

    any_spec = pl.BlockSpec(memory_space=pl.ANY)
    return pl.pallas_call(
        body, name=name, out_shape=[jax.ShapeDtypeStruct((N_DEV,) + b.shape, b.dtype) for b in blocks],
        in_specs=[any_spec] * n, out_specs=[any_spec] * n,
        scratch_shapes=[pltpu.SemaphoreType.DMA((7, n)), pltpu.SemaphoreType.DMA((7, n)), pltpu.SemaphoreType.DMA((n,))],
    )(*blocks)


def _exchange(parts, name):
    n = len(parts)

    def body(*refs):
        g_refs, land_refs = refs[:n], refs[n:2 * n]
        send_sems, recv_sems, local_sems = refs[2 * n:]
        x, y, c = _position()
        me = 4 * x + 2 * y + c
        mine = [pltpu.make_async_copy(g_refs[p].at[me], land_refs[p].at[me], local_sems.at[p]) for p in range(n)]
        for cp in mine:
            cp.start()
        sends, recvs = [], []
        for k in range(1, N_DEV):
            px = 1 - x if k & 4 else x
            py = 1 - y if k & 2 else y
            pc = 1 - c if k & 1 else c
            peer = 4 * px + 2 * py + pc
            for p in range(n):
                sends.append(pltpu.make_async_remote_copy(
                    src_ref=g_refs[p].at[peer], dst_ref=land_refs[p].at[me], send_sem=send_sems.at[k - 1, p],
                    recv_sem=recv_sems.at[k - 1, p], device_id=(px, py, pc), device_id_type=MESH))
                recvs.append(pltpu.make_async_remote_copy(
                    src_ref=g_refs[p].at[me], dst_ref=land_refs[p].at[peer], send_sem=send_sems.at[k - 1, p],
                    recv_sem=recv_sems.at[k - 1, p], device_id=(px, py, pc), device_id_type=MESH))
        for cp in sends:
            cp.start()
        for cp in recvs:
            cp.wait_recv()
        for cp in sends:
            cp.wait_send()
        for cp in mine:
            cp.wait()

    any_spec = pl.BlockSpec(memory_space=pl.ANY)
    return pl.pallas_call(
        body, name=name, out_shape=[jax.ShapeDtypeStruct(p.shape, p.dtype) for p in parts],
        in_specs=[any_spec] * n, out_specs=[any_spec] * n,
        scratch_shapes=[pltpu.SemaphoreType.DMA((7, n)), pltpu.SemaphoreType.DMA((7, n)), pltpu.SemaphoreType.DMA((n,))],
    )(*parts)


SHARDED = (("sb_w_qkv", 2), ("sb_w_o", 1), ("mla_w_down", 1), ("mla_w_uq", 2), ("mla_w_ukv", 2), ("mla_w_o", 1),
           ("fox_w_qkvf", 2), ("fox_w_o", 1), ("ffn_w_gate", 2), ("ffn_w_up", 2), ("ffn_w_down", 1),
           ("pool_w", 2), ("meta", 1), ("mla_q_norm", 1), ("mla_kv_norm", 1))
KEPT_F32 = ("meta", "mla_q_norm", "mla_kv_norm")
REPLICATED = ("norm_mix", "norm_ffn", "pool_scale", "fox_b_f", "final_norm")
WEIGHT_NAMES = ("meta", "norm_mix", "norm_ffn", "pool_w", "pool_scale", "sb_w_qkv", "sb_w_o", "mla_w_down",
                "mla_q_norm", "mla_kv_norm", "mla_w_uq", "mla_w_ukv", "mla_w_o", "fox_w_qkvf", "fox_b_f",
                "fox_w_o", "ffn_w_gate", "ffn_w_up", "ffn_w_down", "final_norm")
LANES = 1024


def _pack_rows(arrays, names):
    parts = []
    for n in names:
        flat = arrays[n].reshape(-1).astype(F32)
        rows = -(-flat.shape[0] // LANES)
        parts.append(jnp.pad(flat, (0, rows * LANES - flat.shape[0])).reshape(rows, LANES))
    rows = sum(p.shape[0] for p in parts)
    parts.append(jnp.zeros((-(-rows // 8) * 8 - rows, LANES), F32))
    return jnp.concatenate(parts, axis=0)


def _unpack_rows(buf, shapes, names):
    out, row = {}, 0
    for n in names:
        size = int(np.prod(shapes[n]))
        rows = -(-size // LANES)
        out[n] = buf[row:row + rows].reshape(-1)[:size].reshape(shapes[n])
        row += rows
    return out


def _whole_from_gathered(g, axis):
    g = jnp.moveaxis(g, 0, axis)
    shp = g.shape
    return g.reshape(shp[:axis] + (shp[axis] * shp[axis + 1],) + shp[axis + 2:])


def _parts_from_whole(whole, axis):
    shp = whole.shape
    g = whole.reshape(shp[:axis] + (N_DEV, shp[axis] // N_DEV) + shp[axis + 1:])
    return jnp.moveaxis(g, axis, 0)


def _kernel_weights(full):
    W = {}
    W["pool_w"] = full["pool_w"][0]
    W["sb_w_qkv"] = full["sb_w_qkv"][0]
    W["sb_w_o"] = full["sb_w_o"][0]
    W["mla_w_down"] = full["mla_w_down"][0]
    uq = full["mla_w_uq"][0].reshape(MLA_Q_RANK, N_HEADS, MLA_NOPE + MLA_ROPE)
    nope = uq[:, :, :MLA_NOPE].reshape(MLA_Q_RANK, N_HEADS * MLA_NOPE)
    rope = uq[:, :, MLA_NOPE:].reshape(MLA_Q_RANK, N_PAIRS, 2 * MLA_ROPE)
    rope = jnp.pad(rope, ((0, 0), (0, 0), (0, 128 - 2 * MLA_ROPE))).reshape(MLA_Q_RANK, N_PAIRS * 128)
    W["mla_w_uq"] = jnp.concatenate([nope, rope], axis=1)
    ukv = full["mla_w_ukv"][0].reshape(MLA_KV_RANK, N_HEADS, 2, HEAD_DIM)
    W["mla_w_ukv"] = jnp.transpose(ukv, (0, 2, 1, 3)).reshape(MLA_KV_RANK, 2 * N_HEADS * HEAD_DIM)
    W["mla_w_o"] = full["mla_w_o"][0]
    qkvf = full["fox_w_qkvf"][0]
    n_qkv = 3 * N_HEADS * HEAD_DIM
    W["fox_w_qkv"] = qkvf[:, :n_qkv]
    W["fox_w_f"] = jnp.pad(qkvf[:, n_qkv:], ((0, 0), (0, 128 - N_HEADS)))
    W["fox_w_qkvf"] = jnp.concatenate([W["fox_w_qkv"], W["fox_w_f"]], axis=1)
    W["fox_w_o"] = full["fox_w_o"][0]
    W["ffn_w_gu"] = jnp.concatenate([full["ffn_w_gate"], full["ffn_w_up"]], axis=2)
    W["ffn_w_down"] = full["ffn_w_down"]
    return W


def _reference_grads(G):
    out = {}
    out["pool_w"] = G["pool_w"][None]
    for n in ("sb_w_qkv", "sb_w_o", "mla_w_down", "mla_w_o", "fox_w_o"):
        out[n] = G[n][None]
    duq = G["mla_w_uq"]
    nope = duq[:, :N_HEADS * MLA_NOPE].reshape(MLA_Q_RANK, N_HEADS, MLA_NOPE)
    rope = duq[:, N_HEADS * MLA_NOPE:].reshape(MLA_Q_RANK, N_PAIRS, 128)[:, :, :2 * MLA_ROPE]
    rope = rope.reshape(MLA_Q_RANK, N_HEADS, MLA_ROPE)
    out["mla_w_uq"] = jnp.concatenate([nope, rope], axis=2).reshape(1, MLA_Q_RANK, -1)
    dukv = G["mla_w_ukv"].reshape(MLA_KV_RANK, 2, N_HEADS, HEAD_DIM)
    out["mla_w_ukv"] = jnp.transpose(dukv, (0, 2, 1, 3)).reshape(1, MLA_KV_RANK, -1)
    out["fox_w_qkvf"] = G["fox_w_qkvf"][None, :, :3 * N_HEADS * HEAD_DIM + N_HEADS]
    out["ffn_w_gate"] = G["ffn_w_gu"][:, :, :D_FF]
    out["ffn_w_up"] = G["ffn_w_gu"][:, :, D_FF:]
    out["ffn_w_down"] = G["ffn_w_down"]
    out["mla_q_norm"] = G["mla_q_norm"]
    out["mla_kv_norm"] = G["mla_kv_norm"]
    return out


def _pairs_col(f16):
    M = f16.shape[0]
    return jnp.transpose(f16.reshape(M, N_PAIRS, 2), (1, 0, 2))


def _pairs_row(f16):
    M = f16.shape[0]
    return jnp.transpose(f16.reshape(M, N_PAIRS, 2), (1, 2, 0))


def _local_step(x, target, W, P):
    S = x.shape[0]
    M = S + ROW0
    G = {}
    gain = lambda name, i: P[name][i][None, :]
    h0 = jnp.concatenate([jnp.zeros((PAD, D_MODEL), F32), P["meta"], x], axis=0)

    def ffn_fwd(h1, i):
        b = _norm_fwd(h1, gain("norm_ffn", i), BF16, f"ffn{i}_norm")
        g, u, act = _ffn_up(b, W["ffn_w_gu"][i], f"ffn{i}_up")
        h2 = _mm_nn(act, W["ffn_w_down"][i], F32, f"ffn{i}_down", res=h1)
        return h2, (h1, b, g, u, act)

    def ffn_bwd(dh2, saved, i):
        h1, b, g, u, act = saved
        dgu = _ffn_dact(dh2, W["ffn_w_down"][i], g, u, f"ffn{i}_dact")
        G.setdefault("ffn_w_down", {})[i] = _mm_tn(act, dh2, f"ffn{i}_dwd")
        db = _mm_nt(dgu, W["ffn_w_gu"][i], F32, f"ffn{i}_db")
        G.setdefault("ffn_w_gu", {})[i] = _mm_tn(b, dgu, f"ffn{i}_dwgu")
        dh1, dgain = _norm_bwd(h1, gain("norm_ffn", i), db, dh2, f"ffn{i}_dnorm")
        G.setdefault("norm_ffn", {})[i] = dgain
        return dh1

    a0 = _norm_fwd(h0, gain("norm_mix", 0), F32, "mix0_norm")
    h1_0, pooled = _pool_fwd(h0, a0, W["pool_w"], P["pool_scale"], "pool_fwd")
    h_1, ffn0 = ffn_fwd(h1_0, 0)

    sb_scale = HEAD_DIM ** -0.5
    a1 = _norm_fwd(h_1, gain("norm_mix", 1), BF16, "mix1_norm")
    sb_qkv = _mm_nn(a1, W["sb_w_qkv"], BF16, "sb_qkv")
    sb_o, sb_tot = _sb_fwd(sb_qkv, sb_scale, "sb_fwd")
    h1_1 = _mm_nn(sb_o, W["sb_w_o"], F32, "sb_out", res=h_1)
    h_2, ffn1 = ffn_fwd(h1_1, 1)

    mla_scale = (MLA_NOPE + MLA_ROPE) ** -0.5
    cos_t, sin_t = _rope_tables(M)
    a2 = _norm_fwd(h_2, gain("norm_mix", 2), BF16, "mix2_norm")
    down = _mm_nn(a2, W["mla_w_down"], F32, "mla_down")
    dq_raw = down[:, :MLA_Q_RANK]
    dkv_raw = down[:, MLA_Q_RANK:MLA_Q_RANK + MLA_KV_RANK]
    kr_raw = down[:, MLA_Q_RANK + MLA_KV_RANK:]
    c_q = _norm_fwd(dq_raw, P["mla_q_norm"], BF16, "mla_qnorm")
    c_kv = _norm_fwd(dkv_raw, P["mla_kv_norm"], BF16, "mla_kvnorm")
    q_lin = _mm_nn(c_q, W["mla_w_uq"], F32, "mla_uq")
    q_all = _rope(q_lin, cos_t, sin_t, BF16, "mla_qrope", lead=D_MODEL)
    kv_all = _mm_nn(c_kv, W["mla_w_ukv"], BF16, "mla_ukv")
    kr_in = jnp.concatenate([kr_raw, kr_raw, jnp.zeros((M, 64), F32)], axis=1)
    kr = _rope(kr_in, cos_t, sin_t, BF16, "mla_krope")
    q_rope = q_all[:, D_MODEL:]
    mla_o, mla_lse = _mla_fwd(q_all, kv_all, q_rope, kr, mla_scale, "mla_fwd")
    h1_2 = _mm_nn(mla_o, W["mla_w_o"], F32, "mla_out", res=h_2)
    h_3, ffn2 = ffn_fwd(h1_2, 2)

    fox_scale = HEAD_DIM ** -0.5
    a3 = _norm_fwd(h_3, gain("norm_mix", 3), BF16, "mix3_norm")
    fox_qkv = _mm_nn(a3, W["fox_w_qkv"], BF16, "fox_qkv")
    f_logit = _mm_nn(a3, W["fox_w_f"], F32, "fox_f")
    b_f = jnp.pad(P["fox_b_f"], ((0, 0), (0, 128 - N_HEADS)))
    Fc = _forget_cumsum(f_logit, b_f, "fox_cumsum")
    f_rows, f_cols = _pairs_row(Fc[:, :N_HEADS]), _pairs_col(Fc[:, :N_HEADS])
    fox_qkv_t = fox_qkv.T
    fox_o_t, fox_lse, fox_ox_t = _fox_fwd(fox_qkv, fox_qkv_t, f_rows, f_cols, fox_scale, "fox_fwd")
    fox_o = fox_o_t.T
    h1_3 = _mm_nn(fox_o, W["fox_w_o"], F32, "fox_out", res=h_3)
    h_4, ffn3 = ffn_fwd(h1_3, 3)

    sq, dh, dgain = _loss_head(h_4, P["final_norm"][None, :], target, "loss_head")
    G["final_norm"] = dgain[0]

    dh = ffn_bwd(dh, ffn3, 3)
    do = _mm_nt(dh, W["fox_w_o"], BF16, "fox_do")
    G["fox_w_o"] = _mm_tn(fox_o, dh, "fox_dwo")
    dq_t, dk, dv, colsum = _fox_bwd(fox_qkv, fox_qkv_t, fox_ox_t, do, do.T, fox_lse, f_rows, f_cols, fox_scale,
                                    "fox_bwd")
    dF = -jnp.transpose(colsum[:, :, ::HEAD_DIM], (1, 0, 2)).reshape(M, N_HEADS)
    dF = jnp.pad(dF, ((0, 0), (0, 128 - N_HEADS)))
    dlogit, db_f = _forget_cumsum_bwd(f_logit, b_f, dF, "fox_dcumsum")
    G["fox_b_f"] = db_f[:, :N_HEADS]
    dproj = jnp.concatenate([dq_t.T, dk.astype(BF16), dv.astype(BF16), dlogit.astype(BF16)], axis=1)
    da = _mm_nt(dproj, W["fox_w_qkvf"], F32, "fox_da")
    G["fox_w_qkvf"] = _mm_tn(a3, dproj, "fox_dwqkvf")
    dh, dgain = _norm_bwd(h_3, gain("norm_mix", 3), da, dh, "mix3_dnorm")
    G.setdefault("norm_mix", {})[3] = dgain

    dh = ffn_bwd(dh, ffn2, 2)
    do = _mm_nt(dh, W["mla_w_o"], BF16, "mla_do")
    G["mla_w_o"] = _mm_tn(mla_o, dh, "mla_dwo")
    dq, dk, dv, dqr, dkr = _mla_bwd(q_all, kv_all, q_rope, kr, mla_o, do, mla_lse, mla_scale, "mla_bwd")
    dqr = _rope(dqr, cos_t, sin_t, BF16, "mla_dqrope", inverse=True)
    dq_all = jnp.concatenate([dq, dqr], axis=1)
    dkr_sum = _rope(jnp.sum(dkr, axis=0), cos_t, sin_t, F32, "mla_dkrope", inverse=True)
    dkr_raw = dkr_sum[:, :MLA_ROPE] + dkr_sum[:, MLA_ROPE:2 * MLA_ROPE]
    dkv_all = jnp.concatenate([dk.astype(BF16), dv.astype(BF16)], axis=1)
    dc_q = _mm_nt(dq_all, W["mla_w_uq"], F32, "mla_dcq")
    G["mla_w_uq"] = _mm_tn(c_q, dq_all, "mla_dwuq")
    dc_kv = _mm_nt(dkv_all, W["mla_w_ukv"], F32, "mla_dckv")
    G["mla_w_ukv"] = _mm_tn(c_kv, dkv_all, "mla_dwukv")
    ddq_raw, G["mla_q_norm"] = _norm_bwd(dq_raw, P["mla_q_norm"], dc_q, None, "mla_dqnorm")
    ddkv_raw, G["mla_kv_norm"] = _norm_bwd(dkv_raw, P["mla_kv_norm"], dc_kv, None, "mla_dkvnorm")
    ddown = jnp.concatenate([ddq_raw, ddkv_raw, dkr_raw], axis=1).astype(BF16)
    da = _mm_nt(ddown, W["mla_w_down"], F32, "mla_da")
    G["mla_w_down"] = _mm_tn(a2, ddown, "mla_dwdown")
    dh, dgain = _norm_bwd(h_2, gain("norm_mix", 2), da, dh, "mix2_dnorm")
    G["norm_mix"][2] = dgain

    dh = ffn_bwd(dh, ffn1, 1)
    do = _mm_nt(dh, W["sb_w_o"], BF16, "sb_do")
    G["sb_w_o"] = _mm_tn(sb_o, dh, "sb_dwo")
    dq, dk, dv = _sb_bwd(sb_qkv, do, sb_tot, sb_scale, "sb_bwd")
    dqkv = jnp.concatenate([dq, dk.astype(BF16), dv.astype(BF16)], axis=1)
    da = _mm_nt(dqkv, W["sb_w_qkv"], F32, "sb_da")
    G["sb_w_qkv"] = _mm_tn(a1, dqkv, "sb_dwqkv")
    dh, dgain = _norm_bwd(h_1, gain("norm_mix", 1), da, dh, "mix1_dnorm")
    G["norm_mix"][1] = dgain

    dh = ffn_bwd(dh, ffn0, 0)
    dpc, G["pool_w"], G["pool_scale"] = _pool_bwd_mix(dh, pooled, W["pool_w"], P["pool_scale"], "pool_dmix")
    da = _pool_bwd_window(dpc, "pool_dwindow")
    dh, dgain, dx = _norm_bwd(h0, gain("norm_mix", 0), da, dh, "mix0_dnorm", token_rows=True)
    G["norm_mix"][0] = dgain

    G["norm_mix"] = jnp.concatenate([G["norm_mix"][i] for i in range(DEPTH)], axis=0)
    G["norm_ffn"] = jnp.concatenate([G["norm_ffn"][i] for i in range(DEPTH)], axis=0)
    G["ffn_w_down"] = jnp.stack([G["ffn_w_down"][i] for i in range(DEPTH)])
    G["ffn_w_gu"] = jnp.stack([G["ffn_w_gu"][i] for i in range(DEPTH)])
    G["meta"] = dh[PAD:ROW0]
    return sq, dx, G


def kernel(x, meta, norm_mix, norm_ffn, pool_w, pool_scale, sb_w_qkv, sb_w_o, mla_w_down, mla_q_norm, mla_kv_norm, mla_w_uq, mla_w_ukv, mla_w_o, fox_w_qkvf, fox_b_f, fox_w_o, ffn_w_gate, ffn_w_up, ffn_w_down, final_norm, loss_target, m_meta, m_norm_mix, m_norm_ffn, m_pool_w, m_pool_scale, m_sb_w_qkv, m_sb_w_o, m_mla_w_down, m_mla_q_norm, m_mla_kv_norm, m_mla_w_uq, m_mla_w_ukv, m_mla_w_o, m_fox_w_qkvf, m_fox_b_f, m_fox_w_o, m_ffn_w_gate, m_ffn_w_up, m_ffn_w_down, m_final_norm, v_meta, v_norm_mix, v_norm_ffn, v_pool_w, v_pool_scale, v_sb_w_qkv, v_sb_w_o, v_mla_w_down, v_mla_q_norm, v_mla_kv_norm, v_mla_w_uq, v_mla_w_ukv, v_mla_w_o, v_fox_w_qkvf, v_fox_b_f, v_fox_w_o, v_ffn_w_gate, v_ffn_w_up, v_ffn_w_down, v_final_norm):
    w = dict(meta=meta, norm_mix=norm_mix, norm_ffn=norm_ffn, pool_w=pool_w, pool_scale=pool_scale,
             sb_w_qkv=sb_w_qkv, sb_w_o=sb_w_o, mla_w_down=mla_w_down, mla_q_norm=mla_q_norm,
             mla_kv_norm=mla_kv_norm, mla_w_uq=mla_w_uq, mla_w_ukv=mla_w_ukv, mla_w_o=mla_w_o,
             fox_w_qkvf=fox_w_qkvf, fox_b_f=fox_b_f, fox_w_o=fox_w_o, ffn_w_gate=ffn_w_gate, ffn_w_up=ffn_w_up,
             ffn_w_down=ffn_w_down, final_norm=final_norm)
    m = dict(meta=m_meta, norm_mix=m_norm_mix, norm_ffn=m_norm_ffn, pool_w=m_pool_w, pool_scale=m_pool_scale,
             sb_w_qkv=m_sb_w_qkv, sb_w_o=m_sb_w_o, mla_w_down=m_mla_w_down, mla_q_norm=m_mla_q_norm,
             mla_kv_norm=m_mla_kv_norm, mla_w_uq=m_mla_w_uq, mla_w_ukv=m_mla_w_ukv, mla_w_o=m_mla_w_o,
             fox_w_qkvf=m_fox_w_qkvf, fox_b_f=m_fox_b_f, fox_w_o=m_fox_w_o, ffn_w_gate=m_ffn_w_gate,
             ffn_w_up=m_ffn_w_up, ffn_w_down=m_ffn_w_down, final_norm=m_final_norm)
    v = dict(meta=v_meta, norm_mix=v_norm_mix, norm_ffn=v_norm_ffn, pool_w=v_pool_w, pool_scale=v_pool_scale,
             sb_w_qkv=v_sb_w_qkv, sb_w_o=v_sb_w_o, mla_w_down=v_mla_w_down, mla_q_norm=v_mla_q_norm,
             mla_kv_norm=v_mla_kv_norm, mla_w_uq=v_mla_w_uq, mla_w_ukv=v_mla_w_ukv, mla_w_o=v_mla_w_o,
             fox_w_qkvf=v_fox_w_qkvf, fox_b_f=v_fox_b_f, fox_w_o=v_fox_w_o, ffn_w_gate=v_ffn_w_gate,
             ffn_w_up=v_ffn_w_up, ffn_w_down=v_ffn_w_down, final_norm=v_final_norm)

    sh_names = tuple(n for n, _ in SHARDED)
    sh_axis = dict(SHARDED)
    shapes = {n: w[n].shape for n in WEIGHT_NAMES}
    wire = lambda n: F32 if n in KEPT_F32 else BF16

    gathered = _all_gather([w[n].astype(wire(n)) for n in sh_names], "gather_weights")
    full = {n: _whole_from_gathered(g, sh_axis[n]) for n, g in zip(sh_names, gathered)}
    W = _kernel_weights(full)
    P = dict(meta=full["meta"], mla_q_norm=full["mla_q_norm"], mla_kv_norm=full["mla_kv_norm"],
             norm_mix=norm_mix, norm_ffn=norm_ffn, pool_scale=pool_scale, fox_b_f=fox_b_f, final_norm=final_norm)

    sq, dx, G = _local_step(x[0], loss_target[0], W, P)
    loss = lax.psum(0.5 * jnp.sum(sq) / D_MODEL, ("x", "y", "c"))
    grad_x = dx[None]

    gw = _reference_grads(G)
    gw["meta"] = G["meta"]
    parts = [_parts_from_whole(gw[n], sh_axis[n]).astype(wire(n)) for n in sh_names]
    landed = _exchange(parts, "exchange_grads")
    results = {}
    for n, got in zip(sh_names, landed):
        rc = (int(np.prod(shapes[n][:-1])), shapes[n][-1])
        outs = _adamw(w[n].reshape(rc), got.reshape((N_DEV,) + rc), m[n].reshape(rc), v[n].reshape(rc), f"adamw_{n}")
        results[n] = [o.reshape(shapes[n]) for o in outs]

    rep_g = dict(norm_mix=G["norm_mix"], norm_ffn=G["norm_ffn"], pool_scale=G["pool_scale"], fox_b_f=G["fox_b_f"],
                 final_norm=G["final_norm"])
    (rep_all,) = _all_gather([_pack_rows(rep_g, REPLICATED)], "gather_replicated_grads")
    rep_out = _adamw(_pack_rows(w, REPLICATED), rep_all, _pack_rows(m, REPLICATED), _pack_rows(v, REPLICATED),
                     "adamw_replicated")
    rep = [_unpack_rows(o, shapes, REPLICATED) for o in rep_out]
    for n in REPLICATED:
        results[n] = [r[n] for r in rep]

    outs = [results[n][k] for k in range(4) for n in WEIGHT_NAMES]
    return (loss, grad_x, *outs)
```

```python
import numpy as np
import jax
import jax.numpy as jnp
from jax import lax
from jax.experimental import pallas as pl
from jax.experimental.pallas import tpu as pltpu

F32 = jnp.float32
BF16 = jnp.bfloat16

N_DEV = 8
D_MODEL = 1024
N_META = 16
PAD = 240
ROW0 = PAD + N_META
EPS = 1e-6
POOL_WINDOWS = (2, 4, 8, 16)
POOL_GROUP = 256
HALO = 128
N_HEADS = 16
HEAD_DIM = 64
N_PAIRS = N_HEADS // 2
MLA_Q_RANK = 384
MLA_KV_RANK = 256
MLA_NOPE = 64
MLA_ROPE = 32
ROPE_THETA = 10000.0
D_FF = 2816
DEPTH = 4
ATTN_TILE = 768
ATTN_BWD_TILE = 384
WALK_TILE = 256
NEG = -1e30
EXP_ZERO = -110.0
VMEM_LIMIT = 56 * 2**20
ADAM_TILE_ELEMS = 192 * 1024

ADAM_LR = 0.001
ADAM_B1 = 0.9
ADAM_B2 = 0.999
ADAM_EPS = 1e-08
ADAM_WD = 0.01
ADAM_STEP = 10

MESH = pl.DeviceIdType.MESH


def _params(sem=None):
    return pltpu.CompilerParams(dimension_semantics=sem, vmem_limit_bytes=VMEM_LIMIT)


def _pick(n, cands):
    for c in cands:
        if n % c == 0:
            return c
    return n


def _col_tile(n, cap=1536):
    best = None
    for t in range(128, min(n, cap) + 1, 128):
        if n % t == 0:
            best = t
    return best if best is not None else n


def _dot(a, b):
    return jnp.dot(a, b, preferred_element_type=F32)


def _dot_nt(a, b):
    return lax.dot_general(a, b, (((1,), (1,)), ((), ())), preferred_element_type=F32)


def _dot_tn(a, b):
    return lax.dot_general(a, b, (((0,), (0,)), ((), ())), preferred_element_type=F32)


def _mm_nn(a, b, out_dtype, name, res=None):
    M, K = a.shape
    N = b.shape[1]
    tm = _pick(M, (768, 512, 256, 128))
    tn = _col_tile(N)

    def body(*refs):
        if res is None:
            a_ref, b_ref, o_ref = refs
        else:
            a_ref, b_ref, r_ref, o_ref = refs
        acc = _dot(a_ref[...].astype(BF16), b_ref[...])
        if res is not None:
            acc = acc + r_ref[...]
        o_ref[...] = acc.astype(o_ref.dtype)

    in_specs = [pl.BlockSpec((tm, K), lambda n, m: (m, 0)), pl.BlockSpec((K, tn), lambda n, m: (0, n))]
    args = [a, b]
    if res is not None:
        in_specs.append(pl.BlockSpec((tm, tn), lambda n, m: (m, n)))
        args.append(res)
    return pl.pallas_call(
        body, name=name, grid=(N // tn, M // tm), in_specs=in_specs,
        out_specs=pl.BlockSpec((tm, tn), lambda n, m: (m, n)),
        out_shape=jax.ShapeDtypeStruct((M, N), out_dtype),
        compiler_params=_params(("parallel", "parallel")))(*args)


def _mm_nt(a, w, out_dtype, name):
    M, N = a.shape
    K = w.shape[0]
    tm = _pick(M, (768, 512, 256, 128)) if N <= 3200 else _pick(M, (256, 128))
    tk = _col_tile(K, 1024)

    def body(a_ref, w_ref, o_ref):
        o_ref[...] = _dot_nt(a_ref[...].astype(BF16), w_ref[...]).astype(o_ref.dtype)

    return pl.pallas_call(
        body, name=name, grid=(K // tk, M // tm),
        in_specs=[pl.BlockSpec((tm, N), lambda k, m: (m, 0)), pl.BlockSpec((tk, N), lambda k, m: (k, 0))],
        out_specs=pl.BlockSpec((tm, tk), lambda k, m: (m, k)),
        out_shape=jax.ShapeDtypeStruct((M, K), out_dtype),
        compiler_params=_params(("parallel", "parallel")))(a, w)


def _mm_tn(a, b, name):
    M, K = a.shape
    N = b.shape[1]
    tm = _pick(M, (768, 512, 256, 128))
    tk = _col_tile(K, 1408)
    tn = _col_tile(N, 1408)

    def body(a_ref, b_ref, o_ref):
        @pl.when(pl.program_id(2) == 0)
        def _():
            o_ref[...] = jnp.zeros_like(o_ref)
        o_ref[...] += _dot_tn(a_ref[...].astype(BF16), b_ref[...].astype(BF16))

    return pl.pallas_call(
        body, name=name, grid=(K // tk, N // tn, M // tm),
        in_specs=[pl.BlockSpec((tm, tk), lambda k, n, m: (m, k)), pl.BlockSpec((tm, tn), lambda k, n, m: (m, n))],
        out_specs=pl.BlockSpec((tk, tn), lambda k, n, m: (k, n)),
        out_shape=jax.ShapeDtypeStruct((K, N), F32),
        compiler_params=_params(("parallel", "parallel", "arbitrary")))(a, b)


def _norm_fwd(h, gain, out_dtype, name):
    M, C = h.shape
    tm = _pick(M, (768, 512, 256, 128))

    def body(h_ref, g_ref, a_ref):
        x = h_ref[...]
        r = lax.rsqrt(jnp.mean(x * x, axis=-1, keepdims=True) + EPS)
        a_ref[...] = ((x * r) * g_ref[...]).astype(a_ref.dtype)

    return pl.pallas_call(
        body, name=name, grid=(M // tm,),
        in_specs=[pl.BlockSpec((tm, C), lambda m: (m, 0)), pl.BlockSpec((1, C), lambda m: (0, 0))],
        out_specs=pl.BlockSpec((tm, C), lambda m: (m, 0)),
        out_shape=jax.ShapeDtypeStruct((M, C), out_dtype),
        compiler_params=_params(("parallel",)))(h, gain)


def _norm_bwd(h, gain, da, dres, name, token_rows=False):
    M, C = h.shape
    tm = ROW0 if token_rows else _pick(M, (768, 512, 256, 128))

    def body(*refs):
        refs = list(refs)
        dx_ref = refs.pop() if token_rows else None
        if dres is None:
            h_ref, g_ref, da_ref, dh_ref, dg_ref = refs
        else:
            h_ref, g_ref, da_ref, dr_ref, dh_ref, dg_ref = refs
        x = h_ref[...]
        r = lax.rsqrt(jnp.mean(x * x, axis=-1, keepdims=True) + EPS)
        y = x * r
        dav = da_ref[...].astype(F32)
        dy = dav * g_ref[...]
        dh = r * (dy - y * jnp.mean(dy * y, axis=-1, keepdims=True))
        if dres is not None:
            dh = dh + dr_ref[...]
        dh_ref[...] = dh
        if token_rows:
            dx_ref[...] = dh

        @pl.when(pl.program_id(0) == 0)
        def _():
            dg_ref[...] = jnp.zeros_like(dg_ref)
        dg_ref[...] += jnp.sum(dav * y, axis=0, keepdims=True)

    row = pl.BlockSpec((tm, C), lambda m: (m, 0))
    vec = pl.BlockSpec((1, C), lambda m: (0, 0))
    in_specs = [row, vec, row] + ([row] if dres is not None else [])
    args = [h, gain, da] + ([dres] if dres is not None else [])
    out_specs = [row, vec]
    out_shape = [jax.ShapeDtypeStruct((M, C), F32), jax.ShapeDtypeStruct((1, C), F32)]
    if token_rows:
        out_specs.append(pl.BlockSpec((tm, C), lambda m: (jnp.maximum(m - 1, 0), 0)))
        out_shape.append(jax.ShapeDtypeStruct((M - ROW0, C), F32))
    return pl.pallas_call(
        body, name=name, grid=(M // tm,), in_specs=in_specs, out_specs=out_specs, out_shape=out_shape,
        compiler_params=_params(("arbitrary",)))(*args)


def _ffn_up(b, w_gu, name):
    M, K = b.shape
    F = w_gu.shape[1] // 2
    tm = _pick(M, (768, 512, 256, 128))
    tn = _col_tile(F, 1408)
    nb = F // tn

    def body(b_ref, wg_ref, wu_ref, g_ref, u_ref, act_ref):
        x = b_ref[...]
        g = _dot(x, wg_ref[...])
        u = _dot(x, wu_ref[...])
        g_ref[...] = g
        u_ref[...] = u
        act_ref[...] = ((g * jax.nn.sigmoid(g)) * u).astype(act_ref.dtype)

    blk = pl.BlockSpec((tm, tn), lambda n, m: (m, n))
    return pl.pallas_call(
        body, name=name, grid=(nb, M // tm),
        in_specs=[pl.BlockSpec((tm, K), lambda n, m: (m, 0)),
                  pl.BlockSpec((K, tn), lambda n, m: (0, n)),
                  pl.BlockSpec((K, tn), lambda n, m: (0, n + nb))],
        out_specs=[blk, blk, blk],
        out_shape=[jax.ShapeDtypeStruct((M, F), F32), jax.ShapeDtypeStruct((M, F), F32),
                   jax.ShapeDtypeStruct((M, F), BF16)],
        compiler_params=_params(("parallel", "parallel")))(b, w_gu, w_gu)


def _ffn_dact(dy, w_d, g, u, name):
    M, K = dy.shape
    F = w_d.shape[0]
    tm = _pick(M, (768, 512, 256, 128))
    tn = _col_tile(F, 1408)
    nb = F // tn

    def body(dy_ref, wd_ref, g_ref, u_ref, dg_ref, du_ref):
        dact = _dot_nt(dy_ref[...].astype(BF16), wd_ref[...])
        gv = g_ref[...]
        s = jax.nn.sigmoid(gv)
        silu = gv * s
        dg_ref[...] = (dact * u_ref[...] * (s * (1.0 + gv * (1.0 - s)))).astype(dg_ref.dtype)
        du_ref[...] = (dact * silu).astype(du_ref.dtype)

    blk = pl.BlockSpec((tm, tn), lambda n, m: (m, n))
    dg, du = pl.pallas_call(
        body, name=name, grid=(nb, M // tm),
        in_specs=[pl.BlockSpec((tm, K), lambda n, m: (m, 0)), pl.BlockSpec((tn, K), lambda n, m: (n, 0)), blk, blk],
        out_specs=[blk, blk],
        out_shape=[jax.ShapeDtypeStruct((M, F), BF16), jax.ShapeDtypeStruct((M, F), BF16)],
        compiler_params=_params(("parallel", "parallel")))(dy, w_d, g, u)
    return jnp.concatenate([dg, du], axis=1)


def _loss_head(h, gain, target, name):
    M, C = h.shape
    tm = ROW0
    assert M % tm == 0 and target.shape[0] == M - ROW0

    def body(h_ref, g_ref, t_ref, sq_ref, dh_ref, dg_ref):
        i = pl.program_id(0)

        @pl.when(i == 0)
        def _():
            sq_ref[...] = jnp.zeros_like(sq_ref)
            dg_ref[...] = jnp.zeros_like(dg_ref)
            dh_ref[...] = jnp.zeros_like(dh_ref)

        @pl.when(i > 0)
        def _():
            x = h_ref[...]
            r = lax.rsqrt(jnp.mean(x * x, axis=-1, keepdims=True) + EPS)
            y = x * r
            err = y * g_ref[...] - t_ref[...]
            sq_ref[...] += jnp.sum(err * err, axis=0, keepdims=True)
            da = err * (1.0 / C)
            dy = da * g_ref[...]
            dh_ref[...] = r * (dy - y * jnp.mean(dy * y, axis=-1, keepdims=True))
            dg_ref[...] += jnp.sum(da * y, axis=0, keepdims=True)

    row = pl.BlockSpec((tm, C), lambda m: (m, 0))
    vec = pl.BlockSpec((1, C), lambda m: (0, 0))
    return pl.pallas_call(
        body, name=name, grid=(M // tm,),
        in_specs=[row, vec, pl.BlockSpec((tm, C), lambda m: (jnp.maximum(m - 1, 0), 0))],
        out_specs=[vec, row, vec],
        out_shape=[jax.ShapeDtypeStruct((1, C), F32), jax.ShapeDtypeStruct((M, C), F32),
                   jax.ShapeDtypeStruct((1, C), F32)],
        compiler_params=_params(("arbitrary",)))(h, gain, target)


def _pool_pos(row0, tm):
    return row0 + lax.broadcasted_iota(jnp.int32, (tm, 1), 0) - PAD


def _pool_fwd(h, a, w, scale, name):
    M, C = a.shape
    tm = 256
    hb = tm // HALO

    def body(h_ref, a_ref, halo_ref, w_ref, s_ref, o_ref, p_ref):
        i = pl.program_id(0)
        row0 = i * tm
        ext = jnp.concatenate([halo_ref[...], a_ref[...]], axis=0)
        src = row0 - HALO + lax.broadcasted_iota(jnp.int32, (tm + HALO, 1), 0)
        ext = jnp.where(src >= PAD, ext, 0.0)
        r = lax.broadcasted_iota(jnp.int32, (tm, tm + HALO), 0)
        c = lax.broadcasted_iota(jnp.int32, (tm, tm + HALO), 1)
        pos = _pool_pos(row0, tm)
        for g, win in enumerate(POOL_WINDOWS):
            band = ((c <= r + HALO) & (c > r + HALO - win)).astype(F32)
            cols = slice(g * POOL_GROUP, (g + 1) * POOL_GROUP)
            xg = ext[:, cols]
            tot = jnp.dot(band, xg, precision=lax.Precision.HIGHEST, preferred_element_type=F32)
            cnt = jnp.clip(pos + 1, 1, win).astype(F32)
            pooled = (tot / cnt - xg[HALO:]).astype(BF16)
            p_ref[:, cols] = pooled
            mixed = _dot(pooled, w_ref[g])
            o_ref[:, cols] = h_ref[:, cols] + mixed * s_ref[:, cols]

    row = pl.BlockSpec((tm, C), lambda m: (m, 0))
    return pl.pallas_call(
        body, name=name, grid=(M // tm,),
        in_specs=[row, row, pl.BlockSpec((HALO, C), lambda m: (jnp.maximum(m * hb - 1, 0), 0)),
                  pl.BlockSpec((4, POOL_GROUP, POOL_GROUP), lambda m: (0, 0, 0)),
                  pl.BlockSpec((1, C), lambda m: (0, 0))],
        out_specs=[row, row],
        out_shape=[jax.ShapeDtypeStruct((M, C), F32), jax.ShapeDtypeStruct((M, C), BF16)],
        compiler_params=_params(("parallel",)))(h, a, a, w, scale)


def _pool_bwd_mix(dout, pooled, w, scale, name):
    M, C = dout.shape
    tm = 256

    def body(do_ref, p_ref, w_ref, s_ref, dpc_ref, dw_ref, ds_ref):
        i = pl.program_id(0)

        @pl.when(i == 0)
        def _():
            dw_ref[...] = jnp.zeros_like(dw_ref)
            ds_ref[...] = jnp.zeros_like(ds_ref)

        pos = _pool_pos(i * tm, tm)
        for g, win in enumerate(POOL_WINDOWS):
            cols = slice(g * POOL_GROUP, (g + 1) * POOL_GROUP)
            do = do_ref[:, cols]
            pooled = p_ref[:, cols]
            mixed = _dot(pooled, w_ref[g])
            ds_ref[:, cols] += jnp.sum(do * mixed, axis=0, keepdims=True)
            dmix = (do * s_ref[:, cols]).astype(BF16)
            dw_ref[g] += _dot_tn(pooled, dmix)
            dp = _dot_nt(dmix, w_ref[g])
            cnt = jnp.clip(pos + 1, 1, win).astype(F32)
            dpc_ref[:, cols] = dp / cnt

    row = pl.BlockSpec((tm, C), lambda m: (m, 0))
    wspec = pl.BlockSpec((4, POOL_GROUP, POOL_GROUP), lambda m: (0, 0, 0))
    vec = pl.BlockSpec((1, C), lambda m: (0, 0))
    return pl.pallas_call(
        body, name=name, grid=(M // tm,),
        in_specs=[row, row, wspec, vec], out_specs=[row, wspec, vec],
        out_shape=[jax.ShapeDtypeStruct((M, C), F32), jax.ShapeDtypeStruct((4, POOL_GROUP, POOL_GROUP), F32),
                   jax.ShapeDtypeStruct((1, C), F32)],
        compiler_params=_params(("arbitrary",)))(dout, pooled, w, scale)


def _pool_bwd_window(dpc, name):
    M, C = dpc.shape
    tm = 256
    hb = tm // HALO
    last = M // HALO - 1

    def body(d_ref, halo_ref, da_ref):
        i = pl.program_id(0)
        row0 = i * tm
        ext = jnp.concatenate([d_ref[...], halo_ref[...]], axis=0)
        src = row0 + lax.broadcasted_iota(jnp.int32, (tm + HALO, 1), 0)
        ext = jnp.where(src < M, ext, 0.0)
        r = lax.broadcasted_iota(jnp.int32, (tm, tm + HALO), 0)
        c = lax.broadcasted_iota(jnp.int32, (tm, tm + HALO), 1)
        pos = _pool_pos(row0, tm)
        for g, win in enumerate(POOL_WINDOWS):
            band = ((c >= r) & (c < r + win)).astype(F32)
            cols = slice(g * POOL_GROUP, (g + 1) * POOL_GROUP)
            xg = ext[:, cols]
            tot = jnp.dot(band, xg, precision=lax.Precision.HIGHEST, preferred_element_type=F32)
            cnt = jnp.clip(pos + 1, 1, win).astype(F32)
            da_ref[:, cols] = jnp.where(pos >= 0, tot - xg[:tm] * cnt, 0.0)

    row = pl.BlockSpec((tm, C), lambda m: (m, 0))
    return pl.pallas_call(
        body, name=name, grid=(M // tm,),
        in_specs=[row, pl.BlockSpec((HALO, C), lambda m: (jnp.minimum((m + 1) * hb, last), 0))],
        out_specs=row, out_shape=jax.ShapeDtypeStruct((M, C), F32),
        compiler_params=_params(("parallel",)))(dpc, dpc)


def _head_masks():
    lane = lax.broadcasted_iota(jnp.int32, (1, 128), 1)
    return lane < HEAD_DIM, lane


def _split_heads(x, first):
    z = jnp.zeros_like(x)
    return jnp.where(first, x, z), jnp.where(first, z, x)


def _split_rope(x, lane):
    z = jnp.zeros_like(x)
    return jnp.where(lane < MLA_ROPE, x, z), jnp.where((lane >= MLA_ROPE) & (lane < 2 * MLA_ROPE), x, z)


def _walk_causal(i, step):
    def mid(kb, carry):
        step(kb, False)
        return carry

    step(0, True)
    lax.fori_loop(1, i, mid, 0)

    @pl.when(i > 0)
    def _():
        step(i, True)


def _mla_fwd(q_all, kv_all, qr, kr, scale, name):
    M = q_all.shape[0]
    t = ATTN_TILE

    def body(q_ref, k_ref, v_ref, qr_ref, kr_ref, o_ref, lse_ref, m_s, l_s, acc_s):
        i = pl.program_id(1)
        first, lane = _head_masks()
        qs = _split_heads(q_ref[...], first)
        qrs = _split_rope(qr_ref[...], lane)
        qcat = tuple(jnp.concatenate([qs[hh], qrs[hh]], axis=1) for hh in range(2))
        m_s[...] = jnp.full_like(m_s, NEG)
        l_s[...] = jnp.zeros_like(l_s)
        acc_s[...] = jnp.zeros_like(acc_s)
        qpos = i * t + lax.broadcasted_iota(jnp.int32, (t, t), 0)
        kidx = lax.broadcasted_iota(jnp.int32, (t, t), 1)

        def step(kb, masked):
            k0 = pl.multiple_of(kb * t, t)
            kcat = jnp.concatenate([k_ref[pl.ds(k0, t), :], kr_ref[pl.ds(k0, t), :]], axis=1)
            vs = _split_heads(v_ref[pl.ds(k0, t), :], first)
            if masked:
                kpos = k0 + kidx
                valid = (kpos <= qpos) & (kpos >= PAD)
            pv = None
            alphas = []
            for hh in range(2):
                s = _dot_nt(qcat[hh], kcat) * scale
                if masked:
                    s = jnp.where(valid, s, NEG)
                m_old = m_s[hh]
                m_new = jnp.maximum(m_old, jnp.max(s, axis=1, keepdims=True))
                p = jnp.exp(s - m_new)
                alpha = jnp.exp(m_old - m_new)
                l_s[hh] = alpha * l_s[hh] + jnp.sum(p, axis=1, keepdims=True)
                m_s[hh] = m_new
                d = _dot(p.astype(BF16), vs[hh])
                pv = d if pv is None else pv + d
                alphas.append(alpha)
            acc_s[...] = acc_s[...] * jnp.where(first, alphas[0], alphas[1]) + pv

        _walk_causal(i, step)
        o_ref[...] = (acc_s[...] * jnp.where(first, 1.0 / l_s[0], 1.0 / l_s[1])).astype(o_ref.dtype)
        lse_ref[:, 0:1] = m_s[0] + jnp.log(l_s[0])
        lse_ref[:, 1:2] = m_s[1] + jnp.log(l_s[1])

    blk = pl.BlockSpec((t, 128), lambda j, i: (i, j))
    return pl.pallas_call(
        body, name=name, grid=(N_PAIRS, M // t),
        in_specs=[blk, pl.BlockSpec((M, 128), lambda j, i: (0, j)), pl.BlockSpec((M, 128), lambda j, i: (0, N_PAIRS + j)),
                  blk, pl.BlockSpec((M, 128), lambda j, i: (0, 0))],
        out_specs=[blk, pl.BlockSpec((None, t, 2), lambda j, i: (j, i, 0))],
        out_shape=[jax.ShapeDtypeStruct((M, N_PAIRS * 128), BF16), jax.ShapeDtypeStruct((N_PAIRS, M, 2), F32)],
        scratch_shapes=[pltpu.VMEM((2, t, 1), F32), pltpu.VMEM((2, t, 1), F32), pltpu.VMEM((t, 128), F32)],
        compiler_params=_params(("parallel", "arbitrary")))(q_all, kv_all, kv_all, qr, kr)


def _mla_bwd(q_all, kv_all, qr, kr, o, do, lse, scale, name):
    M = q_all.shape[0]
    t = ATTN_BWD_TILE

    def body(q_ref, k_ref, v_ref, qr_ref, kr_ref, o_ref, do_ref, lse_ref,
             dq_ref, dk_ref, dv_ref, dqr_ref, dkr_ref, dq_s, lse_s, delta_s):
        i = pl.program_id(1)
        first, lane = _head_masks()

        @pl.when(i == 0)
        def _():
            dk_ref[...] = jnp.zeros_like(dk_ref)
            dv_ref[...] = jnp.zeros_like(dv_ref)
            dkr_ref[...] = jnp.zeros_like(dkr_ref)

        qs = _split_heads(q_ref[...], first)
        qrs = _split_rope(qr_ref[...], lane)
        qcat = tuple(jnp.concatenate([qs[hh], qrs[hh]], axis=1) for hh in range(2))
        dov = do_ref[...]
        dos = _split_heads(dov, first)
        prod = dov.astype(F32) * o_ref[...].astype(F32)
        deltas = (jnp.sum(jnp.where(first, prod, 0.0), axis=1, keepdims=True),
                  jnp.sum(jnp.where(first, 0.0, prod), axis=1, keepdims=True))
        for hh in range(2):
            lse_s[hh] = jnp.broadcast_to(lse_ref[:, hh:hh + 1], (t, t))
            delta_s[hh] = jnp.broadcast_to(deltas[hh], (t, t))
        dq_s[...] = jnp.zeros_like(dq_s)
        qpos = i * t + lax.broadcasted_iota(jnp.int32, (t, t), 0)
        kidx = lax.broadcasted_iota(jnp.int32, (t, t), 1)

        def step(kb, masked):
            k0 = pl.multiple_of(kb * t, t)
            rows = pl.ds(k0, t)
            k = k_ref[rows, :]
            v = v_ref[rows, :]
            kr = kr_ref[rows, :]
            kcat = jnp.concatenate([k, kr], axis=1)
            ks = _split_heads(k, first)
            krs = _split_rope(kr, lane)
            if masked:
                kpos = k0 + kidx
                valid = (kpos <= qpos) & (kpos >= PAD)
            dq = dk = dv = None
            for hh in range(2):
                s = _dot_nt(qcat[hh], kcat) * scale
                if masked:
                    s = jnp.where(valid, s, NEG)
                p = jnp.exp(s - lse_s[hh])
                ds = p * (_dot_nt(dos[hh], v) - delta_s[hh])
                dsb = (ds * scale).astype(BF16)
                a = _dot(dsb, jnp.concatenate([ks[hh], krs[hh]], axis=1))
                b = _dot_tn(dsb, qcat[hh])
                c = _dot_tn(p.astype(BF16), dos[hh])
                dq = a if dq is None else dq + a
                dk = b if dk is None else dk + b
                dv = c if dv is None else dv + c
            dq_s[...] += dq
            dk_ref[rows, :] += dk[:, :128]
            dkr_ref[rows, :] += dk[:, 128:]
            dv_ref[rows, :] += dv

        _walk_causal(i, step)
        dq_ref[...] = dq_s[:, :128].astype(dq_ref.dtype)
        dqr_ref[...] = dq_s[:, 128:].astype(dqr_ref.dtype)

    blk = pl.BlockSpec((t, 128), lambda j, i: (i, j))
    col = pl.BlockSpec((M, 128), lambda j, i: (0, j))
    wide = jax.ShapeDtypeStruct((M, N_PAIRS * 128), F32)
    return pl.pallas_call(
        body, name=name, grid=(N_PAIRS, M // t),
        in_specs=[blk, col, pl.BlockSpec((M, 128), lambda j, i: (0, N_PAIRS + j)), blk,
                  pl.BlockSpec((M, 128), lambda j, i: (0, 0)), blk, blk,
                  pl.BlockSpec((None, t, 2), lambda j, i: (j, i, 0))],
        out_specs=[blk, col, col, blk, pl.BlockSpec((None, M, 128), lambda j, i: (j, 0, 0))],
        out_shape=[jax.ShapeDtypeStruct((M, N_PAIRS * 128), BF16), wide, wide,
                   jax.ShapeDtypeStruct((M, N_PAIRS * 128), BF16), jax.ShapeDtypeStruct((N_PAIRS, M, 128), F32)],
        scratch_shapes=[pltpu.VMEM((t, 256), F32), pltpu.VMEM((2, t, t), F32), pltpu.VMEM((2, t, t), F32)],
        compiler_params=_params(("parallel", "arbitrary")))(q_all, kv_all, kv_all, qr, kr, o, do, lse)


def _tri(t, rel):
    j = lax.broadcasted_iota(jnp.int32, (t, t), 0)
    k = lax.broadcasted_iota(jnp.int32, (t, t), 1)
    m = {"gt": j > k, "le": j <= k, "lt": j < k}[rel]
    return m.astype(BF16)


def _lane_cumsum(x, tri):
    hi = x.astype(BF16)
    lo = (x - hi.astype(F32)).astype(BF16)
    return _dot(hi, tri) + _dot(lo, tri)


def _log_sigmoids(z):
    sp = jnp.log(1.0 + jnp.exp(-jnp.abs(z)))
    return jnp.minimum(z, 0.0) - sp, jnp.minimum(-z, 0.0) - sp


def _sb_fwd(qkv, scale, name):
    M = qkv.shape[0]
    t = WALK_TILE
    ck, cv = N_PAIRS, 2 * N_PAIRS

    def body(q_ref, k_ref, v_ref, o_ref, tot_ref, c_s, acc_s):
        i = pl.program_id(1)
        first, _ = _head_masks()
        qs = _split_heads(q_ref[...], first)
        c_s[...] = jnp.zeros_like(c_s)
        acc_s[...] = jnp.zeros_like(acc_s)
        tri = _tri(t, "gt")
        qpos = i * t + lax.broadcasted_iota(jnp.int32, (t, t), 0)
        kidx = lax.broadcasted_iota(jnp.int32, (t, t), 1)

        def step(it):
            k0 = pl.multiple_of((i - it) * t, t)
            k = k_ref[pl.ds(k0, t), :]
            vs = _split_heads(v_ref[pl.ds(k0, t), :], first)
            kpos = k0 + kidx
            valid = (kpos < qpos) & (kpos >= PAD)
            pv = None
            for hh in range(2):
                z = _dot_nt(qs[hh], k) * scale
                lb, lkr = _log_sigmoids(z)
                lk = jnp.where(valid, lkr, 0.0)
                later = c_s[hh] + _lane_cumsum(lk, tri)
                a = jnp.where(valid, jnp.exp(lb + later), 0.0)
                c_s[hh] = c_s[hh] + jnp.sum(lk, axis=1, keepdims=True)
                d = _dot(a.astype(BF16), vs[hh])
                pv = d if pv is None else pv + d
            acc_s[...] += pv

        def cond(carry):
            it, go = carry
            return (it <= i) & go

        def walk(carry):
            it, _ = carry
            step(it)
            return it + 1, jnp.max(jnp.maximum(c_s[0], c_s[1])) > EXP_ZERO

        walked, _ = lax.while_loop(cond, walk, (jnp.int32(0), True))
        o_ref[...] = acc_s[...].astype(o_ref.dtype)
        tot_ref[:, 0:1] = c_s[0]
        tot_ref[:, 1:2] = c_s[1]
        tot_ref[:, 2:3] = jnp.full((t, 1), walked.astype(F32))

    whole = lambda c0: pl.BlockSpec((M, 128), lambda j, i: (0, c0 + j))
    return pl.pallas_call(
        body, name=name, grid=(N_PAIRS, M // t),
        in_specs=[pl.BlockSpec((t, 128), lambda j, i: (i, j)), whole(ck), whole(cv)],
        out_specs=[pl.BlockSpec((t, 128), lambda j, i: (i, j)), pl.BlockSpec((None, t, 3), lambda j, i: (j, i, 0))],
        out_shape=[jax.ShapeDtypeStruct((M, N_PAIRS * 128), BF16), jax.ShapeDtypeStruct((N_PAIRS, M, 3), F32)],
        scratch_shapes=[pltpu.VMEM((2, t, 1), F32), pltpu.VMEM((t, 128), F32)],
        compiler_params=_params(("parallel", "arbitrary")))(qkv, qkv, qkv)


def _sb_bwd(qkv, do, tot, scale, name):
    M = qkv.shape[0]
    t = WALK_TILE
    ck, cv = N_PAIRS, 2 * N_PAIRS

    def body(q_ref, k_ref, v_ref, do_ref, tot_ref, dq_ref, dk_ref, dv_ref, pc_s, dc_s, dq_s):
        i = pl.program_id(1)
        first, _ = _head_masks()

        @pl.when(i == 0)
        def _():
            dk_ref[...] = jnp.zeros_like(dk_ref)
            dv_ref[...] = jnp.zeros_like(dv_ref)

        qs = _split_heads(q_ref[...], first)
        dos = _split_heads(do_ref[...], first)
        pc_s[...] = jnp.zeros_like(pc_s)
        dc_s[...] = jnp.zeros_like(dc_s)
        dq_s[...] = jnp.zeros_like(dq_s)
        tri_le = _tri(t, "le")
        tri_lt = _tri(t, "lt")
        qpos = i * t + lax.broadcasted_iota(jnp.int32, (t, t), 0)
        kidx = lax.broadcasted_iota(jnp.int32, (t, t), 1)

        def step(kb, carry):
            k0 = pl.multiple_of(kb * t, t)
            rows = pl.ds(k0, t)
            k = k_ref[rows, :]
            v = v_ref[rows, :]
            ks = _split_heads(k, first)
            kpos = k0 + kidx
            valid = (kpos < qpos) & (kpos >= PAD)
            dq = dk = dv = None
            for hh in range(2):
                z = _dot_nt(qs[hh], k) * scale
                lb, lkr = _log_sigmoids(z)
                lk = jnp.where(valid, lkr, 0.0)
                later = tot_ref[:, hh:hh + 1] - (pc_s[hh] + _lane_cumsum(lk, tri_le))
                a = jnp.where(valid, jnp.exp(lb + later), 0.0)
                dl = a * _dot_nt(dos[hh], v)
                early = dc_s[hh] + _lane_cumsum(dl, tri_lt)
                sg = jnp.exp(lb)
                dz = jnp.where(valid, dl * (1.0 - sg) - early * sg, 0.0) * scale
                pc_s[hh] = pc_s[hh] + jnp.sum(lk, axis=1, keepdims=True)
                dc_s[hh] = dc_s[hh] + jnp.sum(dl, axis=1, keepdims=True)
                dzb = dz.astype(BF16)
                x = _dot(dzb, ks[hh])
                y = _dot_tn(dzb, qs[hh])
                w = _dot_tn(a.astype(BF16), dos[hh])
                dq = x if dq is None else dq + x
                dk = y if dk is None else dk + y
                dv = w if dv is None else dv + w
            dq_s[...] += dq
            dk_ref[rows, :] += dk
            dv_ref[rows, :] += dv
            return carry

        walked = jnp.max(tot_ref[:, 2:3]).astype(jnp.int32)
        lax.fori_loop(i + 1 - walked, i + 1, step, 0)
        dq_ref[...] = dq_s[...].astype(dq_ref.dtype)

    whole = lambda c0: pl.BlockSpec((M, 128), lambda j, i: (0, c0 + j))
    blk = pl.BlockSpec((t, 128), lambda j, i: (i, j))
    col = pl.BlockSpec((M, 128), lambda j, i: (0, j))
    return pl.pallas_call(
        body, name=name, grid=(N_PAIRS, M // t),
        in_specs=[blk, whole(ck), whole(cv), blk, pl.BlockSpec((None, t, 3), lambda j, i: (j, i, 0))],
        out_specs=[blk, col, col],
        out_shape=[jax.ShapeDtypeStruct((M, N_PAIRS * 128), BF16), jax.ShapeDtypeStruct((M, N_PAIRS * 128), F32),
                   jax.ShapeDtypeStruct((M, N_PAIRS * 128), F32)],
        scratch_shapes=[pltpu.VMEM((2, t, 1), F32), pltpu.VMEM((2, t, 1), F32), pltpu.VMEM((t, 128), F32)],
        compiler_params=_params(("parallel", "arbitrary")))(qkv, qkv, qkv, do, tot)


def _rows_between(lo, hi):
    r = lax.broadcasted_iota(jnp.int32, (128, 1), 0)
    return (r >= lo) & (r < hi)


def _lanes_between(lo, hi):
    c = lax.broadcasted_iota(jnp.int32, (1, 128), 1)
    return (c >= lo) & (c < hi)


def _keep(x, mask):
    return jnp.where(mask, x, jnp.zeros_like(x))


def _valid_mask(i, kb, t):
    kpos = kb * t + lax.broadcasted_iota(jnp.int32, (t, t), 0)
    qpos = i * t + lax.broadcasted_iota(jnp.int32, (t, t), 1)
    return (kpos <= qpos) & (kpos >= PAD)


def _fox_fwd(qkv, qkv_t, f_rows, f_cols, scale, name):
    M = qkv.shape[0]
    t = WALK_TILE
    first_blk = PAD // t
    ck, cv = N_PAIRS, 2 * N_PAIRS

    def body(qt_ref, k_ref, vt_ref, fq_ref, fk_ref, o_ref, lse_ref, ox_ref, m_s, l_s, acc_s, accx_s, kmax_s):
        i = pl.program_id(1)

        @pl.when(i == 0)
        def _():
            first = _lanes_between(0, 64)

            def block_max(kb, carry):
                kk = k_ref[pl.ds(pl.multiple_of(kb * t, t), t), :].astype(F32)
                kk = kk * kk
                a = jnp.max(jnp.sum(jnp.where(first, kk, 0.0), axis=1, keepdims=True))
                b = jnp.max(jnp.sum(jnp.where(first, 0.0, kk), axis=1, keepdims=True))
                return jnp.maximum(carry[0], a), jnp.maximum(carry[1], b)

            a, b = lax.fori_loop(0, M // t, block_max, (jnp.float32(0.0), jnp.float32(0.0)))
            kmax_s[0] = a
            kmax_s[1] = b

        qt = qt_ref[...]
        qts = (_keep(qt, _rows_between(0, 64)), _keep(qt, _rows_between(64, 128)))
        qf = qt.astype(F32)
        qf = qf * qf
        qbound = tuple(
            (1.001 * scale) * jnp.sqrt(jnp.sum(qf[HEAD_DIM * hh:HEAD_DIM * (hh + 1)], axis=0, keepdims=True) * kmax_s[hh])
            for hh in range(2))
        m_s[...] = jnp.full_like(m_s, NEG)
        l_s[...] = jnp.zeros_like(l_s)
        acc_s[...] = jnp.zeros_like(acc_s)
        accx_s[...] = jnp.zeros_like(accx_s)

        def step(kb, masked):
            k0 = pl.multiple_of(kb * t, t)
            rows = pl.ds(k0, t)
            k = k_ref[rows, :]
            if masked:
                valid = _valid_mask(i, kb, t)
            for hh in range(2):
                s = _dot(k, qts[hh]) * scale + (fq_ref[hh:hh + 1, :] - fk_ref[rows, hh:hh + 1])
                if masked:
                    s = jnp.where(valid, s, NEG)
                m_old = m_s[hh]
                m_new = jnp.maximum(m_old, jnp.max(s, axis=0, keepdims=True))
                p = jnp.exp(s - m_new)
                alpha = jnp.exp(m_old - m_new)
                l_s[hh] = alpha * l_s[hh] + jnp.sum(p, axis=0, keepdims=True)
                m_s[hh] = m_new
                pb = p.astype(BF16)
                hr = slice(HEAD_DIM * hh, HEAD_DIM * (hh + 1))
                vt = vt_ref[hr, rows]
                acc_s[hr, :] = acc_s[hr, :] * alpha + _dot(vt, pb)
                accx_s[hr, :] = accx_s[hr, :] * alpha + _dot(vt, (p - pb.astype(F32)).astype(BF16))

        def keep_going(kb):
            k0 = pl.multiple_of(kb * t, t)
            worst = None
            for hh in range(2):
                f0 = jnp.max(fk_ref[pl.ds(k0, 8), hh:hh + 1])
                w = jnp.max(qbound[hh] + (fq_ref[hh:hh + 1, :] - f0) - m_s[hh])
                worst = w if worst is None else jnp.maximum(worst, w)
            return worst > EXP_ZERO

        def cond(carry):
            kb, go, _ = carry
            return (kb > first_blk) & go

        def walk(carry):
            kb, _, n = carry
            step(kb, False)
            return kb - 1, keep_going(kb), n + 1

        step(i, True)
        _, go, n = lax.while_loop(cond, walk, (i - 1, keep_going(i), jnp.int32(1)))
        first_too = go & (i > first_blk)

        @pl.when(first_too)
        def _():
            step(first_blk, True)

        walked = n + first_too.astype(jnp.int32)
        for hh in range(2):
            hr = slice(HEAD_DIM * hh, HEAD_DIM * (hh + 1))
            inv = 1.0 / l_s[hh]
            o_ref[hr, :] = (acc_s[hr, :] * inv).astype(o_ref.dtype)
            ox_ref[hr, :] = (acc_s[hr, :] + accx_s[hr, :]) * inv
            lse_ref[hh:hh + 1, :] = m_s[hh] + jnp.log(l_s[hh])
        lse_ref[2:3, :] = jnp.full((1, t), walked.astype(F32))

    blk = pl.BlockSpec((128, t), lambda j, i: (j, i))
    stat = pl.BlockSpec((None, 2, t), lambda j, i: (j, 0, i))
    return pl.pallas_call(
        body, name=name, grid=(N_PAIRS, M // t),
        in_specs=[blk, pl.BlockSpec((M, 128), lambda j, i: (0, ck + j)), pl.BlockSpec((128, M), lambda j, i: (cv + j, 0)),
                  stat, pl.BlockSpec((None, M, 2), lambda j, i: (j, 0, 0))],
        out_specs=[blk, pl.BlockSpec((None, 3, t), lambda j, i: (j, 0, i)), blk],
        out_shape=[jax.ShapeDtypeStruct((N_PAIRS * 128, M), BF16), jax.ShapeDtypeStruct((N_PAIRS, 3, M), F32),
                   jax.ShapeDtypeStruct((N_PAIRS * 128, M), F32)],
        scratch_shapes=[pltpu.VMEM((2, 1, t), F32), pltpu.VMEM((2, 1, t), F32), pltpu.VMEM((128, t), F32),
                        pltpu.VMEM((128, t), F32), pltpu.SMEM((2,), F32)],
        compiler_params=_params(("parallel", "arbitrary")))(qkv_t, qkv, qkv_t, f_rows, f_cols)


def _fox_bwd(qkv, qkv_t, o_t, do, do_t, lse, f_rows, f_cols, scale, name):
    M = qkv.shape[0]
    t = WALK_TILE
    first_blk = PAD // t
    ck, cv = N_PAIRS, 2 * N_PAIRS

    def body(q_ref, qt_ref, k_ref, kt_ref, v_ref, ot_ref, do_ref, dot_ref, lse_ref, fq_ref, fk_ref,
             dq_ref, dk_ref, dv_ref, cs_ref, dq_s):
        i = pl.program_id(1)

        @pl.when(i == 0)
        def _():
            dk_ref[...] = jnp.zeros_like(dk_ref)
            dv_ref[...] = jnp.zeros_like(dv_ref)
            cs_ref[...] = jnp.zeros_like(cs_ref)

        heads_l = (_lanes_between(0, 64), _lanes_between(64, 128))
        heads_r = (_rows_between(0, 64), _rows_between(64, 128))
        q = q_ref[...]
        qt = qt_ref[...]
        do = do_ref[...]
        dot = dot_ref[...]
        qs = tuple(_keep(q, m) for m in heads_l)
        qts = tuple(_keep(qt, m) for m in heads_r)
        dos = tuple(_keep(do, m) for m in heads_l)
        dots = tuple(_keep(dot, m) for m in heads_r)
        prod = dot.astype(F32) * ot_ref[...]
        deltas = tuple(jnp.sum(prod[HEAD_DIM * hh:HEAD_DIM * (hh + 1)], axis=0, keepdims=True) for hh in range(2))
        ones = tuple(m.astype(BF16) * jnp.ones((t, 128), BF16) for m in heads_l)
        dq_s[...] = jnp.zeros_like(dq_s)

        def step(kb, masked):
            k0 = pl.multiple_of(kb * t, t)
            rows = pl.ds(k0, t)
            k = k_ref[rows, :]
            v = v_ref[rows, :]
            if masked:
                valid = _valid_mask(i, kb, t)
            dk = dv = cs = None
            for hh in range(2):
                s = _dot(k, qts[hh]) * scale + (fq_ref[hh:hh + 1, :] - fk_ref[rows, hh:hh + 1])
                if masked:
                    s = jnp.where(valid, s, NEG)
                p = jnp.exp(s - lse_ref[hh:hh + 1, :])
                ds = p * (_dot(v, dots[hh]) - deltas[hh])
                hi = ds.astype(BF16)
                lo = (ds - hi.astype(F32)).astype(BF16)
                c = _dot(hi, ones[hh]) + _dot(lo, ones[hh])
                dsb = (ds * scale).astype(BF16)
                hr = slice(HEAD_DIM * hh, HEAD_DIM * (hh + 1))
                dq_s[hr, :] += _dot(kt_ref[hr, rows], dsb)
                a = _dot(dsb, qs[hh])
                b = _dot(p.astype(BF16), dos[hh])
                dk = a if dk is None else dk + a
                dv = b if dv is None else dv + b
                cs = c if cs is None else cs + c
            dk_ref[rows, :] += dk
            dv_ref[rows, :] += dv
            cs_ref[rows, :] += cs

        first_walked = i + 1 - jnp.max(lse_ref[2:3, :]).astype(jnp.int32)

        def mid(kb, carry):
            step(kb, False)
            return carry

        @pl.when((first_walked == first_blk) & (i > first_blk))
        def _():
            step(first_blk, True)

        lax.fori_loop(jnp.maximum(first_walked, first_blk + 1), i, mid, 0)
        step(i, True)
        dq_ref[...] = dq_s[...].astype(dq_ref.dtype)

    rblk = pl.BlockSpec((t, 128), lambda j, i: (i, j))
    tblk = pl.BlockSpec((128, t), lambda j, i: (j, i))
    stat = pl.BlockSpec((None, 2, t), lambda j, i: (j, 0, i))
    stat3 = pl.BlockSpec((None, 3, t), lambda j, i: (j, 0, i))
    col = pl.BlockSpec((M, 128), lambda j, i: (0, j))
    wide = jax.ShapeDtypeStruct((M, N_PAIRS * 128), F32)
    return pl.pallas_call(
        body, name=name, grid=(N_PAIRS, M // t),
        in_specs=[rblk, tblk, pl.BlockSpec((M, 128), lambda j, i: (0, ck + j)),
                  pl.BlockSpec((128, M), lambda j, i: (ck + j, 0)), pl.BlockSpec((M, 128), lambda j, i: (0, cv + j)),
                  tblk, rblk, tblk, stat3, stat, pl.BlockSpec((None, M, 2), lambda j, i: (j, 0, 0))],
        out_specs=[tblk, col, col, pl.BlockSpec((None, M, 128), lambda j, i: (j, 0, 0))],
        out_shape=[jax.ShapeDtypeStruct((N_PAIRS * 128, M), BF16), wide, wide,
                   jax.ShapeDtypeStruct((N_PAIRS, M, 128), F32)],
        scratch_shapes=[pltpu.VMEM((128, t), F32)],
        compiler_params=_params(("parallel", "arbitrary")))(qkv, qkv_t, qkv, qkv_t, qkv, o_t, do, do_t, lse,
                                                            f_rows, f_cols)


def _rope_tables(M):
    pos = (jnp.arange(M, dtype=jnp.int32) - PAD).astype(F32)
    inv = ROPE_THETA ** (-jnp.arange(0, MLA_ROPE, 2, dtype=F32) / MLA_ROPE)
    ang = pos[:, None] * inv[None, :]
    cos, sin = jnp.cos(ang), jnp.sin(ang)
    z = jnp.zeros((M, 64), F32)
    cos_t = jnp.concatenate([cos, cos, cos, cos, z], axis=1)
    sin_t = jnp.concatenate([-sin, sin, -sin, sin, z], axis=1)
    return cos_t, sin_t


def _rope(x, cos_t, sin_t, out_dtype, name, inverse=False, lead=0):
    M, C = x.shape
    tm = _pick(M, (768, 512, 256, 128))
    nblk = (C - lead) // 128
    sign = -1.0 if inverse else 1.0

    def body(x_ref, c_ref, s_ref, o_ref):
        lane = lax.broadcasted_iota(jnp.int32, (1, 128), 1)
        low = (lane % MLA_ROPE) < (MLA_ROPE // 2)
        cos = c_ref[...]
        sin = s_ref[...] * sign
        if lead:
            o_ref[:, :lead] = x_ref[:, :lead].astype(o_ref.dtype)
        for b in range(nblk):
            cols = slice(lead + b * 128, lead + (b + 1) * 128)
            v = x_ref[:, cols].astype(F32)
            up = pltpu.roll(v, 128 - MLA_ROPE // 2, 1)
            down = pltpu.roll(v, MLA_ROPE // 2, 1)
            o_ref[:, cols] = (v * cos + jnp.where(low, up, down) * sin).astype(o_ref.dtype)

    row = pl.BlockSpec((tm, C), lambda m: (m, 0))
    tab = pl.BlockSpec((tm, 128), lambda m: (m, 0))
    return pl.pallas_call(
        body, name=name, grid=(M // tm,), in_specs=[row, tab, tab], out_specs=row,
        out_shape=jax.ShapeDtypeStruct((M, C), out_dtype),
        compiler_params=_params(("parallel",)))(x, cos_t, sin_t)


def _forget_cumsum(f_logit, bias, name):
    M = f_logit.shape[0]
    tm = 256

    def body(f_ref, b_ref, o_ref, c_s):
        i = pl.program_id(0)

        @pl.when(i == 0)
        def _():
            c_s[...] = jnp.zeros_like(c_s)
        ls, _ = _log_sigmoids(f_ref[...] + b_ref[...])
        rows = i * tm + lax.broadcasted_iota(jnp.int32, (tm, 1), 0)
        ls = jnp.where(rows >= PAD, ls, 0.0)
        r = lax.broadcasted_iota(jnp.int32, (tm, tm), 0)
        c = lax.broadcasted_iota(jnp.int32, (tm, tm), 1)
        tri = (c <= r).astype(F32)
        cum = jnp.dot(tri, ls, precision=lax.Precision.HIGHEST, preferred_element_type=F32) + c_s[...]
        o_ref[...] = cum
        c_s[...] = cum[tm - 1:tm, :]

    row = pl.BlockSpec((tm, 128), lambda m: (m, 0))
    return pl.pallas_call(
        body, name=name, grid=(M // tm,),
        in_specs=[row, pl.BlockSpec((1, 128), lambda m: (0, 0))], out_specs=row,
        out_shape=jax.ShapeDtypeStruct((M, 128), F32), scratch_shapes=[pltpu.VMEM((1, 128), F32)],
        compiler_params=_params(("arbitrary",)))(f_logit, bias)


def _forget_cumsum_bwd(f_logit, bias, dF, name):
    M = f_logit.shape[0]
    tm = 256
    nb = M // tm

    def body(f_ref, b_ref, d_ref, o_ref, db_ref, c_s):
        i = pl.program_id(0)

        @pl.when(i == 0)
        def _():
            c_s[...] = jnp.zeros_like(c_s)
            db_ref[...] = jnp.zeros_like(db_ref)
        r = lax.broadcasted_iota(jnp.int32, (tm, tm), 0)
        c = lax.broadcasted_iota(jnp.int32, (tm, tm), 1)
        tri = (c >= r).astype(F32)
        cum = jnp.dot(tri, d_ref[...], precision=lax.Precision.HIGHEST, preferred_element_type=F32) + c_s[...]
        c_s[...] = cum[0:1, :]
        _, lsn = _log_sigmoids(f_ref[...] + b_ref[...])
        rows = (nb - 1 - i) * tm + lax.broadcasted_iota(jnp.int32, (tm, 1), 0)
        dl = jnp.where(rows >= PAD, cum * jnp.exp(lsn), 0.0)
        o_ref[...] = dl
        db_ref[...] += jnp.sum(dl, axis=0, keepdims=True)

    row = pl.BlockSpec((tm, 128), lambda m: (nb - 1 - m, 0))
    vec = pl.BlockSpec((1, 128), lambda m: (0, 0))
    return pl.pallas_call(
        body, name=name, grid=(nb,), in_specs=[row, vec, row], out_specs=[row, vec],
        out_shape=[jax.ShapeDtypeStruct((M, 128), F32), jax.ShapeDtypeStruct((1, 128), F32)],
        scratch_shapes=[pltpu.VMEM((1, 128), F32)],
        compiler_params=_params(("arbitrary",)))(f_logit, bias, dF)


def _adamw(w, parts, m, v, name):
    R, C = w.shape
    tr = R
    for d in range(8, R, 8):
        if R % d == 0 and d * C <= ADAM_TILE_ELEMS:
            tr = d
    c1 = 1.0 - ADAM_B1 ** ADAM_STEP
    c2 = 1.0 - ADAM_B2 ** ADAM_STEP

    def body(w_ref, s_ref, m_ref, v_ref, g_ref, d_ref, mo_ref, vo_ref):
        g = s_ref[0].astype(F32)
        for k in range(1, N_DEV):
            g = g + s_ref[k].astype(F32)
        mn = ADAM_B1 * m_ref[...] + (1.0 - ADAM_B1) * g
        vn = ADAM_B2 * v_ref[...] + (1.0 - ADAM_B2) * (g * g)
        m_hat = mn / c1
        v_hat = vn / c2
        g_ref[...] = g
        d_ref[...] = -ADAM_LR * (m_hat / (jnp.sqrt(v_hat) + ADAM_EPS) + ADAM_WD * w_ref[...])
        mo_ref[...] = mn
        vo_ref[...] = vn

    row = pl.BlockSpec((tr, C), lambda r: (r, 0))
    shp = jax.ShapeDtypeStruct((R, C), F32)
    return pl.pallas_call(
        body, name=name, grid=(R // tr,),
        in_specs=[row, pl.BlockSpec((N_DEV, tr, C), lambda r: (0, r, 0)), row, row],
        out_specs=[row, row, row, row], out_shape=[shp, shp, shp, shp],
        compiler_params=_params(("parallel",)))(w, parts, m, v)


def _position():
    return lax.axis_index("x"), lax.axis_index("y"), lax.axis_index("c")


def _all_gather(blocks, name):
    n = len(blocks)

    def body(*refs):
        x_refs, out_refs = refs[:n], refs[n:2 * n]
        send_sems, recv_sems, local_sems = refs[2 * n:]
        x, y, c = _position()
        me, sibling = (x, y, c), (x, y, 1 - c)
        chips = [(1 - x, y), (x, 1 - y), (1 - x, 1 - y)]

        def copies(k, block, to, own=False):
            slot = 4 * block[0] + 2 * block[1] + block[2]
            return [pltpu.make_async_remote_copy(
                src_ref=x_refs[p] if own else out_refs[p].at[slot], dst_ref=out_refs[p].at[slot],
                send_sem=send_sems.at[k, p], recv_sem=recv_sems.at[k, p], device_id=to, device_id_type=MESH)
                for p in range(n)]

        mine = [pltpu.make_async_copy(x_refs[p], out_refs[p].at[4 * x + 2 * y + c], local_sems.at[p]) for p in range(n)]
        for cp in mine:
            cp.start()
        first = copies(0, me, sibling, own=True)
        for j, chip in enumerate(chips):
            first += copies(1 + j, me, (*chip, c), own=True)
        for cp in first:
            cp.start()
        passed = []
        for j, chip in enumerate(chips):
            for cp in copies(1 + j, (*chip, c), me):
                cp.wait_recv()
            onward = copies(4 + j, (*chip, c), sibling)
            for cp in onward:
                cp.start()
            passed += onward
        for cp in copies(0, sibling, me):
            cp.wait_recv()
        for j, chip in enumerate(chips):
            for cp in copies(4 + j, (*chip, 1 - c), me):
                cp.wait_recv()
        for cp in first + passed:
            cp.wait_send()
        for cp in mine:
            cp.wait()

    any_spec = pl.BlockSpec(memory_space=pl.ANY)
    return pl.pallas_call(
        body, name=name, out_shape=[jax.ShapeDtypeStruct((N_DEV,) + b.shape, b.dtype) for b in blocks],
        in_specs=[any_spec] * n, out_specs=[any_spec] * n,
        scratch_shapes=[pltpu.SemaphoreType.DMA((7, n)), pltpu.SemaphoreType.DMA((7, n)), pltpu.SemaphoreType.DMA((n,))],
    )(*blocks)


def _exchange(parts, name):
    n = len(parts)

    def body(*refs):
        g_refs, land_refs = refs[:n], refs[n:2 * n]
        send_sems, recv_sems, local_sems = refs[2 * n:]
        x, y, c = _position()
        me = 4 * x + 2 * y + c
        mine = [pltpu.make_async_copy(g_refs[p].at[me], land_refs[p].at[me], local_sems.at[p]) for p in range(n)]
        for cp in mine:
            cp.start()
        sends, recvs = [], []
        for k in range(1, N_DEV):
            px = 1 - x if k & 4 else x
            py = 1 - y if k & 2 else y
            pc = 1 - c if k & 1 else c
            peer = 4 * px + 2 * py + pc
            for p in range(n):
                sends.append(pltpu.make_async_remote_copy(
                    src_ref=g_refs[p].at[peer], dst_ref=land_refs[p].at[me], send_sem=send_sems.at[k - 1, p],
                    recv_sem=recv_sems.at[k - 1, p], device_id=(px, py, pc), device_id_type=MESH))
                recvs.append(pltpu.make_async_remote_copy(
                    src_ref=g_refs[p].at[me], dst_ref=land_refs[p].at[peer], send_sem=send_sems.at[k - 1, p],
                    recv_sem=recv_sems.at[k - 1, p], device_id=(px, py, pc), device_id_type=MESH))
        for cp in sends:
            cp.start()
        for cp in recvs:
            cp.wait_recv()
        for cp in sends:
            cp.wait_send()
        for cp in mine:
            cp.wait()

    any_spec = pl.BlockSpec(memory_space=pl.ANY)
    return pl.pallas_call(
        body, name=name, out_shape=[jax.ShapeDtypeStruct(p.shape, p.dtype) for p in parts],
        in_specs=[any_spec] * n, out_specs=[any_spec] * n,
        scratch_shapes=[pltpu.SemaphoreType.DMA((7, n)), pltpu.SemaphoreType.DMA((7, n)), pltpu.SemaphoreType.DMA((n,))],
    )(*parts)


SHARDED = (("sb_w_qkv", 2), ("sb_w_o", 1), ("mla_w_down", 1), ("mla_w_uq", 2), ("mla_w_ukv", 2), ("mla_w_o", 1),
           ("fox_w_qkvf", 2), ("fox_w_o", 1), ("ffn_w_gate", 2), ("ffn_w_up", 2), ("ffn_w_down", 1),
           ("pool_w", 2), ("meta", 1), ("mla_q_norm", 1), ("mla_kv_norm", 1))
KEPT_F32 = ("meta", "mla_q_norm", "mla_kv_norm")
REPLICATED = ("norm_mix", "norm_ffn", "pool_scale", "fox_b_f", "final_norm")
WEIGHT_NAMES = ("meta", "norm_mix", "norm_ffn", "pool_w", "pool_scale", "sb_w_qkv", "sb_w_o", "mla_w_down",
                "mla_q_norm", "mla_kv_norm", "mla_w_uq", "mla_w_ukv", "mla_w_o", "fox_w_qkvf", "fox_b_f",
                "fox_w_o", "ffn_w_gate", "ffn_w_up", "ffn_w_down", "final_norm")
LANES = 1024


def _pack_rows(arrays, names):
    parts = []
    for n in names:
        flat = arrays[n].reshape(-1).astype(F32)
        rows = -(-flat.shape[0] // LANES)
        parts.append(jnp.pad(flat, (0, rows * LANES - flat.shape[0])).reshape(rows, LANES))
    rows = sum(p.shape[0] for p in parts)
    parts.append(jnp.zeros((-(-rows // 8) * 8 - rows, LANES), F32))
    return jnp.concatenate(parts, axis=0)


def _unpack_rows(buf, shapes, names):
    out, row = {}, 0
    for n in names:
        size = int(np.prod(shapes[n]))
        rows = -(-size // LANES)
        out[n] = buf[row:row + rows].reshape(-1)[:size].reshape(shapes[n])
        row += rows
    return out


def _whole_from_gathered(g, axis):
    g = jnp.moveaxis(g, 0, axis)
    shp = g.shape
    return g.reshape(shp[:axis] + (shp[axis] * shp[axis + 1],) + shp[axis + 2:])


def _parts_from_whole(whole, axis):
    shp = whole.shape
    g = whole.reshape(shp[:axis] + (N_DEV, shp[axis] // N_DEV) + shp[axis + 1:])
    return jnp.moveaxis(g, axis, 0)


def _kernel_weights(full):
    W = {}
    W["pool_w"] = full["pool_w"][0]
    W["sb_w_qkv"] = full["sb_w_qkv"][0]
    W["sb_w_o"] = full["sb_w_o"][0]
    W["mla_w_down"] = full["mla_w_down"][0]
    uq = full["mla_w_uq"][0].reshape(MLA_Q_RANK, N_HEADS, MLA_NOPE + MLA_ROPE)
    nope = uq[:, :, :MLA_NOPE].reshape(MLA_Q_RANK, N_HEADS * MLA_NOPE)
    rope = uq[:, :, MLA_NOPE:].reshape(MLA_Q_RANK, N_PAIRS, 2 * MLA_ROPE)
    rope = jnp.pad(rope, ((0, 0), (0, 0), (0, 128 - 2 * MLA_ROPE))).reshape(MLA_Q_RANK, N_PAIRS * 128)
    W["mla_w_uq"] = jnp.concatenate([nope, rope], axis=1)
    ukv = full["mla_w_ukv"][0].reshape(MLA_KV_RANK, N_HEADS, 2, HEAD_DIM)
    W["mla_w_ukv"] = jnp.transpose(ukv, (0, 2, 1, 3)).reshape(MLA_KV_RANK, 2 * N_HEADS * HEAD_DIM)
    W["mla_w_o"] = full["mla_w_o"][0]
    qkvf = full["fox_w_qkvf"][0]
    n_qkv = 3 * N_HEADS * HEAD_DIM
    W["fox_w_qkv"] = qkvf[:, :n_qkv]
    W["fox_w_f"] = jnp.pad(qkvf[:, n_qkv:], ((0, 0), (0, 128 - N_HEADS)))
    W["fox_w_qkvf"] = jnp.concatenate([W["fox_w_qkv"], W["fox_w_f"]], axis=1)
    W["fox_w_o"] = full["fox_w_o"][0]
    W["ffn_w_gu"] = jnp.concatenate([full["ffn_w_gate"], full["ffn_w_up"]], axis=2)
    W["ffn_w_down"] = full["ffn_w_down"]
    return W


def _reference_grads(G):
    out = {}
    out["pool_w"] = G["pool_w"][None]
    for n in ("sb_w_qkv", "sb_w_o", "mla_w_down", "mla_w_o", "fox_w_o"):
        out[n] = G[n][None]
    duq = G["mla_w_uq"]
    nope = duq[:, :N_HEADS * MLA_NOPE].reshape(MLA_Q_RANK, N_HEADS, MLA_NOPE)
    rope = duq[:, N_HEADS * MLA_NOPE:].reshape(MLA_Q_RANK, N_PAIRS, 128)[:, :, :2 * MLA_ROPE]
    rope = rope.reshape(MLA_Q_RANK, N_HEADS, MLA_ROPE)
    out["mla_w_uq"] = jnp.concatenate([nope, rope], axis=2).reshape(1, MLA_Q_RANK, -1)
    dukv = G["mla_w_ukv"].reshape(MLA_KV_RANK, 2, N_HEADS, HEAD_DIM)
    out["mla_w_ukv"] = jnp.transpose(dukv, (0, 2, 1, 3)).reshape(1, MLA_KV_RANK, -1)
    out["fox_w_qkvf"] = G["fox_w_qkvf"][None, :, :3 * N_HEADS * HEAD_DIM + N_HEADS]
    out["ffn_w_gate"] = G["ffn_w_gu"][:, :, :D_FF]
    out["ffn_w_up"] = G["ffn_w_gu"][:, :, D_FF:]
    out["ffn_w_down"] = G["ffn_w_down"]
    out["mla_q_norm"] = G["mla_q_norm"]
    out["mla_kv_norm"] = G["mla_kv_norm"]
    return out


def _pairs_col(f16):
    M = f16.shape[0]
    return jnp.transpose(f16.reshape(M, N_PAIRS, 2), (1, 0, 2))


def _pairs_row(f16):
    M = f16.shape[0]
    return jnp.transpose(f16.reshape(M, N_PAIRS, 2), (1, 2, 0))


def _local_step(x, target, W, P):
    S = x.shape[0]
    M = S + ROW0
    G = {}
    gain = lambda name, i: P[name][i][None, :]
    h0 = jnp.concatenate([jnp.zeros((PAD, D_MODEL), F32), P["meta"], x], axis=0)

    def ffn_fwd(h1, i):
        b = _norm_fwd(h1, gain("norm_ffn", i), BF16, f"ffn{i}_norm")
        g, u, act = _ffn_up(b, W["ffn_w_gu"][i], f"ffn{i}_up")
        h2 = _mm_nn(act, W["ffn_w_down"][i], F32, f"ffn{i}_down", res=h1)
        return h2, (h1, b, g, u, act)

    def ffn_bwd(dh2, saved, i):
        h1, b, g, u, act = saved
        dgu = _ffn_dact(dh2, W["ffn_w_down"][i], g, u, f"ffn{i}_dact")
        G.setdefault("ffn_w_down", {})[i] = _mm_tn(act, dh2, f"ffn{i}_dwd")
        db = _mm_nt(dgu, W["ffn_w_gu"][i], F32, f"ffn{i}_db")
        G.setdefault("ffn_w_gu", {})[i] = _mm_tn(b, dgu, f"ffn{i}_dwgu")
        dh1, dgain = _norm_bwd(h1, gain("norm_ffn", i), db, dh2, f"ffn{i}_dnorm")
        G.setdefault("norm_ffn", {})[i] = dgain
        return dh1

    a0 = _norm_fwd(h0, gain("norm_mix", 0), F32, "mix0_norm")
    h1_0, pooled = _pool_fwd(h0, a0, W["pool_w"], P["pool_scale"], "pool_fwd")
    h_1, ffn0 = ffn_fwd(h1_0, 0)

    sb_scale = HEAD_DIM ** -0.5
    a1 = _norm_fwd(h_1, gain("norm_mix", 1), BF16, "mix1_norm")
    sb_qkv = _mm_nn(a1, W["sb_w_qkv"], BF16, "sb_qkv")
    sb_o, sb_tot = _sb_fwd(sb_qkv, sb_scale, "sb_fwd")
    h1_1 = _mm_nn(sb_o, W["sb_w_o"], F32, "sb_out", res=h_1)
    h_2, ffn1 = ffn_fwd(h1_1, 1)

    mla_scale = (MLA_NOPE + MLA_ROPE) ** -0.5
    cos_t, sin_t = _rope_tables(M)
    a2 = _norm_fwd(h_2, gain("norm_mix", 2), BF16, "mix2_norm")
    down = _mm_nn(a2, W["mla_w_down"], F32, "mla_down")
    dq_raw = down[:, :MLA_Q_RANK]
    dkv_raw = down[:, MLA_Q_RANK:MLA_Q_RANK + MLA_KV_RANK]
    kr_raw = down[:, MLA_Q_RANK + MLA_KV_RANK:]
    c_q = _norm_fwd(dq_raw, P["mla_q_norm"], BF16, "mla_qnorm")
    c_kv = _norm_fwd(dkv_raw, P["mla_kv_norm"], BF16, "mla_kvnorm")
    q_lin = _mm_nn(c_q, W["mla_w_uq"], F32, "mla_uq")
    q_all = _rope(q_lin, cos_t, sin_t, BF16, "mla_qrope", lead=D_MODEL)
    kv_all = _mm_nn(c_kv, W["mla_w_ukv"], BF16, "mla_ukv")
    kr_in = jnp.concatenate([kr_raw, kr_raw, jnp.zeros((M, 64), F32)], axis=1)
    kr = _rope(kr_in, cos_t, sin_t, BF16, "mla_krope")
    q_rope = q_all[:, D_MODEL:]
    mla_o, mla_lse = _mla_fwd(q_all, kv_all, q_rope, kr, mla_scale, "mla_fwd")
    h1_2 = _mm_nn(mla_o, W["mla_w_o"], F32, "mla_out", res=h_2)
    h_3, ffn2 = ffn_fwd(h1_2, 2)

    fox_scale = HEAD_DIM ** -0.5
    a3 = _norm_fwd(h_3, gain("norm_mix", 3), BF16, "mix3_norm")
    fox_qkv = _mm_nn(a3, W["fox_w_qkv"], BF16, "fox_qkv")
    f_logit = _mm_nn(a3, W["fox_w_f"], F32, "fox_f")
    b_f = jnp.pad(P["fox_b_f"], ((0, 0), (0, 128 - N_HEADS)))
    Fc = _forget_cumsum(f_logit, b_f, "fox_cumsum")
    f_rows, f_cols = _pairs_row(Fc[:, :N_HEADS]), _pairs_col(Fc[:, :N_HEADS])
    fox_qkv_t = fox_qkv.T
    fox_o_t, fox_lse, fox_ox_t = _fox_fwd(fox_qkv, fox_qkv_t, f_rows, f_cols, fox_scale, "fox_fwd")
    fox_o = fox_o_t.T
    h1_3 = _mm_nn(fox_o, W["fox_w_o"], F32, "fox_out", res=h_3)
    h_4, ffn3 = ffn_fwd(h1_3, 3)

    sq, dh, dgain = _loss_head(h_4, P["final_norm"][None, :], target, "loss_head")
    G["final_norm"] = dgain[0]

    dh = ffn_bwd(dh, ffn3, 3)
    do = _mm_nt(dh, W["fox_w_o"], BF16, "fox_do")
    G["fox_w_o"] = _mm_tn(fox_o, dh, "fox_dwo")
    dq_t, dk, dv, colsum = _fox_bwd(fox_qkv, fox_qkv_t, fox_ox_t, do, do.T, fox_lse, f_rows, f_cols, fox_scale,
                                    "fox_bwd")
    dF = -jnp.transpose(colsum[:, :, ::HEAD_DIM], (1, 0, 2)).reshape(M, N_HEADS)
    dF = jnp.pad(dF, ((0, 0), (0, 128 - N_HEADS)))
    dlogit, db_f = _forget_cumsum_bwd(f_logit, b_f, dF, "fox_dcumsum")
    G["fox_b_f"] = db_f[:, :N_HEADS]
    dproj = jnp.concatenate([dq_t.T, dk.astype(BF16), dv.astype(BF16), dlogit.astype(BF16)], axis=1)
    da = _mm_nt(dproj, W["fox_w_qkvf"], F32, "fox_da")
    G["fox_w_qkvf"] = _mm_tn(a3, dproj, "fox_dwqkvf")
    dh, dgain = _norm_bwd(h_3, gain("norm_mix", 3), da, dh, "mix3_dnorm")
    G.setdefault("norm_mix", {})[3] = dgain

    dh = ffn_bwd(dh, ffn2, 2)
    do = _mm_nt(dh, W["mla_w_o"], BF16, "mla_do")
    G["mla_w_o"] = _mm_tn(mla_o, dh, "mla_dwo")
    dq, dk, dv, dqr, dkr = _mla_bwd(q_all, kv_all, q_rope, kr, mla_o, do, mla_lse, mla_scale, "mla_bwd")
    dqr = _rope(dqr, cos_t, sin_t, BF16, "mla_dqrope", inverse=True)
    dq_all = jnp.concatenate([dq, dqr], axis=1)
    dkr_sum = _rope(jnp.sum(dkr, axis=0), cos_t, sin_t, F32, "mla_dkrope", inverse=True)
    dkr_raw = dkr_sum[:, :MLA_ROPE] + dkr_sum[:, MLA_ROPE:2 * MLA_ROPE]
    dkv_all = jnp.concatenate([dk.astype(BF16), dv.astype(BF16)], axis=1)
    dc_q = _mm_nt(dq_all, W["mla_w_uq"], F32, "mla_dcq")
    G["mla_w_uq"] = _mm_tn(c_q, dq_all, "mla_dwuq")
    dc_kv = _mm_nt(dkv_all, W["mla_w_ukv"], F32, "mla_dckv")
    G["mla_w_ukv"] = _mm_tn(c_kv, dkv_all, "mla_dwukv")
    ddq_raw, G["mla_q_norm"] = _norm_bwd(dq_raw, P["mla_q_norm"], dc_q, None, "mla_dqnorm")
    ddkv_raw, G["mla_kv_norm"] = _norm_bwd(dkv_raw, P["mla_kv_norm"], dc_kv, None, "mla_dkvnorm")
    ddown = jnp.concatenate([ddq_raw, ddkv_raw, dkr_raw], axis=1).astype(BF16)
    da = _mm_nt(ddown, W["mla_w_down"], F32, "mla_da")
    G["mla_w_down"] = _mm_tn(a2, ddown, "mla_dwdown")
    dh, dgain = _norm_bwd(h_2, gain("norm_mix", 2), da, dh, "mix2_dnorm")
    G["norm_mix"][2] = dgain

    dh = ffn_bwd(dh, ffn1, 1)
    do = _mm_nt(dh, W["sb_w_o"], BF16, "sb_do")
    G["sb_w_o"] = _mm_tn(sb_o, dh, "sb_dwo")
    dq, dk, dv = _sb_bwd(sb_qkv, do, sb_tot, sb_scale, "sb_bwd")
    dqkv = jnp.concatenate([dq, dk.astype(BF16), dv.astype(BF16)], axis=1)
    da = _mm_nt(dqkv, W["sb_w_qkv"], F32, "sb_da")
    G["sb_w_qkv"] = _mm_tn(a1, dqkv, "sb_dwqkv")
    dh, dgain = _norm_bwd(h_1, gain("norm_mix", 1), da, dh, "mix1_dnorm")
    G["norm_mix"][1] = dgain

    dh = ffn_bwd(dh, ffn0, 0)
    dpc, G["pool_w"], G["pool_scale"] = _pool_bwd_mix(dh, pooled, W["pool_w"], P["pool_scale"], "pool_dmix")
    da = _pool_bwd_window(dpc, "pool_dwindow")
    dh, dgain, dx = _norm_bwd(h0, gain("norm_mix", 0), da, dh, "mix0_dnorm", token_rows=True)
    G["norm_mix"][0] = dgain

    G["norm_mix"] = jnp.concatenate([G["norm_mix"][i] for i in range(DEPTH)], axis=0)
    G["norm_ffn"] = jnp.concatenate([G["norm_ffn"][i] for i in range(DEPTH)], axis=0)
    G["ffn_w_down"] = jnp.stack([G["ffn_w_down"][i] for i in range(DEPTH)])
    G["ffn_w_gu"] = jnp.stack([G["ffn_w_gu"][i] for i in range(DEPTH)])
    G["meta"] = dh[PAD:ROW0]
    return sq, dx, G


def kernel(x, meta, norm_mix, norm_ffn, pool_w, pool_scale, sb_w_qkv, sb_w_o, mla_w_down, mla_q_norm, mla_kv_norm, mla_w_uq, mla_w_ukv, mla_w_o, fox_w_qkvf, fox_b_f, fox_w_o, ffn_w_gate, ffn_w_up, ffn_w_down, final_norm, loss_target, m_meta, m_norm_mix, m_norm_ffn, m_pool_w, m_pool_scale, m_sb_w_qkv, m_sb_w_o, m_mla_w_down, m_mla_q_norm, m_mla_kv_norm, m_mla_w_uq, m_mla_w_ukv, m_mla_w_o, m_fox_w_qkvf, m_fox_b_f, m_fox_w_o, m_ffn_w_gate, m_ffn_w_up, m_ffn_w_down, m_final_norm, v_meta, v_norm_mix, v_norm_ffn, v_pool_w, v_pool_scale, v_sb_w_qkv, v_sb_w_o, v_mla_w_down, v_mla_q_norm, v_mla_kv_norm, v_mla_w_uq, v_mla_w_ukv, v_mla_w_o, v_fox_w_qkvf, v_fox_b_f, v_fox_w_o, v_ffn_w_gate, v_ffn_w_up, v_ffn_w_down, v_final_norm):
    w = dict(meta=meta, norm_mix=norm_mix, norm_ffn=norm_ffn, pool_w=pool_w, pool_scale=pool_scale,
             sb_w_qkv=sb_w_qkv, sb_w_o=sb_w_o, mla_w_down=mla_w_down, mla_q_norm=mla_q_norm,
             mla_kv_norm=mla_kv_norm, mla_w_uq=mla_w_uq, mla_w_ukv=mla_w_ukv, mla_w_o=mla_w_o,
             fox_w_qkvf=fox_w_qkvf, fox_b_f=fox_b_f, fox_w_o=fox_w_o, ffn_w_gate=ffn_w_gate, ffn_w_up=ffn_w_up,
             ffn_w_down=ffn_w_down, final_norm=final_norm)
    m = dict(meta=m_meta, norm_mix=m_norm_mix, norm_ffn=m_norm_ffn, pool_w=m_pool_w, pool_scale=m_pool_scale,
             sb_w_qkv=m_sb_w_qkv, sb_w_o=m_sb_w_o, mla_w_down=m_mla_w_down, mla_q_norm=m_mla_q_norm,
             mla_kv_norm=m_mla_kv_norm, mla_w_uq=m_mla_w_uq, mla_w_ukv=m_mla_w_ukv, mla_w_o=m_mla_w_o,
             fox_w_qkvf=m_fox_w_qkvf, fox_b_f=m_fox_b_f, fox_w_o=m_fox_w_o, ffn_w_gate=m_ffn_w_gate,
             ffn_w_up=m_ffn_w_up, ffn_w_down=m_ffn_w_down, final_norm=m_final_norm)
    v = dict(meta=v_meta, norm_mix=v_norm_mix, norm_ffn=v_norm_ffn, pool_w=v_pool_w, pool_scale=v_pool_scale,
             sb_w_qkv=v_sb_w_qkv, sb_w_o=v_sb_w_o, mla_w_down=v_mla_w_down, mla_q_norm=v_mla_q_norm,
             mla_kv_norm=v_mla_kv_norm, mla_w_uq=v_mla_w_uq, mla_w_ukv=v_mla_w_ukv, mla_w_o=v_mla_w_o,
             fox_w_qkvf=v_fox_w_qkvf, fox_b_f=v_fox_b_f, fox_w_o=v_fox_w_o, ffn_w_gate=v_ffn_w_gate,
             ffn_w_up=v_ffn_w_up, ffn_w_down=v_ffn_w_down, final_norm=v_final_norm)

    sh_names = tuple(n for n, _ in SHARDED)
    sh_axis = dict(SHARDED)
    shapes = {n: w[n].shape for n in WEIGHT_NAMES}
    wire = lambda n: F32 if n in KEPT_F32 else BF16

    gathered = _all_gather([w[n].astype(wire(n)) for n in sh_names], "gather_weights")
    full = {n: _whole_from_gathered(g, sh_axis[n]) for n, g in zip(sh_names, gathered)}
    W = _kernel_weights(full)
    P = dict(meta=full["meta"], mla_q_norm=full["mla_q_norm"], mla_kv_norm=full["mla_kv_norm"],
             norm_mix=norm_mix, norm_ffn=norm_ffn, pool_scale=pool_scale, fox_b_f=fox_b_f, final_norm=final_norm)

    sq, dx, G = _local_step(x[0], loss_target[0], W, P)
    loss = lax.psum(0.5 * jnp.sum(sq) / D_MODEL, ("x", "y", "c"))
    grad_x = dx[None]

    gw = _reference_grads(G)
    gw["meta"] = G["meta"]
    parts = [_parts_from_whole(gw[n], sh_axis[n]).astype(wire(n)) for n in sh_names]
    landed = _exchange(parts, "exchange_grads")
    results = {}
    for n, got in zip(sh_names, landed):
        rc = (int(np.prod(shapes[n][:-1])), shapes[n][-1])
        outs = _adamw(w[n].reshape(rc), got.reshape((N_DEV,) + rc), m[n].reshape(rc), v[n].reshape(rc), f"adamw_{n}")
        results[n] = [o.reshape(shapes[n]) for o in outs]

    rep_g = dict(norm_mix=G["norm_mix"], norm_ffn=G["norm_ffn"], pool_scale=G["pool_scale"], fox_b_f=G["fox_b_f"],
                 final_norm=G["final_norm"])
    (rep_all,) = _all_gather([_pack_rows(rep_g, REPLICATED)], "gather_replicated_grads")
    rep_out = _adamw(_pack_rows(w, REPLICATED), rep_all, _pack_rows(m, REPLICATED), _pack_rows(v, REPLICATED),
                     "adamw_replicated")
    rep = [_unpack_rows(o, shapes, REPLICATED) for o in rep_out]
    for n in REPLICATED:
        results[n] = [r[n] for r in rep]

    outs = [results[n][k] for k in range(4) for n in WEIGHT_NAMES]
    return (loss, grad_x, *outs)
```

```python
import numpy as np
import jax
import jax.numpy as jnp
from jax import lax
from jax.experimental import pallas as pl
from jax.experimental.pallas import tpu as pltpu

F32 = jnp.float32
BF16 = jnp.bfloat16

N_DEV = 8
D_MODEL = 1024
N_META = 16
PAD = 240
ROW0 = PAD + N_META
EPS = 1e-6
POOL_WINDOWS = (2, 4, 8, 16)
POOL_GROUP = 256
HALO = 128
N_HEADS = 16
HEAD_DIM = 64
N_PAIRS = N_HEADS // 2
MLA_Q_RANK = 384
MLA_KV_RANK = 256
MLA_NOPE = 64
MLA_ROPE = 32
ROPE_THETA = 10000.0
D_FF = 2816
DEPTH = 4
ATTN_TILE = 768
ATTN_BWD_TILE = 768
WALK_TILE = 256
NEG = -1e30
EXP_ZERO = -110.0
VMEM_LIMIT = 56 * 2**20
ADAM_TILE_ELEMS = 192 * 1024

ADAM_LR = 0.001
ADAM_B1 = 0.9
ADAM_B2 = 0.999
ADAM_EPS = 1e-08
ADAM_WD = 0.01
ADAM_STEP = 10

MESH = pl.DeviceIdType.MESH


def _params(sem=None):
    return pltpu.CompilerParams(dimension_semantics=sem, vmem_limit_bytes=VMEM_LIMIT)


def _pick(n, cands):
    for c in cands:
        if n % c == 0:
            return c
    return n


def _col_tile(n, cap=1536):
    best = None
    for t in range(128, min(n, cap) + 1, 128):
        if n % t == 0:
            best = t
    return best if best is not None else n


def _dot(a, b):
    return jnp.dot(a, b, preferred_element_type=F32)


def _dot_nt(a, b):
    return lax.dot_general(a, b, (((1,), (1,)), ((), ())), preferred_element_type=F32)


def _dot_tn(a, b):
    return lax.dot_general(a, b, (((0,), (0,)), ((), ())), preferred_element_type=F32)


def _mm_nn(a, b, out_dtype, name, res=None):
    M, K = a.shape
    N = b.shape[1]
    tm = _pick(M, (768, 512, 256, 128))
    tn = _col_tile(N)

    def body(*refs):
        if res is None:
            a_ref, b_ref, o_ref = refs
        else:
            a_ref, b_ref, r_ref, o_ref = refs
        acc = _dot(a_ref[...].astype(BF16), b_ref[...])
        if res is not None:
            acc = acc + r_ref[...]
        o_ref[...] = acc.astype(o_ref.dtype)

    in_specs = [pl.BlockSpec((tm, K), lambda n, m: (m, 0)), pl.BlockSpec((K, tn), lambda n, m: (0, n))]
    args = [a, b]
    if res is not None:
        in_specs.append(pl.BlockSpec((tm, tn), lambda n, m: (m, n)))
        args.append(res)
    return pl.pallas_call(
        body, name=name, grid=(N // tn, M // tm), in_specs=in_specs,
        out_specs=pl.BlockSpec((tm, tn), lambda n, m: (m, n)),
        out_shape=jax.ShapeDtypeStruct((M, N), out_dtype),
        compiler_params=_params(("parallel", "parallel")))(*args)


def _mm_nt(a, w, out_dtype, name):
    M, N = a.shape
    K = w.shape[0]
    tm = _pick(M, (768, 512, 256, 128)) if N <= 3200 else _pick(M, (256, 128))
    tk = _col_tile(K, 1024)

    def body(a_ref, w_ref, o_ref):
        o_ref[...] = _dot_nt(a_ref[...].astype(BF16), w_ref[...]).astype(o_ref.dtype)

    return pl.pallas_call(
        body, name=name, grid=(K // tk, M // tm),
        in_specs=[pl.BlockSpec((tm, N), lambda k, m: (m, 0)), pl.BlockSpec((tk, N), lambda k, m: (k, 0))],
        out_specs=pl.BlockSpec((tm, tk), lambda k, m: (m, k)),
        out_shape=jax.ShapeDtypeStruct((M, K), out_dtype),
        compiler_params=_params(("parallel", "parallel")))(a, w)


def _mm_nt2(a1, w1, a2, w2, out_dtype, name):
    M, N = a1.shape
    K = w1.shape[0]
    tm = _pick(M, (384, 256, 128))

    def body(a1_ref, w1_ref, a2_ref, w2_ref, o_ref):
        o_ref[...] = (_dot_nt(a1_ref[...], w1_ref[...]) + _dot_nt(a2_ref[...], w2_ref[...])).astype(o_ref.dtype)

    a_spec = pl.BlockSpec((tm, N), lambda m: (m, 0))
    w_spec = pl.BlockSpec((K, N), lambda m: (0, 0))
    return pl.pallas_call(
        body, name=name, grid=(M // tm,), in_specs=[a_spec, w_spec, a_spec, w_spec],
        out_specs=pl.BlockSpec((tm, K), lambda m: (m, 0)),
        out_shape=jax.ShapeDtypeStruct((M, K), out_dtype),
        compiler_params=_params(("parallel",)))(a1, w1, a2, w2)


def _mm_tn(a, b, name):
    M, K = a.shape
    N = b.shape[1]
    tm = _pick(M, (768, 512, 256, 128))
    tk = _col_tile(K, 1408)
    tn = _col_tile(N, 1408)

    def body(a_ref, b_ref, o_ref):
        @pl.when(pl.program_id(2) == 0)
        def _():
            o_ref[...] = jnp.zeros_like(o_ref)
        o_ref[...] += _dot_tn(a_ref[...].astype(BF16), b_ref[...].astype(BF16))

    return pl.pallas_call(
        body, name=name, grid=(K // tk, N // tn, M // tm),
        in_specs=[pl.BlockSpec((tm, tk), lambda k, n, m: (m, k)), pl.BlockSpec((tm, tn), lambda k, n, m: (m, n))],
        out_specs=pl.BlockSpec((tk, tn), lambda k, n, m: (k, n)),
        out_shape=jax.ShapeDtypeStruct((K, N), F32),
        compiler_params=_params(("parallel", "parallel", "arbitrary")))(a, b)


def _norm_fwd(h, gain, out_dtype, name):
    M, C = h.shape
    tm = _pick(M, (768, 512, 256, 128))

    def body(h_ref, g_ref, a_ref):
        x = h_ref[...]
        r = lax.rsqrt(jnp.mean(x * x, axis=-1, keepdims=True) + EPS)
        a_ref[...] = ((x * r) * g_ref[...]).astype(a_ref.dtype)

    return pl.pallas_call(
        body, name=name, grid=(M // tm,),
        in_specs=[pl.BlockSpec((tm, C), lambda m: (m, 0)), pl.BlockSpec((1, C), lambda m: (0, 0))],
        out_specs=pl.BlockSpec((tm, C), lambda m: (m, 0)),
        out_shape=jax.ShapeDtypeStruct((M, C), out_dtype),
        compiler_params=_params(("parallel",)))(h, gain)


def _norm_bwd(h, gain, da, dres, name, token_rows=False):
    M, C = h.shape
    tm = ROW0 if token_rows else _pick(M, (768, 512, 256, 128))

    def body(*refs):
        refs = list(refs)
        dx_ref = refs.pop() if token_rows else None
        if dres is None:
            h_ref, g_ref, da_ref, dh_ref, dg_ref = refs
        else:
            h_ref, g_ref, da_ref, dr_ref, dh_ref, dg_ref = refs
        x = h_ref[...]
        r = lax.rsqrt(jnp.mean(x * x, axis=-1, keepdims=True) + EPS)
        y = x * r
        dav = da_ref[...].astype(F32)
        dy = dav * g_ref[...]
        dh = r * (dy - y * jnp.mean(dy * y, axis=-1, keepdims=True))
        if dres is not None:
            dh = dh + dr_ref[...]
        dh_ref[...] = dh
        if token_rows:
            dx_ref[...] = dh

        @pl.when(pl.program_id(0) == 0)
        def _():
            dg_ref[...] = jnp.zeros_like(dg_ref)
        dg_ref[...] += jnp.sum(dav * y, axis=0, keepdims=True)

    row = pl.BlockSpec((tm, C), lambda m: (m, 0))
    vec = pl.BlockSpec((1, C), lambda m: (0, 0))
    in_specs = [row, vec, row] + ([row] if dres is not None else [])
    args = [h, gain, da] + ([dres] if dres is not None else [])
    out_specs = [row, vec]
    out_shape = [jax.ShapeDtypeStruct((M, C), F32), jax.ShapeDtypeStruct((1, C), F32)]
    if token_rows:
        out_specs.append(pl.BlockSpec((tm, C), lambda m: (jnp.maximum(m - 1, 0), 0)))
        out_shape.append(jax.ShapeDtypeStruct((M - ROW0, C), F32))
    return pl.pallas_call(
        body, name=name, grid=(M // tm,), in_specs=in_specs, out_specs=out_specs, out_shape=out_shape,
        compiler_params=_params(("arbitrary",)))(*args)


def _ffn_up(b, w_g, w_u, name):
    M, K = b.shape
    F = w_g.shape[1]
    tm = _pick(M, (768, 512, 256, 128))
    tn = _col_tile(F, 1408)
    nb = F // tn

    def body(b_ref, wg_ref, wu_ref, g_ref, u_ref, act_ref):
        x = b_ref[...]
        g = _dot(x, wg_ref[...])
        u = _dot(x, wu_ref[...])
        g_ref[...] = g
        u_ref[...] = u
        act_ref[...] = ((g * jax.nn.sigmoid(g)) * u).astype(act_ref.dtype)

    blk = pl.BlockSpec((tm, tn), lambda n, m: (m, n))
    return pl.pallas_call(
        body, name=name, grid=(nb, M // tm),
        in_specs=[pl.BlockSpec((tm, K), lambda n, m: (m, 0)),
                  pl.BlockSpec((K, tn), lambda n, m: (0, n)),
                  pl.BlockSpec((K, tn), lambda n, m: (0, n))],
        out_specs=[blk, blk, blk],
        out_shape=[jax.ShapeDtypeStruct((M, F), F32), jax.ShapeDtypeStruct((M, F), F32),
                   jax.ShapeDtypeStruct((M, F), BF16)],
        compiler_params=_params(("parallel", "parallel")))(b, w_g, w_u)


def _ffn_dact(dy, w_d, g, u, name):
    M, K = dy.shape
    F = w_d.shape[0]
    tm = _pick(M, (768, 512, 256, 128))
    tn = _col_tile(F, 1408)
    nb = F // tn

    def body(dy_ref, wd_ref, g_ref, u_ref, dg_ref, du_ref):
        dact = _dot_nt(dy_ref[...].astype(BF16), wd_ref[...])
        gv = g_ref[...]
        s = jax.nn.sigmoid(gv)
        silu = gv * s
        dg_ref[...] = (dact * u_ref[...] * (s * (1.0 + gv * (1.0 - s)))).astype(dg_ref.dtype)
        du_ref[...] = (dact * silu).astype(du_ref.dtype)

    blk = pl.BlockSpec((tm, tn), lambda n, m: (m, n))
    return pl.pallas_call(
        body, name=name, grid=(nb, M // tm),
        in_specs=[pl.BlockSpec((tm, K), lambda n, m: (m, 0)), pl.BlockSpec((tn, K), lambda n, m: (n, 0)), blk, blk],
        out_specs=[blk, blk],
        out_shape=[jax.ShapeDtypeStruct((M, F), BF16), jax.ShapeDtypeStruct((M, F), BF16)],
        compiler_params=_params(("parallel", "parallel")))(dy, w_d, g, u)


def _loss_head(h, gain, target, name):
    M, C = h.shape
    tm = ROW0
    assert M % tm == 0 and target.shape[0] == M - ROW0

    def body(h_ref, g_ref, t_ref, sq_ref, dh_ref, dg_ref):
        i = pl.program_id(0)

        @pl.when(i == 0)
        def _():
            sq_ref[...] = jnp.zeros_like(sq_ref)
            dg_ref[...] = jnp.zeros_like(dg_ref)
            dh_ref[...] = jnp.zeros_like(dh_ref)

        @pl.when(i > 0)
        def _():
            x = h_ref[...]
            r = lax.rsqrt(jnp.mean(x * x, axis=-1, keepdims=True) + EPS)
            y = x * r
            err = y * g_ref[...] - t_ref[...]
            sq_ref[...] += jnp.sum(err * err, axis=0, keepdims=True)
            da = err * (1.0 / C)
            dy = da * g_ref[...]
            dh_ref[...] = r * (dy - y * jnp.mean(dy * y, axis=-1, keepdims=True))
            dg_ref[...] += jnp.sum(da * y, axis=0, keepdims=True)

    row = pl.BlockSpec((tm, C), lambda m: (m, 0))
    vec = pl.BlockSpec((1, C), lambda m: (0, 0))
    return pl.pallas_call(
        body, name=name, grid=(M // tm,),
        in_specs=[row, vec, pl.BlockSpec((tm, C), lambda m: (jnp.maximum(m - 1, 0), 0))],
        out_specs=[vec, row, vec],
        out_shape=[jax.ShapeDtypeStruct((1, C), F32), jax.ShapeDtypeStruct((M, C), F32),
                   jax.ShapeDtypeStruct((1, C), F32)],
        compiler_params=_params(("arbitrary",)))(h, gain, target)


def _pool_pos(row0, tm):
    return row0 + lax.broadcasted_iota(jnp.int32, (tm, 1), 0) - PAD


def _pool_fwd(h, a, w, scale, name):
    M, C = a.shape
    tm = 256
    hb = tm // HALO

    def body(h_ref, a_ref, halo_ref, w_ref, s_ref, o_ref, p_ref):
        i = pl.program_id(0)
        row0 = i * tm
        ext = jnp.concatenate([halo_ref[...], a_ref[...]], axis=0)
        src = row0 - HALO + lax.broadcasted_iota(jnp.int32, (tm + HALO, 1), 0)
        ext = jnp.where(src >= PAD, ext, 0.0)
        r = lax.broadcasted_iota(jnp.int32, (tm, tm + HALO), 0)
        c = lax.broadcasted_iota(jnp.int32, (tm, tm + HALO), 1)
        pos = _pool_pos(row0, tm)
        for g, win in enumerate(POOL_WINDOWS):
            band = ((c <= r + HALO) & (c > r + HALO - win)).astype(F32)
            cols = slice(g * POOL_GROUP, (g + 1) * POOL_GROUP)
            xg = ext[:, cols]
            tot = jnp.dot(band, xg, precision=lax.Precision.HIGHEST, preferred_element_type=F32)
            cnt = jnp.clip(pos + 1, 1, win).astype(F32)
            pooled = (tot / cnt - xg[HALO:]).astype(BF16)
            p_ref[:, cols] = pooled
            mixed = _dot(pooled, w_ref[g])
            o_ref[:, cols] = h_ref[:, cols] + mixed * s_ref[:, cols]

    row = pl.BlockSpec((tm, C), lambda m: (m, 0))
    return pl.pallas_call(
        body, name=name, grid=(M // tm,),
        in_specs=[row, row, pl.BlockSpec((HALO, C), lambda m: (jnp.maximum(m * hb - 1, 0), 0)),
                  pl.BlockSpec((4, POOL_GROUP, POOL_GROUP), lambda m: (0, 0, 0)),
                  pl.BlockSpec((1, C), lambda m: (0, 0))],
        out_specs=[row, row],
        out_shape=[jax.ShapeDtypeStruct((M, C), F32), jax.ShapeDtypeStruct((M, C), BF16)],
        compiler_params=_params(("parallel",)))(h, a, a, w, scale)


def _pool_bwd_mix(dout, pooled, w, scale, name):
    M, C = dout.shape
    tm = 256

    def body(do_ref, p_ref, w_ref, s_ref, dpc_ref, dw_ref, ds_ref):
        i = pl.program_id(0)

        @pl.when(i == 0)
        def _():
            dw_ref[...] = jnp.zeros_like(dw_ref)
            ds_ref[...] = jnp.zeros_like(ds_ref)

        pos = _pool_pos(i * tm, tm)
        for g, win in enumerate(POOL_WINDOWS):
            cols = slice(g * POOL_GROUP, (g + 1) * POOL_GROUP)
            do = do_ref[:, cols]
            pooled = p_ref[:, cols]
            mixed = _dot(pooled, w_ref[g])
            ds_ref[:, cols] += jnp.sum(do * mixed, axis=0, keepdims=True)
            dmix = (do * s_ref[:, cols]).astype(BF16)
            dw_ref[g] += _dot_tn(pooled, dmix)
            dp = _dot_nt(dmix, w_ref[g])
            cnt = jnp.clip(pos + 1, 1, win).astype(F32)
            dpc_ref[:, cols] = dp / cnt

    row = pl.BlockSpec((tm, C), lambda m: (m, 0))
    wspec = pl.BlockSpec((4, POOL_GROUP, POOL_GROUP), lambda m: (0, 0, 0))
    vec = pl.BlockSpec((1, C), lambda m: (0, 0))
    return pl.pallas_call(
        body, name=name, grid=(M // tm,),
        in_specs=[row, row, wspec, vec], out_specs=[row, wspec, vec],
        out_shape=[jax.ShapeDtypeStruct((M, C), F32), jax.ShapeDtypeStruct((4, POOL_GROUP, POOL_GROUP), F32),
                   jax.ShapeDtypeStruct((1, C), F32)],
        compiler_params=_params(("arbitrary",)))(dout, pooled, w, scale)


def _pool_bwd_window(dpc, name):
    M, C = dpc.shape
    tm = 256
    hb = tm // HALO
    last = M // HALO - 1

    def body(d_ref, halo_ref, da_ref):
        i = pl.program_id(0)
        row0 = i * tm
        ext = jnp.concatenate([d_ref[...], halo_ref[...]], axis=0)
        src = row0 + lax.broadcasted_iota(jnp.int32, (tm + HALO, 1), 0)
        ext = jnp.where(src < M, ext, 0.0)
        r = lax.broadcasted_iota(jnp.int32, (tm, tm + HALO), 0)
        c = lax.broadcasted_iota(jnp.int32, (tm, tm + HALO), 1)
        pos = _pool_pos(row0, tm)
        for g, win in enumerate(POOL_WINDOWS):
            band = ((c >= r) & (c < r + win)).astype(F32)
            cols = slice(g * POOL_GROUP, (g + 1) * POOL_GROUP)
            xg = ext[:, cols]
            tot = jnp.dot(band, xg, precision=lax.Precision.HIGHEST, preferred_element_type=F32)
            cnt = jnp.clip(pos + 1, 1, win).astype(F32)
            da_ref[:, cols] = jnp.where(pos >= 0, tot - xg[:tm] * cnt, 0.0)

    row = pl.BlockSpec((tm, C), lambda m: (m, 0))
    return pl.pallas_call(
        body, name=name, grid=(M // tm,),
        in_specs=[row, pl.BlockSpec((HALO, C), lambda m: (jnp.minimum((m + 1) * hb, last), 0))],
        out_specs=row, out_shape=jax.ShapeDtypeStruct((M, C), F32),
        compiler_params=_params(("parallel",)))(dpc, dpc)


def _head_masks():
    lane = lax.broadcasted_iota(jnp.int32, (1, 128), 1)
    return lane < HEAD_DIM, lane


def _split_heads(x, first):
    z = jnp.zeros_like(x)
    return jnp.where(first, x, z), jnp.where(first, z, x)


def _split_rope(x, lane):
    z = jnp.zeros_like(x)
    return jnp.where(lane < MLA_ROPE, x, z), jnp.where((lane >= MLA_ROPE) & (lane < 2 * MLA_ROPE), x, z)


def _walk_causal(i, step):
    def mid(kb, carry):
        step(kb, False)
        return carry

    step(0, True)
    lax.fori_loop(1, i, mid, 0)

    @pl.when(i > 0)
    def _():
        step(i, True)


def _mla_fwd(q_all, kv_all, qr, kr, scale, name):
    M = q_all.shape[0]
    t = ATTN_TILE

    def body(q_ref, k_ref, v_ref, qr_ref, kr_ref, o_ref, lse_ref, m_s, l_s, acc_s):
        i = pl.program_id(1)
        first, lane = _head_masks()
        qs = _split_heads(q_ref[...], first)
        qrs = _split_rope(qr_ref[...], lane)
        qcat = tuple(jnp.concatenate([qs[hh], qrs[hh]], axis=1) for hh in range(2))
        m_s[...] = jnp.full_like(m_s, NEG)
        l_s[...] = jnp.zeros_like(l_s)
        acc_s[...] = jnp.zeros_like(acc_s)
        qpos = i * t + lax.broadcasted_iota(jnp.int32, (t, t), 0)
        kidx = lax.broadcasted_iota(jnp.int32, (t, t), 1)

        def step(kb, masked):
            k0 = pl.multiple_of(kb * t, t)
            kcat = jnp.concatenate([k_ref[pl.ds(k0, t), :], kr_ref[pl.ds(k0, t), :]], axis=1)
            vs = _split_heads(v_ref[pl.ds(k0, t), :], first)
            if masked:
                kpos = k0 + kidx
                valid = (kpos <= qpos) & (kpos >= PAD)
            pv = None
            alphas = []
            for hh in range(2):
                s = _dot_nt(qcat[hh], kcat) * scale
                if masked:
                    s = jnp.where(valid, s, NEG)
                m_old = m_s[hh]
                m_new = jnp.maximum(m_old, jnp.max(s, axis=1, keepdims=True))
                p = jnp.exp(s - m_new)
                alpha = jnp.exp(m_old - m_new)
                l_s[hh] = alpha * l_s[hh] + jnp.sum(p, axis=1, keepdims=True)
                m_s[hh] = m_new
                d = _dot(p.astype(BF16), vs[hh])
                pv = d if pv is None else pv + d
                alphas.append(alpha)
            acc_s[...] = acc_s[...] * jnp.where(first, alphas[0], alphas[1]) + pv

        _walk_causal(i, step)
        o_ref[...] = (acc_s[...] * jnp.where(first, 1.0 / l_s[0], 1.0 / l_s[1])).astype(o_ref.dtype)
        lse_ref[:, 0:1] = m_s[0] + jnp.log(l_s[0])
        lse_ref[:, 1:2] = m_s[1] + jnp.log(l_s[1])

    blk = pl.BlockSpec((t, 128), lambda j, i: (i, j))
    return pl.pallas_call(
        body, name=name, grid=(N_PAIRS, M // t),
        in_specs=[blk, pl.BlockSpec((M, 128), lambda j, i: (0, j)), pl.BlockSpec((M, 128), lambda j, i: (0, N_PAIRS + j)),
                  blk, pl.BlockSpec((M, 128), lambda j, i: (0, 0))],
        out_specs=[blk, pl.BlockSpec((None, t, 2), lambda j, i: (j, i, 0))],
        out_shape=[jax.ShapeDtypeStruct((M, N_PAIRS * 128), BF16), jax.ShapeDtypeStruct((N_PAIRS, M, 2), F32)],
        scratch_shapes=[pltpu.VMEM((2, t, 1), F32), pltpu.VMEM((2, t, 1), F32), pltpu.VMEM((t, 128), F32)],
        compiler_params=_params(("parallel", "arbitrary")))(q_all, kv_all, kv_all, qr, kr)


def _mla_bwd(q_all, kv_all, qr, kr, o, do, lse, scale, name):
    M = q_all.shape[0]
    t = ATTN_BWD_TILE

    def body(q_ref, kv_hbm, qr_ref, kr_hbm, o_ref, do_ref, lse_ref,
             dq_ref, dk_hbm, dv_hbm, dqr_ref, dkr_hbm,
             k_ref, v_ref, kr_ref, dk_ref, dv_ref, dkr_ref, dq_s, lse_s, delta_s):
        j = pl.program_id(0)
        i = pl.program_id(1)
        first, lane = _head_masks()
        every = pl.ds(0, M)
        kcols = pl.ds(pl.multiple_of(j * 128, 128), 128)
        vcols = pl.ds(pl.multiple_of((N_PAIRS + j) * 128, 128), 128)

        @pl.when(i == 0)
        def _():
            pltpu.sync_copy(kv_hbm.at[every, kcols], k_ref)
            pltpu.sync_copy(kv_hbm.at[every, vcols], v_ref)
            pltpu.sync_copy(kr_hbm, kr_ref)
            dk_ref[...] = jnp.zeros_like(dk_ref)
            dv_ref[...] = jnp.zeros_like(dv_ref)
            dkr_ref[...] = jnp.zeros_like(dkr_ref)

        qs = _split_heads(q_ref[...], first)
        qrs = _split_rope(qr_ref[...], lane)
        qcat = tuple(jnp.concatenate([qs[hh], qrs[hh]], axis=1) for hh in range(2))
        dov = do_ref[...]
        dos = _split_heads(dov, first)
        prod = dov.astype(F32) * o_ref[...].astype(F32)
        deltas = (jnp.sum(jnp.where(first, prod, 0.0), axis=1, keepdims=True),
                  jnp.sum(jnp.where(first, 0.0, prod), axis=1, keepdims=True))
        for hh in range(2):
            lse_s[hh] = jnp.broadcast_to(lse_ref[:, hh:hh + 1], (t, t))
            delta_s[hh] = jnp.broadcast_to(deltas[hh], (t, t))
        dq_s[...] = jnp.zeros_like(dq_s)
        qpos = i * t + lax.broadcasted_iota(jnp.int32, (t, t), 0)
        kidx = lax.broadcasted_iota(jnp.int32, (t, t), 1)

        def step(kb, masked):
            k0 = pl.multiple_of(kb * t, t)
            rows = pl.ds(k0, t)
            k = k_ref[rows, :]
            v = v_ref[rows, :]
            kr = kr_ref[rows, :]
            kcat = jnp.concatenate([k, kr], axis=1)
            ks = _split_heads(k, first)
            krs = _split_rope(kr, lane)
            if masked:
                kpos = k0 + kidx
                valid = (kpos <= qpos) & (kpos >= PAD)
            dq = dk = dv = None
            for hh in range(2):
                s = _dot_nt(qcat[hh], kcat) * scale
                if masked:
                    s = jnp.where(valid, s, NEG)
                p = jnp.exp(s - lse_s[hh])
                ds = p * (_dot_nt(dos[hh], v) - delta_s[hh])
                dsb = (ds * scale).astype(BF16)
                a = _dot(dsb, jnp.concatenate([ks[hh], krs[hh]], axis=1))
                b = _dot_tn(dsb, qcat[hh])
                c = _dot_tn(p.astype(BF16), dos[hh])
                dq = a if dq is None else dq + a
                dk = b if dk is None else dk + b
                dv = c if dv is None else dv + c
            dq_s[...] += dq
            dk_ref[rows, :] += dk[:, :128]
            dkr_ref[rows, :] += dk[:, 128:]
            dv_ref[rows, :] += dv

        _walk_causal(i, step)
        dq_ref[...] = dq_s[:, :128].astype(dq_ref.dtype)
        dqr_ref[...] = dq_s[:, 128:].astype(dqr_ref.dtype)

        @pl.when(i == M // t - 1)
        def _():
            pltpu.sync_copy(dk_ref, dk_hbm.at[every, kcols])
            pltpu.sync_copy(dv_ref, dv_hbm.at[every, kcols])
            pltpu.sync_copy(dkr_ref, dkr_hbm.at[j])

    blk = pl.BlockSpec((t, 128), lambda j, i: (i, j))
    whole = pl.BlockSpec(memory_space=pl.ANY)
    wide = jax.ShapeDtypeStruct((M, N_PAIRS * 128), F32)
    slab = lambda dtype: pltpu.VMEM((M, 128), dtype)
    return pl.pallas_call(
        body, name=name, grid=(N_PAIRS, M // t),
        in_specs=[blk, whole, blk, whole, blk, blk, pl.BlockSpec((None, t, 2), lambda j, i: (j, i, 0))],
        out_specs=[blk, whole, whole, blk, whole],
        out_shape=[jax.ShapeDtypeStruct((M, N_PAIRS * 128), BF16), wide, wide,
                   jax.ShapeDtypeStruct((M, N_PAIRS * 128), BF16), jax.ShapeDtypeStruct((N_PAIRS, M, 128), F32)],
        scratch_shapes=[slab(BF16), slab(BF16), slab(BF16), slab(F32), slab(F32), slab(F32),
                        pltpu.VMEM((t, 256), F32), pltpu.VMEM((2, t, t), F32), pltpu.VMEM((2, t, t), F32)],
        compiler_params=_params(("arbitrary", "arbitrary")))(q_all, kv_all, qr, kr, o, do, lse)


def _tri(t, rel):
    j = lax.broadcasted_iota(jnp.int32, (t, t), 0)
    k = lax.broadcasted_iota(jnp.int32, (t, t), 1)
    m = {"gt": j > k, "le": j <= k, "lt": j < k}[rel]
    return m.astype(BF16)


def _lane_cumsum(x, tri):
    hi = x.astype(BF16)
    lo = (x - hi.astype(F32)).astype(BF16)
    return _dot(hi, tri) + _dot(lo, tri)


def _log_sigmoids(z):
    sp = jnp.log(1.0 + jnp.exp(-jnp.abs(z)))
    return jnp.minimum(z, 0.0) - sp, jnp.minimum(-z, 0.0) - sp


def _sb_fwd(qkv, scale, name):
    M = qkv.shape[0]
    t = WALK_TILE
    ck, cv = N_PAIRS, 2 * N_PAIRS

    def body(q_ref, k_ref, v_ref, o_ref, tot_ref, c_s, acc_s):
        i = pl.program_id(1)
        first, _ = _head_masks()
        qs = _split_heads(q_ref[...], first)
        c_s[...] = jnp.zeros_like(c_s)
        acc_s[...] = jnp.zeros_like(acc_s)
        tri = _tri(t, "gt")
        qpos = i * t + lax.broadcasted_iota(jnp.int32, (t, t), 0)
        kidx = lax.broadcasted_iota(jnp.int32, (t, t), 1)

        def step(it):
            k0 = pl.multiple_of((i - it) * t, t)
            k = k_ref[pl.ds(k0, t), :]
            vs = _split_heads(v_ref[pl.ds(k0, t), :], first)
            kpos = k0 + kidx
            valid = (kpos < qpos) & (kpos >= PAD)
            pv = None
            for hh in range(2):
                z = _dot_nt(qs[hh], k) * scale
                lb, lkr = _log_sigmoids(z)
                lk = jnp.where(valid, lkr, 0.0)
                later = c_s[hh] + _lane_cumsum(lk, tri)
                a = jnp.where(valid, jnp.exp(lb + later), 0.0)
                c_s[hh] = c_s[hh] + jnp.sum(lk, axis=1, keepdims=True)
                d = _dot(a.astype(BF16), vs[hh])
                pv = d if pv is None else pv + d
            acc_s[...] += pv

        def cond(carry):
            it, go = carry
            return (it <= i) & go

        def walk(carry):
            it, _ = carry
            step(it)
            return it + 1, jnp.max(jnp.maximum(c_s[0], c_s[1])) > EXP_ZERO

        walked, _ = lax.while_loop(cond, walk, (jnp.int32(0), True))
        o_ref[...] = acc_s[...].astype(o_ref.dtype)
        tot_ref[:, 0:1] = c_s[0]
        tot_ref[:, 1:2] = c_s[1]
        tot_ref[:, 2:3] = jnp.full((t, 1), walked.astype(F32))

    whole = lambda c0: pl.BlockSpec((M, 128), lambda j, i: (0, c0 + j))
    return pl.pallas_call(
        body, name=name, grid=(N_PAIRS, M // t),
        in_specs=[pl.BlockSpec((t, 128), lambda j, i: (i, j)), whole(ck), whole(cv)],
        out_specs=[pl.BlockSpec((t, 128), lambda j, i: (i, j)), pl.BlockSpec((None, t, 3), lambda j, i: (j, i, 0))],
        out_shape=[jax.ShapeDtypeStruct((M, N_PAIRS * 128), BF16), jax.ShapeDtypeStruct((N_PAIRS, M, 3), F32)],
        scratch_shapes=[pltpu.VMEM((2, t, 1), F32), pltpu.VMEM((t, 128), F32)],
        compiler_params=_params(("parallel", "arbitrary")))(qkv, qkv, qkv)


def _sb_bwd(qkv, do, tot, scale, name):
    M = qkv.shape[0]
    t = WALK_TILE
    ck, cv = N_PAIRS, 2 * N_PAIRS

    def body(q_ref, k_ref, v_ref, do_ref, tot_ref, dq_ref, dk_ref, dv_ref, pc_s, dc_s, dq_s):
        i = pl.program_id(1)
        first, _ = _head_masks()

        @pl.when(i == 0)
        def _():
            dk_ref[...] = jnp.zeros_like(dk_ref)
            dv_ref[...] = jnp.zeros_like(dv_ref)

        qs = _split_heads(q_ref[...], first)
        dos = _split_heads(do_ref[...], first)
        pc_s[...] = jnp.zeros_like(pc_s)
        dc_s[...] = jnp.zeros_like(dc_s)
        dq_s[...] = jnp.zeros_like(dq_s)
        tri_le = _tri(t, "le")
        tri_lt = _tri(t, "lt")
        qpos = i * t + lax.broadcasted_iota(jnp.int32, (t, t), 0)
        kidx = lax.broadcasted_iota(jnp.int32, (t, t), 1)

        def step(kb, carry):
            k0 = pl.multiple_of(kb * t, t)
            rows = pl.ds(k0, t)
            k = k_ref[rows, :]
            v = v_ref[rows, :]
            ks = _split_heads(k, first)
            kpos = k0 + kidx
            valid = (kpos < qpos) & (kpos >= PAD)
            dq = dk = dv = None
            for hh in range(2):
                z = _dot_nt(qs[hh], k) * scale
                lb, lkr = _log_sigmoids(z)
                lk = jnp.where(valid, lkr, 0.0)
                later = tot_ref[:, hh:hh + 1] - (pc_s[hh] + _lane_cumsum(lk, tri_le))
                a = jnp.where(valid, jnp.exp(lb + later), 0.0)
                dl = a * _dot_nt(dos[hh], v)
                early = dc_s[hh] + _lane_cumsum(dl, tri_lt)
                sg = jnp.exp(lb)
                dz = jnp.where(valid, dl * (1.0 - sg) - early * sg, 0.0) * scale
                pc_s[hh] = pc_s[hh] + jnp.sum(lk, axis=1, keepdims=True)
                dc_s[hh] = dc_s[hh] + jnp.sum(dl, axis=1, keepdims=True)
                dzb = dz.astype(BF16)
                x = _dot(dzb, ks[hh])
                y = _dot_tn(dzb, qs[hh])
                w = _dot_tn(a.astype(BF16), dos[hh])
                dq = x if dq is None else dq + x
                dk = y if dk is None else dk + y
                dv = w if dv is None else dv + w
            dq_s[...] += dq
            dk_ref[rows, :] += dk
            dv_ref[rows, :] += dv
            return carry

        walked = jnp.max(tot_ref[:, 2:3]).astype(jnp.int32)
        lax.fori_loop(i + 1 - walked, i + 1, step, 0)
        dq_ref[...] = dq_s[...].astype(dq_ref.dtype)

    whole = lambda c0: pl.BlockSpec((M, 128), lambda j, i: (0, c0 + j))
    blk = pl.BlockSpec((t, 128), lambda j, i: (i, j))
    col = pl.BlockSpec((M, 128), lambda j, i: (0, j))
    return pl.pallas_call(
        body, name=name, grid=(N_PAIRS, M // t),
        in_specs=[blk, whole(ck), whole(cv), blk, pl.BlockSpec((None, t, 3), lambda j, i: (j, i, 0))],
        out_specs=[blk, col, col],
        out_shape=[jax.ShapeDtypeStruct((M, N_PAIRS * 128), BF16), jax.ShapeDtypeStruct((M, N_PAIRS * 128), F32),
                   jax.ShapeDtypeStruct((M, N_PAIRS * 128), F32)],
        scratch_shapes=[pltpu.VMEM((2, t, 1), F32), pltpu.VMEM((2, t, 1), F32), pltpu.VMEM((t, 128), F32)],
        compiler_params=_params(("parallel", "arbitrary")))(qkv, qkv, qkv, do, tot)


def _rows_between(lo, hi):
    r = lax.broadcasted_iota(jnp.int32, (128, 1), 0)
    return (r >= lo) & (r < hi)


def _lanes_between(lo, hi):
    c = lax.broadcasted_iota(jnp.int32, (1, 128), 1)
    return (c >= lo) & (c < hi)


def _keep(x, mask):
    return jnp.where(mask, x, jnp.zeros_like(x))


def _valid_mask(i, kb, t):
    kpos = kb * t + lax.broadcasted_iota(jnp.int32, (t, t), 0)
    qpos = i * t + lax.broadcasted_iota(jnp.int32, (t, t), 1)
    return (kpos <= qpos) & (kpos >= PAD)


def _fox_fwd(qkv, qkv_t, f_rows, f_cols, scale, name):
    M = qkv.shape[0]
    t = WALK_TILE
    first_blk = PAD // t
    ck, cv = N_PAIRS, 2 * N_PAIRS

    def body(qt_ref, k_ref, vt_ref, fq_ref, fk_ref, o_ref, lse_ref, ox_ref, m_s, l_s, acc_s, accx_s, kmax_s):
        i = pl.program_id(1)

        @pl.when(i == 0)
        def _():
            first = _lanes_between(0, 64)

            def block_max(kb, carry):
                kk = k_ref[pl.ds(pl.multiple_of(kb * t, t), t), :].astype(F32)
                kk = kk * kk
                a = jnp.max(jnp.sum(jnp.where(first, kk, 0.0), axis=1, keepdims=True))
                b = jnp.max(jnp.sum(jnp.where(first, 0.0, kk), axis=1, keepdims=True))
                return jnp.maximum(carry[0], a), jnp.maximum(carry[1], b)

            a, b = lax.fori_loop(0, M // t, block_max, (jnp.float32(0.0), jnp.float32(0.0)))
            kmax_s[0] = a
            kmax_s[1] = b

        qt = qt_ref[...]
        qts = (_keep(qt, _rows_between(0, 64)), _keep(qt, _rows_between(64, 128)))
        qf = qt.astype(F32)
        qf = qf * qf
        qbound = tuple(
            (1.001 * scale) * jnp.sqrt(jnp.sum(qf[HEAD_DIM * hh:HEAD_DIM * (hh + 1)], axis=0, keepdims=True) * kmax_s[hh])
            for hh in range(2))
        m_s[...] = jnp.full_like(m_s, NEG)
        l_s[...] = jnp.zeros_like(l_s)
        acc_s[...] = jnp.zeros_like(acc_s)
        accx_s[...] = jnp.zeros_like(accx_s)

        def step(kb, masked):
            k0 = pl.multiple_of(kb * t, t)
            rows = pl.ds(k0, t)
            k = k_ref[rows, :]
            if masked:
                valid = _valid_mask(i, kb, t)
            for hh in range(2):
                s = _dot(k, qts[hh]) * scale + (fq_ref[hh:hh + 1, :] - fk_ref[rows, hh:hh + 1])
                if masked:
                    s = jnp.where(valid, s, NEG)
                m_old = m_s[hh]
                m_new = jnp.maximum(m_old, jnp.max(s, axis=0, keepdims=True))
                p = jnp.exp(s - m_new)
                alpha = jnp.exp(m_old - m_new)
                l_s[hh] = alpha * l_s[hh] + jnp.sum(p, axis=0, keepdims=True)
                m_s[hh] = m_new
                pb = p.astype(BF16)
                hr = slice(HEAD_DIM * hh, HEAD_DIM * (hh + 1))
                vt = vt_ref[hr, rows]
                acc_s[hr, :] = acc_s[hr, :] * alpha + _dot(vt, pb)
                accx_s[hr, :] = accx_s[hr, :] * alpha + _dot(vt, (p - pb.astype(F32)).astype(BF16))

        def keep_going(kb):
            k0 = pl.multiple_of(kb * t, t)
            worst = None
            for hh in range(2):
                f0 = jnp.max(fk_ref[pl.ds(k0, 8), hh:hh + 1])
                w = jnp.max(qbound[hh] + (fq_ref[hh:hh + 1, :] - f0) - m_s[hh])
                worst = w if worst is None else jnp.maximum(worst, w)
            return worst > EXP_ZERO

        def cond(carry):
            kb, go, _ = carry
            return (kb > first_blk) & go

        def walk(carry):
            kb, _, n = carry
            step(kb, False)
            return kb - 1, keep_going(kb), n + 1

        step(i, True)
        _, go, n = lax.while_loop(cond, walk, (i - 1, keep_going(i), jnp.int32(1)))
        first_too = go & (i > first_blk)

        @pl.when(first_too)
        def _():
            step(first_blk, True)

        walked = n + first_too.astype(jnp.int32)
        for hh in range(2):
            hr = slice(HEAD_DIM * hh, HEAD_DIM * (hh + 1))
            inv = 1.0 / l_s[hh]
            o_ref[hr, :] = (acc_s[hr, :] * inv).astype(o_ref.dtype)
            ox_ref[hr, :] = (acc_s[hr, :] + accx_s[hr, :]) * inv
            lse_ref[hh:hh + 1, :] = m_s[hh] + jnp.log(l_s[hh])
        lse_ref[2:3, :] = jnp.full((1, t), walked.astype(F32))

    blk = pl.BlockSpec((128, t), lambda j, i: (j, i))
    stat = pl.BlockSpec((None, 2, t), lambda j, i: (j, 0, i))
    return pl.pallas_call(
        body, name=name, grid=(N_PAIRS, M // t),
        in_specs=[blk, pl.BlockSpec((M, 128), lambda j, i: (0, ck + j)), pl.BlockSpec((128, M), lambda j, i: (cv + j, 0)),
                  stat, pl.BlockSpec((None, M, 2), lambda j, i: (j, 0, 0))],
        out_specs=[blk, pl.BlockSpec((None, 3, t), lambda j, i: (j, 0, i)), blk],
        out_shape=[jax.ShapeDtypeStruct((N_PAIRS * 128, M), BF16), jax.ShapeDtypeStruct((N_PAIRS, 3, M), F32),
                   jax.ShapeDtypeStruct((N_PAIRS * 128, M), F32)],
        scratch_shapes=[pltpu.VMEM((2, 1, t), F32), pltpu.VMEM((2, 1, t), F32), pltpu.VMEM((128, t), F32),
                        pltpu.VMEM((128, t), F32), pltpu.SMEM((2,), F32)],
        compiler_params=_params(("parallel", "arbitrary")))(qkv_t, qkv, qkv_t, f_rows, f_cols)


def _fox_bwd(qkv, qkv_t, o_t, do, do_t, lse, f_rows, f_cols, scale, name):
    M = qkv.shape[0]
    t = WALK_TILE
    first_blk = PAD // t
    ck, cv = N_PAIRS, 2 * N_PAIRS

    def body(q_ref, qt_ref, k_ref, kt_ref, v_ref, ot_ref, do_ref, dot_ref, lse_ref, fq_ref, fk_ref,
             dq_ref, dk_ref, dv_ref, cs_ref, dq_s):
        i = pl.program_id(1)

        @pl.when(i == 0)
        def _():
            dk_ref[...] = jnp.zeros_like(dk_ref)
            dv_ref[...] = jnp.zeros_like(dv_ref)
            cs_ref[...] = jnp.zeros_like(cs_ref)

        heads_l = (_lanes_between(0, 64), _lanes_between(64, 128))
        heads_r = (_rows_between(0, 64), _rows_between(64, 128))
        q = q_ref[...]
        qt = qt_ref[...]
        do = do_ref[...]
        dot = dot_ref[...]
        qs = tuple(_keep(q, m) for m in heads_l)
        qts = tuple(_keep(qt, m) for m in heads_r)
        dos = tuple(_keep(do, m) for m in heads_l)
        dots = tuple(_keep(dot, m) for m in heads_r)
        prod = dot.astype(F32) * ot_ref[...]
        deltas = tuple(jnp.sum(prod[HEAD_DIM * hh:HEAD_DIM * (hh + 1)], axis=0, keepdims=True) for hh in range(2))
        ones = tuple(m.astype(BF16) * jnp.ones((t, 128), BF16) for m in heads_l)
        dq_s[...] = jnp.zeros_like(dq_s)

        def step(kb, masked):
            k0 = pl.multiple_of(kb * t, t)
            rows = pl.ds(k0, t)
            k = k_ref[rows, :]
            v = v_ref[rows, :]
            if masked:
                valid = _valid_mask(i, kb, t)
            dk = dv = cs = None
            for hh in range(2):
                s = _dot(k, qts[hh]) * scale + (fq_ref[hh:hh + 1, :] - fk_ref[rows, hh:hh + 1])
                if masked:
                    s = jnp.where(valid, s, NEG)
                p = jnp.exp(s - lse_ref[hh:hh + 1, :])
                ds = p * (_dot(v, dots[hh]) - deltas[hh])
                hi = ds.astype(BF16)
                lo = (ds - hi.astype(F32)).astype(BF16)
                c = _dot(hi, ones[hh]) + _dot(lo, ones[hh])
                dsb = (ds * scale).astype(BF16)
                hr = slice(HEAD_DIM * hh, HEAD_DIM * (hh + 1))
                dq_s[hr, :] += _dot(kt_ref[hr, rows], dsb)
                a = _dot(dsb, qs[hh])
                b = _dot(p.astype(BF16), dos[hh])
                dk = a if dk is None else dk + a
                dv = b if dv is None else dv + b
                cs = c if cs is None else cs + c
            dk_ref[rows, :] += dk
            dv_ref[rows, :] += dv
            cs_ref[rows, :] += cs

        first_walked = i + 1 - jnp.max(lse_ref[2:3, :]).astype(jnp.int32)

        def mid(kb, carry):
            step(kb, False)
            return carry

        @pl.when((first_walked == first_blk) & (i > first_blk))
        def _():
            step(first_blk, True)

        lax.fori_loop(jnp.maximum(first_walked, first_blk + 1), i, mid, 0)
        step(i, True)
        dq_ref[...] = dq_s[...].astype(dq_ref.dtype)

    rblk = pl.BlockSpec((t, 128), lambda j, i: (i, j))
    tblk = pl.BlockSpec((128, t), lambda j, i: (j, i))
    stat = pl.BlockSpec((None, 2, t), lambda j, i: (j, 0, i))
    stat3 = pl.BlockSpec((None, 3, t), lambda j, i: (j, 0, i))
    col = pl.BlockSpec((M, 128), lambda j, i: (0, j))
    wide = jax.ShapeDtypeStruct((M, N_PAIRS * 128), F32)
    return pl.pallas_call(
        body, name=name, grid=(N_PAIRS, M // t),
        in_specs=[rblk, tblk, pl.BlockSpec((M, 128), lambda j, i: (0, ck + j)),
                  pl.BlockSpec((128, M), lambda j, i: (ck + j, 0)), pl.BlockSpec((M, 128), lambda j, i: (0, cv + j)),
                  tblk, rblk, tblk, stat3, stat, pl.BlockSpec((None, M, 2), lambda j, i: (j, 0, 0))],
        out_specs=[tblk, col, col, pl.BlockSpec((None, M, 128), lambda j, i: (j, 0, 0))],
        out_shape=[jax.ShapeDtypeStruct((N_PAIRS * 128, M), BF16), wide, wide,
                   jax.ShapeDtypeStruct((N_PAIRS, M, 128), F32)],
        scratch_shapes=[pltpu.VMEM((128, t), F32)],
        compiler_params=_params(("parallel", "arbitrary")))(qkv, qkv_t, qkv, qkv_t, qkv, o_t, do, do_t, lse,
                                                            f_rows, f_cols)


def _rope_tables(M):
    pos = (jnp.arange(M, dtype=jnp.int32) - PAD).astype(F32)
    inv = ROPE_THETA ** (-jnp.arange(0, MLA_ROPE, 2, dtype=F32) / MLA_ROPE)
    ang = pos[:, None] * inv[None, :]
    cos, sin = jnp.cos(ang), jnp.sin(ang)
    z = jnp.zeros((M, 64), F32)
    cos_t = jnp.concatenate([cos, cos, cos, cos, z], axis=1)
    sin_t = jnp.concatenate([-sin, sin, -sin, sin, z], axis=1)
    return cos_t, sin_t


def _rope(x, cos_t, sin_t, out_dtype, name, inverse=False, lead=0):
    M, C = x.shape
    tm = _pick(M, (768, 512, 256, 128))
    nblk = (C - lead) // 128
    sign = -1.0 if inverse else 1.0

    def body(x_ref, c_ref, s_ref, o_ref):
        lane = lax.broadcasted_iota(jnp.int32, (1, 128), 1)
        low = (lane % MLA_ROPE) < (MLA_ROPE // 2)
        cos = c_ref[...]
        sin = s_ref[...] * sign
        if lead:
            o_ref[:, :lead] = x_ref[:, :lead].astype(o_ref.dtype)
        for b in range(nblk):
            cols = slice(lead + b * 128, lead + (b + 1) * 128)
            v = x_ref[:, cols].astype(F32)
            up = pltpu.roll(v, 128 - MLA_ROPE // 2, 1)
            down = pltpu.roll(v, MLA_ROPE // 2, 1)
            o_ref[:, cols] = (v * cos + jnp.where(low, up, down) * sin).astype(o_ref.dtype)

    row = pl.BlockSpec((tm, C), lambda m: (m, 0))
    tab = pl.BlockSpec((tm, 128), lambda m: (m, 0))
    return pl.pallas_call(
        body, name=name, grid=(M // tm,), in_specs=[row, tab, tab], out_specs=row,
        out_shape=jax.ShapeDtypeStruct((M, C), out_dtype),
        compiler_params=_params(("parallel",)))(x, cos_t, sin_t)


def _forget_cumsum(f_logit, bias, name):
    M = f_logit.shape[0]
    tm = 256

    def body(f_ref, b_ref, o_ref, c_s):
        i = pl.program_id(0)

        @pl.when(i == 0)
        def _():
            c_s[...] = jnp.zeros_like(c_s)
        ls, _ = _log_sigmoids(f_ref[...] + b_ref[...])
        rows = i * tm + lax.broadcasted_iota(jnp.int32, (tm, 1), 0)
        ls = jnp.where(rows >= PAD, ls, 0.0)
        r = lax.broadcasted_iota(jnp.int32, (tm, tm), 0)
        c = lax.broadcasted_iota(jnp.int32, (tm, tm), 1)
        tri = (c <= r).astype(F32)
        cum = jnp.dot(tri, ls, precision=lax.Precision.HIGHEST, preferred_element_type=F32) + c_s[...]
        o_ref[...] = cum
        c_s[...] = cum[tm - 1:tm, :]

    row = pl.BlockSpec((tm, 128), lambda m: (m, 0))
    return pl.pallas_call(
        body, name=name, grid=(M // tm,),
        in_specs=[row, pl.BlockSpec((1, 128), lambda m: (0, 0))], out_specs=row,
        out_shape=jax.ShapeDtypeStruct((M, 128), F32), scratch_shapes=[pltpu.VMEM((1, 128), F32)],
        compiler_params=_params(("arbitrary",)))(f_logit, bias)


def _forget_cumsum_bwd(f_logit, bias, colsum, name):
    M = f_logit.shape[0]
    tm = 256
    nb = M // tm

    def body(f_ref, b_ref, cs_ref, o_ref, db_ref, c_s):
        i = pl.program_id(0)

        @pl.when(i == 0)
        def _():
            c_s[...] = jnp.zeros_like(c_s)
            db_ref[...] = jnp.zeros_like(db_ref)
        rr = lax.broadcasted_iota(jnp.int32, (128, 128), 0)
        cc = lax.broadcasted_iota(jnp.int32, (128, 128), 1)
        dF = None
        for j in range(N_PAIRS):
            sel = (((rr == 0) & (cc == 2 * j)) | ((rr == HEAD_DIM) & (cc == 2 * j + 1))).astype(F32)
            d = jnp.dot(cs_ref[j], sel, precision=lax.Precision.HIGHEST, preferred_element_type=F32)
            dF = d if dF is None else dF + d
        r = lax.broadcasted_iota(jnp.int32, (tm, tm), 0)
        c = lax.broadcasted_iota(jnp.int32, (tm, tm), 1)
        tri = (c >= r).astype(F32)
        cum = c_s[...] - jnp.dot(tri, dF, precision=lax.Precision.HIGHEST, preferred_element_type=F32)
        c_s[...] = cum[0:1, :]
        _, lsn = _log_sigmoids(f_ref[...] + b_ref[...])
        rows = (nb - 1 - i) * tm + lax.broadcasted_iota(jnp.int32, (tm, 1), 0)
        dl = jnp.where(rows >= PAD, cum * jnp.exp(lsn), 0.0)
        o_ref[...] = dl
        db_ref[...] += jnp.sum(dl, axis=0, keepdims=True)

    row = pl.BlockSpec((tm, 128), lambda m: (nb - 1 - m, 0))
    vec = pl.BlockSpec((1, 128), lambda m: (0, 0))
    return pl.pallas_call(
        body, name=name, grid=(nb,),
        in_specs=[row, vec, pl.BlockSpec((N_PAIRS, tm, 128), lambda m: (0, nb - 1 - m, 0))], out_specs=[row, vec],
        out_shape=[jax.ShapeDtypeStruct((M, 128), F32), jax.ShapeDtypeStruct((1, 128), F32)],
        scratch_shapes=[pltpu.VMEM((1, 128), F32)],
        compiler_params=_params(("arbitrary",)))(f_logit, bias, colsum)


def _adamw(w, parts, m, v, name):
    R, C = w.shape
    tr = R
    for d in range(8, R, 8):
        if R % d == 0 and d * C <= ADAM_TILE_ELEMS:
            tr = d
    c1 = 1.0 - ADAM_B1 ** ADAM_STEP
    c2 = 1.0 - ADAM_B2 ** ADAM_STEP

    def body(w_ref, s_ref, m_ref, v_ref, g_ref, d_ref, mo_ref, vo_ref):
        g = s_ref[0].astype(F32)
        for k in range(1, N_DEV):
            g = g + s_ref[k].astype(F32)
        mn = ADAM_B1 * m_ref[...] + (1.0 - ADAM_B1) * g
        vn = ADAM_B2 * v_ref[...] + (1.0 - ADAM_B2) * (g * g)
        m_hat = mn / c1
        v_hat = vn / c2
        g_ref[...] = g
        d_ref[...] = -ADAM_LR * (m_hat / (jnp.sqrt(v_hat) + ADAM_EPS) + ADAM_WD * w_ref[...])
        mo_ref[...] = mn
        vo_ref[...] = vn

    row = pl.BlockSpec((tr, C), lambda r: (r, 0))
    shp = jax.ShapeDtypeStruct((R, C), F32)
    return pl.pallas_call(
        body, name=name, grid=(R // tr,),
        in_specs=[row, pl.BlockSpec((N_DEV, tr, C), lambda r: (0, r, 0)), row, row],
        out_specs=[row, row, row, row], out_shape=[shp, shp, shp, shp],
        compiler_params=_params(("parallel",)))(w, parts, m, v)


def _position():
    return lax.axis_index("x"), lax.axis_index("y"), lax.axis_index("c")


def _all_gather(blocks, name):
    n = len(blocks)

    def body(*refs):
        x_refs, out_refs = refs[:n], refs[n:2 * n]
        send_sems, recv_sems, local_sems = refs[2 * n:]
        x, y, c = _position()
        me, sibling = (x, y, c), (x, y, 1 - c)
        chips = [(1 - x, y), (x, 1 - y), (1 - x, 1 - y)]

        def copies(k, block, to, own=False):
            slot = 4 * block[0] + 2 * block[1] + block[2]
            return [pltpu.make_async_remote_copy(
                src_ref=x_refs[p] if own else out_refs[p].at[slot], dst_ref=out_refs[p].at[slot],
                send_sem=send_sems.at[k, p], recv_sem=recv_sems.at[k, p], device_id=to, device_id_type=MESH)
                for p in range(n)]

        mine = [pltpu.make_async_copy(x_refs[p], out_refs[p].at[4 * x + 2 * y + c], local_sems.at[p]) for p in range(n)]
        for cp in mine:
            cp.start()
        first = copies(0, me, sibling, own=True)
        for j, chip in enumerate(chips):
            first += copies(1 + j, me, (*chip, c), own=True)
        for cp in first:
            cp.start()
        passed = []
        for j, chip in enumerate(chips):
            for cp in copies(1 + j, (*chip, c), me):
                cp.wait_recv()
            onward = copies(4 + j, (*chip, c), sibling)
            for cp in onward:
                cp.start()
            passed += onward
        for cp in copies(0, sibling, me):
            cp.wait_recv()
        for j, chip in enumerate(chips):
            for cp in copies(4 + j, (*chip, 1 - c), me):
                cp.wait_recv()
        for cp in first + passed:
            cp.wait_send()
        for cp in mine:
            cp.wait()

    any_spec = pl.BlockSpec(memory_space=pl.ANY)
    return pl.pallas_call(
        body, name=name, out_shape=[jax.ShapeDtypeStruct((N_DEV,) + b.shape, b.dtype) for b in blocks],
        in_specs=[any_spec] * n, out_specs=[any_spec] * n,
        scratch_shapes=[pltpu.SemaphoreType.DMA((7, n)), pltpu.SemaphoreType.DMA((7, n)), pltpu.SemaphoreType.DMA((n,))],
    )(*blocks)


def _exchange(parts, name):
    n = len(parts)

    def body(*refs):
        g_refs, land_refs = refs[:n], refs[n:2 * n]
        send_sems, recv_sems, local_sems = refs[2 * n:]
        x, y, c = _position()
        me = 4 * x + 2 * y + c
        mine = [pltpu.make_async_copy(g_refs[p].at[me], land_refs[p].at[me], local_sems.at[p]) for p in range(n)]
        for cp in mine:
            cp.start()
        sends, recvs = [], []
        for k in range(1, N_DEV):
            px = 1 - x if k & 4 else x
            py = 1 - y if k & 2 else y
            pc = 1 - c if k & 1 else c
            peer = 4 * px + 2 * py + pc
            for p in range(n):
                sends.append(pltpu.make_async_remote_copy(
                    src_ref=g_refs[p].at[peer], dst_ref=land_refs[p].at[me], send_sem=send_sems.at[k - 1, p],
                    recv_sem=recv_sems.at[k - 1, p], device_id=(px, py, pc), device_id_type=MESH))
                recvs.append(pltpu.make_async_remote_copy(
                    src_ref=g_refs[p].at[me], dst_ref=land_refs[p].at[peer], send_sem=send_sems.at[k - 1, p],
                    recv_sem=recv_sems.at[k - 1, p], device_id=(px, py, pc), device_id_type=MESH))
        for cp in sends:
            cp.start()
        for cp in recvs:
            cp.wait_recv()
        for cp in sends:
            cp.wait_send()
        for cp in mine:
            cp.wait()

    any_spec = pl.BlockSpec(memory_space=pl.ANY)
    return pl.pallas_call(
        body, name=name, out_shape=[jax.ShapeDtypeStruct(p.shape, p.dtype) for p in parts],
        in_specs=[any_spec] * n, out_specs=[any_spec] * n,
        scratch_shapes=[pltpu.SemaphoreType.DMA((7, n)), pltpu.SemaphoreType.DMA((7, n)), pltpu.SemaphoreType.DMA((n,))],
    )(*parts)


SHARDED = (("sb_w_qkv", 2), ("sb_w_o", 1), ("mla_w_down", 1), ("mla_w_uq", 2), ("mla_w_ukv", 2), ("mla_w_o", 1),
           ("fox_w_qkvf", 2), ("fox_w_o", 1), ("ffn_w_gate", 2), ("ffn_w_up", 2), ("ffn_w_down", 1),
           ("pool_w", 2), ("meta", 1), ("mla_q_norm", 1), ("mla_kv_norm", 1))
KEPT_F32 = ("meta", "mla_q_norm", "mla_kv_norm")
REPLICATED = ("norm_mix", "norm_ffn", "pool_scale", "fox_b_f", "final_norm")
WEIGHT_NAMES = ("meta", "norm_mix", "norm_ffn", "pool_w", "pool_scale", "sb_w_qkv", "sb_w_o", "mla_w_down",
                "mla_q_norm", "mla_kv_norm", "mla_w_uq", "mla_w_ukv", "mla_w_o", "fox_w_qkvf", "fox_b_f",
                "fox_w_o", "ffn_w_gate", "ffn_w_up", "ffn_w_down", "final_norm")
LANES = 1024


def _pack_rows(arrays, names):
    parts = []
    for n in names:
        flat = arrays[n].reshape(-1).astype(F32)
        rows = -(-flat.shape[0] // LANES)
        parts.append(jnp.pad(flat, (0, rows * LANES - flat.shape[0])).reshape(rows, LANES))
    rows = sum(p.shape[0] for p in parts)
    parts.append(jnp.zeros((-(-rows // 8) * 8 - rows, LANES), F32))
    return jnp.concatenate(parts, axis=0)


def _unpack_rows(buf, shapes, names):
    out, row = {}, 0
    for n in names:
        size = int(np.prod(shapes[n]))
        rows = -(-size // LANES)
        out[n] = buf[row:row + rows].reshape(-1)[:size].reshape(shapes[n])
        row += rows
    return out


def _whole_from_gathered(g, axis):
    g = jnp.moveaxis(g, 0, axis)
    shp = g.shape
    return g.reshape(shp[:axis] + (shp[axis] * shp[axis + 1],) + shp[axis + 2:])


def _parts_from_whole(whole, axis):
    shp = whole.shape
    g = whole.reshape(shp[:axis] + (N_DEV, shp[axis] // N_DEV) + shp[axis + 1:])
    return jnp.moveaxis(g, axis, 0)


def _kernel_weights(full):
    W = {}
    W["pool_w"] = full["pool_w"][0]
    W["sb_w_qkv"] = full["sb_w_qkv"][0]
    W["sb_w_o"] = full["sb_w_o"][0]
    W["mla_w_down"] = full["mla_w_down"][0]
    uq = full["mla_w_uq"][0].reshape(MLA_Q_RANK, N_HEADS, MLA_NOPE + MLA_ROPE)
    nope = uq[:, :, :MLA_NOPE].reshape(MLA_Q_RANK, N_HEADS * MLA_NOPE)
    rope = uq[:, :, MLA_NOPE:].reshape(MLA_Q_RANK, N_PAIRS, 2 * MLA_ROPE)
    rope = jnp.pad(rope, ((0, 0), (0, 0), (0, 128 - 2 * MLA_ROPE))).reshape(MLA_Q_RANK, N_PAIRS * 128)
    W["mla_w_uq"] = jnp.concatenate([nope, rope], axis=1)
    ukv = full["mla_w_ukv"][0].reshape(MLA_KV_RANK, N_HEADS, 2, HEAD_DIM)
    W["mla_w_ukv"] = jnp.transpose(ukv, (0, 2, 1, 3)).reshape(MLA_KV_RANK, 2 * N_HEADS * HEAD_DIM)
    W["mla_w_o"] = full["mla_w_o"][0]
    qkvf = full["fox_w_qkvf"][0]
    n_qkv = 3 * N_HEADS * HEAD_DIM
    W["fox_w_qkv"] = qkvf[:, :n_qkv]
    W["fox_w_f"] = jnp.pad(qkvf[:, n_qkv:], ((0, 0), (0, 128 - N_HEADS)))
    W["fox_w_qkvf"] = jnp.concatenate([W["fox_w_qkv"], W["fox_w_f"]], axis=1)
    W["fox_w_o"] = full["fox_w_o"][0]
    W["ffn_w_gate"] = full["ffn_w_gate"]
    W["ffn_w_up"] = full["ffn_w_up"]
    W["ffn_w_down"] = full["ffn_w_down"]
    return W


def _reference_grads(G):
    out = {}
    out["pool_w"] = G["pool_w"][None]
    for n in ("sb_w_qkv", "sb_w_o", "mla_w_down", "mla_w_o", "fox_w_o"):
        out[n] = G[n][None]
    duq = G["mla_w_uq"]
    nope = duq[:, :N_HEADS * MLA_NOPE].reshape(MLA_Q_RANK, N_HEADS, MLA_NOPE)
    rope = duq[:, N_HEADS * MLA_NOPE:].reshape(MLA_Q_RANK, N_PAIRS, 128)[:, :, :2 * MLA_ROPE]
    rope = rope.reshape(MLA_Q_RANK, N_HEADS, MLA_ROPE)
    out["mla_w_uq"] = jnp.concatenate([nope, rope], axis=2).reshape(1, MLA_Q_RANK, -1)
    dukv = G["mla_w_ukv"].reshape(MLA_KV_RANK, 2, N_HEADS, HEAD_DIM)
    out["mla_w_ukv"] = jnp.transpose(dukv, (0, 2, 1, 3)).reshape(1, MLA_KV_RANK, -1)
    out["fox_w_qkvf"] = G["fox_w_qkvf"][None, :, :3 * N_HEADS * HEAD_DIM + N_HEADS]
    out["ffn_w_gate"] = G["ffn_w_gate"]
    out["ffn_w_up"] = G["ffn_w_up"]
    out["ffn_w_down"] = G["ffn_w_down"]
    out["mla_q_norm"] = G["mla_q_norm"]
    out["mla_kv_norm"] = G["mla_kv_norm"]
    return out


def _pairs_col(f16):
    M = f16.shape[0]
    return jnp.transpose(f16.reshape(M, N_PAIRS, 2), (1, 0, 2))


def _pairs_row(f16):
    M = f16.shape[0]
    return jnp.transpose(f16.reshape(M, N_PAIRS, 2), (1, 2, 0))


def _local_step(x, target, W, P):
    S = x.shape[0]
    M = S + ROW0
    G = {}
    gain = lambda name, i: P[name][i][None, :]
    h0 = jnp.concatenate([jnp.zeros((PAD, D_MODEL), F32), P["meta"], x], axis=0)

    def ffn_fwd(h1, i):
        b = _norm_fwd(h1, gain("norm_ffn", i), BF16, f"ffn{i}_norm")
        g, u, act = _ffn_up(b, W["ffn_w_gate"][i], W["ffn_w_up"][i], f"ffn{i}_up")
        h2 = _mm_nn(act, W["ffn_w_down"][i], F32, f"ffn{i}_down", res=h1)
        return h2, (h1, b, g, u, act)

    def ffn_bwd(dh2, saved, i):
        h1, b, g, u, act = saved
        dg, du = _ffn_dact(dh2, W["ffn_w_down"][i], g, u, f"ffn{i}_dact")
        G.setdefault("ffn_w_down", {})[i] = _mm_tn(act, dh2, f"ffn{i}_dwd")
        db = _mm_nt2(dg, W["ffn_w_gate"][i], du, W["ffn_w_up"][i], F32, f"ffn{i}_db")
        G.setdefault("ffn_w_gate", {})[i] = _mm_tn(b, dg, f"ffn{i}_dwg")
        G.setdefault("ffn_w_up", {})[i] = _mm_tn(b, du, f"ffn{i}_dwu")
        dh1, dgain = _norm_bwd(h1, gain("norm_ffn", i), db, dh2, f"ffn{i}_dnorm")
        G.setdefault("norm_ffn", {})[i] = dgain
        return dh1

    a0 = _norm_fwd(h0, gain("norm_mix", 0), F32, "mix0_norm")
    h1_0, pooled = _pool_fwd(h0, a0, W["pool_w"], P["pool_scale"], "pool_fwd")
    h_1, ffn0 = ffn_fwd(h1_0, 0)

    sb_scale = HEAD_DIM ** -0.5
    a1 = _norm_fwd(h_1, gain("norm_mix", 1), BF16, "mix1_norm")
    sb_qkv = _mm_nn(a1, W["sb_w_qkv"], BF16, "sb_qkv")
    sb_o, sb_tot = _sb_fwd(sb_qkv, sb_scale, "sb_fwd")
    h1_1 = _mm_nn(sb_o, W["sb_w_o"], F32, "sb_out", res=h_1)
    h_2, ffn1 = ffn_fwd(h1_1, 1)

    mla_scale = (MLA_NOPE + MLA_ROPE) ** -0.5
    cos_t, sin_t = _rope_tables(M)
    a2 = _norm_fwd(h_2, gain("norm_mix", 2), BF16, "mix2_norm")
    down = _mm_nn(a2, W["mla_w_down"], F32, "mla_down")
    dq_raw = down[:, :MLA_Q_RANK]
    dkv_raw = down[:, MLA_Q_RANK:MLA_Q_RANK + MLA_KV_RANK]
    kr_raw = down[:, MLA_Q_RANK + MLA_KV_RANK:]
    c_q = _norm_fwd(dq_raw, P["mla_q_norm"], BF16, "mla_qnorm")
    c_kv = _norm_fwd(dkv_raw, P["mla_kv_norm"], BF16, "mla_kvnorm")
    q_lin = _mm_nn(c_q, W["mla_w_uq"], F32, "mla_uq")
    q_all = _rope(q_lin, cos_t, sin_t, BF16, "mla_qrope", lead=D_MODEL)
    kv_all = _mm_nn(c_kv, W["mla_w_ukv"], BF16, "mla_ukv")
    kr_in = jnp.concatenate([kr_raw, kr_raw, jnp.zeros((M, 64), F32)], axis=1)
    kr = _rope(kr_in, cos_t, sin_t, BF16, "mla_krope")
    q_rope = q_all[:, D_MODEL:]
    mla_o, mla_lse = _mla_fwd(q_all, kv_all, q_rope, kr, mla_scale, "mla_fwd")
    h1_2 = _mm_nn(mla_o, W["mla_w_o"], F32, "mla_out", res=h_2)
    h_3, ffn2 = ffn_fwd(h1_2, 2)

    fox_scale = HEAD_DIM ** -0.5
    a3 = _norm_fwd(h_3, gain("norm_mix", 3), BF16, "mix3_norm")
    fox_qkv = _mm_nn(a3, W["fox_w_qkv"], BF16, "fox_qkv")
    f_logit = _mm_nn(a3, W["fox_w_f"], F32, "fox_f")
    b_f = jnp.pad(P["fox_b_f"], ((0, 0), (0, 128 - N_HEADS)))
    Fc = _forget_cumsum(f_logit, b_f, "fox_cumsum")
    f_rows, f_cols = _pairs_row(Fc[:, :N_HEADS]), _pairs_col(Fc[:, :N_HEADS])
    fox_qkv_t = fox_qkv.T
    fox_o_t, fox_lse, fox_ox_t = _fox_fwd(fox_qkv, fox_qkv_t, f_rows, f_cols, fox_scale, "fox_fwd")
    fox_o = fox_o_t.T
    h1_3 = _mm_nn(fox_o, W["fox_w_o"], F32, "fox_out", res=h_3)
    h_4, ffn3 = ffn_fwd(h1_3, 3)

    sq, dh, dgain = _loss_head(h_4, P["final_norm"][None, :], target, "loss_head")
    G["final_norm"] = dgain[0]

    dh = ffn_bwd(dh, ffn3, 3)
    do = _mm_nt(dh, W["fox_w_o"], BF16, "fox_do")
    G["fox_w_o"] = _mm_tn(fox_o, dh, "fox_dwo")
    dq_t, dk, dv, colsum = _fox_bwd(fox_qkv, fox_qkv_t, fox_ox_t, do, do.T, fox_lse, f_rows, f_cols, fox_scale,
                                    "fox_bwd")
    dlogit, db_f = _forget_cumsum_bwd(f_logit, b_f, colsum, "fox_dcumsum")
    G["fox_b_f"] = db_f[:, :N_HEADS]
    dproj = jnp.concatenate([dq_t.T, dk.astype(BF16), dv.astype(BF16), dlogit.astype(BF16)], axis=1)
    da = _mm_nt(dproj, W["fox_w_qkvf"], F32, "fox_da")
    G["fox_w_qkvf"] = _mm_tn(a3, dproj, "fox_dwqkvf")
    dh, dgain = _norm_bwd(h_3, gain("norm_mix", 3), da, dh, "mix3_dnorm")
    G.setdefault("norm_mix", {})[3] = dgain

    dh = ffn_bwd(dh, ffn2, 2)
    do = _mm_nt(dh, W["mla_w_o"], BF16, "mla_do")
    G["mla_w_o"] = _mm_tn(mla_o, dh, "mla_dwo")
    dq, dk, dv, dqr, dkr = _mla_bwd(q_all, kv_all, q_rope, kr, mla_o, do, mla_lse, mla_scale, "mla_bwd")
    dqr = _rope(dqr, cos_t, sin_t, BF16, "mla_dqrope", inverse=True)
    dq_all = jnp.concatenate([dq, dqr], axis=1)
    dkr_sum = _rope(jnp.sum(dkr, axis=0), cos_t, sin_t, F32, "mla_dkrope", inverse=True)
    dkr_raw = dkr_sum[:, :MLA_ROPE] + dkr_sum[:, MLA_ROPE:2 * MLA_ROPE]
    dkv_all = jnp.concatenate([dk.astype(BF16), dv.astype(BF16)], axis=1)
    dc_q = _mm_nt(dq_all, W["mla_w_uq"], F32, "mla_dcq")
    G["mla_w_uq"] = _mm_tn(c_q, dq_all, "mla_dwuq")
    dc_kv = _mm_nt(dkv_all, W["mla_w_ukv"], F32, "mla_dckv")
    G["mla_w_ukv"] = _mm_tn(c_kv, dkv_all, "mla_dwukv")
    ddq_raw, G["mla_q_norm"] = _norm_bwd(dq_raw, P["mla_q_norm"], dc_q, None, "mla_dqnorm")
    ddkv_raw, G["mla_kv_norm"] = _norm_bwd(dkv_raw, P["mla_kv_norm"], dc_kv, None, "mla_dkvnorm")
    ddown = jnp.concatenate([ddq_raw, ddkv_raw, dkr_raw], axis=1).astype(BF16)
    da = _mm_nt(ddown, W["mla_w_down"], F32, "mla_da")
    G["mla_w_down"] = _mm_tn(a2, ddown, "mla_dwdown")
    dh, dgain = _norm_bwd(h_2, gain("norm_mix", 2), da, dh, "mix2_dnorm")
    G["norm_mix"][2] = dgain

    dh = ffn_bwd(dh, ffn1, 1)
    do = _mm_nt(dh, W["sb_w_o"], BF16, "sb_do")
    G["sb_w_o"] = _mm_tn(sb_o, dh, "sb_dwo")
    dq, dk, dv = _sb_bwd(sb_qkv, do, sb_tot, sb_scale, "sb_bwd")
    dqkv = jnp.concatenate([dq, dk.astype(BF16), dv.astype(BF16)], axis=1)
    da = _mm_nt(dqkv, W["sb_w_qkv"], F32, "sb_da")
    G["sb_w_qkv"] = _mm_tn(a1, dqkv, "sb_dwqkv")
    dh, dgain = _norm_bwd(h_1, gain("norm_mix", 1), da, dh, "mix1_dnorm")
    G["norm_mix"][1] = dgain

    dh = ffn_bwd(dh, ffn0, 0)
    dpc, G["pool_w"], G["pool_scale"] = _pool_bwd_mix(dh, pooled, W["pool_w"], P["pool_scale"], "pool_dmix")
    da = _pool_bwd_window(dpc, "pool_dwindow")
    dh, dgain, dx = _norm_bwd(h0, gain("norm_mix", 0), da, dh, "mix0_dnorm", token_rows=True)
    G["norm_mix"][0] = dgain

    G["norm_mix"] = jnp.concatenate([G["norm_mix"][i] for i in range(DEPTH)], axis=0)
    G["norm_ffn"] = jnp.concatenate([G["norm_ffn"][i] for i in range(DEPTH)], axis=0)
    G["ffn_w_down"] = jnp.stack([G["ffn_w_down"][i] for i in range(DEPTH)])
    G["ffn_w_gate"] = jnp.stack([G["ffn_w_gate"][i] for i in range(DEPTH)])
    G["ffn_w_up"] = jnp.stack([G["ffn_w_up"][i] for i in range(DEPTH)])
    G["meta"] = dh[PAD:ROW0]
    return sq, dx, G


def kernel(x, meta, norm_mix, norm_ffn, pool_w, pool_scale, sb_w_qkv, sb_w_o, mla_w_down, mla_q_norm, mla_kv_norm, mla_w_uq, mla_w_ukv, mla_w_o, fox_w_qkvf, fox_b_f, fox_w_o, ffn_w_gate, ffn_w_up, ffn_w_down, final_norm, loss_target, m_meta, m_norm_mix, m_norm_ffn, m_pool_w, m_pool_scale, m_sb_w_qkv, m_sb_w_o, m_mla_w_down, m_mla_q_norm, m_mla_kv_norm, m_mla_w_uq, m_mla_w_ukv, m_mla_w_o, m_fox_w_qkvf, m_fox_b_f, m_fox_w_o, m_ffn_w_gate, m_ffn_w_up, m_ffn_w_down, m_final_norm, v_meta, v_norm_mix, v_norm_ffn, v_pool_w, v_pool_scale, v_sb_w_qkv, v_sb_w_o, v_mla_w_down, v_mla_q_norm, v_mla_kv_norm, v_mla_w_uq, v_mla_w_ukv, v_mla_w_o, v_fox_w_qkvf, v_fox_b_f, v_fox_w_o, v_ffn_w_gate, v_ffn_w_up, v_ffn_w_down, v_final_norm):
    w = dict(meta=meta, norm_mix=norm_mix, norm_ffn=norm_ffn, pool_w=pool_w, pool_scale=pool_scale,
             sb_w_qkv=sb_w_qkv, sb_w_o=sb_w_o, mla_w_down=mla_w_down, mla_q_norm=mla_q_norm,
             mla_kv_norm=mla_kv_norm, mla_w_uq=mla_w_uq, mla_w_ukv=mla_w_ukv, mla_w_o=mla_w_o,
             fox_w_qkvf=fox_w_qkvf, fox_b_f=fox_b_f, fox_w_o=fox_w_o, ffn_w_gate=ffn_w_gate, ffn_w_up=ffn_w_up,
             ffn_w_down=ffn_w_down, final_norm=final_norm)
    m = dict(meta=m_meta, norm_mix=m_norm_mix, norm_ffn=m_norm_ffn, pool_w=m_pool_w, pool_scale=m_pool_scale,
             sb_w_qkv=m_sb_w_qkv, sb_w_o=m_sb_w_o, mla_w_down=m_mla_w_down, mla_q_norm=m_mla_q_norm,
             mla_kv_norm=m_mla_kv_norm, mla_w_uq=m_mla_w_uq, mla_w_ukv=m_mla_w_ukv, mla_w_o=m_mla_w_o,
             fox_w_qkvf=m_fox_w_qkvf, fox_b_f=m_fox_b_f, fox_w_o=m_fox_w_o, ffn_w_gate=m_ffn_w_gate,
             ffn_w_up=m_ffn_w_up, ffn_w_down=m_ffn_w_down, final_norm=m_final_norm)
    v = dict(meta=v_meta, norm_mix=v_norm_mix, norm_ffn=v_norm_ffn, pool_w=v_pool_w, pool_scale=v_pool_scale,
             sb_w_qkv=v_sb_w_qkv, sb_w_o=v_sb_w_o, mla_w_down=v_mla_w_down, mla_q_norm=v_mla_q_norm,
             mla_kv_norm=v_mla_kv_norm, mla_w_uq=v_mla_w_uq, mla_w_ukv=v_mla_w_ukv, mla_w_o=v_mla_w_o,
             fox_w_qkvf=v_fox_w_qkvf, fox_b_f=v_fox_b_f, fox_w_o=v_fox_w_o, ffn_w_gate=v_ffn_w_gate,
             ffn_w_up=v_ffn_w_up, ffn_w_down=v_ffn_w_down, final_norm=v_final_norm)

    sh_names = tuple(n for n, _ in SHARDED)
    sh_axis = dict(SHARDED)
    shapes = {n: w[n].shape for n in WEIGHT_NAMES}
    wire = lambda n: F32 if n in KEPT_F32 else BF16

    gathered = _all_gather([w[n].astype(wire(n)) for n in sh_names], "gather_weights")
    full = {n: _whole_from_gathered(g, sh_axis[n]) for n, g in zip(sh_names, gathered)}
    W = _kernel_weights(full)
    P = dict(meta=full["meta"], mla_q_norm=full["mla_q_norm"], mla_kv_norm=full["mla_kv_norm"],
             norm_mix=norm_mix, norm_ffn=norm_ffn, pool_scale=pool_scale, fox_b_f=fox_b_f, final_norm=final_norm)

    sq, dx, G = _local_step(x[0], loss_target[0], W, P)
    loss = lax.psum(0.5 * jnp.sum(sq) / D_MODEL, ("x", "y", "c"))
    grad_x = dx[None]

    gw = _reference_grads(G)
    gw["meta"] = G["meta"]
    parts = [_parts_from_whole(gw[n], sh_axis[n]).astype(wire(n)) for n in sh_names]
    landed = _exchange(parts, "exchange_grads")
    results = {}
    for n, got in zip(sh_names, landed):
        rc = (int(np.prod(shapes[n][:-1])), shapes[n][-1])
        outs = _adamw(w[n].reshape(rc), got.reshape((N_DEV,) + rc), m[n].reshape(rc), v[n].reshape(rc), f"adamw_{n}")
        results[n] = [o.reshape(shapes[n]) for o in outs]

    rep_g = dict(norm_mix=G["norm_mix"], norm_ffn=G["norm_ffn"], pool_scale=G["pool_scale"], fox_b_f=G["fox_b_f"],
                 final_norm=G["final_norm"])
    (rep_all,) = _all_gather([_pack_rows(rep_g, REPLICATED)], "gather_replicated_grads")
    rep_out = _adamw(_pack_rows(w, REPLICATED), rep_all, _pack_rows(m, REPLICATED), _pack_rows(v, REPLICATED),
                     "adamw_replicated")
    rep = [_unpack_rows(o, shapes, REPLICATED) for o in rep_out]
    for n in REPLICATED:
        results[n] = [r[n] for r in rep]

    outs = [results[n][k] for k in range(4) for n in WEIGHT_NAMES]
    return (loss, grad_x, *outs)
```

```python
import numpy as np
import jax
import jax.numpy as jnp
from jax import lax
from jax.experimental import pallas as pl
from jax.experimental.pallas import tpu as pltpu

F32 = jnp.float32
BF16 = jnp.bfloat16

N_DEV = 8
D_MODEL = 1024
N_META = 16
PAD = 240
ROW0 = PAD + N_META
EPS = 1e-6
POOL_WINDOWS = (2, 4, 8, 16)
POOL_GROUP = 256
HALO = 128
N_HEADS = 16
HEAD_DIM = 64
N_PAIRS = N_HEADS // 2
MLA_Q_RANK = 384
MLA_KV_RANK = 256
MLA_NOPE = 64
MLA_ROPE = 32
ROPE_THETA = 10000.0
D_FF = 2816
DEPTH = 4
ATTN_TILE = 768
ATTN_BWD_TILE = 768
WALK_TILE = 256
NEG = -1e30
EXP_ZERO = -110.0
VMEM_LIMIT = 56 * 2**20
ADAM_TILE_ELEMS = 192 * 1024

ADAM_LR = 0.001
ADAM_B1 = 0.9
ADAM_B2 = 0.999
ADAM_EPS = 1e-08
ADAM_WD = 0.01
ADAM_STEP = 10

MESH = pl.DeviceIdType.MESH


def _params(sem=None):
    return pltpu.CompilerParams(dimension_semantics=sem, vmem_limit_bytes=VMEM_LIMIT)


def _pick(n, cands):
    for c in cands:
        if n % c == 0:
            return c
    return n


def _col_tile(n, cap=1536):
    best = None
    for t in range(128, min(n, cap) + 1, 128):
        if n % t == 0:
            best = t
    return best if best is not None else n


def _dot(a, b):
    return jnp.dot(a, b, preferred_element_type=F32)


def _dot_nt(a, b):
    return lax.dot_general(a, b, (((1,), (1,)), ((), ())), preferred_element_type=F32)


def _dot_tn(a, b):
    return lax.dot_general(a, b, (((0,), (0,)), ((), ())), preferred_element_type=F32)


def _mm_nn(a, b, out_dtype, name, res=None):
    M, K = a.shape
    N = b.shape[1]
    tm = _pick(M, (768, 512, 256, 128))
    tn = _col_tile(N)

    def body(*refs):
        if res is None:
            a_ref, b_ref, o_ref = refs
        else:
            a_ref, b_ref, r_ref, o_ref = refs
        acc = _dot(a_ref[...].astype(BF16), b_ref[...])
        if res is not None:
            acc = acc + r_ref[...]
        o_ref[...] = acc.astype(o_ref.dtype)

    in_specs = [pl.BlockSpec((tm, K), lambda n, m: (m, 0)), pl.BlockSpec((K, tn), lambda n, m: (0, n))]
    args = [a, b]
    if res is not None:
        in_specs.append(pl.BlockSpec((tm, tn), lambda n, m: (m, n)))
        args.append(res)
    return pl.pallas_call(
        body, name=name, grid=(N // tn, M // tm), in_specs=in_specs,
        out_specs=pl.BlockSpec((tm, tn), lambda n, m: (m, n)),
        out_shape=jax.ShapeDtypeStruct((M, N), out_dtype),
        compiler_params=_params(("parallel", "parallel")))(*args)


def _mm_nt(a, w, out_dtype, name):
    M, N = a.shape
    K = w.shape[0]
    tm = _pick(M, (768, 512, 256, 128)) if N <= 3200 else _pick(M, (256, 128))
    tk = _col_tile(K, 1024)

    def body(a_ref, w_ref, o_ref):
        o_ref[...] = _dot_nt(a_ref[...].astype(BF16), w_ref[...]).astype(o_ref.dtype)

    return pl.pallas_call(
        body, name=name, grid=(K // tk, M // tm),
        in_specs=[pl.BlockSpec((tm, N), lambda k, m: (m, 0)), pl.BlockSpec((tk, N), lambda k, m: (k, 0))],
        out_specs=pl.BlockSpec((tm, tk), lambda k, m: (m, k)),
        out_shape=jax.ShapeDtypeStruct((M, K), out_dtype),
        compiler_params=_params(("parallel", "parallel")))(a, w)


def _mm_nt2(a1, w1, a2, w2, out_dtype, name):
    M, N = a1.shape
    K = w1.shape[0]
    tm = _pick(M, (384, 256, 128))

    def body(a1_ref, w1_ref, a2_ref, w2_ref, o_ref):
        o_ref[...] = (_dot_nt(a1_ref[...], w1_ref[...]) + _dot_nt(a2_ref[...], w2_ref[...])).astype(o_ref.dtype)

    a_spec = pl.BlockSpec((tm, N), lambda m: (m, 0))
    w_spec = pl.BlockSpec((K, N), lambda m: (0, 0))
    return pl.pallas_call(
        body, name=name, grid=(M // tm,), in_specs=[a_spec, w_spec, a_spec, w_spec],
        out_specs=pl.BlockSpec((tm, K), lambda m: (m, 0)),
        out_shape=jax.ShapeDtypeStruct((M, K), out_dtype),
        compiler_params=_params(("parallel",)))(a1, w1, a2, w2)


def _mm_tn(a, b, name):
    M, K = a.shape
    N = b.shape[1]
    tm = _pick(M, (768, 512, 256, 128))
    tk = _col_tile(K, 1408)
    tn = _col_tile(N, 1408)

    def body(a_ref, b_ref, o_ref):
        @pl.when(pl.program_id(2) == 0)
        def _():
            o_ref[...] = jnp.zeros_like(o_ref)
        o_ref[...] += _dot_tn(a_ref[...].astype(BF16), b_ref[...].astype(BF16))

    return pl.pallas_call(
        body, name=name, grid=(K // tk, N // tn, M // tm),
        in_specs=[pl.BlockSpec((tm, tk), lambda k, n, m: (m, k)), pl.BlockSpec((tm, tn), lambda k, n, m: (m, n))],
        out_specs=pl.BlockSpec((tk, tn), lambda k, n, m: (k, n)),
        out_shape=jax.ShapeDtypeStruct((K, N), F32),
        compiler_params=_params(("parallel", "parallel", "arbitrary")))(a, b)


def _norm_fwd(h, gain, out_dtype, name):
    M, C = h.shape
    tm = _pick(M, (768, 512, 256, 128))

    def body(h_ref, g_ref, a_ref):
        x = h_ref[...]
        r = lax.rsqrt(jnp.mean(x * x, axis=-1, keepdims=True) + EPS)
        a_ref[...] = ((x * r) * g_ref[...]).astype(a_ref.dtype)

    return pl.pallas_call(
        body, name=name, grid=(M // tm,),
        in_specs=[pl.BlockSpec((tm, C), lambda m: (m, 0)), pl.BlockSpec((1, C), lambda m: (0, 0))],
        out_specs=pl.BlockSpec((tm, C), lambda m: (m, 0)),
        out_shape=jax.ShapeDtypeStruct((M, C), out_dtype),
        compiler_params=_params(("parallel",)))(h, gain)


def _norm_bwd(h, gain, da, dres, name, token_rows=False):
    M, C = h.shape
    tm = ROW0 if token_rows else _pick(M, (768, 512, 256, 128))

    def body(*refs):
        refs = list(refs)
        dx_ref = refs.pop() if token_rows else None
        if dres is None:
            h_ref, g_ref, da_ref, dh_ref, dg_ref = refs
        else:
            h_ref, g_ref, da_ref, dr_ref, dh_ref, dg_ref = refs
        x = h_ref[...]
        r = lax.rsqrt(jnp.mean(x * x, axis=-1, keepdims=True) + EPS)
        y = x * r
        dav = da_ref[...].astype(F32)
        dy = dav * g_ref[...]
        dh = r * (dy - y * jnp.mean(dy * y, axis=-1, keepdims=True))
        if dres is not None:
            dh = dh + dr_ref[...]
        dh_ref[...] = dh
        if token_rows:
            dx_ref[...] = dh

        @pl.when(pl.program_id(0) == 0)
        def _():
            dg_ref[...] = jnp.zeros_like(dg_ref)
        dg_ref[...] += jnp.sum(dav * y, axis=0, keepdims=True)

    row = pl.BlockSpec((tm, C), lambda m: (m, 0))
    vec = pl.BlockSpec((1, C), lambda m: (0, 0))
    in_specs = [row, vec, row] + ([row] if dres is not None else [])
    args = [h, gain, da] + ([dres] if dres is not None else [])
    out_specs = [row, vec]
    out_shape = [jax.ShapeDtypeStruct((M, C), F32), jax.ShapeDtypeStruct((1, C), F32)]
    if token_rows:
        out_specs.append(pl.BlockSpec((tm, C), lambda m: (jnp.maximum(m - 1, 0), 0)))
        out_shape.append(jax.ShapeDtypeStruct((M - ROW0, C), F32))
    return pl.pallas_call(
        body, name=name, grid=(M // tm,), in_specs=in_specs, out_specs=out_specs, out_shape=out_shape,
        compiler_params=_params(("arbitrary",)))(*args)


def _ffn_up(b, w_g, w_u, name):
    M, K = b.shape
    F = w_g.shape[1]
    tm = _pick(M, (768, 512, 256, 128))
    tn = _col_tile(F, 1408)
    nb = F // tn

    def body(b_ref, wg_ref, wu_ref, g_ref, u_ref, act_ref):
        x = b_ref[...]
        g = _dot(x, wg_ref[...])
        u = _dot(x, wu_ref[...])
        g_ref[...] = g
        u_ref[...] = u
        act_ref[...] = ((g * jax.nn.sigmoid(g)) * u).astype(act_ref.dtype)

    blk = pl.BlockSpec((tm, tn), lambda n, m: (m, n))
    return pl.pallas_call(
        body, name=name, grid=(nb, M // tm),
        in_specs=[pl.BlockSpec((tm, K), lambda n, m: (m, 0)),
                  pl.BlockSpec((K, tn), lambda n, m: (0, n)),
                  pl.BlockSpec((K, tn), lambda n, m: (0, n))],
        out_specs=[blk, blk, blk],
        out_shape=[jax.ShapeDtypeStruct((M, F), F32), jax.ShapeDtypeStruct((M, F), F32),
                   jax.ShapeDtypeStruct((M, F), BF16)],
        compiler_params=_params(("parallel", "parallel")))(b, w_g, w_u)


def _ffn_dact(dy, w_d, g, u, name):
    M, K = dy.shape
    F = w_d.shape[0]
    tm = _pick(M, (768, 512, 256, 128))
    tn = _col_tile(F, 1408)
    nb = F // tn

    def body(dy_ref, wd_ref, g_ref, u_ref, dg_ref, du_ref):
        dact = _dot_nt(dy_ref[...].astype(BF16), wd_ref[...])
        gv = g_ref[...]
        s = jax.nn.sigmoid(gv)
        silu = gv * s
        dg_ref[...] = (dact * u_ref[...] * (s * (1.0 + gv * (1.0 - s)))).astype(dg_ref.dtype)
        du_ref[...] = (dact * silu).astype(du_ref.dtype)

    blk = pl.BlockSpec((tm, tn), lambda n, m: (m, n))
    return pl.pallas_call(
        body, name=name, grid=(nb, M // tm),
        in_specs=[pl.BlockSpec((tm, K), lambda n, m: (m, 0)), pl.BlockSpec((tn, K), lambda n, m: (n, 0)), blk, blk],
        out_specs=[blk, blk],
        out_shape=[jax.ShapeDtypeStruct((M, F), BF16), jax.ShapeDtypeStruct((M, F), BF16)],
        compiler_params=_params(("parallel", "parallel")))(dy, w_d, g, u)


def _loss_head(h, gain, target, name):
    M, C = h.shape
    tm = ROW0
    assert M % tm == 0 and target.shape[0] == M - ROW0

    def body(h_ref, g_ref, t_ref, sq_ref, dh_ref, dg_ref):
        i = pl.program_id(0)

        @pl.when(i == 0)
        def _():
            sq_ref[...] = jnp.zeros_like(sq_ref)
            dg_ref[...] = jnp.zeros_like(dg_ref)
            dh_ref[...] = jnp.zeros_like(dh_ref)

        @pl.when(i > 0)
        def _():
            x = h_ref[...]
            r = lax.rsqrt(jnp.mean(x * x, axis=-1, keepdims=True) + EPS)
            y = x * r
            err = y * g_ref[...] - t_ref[...]
            sq_ref[...] += jnp.sum(err * err, axis=0, keepdims=True)
            da = err * (1.0 / C)
            dy = da * g_ref[...]
            dh_ref[...] = r * (dy - y * jnp.mean(dy * y, axis=-1, keepdims=True))
            dg_ref[...] += jnp.sum(da * y, axis=0, keepdims=True)

    row = pl.BlockSpec((tm, C), lambda m: (m, 0))
    vec = pl.BlockSpec((1, C), lambda m: (0, 0))
    return pl.pallas_call(
        body, name=name, grid=(M // tm,),
        in_specs=[row, vec, pl.BlockSpec((tm, C), lambda m: (jnp.maximum(m - 1, 0), 0))],
        out_specs=[vec, row, vec],
        out_shape=[jax.ShapeDtypeStruct((1, C), F32), jax.ShapeDtypeStruct((M, C), F32),
                   jax.ShapeDtypeStruct((1, C), F32)],
        compiler_params=_params(("arbitrary",)))(h, gain, target)


def _pool_pos(row0, tm):
    return row0 + lax.broadcasted_iota(jnp.int32, (tm, 1), 0) - PAD


def _pool_fwd(h, a, w, scale, name):
    M, C = a.shape
    tm = 256
    hb = tm // HALO

    def body(h_ref, a_ref, halo_ref, w_ref, s_ref, o_ref, p_ref):
        i = pl.program_id(0)
        row0 = i * tm
        ext = jnp.concatenate([halo_ref[...], a_ref[...]], axis=0)
        src = row0 - HALO + lax.broadcasted_iota(jnp.int32, (tm + HALO, 1), 0)
        ext = jnp.where(src >= PAD, ext, 0.0)
        r = lax.broadcasted_iota(jnp.int32, (tm, tm + HALO), 0)
        c = lax.broadcasted_iota(jnp.int32, (tm, tm + HALO), 1)
        pos = _pool_pos(row0, tm)
        for g, win in enumerate(POOL_WINDOWS):
            band = ((c <= r + HALO) & (c > r + HALO - win)).astype(F32)
            cols = slice(g * POOL_GROUP, (g + 1) * POOL_GROUP)
            xg = ext[:, cols]
            tot = jnp.dot(band, xg, precision=lax.Precision.HIGHEST, preferred_element_type=F32)
            cnt = jnp.clip(pos + 1, 1, win).astype(F32)
            pooled = (tot / cnt - xg[HALO:]).astype(BF16)
            p_ref[:, cols] = pooled
            mixed = _dot(pooled, w_ref[g])
            o_ref[:, cols] = h_ref[:, cols] + mixed * s_ref[:, cols]

    row = pl.BlockSpec((tm, C), lambda m: (m, 0))
    return pl.pallas_call(
        body, name=name, grid=(M // tm,),
        in_specs=[row, row, pl.BlockSpec((HALO, C), lambda m: (jnp.maximum(m * hb - 1, 0), 0)),
                  pl.BlockSpec((4, POOL_GROUP, POOL_GROUP), lambda m: (0, 0, 0)),
                  pl.BlockSpec((1, C), lambda m: (0, 0))],
        out_specs=[row, row],
        out_shape=[jax.ShapeDtypeStruct((M, C), F32), jax.ShapeDtypeStruct((M, C), BF16)],
        compiler_params=_params(("parallel",)))(h, a, a, w, scale)


def _pool_bwd_mix(dout, pooled, w, scale, name):
    M, C = dout.shape
    tm = 256

    def body(do_ref, p_ref, w_ref, s_ref, dpc_ref, dw_ref, ds_ref):
        i = pl.program_id(0)

        @pl.when(i == 0)
        def _():
            dw_ref[...] = jnp.zeros_like(dw_ref)
            ds_ref[...] = jnp.zeros_like(ds_ref)

        pos = _pool_pos(i * tm, tm)
        for g, win in enumerate(POOL_WINDOWS):
            cols = slice(g * POOL_GROUP, (g + 1) * POOL_GROUP)
            do = do_ref[:, cols]
            pooled = p_ref[:, cols]
            mixed = _dot(pooled, w_ref[g])
            ds_ref[:, cols] += jnp.sum(do * mixed, axis=0, keepdims=True)
            dmix = (do * s_ref[:, cols]).astype(BF16)
            dw_ref[g] += _dot_tn(pooled, dmix)
            dp = _dot_nt(dmix, w_ref[g])
            cnt = jnp.clip(pos + 1, 1, win).astype(F32)
            dpc_ref[:, cols] = dp / cnt

    row = pl.BlockSpec((tm, C), lambda m: (m, 0))
    wspec = pl.BlockSpec((4, POOL_GROUP, POOL_GROUP), lambda m: (0, 0, 0))
    vec = pl.BlockSpec((1, C), lambda m: (0, 0))
    return pl.pallas_call(
        body, name=name, grid=(M // tm,),
        in_specs=[row, row, wspec, vec], out_specs=[row, wspec, vec],
        out_shape=[jax.ShapeDtypeStruct((M, C), F32), jax.ShapeDtypeStruct((4, POOL_GROUP, POOL_GROUP), F32),
                   jax.ShapeDtypeStruct((1, C), F32)],
        compiler_params=_params(("arbitrary",)))(dout, pooled, w, scale)


def _pool_bwd_window(dpc, name):
    M, C = dpc.shape
    tm = 256
    hb = tm // HALO
    last = M // HALO - 1

    def body(d_ref, halo_ref, da_ref):
        i = pl.program_id(0)
        row0 = i * tm
        ext = jnp.concatenate([d_ref[...], halo_ref[...]], axis=0)
        src = row0 + lax.broadcasted_iota(jnp.int32, (tm + HALO, 1), 0)
        ext = jnp.where(src < M, ext, 0.0)
        r = lax.broadcasted_iota(jnp.int32, (tm, tm + HALO), 0)
        c = lax.broadcasted_iota(jnp.int32, (tm, tm + HALO), 1)
        pos = _pool_pos(row0, tm)
        for g, win in enumerate(POOL_WINDOWS):
            band = ((c >= r) & (c < r + win)).astype(F32)
            cols = slice(g * POOL_GROUP, (g + 1) * POOL_GROUP)
            xg = ext[:, cols]
            tot = jnp.dot(band, xg, precision=lax.Precision.HIGHEST, preferred_element_type=F32)
            cnt = jnp.clip(pos + 1, 1, win).astype(F32)
            da_ref[:, cols] = jnp.where(pos >= 0, tot - xg[:tm] * cnt, 0.0)

    row = pl.BlockSpec((tm, C), lambda m: (m, 0))
    return pl.pallas_call(
        body, name=name, grid=(M // tm,),
        in_specs=[row, pl.BlockSpec((HALO, C), lambda m: (jnp.minimum((m + 1) * hb, last), 0))],
        out_specs=row, out_shape=jax.ShapeDtypeStruct((M, C), F32),
        compiler_params=_params(("parallel",)))(dpc, dpc)


def _head_masks():
    lane = lax.broadcasted_iota(jnp.int32, (1, 128), 1)
    return lane < HEAD_DIM, lane


def _split_heads(x, first):
    z = jnp.zeros_like(x)
    return jnp.where(first, x, z), jnp.where(first, z, x)


def _split_rope(x, lane):
    z = jnp.zeros_like(x)
    return jnp.where(lane < MLA_ROPE, x, z), jnp.where((lane >= MLA_ROPE) & (lane < 2 * MLA_ROPE), x, z)


def _walk_causal(i, step):
    def mid(kb, carry):
        step(kb, False)
        return carry

    step(0, True)
    lax.fori_loop(1, i, mid, 0)

    @pl.when(i > 0)
    def _():
        step(i, True)


def _mla_fwd(q_all, kv_all, qr, kr, scale, name):
    M = q_all.shape[0]
    t = ATTN_TILE

    def body(q_ref, k_ref, v_ref, qr_ref, kr_ref, o_ref, lse_ref, m_s, l_s, acc_s):
        i = pl.program_id(1)
        first, lane = _head_masks()
        qs = _split_heads(q_ref[...], first)
        qrs = _split_rope(qr_ref[...], lane)
        qcat = tuple(jnp.concatenate([qs[hh], qrs[hh]], axis=1) for hh in range(2))
        m_s[...] = jnp.full_like(m_s, NEG)
        l_s[...] = jnp.zeros_like(l_s)
        acc_s[...] = jnp.zeros_like(acc_s)
        qpos = i * t + lax.broadcasted_iota(jnp.int32, (t, t), 0)
        kidx = lax.broadcasted_iota(jnp.int32, (t, t), 1)

        def step(kb, masked):
            k0 = pl.multiple_of(kb * t, t)
            kcat = jnp.concatenate([k_ref[pl.ds(k0, t), :], kr_ref[pl.ds(k0, t), :]], axis=1)
            vs = _split_heads(v_ref[pl.ds(k0, t), :], first)
            if masked:
                kpos = k0 + kidx
                valid = (kpos <= qpos) & (kpos >= PAD)
            pv = None
            alphas = []
            for hh in range(2):
                s = _dot_nt(qcat[hh], kcat) * scale
                if masked:
                    s = jnp.where(valid, s, NEG)
                m_old = m_s[hh]
                m_new = jnp.maximum(m_old, jnp.max(s, axis=1, keepdims=True))
                p = jnp.exp(s - m_new)
                alpha = jnp.exp(m_old - m_new)
                l_s[hh] = alpha * l_s[hh] + jnp.sum(p, axis=1, keepdims=True)
                m_s[hh] = m_new
                d = _dot(p.astype(BF16), vs[hh])
                pv = d if pv is None else pv + d
                alphas.append(alpha)
            acc_s[...] = acc_s[...] * jnp.where(first, alphas[0], alphas[1]) + pv

        _walk_causal(i, step)
        o_ref[...] = (acc_s[...] * jnp.where(first, 1.0 / l_s[0], 1.0 / l_s[1])).astype(o_ref.dtype)
        lse_ref[:, 0:1] = m_s[0] + jnp.log(l_s[0])
        lse_ref[:, 1:2] = m_s[1] + jnp.log(l_s[1])

    blk = pl.BlockSpec((t, 128), lambda j, i: (i, j))
    return pl.pallas_call(
        body, name=name, grid=(N_PAIRS, M // t),
        in_specs=[blk, pl.BlockSpec((M, 128), lambda j, i: (0, j)), pl.BlockSpec((M, 128), lambda j, i: (0, N_PAIRS + j)),
                  blk, pl.BlockSpec((M, 128), lambda j, i: (0, 0))],
        out_specs=[blk, pl.BlockSpec((None, t, 2), lambda j, i: (j, i, 0))],
        out_shape=[jax.ShapeDtypeStruct((M, N_PAIRS * 128), BF16), jax.ShapeDtypeStruct((N_PAIRS, M, 2), F32)],
        scratch_shapes=[pltpu.VMEM((2, t, 1), F32), pltpu.VMEM((2, t, 1), F32), pltpu.VMEM((t, 128), F32)],
        compiler_params=_params(("parallel", "arbitrary")))(q_all, kv_all, kv_all, qr, kr)


def _mla_bwd(q_all, kv_all, qr, kr, o, do, lse, scale, name):
    M = q_all.shape[0]
    t = ATTN_BWD_TILE

    def body(q_ref, kv_hbm, qr_ref, kr_hbm, o_ref, do_ref, lse_ref,
             dq_ref, dk_hbm, dv_hbm, dqr_ref, dkr_hbm,
             k_ref, v_ref, kr_ref, dk_ref, dv_ref, dkr_ref, dq_s, lse_s, delta_s):
        j = pl.program_id(0)
        i = pl.program_id(1)
        first, lane = _head_masks()
        every = pl.ds(0, M)
        kcols = pl.ds(pl.multiple_of(j * 128, 128), 128)
        vcols = pl.ds(pl.multiple_of((N_PAIRS + j) * 128, 128), 128)

        @pl.when(i == 0)
        def _():
            pltpu.sync_copy(kv_hbm.at[every, kcols], k_ref)
            pltpu.sync_copy(kv_hbm.at[every, vcols], v_ref)
            pltpu.sync_copy(kr_hbm, kr_ref)
            dk_ref[...] = jnp.zeros_like(dk_ref)
            dv_ref[...] = jnp.zeros_like(dv_ref)
            dkr_ref[...] = jnp.zeros_like(dkr_ref)

        qs = _split_heads(q_ref[...], first)
        qrs = _split_rope(qr_ref[...], lane)
        qcat = tuple(jnp.concatenate([qs[hh], qrs[hh]], axis=1) for hh in range(2))
        dov = do_ref[...]
        dos = _split_heads(dov, first)
        prod = dov.astype(F32) * o_ref[...].astype(F32)
        deltas = (jnp.sum(jnp.where(first, prod, 0.0), axis=1, keepdims=True),
                  jnp.sum(jnp.where(first, 0.0, prod), axis=1, keepdims=True))
        for hh in range(2):
            lse_s[hh] = jnp.broadcast_to(lse_ref[:, hh:hh + 1], (t, t))
            delta_s[hh] = jnp.broadcast_to(deltas[hh], (t, t))
        dq_s[...] = jnp.zeros_like(dq_s)
        qpos = i * t + lax.broadcasted_iota(jnp.int32, (t, t), 0)
        kidx = lax.broadcasted_iota(jnp.int32, (t, t), 1)

        def step(kb, masked):
            k0 = pl.multiple_of(kb * t, t)
            rows = pl.ds(k0, t)
            k = k_ref[rows, :]
            v = v_ref[rows, :]
            kr = kr_ref[rows, :]
            kcat = jnp.concatenate([k, kr], axis=1)
            ks = _split_heads(k, first)
            krs = _split_rope(kr, lane)
            if masked:
                kpos = k0 + kidx
                valid = (kpos <= qpos) & (kpos >= PAD)
            dq = dk = dv = None
            for hh in range(2):
                s = _dot_nt(qcat[hh], kcat) * scale
                if masked:
                    s = jnp.where(valid, s, NEG)
                p = jnp.exp(s - lse_s[hh])
                ds = p * (_dot_nt(dos[hh], v) - delta_s[hh])
                dsb = (ds * scale).astype(BF16)
                a = _dot(dsb, jnp.concatenate([ks[hh], krs[hh]], axis=1))
                b = _dot_tn(dsb, qcat[hh])
                c = _dot_tn(p.astype(BF16), dos[hh])
                dq = a if dq is None else dq + a
                dk = b if dk is None else dk + b
                dv = c if dv is None else dv + c
            dq_s[...] += dq
            dk_ref[rows, :] += dk[:, :128]
            dkr_ref[rows, :] += dk[:, 128:]
            dv_ref[rows, :] += dv

        _walk_causal(i, step)
        dq_ref[...] = dq_s[:, :128].astype(dq_ref.dtype)
        dqr_ref[...] = dq_s[:, 128:].astype(dqr_ref.dtype)

        @pl.when(i == M // t - 1)
        def _():
            pltpu.sync_copy(dk_ref, dk_hbm.at[every, kcols])
            pltpu.sync_copy(dv_ref, dv_hbm.at[every, kcols])
            pltpu.sync_copy(dkr_ref, dkr_hbm.at[j])

    blk = pl.BlockSpec((t, 128), lambda j, i: (i, j))
    whole = pl.BlockSpec(memory_space=pl.ANY)
    wide = jax.ShapeDtypeStruct((M, N_PAIRS * 128), F32)
    slab = lambda dtype: pltpu.VMEM((M, 128), dtype)
    return pl.pallas_call(
        body, name=name, grid=(N_PAIRS, M // t),
        in_specs=[blk, whole, blk, whole, blk, blk, pl.BlockSpec((None, t, 2), lambda j, i: (j, i, 0))],
        out_specs=[blk, whole, whole, blk, whole],
        out_shape=[jax.ShapeDtypeStruct((M, N_PAIRS * 128), BF16), wide, wide,
                   jax.ShapeDtypeStruct((M, N_PAIRS * 128), BF16), jax.ShapeDtypeStruct((N_PAIRS, M, 128), F32)],
        scratch_shapes=[slab(BF16), slab(BF16), slab(BF16), slab(F32), slab(F32), slab(F32),
                        pltpu.VMEM((t, 256), F32), pltpu.VMEM((2, t, t), F32), pltpu.VMEM((2, t, t), F32)],
        compiler_params=_params(("arbitrary", "arbitrary")))(q_all, kv_all, qr, kr, o, do, lse)


def _tri(t, rel):
    j = lax.broadcasted_iota(jnp.int32, (t, t), 0)
    k = lax.broadcasted_iota(jnp.int32, (t, t), 1)
    m = {"gt": j > k, "le": j <= k, "lt": j < k}[rel]
    return m.astype(BF16)


def _lane_cumsum(x, tri):
    hi = x.astype(BF16)
    lo = (x - hi.astype(F32)).astype(BF16)
    return _dot(hi, tri) + _dot(lo, tri)


def _log_sigmoids(z):
    sp = jnp.log(1.0 + jnp.exp(-jnp.abs(z)))
    return jnp.minimum(z, 0.0) - sp, jnp.minimum(-z, 0.0) - sp


def _sb_fwd(qkv, scale, name):
    M = qkv.shape[0]
    t = WALK_TILE
    ck, cv = N_PAIRS, 2 * N_PAIRS

    def body(q_ref, k_ref, v_ref, o_ref, tot_ref, c_s, acc_s):
        i = pl.program_id(1)
        first, _ = _head_masks()
        qs = _split_heads(q_ref[...], first)
        c_s[...] = jnp.zeros_like(c_s)
        acc_s[...] = jnp.zeros_like(acc_s)
        tri = _tri(t, "gt")
        qpos = i * t + lax.broadcasted_iota(jnp.int32, (t, t), 0)
        kidx = lax.broadcasted_iota(jnp.int32, (t, t), 1)

        def step(it):
            k0 = pl.multiple_of((i - it) * t, t)
            k = k_ref[pl.ds(k0, t), :]
            vs = _split_heads(v_ref[pl.ds(k0, t), :], first)
            kpos = k0 + kidx
            valid = (kpos < qpos) & (kpos >= PAD)
            pv = None
            for hh in range(2):
                z = _dot_nt(qs[hh], k) * scale
                lb, lkr = _log_sigmoids(z)
                lk = jnp.where(valid, lkr, 0.0)
                later = c_s[hh] + _lane_cumsum(lk, tri)
                a = jnp.where(valid, jnp.exp(lb + later), 0.0)
                c_s[hh] = c_s[hh] + jnp.sum(lk, axis=1, keepdims=True)
                d = _dot(a.astype(BF16), vs[hh])
                pv = d if pv is None else pv + d
            acc_s[...] += pv

        def cond(carry):
            it, go = carry
            return (it <= i) & go

        def walk(carry):
            it, _ = carry
            step(it)
            return it + 1, jnp.max(jnp.maximum(c_s[0], c_s[1])) > EXP_ZERO

        walked, _ = lax.while_loop(cond, walk, (jnp.int32(0), True))
        o_ref[...] = acc_s[...].astype(o_ref.dtype)
        tot_ref[:, 0:1] = c_s[0]
        tot_ref[:, 1:2] = c_s[1]
        tot_ref[:, 2:3] = jnp.full((t, 1), walked.astype(F32))

    whole = lambda c0: pl.BlockSpec((M, 128), lambda j, i: (0, c0 + j))
    return pl.pallas_call(
        body, name=name, grid=(N_PAIRS, M // t),
        in_specs=[pl.BlockSpec((t, 128), lambda j, i: (i, j)), whole(ck), whole(cv)],
        out_specs=[pl.BlockSpec((t, 128), lambda j, i: (i, j)), pl.BlockSpec((None, t, 3), lambda j, i: (j, i, 0))],
        out_shape=[jax.ShapeDtypeStruct((M, N_PAIRS * 128), BF16), jax.ShapeDtypeStruct((N_PAIRS, M, 3), F32)],
        scratch_shapes=[pltpu.VMEM((2, t, 1), F32), pltpu.VMEM((t, 128), F32)],
        compiler_params=_params(("parallel", "arbitrary")))(qkv, qkv, qkv)


def _sb_bwd(qkv, do, tot, scale, name):
    M = qkv.shape[0]
    t = WALK_TILE
    ck, cv = N_PAIRS, 2 * N_PAIRS

    def body(q_ref, k_ref, v_ref, do_ref, tot_ref, dq_ref, dk_ref, dv_ref, pc_s, dc_s, dq_s):
        i = pl.program_id(1)
        first, _ = _head_masks()

        @pl.when(i == 0)
        def _():
            dk_ref[...] = jnp.zeros_like(dk_ref)
            dv_ref[...] = jnp.zeros_like(dv_ref)

        qs = _split_heads(q_ref[...], first)
        dos = _split_heads(do_ref[...], first)
        pc_s[...] = jnp.zeros_like(pc_s)
        dc_s[...] = jnp.zeros_like(dc_s)
        dq_s[...] = jnp.zeros_like(dq_s)
        tri_le = _tri(t, "le")
        tri_lt = _tri(t, "lt")
        qpos = i * t + lax.broadcasted_iota(jnp.int32, (t, t), 0)
        kidx = lax.broadcasted_iota(jnp.int32, (t, t), 1)

        def step(kb, carry):
            k0 = pl.multiple_of(kb * t, t)
            rows = pl.ds(k0, t)
            k = k_ref[rows, :]
            v = v_ref[rows, :]
            ks = _split_heads(k, first)
            kpos = k0 + kidx
            valid = (kpos < qpos) & (kpos >= PAD)
            dq = dk = dv = None
            for hh in range(2):
                z = _dot_nt(qs[hh], k) * scale
                lb, lkr = _log_sigmoids(z)
                lk = jnp.where(valid, lkr, 0.0)
                later = tot_ref[:, hh:hh + 1] - (pc_s[hh] + _lane_cumsum(lk, tri_le))
                a = jnp.where(valid, jnp.exp(lb + later), 0.0)
                dl = a * _dot_nt(dos[hh], v)
                early = dc_s[hh] + _lane_cumsum(dl, tri_lt)
                sg = jnp.exp(lb)
                dz = jnp.where(valid, dl * (1.0 - sg) - early * sg, 0.0) * scale
                pc_s[hh] = pc_s[hh] + jnp.sum(lk, axis=1, keepdims=True)
                dc_s[hh] = dc_s[hh] + jnp.sum(dl, axis=1, keepdims=True)
                dzb = dz.astype(BF16)
                x = _dot(dzb, ks[hh])
                y = _dot_tn(dzb, qs[hh])
                w = _dot_tn(a.astype(BF16), dos[hh])
                dq = x if dq is None else dq + x
                dk = y if dk is None else dk + y
                dv = w if dv is None else dv + w
            dq_s[...] += dq
            dk_ref[rows, :] += dk
            dv_ref[rows, :] += dv
            return carry

        walked = jnp.max(tot_ref[:, 2:3]).astype(jnp.int32)
        lax.fori_loop(i + 1 - walked, i + 1, step, 0)
        dq_ref[...] = dq_s[...].astype(dq_ref.dtype)

    whole = lambda c0: pl.BlockSpec((M, 128), lambda j, i: (0, c0 + j))
    blk = pl.BlockSpec((t, 128), lambda j, i: (i, j))
    col = pl.BlockSpec((M, 128), lambda j, i: (0, j))
    return pl.pallas_call(
        body, name=name, grid=(N_PAIRS, M // t),
        in_specs=[blk, whole(ck), whole(cv), blk, pl.BlockSpec((None, t, 3), lambda j, i: (j, i, 0))],
        out_specs=[blk, col, col],
        out_shape=[jax.ShapeDtypeStruct((M, N_PAIRS * 128), BF16), jax.ShapeDtypeStruct((M, N_PAIRS * 128), F32),
                   jax.ShapeDtypeStruct((M, N_PAIRS * 128), F32)],
        scratch_shapes=[pltpu.VMEM((2, t, 1), F32), pltpu.VMEM((2, t, 1), F32), pltpu.VMEM((t, 128), F32)],
        compiler_params=_params(("parallel", "arbitrary")))(qkv, qkv, qkv, do, tot)


def _rows_between(lo, hi):
    r = lax.broadcasted_iota(jnp.int32, (128, 1), 0)
    return (r >= lo) & (r < hi)


def _lanes_between(lo, hi):
    c = lax.broadcasted_iota(jnp.int32, (1, 128), 1)
    return (c >= lo) & (c < hi)


def _keep(x, mask):
    return jnp.where(mask, x, jnp.zeros_like(x))


def _valid_mask(i, kb, t):
    kpos = kb * t + lax.broadcasted_iota(jnp.int32, (t, t), 0)
    qpos = i * t + lax.broadcasted_iota(jnp.int32, (t, t), 1)
    return (kpos <= qpos) & (kpos >= PAD)


def _fox_fwd(qkv, qkv_t, f_rows, f_cols, scale, name):
    M = qkv.shape[0]
    t = WALK_TILE
    first_blk = PAD // t
    ck, cv = N_PAIRS, 2 * N_PAIRS

    def body(qt_ref, k_ref, vt_ref, fq_ref, fk_ref, o_ref, lse_ref, ox_ref, m_s, l_s, acc_s, accx_s, kmax_s):
        i = pl.program_id(1)

        @pl.when(i == 0)
        def _():
            first = _lanes_between(0, 64)

            def block_max(kb, carry):
                kk = k_ref[pl.ds(pl.multiple_of(kb * t, t), t), :].astype(F32)
                kk = kk * kk
                a = jnp.max(jnp.sum(jnp.where(first, kk, 0.0), axis=1, keepdims=True))
                b = jnp.max(jnp.sum(jnp.where(first, 0.0, kk), axis=1, keepdims=True))
                return jnp.maximum(carry[0], a), jnp.maximum(carry[1], b)

            a, b = lax.fori_loop(0, M // t, block_max, (jnp.float32(0.0), jnp.float32(0.0)))
            kmax_s[0] = a
            kmax_s[1] = b

        qt = qt_ref[...]
        qts = (_keep(qt, _rows_between(0, 64)), _keep(qt, _rows_between(64, 128)))
        qf = qt.astype(F32)
        qf = qf * qf
        qbound = tuple(
            (1.001 * scale) * jnp.sqrt(jnp.sum(qf[HEAD_DIM * hh:HEAD_DIM * (hh + 1)], axis=0, keepdims=True) * kmax_s[hh])
            for hh in range(2))
        m_s[...] = jnp.full_like(m_s, NEG)
        l_s[...] = jnp.zeros_like(l_s)
        acc_s[...] = jnp.zeros_like(acc_s)
        accx_s[...] = jnp.zeros_like(accx_s)

        def step(kb, masked):
            k0 = pl.multiple_of(kb * t, t)
            rows = pl.ds(k0, t)
            k = k_ref[rows, :]
            if masked:
                valid = _valid_mask(i, kb, t)
            for hh in range(2):
                s = _dot(k, qts[hh]) * scale + (fq_ref[hh:hh + 1, :] - fk_ref[rows, hh:hh + 1])
                if masked:
                    s = jnp.where(valid, s, NEG)
                m_old = m_s[hh]
                m_new = jnp.maximum(m_old, jnp.max(s, axis=0, keepdims=True))
                p = jnp.exp(s - m_new)
                alpha = jnp.exp(m_old - m_new)
                l_s[hh] = alpha * l_s[hh] + jnp.sum(p, axis=0, keepdims=True)
                m_s[hh] = m_new
                pb = p.astype(BF16)
                hr = slice(HEAD_DIM * hh, HEAD_DIM * (hh + 1))
                vt = vt_ref[hr, rows]
                acc_s[hr, :] = acc_s[hr, :] * alpha + _dot(vt, pb)
                accx_s[hr, :] = accx_s[hr, :] * alpha + _dot(vt, (p - pb.astype(F32)).astype(BF16))

        def keep_going(kb):
            k0 = pl.multiple_of(kb * t, t)
            worst = None
            for hh in range(2):
                f0 = jnp.max(fk_ref[pl.ds(k0, 8), hh:hh + 1])
                w = jnp.max(qbound[hh] + (fq_ref[hh:hh + 1, :] - f0) - m_s[hh])
                worst = w if worst is None else jnp.maximum(worst, w)
            return worst > EXP_ZERO

        def cond(carry):
            kb, go, _ = carry
            return (kb > first_blk) & go

        def walk(carry):
            kb, _, n = carry
            step(kb, False)
            return kb - 1, keep_going(kb), n + 1

        step(i, True)
        _, go, n = lax.while_loop(cond, walk, (i - 1, keep_going(i), jnp.int32(1)))
        first_too = go & (i > first_blk)

        @pl.when(first_too)
        def _():
            step(first_blk, True)

        walked = n + first_too.astype(jnp.int32)
        for hh in range(2):
            hr = slice(HEAD_DIM * hh, HEAD_DIM * (hh + 1))
            inv = 1.0 / l_s[hh]
            o_ref[hr, :] = (acc_s[hr, :] * inv).astype(o_ref.dtype)
            ox_ref[hr, :] = (acc_s[hr, :] + accx_s[hr, :]) * inv
            lse_ref[hh:hh + 1, :] = m_s[hh] + jnp.log(l_s[hh])
        lse_ref[2:3, :] = jnp.full((1, t), walked.astype(F32))

    blk = pl.BlockSpec((128, t), lambda j, i: (j, i))
    stat = pl.BlockSpec((None, 2, t), lambda j, i: (j, 0, i))
    return pl.pallas_call(
        body, name=name, grid=(N_PAIRS, M // t),
        in_specs=[blk, pl.BlockSpec((M, 128), lambda j, i: (0, ck + j)), pl.BlockSpec((128, M), lambda j, i: (cv + j, 0)),
                  stat, pl.BlockSpec((None, M, 2), lambda j, i: (j, 0, 0))],
        out_specs=[blk, pl.BlockSpec((None, 3, t), lambda j, i: (j, 0, i)), blk],
        out_shape=[jax.ShapeDtypeStruct((N_PAIRS * 128, M), BF16), jax.ShapeDtypeStruct((N_PAIRS, 3, M), F32),
                   jax.ShapeDtypeStruct((N_PAIRS * 128, M), F32)],
        scratch_shapes=[pltpu.VMEM((2, 1, t), F32), pltpu.VMEM((2, 1, t), F32), pltpu.VMEM((128, t), F32),
                        pltpu.VMEM((128, t), F32), pltpu.SMEM((2,), F32)],
        compiler_params=_params(("parallel", "arbitrary")))(qkv_t, qkv, qkv_t, f_rows, f_cols)


def _fox_bwd(qkv, qkv_t, o_t, do, do_t, lse, f_rows, f_cols, scale, name):
    M = qkv.shape[0]
    t = WALK_TILE
    first_blk = PAD // t
    ck, cv = N_PAIRS, 2 * N_PAIRS

    def body(q_ref, qt_ref, k_ref, kt_ref, v_ref, ot_ref, do_ref, dot_ref, lse_ref, fq_ref, fk_ref,
             dq_ref, dk_ref, dv_ref, cs_ref, dq_s):
        i = pl.program_id(1)

        @pl.when(i == 0)
        def _():
            dk_ref[...] = jnp.zeros_like(dk_ref)
            dv_ref[...] = jnp.zeros_like(dv_ref)
            cs_ref[...] = jnp.zeros_like(cs_ref)

        heads_l = (_lanes_between(0, 64), _lanes_between(64, 128))
        heads_r = (_rows_between(0, 64), _rows_between(64, 128))
        q = q_ref[...]
        qt = qt_ref[...]
        do = do_ref[...]
        dot = dot_ref[...]
        qs = tuple(_keep(q, m) for m in heads_l)
        qts = tuple(_keep(qt, m) for m in heads_r)
        dos = tuple(_keep(do, m) for m in heads_l)
        dots = tuple(_keep(dot, m) for m in heads_r)
        prod = dot.astype(F32) * ot_ref[...]
        deltas = tuple(jnp.sum(prod[HEAD_DIM * hh:HEAD_DIM * (hh + 1)], axis=0, keepdims=True) for hh in range(2))
        ones = tuple(m.astype(BF16) * jnp.ones((t, 128), BF16) for m in heads_l)
        dq_s[...] = jnp.zeros_like(dq_s)

        def step(kb, masked):
            k0 = pl.multiple_of(kb * t, t)
            rows = pl.ds(k0, t)
            k = k_ref[rows, :]
            v = v_ref[rows, :]
            if masked:
                valid = _valid_mask(i, kb, t)
            dk = dv = cs = None
            for hh in range(2):
                s = _dot(k, qts[hh]) * scale + (fq_ref[hh:hh + 1, :] - fk_ref[rows, hh:hh + 1])
                if masked:
                    s = jnp.where(valid, s, NEG)
                p = jnp.exp(s - lse_ref[hh:hh + 1, :])
                ds = p * (_dot(v, dots[hh]) - deltas[hh])
                hi = ds.astype(BF16)
                lo = (ds - hi.astype(F32)).astype(BF16)
                c = _dot(hi, ones[hh]) + _dot(lo, ones[hh])
                dsb = (ds * scale).astype(BF16)
                hr = slice(HEAD_DIM * hh, HEAD_DIM * (hh + 1))
                dq_s[hr, :] += _dot(kt_ref[hr, rows], dsb)
                a = _dot(dsb, qs[hh])
                b = _dot(p.astype(BF16), dos[hh])
                dk = a if dk is None else dk + a
                dv = b if dv is None else dv + b
                cs = c if cs is None else cs + c
            dk_ref[rows, :] += dk
            dv_ref[rows, :] += dv
            cs_ref[rows, :] += cs

        first_walked = i + 1 - jnp.max(lse_ref[2:3, :]).astype(jnp.int32)

        def mid(kb, carry):
            step(kb, False)
            return carry

        @pl.when((first_walked == first_blk) & (i > first_blk))
        def _():
            step(first_blk, True)

        lax.fori_loop(jnp.maximum(first_walked, first_blk + 1), i, mid, 0)
        step(i, True)
        dq_ref[...] = dq_s[...].astype(dq_ref.dtype)

    rblk = pl.BlockSpec((t, 128), lambda j, i: (i, j))
    tblk = pl.BlockSpec((128, t), lambda j, i: (j, i))
    stat = pl.BlockSpec((None, 2, t), lambda j, i: (j, 0, i))
    stat3 = pl.BlockSpec((None, 3, t), lambda j, i: (j, 0, i))
    col = pl.BlockSpec((M, 128), lambda j, i: (0, j))
    wide = jax.ShapeDtypeStruct((M, N_PAIRS * 128), F32)
    return pl.pallas_call(
        body, name=name, grid=(N_PAIRS, M // t),
        in_specs=[rblk, tblk, pl.BlockSpec((M, 128), lambda j, i: (0, ck + j)),
                  pl.BlockSpec((128, M), lambda j, i: (ck + j, 0)), pl.BlockSpec((M, 128), lambda j, i: (0, cv + j)),
                  tblk, rblk, tblk, stat3, stat, pl.BlockSpec((None, M, 2), lambda j, i: (j, 0, 0))],
        out_specs=[tblk, col, col, pl.BlockSpec((None, M, 128), lambda j, i: (j, 0, 0))],
        out_shape=[jax.ShapeDtypeStruct((N_PAIRS * 128, M), BF16), wide, wide,
                   jax.ShapeDtypeStruct((N_PAIRS, M, 128), F32)],
        scratch_shapes=[pltpu.VMEM((128, t), F32)],
        compiler_params=_params(("parallel", "arbitrary")))(qkv, qkv_t, qkv, qkv_t, qkv, o_t, do, do_t, lse,
                                                            f_rows, f_cols)


def _rope_tables(M):
    pos = (jnp.arange(M, dtype=jnp.int32) - PAD).astype(F32)
    inv = ROPE_THETA ** (-jnp.arange(0, MLA_ROPE, 2, dtype=F32) / MLA_ROPE)
    ang = pos[:, None] * inv[None, :]
    cos, sin = jnp.cos(ang), jnp.sin(ang)
    z = jnp.zeros((M, 64), F32)
    cos_t = jnp.concatenate([cos, cos, cos, cos, z], axis=1)
    sin_t = jnp.concatenate([-sin, sin, -sin, sin, z], axis=1)
    return cos_t, sin_t


def _rope(x, cos_t, sin_t, out_dtype, name, inverse=False, lead=0):
    M, C = x.shape
    tm = _pick(M, (768, 512, 256, 128))
    nblk = (C - lead) // 128
    sign = -1.0 if inverse else 1.0

    def body(x_ref, c_ref, s_ref, o_ref):
        lane = lax.broadcasted_iota(jnp.int32, (1, 128), 1)
        low = (lane % MLA_ROPE) < (MLA_ROPE // 2)
        cos = c_ref[...]
        sin = s_ref[...] * sign
        if lead:
            o_ref[:, :lead] = x_ref[:, :lead].astype(o_ref.dtype)
        for b in range(nblk):
            cols = slice(lead + b * 128, lead + (b + 1) * 128)
            v = x_ref[:, cols].astype(F32)
            up = pltpu.roll(v, 128 - MLA_ROPE // 2, 1)
            down = pltpu.roll(v, MLA_ROPE // 2, 1)
            o_ref[:, cols] = (v * cos + jnp.where(low, up, down) * sin).astype(o_ref.dtype)

    row = pl.BlockSpec((tm, C), lambda m: (m, 0))
    tab = pl.BlockSpec((tm, 128), lambda m: (m, 0))
    return pl.pallas_call(
        body, name=name, grid=(M // tm,), in_specs=[row, tab, tab], out_specs=row,
        out_shape=jax.ShapeDtypeStruct((M, C), out_dtype),
        compiler_params=_params(("parallel",)))(x, cos_t, sin_t)


def _forget_cumsum(f_logit, bias, name):
    M = f_logit.shape[0]
    tm = 256

    def body(f_ref, b_ref, o_ref, c_s):
        i = pl.program_id(0)

        @pl.when(i == 0)
        def _():
            c_s[...] = jnp.zeros_like(c_s)
        ls, _ = _log_sigmoids(f_ref[...] + b_ref[...])
        rows = i * tm + lax.broadcasted_iota(jnp.int32, (tm, 1), 0)
        ls = jnp.where(rows >= PAD, ls, 0.0)
        r = lax.broadcasted_iota(jnp.int32, (tm, tm), 0)
        c = lax.broadcasted_iota(jnp.int32, (tm, tm), 1)
        tri = (c <= r).astype(F32)
        cum = jnp.dot(tri, ls, precision=lax.Precision.HIGHEST, preferred_element_type=F32) + c_s[...]
        o_ref[...] = cum
        c_s[...] = cum[tm - 1:tm, :]

    row = pl.BlockSpec((tm, 128), lambda m: (m, 0))
    return pl.pallas_call(
        body, name=name, grid=(M // tm,),
        in_specs=[row, pl.BlockSpec((1, 128), lambda m: (0, 0))], out_specs=row,
        out_shape=jax.ShapeDtypeStruct((M, 128), F32), scratch_shapes=[pltpu.VMEM((1, 128), F32)],
        compiler_params=_params(("arbitrary",)))(f_logit, bias)


def _forget_cumsum_bwd(f_logit, bias, colsum, name):
    M = f_logit.shape[0]
    tm = 256
    nb = M // tm

    def body(f_ref, b_ref, cs_ref, o_ref, db_ref, c_s):
        i = pl.program_id(0)

        @pl.when(i == 0)
        def _():
            c_s[...] = jnp.zeros_like(c_s)
            db_ref[...] = jnp.zeros_like(db_ref)
        rr = lax.broadcasted_iota(jnp.int32, (128, 128), 0)
        cc = lax.broadcasted_iota(jnp.int32, (128, 128), 1)
        dF = None
        for j in range(N_PAIRS):
            sel = (((rr == 0) & (cc == 2 * j)) | ((rr == HEAD_DIM) & (cc == 2 * j + 1))).astype(F32)
            d = jnp.dot(cs_ref[j], sel, precision=lax.Precision.HIGHEST, preferred_element_type=F32)
            dF = d if dF is None else dF + d
        r = lax.broadcasted_iota(jnp.int32, (tm, tm), 0)
        c = lax.broadcasted_iota(jnp.int32, (tm, tm), 1)
        tri = (c >= r).astype(F32)
        cum = c_s[...] - jnp.dot(tri, dF, precision=lax.Precision.HIGHEST, preferred_element_type=F32)
        c_s[...] = cum[0:1, :]
        _, lsn = _log_sigmoids(f_ref[...] + b_ref[...])
        rows = (nb - 1 - i) * tm + lax.broadcasted_iota(jnp.int32, (tm, 1), 0)
        dl = jnp.where(rows >= PAD, cum * jnp.exp(lsn), 0.0)
        o_ref[...] = dl
        db_ref[...] += jnp.sum(dl, axis=0, keepdims=True)

    row = pl.BlockSpec((tm, 128), lambda m: (nb - 1 - m, 0))
    vec = pl.BlockSpec((1, 128), lambda m: (0, 0))
    return pl.pallas_call(
        body, name=name, grid=(nb,),
        in_specs=[row, vec, pl.BlockSpec((N_PAIRS, tm, 128), lambda m: (0, nb - 1 - m, 0))], out_specs=[row, vec],
        out_shape=[jax.ShapeDtypeStruct((M, 128), F32), jax.ShapeDtypeStruct((1, 128), F32)],
        scratch_shapes=[pltpu.VMEM((1, 128), F32)],
        compiler_params=_params(("arbitrary",)))(f_logit, bias, colsum)


def _adamw(w, parts, m, v, name):
    R, C = w.shape
    n_parts = parts.shape[0]
    tr = R
    for d in range(8, R, 8):
        if R % d == 0 and d * C <= ADAM_TILE_ELEMS:
            tr = d
    c1 = 1.0 - ADAM_B1 ** ADAM_STEP
    c2 = 1.0 - ADAM_B2 ** ADAM_STEP

    def body(w_ref, s_ref, m_ref, v_ref, g_ref, d_ref, mo_ref, vo_ref):
        g = s_ref[0].astype(F32)
        for k in range(1, n_parts):
            g = g + s_ref[k].astype(F32)
        mn = ADAM_B1 * m_ref[...] + (1.0 - ADAM_B1) * g
        vn = ADAM_B2 * v_ref[...] + (1.0 - ADAM_B2) * (g * g)
        m_hat = mn / c1
        v_hat = vn / c2
        g_ref[...] = g
        d_ref[...] = -ADAM_LR * (m_hat / (jnp.sqrt(v_hat) + ADAM_EPS) + ADAM_WD * w_ref[...])
        mo_ref[...] = mn
        vo_ref[...] = vn

    row = pl.BlockSpec((tr, C), lambda r: (r, 0))
    shp = jax.ShapeDtypeStruct((R, C), F32)
    return pl.pallas_call(
        body, name=name, grid=(R // tr,),
        in_specs=[row, pl.BlockSpec((n_parts, tr, C), lambda r: (0, r, 0)), row, row],
        out_specs=[row, row, row, row], out_shape=[shp, shp, shp, shp],
        compiler_params=_params(("parallel",)))(w, parts, m, v)


def _position():
    return lax.axis_index("x"), lax.axis_index("y"), lax.axis_index("c")


def _all_gather(blocks, name):
    n = len(blocks)

    def body(*refs):
        x_refs, out_refs = refs[:n], refs[n:2 * n]
        send_sems, recv_sems, local_sems = refs[2 * n:]
        x, y, c = _position()
        me, sibling = (x, y, c), (x, y, 1 - c)
        chips = [(1 - x, y), (x, 1 - y), (1 - x, 1 - y)]

        def copies(k, block, to, own=False):
            slot = 4 * block[0] + 2 * block[1] + block[2]
            return [pltpu.make_async_remote_copy(
                src_ref=x_refs[p] if own else out_refs[p].at[slot], dst_ref=out_refs[p].at[slot],
                send_sem=send_sems.at[k, p], recv_sem=recv_sems.at[k, p], device_id=to, device_id_type=MESH)
                for p in range(n)]

        mine = [pltpu.make_async_copy(x_refs[p], out_refs[p].at[4 * x + 2 * y + c], local_sems.at[p]) for p in range(n)]
        for cp in mine:
            cp.start()
        first = copies(0, me, sibling, own=True)
        for j, chip in enumerate(chips):
            first += copies(1 + j, me, (*chip, c), own=True)
        for cp in first:
            cp.start()
        passed = []
        for j, chip in enumerate(chips):
            for cp in copies(1 + j, (*chip, c), me):
                cp.wait_recv()
            onward = copies(4 + j, (*chip, c), sibling)
            for cp in onward:
                cp.start()
            passed += onward
        for cp in copies(0, sibling, me):
            cp.wait_recv()
        for j, chip in enumerate(chips):
            for cp in copies(4 + j, (*chip, 1 - c), me):
                cp.wait_recv()
        for cp in first + passed:
            cp.wait_send()
        for cp in mine:
            cp.wait()

    any_spec = pl.BlockSpec(memory_space=pl.ANY)
    return pl.pallas_call(
        body, name=name, out_shape=[jax.ShapeDtypeStruct((N_DEV,) + b.shape, b.dtype) for b in blocks],
        in_specs=[any_spec] * n, out_specs=[any_spec] * n,
        scratch_shapes=[pltpu.SemaphoreType.DMA((7, n)), pltpu.SemaphoreType.DMA((7, n)), pltpu.SemaphoreType.DMA((n,))],
    )(*blocks)


N_CHIPS = 4


def _exchange_siblings(parts, name):
    n = len(parts)

    def body(*refs):
        g_refs, land_refs = refs[:n], refs[n:2 * n]
        send_sems, recv_sems = refs[2 * n:]
        x, y, c = _position()
        sibling = (x, y, 1 - c)
        sends, recvs = [], []
        for q in range(N_CHIPS):
            for p in range(n):
                sends.append(pltpu.make_async_remote_copy(
                    src_ref=g_refs[p].at[2 * q + (1 - c)], dst_ref=land_refs[p].at[q], send_sem=send_sems.at[q, p],
                    recv_sem=recv_sems.at[q, p], device_id=sibling, device_id_type=MESH))
                recvs.append(pltpu.make_async_remote_copy(
                    src_ref=g_refs[p].at[2 * q + c], dst_ref=land_refs[p].at[q], send_sem=send_sems.at[q, p],
                    recv_sem=recv_sems.at[q, p], device_id=sibling, device_id_type=MESH))
        for cp in sends:
            cp.start()
        for cp in recvs:
            cp.wait_recv()
        for cp in sends:
            cp.wait_send()

    any_spec = pl.BlockSpec(memory_space=pl.ANY)
    return pl.pallas_call(
        body, name=name, out_shape=[jax.ShapeDtypeStruct((N_CHIPS,) + p.shape[1:], p.dtype) for p in parts],
        in_specs=[any_spec] * n, out_specs=[any_spec] * n,
        scratch_shapes=[pltpu.SemaphoreType.DMA((N_CHIPS, n)), pltpu.SemaphoreType.DMA((N_CHIPS, n))],
    )(*parts)


def _pair_sum(part, from_sibling, name):
    _, R, C = part.shape
    tr = R
    for d in range(8, R, 8):
        if R % d == 0 and d * C <= ADAM_TILE_ELEMS:
            tr = d

    def body(a_ref, b_ref, o_ref):
        c = lax.axis_index("c")
        for q in range(N_CHIPS):
            o_ref[q] = (a_ref[2 * q + c].astype(F32) + b_ref[q].astype(F32)).astype(o_ref.dtype)

    return pl.pallas_call(
        body, name=name, grid=(R // tr,),
        in_specs=[pl.BlockSpec((N_DEV, tr, C), lambda r: (0, r, 0)), pl.BlockSpec((N_CHIPS, tr, C), lambda r: (0, r, 0))],
        out_specs=pl.BlockSpec((N_CHIPS, tr, C), lambda r: (0, r, 0)),
        out_shape=jax.ShapeDtypeStruct((N_CHIPS, R, C), part.dtype),
        compiler_params=_params(("parallel",)))(part, from_sibling)


def _exchange_chips(sums, name):
    n = len(sums)

    def body(*refs):
        g_refs, land_refs = refs[:n], refs[n:2 * n]
        send_sems, recv_sems, local_sems = refs[2 * n:]
        x, y, c = _position()
        me = 2 * x + y
        mine = [pltpu.make_async_copy(g_refs[p].at[me], land_refs[p].at[me], local_sems.at[p]) for p in range(n)]
        for cp in mine:
            cp.start()
        sends, recvs = [], []
        for k in range(1, N_CHIPS):
            px = 1 - x if k & 2 else x
            py = 1 - y if k & 1 else y
            peer = 2 * px + py
            for p in range(n):
                sends.append(pltpu.make_async_remote_copy(
                    src_ref=g_refs[p].at[peer], dst_ref=land_refs[p].at[me], send_sem=send_sems.at[k - 1, p],
                    recv_sem=recv_sems.at[k - 1, p], device_id=(px, py, c), device_id_type=MESH))
                recvs.append(pltpu.make_async_remote_copy(
                    src_ref=g_refs[p].at[me], dst_ref=land_refs[p].at[peer], send_sem=send_sems.at[k - 1, p],
                    recv_sem=recv_sems.at[k - 1, p], device_id=(px, py, c), device_id_type=MESH))
        for cp in sends:
            cp.start()
        for cp in recvs:
            cp.wait_recv()
        for cp in sends:
            cp.wait_send()
        for cp in mine:
            cp.wait()

    any_spec = pl.BlockSpec(memory_space=pl.ANY)
    return pl.pallas_call(
        body, name=name, out_shape=[jax.ShapeDtypeStruct(p.shape, p.dtype) for p in sums],
        in_specs=[any_spec] * n, out_specs=[any_spec] * n,
        scratch_shapes=[pltpu.SemaphoreType.DMA((3, n)), pltpu.SemaphoreType.DMA((3, n)), pltpu.SemaphoreType.DMA((n,))],
    )(*sums)


SHARDED = (("sb_w_qkv", 2), ("sb_w_o", 1), ("mla_w_down", 1), ("mla_w_uq", 2), ("mla_w_ukv", 2), ("mla_w_o", 1),
           ("fox_w_qkvf", 2), ("fox_w_o", 1), ("ffn_w_gate", 2), ("ffn_w_up", 2), ("ffn_w_down", 1),
           ("pool_w", 2), ("meta", 1), ("mla_q_norm", 1), ("mla_kv_norm", 1))
KEPT_F32 = ("meta", "mla_q_norm", "mla_kv_norm")
REPLICATED = ("norm_mix", "norm_ffn", "pool_scale", "fox_b_f", "final_norm")
WEIGHT_NAMES = ("meta", "norm_mix", "norm_ffn", "pool_w", "pool_scale", "sb_w_qkv", "sb_w_o", "mla_w_down",
                "mla_q_norm", "mla_kv_norm", "mla_w_uq", "mla_w_ukv", "mla_w_o", "fox_w_qkvf", "fox_b_f",
                "fox_w_o", "ffn_w_gate", "ffn_w_up", "ffn_w_down", "final_norm")
LANES = 1024


def _pack_rows(arrays, names):
    parts = []
    for n in names:
        flat = arrays[n].reshape(-1).astype(F32)
        rows = -(-flat.shape[0] // LANES)
        parts.append(jnp.pad(flat, (0, rows * LANES - flat.shape[0])).reshape(rows, LANES))
    rows = sum(p.shape[0] for p in parts)
    parts.append(jnp.zeros((-(-rows // 8) * 8 - rows, LANES), F32))
    return jnp.concatenate(parts, axis=0)


def _unpack_rows(buf, shapes, names):
    out, row = {}, 0
    for n in names:
        size = int(np.prod(shapes[n]))
        rows = -(-size // LANES)
        out[n] = buf[row:row + rows].reshape(-1)[:size].reshape(shapes[n])
        row += rows
    return out


def _whole_from_gathered(g, axis):
    g = jnp.moveaxis(g, 0, axis)
    shp = g.shape
    return g.reshape(shp[:axis] + (shp[axis] * shp[axis + 1],) + shp[axis + 2:])


def _parts_from_whole(whole, axis):
    shp = whole.shape
    g = whole.reshape(shp[:axis] + (N_DEV, shp[axis] // N_DEV) + shp[axis + 1:])
    return jnp.moveaxis(g, axis, 0)


def _kernel_weights(full):
    W = {}
    W["pool_w"] = full["pool_w"][0]
    W["sb_w_qkv"] = full["sb_w_qkv"][0]
    W["sb_w_o"] = full["sb_w_o"][0]
    W["mla_w_down"] = full["mla_w_down"][0]
    uq = full["mla_w_uq"][0].reshape(MLA_Q_RANK, N_HEADS, MLA_NOPE + MLA_ROPE)
    nope = uq[:, :, :MLA_NOPE].reshape(MLA_Q_RANK, N_HEADS * MLA_NOPE)
    rope = uq[:, :, MLA_NOPE:].reshape(MLA_Q_RANK, N_PAIRS, 2 * MLA_ROPE)
    rope = jnp.pad(rope, ((0, 0), (0, 0), (0, 128 - 2 * MLA_ROPE))).reshape(MLA_Q_RANK, N_PAIRS * 128)
    W["mla_w_uq"] = jnp.concatenate([nope, rope], axis=1)
    ukv = full["mla_w_ukv"][0].reshape(MLA_KV_RANK, N_HEADS, 2, HEAD_DIM)
    W["mla_w_ukv"] = jnp.transpose(ukv, (0, 2, 1, 3)).reshape(MLA_KV_RANK, 2 * N_HEADS * HEAD_DIM)
    W["mla_w_o"] = full["mla_w_o"][0]
    qkvf = full["fox_w_qkvf"][0]
    n_qkv = 3 * N_HEADS * HEAD_DIM
    W["fox_w_qkv"] = qkvf[:, :n_qkv]
    W["fox_w_f"] = jnp.pad(qkvf[:, n_qkv:], ((0, 0), (0, 128 - N_HEADS)))
    W["fox_w_qkvf"] = jnp.concatenate([W["fox_w_qkv"], W["fox_w_f"]], axis=1)
    W["fox_w_o"] = full["fox_w_o"][0]
    W["ffn_w_gate"] = full["ffn_w_gate"]
    W["ffn_w_up"] = full["ffn_w_up"]
    W["ffn_w_down"] = full["ffn_w_down"]
    return W


def _reference_grads(G):
    out = {}
    out["pool_w"] = G["pool_w"][None]
    for n in ("sb_w_qkv", "sb_w_o", "mla_w_down", "mla_w_o", "fox_w_o"):
        out[n] = G[n][None]
    duq = G["mla_w_uq"]
    nope = duq[:, :N_HEADS * MLA_NOPE].reshape(MLA_Q_RANK, N_HEADS, MLA_NOPE)
    rope = duq[:, N_HEADS * MLA_NOPE:].reshape(MLA_Q_RANK, N_PAIRS, 128)[:, :, :2 * MLA_ROPE]
    rope = rope.reshape(MLA_Q_RANK, N_HEADS, MLA_ROPE)
    out["mla_w_uq"] = jnp.concatenate([nope, rope], axis=2).reshape(1, MLA_Q_RANK, -1)
    dukv = G["mla_w_ukv"].reshape(MLA_KV_RANK, 2, N_HEADS, HEAD_DIM)
    out["mla_w_ukv"] = jnp.transpose(dukv, (0, 2, 1, 3)).reshape(1, MLA_KV_RANK, -1)
    out["fox_w_qkvf"] = G["fox_w_qkvf"][None, :, :3 * N_HEADS * HEAD_DIM + N_HEADS]
    out["ffn_w_gate"] = G["ffn_w_gate"]
    out["ffn_w_up"] = G["ffn_w_up"]
    out["ffn_w_down"] = G["ffn_w_down"]
    out["mla_q_norm"] = G["mla_q_norm"]
    out["mla_kv_norm"] = G["mla_kv_norm"]
    return out


def _pairs_col(f16):
    M = f16.shape[0]
    return jnp.transpose(f16.reshape(M, N_PAIRS, 2), (1, 0, 2))


def _pairs_row(f16):
    M = f16.shape[0]
    return jnp.transpose(f16.reshape(M, N_PAIRS, 2), (1, 2, 0))


def _local_step(x, target, W, P):
    S = x.shape[0]
    M = S + ROW0
    G = {}
    gain = lambda name, i: P[name][i][None, :]
    h0 = jnp.concatenate([jnp.zeros((PAD, D_MODEL), F32), P["meta"], x], axis=0)

    def ffn_fwd(h1, i):
        b = _norm_fwd(h1, gain("norm_ffn", i), BF16, f"ffn{i}_norm")
        g, u, act = _ffn_up(b, W["ffn_w_gate"][i], W["ffn_w_up"][i], f"ffn{i}_up")
        h2 = _mm_nn(act, W["ffn_w_down"][i], F32, f"ffn{i}_down", res=h1)
        return h2, (h1, b, g, u, act)

    def ffn_bwd(dh2, saved, i):
        h1, b, g, u, act = saved
        dg, du = _ffn_dact(dh2, W["ffn_w_down"][i], g, u, f"ffn{i}_dact")
        G.setdefault("ffn_w_down", {})[i] = _mm_tn(act, dh2, f"ffn{i}_dwd")
        db = _mm_nt2(dg, W["ffn_w_gate"][i], du, W["ffn_w_up"][i], F32, f"ffn{i}_db")
        G.setdefault("ffn_w_gate", {})[i] = _mm_tn(b, dg, f"ffn{i}_dwg")
        G.setdefault("ffn_w_up", {})[i] = _mm_tn(b, du, f"ffn{i}_dwu")
        dh1, dgain = _norm_bwd(h1, gain("norm_ffn", i), db, dh2, f"ffn{i}_dnorm")
        G.setdefault("norm_ffn", {})[i] = dgain
        return dh1

    a0 = _norm_fwd(h0, gain("norm_mix", 0), F32, "mix0_norm")
    h1_0, pooled = _pool_fwd(h0, a0, W["pool_w"], P["pool_scale"], "pool_fwd")
    h_1, ffn0 = ffn_fwd(h1_0, 0)

    sb_scale = HEAD_DIM ** -0.5
    a1 = _norm_fwd(h_1, gain("norm_mix", 1), BF16, "mix1_norm")
    sb_qkv = _mm_nn(a1, W["sb_w_qkv"], BF16, "sb_qkv")
    sb_o, sb_tot = _sb_fwd(sb_qkv, sb_scale, "sb_fwd")
    h1_1 = _mm_nn(sb_o, W["sb_w_o"], F32, "sb_out", res=h_1)
    h_2, ffn1 = ffn_fwd(h1_1, 1)

    mla_scale = (MLA_NOPE + MLA_ROPE) ** -0.5
    cos_t, sin_t = _rope_tables(M)
    a2 = _norm_fwd(h_2, gain("norm_mix", 2), BF16, "mix2_norm")
    down = _mm_nn(a2, W["mla_w_down"], F32, "mla_down")
    dq_raw = down[:, :MLA_Q_RANK]
    dkv_raw = down[:, MLA_Q_RANK:MLA_Q_RANK + MLA_KV_RANK]
    kr_raw = down[:, MLA_Q_RANK + MLA_KV_RANK:]
    c_q = _norm_fwd(dq_raw, P["mla_q_norm"], BF16, "mla_qnorm")
    c_kv = _norm_fwd(dkv_raw, P["mla_kv_norm"], BF16, "mla_kvnorm")
    q_lin = _mm_nn(c_q, W["mla_w_uq"], F32, "mla_uq")
    q_all = _rope(q_lin, cos_t, sin_t, BF16, "mla_qrope", lead=D_MODEL)
    kv_all = _mm_nn(c_kv, W["mla_w_ukv"], BF16, "mla_ukv")
    kr_in = jnp.concatenate([kr_raw, kr_raw, jnp.zeros((M, 64), F32)], axis=1)
    kr = _rope(kr_in, cos_t, sin_t, BF16, "mla_krope")
    q_rope = q_all[:, D_MODEL:]
    mla_o, mla_lse = _mla_fwd(q_all, kv_all, q_rope, kr, mla_scale, "mla_fwd")
    h1_2 = _mm_nn(mla_o, W["mla_w_o"], F32, "mla_out", res=h_2)
    h_3, ffn2 = ffn_fwd(h1_2, 2)

    fox_scale = HEAD_DIM ** -0.5
    a3 = _norm_fwd(h_3, gain("norm_mix", 3), BF16, "mix3_norm")
    fox_qkv = _mm_nn(a3, W["fox_w_qkv"], BF16, "fox_qkv")
    f_logit = _mm_nn(a3, W["fox_w_f"], F32, "fox_f")
    b_f = jnp.pad(P["fox_b_f"], ((0, 0), (0, 128 - N_HEADS)))
    Fc = _forget_cumsum(f_logit, b_f, "fox_cumsum")
    f_rows, f_cols = _pairs_row(Fc[:, :N_HEADS]), _pairs_col(Fc[:, :N_HEADS])
    fox_qkv_t = fox_qkv.T
    fox_o_t, fox_lse, fox_ox_t = _fox_fwd(fox_qkv, fox_qkv_t, f_rows, f_cols, fox_scale, "fox_fwd")
    fox_o = fox_o_t.T
    h1_3 = _mm_nn(fox_o, W["fox_w_o"], F32, "fox_out", res=h_3)
    h_4, ffn3 = ffn_fwd(h1_3, 3)

    sq, dh, dgain = _loss_head(h_4, P["final_norm"][None, :], target, "loss_head")
    G["final_norm"] = dgain[0]

    dh = ffn_bwd(dh, ffn3, 3)
    do = _mm_nt(dh, W["fox_w_o"], BF16, "fox_do")
    G["fox_w_o"] = _mm_tn(fox_o, dh, "fox_dwo")
    dq_t, dk, dv, colsum = _fox_bwd(fox_qkv, fox_qkv_t, fox_ox_t, do, do.T, fox_lse, f_rows, f_cols, fox_scale,
                                    "fox_bwd")
    dlogit, db_f = _forget_cumsum_bwd(f_logit, b_f, colsum, "fox_dcumsum")
    G["fox_b_f"] = db_f[:, :N_HEADS]
    dproj = jnp.concatenate([dq_t.T, dk.astype(BF16), dv.astype(BF16), dlogit.astype(BF16)], axis=1)
    da = _mm_nt(dproj, W["fox_w_qkvf"], F32, "fox_da")
    G["fox_w_qkvf"] = _mm_tn(a3, dproj, "fox_dwqkvf")
    dh, dgain = _norm_bwd(h_3, gain("norm_mix", 3), da, dh, "mix3_dnorm")
    G.setdefault("norm_mix", {})[3] = dgain

    dh = ffn_bwd(dh, ffn2, 2)
    do = _mm_nt(dh, W["mla_w_o"], BF16, "mla_do")
    G["mla_w_o"] = _mm_tn(mla_o, dh, "mla_dwo")
    dq, dk, dv, dqr, dkr = _mla_bwd(q_all, kv_all, q_rope, kr, mla_o, do, mla_lse, mla_scale, "mla_bwd")
    dqr = _rope(dqr, cos_t, sin_t, BF16, "mla_dqrope", inverse=True)
    dq_all = jnp.concatenate([dq, dqr], axis=1)
    dkr_sum = _rope(jnp.sum(dkr, axis=0), cos_t, sin_t, F32, "mla_dkrope", inverse=True)
    dkr_raw = dkr_sum[:, :MLA_ROPE] + dkr_sum[:, MLA_ROPE:2 * MLA_ROPE]
    dkv_all = jnp.concatenate([dk.astype(BF16), dv.astype(BF16)], axis=1)
    dc_q = _mm_nt(dq_all, W["mla_w_uq"], F32, "mla_dcq")
    G["mla_w_uq"] = _mm_tn(c_q, dq_all, "mla_dwuq")
    dc_kv = _mm_nt(dkv_all, W["mla_w_ukv"], F32, "mla_dckv")
    G["mla_w_ukv"] = _mm_tn(c_kv, dkv_all, "mla_dwukv")
    ddq_raw, G["mla_q_norm"] = _norm_bwd(dq_raw, P["mla_q_norm"], dc_q, None, "mla_dqnorm")
    ddkv_raw, G["mla_kv_norm"] = _norm_bwd(dkv_raw, P["mla_kv_norm"], dc_kv, None, "mla_dkvnorm")
    ddown = jnp.concatenate([ddq_raw, ddkv_raw, dkr_raw], axis=1).astype(BF16)
    da = _mm_nt(ddown, W["mla_w_down"], F32, "mla_da")
    G["mla_w_down"] = _mm_tn(a2, ddown, "mla_dwdown")
    dh, dgain = _norm_bwd(h_2, gain("norm_mix", 2), da, dh, "mix2_dnorm")
    G["norm_mix"][2] = dgain

    dh = ffn_bwd(dh, ffn1, 1)
    do = _mm_nt(dh, W["sb_w_o"], BF16, "sb_do")
    G["sb_w_o"] = _mm_tn(sb_o, dh, "sb_dwo")
    dq, dk, dv = _sb_bwd(sb_qkv, do, sb_tot, sb_scale, "sb_bwd")
    dqkv = jnp.concatenate([dq, dk.astype(BF16), dv.astype(BF16)], axis=1)
    da = _mm_nt(dqkv, W["sb_w_qkv"], F32, "sb_da")
    G["sb_w_qkv"] = _mm_tn(a1, dqkv, "sb_dwqkv")
    dh, dgain = _norm_bwd(h_1, gain("norm_mix", 1), da, dh, "mix1_dnorm")
    G["norm_mix"][1] = dgain

    dh = ffn_bwd(dh, ffn0, 0)
    dpc, G["pool_w"], G["pool_scale"] = _pool_bwd_mix(dh, pooled, W["pool_w"], P["pool_scale"], "pool_dmix")
    da = _pool_bwd_window(dpc, "pool_dwindow")
    dh, dgain, dx = _norm_bwd(h0, gain("norm_mix", 0), da, dh, "mix0_dnorm", token_rows=True)
    G["norm_mix"][0] = dgain

    G["norm_mix"] = jnp.concatenate([G["norm_mix"][i] for i in range(DEPTH)], axis=0)
    G["norm_ffn"] = jnp.concatenate([G["norm_ffn"][i] for i in range(DEPTH)], axis=0)
    G["ffn_w_down"] = jnp.stack([G["ffn_w_down"][i] for i in range(DEPTH)])
    G["ffn_w_gate"] = jnp.stack([G["ffn_w_gate"][i] for i in range(DEPTH)])
    G["ffn_w_up"] = jnp.stack([G["ffn_w_up"][i] for i in range(DEPTH)])
    G["meta"] = dh[PAD:ROW0]
    return sq, dx, G


def kernel(x, meta, norm_mix, norm_ffn, pool_w, pool_scale, sb_w_qkv, sb_w_o, mla_w_down, mla_q_norm, mla_kv_norm, mla_w_uq, mla_w_ukv, mla_w_o, fox_w_qkvf, fox_b_f, fox_w_o, ffn_w_gate, ffn_w_up, ffn_w_down, final_norm, loss_target, m_meta, m_norm_mix, m_norm_ffn, m_pool_w, m_pool_scale, m_sb_w_qkv, m_sb_w_o, m_mla_w_down, m_mla_q_norm, m_mla_kv_norm, m_mla_w_uq, m_mla_w_ukv, m_mla_w_o, m_fox_w_qkvf, m_fox_b_f, m_fox_w_o, m_ffn_w_gate, m_ffn_w_up, m_ffn_w_down, m_final_norm, v_meta, v_norm_mix, v_norm_ffn, v_pool_w, v_pool_scale, v_sb_w_qkv, v_sb_w_o, v_mla_w_down, v_mla_q_norm, v_mla_kv_norm, v_mla_w_uq, v_mla_w_ukv, v_mla_w_o, v_fox_w_qkvf, v_fox_b_f, v_fox_w_o, v_ffn_w_gate, v_ffn_w_up, v_ffn_w_down, v_final_norm):
    w = dict(meta=meta, norm_mix=norm_mix, norm_ffn=norm_ffn, pool_w=pool_w, pool_scale=pool_scale,
             sb_w_qkv=sb_w_qkv, sb_w_o=sb_w_o, mla_w_down=mla_w_down, mla_q_norm=mla_q_norm,
             mla_kv_norm=mla_kv_norm, mla_w_uq=mla_w_uq, mla_w_ukv=mla_w_ukv, mla_w_o=mla_w_o,
             fox_w_qkvf=fox_w_qkvf, fox_b_f=fox_b_f, fox_w_o=fox_w_o, ffn_w_gate=ffn_w_gate, ffn_w_up=ffn_w_up,
             ffn_w_down=ffn_w_down, final_norm=final_norm)
    m = dict(meta=m_meta, norm_mix=m_norm_mix, norm_ffn=m_norm_ffn, pool_w=m_pool_w, pool_scale=m_pool_scale,
             sb_w_qkv=m_sb_w_qkv, sb_w_o=m_sb_w_o, mla_w_down=m_mla_w_down, mla_q_norm=m_mla_q_norm,
             mla_kv_norm=m_mla_kv_norm, mla_w_uq=m_mla_w_uq, mla_w_ukv=m_mla_w_ukv, mla_w_o=m_mla_w_o,
             fox_w_qkvf=m_fox_w_qkvf, fox_b_f=m_fox_b_f, fox_w_o=m_fox_w_o, ffn_w_gate=m_ffn_w_gate,
             ffn_w_up=m_ffn_w_up, ffn_w_down=m_ffn_w_down, final_norm=m_final_norm)
    v = dict(meta=v_meta, norm_mix=v_norm_mix, norm_ffn=v_norm_ffn, pool_w=v_pool_w, pool_scale=v_pool_scale,
             sb_w_qkv=v_sb_w_qkv, sb_w_o=v_sb_w_o, mla_w_down=v_mla_w_down, mla_q_norm=v_mla_q_norm,
             mla_kv_norm=v_mla_kv_norm, mla_w_uq=v_mla_w_uq, mla_w_ukv=v_mla_w_ukv, mla_w_o=v_mla_w_o,
             fox_w_qkvf=v_fox_w_qkvf, fox_b_f=v_fox_b_f, fox_w_o=v_fox_w_o, ffn_w_gate=v_ffn_w_gate,
             ffn_w_up=v_ffn_w_up, ffn_w_down=v_ffn_w_down, final_norm=v_final_norm)

    sh_names = tuple(n for n, _ in SHARDED)
    sh_axis = dict(SHARDED)
    shapes = {n: w[n].shape for n in WEIGHT_NAMES}
    wire = lambda n: F32 if n in KEPT_F32 else BF16

    gathered = _all_gather([w[n].astype(wire(n)) for n in sh_names], "gather_weights")
    full = {n: _whole_from_gathered(g, sh_axis[n]) for n, g in zip(sh_names, gathered)}
    W = _kernel_weights(full)
    P = dict(meta=full["meta"], mla_q_norm=full["mla_q_norm"], mla_kv_norm=full["mla_kv_norm"],
             norm_mix=norm_mix, norm_ffn=norm_ffn, pool_scale=pool_scale, fox_b_f=fox_b_f, final_norm=final_norm)

    sq, dx, G = _local_step(x[0], loss_target[0], W, P)
    loss = lax.psum(0.5 * jnp.sum(sq) / D_MODEL, ("x", "y", "c"))
    grad_x = dx[None]

    gw = _reference_grads(G)
    gw["meta"] = G["meta"]
    rc = {n: (int(np.prod(shapes[n][:-1])), shapes[n][-1]) for n in sh_names}
    parts = [_parts_from_whole(gw[n], sh_axis[n]).astype(wire(n)).reshape((N_DEV,) + rc[n]) for n in sh_names]
    from_sibling = _exchange_siblings(parts, "exchange_grads_d2d")
    sums = [_pair_sum(a, b, f"pair_sum_{n}") for n, a, b in zip(sh_names, parts, from_sibling)]
    landed = _exchange_chips(sums, "exchange_grads_ici")
    results = {}
    for n, got in zip(sh_names, landed):
        outs = _adamw(w[n].reshape(rc[n]), got, m[n].reshape(rc[n]), v[n].reshape(rc[n]), f"adamw_{n}")
        results[n] = [o.reshape(shapes[n]) for o in outs]

    rep_g = dict(norm_mix=G["norm_mix"], norm_ffn=G["norm_ffn"], pool_scale=G["pool_scale"], fox_b_f=G["fox_b_f"],
                 final_norm=G["final_norm"])
    (rep_all,) = _all_gather([_pack_rows(rep_g, REPLICATED)], "gather_replicated_grads")
    rep_out = _adamw(_pack_rows(w, REPLICATED), rep_all, _pack_rows(m, REPLICATED), _pack_rows(v, REPLICATED),
                     "adamw_replicated")
    rep = [_unpack_rows(o, shapes, REPLICATED) for o in rep_out]
    for n in REPLICATED:
        results[n] = [r[n] for r in rep]

    outs = [results[n][k] for k in range(4) for n in WEIGHT_NAMES]
    return (loss, grad_x, *outs)
```

```python
import numpy as np
import jax
import jax.numpy as jnp
from jax import lax
from jax.experimental import pallas as pl
from jax.experimental.pallas import tpu as pltpu

F32 = jnp.float32
BF16 = jnp.bfloat16

N_DEV = 8
D_MODEL = 1024
N_META = 16
PAD = 240
ROW0 = PAD + N_META
EPS = 1e-6
POOL_WINDOWS = (2, 4, 8, 16)
POOL_GROUP = 256
HALO = 128
N_HEADS = 16
HEAD_DIM = 64
N_PAIRS = N_HEADS // 2
MLA_Q_RANK = 384
MLA_KV_RANK = 256
MLA_NOPE = 64
MLA_ROPE = 32
ROPE_THETA = 10000.0
D_FF = 2816
DEPTH = 4
ATTN_TILE = 768
ATTN_BWD_TILE = 768
WALK_TILE = 256
NEG = -1e30
LOG2E = 1.4426950408889634
EXP_ZERO = -110.0
VMEM_LIMIT = 56 * 2**20
ADAM_TILE_ELEMS = 192 * 1024

ADAM_LR = 0.001
ADAM_B1 = 0.9
ADAM_B2 = 0.999
ADAM_EPS = 1e-08
ADAM_WD = 0.01
ADAM_STEP = 10

MESH = pl.DeviceIdType.MESH


def _params(sem=None):
    return pltpu.CompilerParams(dimension_semantics=sem, vmem_limit_bytes=VMEM_LIMIT)


def _pick(n, cands):
    for c in cands:
        if n % c == 0:
            return c
    return n


def _col_tile(n, cap=1536):
    best = None
    for t in range(128, min(n, cap) + 1, 128):
        if n % t == 0:
            best = t
    return best if best is not None else n


def _dot(a, b):
    return jnp.dot(a, b, preferred_element_type=F32)


def _dot_nt(a, b):
    return lax.dot_general(a, b, (((1,), (1,)), ((), ())), preferred_element_type=F32)


def _dot_tn(a, b):
    return lax.dot_general(a, b, (((0,), (0,)), ((), ())), preferred_element_type=F32)


def _mm_nn(a, b, out_dtype, name, res=None):
    M, K = a.shape
    N = b.shape[1]
    tm = _pick(M, (768, 512, 256, 128))
    tn = _col_tile(N)

    def body(*refs):
        if res is None:
            a_ref, b_ref, o_ref = refs
        else:
            a_ref, b_ref, r_ref, o_ref = refs
        acc = _dot(a_ref[...].astype(BF16), b_ref[...])
        if res is not None:
            acc = acc + r_ref[...]
        o_ref[...] = acc.astype(o_ref.dtype)

    in_specs = [pl.BlockSpec((tm, K), lambda n, m: (m, 0)), pl.BlockSpec((K, tn), lambda n, m: (0, n))]
    args = [a, b]
    if res is not None:
        in_specs.append(pl.BlockSpec((tm, tn), lambda n, m: (m, n)))
        args.append(res)
    return pl.pallas_call(
        body, name=name, grid=(N // tn, M // tm), in_specs=in_specs,
        out_specs=pl.BlockSpec((tm, tn), lambda n, m: (m, n)),
        out_shape=jax.ShapeDtypeStruct((M, N), out_dtype),
        compiler_params=_params(("parallel", "parallel")))(*args)


def _mm_nt(a, w, out_dtype, name):
    M, N = a.shape
    K = w.shape[0]
    tm = _pick(M, (768, 512, 256, 128)) if N <= 3200 else _pick(M, (256, 128))
    tk = _col_tile(K, 1024)

    def body(a_ref, w_ref, o_ref):
        o_ref[...] = _dot_nt(a_ref[...].astype(BF16), w_ref[...]).astype(o_ref.dtype)

    return pl.pallas_call(
        body, name=name, grid=(K // tk, M // tm),
        in_specs=[pl.BlockSpec((tm, N), lambda k, m: (m, 0)), pl.BlockSpec((tk, N), lambda k, m: (k, 0))],
        out_specs=pl.BlockSpec((tm, tk), lambda k, m: (m, k)),
        out_shape=jax.ShapeDtypeStruct((M, K), out_dtype),
        compiler_params=_params(("parallel", "parallel")))(a, w)


def _mm_nt2(a1, w1, a2, w2, out_dtype, name):
    M, N = a1.shape
    K = w1.shape[0]
    tm = _pick(M, (384, 256, 128))

    def body(a1_ref, w1_ref, a2_ref, w2_ref, o_ref):
        o_ref[...] = (_dot_nt(a1_ref[...], w1_ref[...]) + _dot_nt(a2_ref[...], w2_ref[...])).astype(o_ref.dtype)

    a_spec = pl.BlockSpec((tm, N), lambda m: (m, 0))
    w_spec = pl.BlockSpec((K, N), lambda m: (0, 0))
    return pl.pallas_call(
        body, name=name, grid=(M // tm,), in_specs=[a_spec, w_spec, a_spec, w_spec],
        out_specs=pl.BlockSpec((tm, K), lambda m: (m, 0)),
        out_shape=jax.ShapeDtypeStruct((M, K), out_dtype),
        compiler_params=_params(("parallel",)))(a1, w1, a2, w2)


def _mm_tn(a, b, name):
    M, K = a.shape
    N = b.shape[1]
    tm = _pick(M, (768, 512, 256, 128))
    tk = _col_tile(K, 1408)
    tn = _col_tile(N, 1408)

    def body(a_ref, b_ref, o_ref):
        @pl.when(pl.program_id(2) == 0)
        def _():
            o_ref[...] = jnp.zeros_like(o_ref)
        o_ref[...] += _dot_tn(a_ref[...].astype(BF16), b_ref[...].astype(BF16))

    return pl.pallas_call(
        body, name=name, grid=(K // tk, N // tn, M // tm),
        in_specs=[pl.BlockSpec((tm, tk), lambda k, n, m: (m, k)), pl.BlockSpec((tm, tn), lambda k, n, m: (m, n))],
        out_specs=pl.BlockSpec((tk, tn), lambda k, n, m: (k, n)),
        out_shape=jax.ShapeDtypeStruct((K, N), F32),
        compiler_params=_params(("parallel", "parallel", "arbitrary")))(a, b)


def _norm_fwd(h, gain, out_dtype, name):
    M, C = h.shape
    tm = _pick(M, (768, 512, 256, 128))

    def body(h_ref, g_ref, a_ref):
        x = h_ref[...]
        r = lax.rsqrt(jnp.mean(x * x, axis=-1, keepdims=True) + EPS)
        a_ref[...] = ((x * r) * g_ref[...]).astype(a_ref.dtype)

    return pl.pallas_call(
        body, name=name, grid=(M // tm,),
        in_specs=[pl.BlockSpec((tm, C), lambda m: (m, 0)), pl.BlockSpec((1, C), lambda m: (0, 0))],
        out_specs=pl.BlockSpec((tm, C), lambda m: (m, 0)),
        out_shape=jax.ShapeDtypeStruct((M, C), out_dtype),
        compiler_params=_params(("parallel",)))(h, gain)


def _norm_bwd(h, gain, da, dres, name, token_rows=False):
    M, C = h.shape
    tm = ROW0 if token_rows else _pick(M, (768, 512, 256, 128))

    def body(*refs):
        refs = list(refs)
        dx_ref = refs.pop() if token_rows else None
        if dres is None:
            h_ref, g_ref, da_ref, dh_ref, dg_ref = refs
        else:
            h_ref, g_ref, da_ref, dr_ref, dh_ref, dg_ref = refs
        x = h_ref[...]
        r = lax.rsqrt(jnp.mean(x * x, axis=-1, keepdims=True) + EPS)
        y = x * r
        dav = da_ref[...].astype(F32)
        dy = dav * g_ref[...]
        dh = r * (dy - y * jnp.mean(dy * y, axis=-1, keepdims=True))
        if dres is not None:
            dh = dh + dr_ref[...]
        dh_ref[...] = dh
        if token_rows:
            dx_ref[...] = dh

        @pl.when(pl.program_id(0) == 0)
        def _():
            dg_ref[...] = jnp.zeros_like(dg_ref)
        dg_ref[...] += jnp.sum(dav * y, axis=0, keepdims=True)

    row = pl.BlockSpec((tm, C), lambda m: (m, 0))
    vec = pl.BlockSpec((1, C), lambda m: (0, 0))
    in_specs = [row, vec, row] + ([row] if dres is not None else [])
    args = [h, gain, da] + ([dres] if dres is not None else [])
    out_specs = [row, vec]
    out_shape = [jax.ShapeDtypeStruct((M, C), F32), jax.ShapeDtypeStruct((1, C), F32)]
    if token_rows:
        out_specs.append(pl.BlockSpec((tm, C), lambda m: (jnp.maximum(m - 1, 0), 0)))
        out_shape.append(jax.ShapeDtypeStruct((M - ROW0, C), F32))
    return pl.pallas_call(
        body, name=name, grid=(M // tm,), in_specs=in_specs, out_specs=out_specs, out_shape=out_shape,
        compiler_params=_params(("arbitrary",)))(*args)


def _ffn_up(b, w_g, w_u, name):
    M, K = b.shape
    F = w_g.shape[1]
    tm = _pick(M, (768, 512, 256, 128))
    tn = _col_tile(F, 1408)
    nb = F // tn

    def body(b_ref, wg_ref, wu_ref, g_ref, u_ref, act_ref):
        x = b_ref[...]
        g = _dot(x, wg_ref[...])
        u = _dot(x, wu_ref[...])
        g_ref[...] = g
        u_ref[...] = u
        act_ref[...] = ((g * jax.nn.sigmoid(g)) * u).astype(act_ref.dtype)

    blk = pl.BlockSpec((tm, tn), lambda n, m: (m, n))
    return pl.pallas_call(
        body, name=name, grid=(nb, M // tm),
        in_specs=[pl.BlockSpec((tm, K), lambda n, m: (m, 0)),
                  pl.BlockSpec((K, tn), lambda n, m: (0, n)),
                  pl.BlockSpec((K, tn), lambda n, m: (0, n))],
        out_specs=[blk, blk, blk],
        out_shape=[jax.ShapeDtypeStruct((M, F), F32), jax.ShapeDtypeStruct((M, F), F32),
                   jax.ShapeDtypeStruct((M, F), BF16)],
        compiler_params=_params(("parallel", "parallel")))(b, w_g, w_u)


def _ffn_dact(dy, w_d, g, u, name):
    M, K = dy.shape
    F = w_d.shape[0]
    tm = _pick(M, (768, 512, 256, 128))
    tn = _col_tile(F, 1408)
    nb = F // tn

    def body(dy_ref, wd_ref, g_ref, u_ref, dg_ref, du_ref):
        dact = _dot_nt(dy_ref[...].astype(BF16), wd_ref[...])
        gv = g_ref[...]
        s = jax.nn.sigmoid(gv)
        silu = gv * s
        dg_ref[...] = (dact * u_ref[...] * (s * (1.0 + gv * (1.0 - s)))).astype(dg_ref.dtype)
        du_ref[...] = (dact * silu).astype(du_ref.dtype)

    blk = pl.BlockSpec((tm, tn), lambda n, m: (m, n))
    return pl.pallas_call(
        body, name=name, grid=(nb, M // tm),
        in_specs=[pl.BlockSpec((tm, K), lambda n, m: (m, 0)), pl.BlockSpec((tn, K), lambda n, m: (n, 0)), blk, blk],
        out_specs=[blk, blk],
        out_shape=[jax.ShapeDtypeStruct((M, F), BF16), jax.ShapeDtypeStruct((M, F), BF16)],
        compiler_params=_params(("parallel", "parallel")))(dy, w_d, g, u)


def _loss_head(h, gain, target, name):
    M, C = h.shape
    tm = ROW0
    assert M % tm == 0 and target.shape[0] == M - ROW0

    def body(h_ref, g_ref, t_ref, sq_ref, dh_ref, dg_ref):
        i = pl.program_id(0)

        @pl.when(i == 0)
        def _():
            sq_ref[...] = jnp.zeros_like(sq_ref)
            dg_ref[...] = jnp.zeros_like(dg_ref)
            dh_ref[...] = jnp.zeros_like(dh_ref)

        @pl.when(i > 0)
        def _():
            x = h_ref[...]
            r = lax.rsqrt(jnp.mean(x * x, axis=-1, keepdims=True) + EPS)
            y = x * r
            err = y * g_ref[...] - t_ref[...]
            sq_ref[...] += jnp.sum(err * err, axis=0, keepdims=True)
            da = err * (1.0 / C)
            dy = da * g_ref[...]
            dh_ref[...] = r * (dy - y * jnp.mean(dy * y, axis=-1, keepdims=True))
            dg_ref[...] += jnp.sum(da * y, axis=0, keepdims=True)

    row = pl.BlockSpec((tm, C), lambda m: (m, 0))
    vec = pl.BlockSpec((1, C), lambda m: (0, 0))
    return pl.pallas_call(
        body, name=name, grid=(M // tm,),
        in_specs=[row, vec, pl.BlockSpec((tm, C), lambda m: (jnp.maximum(m - 1, 0), 0))],
        out_specs=[vec, row, vec],
        out_shape=[jax.ShapeDtypeStruct((1, C), F32), jax.ShapeDtypeStruct((M, C), F32),
                   jax.ShapeDtypeStruct((1, C), F32)],
        compiler_params=_params(("arbitrary",)))(h, gain, target)


def _band_dot(band, x):
    hi = x.astype(BF16)
    rest = x - hi.astype(F32)
    mid = rest.astype(BF16)
    lo = (rest - mid.astype(F32)).astype(BF16)
    return _dot(band, hi) + _dot(band, mid) + _dot(band, lo)


def _pool_pos(row0, tm):
    return row0 + lax.broadcasted_iota(jnp.int32, (tm, 1), 0) - PAD


def _pool_fwd(h, a, w, scale, name):
    M, C = a.shape
    tm = 256
    hb = tm // HALO

    def body(h_ref, a_ref, halo_ref, w_ref, s_ref, o_ref, p_ref):
        i = pl.program_id(0)
        row0 = i * tm
        ext = jnp.concatenate([halo_ref[...], a_ref[...]], axis=0)
        src = row0 - HALO + lax.broadcasted_iota(jnp.int32, (tm + HALO, 1), 0)
        ext = jnp.where(src >= PAD, ext, 0.0)
        r = lax.broadcasted_iota(jnp.int32, (tm, tm + HALO), 0)
        c = lax.broadcasted_iota(jnp.int32, (tm, tm + HALO), 1)
        pos = _pool_pos(row0, tm)
        for g, win in enumerate(POOL_WINDOWS):
            band = ((c <= r + HALO) & (c > r + HALO - win)).astype(BF16)
            cols = slice(g * POOL_GROUP, (g + 1) * POOL_GROUP)
            xg = ext[:, cols]
            tot = _band_dot(band, xg)
            cnt = jnp.clip(pos + 1, 1, win).astype(F32)
            pooled = (tot / cnt - xg[HALO:]).astype(BF16)
            p_ref[:, cols] = pooled
            mixed = _dot(pooled, w_ref[g])
            o_ref[:, cols] = h_ref[:, cols] + mixed * s_ref[:, cols]

    row = pl.BlockSpec((tm, C), lambda m: (m, 0))
    return pl.pallas_call(
        body, name=name, grid=(M // tm,),
        in_specs=[row, row, pl.BlockSpec((HALO, C), lambda m: (jnp.maximum(m * hb - 1, 0), 0)),
                  pl.BlockSpec((4, POOL_GROUP, POOL_GROUP), lambda m: (0, 0, 0)),
                  pl.BlockSpec((1, C), lambda m: (0, 0))],
        out_specs=[row, row],
        out_shape=[jax.ShapeDtypeStruct((M, C), F32), jax.ShapeDtypeStruct((M, C), BF16)],
        compiler_params=_params(("parallel",)))(h, a, a, w, scale)


def _pool_bwd_mix(dout, pooled, w, scale, name):
    M, C = dout.shape
    tm = 256

    def body(do_ref, p_ref, w_ref, s_ref, dpc_ref, dw_ref, ds_ref):
        i = pl.program_id(0)

        @pl.when(i == 0)
        def _():
            dw_ref[...] = jnp.zeros_like(dw_ref)
            ds_ref[...] = jnp.zeros_like(ds_ref)

        pos = _pool_pos(i * tm, tm)
        for g, win in enumerate(POOL_WINDOWS):
            cols = slice(g * POOL_GROUP, (g + 1) * POOL_GROUP)
            do = do_ref[:, cols]
            pooled = p_ref[:, cols]
            mixed = _dot(pooled, w_ref[g])
            ds_ref[:, cols] += jnp.sum(do * mixed, axis=0, keepdims=True)
            dmix = (do * s_ref[:, cols]).astype(BF16)
            dw_ref[g] += _dot_tn(pooled, dmix)
            dp = _dot_nt(dmix, w_ref[g])
            cnt = jnp.clip(pos + 1, 1, win).astype(F32)
            dpc_ref[:, cols] = dp / cnt

    row = pl.BlockSpec((tm, C), lambda m: (m, 0))
    wspec = pl.BlockSpec((4, POOL_GROUP, POOL_GROUP), lambda m: (0, 0, 0))
    vec = pl.BlockSpec((1, C), lambda m: (0, 0))
    return pl.pallas_call(
        body, name=name, grid=(M // tm,),
        in_specs=[row, row, wspec, vec], out_specs=[row, wspec, vec],
        out_shape=[jax.ShapeDtypeStruct((M, C), F32), jax.ShapeDtypeStruct((4, POOL_GROUP, POOL_GROUP), F32),
                   jax.ShapeDtypeStruct((1, C), F32)],
        compiler_params=_params(("arbitrary",)))(dout, pooled, w, scale)


def _pool_bwd_window(dpc, name):
    M, C = dpc.shape
    tm = 256
    hb = tm // HALO
    last = M // HALO - 1

    def body(d_ref, halo_ref, da_ref):
        i = pl.program_id(0)
        row0 = i * tm
        ext = jnp.concatenate([d_ref[...], halo_ref[...]], axis=0)
        src = row0 + lax.broadcasted_iota(jnp.int32, (tm + HALO, 1), 0)
        ext = jnp.where(src < M, ext, 0.0)
        r = lax.broadcasted_iota(jnp.int32, (tm, tm + HALO), 0)
        c = lax.broadcasted_iota(jnp.int32, (tm, tm + HALO), 1)
        pos = _pool_pos(row0, tm)
        for g, win in enumerate(POOL_WINDOWS):
            band = ((c >= r) & (c < r + win)).astype(BF16)
            cols = slice(g * POOL_GROUP, (g + 1) * POOL_GROUP)
            xg = ext[:, cols]
            tot = _band_dot(band, xg)
            cnt = jnp.clip(pos + 1, 1, win).astype(F32)
            da_ref[:, cols] = jnp.where(pos >= 0, tot - xg[:tm] * cnt, 0.0)

    row = pl.BlockSpec((tm, C), lambda m: (m, 0))
    return pl.pallas_call(
        body, name=name, grid=(M // tm,),
        in_specs=[row, pl.BlockSpec((HALO, C), lambda m: (jnp.minimum((m + 1) * hb, last), 0))],
        out_specs=row, out_shape=jax.ShapeDtypeStruct((M, C), F32),
        compiler_params=_params(("parallel",)))(dpc, dpc)


def _head_masks():
    lane = lax.broadcasted_iota(jnp.int32, (1, 128), 1)
    return lane < HEAD_DIM, lane


def _split_heads(x, first):
    z = jnp.zeros_like(x)
    return jnp.where(first, x, z), jnp.where(first, z, x)


def _split_rope(x, lane):
    z = jnp.zeros_like(x)
    return jnp.where(lane < MLA_ROPE, x, z), jnp.where((lane >= MLA_ROPE) & (lane < 2 * MLA_ROPE), x, z)


def _walk_causal(i, step):
    def mid(kb, carry):
        step(kb, False)
        return carry

    step(0, True)
    lax.fori_loop(1, i, mid, 0)

    @pl.when(i > 0)
    def _():
        step(i, True)


def _mla_fwd(q_all, kv_all, qr, kr, scale, name):
    M = q_all.shape[0]
    t = ATTN_TILE

    def body(q_ref, k_ref, v_ref, qr_ref, kr_ref, o_ref, lse_ref, m_s, l_s, acc_s):
        i = pl.program_id(1)
        first, lane = _head_masks()
        qs = _split_heads(q_ref[...], first)
        qrs = _split_rope(qr_ref[...], lane)
        qcat = tuple(jnp.concatenate([qs[hh], qrs[hh]], axis=1) for hh in range(2))
        m_s[...] = jnp.full_like(m_s, NEG)
        l_s[...] = jnp.zeros_like(l_s)
        acc_s[...] = jnp.zeros_like(acc_s)
        qpos = i * t + lax.broadcasted_iota(jnp.int32, (t, t), 0)
        kidx = lax.broadcasted_iota(jnp.int32, (t, t), 1)

        def step(kb, masked):
            k0 = pl.multiple_of(kb * t, t)
            kcat = jnp.concatenate([k_ref[pl.ds(k0, t), :], kr_ref[pl.ds(k0, t), :]], axis=1)
            vs = _split_heads(v_ref[pl.ds(k0, t), :], first)
            if masked:
                kpos = k0 + kidx
                valid = (kpos <= qpos) & (kpos >= PAD)
            pv = None
            alphas = []
            for hh in range(2):
                s = _dot_nt(qcat[hh], kcat)
                if masked:
                    s = jnp.where(valid, s, NEG)
                m_old = m_s[hh]
                m_new = jnp.maximum(m_old, jnp.max(s, axis=1, keepdims=True))
                p = jnp.exp2((s - m_new) * (scale * LOG2E))
                alpha = jnp.exp2((m_old - m_new) * (scale * LOG2E))
                l_s[hh] = alpha * l_s[hh] + jnp.sum(p, axis=1, keepdims=True)
                m_s[hh] = m_new
                d = _dot(p.astype(BF16), vs[hh])
                pv = d if pv is None else pv + d
                alphas.append(alpha)
            acc_s[...] = acc_s[...] * jnp.where(first, alphas[0], alphas[1]) + pv

        _walk_causal(i, step)
        o_ref[...] = (acc_s[...] * jnp.where(first, 1.0 / l_s[0], 1.0 / l_s[1])).astype(o_ref.dtype)
        lse_ref[:, 0:1] = m_s[0] * scale + jnp.log(l_s[0])
        lse_ref[:, 1:2] = m_s[1] * scale + jnp.log(l_s[1])

    blk = pl.BlockSpec((t, 128), lambda j, i: (i, j))
    return pl.pallas_call(
        body, name=name, grid=(N_PAIRS, M // t),
        in_specs=[blk, pl.BlockSpec((M, 128), lambda j, i: (0, j)), pl.BlockSpec((M, 128), lambda j, i: (0, N_PAIRS + j)),
                  blk, pl.BlockSpec((M, 128), lambda j, i: (0, 0))],
        out_specs=[blk, pl.BlockSpec((None, t, 2), lambda j, i: (j, i, 0))],
        out_shape=[jax.ShapeDtypeStruct((M, N_PAIRS * 128), BF16), jax.ShapeDtypeStruct((N_PAIRS, M, 2), F32)],
        scratch_shapes=[pltpu.VMEM((2, t, 1), F32), pltpu.VMEM((2, t, 1), F32), pltpu.VMEM((t, 128), F32)],
        compiler_params=_params(("parallel", "arbitrary")))(q_all, kv_all, kv_all, qr, kr)


def _mla_bwd(q_all, kv_all, qr, kr, o, do, lse, scale, name):
    M = q_all.shape[0]
    t = ATTN_BWD_TILE

    def body(q_ref, kv_hbm, qr_ref, kr_hbm, o_ref, do_ref, lse_ref,
             dq_ref, dk_hbm, dv_hbm, dqr_ref, dkr_hbm,
             k_ref, v_ref, kr_ref, dk_ref, dv_ref, dkr_ref, dq_s, lse_s, delta_s):
        j = pl.program_id(0)
        i = pl.program_id(1)
        first, lane = _head_masks()
        every = pl.ds(0, M)
        kcols = pl.ds(pl.multiple_of(j * 128, 128), 128)
        vcols = pl.ds(pl.multiple_of((N_PAIRS + j) * 128, 128), 128)

        @pl.when(i == 0)
        def _():
            pltpu.sync_copy(kv_hbm.at[every, kcols], k_ref)
            pltpu.sync_copy(kv_hbm.at[every, vcols], v_ref)
            pltpu.sync_copy(kr_hbm, kr_ref)
            dk_ref[...] = jnp.zeros_like(dk_ref)
            dv_ref[...] = jnp.zeros_like(dv_ref)
            dkr_ref[...] = jnp.zeros_like(dkr_ref)

        qs = _split_heads(q_ref[...], first)
        qrs = _split_rope(qr_ref[...], lane)
        qcat = tuple(jnp.concatenate([qs[hh], qrs[hh]], axis=1) for hh in range(2))
        dov = do_ref[...]
        dos = _split_heads(dov, first)
        prod = dov.astype(F32) * o_ref[...].astype(F32)
        deltas = (jnp.sum(jnp.where(first, prod, 0.0), axis=1, keepdims=True),
                  jnp.sum(jnp.where(first, 0.0, prod), axis=1, keepdims=True))
        for hh in range(2):
            lse_s[hh] = jnp.broadcast_to(lse_ref[:, hh:hh + 1], (t, t))
            delta_s[hh] = jnp.broadcast_to(deltas[hh], (t, t))
        dq_s[...] = jnp.zeros_like(dq_s)
        qpos = i * t + lax.broadcasted_iota(jnp.int32, (t, t), 0)
        kidx = lax.broadcasted_iota(jnp.int32, (t, t), 1)

        def step(kb, masked):
            k0 = pl.multiple_of(kb * t, t)
            rows = pl.ds(k0, t)
            k = k_ref[rows, :]
            v = v_ref[rows, :]
            kr = kr_ref[rows, :]
            kcat = jnp.concatenate([k, kr], axis=1)
            ks = _split_heads(k, first)
            krs = _split_rope(kr, lane)
            if masked:
                kpos = k0 + kidx
                valid = (kpos <= qpos) & (kpos >= PAD)
            dq = dk = dv = None
            for hh in range(2):
                s = _dot_nt(qcat[hh], kcat) * scale
                if masked:
                    s = jnp.where(valid, s, NEG)
                p = jnp.exp(s - lse_s[hh])
                ds = p * (_dot_nt(dos[hh], v) - delta_s[hh])
                dsb = (ds * scale).astype(BF16)
                a = _dot(dsb, jnp.concatenate([ks[hh], krs[hh]], axis=1))
                b = _dot_tn(dsb, qcat[hh])
                c = _dot_tn(p.astype(BF16), dos[hh])
                dq = a if dq is None else dq + a
                dk = b if dk is None else dk + b
                dv = c if dv is None else dv + c
            dq_s[...] += dq
            dk_ref[rows, :] += dk[:, :128]
            dkr_ref[rows, :] += dk[:, 128:]
            dv_ref[rows, :] += dv

        _walk_causal(i, step)
        dq_ref[...] = dq_s[:, :128].astype(dq_ref.dtype)
        dqr_ref[...] = dq_s[:, 128:].astype(dqr_ref.dtype)

        @pl.when(i == M // t - 1)
        def _():
            pltpu.sync_copy(dk_ref, dk_hbm.at[every, kcols])
            pltpu.sync_copy(dv_ref, dv_hbm.at[every, kcols])
            pltpu.sync_copy(dkr_ref, dkr_hbm.at[j])

    blk = pl.BlockSpec((t, 128), lambda j, i: (i, j))
    whole = pl.BlockSpec(memory_space=pl.ANY)
    wide = jax.ShapeDtypeStruct((M, N_PAIRS * 128), F32)
    slab = lambda dtype: pltpu.VMEM((M, 128), dtype)
    return pl.pallas_call(
        body, name=name, grid=(N_PAIRS, M // t),
        in_specs=[blk, whole, blk, whole, blk, blk, pl.BlockSpec((None, t, 2), lambda j, i: (j, i, 0))],
        out_specs=[blk, whole, whole, blk, whole],
        out_shape=[jax.ShapeDtypeStruct((M, N_PAIRS * 128), BF16), wide, wide,
                   jax.ShapeDtypeStruct((M, N_PAIRS * 128), BF16), jax.ShapeDtypeStruct((N_PAIRS, M, 128), F32)],
        scratch_shapes=[slab(BF16), slab(BF16), slab(BF16), slab(F32), slab(F32), slab(F32),
                        pltpu.VMEM((t, 256), F32), pltpu.VMEM((2, t, t), F32), pltpu.VMEM((2, t, t), F32)],
        compiler_params=_params(("arbitrary", "arbitrary")))(q_all, kv_all, qr, kr, o, do, lse)


def _tri(t, rel):
    j = lax.broadcasted_iota(jnp.int32, (t, t), 0)
    k = lax.broadcasted_iota(jnp.int32, (t, t), 1)
    m = {"gt": j > k, "le": j <= k, "lt": j < k}[rel]
    return m.astype(BF16)


def _lane_cumsum(x, tri):
    hi = x.astype(BF16)
    lo = (x - hi.astype(F32)).astype(BF16)
    return _dot(hi, tri) + _dot(lo, tri)


def _log_sigmoids(z):
    sp = jnp.log(1.0 + jnp.exp(-jnp.abs(z)))
    return jnp.minimum(z, 0.0) - sp, jnp.minimum(-z, 0.0) - sp


def _log_sigmoids_fast(z):
    lk = -(jnp.maximum(z, 0.0) + jnp.log(1.0 + jnp.exp(-jnp.abs(z))))
    return lk + z, lk


def _sb_fwd(qkv, scale, name):
    M = qkv.shape[0]
    t = WALK_TILE
    ck, cv = N_PAIRS, 2 * N_PAIRS

    def body(q_ref, k_ref, v_ref, o_ref, tot_ref, c_s, acc_s):
        i = pl.program_id(1)
        first, _ = _head_masks()
        qs = _split_heads(q_ref[...], first)
        c_s[...] = jnp.zeros_like(c_s)
        acc_s[...] = jnp.zeros_like(acc_s)
        tri = _tri(t, "gt")
        qpos = i * t + lax.broadcasted_iota(jnp.int32, (t, t), 0)
        kidx = lax.broadcasted_iota(jnp.int32, (t, t), 1)

        def step(kb, masked):
            k0 = pl.multiple_of(kb * t, t)
            k = k_ref[pl.ds(k0, t), :]
            vs = _split_heads(v_ref[pl.ds(k0, t), :], first)
            if masked:
                kpos = k0 + kidx
                valid = (kpos < qpos) & (kpos >= PAD)
            pv = None
            for hh in range(2):
                z = _dot_nt(qs[hh], k) * scale
                lb, lk = _log_sigmoids_fast(z)
                if masked:
                    lk = jnp.where(valid, lk, 0.0)
                a = jnp.exp(lb + (c_s[hh] + _lane_cumsum(lk, tri)))
                if masked:
                    a = jnp.where(valid, a, 0.0)
                c_s[hh] = c_s[hh] + jnp.sum(lk, axis=1, keepdims=True)
                d = _dot(a.astype(BF16), vs[hh])
                pv = d if pv is None else pv + d
            acc_s[...] += pv

        def keep_going():
            return jnp.max(jnp.maximum(c_s[0], c_s[1])) > EXP_ZERO

        def cond(carry):
            kb, go, _ = carry
            return (kb >= 1) & go

        def walk(carry):
            kb, _, n = carry
            step(kb, False)
            return kb - 1, keep_going(), n + 1

        step(i, True)
        _, go, n = lax.while_loop(cond, walk, (i - 1, keep_going(), jnp.int32(1)))
        first_too = go & (i > 0)

        @pl.when(first_too)
        def _():
            step(0, True)

        walked = n + first_too.astype(jnp.int32)
        o_ref[...] = acc_s[...].astype(o_ref.dtype)
        tot_ref[:, 0:1] = c_s[0]
        tot_ref[:, 1:2] = c_s[1]
        tot_ref[:, 2:3] = jnp.full((t, 1), walked.astype(F32))

    whole = lambda c0: pl.BlockSpec((M, 128), lambda j, i: (0, c0 + j))
    return pl.pallas_call(
        body, name=name, grid=(N_PAIRS, M // t),
        in_specs=[pl.BlockSpec((t, 128), lambda j, i: (i, j)), whole(ck), whole(cv)],
        out_specs=[pl.BlockSpec((t, 128), lambda j, i: (i, j)), pl.BlockSpec((None, t, 3), lambda j, i: (j, i, 0))],
        out_shape=[jax.ShapeDtypeStruct((M, N_PAIRS * 128), BF16), jax.ShapeDtypeStruct((N_PAIRS, M, 3), F32)],
        scratch_shapes=[pltpu.VMEM((2, t, 1), F32), pltpu.VMEM((t, 128), F32)],
        compiler_params=_params(("parallel", "arbitrary")))(qkv, qkv, qkv)


def _sb_bwd(qkv, do, tot, scale, name):
    M = qkv.shape[0]
    t = WALK_TILE
    ck, cv = N_PAIRS, 2 * N_PAIRS

    def body(q_ref, k_ref, v_ref, do_ref, tot_ref, dq_ref, dk_ref, dv_ref, pc_s, dc_s, dq_s):
        i = pl.program_id(1)
        first, _ = _head_masks()

        @pl.when(i == 0)
        def _():
            dk_ref[...] = jnp.zeros_like(dk_ref)
            dv_ref[...] = jnp.zeros_like(dv_ref)

        qs = _split_heads(q_ref[...], first)
        dos = _split_heads(do_ref[...], first)
        pc_s[...] = jnp.zeros_like(pc_s)
        dc_s[...] = jnp.zeros_like(dc_s)
        dq_s[...] = jnp.zeros_like(dq_s)
        tri_le = _tri(t, "le")
        tri_lt = _tri(t, "lt")
        qpos = i * t + lax.broadcasted_iota(jnp.int32, (t, t), 0)
        kidx = lax.broadcasted_iota(jnp.int32, (t, t), 1)

        def step(kb, masked):
            k0 = pl.multiple_of(kb * t, t)
            rows = pl.ds(k0, t)
            k = k_ref[rows, :]
            v = v_ref[rows, :]
            ks = _split_heads(k, first)
            if masked:
                kpos = k0 + kidx
                valid = (kpos < qpos) & (kpos >= PAD)
            dq = dk = dv = None
            for hh in range(2):
                z = _dot_nt(qs[hh], k) * scale
                lb, lk = _log_sigmoids_fast(z)
                if masked:
                    lk = jnp.where(valid, lk, 0.0)
                later = tot_ref[:, hh:hh + 1] - (pc_s[hh] + _lane_cumsum(lk, tri_le))
                a = jnp.exp(lb + later)
                if masked:
                    a = jnp.where(valid, a, 0.0)
                dl = a * _dot_nt(dos[hh], v)
                early = dc_s[hh] + _lane_cumsum(dl, tri_lt)
                sg = jnp.exp(lb)
                dz = (dl * (1.0 - sg) - early * sg) * scale
                if masked:
                    dz = jnp.where(valid, dz, 0.0)
                pc_s[hh] = pc_s[hh] + jnp.sum(lk, axis=1, keepdims=True)
                dc_s[hh] = dc_s[hh] + jnp.sum(dl, axis=1, keepdims=True)
                dzb = dz.astype(BF16)
                x = _dot(dzb, ks[hh])
                y = _dot_tn(dzb, qs[hh])
                w = _dot_tn(a.astype(BF16), dos[hh])
                dq = x if dq is None else dq + x
                dk = y if dk is None else dk + y
                dv = w if dv is None else dv + w
            dq_s[...] += dq
            dk_ref[rows, :] += dk
            dv_ref[rows, :] += dv

        first_walked = i + 1 - jnp.max(tot_ref[:, 2:3]).astype(jnp.int32)

        def mid(kb, carry):
            step(kb, False)
            return carry

        @pl.when((first_walked == 0) & (i > 0))
        def _():
            step(0, True)

        lax.fori_loop(jnp.maximum(first_walked, 1), i, mid, 0)
        step(i, True)
        dq_ref[...] = dq_s[...].astype(dq_ref.dtype)

    whole = lambda c0: pl.BlockSpec((M, 128), lambda j, i: (0, c0 + j))
    blk = pl.BlockSpec((t, 128), lambda j, i: (i, j))
    col = pl.BlockSpec((M, 128), lambda j, i: (0, j))
    return pl.pallas_call(
        body, name=name, grid=(N_PAIRS, M // t),
        in_specs=[blk, whole(ck), whole(cv), blk, pl.BlockSpec((None, t, 3), lambda j, i: (j, i, 0))],
        out_specs=[blk, col, col],
        out_shape=[jax.ShapeDtypeStruct((M, N_PAIRS * 128), BF16), jax.ShapeDtypeStruct((M, N_PAIRS * 128), F32),
                   jax.ShapeDtypeStruct((M, N_PAIRS * 128), F32)],
        scratch_shapes=[pltpu.VMEM((2, t, 1), F32), pltpu.VMEM((2, t, 1), F32), pltpu.VMEM((t, 128), F32)],
        compiler_params=_params(("parallel", "arbitrary")))(qkv, qkv, qkv, do, tot)


def _rows_between(lo, hi):
    r = lax.broadcasted_iota(jnp.int32, (128, 1), 0)
    return (r >= lo) & (r < hi)


def _lanes_between(lo, hi):
    c = lax.broadcasted_iota(jnp.int32, (1, 128), 1)
    return (c >= lo) & (c < hi)


def _keep(x, mask):
    return jnp.where(mask, x, jnp.zeros_like(x))


def _valid_mask(i, kb, t):
    kpos = kb * t + lax.broadcasted_iota(jnp.int32, (t, t), 0)
    qpos = i * t + lax.broadcasted_iota(jnp.int32, (t, t), 1)
    return (kpos <= qpos) & (kpos >= PAD)


def _fox_fwd(qkv, qkv_t, f_rows, f_cols, scale, name):
    M = qkv.shape[0]
    t = WALK_TILE
    first_blk = PAD // t
    ck, cv = N_PAIRS, 2 * N_PAIRS

    def body(qt_ref, k_ref, vt_ref, fq_ref, fk_ref, o_ref, lse_ref, ox_ref, m_s, l_s, acc_s, accx_s, kmax_s):
        i = pl.program_id(1)

        @pl.when(i == 0)
        def _():
            first = _lanes_between(0, 64)

            def block_max(kb, carry):
                kk = k_ref[pl.ds(pl.multiple_of(kb * t, t), t), :].astype(F32)
                kk = kk * kk
                a = jnp.max(jnp.sum(jnp.where(first, kk, 0.0), axis=1, keepdims=True))
                b = jnp.max(jnp.sum(jnp.where(first, 0.0, kk), axis=1, keepdims=True))
                return jnp.maximum(carry[0], a), jnp.maximum(carry[1], b)

            a, b = lax.fori_loop(0, M // t, block_max, (jnp.float32(0.0), jnp.float32(0.0)))
            kmax_s[0] = a
            kmax_s[1] = b

        qt = qt_ref[...]
        qts = (_keep(qt, _rows_between(0, 64)), _keep(qt, _rows_between(64, 128)))
        qf = qt.astype(F32)
        qf = qf * qf
        qbound = tuple(
            (1.001 * scale) * jnp.sqrt(jnp.sum(qf[HEAD_DIM * hh:HEAD_DIM * (hh + 1)], axis=0, keepdims=True) * kmax_s[hh])
            for hh in range(2))
        m_s[...] = jnp.full_like(m_s, NEG)
        l_s[...] = jnp.zeros_like(l_s)
        acc_s[...] = jnp.zeros_like(acc_s)
        accx_s[...] = jnp.zeros_like(accx_s)

        def step(kb, masked):
            k0 = pl.multiple_of(kb * t, t)
            rows = pl.ds(k0, t)
            k = k_ref[rows, :]
            if masked:
                valid = _valid_mask(i, kb, t)
            for hh in range(2):
                s = _dot(k, qts[hh]) * scale + (fq_ref[hh:hh + 1, :] - fk_ref[rows, hh:hh + 1])
                if masked:
                    s = jnp.where(valid, s, NEG)
                m_old = m_s[hh]
                m_new = jnp.maximum(m_old, jnp.max(s, axis=0, keepdims=True))
                p = jnp.exp(s - m_new)
                alpha = jnp.exp(m_old - m_new)
                l_s[hh] = alpha * l_s[hh] + jnp.sum(p, axis=0, keepdims=True)
                m_s[hh] = m_new
                pb = p.astype(BF16)
                hr = slice(HEAD_DIM * hh, HEAD_DIM * (hh + 1))
                vt = vt_ref[hr, rows]
                acc_s[hr, :] = acc_s[hr, :] * alpha + _dot(vt, pb)
                accx_s[hr, :] = accx_s[hr, :] * alpha + _dot(vt, (p - pb.astype(F32)).astype(BF16))

        def keep_going(kb):
            k0 = pl.multiple_of(kb * t, t)
            worst = None
            for hh in range(2):
                f0 = jnp.max(fk_ref[pl.ds(k0, 8), hh:hh + 1])
                w = jnp.max(qbound[hh] + (fq_ref[hh:hh + 1, :] - f0) - m_s[hh])
                worst = w if worst is None else jnp.maximum(worst, w)
            return worst > EXP_ZERO

        def cond(carry):
            kb, go, _ = carry
            return (kb > first_blk) & go

        def walk(carry):
            kb, _, n = carry
            step(kb, False)
            return kb - 1, keep_going(kb), n + 1

        step(i, True)
        _, go, n = lax.while_loop(cond, walk, (i - 1, keep_going(i), jnp.int32(1)))
        first_too = go & (i > first_blk)

        @pl.when(first_too)
        def _():
            step(first_blk, True)

        walked = n + first_too.astype(jnp.int32)
        for hh in range(2):
            hr = slice(HEAD_DIM * hh, HEAD_DIM * (hh + 1))
            inv = 1.0 / l_s[hh]
            o_ref[hr, :] = (acc_s[hr, :] * inv).astype(o_ref.dtype)
            ox_ref[hr, :] = (acc_s[hr, :] + accx_s[hr, :]) * inv
            lse_ref[hh:hh + 1, :] = m_s[hh] + jnp.log(l_s[hh])
        lse_ref[2:3, :] = jnp.full((1, t), walked.astype(F32))

    blk = pl.BlockSpec((128, t), lambda j, i: (j, i))
    stat = pl.BlockSpec((None, 2, t), lambda j, i: (j, 0, i))
    return pl.pallas_call(
        body, name=name, grid=(N_PAIRS, M // t),
        in_specs=[blk, pl.BlockSpec((M, 128), lambda j, i: (0, ck + j)), pl.BlockSpec((128, M), lambda j, i: (cv + j, 0)),
                  stat, pl.BlockSpec((None, M, 2), lambda j, i: (j, 0, 0))],
        out_specs=[blk, pl.BlockSpec((None, 3, t), lambda j, i: (j, 0, i)), blk],
        out_shape=[jax.ShapeDtypeStruct((N_PAIRS * 128, M), BF16), jax.ShapeDtypeStruct((N_PAIRS, 3, M), F32),
                   jax.ShapeDtypeStruct((N_PAIRS * 128, M), F32)],
        scratch_shapes=[pltpu.VMEM((2, 1, t), F32), pltpu.VMEM((2, 1, t), F32), pltpu.VMEM((128, t), F32),
                        pltpu.VMEM((128, t), F32), pltpu.SMEM((2,), F32)],
        compiler_params=_params(("parallel", "arbitrary")))(qkv_t, qkv, qkv_t, f_rows, f_cols)


def _fox_bwd(qkv, qkv_t, o_t, do, do_t, lse, f_rows, f_cols, scale, name):
    M = qkv.shape[0]
    t = WALK_TILE
    first_blk = PAD // t
    ck, cv = N_PAIRS, 2 * N_PAIRS

    def body(q_ref, qt_ref, k_ref, kt_ref, v_ref, ot_ref, do_ref, dot_ref, lse_ref, fq_ref, fk_ref,
             dq_ref, dk_ref, dv_ref, cs_ref, dq_s):
        i = pl.program_id(1)

        @pl.when(i == 0)
        def _():
            dk_ref[...] = jnp.zeros_like(dk_ref)
            dv_ref[...] = jnp.zeros_like(dv_ref)
            cs_ref[...] = jnp.zeros_like(cs_ref)

        heads_l = (_lanes_between(0, 64), _lanes_between(64, 128))
        heads_r = (_rows_between(0, 64), _rows_between(64, 128))
        q = q_ref[...]
        qt = qt_ref[...]
        do = do_ref[...]
        dot = dot_ref[...]
        qs = tuple(_keep(q, m) for m in heads_l)
        qts = tuple(_keep(qt, m) for m in heads_r)
        dos = tuple(_keep(do, m) for m in heads_l)
        dots = tuple(_keep(dot, m) for m in heads_r)
        prod = dot.astype(F32) * ot_ref[...]
        deltas = tuple(jnp.sum(prod[HEAD_DIM * hh:HEAD_DIM * (hh + 1)], axis=0, keepdims=True) for hh in range(2))
        ones = tuple(m.astype(BF16) * jnp.ones((t, 128), BF16) for m in heads_l)
        dq_s[...] = jnp.zeros_like(dq_s)

        def step(kb, masked):
            k0 = pl.multiple_of(kb * t, t)
            rows = pl.ds(k0, t)
            k = k_ref[rows, :]
            v = v_ref[rows, :]
            if masked:
                valid = _valid_mask(i, kb, t)
            dk = dv = cs = None
            for hh in range(2):
                s = _dot(k, qts[hh]) * scale + (fq_ref[hh:hh + 1, :] - fk_ref[rows, hh:hh + 1])
                if masked:
                    s = jnp.where(valid, s, NEG)
                p = jnp.exp(s - lse_ref[hh:hh + 1, :])
                ds = p * (_dot(v, dots[hh]) - deltas[hh])
                hi = ds.astype(BF16)
                lo = (ds - hi.astype(F32)).astype(BF16)
                c = _dot(hi, ones[hh]) + _dot(lo, ones[hh])
                dsb = (ds * scale).astype(BF16)
                hr = slice(HEAD_DIM * hh, HEAD_DIM * (hh + 1))
                dq_s[hr, :] += _dot(kt_ref[hr, rows], dsb)
                a = _dot(dsb, qs[hh])
                b = _dot(p.astype(BF16), dos[hh])
                dk = a if dk is None else dk + a
                dv = b if dv is None else dv + b
                cs = c if cs is None else cs + c
            dk_ref[rows, :] += dk
            dv_ref[rows, :] += dv
            cs_ref[rows, :] += cs

        first_walked = i + 1 - jnp.max(lse_ref[2:3, :]).astype(jnp.int32)

        def mid(kb, carry):
            step(kb, False)
            return carry

        @pl.when((first_walked == first_blk) & (i > first_blk))
        def _():
            step(first_blk, True)

        lax.fori_loop(jnp.maximum(first_walked, first_blk + 1), i, mid, 0)
        step(i, True)
        dq_ref[...] = dq_s[...].astype(dq_ref.dtype)

    rblk = pl.BlockSpec((t, 128), lambda j, i: (i, j))
    tblk = pl.BlockSpec((128, t), lambda j, i: (j, i))
    stat = pl.BlockSpec((None, 2, t), lambda j, i: (j, 0, i))
    stat3 = pl.BlockSpec((None, 3, t), lambda j, i: (j, 0, i))
    col = pl.BlockSpec((M, 128), lambda j, i: (0, j))
    wide = jax.ShapeDtypeStruct((M, N_PAIRS * 128), F32)
    return pl.pallas_call(
        body, name=name, grid=(N_PAIRS, M // t),
        in_specs=[rblk, tblk, pl.BlockSpec((M, 128), lambda j, i: (0, ck + j)),
                  pl.BlockSpec((128, M), lambda j, i: (ck + j, 0)), pl.BlockSpec((M, 128), lambda j, i: (0, cv + j)),
                  tblk, rblk, tblk, stat3, stat, pl.BlockSpec((None, M, 2), lambda j, i: (j, 0, 0))],
        out_specs=[tblk, col, col, pl.BlockSpec((None, M, 128), lambda j, i: (j, 0, 0))],
        out_shape=[jax.ShapeDtypeStruct((N_PAIRS * 128, M), BF16), wide, wide,
                   jax.ShapeDtypeStruct((N_PAIRS, M, 128), F32)],
        scratch_shapes=[pltpu.VMEM((128, t), F32)],
        compiler_params=_params(("parallel", "arbitrary")))(qkv, qkv_t, qkv, qkv_t, qkv, o_t, do, do_t, lse,
                                                            f_rows, f_cols)


def _rope_tables(M):
    pos = (jnp.arange(M, dtype=jnp.int32) - PAD).astype(F32)
    inv = ROPE_THETA ** (-jnp.arange(0, MLA_ROPE, 2, dtype=F32) / MLA_ROPE)
    ang = pos[:, None] * inv[None, :]
    cos, sin = jnp.cos(ang), jnp.sin(ang)
    z = jnp.zeros((M, 64), F32)
    cos_t = jnp.concatenate([cos, cos, cos, cos, z], axis=1)
    sin_t = jnp.concatenate([-sin, sin, -sin, sin, z], axis=1)
    return cos_t, sin_t


def _rope(x, cos_t, sin_t, out_dtype, name, inverse=False, lead=0):
    M, C = x.shape
    tm = _pick(M, (768, 512, 256, 128))
    nblk = (C - lead) // 128
    sign = -1.0 if inverse else 1.0

    def body(x_ref, c_ref, s_ref, o_ref):
        lane = lax.broadcasted_iota(jnp.int32, (1, 128), 1)
        low = (lane % MLA_ROPE) < (MLA_ROPE // 2)
        cos = c_ref[...]
        sin = s_ref[...] * sign
        if lead:
            o_ref[:, :lead] = x_ref[:, :lead].astype(o_ref.dtype)
        for b in range(nblk):
            cols = slice(lead + b * 128, lead + (b + 1) * 128)
            v = x_ref[:, cols].astype(F32)
            up = pltpu.roll(v, 128 - MLA_ROPE // 2, 1)
            down = pltpu.roll(v, MLA_ROPE // 2, 1)
            o_ref[:, cols] = (v * cos + jnp.where(low, up, down) * sin).astype(o_ref.dtype)

    row = pl.BlockSpec((tm, C), lambda m: (m, 0))
    tab = pl.BlockSpec((tm, 128), lambda m: (m, 0))
    return pl.pallas_call(
        body, name=name, grid=(M // tm,), in_specs=[row, tab, tab], out_specs=row,
        out_shape=jax.ShapeDtypeStruct((M, C), out_dtype),
        compiler_params=_params(("parallel",)))(x, cos_t, sin_t)


def _forget_cumsum(f_logit, bias, name):
    M = f_logit.shape[0]
    tm = 256

    def body(f_ref, b_ref, o_ref, c_s):
        i = pl.program_id(0)

        @pl.when(i == 0)
        def _():
            c_s[...] = jnp.zeros_like(c_s)
        ls, _ = _log_sigmoids(f_ref[...] + b_ref[...])
        rows = i * tm + lax.broadcasted_iota(jnp.int32, (tm, 1), 0)
        ls = jnp.where(rows >= PAD, ls, 0.0)
        r = lax.broadcasted_iota(jnp.int32, (tm, tm), 0)
        c = lax.broadcasted_iota(jnp.int32, (tm, tm), 1)
        tri = (c <= r).astype(F32)
        cum = jnp.dot(tri, ls, precision=lax.Precision.HIGHEST, preferred_element_type=F32) + c_s[...]
        o_ref[...] = cum
        c_s[...] = cum[tm - 1:tm, :]

    row = pl.BlockSpec((tm, 128), lambda m: (m, 0))
    return pl.pallas_call(
        body, name=name, grid=(M // tm,),
        in_specs=[row, pl.BlockSpec((1, 128), lambda m: (0, 0))], out_specs=row,
        out_shape=jax.ShapeDtypeStruct((M, 128), F32), scratch_shapes=[pltpu.VMEM((1, 128), F32)],
        compiler_params=_params(("arbitrary",)))(f_logit, bias)


def _forget_cumsum_bwd(f_logit, bias, colsum, name):
    M = f_logit.shape[0]
    tm = 256
    nb = M // tm

    def body(f_ref, b_ref, cs_ref, o_ref, db_ref, c_s):
        i = pl.program_id(0)

        @pl.when(i == 0)
        def _():
            c_s[...] = jnp.zeros_like(c_s)
            db_ref[...] = jnp.zeros_like(db_ref)
        rr = lax.broadcasted_iota(jnp.int32, (128, 128), 0)
        cc = lax.broadcasted_iota(jnp.int32, (128, 128), 1)
        dF = None
        for j in range(N_PAIRS):
            sel = (((rr == 0) & (cc == 2 * j)) | ((rr == HEAD_DIM) & (cc == 2 * j + 1))).astype(F32)
            d = jnp.dot(cs_ref[j], sel, precision=lax.Precision.HIGHEST, preferred_element_type=F32)
            dF = d if dF is None else dF + d
        r = lax.broadcasted_iota(jnp.int32, (tm, tm), 0)
        c = lax.broadcasted_iota(jnp.int32, (tm, tm), 1)
        tri = (c >= r).astype(F32)
        cum = c_s[...] - jnp.dot(tri, dF, precision=lax.Precision.HIGHEST, preferred_element_type=F32)
        c_s[...] = cum[0:1, :]
        _, lsn = _log_sigmoids(f_ref[...] + b_ref[...])
        rows = (nb - 1 - i) * tm + lax.broadcasted_iota(jnp.int32, (tm, 1), 0)
        dl = jnp.where(rows >= PAD, cum * jnp.exp(lsn), 0.0)
        o_ref[...] = dl
        db_ref[...] += jnp.sum(dl, axis=0, keepdims=True)

    row = pl.BlockSpec((tm, 128), lambda m: (nb - 1 - m, 0))
    vec = pl.BlockSpec((1, 128), lambda m: (0, 0))
    return pl.pallas_call(
        body, name=name, grid=(nb,),
        in_specs=[row, vec, pl.BlockSpec((N_PAIRS, tm, 128), lambda m: (0, nb - 1 - m, 0))], out_specs=[row, vec],
        out_shape=[jax.ShapeDtypeStruct((M, 128), F32), jax.ShapeDtypeStruct((1, 128), F32)],
        scratch_shapes=[pltpu.VMEM((1, 128), F32)],
        compiler_params=_params(("arbitrary",)))(f_logit, bias, colsum)


def _adamw(w, parts, m, v, name):
    R, C = w.shape
    n_parts = parts.shape[0]
    tr = R
    for d in range(8, R, 8):
        if R % d == 0 and d * C <= ADAM_TILE_ELEMS:
            tr = d
    c1 = 1.0 - ADAM_B1 ** ADAM_STEP
    c2 = 1.0 - ADAM_B2 ** ADAM_STEP

    def body(w_ref, s_ref, m_ref, v_ref, g_ref, d_ref, mo_ref, vo_ref):
        g = s_ref[0].astype(F32)
        for k in range(1, n_parts):
            g = g + s_ref[k].astype(F32)
        mn = ADAM_B1 * m_ref[...] + (1.0 - ADAM_B1) * g
        vn = ADAM_B2 * v_ref[...] + (1.0 - ADAM_B2) * (g * g)
        m_hat = mn / c1
        v_hat = vn / c2
        g_ref[...] = g
        d_ref[...] = -ADAM_LR * (m_hat / (jnp.sqrt(v_hat) + ADAM_EPS) + ADAM_WD * w_ref[...])
        mo_ref[...] = mn
        vo_ref[...] = vn

    row = pl.BlockSpec((tr, C), lambda r: (r, 0))
    shp = jax.ShapeDtypeStruct((R, C), F32)
    return pl.pallas_call(
        body, name=name, grid=(R // tr,),
        in_specs=[row, pl.BlockSpec((n_parts, tr, C), lambda r: (0, r, 0)), row, row],
        out_specs=[row, row, row, row], out_shape=[shp, shp, shp, shp],
        compiler_params=_params(("parallel",)))(w, parts, m, v)


def _position():
    return lax.axis_index("x"), lax.axis_index("y"), lax.axis_index("c")


def _all_gather(blocks, name):
    n = len(blocks)

    def body(*refs):
        x_refs, out_refs = refs[:n], refs[n:2 * n]
        send_sems, recv_sems, local_sems = refs[2 * n:]
        x, y, c = _position()
        me, sibling = (x, y, c), (x, y, 1 - c)
        chips = [(1 - x, y), (x, 1 - y), (1 - x, 1 - y)]

        def copies(k, block, to, own=False):
            slot = 4 * block[0] + 2 * block[1] + block[2]
            return [pltpu.make_async_remote_copy(
                src_ref=x_refs[p] if own else out_refs[p].at[slot], dst_ref=out_refs[p].at[slot],
                send_sem=send_sems.at[k, p], recv_sem=recv_sems.at[k, p], device_id=to, device_id_type=MESH)
                for p in range(n)]

        mine = [pltpu.make_async_copy(x_refs[p], out_refs[p].at[4 * x + 2 * y + c], local_sems.at[p]) for p in range(n)]
        for cp in mine:
            cp.start()
        first = copies(0, me, sibling, own=True)
        for j, chip in enumerate(chips):
            first += copies(1 + j, me, (*chip, c), own=True)
        for cp in first:
            cp.start()
        passed = []
        for j, chip in enumerate(chips):
            for cp in copies(1 + j, (*chip, c), me):
                cp.wait_recv()
            onward = copies(4 + j, (*chip, c), sibling)
            for cp in onward:
                cp.start()
            passed += onward
        for cp in copies(0, sibling, me):
            cp.wait_recv()
        for j, chip in enumerate(chips):
            for cp in copies(4 + j, (*chip, 1 - c), me):
                cp.wait_recv()
        for cp in first + passed:
            cp.wait_send()
        for cp in mine:
            cp.wait()

    any_spec = pl.BlockSpec(memory_space=pl.ANY)
    return pl.pallas_call(
        body, name=name, out_shape=[jax.ShapeDtypeStruct((N_DEV,) + b.shape, b.dtype) for b in blocks],
        in_specs=[any_spec] * n, out_specs=[any_spec] * n,
        scratch_shapes=[pltpu.SemaphoreType.DMA((7, n)), pltpu.SemaphoreType.DMA((7, n)), pltpu.SemaphoreType.DMA((n,))],
    )(*blocks)


N_CHIPS = 4


def _exchange_siblings(parts, name):
    n = len(parts)

    def body(*refs):
        g_refs, land_refs = refs[:n], refs[n:2 * n]
        send_sems, recv_sems = refs[2 * n:]
        x, y, c = _position()
        sibling = (x, y, 1 - c)
        sends, recvs = [], []
        for q in range(N_CHIPS):
            for p in range(n):
                sends.append(pltpu.make_async_remote_copy(
                    src_ref=g_refs[p].at[2 * q + (1 - c)], dst_ref=land_refs[p].at[q], send_sem=send_sems.at[q, p],
                    recv_sem=recv_sems.at[q, p], device_id=sibling, device_id_type=MESH))
                recvs.append(pltpu.make_async_remote_copy(
                    src_ref=g_refs[p].at[2 * q + c], dst_ref=land_refs[p].at[q], send_sem=send_sems.at[q, p],
                    recv_sem=recv_sems.at[q, p], device_id=sibling, device_id_type=MESH))
        for cp in sends:
            cp.start()
        for cp in recvs:
            cp.wait_recv()
        for cp in sends:
            cp.wait_send()

    any_spec = pl.BlockSpec(memory_space=pl.ANY)
    return pl.pallas_call(
        body, name=name, out_shape=[jax.ShapeDtypeStruct((N_CHIPS,) + p.shape[1:], p.dtype) for p in parts],
        in_specs=[any_spec] * n, out_specs=[any_spec] * n,
        scratch_shapes=[pltpu.SemaphoreType.DMA((N_CHIPS, n)), pltpu.SemaphoreType.DMA((N_CHIPS, n))],
    )(*parts)


def _pair_sum(part, from_sibling, name):
    _, R, C = part.shape
    tr = R
    for d in range(8, R, 8):
        if R % d == 0 and d * C <= ADAM_TILE_ELEMS:
            tr = d

    def body(a_ref, b_ref, o_ref):
        c = lax.axis_index("c")
        for q in range(N_CHIPS):
            o_ref[q] = (a_ref[2 * q + c].astype(F32) + b_ref[q].astype(F32)).astype(o_ref.dtype)

    return pl.pallas_call(
        body, name=name, grid=(R // tr,),
        in_specs=[pl.BlockSpec((N_DEV, tr, C), lambda r: (0, r, 0)), pl.BlockSpec((N_CHIPS, tr, C), lambda r: (0, r, 0))],
        out_specs=pl.BlockSpec((N_CHIPS, tr, C), lambda r: (0, r, 0)),
        out_shape=jax.ShapeDtypeStruct((N_CHIPS, R, C), part.dtype),
        compiler_params=_params(("parallel",)))(part, from_sibling)


def _exchange_chips(sums, name):
    n = len(sums)

    def body(*refs):
        g_refs, land_refs = refs[:n], refs[n:2 * n]
        send_sems, recv_sems, local_sems = refs[2 * n:]
        x, y, c = _position()
        me = 2 * x + y
        mine = [pltpu.make_async_copy(g_refs[p].at[me], land_refs[p].at[me], local_sems.at[p]) for p in range(n)]
        for cp in mine:
            cp.start()
        sends, recvs = [], []
        for k in range(1, N_CHIPS):
            px = 1 - x if k & 2 else x
            py = 1 - y if k & 1 else y
            peer = 2 * px + py
            for p in range(n):
                sends.append(pltpu.make_async_remote_copy(
                    src_ref=g_refs[p].at[peer], dst_ref=land_refs[p].at[me], send_sem=send_sems.at[k - 1, p],
                    recv_sem=recv_sems.at[k - 1, p], device_id=(px, py, c), device_id_type=MESH))
                recvs.append(pltpu.make_async_remote_copy(
                    src_ref=g_refs[p].at[me], dst_ref=land_refs[p].at[peer], send_sem=send_sems.at[k - 1, p],
                    recv_sem=recv_sems.at[k - 1, p], device_id=(px, py, c), device_id_type=MESH))
        for cp in sends:
            cp.start()
        for cp in recvs:
            cp.wait_recv()
        for cp in sends:
            cp.wait_send()
        for cp in mine:
            cp.wait()

    any_spec = pl.BlockSpec(memory_space=pl.ANY)
    return pl.pallas_call(
        body, name=name, out_shape=[jax.ShapeDtypeStruct(p.shape, p.dtype) for p in sums],
        in_specs=[any_spec] * n, out_specs=[any_spec] * n,
        scratch_shapes=[pltpu.SemaphoreType.DMA((3, n)), pltpu.SemaphoreType.DMA((3, n)), pltpu.SemaphoreType.DMA((n,))],
    )(*sums)


SHARDED = (("sb_w_qkv", 2), ("sb_w_o", 1), ("mla_w_down", 1), ("mla_w_uq", 2), ("mla_w_ukv", 2), ("mla_w_o", 1),
           ("fox_w_qkvf", 2), ("fox_w_o", 1), ("ffn_w_gate", 2), ("ffn_w_up", 2), ("ffn_w_down", 1),
           ("pool_w", 2), ("meta", 1), ("mla_q_norm", 1), ("mla_kv_norm", 1))
KEPT_F32 = ("meta", "mla_q_norm", "mla_kv_norm")
REPLICATED = ("norm_mix", "norm_ffn", "pool_scale", "fox_b_f", "final_norm")
WEIGHT_NAMES = ("meta", "norm_mix", "norm_ffn", "pool_w", "pool_scale", "sb_w_qkv", "sb_w_o", "mla_w_down",
                "mla_q_norm", "mla_kv_norm", "mla_w_uq", "mla_w_ukv", "mla_w_o", "fox_w_qkvf", "fox_b_f",
                "fox_w_o", "ffn_w_gate", "ffn_w_up", "ffn_w_down", "final_norm")
LANES = 1024


def _pack_rows(arrays, names):
    parts = []
    for n in names:
        flat = arrays[n].reshape(-1).astype(F32)
        rows = -(-flat.shape[0] // LANES)
        parts.append(jnp.pad(flat, (0, rows * LANES - flat.shape[0])).reshape(rows, LANES))
    rows = sum(p.shape[0] for p in parts)
    parts.append(jnp.zeros((-(-rows // 8) * 8 - rows, LANES), F32))
    return jnp.concatenate(parts, axis=0)


def _unpack_rows(buf, shapes, names):
    out, row = {}, 0
    for n in names:
        size = int(np.prod(shapes[n]))
        rows = -(-size // LANES)
        out[n] = buf[row:row + rows].reshape(-1)[:size].reshape(shapes[n])
        row += rows
    return out


def _whole_from_gathered(g, axis):
    g = jnp.moveaxis(g, 0, axis)
    shp = g.shape
    return g.reshape(shp[:axis] + (shp[axis] * shp[axis + 1],) + shp[axis + 2:])


def _parts_from_whole(whole, axis):
    shp = whole.shape
    g = whole.reshape(shp[:axis] + (N_DEV, shp[axis] // N_DEV) + shp[axis + 1:])
    return jnp.moveaxis(g, axis, 0)


def _kernel_weights(full):
    W = {}
    W["pool_w"] = full["pool_w"][0]
    W["sb_w_qkv"] = full["sb_w_qkv"][0]
    W["sb_w_o"] = full["sb_w_o"][0]
    W["mla_w_down"] = full["mla_w_down"][0]
    uq = full["mla_w_uq"][0].reshape(MLA_Q_RANK, N_HEADS, MLA_NOPE + MLA_ROPE)
    nope = uq[:, :, :MLA_NOPE].reshape(MLA_Q_RANK, N_HEADS * MLA_NOPE)
    rope = uq[:, :, MLA_NOPE:].reshape(MLA_Q_RANK, N_PAIRS, 2 * MLA_ROPE)
    rope = jnp.pad(rope, ((0, 0), (0, 0), (0, 128 - 2 * MLA_ROPE))).reshape(MLA_Q_RANK, N_PAIRS * 128)
    W["mla_w_uq"] = jnp.concatenate([nope, rope], axis=1)
    ukv = full["mla_w_ukv"][0].reshape(MLA_KV_RANK, N_HEADS, 2, HEAD_DIM)
    W["mla_w_ukv"] = jnp.transpose(ukv, (0, 2, 1, 3)).reshape(MLA_KV_RANK, 2 * N_HEADS * HEAD_DIM)
    W["mla_w_o"] = full["mla_w_o"][0]
    qkvf = full["fox_w_qkvf"][0]
    n_qkv = 3 * N_HEADS * HEAD_DIM
    W["fox_w_qkv"] = qkvf[:, :n_qkv]
    W["fox_w_f"] = jnp.pad(qkvf[:, n_qkv:], ((0, 0), (0, 128 - N_HEADS)))
    W["fox_w_qkvf"] = jnp.concatenate([W["fox_w_qkv"], W["fox_w_f"]], axis=1)
    W["fox_w_o"] = full["fox_w_o"][0]
    W["ffn_w_gate"] = full["ffn_w_gate"]
    W["ffn_w_up"] = full["ffn_w_up"]
    W["ffn_w_down"] = full["ffn_w_down"]
    return W


def _reference_grads(G):
    out = {}
    out["pool_w"] = G["pool_w"][None]
    for n in ("sb_w_qkv", "sb_w_o", "mla_w_down", "mla_w_o", "fox_w_o"):
        out[n] = G[n][None]
    duq = G["mla_w_uq"]
    nope = duq[:, :N_HEADS * MLA_NOPE].reshape(MLA_Q_RANK, N_HEADS, MLA_NOPE)
    rope = duq[:, N_HEADS * MLA_NOPE:].reshape(MLA_Q_RANK, N_PAIRS, 128)[:, :, :2 * MLA_ROPE]
    rope = rope.reshape(MLA_Q_RANK, N_HEADS, MLA_ROPE)
    out["mla_w_uq"] = jnp.concatenate([nope, rope], axis=2).reshape(1, MLA_Q_RANK, -1)
    dukv = G["mla_w_ukv"].reshape(MLA_KV_RANK, 2, N_HEADS, HEAD_DIM)
    out["mla_w_ukv"] = jnp.transpose(dukv, (0, 2, 1, 3)).reshape(1, MLA_KV_RANK, -1)
    out["fox_w_qkvf"] = G["fox_w_qkvf"][None, :, :3 * N_HEADS * HEAD_DIM + N_HEADS]
    out["ffn_w_gate"] = G["ffn_w_gate"]
    out["ffn_w_up"] = G["ffn_w_up"]
    out["ffn_w_down"] = G["ffn_w_down"]
    out["mla_q_norm"] = G["mla_q_norm"]
    out["mla_kv_norm"] = G["mla_kv_norm"]
    return out


def _pairs_col(f16):
    M = f16.shape[0]
    return jnp.transpose(f16.reshape(M, N_PAIRS, 2), (1, 0, 2))


def _pairs_row(f16):
    M = f16.shape[0]
    return jnp.transpose(f16.reshape(M, N_PAIRS, 2), (1, 2, 0))


def _local_step(x, target, W, P):
    S = x.shape[0]
    M = S + ROW0
    G = {}
    gain = lambda name, i: P[name][i][None, :]
    h0 = jnp.concatenate([jnp.zeros((PAD, D_MODEL), F32), P["meta"], x], axis=0)

    def ffn_fwd(h1, i):
        b = _norm_fwd(h1, gain("norm_ffn", i), BF16, f"ffn{i}_norm")
        g, u, act = _ffn_up(b, W["ffn_w_gate"][i], W["ffn_w_up"][i], f"ffn{i}_up")
        h2 = _mm_nn(act, W["ffn_w_down"][i], F32, f"ffn{i}_down", res=h1)
        return h2, (h1, b, g, u, act)

    def ffn_bwd(dh2, saved, i):
        h1, b, g, u, act = saved
        dg, du = _ffn_dact(dh2, W["ffn_w_down"][i], g, u, f"ffn{i}_dact")
        G.setdefault("ffn_w_down", {})[i] = _mm_tn(act, dh2, f"ffn{i}_dwd")
        db = _mm_nt2(dg, W["ffn_w_gate"][i], du, W["ffn_w_up"][i], F32, f"ffn{i}_db")
        G.setdefault("ffn_w_gate", {})[i] = _mm_tn(b, dg, f"ffn{i}_dwg")
        G.setdefault("ffn_w_up", {})[i] = _mm_tn(b, du, f"ffn{i}_dwu")
        dh1, dgain = _norm_bwd(h1, gain("norm_ffn", i), db, dh2, f"ffn{i}_dnorm")
        G.setdefault("norm_ffn", {})[i] = dgain
        return dh1

    a0 = _norm_fwd(h0, gain("norm_mix", 0), F32, "mix0_norm")
    h1_0, pooled = _pool_fwd(h0, a0, W["pool_w"], P["pool_scale"], "pool_fwd")
    h_1, ffn0 = ffn_fwd(h1_0, 0)

    sb_scale = HEAD_DIM ** -0.5
    a1 = _norm_fwd(h_1, gain("norm_mix", 1), BF16, "mix1_norm")
    sb_qkv = _mm_nn(a1, W["sb_w_qkv"], BF16, "sb_qkv")
    sb_o, sb_tot = _sb_fwd(sb_qkv, sb_scale, "sb_fwd")
    h1_1 = _mm_nn(sb_o, W["sb_w_o"], F32, "sb_out", res=h_1)
    h_2, ffn1 = ffn_fwd(h1_1, 1)

    mla_scale = (MLA_NOPE + MLA_ROPE) ** -0.5
    cos_t, sin_t = _rope_tables(M)
    a2 = _norm_fwd(h_2, gain("norm_mix", 2), BF16, "mix2_norm")
    down = _mm_nn(a2, W["mla_w_down"], F32, "mla_down")
    dq_raw = down[:, :MLA_Q_RANK]
    dkv_raw = down[:, MLA_Q_RANK:MLA_Q_RANK + MLA_KV_RANK]
    kr_raw = down[:, MLA_Q_RANK + MLA_KV_RANK:]
    c_q = _norm_fwd(dq_raw, P["mla_q_norm"], BF16, "mla_qnorm")
    c_kv = _norm_fwd(dkv_raw, P["mla_kv_norm"], BF16, "mla_kvnorm")
    q_lin = _mm_nn(c_q, W["mla_w_uq"], F32, "mla_uq")
    q_all = _rope(q_lin, cos_t, sin_t, BF16, "mla_qrope", lead=D_MODEL)
    kv_all = _mm_nn(c_kv, W["mla_w_ukv"], BF16, "mla_ukv")
    kr_in = jnp.concatenate([kr_raw, kr_raw, jnp.zeros((M, 64), F32)], axis=1)
    kr = _rope(kr_in, cos_t, sin_t, BF16, "mla_krope")
    q_rope = q_all[:, D_MODEL:]
    mla_o, mla_lse = _mla_fwd(q_all, kv_all, q_rope, kr, mla_scale, "mla_fwd")
    h1_2 = _mm_nn(mla_o, W["mla_w_o"], F32, "mla_out", res=h_2)
    h_3, ffn2 = ffn_fwd(h1_2, 2)

    fox_scale = HEAD_DIM ** -0.5
    a3 = _norm_fwd(h_3, gain("norm_mix", 3), BF16, "mix3_norm")
    fox_qkv = _mm_nn(a3, W["fox_w_qkv"], BF16, "fox_qkv")
    f_logit = _mm_nn(a3, W["fox_w_f"], F32, "fox_f")
    b_f = jnp.pad(P["fox_b_f"], ((0, 0), (0, 128 - N_HEADS)))
    Fc = _forget_cumsum(f_logit, b_f, "fox_cumsum")
    f_rows, f_cols = _pairs_row(Fc[:, :N_HEADS]), _pairs_col(Fc[:, :N_HEADS])
    fox_qkv_t = fox_qkv.T
    fox_o_t, fox_lse, fox_ox_t = _fox_fwd(fox_qkv, fox_qkv_t, f_rows, f_cols, fox_scale, "fox_fwd")
    fox_o = fox_o_t.T
    h1_3 = _mm_nn(fox_o, W["fox_w_o"], F32, "fox_out", res=h_3)
    h_4, ffn3 = ffn_fwd(h1_3, 3)

    sq, dh, dgain = _loss_head(h_4, P["final_norm"][None, :], target, "loss_head")
    G["final_norm"] = dgain[0]

    dh = ffn_bwd(dh, ffn3, 3)
    do = _mm_nt(dh, W["fox_w_o"], BF16, "fox_do")
    G["fox_w_o"] = _mm_tn(fox_o, dh, "fox_dwo")
    dq_t, dk, dv, colsum = _fox_bwd(fox_qkv, fox_qkv_t, fox_ox_t, do, do.T, fox_lse, f_rows, f_cols, fox_scale,
                                    "fox_bwd")
    dlogit, db_f = _forget_cumsum_bwd(f_logit, b_f, colsum, "fox_dcumsum")
    G["fox_b_f"] = db_f[:, :N_HEADS]
    dproj = jnp.concatenate([dq_t.T, dk.astype(BF16), dv.astype(BF16), dlogit.astype(BF16)], axis=1)
    da = _mm_nt(dproj, W["fox_w_qkvf"], F32, "fox_da")
    G["fox_w_qkvf"] = _mm_tn(a3, dproj, "fox_dwqkvf")
    dh, dgain = _norm_bwd(h_3, gain("norm_mix", 3), da, dh, "mix3_dnorm")
    G.setdefault("norm_mix", {})[3] = dgain

    dh = ffn_bwd(dh, ffn2, 2)
    do = _mm_nt(dh, W["mla_w_o"], BF16, "mla_do")
    G["mla_w_o"] = _mm_tn(mla_o, dh, "mla_dwo")
    dq, dk, dv, dqr, dkr = _mla_bwd(q_all, kv_all, q_rope, kr, mla_o, do, mla_lse, mla_scale, "mla_bwd")
    dqr = _rope(dqr, cos_t, sin_t, BF16, "mla_dqrope", inverse=True)
    dq_all = jnp.concatenate([dq, dqr], axis=1)
    dkr_sum = _rope(jnp.sum(dkr, axis=0), cos_t, sin_t, F32, "mla_dkrope", inverse=True)
    dkr_raw = dkr_sum[:, :MLA_ROPE] + dkr_sum[:, MLA_ROPE:2 * MLA_ROPE]
    dkv_all = jnp.concatenate([dk.astype(BF16), dv.astype(BF16)], axis=1)
    dc_q = _mm_nt(dq_all, W["mla_w_uq"], F32, "mla_dcq")
    G["mla_w_uq"] = _mm_tn(c_q, dq_all, "mla_dwuq")
    dc_kv = _mm_nt(dkv_all, W["mla_w_ukv"], F32, "mla_dckv")
    G["mla_w_ukv"] = _mm_tn(c_kv, dkv_all, "mla_dwukv")
    ddq_raw, G["mla_q_norm"] = _norm_bwd(dq_raw, P["mla_q_norm"], dc_q, None, "mla_dqnorm")
    ddkv_raw, G["mla_kv_norm"] = _norm_bwd(dkv_raw, P["mla_kv_norm"], dc_kv, None, "mla_dkvnorm")
    ddown = jnp.concatenate([ddq_raw, ddkv_raw, dkr_raw], axis=1).astype(BF16)
    da = _mm_nt(ddown, W["mla_w_down"], F32, "mla_da")
    G["mla_w_down"] = _mm_tn(a2, ddown, "mla_dwdown")
    dh, dgain = _norm_bwd(h_2, gain("norm_mix", 2), da, dh, "mix2_dnorm")
    G["norm_mix"][2] = dgain

    dh = ffn_bwd(dh, ffn1, 1)
    do = _mm_nt(dh, W["sb_w_o"], BF16, "sb_do")
    G["sb_w_o"] = _mm_tn(sb_o, dh, "sb_dwo")
    dq, dk, dv = _sb_bwd(sb_qkv, do, sb_tot, sb_scale, "sb_bwd")
    dqkv = jnp.concatenate([dq, dk.astype(BF16), dv.astype(BF16)], axis=1)
    da = _mm_nt(dqkv, W["sb_w_qkv"], F32, "sb_da")
    G["sb_w_qkv"] = _mm_tn(a1, dqkv, "sb_dwqkv")
    dh, dgain = _norm_bwd(h_1, gain("norm_mix", 1), da, dh, "mix1_dnorm")
    G["norm_mix"][1] = dgain

    dh = ffn_bwd(dh, ffn0, 0)
    dpc, G["pool_w"], G["pool_scale"] = _pool_bwd_mix(dh, pooled, W["pool_w"], P["pool_scale"], "pool_dmix")
    da = _pool_bwd_window(dpc, "pool_dwindow")
    dh, dgain, dx = _norm_bwd(h0, gain("norm_mix", 0), da, dh, "mix0_dnorm", token_rows=True)
    G["norm_mix"][0] = dgain

    G["norm_mix"] = jnp.concatenate([G["norm_mix"][i] for i in range(DEPTH)], axis=0)
    G["norm_ffn"] = jnp.concatenate([G["norm_ffn"][i] for i in range(DEPTH)], axis=0)
    G["ffn_w_down"] = jnp.stack([G["ffn_w_down"][i] for i in range(DEPTH)])
    G["ffn_w_gate"] = jnp.stack([G["ffn_w_gate"][i] for i in range(DEPTH)])
    G["ffn_w_up"] = jnp.stack([G["ffn_w_up"][i] for i in range(DEPTH)])
    G["meta"] = dh[PAD:ROW0]
    return sq, dx, G


def kernel(x, meta, norm_mix, norm_ffn, pool_w, pool_scale, sb_w_qkv, sb_w_o, mla_w_down, mla_q_norm, mla_kv_norm, mla_w_uq, mla_w_ukv, mla_w_o, fox_w_qkvf, fox_b_f, fox_w_o, ffn_w_gate, ffn_w_up, ffn_w_down, final_norm, loss_target, m_meta, m_norm_mix, m_norm_ffn, m_pool_w, m_pool_scale, m_sb_w_qkv, m_sb_w_o, m_mla_w_down, m_mla_q_norm, m_mla_kv_norm, m_mla_w_uq, m_mla_w_ukv, m_mla_w_o, m_fox_w_qkvf, m_fox_b_f, m_fox_w_o, m_ffn_w_gate, m_ffn_w_up, m_ffn_w_down, m_final_norm, v_meta, v_norm_mix, v_norm_ffn, v_pool_w, v_pool_scale, v_sb_w_qkv, v_sb_w_o, v_mla_w_down, v_mla_q_norm, v_mla_kv_norm, v_mla_w_uq, v_mla_w_ukv, v_mla_w_o, v_fox_w_qkvf, v_fox_b_f, v_fox_w_o, v_ffn_w_gate, v_ffn_w_up, v_ffn_w_down, v_final_norm):
    w = dict(meta=meta, norm_mix=norm_mix, norm_ffn=norm_ffn, pool_w=pool_w, pool_scale=pool_scale,
             sb_w_qkv=sb_w_qkv, sb_w_o=sb_w_o, mla_w_down=mla_w_down, mla_q_norm=mla_q_norm,
             mla_kv_norm=mla_kv_norm, mla_w_uq=mla_w_uq, mla_w_ukv=mla_w_ukv, mla_w_o=mla_w_o,
             fox_w_qkvf=fox_w_qkvf, fox_b_f=fox_b_f, fox_w_o=fox_w_o, ffn_w_gate=ffn_w_gate, ffn_w_up=ffn_w_up,
             ffn_w_down=ffn_w_down, final_norm=final_norm)
    m = dict(meta=m_meta, norm_mix=m_norm_mix, norm_ffn=m_norm_ffn, pool_w=m_pool_w, pool_scale=m_pool_scale,
             sb_w_qkv=m_sb_w_qkv, sb_w_o=m_sb_w_o, mla_w_down=m_mla_w_down, mla_q_norm=m_mla_q_norm,
             mla_kv_norm=m_mla_kv_norm, mla_w_uq=m_mla_w_uq, mla_w_ukv=m_mla_w_ukv, mla_w_o=m_mla_w_o,
             fox_w_qkvf=m_fox_w_qkvf, fox_b_f=m_fox_b_f, fox_w_o=m_fox_w_o, ffn_w_gate=m_ffn_w_gate,
             ffn_w_up=m_ffn_w_up, ffn_w_down=m_ffn_w_down, final_norm=m_final_norm)
    v = dict(meta=v_meta, norm_mix=v_norm_mix, norm_ffn=v_norm_ffn, pool_w=v_pool_w, pool_scale=v_pool_scale,
             sb_w_qkv=v_sb_w_qkv, sb_w_o=v_sb_w_o, mla_w_down=v_mla_w_down, mla_q_norm=v_mla_q_norm,
             mla_kv_norm=v_mla_kv_norm, mla_w_uq=v_mla_w_uq, mla_w_ukv=v_mla_w_ukv, mla_w_o=v_mla_w_o,
             fox_w_qkvf=v_fox_w_qkvf, fox_b_f=v_fox_b_f, fox_w_o=v_fox_w_o, ffn_w_gate=v_ffn_w_gate,
             ffn_w_up=v_ffn_w_up, ffn_w_down=v_ffn_w_down, final_norm=v_final_norm)

    sh_names = tuple(n for n, _ in SHARDED)
    sh_axis = dict(SHARDED)
    shapes = {n: w[n].shape for n in WEIGHT_NAMES}
    wire = lambda n: F32 if n in KEPT_F32 else BF16

    gathered = _all_gather([w[n].astype(wire(n)) for n in sh_names], "gather_weights")
    full = {n: _whole_from_gathered(g, sh_axis[n]) for n, g in zip(sh_names, gathered)}
    W = _kernel_weights(full)
    P = dict(meta=full["meta"], mla_q_norm=full["mla_q_norm"], mla_kv_norm=full["mla_kv_norm"],
             norm_mix=norm_mix, norm_ffn=norm_ffn, pool_scale=pool_scale, fox_b_f=fox_b_f, final_norm=final_norm)

    sq, dx, G = _local_step(x[0], loss_target[0], W, P)
    loss = lax.psum(0.5 * jnp.sum(sq) / D_MODEL, ("x", "y", "c"))
    grad_x = dx[None]

    gw = _reference_grads(G)
    gw["meta"] = G["meta"]
    rc = {n: (int(np.prod(shapes[n][:-1])), shapes[n][-1]) for n in sh_names}
    parts = [_parts_from_whole(gw[n], sh_axis[n]).astype(wire(n)).reshape((N_DEV,) + rc[n]) for n in sh_names]
    from_sibling = _exchange_siblings(parts, "exchange_grads_d2d")
    sums = [_pair_sum(a, b, f"pair_sum_{n}") for n, a, b in zip(sh_names, parts, from_sibling)]
    landed = _exchange_chips(sums, "exchange_grads_ici")
    results = {}
    for n, got in zip(sh_names, landed):
        outs = _adamw(w[n].reshape(rc[n]), got, m[n].reshape(rc[n]), v[n].reshape(rc[n]), f"adamw_{n}")
        results[n] = [o.reshape(shapes[n]) for o in outs]

    rep_g = dict(norm_mix=G["norm_mix"], norm_ffn=G["norm_ffn"], pool_scale=G["pool_scale"], fox_b_f=G["fox_b_f"],
                 final_norm=G["final_norm"])
    (rep_all,) = _all_gather([_pack_rows(rep_g, REPLICATED)], "gather_replicated_grads")
    rep_out = _adamw(_pack_rows(w, REPLICATED), rep_all, _pack_rows(m, REPLICATED), _pack_rows(v, REPLICATED),
                     "adamw_replicated")
    rep = [_unpack_rows(o, shapes, REPLICATED) for o in rep_out]
    for n in REPLICATED:
        results[n] = [r[n] for r in rep]

    outs = [results[n][k] for k in range(4) for n in WEIGHT_NAMES]
    return (loss, grad_x, *outs)
```

```python
import numpy as np
import jax
import jax.numpy as jnp
from jax import lax
from jax.experimental import pallas as pl
from jax.experimental.pallas import tpu as pltpu

F32 = jnp.float32
BF16 = jnp.bfloat16

N_DEV = 8
D_MODEL = 1024
N_META = 16
PAD = 240
ROW0 = PAD + N_META
EPS = 1e-6
POOL_WINDOWS = (2, 4, 8, 16)
POOL_GROUP = 256
HALO = 128
N_HEADS = 16
HEAD_DIM = 64
N_PAIRS = N_HEADS // 2
MLA_Q_RANK = 384
MLA_KV_RANK = 256
MLA_NOPE = 64
MLA_ROPE = 32
ROPE_THETA = 10000.0
D_FF = 2816
DEPTH = 4
ATTN_TILE = 768
ATTN_BWD_TILE = 768
WALK_TILE = 256
FOX_TILE = 384
NEG = -1e30
LOG2E = 1.4426950408889634
EXP_ZERO = -110.0
VMEM_LIMIT = 56 * 2**20
ADAM_TILE_ELEMS = 192 * 1024

ADAM_LR = 0.001
ADAM_B1 = 0.9
ADAM_B2 = 0.999
ADAM_EPS = 1e-08
ADAM_WD = 0.01
ADAM_STEP = 10

MESH = pl.DeviceIdType.MESH


def _params(sem=None):
    return pltpu.CompilerParams(dimension_semantics=sem, vmem_limit_bytes=VMEM_LIMIT)


def _pick(n, cands):
    for c in cands:
        if n % c == 0:
            return c
    return n


def _col_tile(n, cap=1536):
    best = None
    for t in range(128, min(n, cap) + 1, 128):
        if n % t == 0:
            best = t
    return best if best is not None else n


def _dot(a, b):
    return jnp.dot(a, b, preferred_element_type=F32)


def _dot_nt(a, b):
    return lax.dot_general(a, b, (((1,), (1,)), ((), ())), preferred_element_type=F32)


def _dot_tn(a, b):
    return lax.dot_general(a, b, (((0,), (0,)), ((), ())), preferred_element_type=F32)


def _mm_nn(a, b, out_dtype, name, res=None):
    M, K = a.shape
    N = b.shape[1]
    tm = _pick(M, (768, 512, 256, 128))
    tn = _col_tile(N)

    def body(*refs):
        if res is None:
            a_ref, b_ref, o_ref = refs
        else:
            a_ref, b_ref, r_ref, o_ref = refs
        acc = _dot(a_ref[...].astype(BF16), b_ref[...])
        if res is not None:
            acc = acc + r_ref[...]
        o_ref[...] = acc.astype(o_ref.dtype)

    in_specs = [pl.BlockSpec((tm, K), lambda n, m: (m, 0)), pl.BlockSpec((K, tn), lambda n, m: (0, n))]
    args = [a, b]
    if res is not None:
        in_specs.append(pl.BlockSpec((tm, tn), lambda n, m: (m, n)))
        args.append(res)
    return pl.pallas_call(
        body, name=name, grid=(N // tn, M // tm), in_specs=in_specs,
        out_specs=pl.BlockSpec((tm, tn), lambda n, m: (m, n)),
        out_shape=jax.ShapeDtypeStruct((M, N), out_dtype),
        compiler_params=_params(("parallel", "parallel")))(*args)


def _mm_nt(a, w, out_dtype, name):
    M, N = a.shape
    K = w.shape[0]
    tm = _pick(M, (768, 512, 256, 128)) if N <= 3200 else _pick(M, (256, 128))
    tk = _col_tile(K, 1024)

    def body(a_ref, w_ref, o_ref):
        o_ref[...] = _dot_nt(a_ref[...].astype(BF16), w_ref[...]).astype(o_ref.dtype)

    return pl.pallas_call(
        body, name=name, grid=(K // tk, M // tm),
        in_specs=[pl.BlockSpec((tm, N), lambda k, m: (m, 0)), pl.BlockSpec((tk, N), lambda k, m: (k, 0))],
        out_specs=pl.BlockSpec((tm, tk), lambda k, m: (m, k)),
        out_shape=jax.ShapeDtypeStruct((M, K), out_dtype),
        compiler_params=_params(("parallel", "parallel")))(a, w)


def _mm_nt2(a1, w1, a2, w2, out_dtype, name):
    M, N = a1.shape
    K = w1.shape[0]
    tm = _pick(M, (384, 256, 128))

    def body(a1_ref, w1_ref, a2_ref, w2_ref, o_ref):
        o_ref[...] = (_dot_nt(a1_ref[...], w1_ref[...]) + _dot_nt(a2_ref[...], w2_ref[...])).astype(o_ref.dtype)

    a_spec = pl.BlockSpec((tm, N), lambda m: (m, 0))
    w_spec = pl.BlockSpec((K, N), lambda m: (0, 0))
    return pl.pallas_call(
        body, name=name, grid=(M // tm,), in_specs=[a_spec, w_spec, a_spec, w_spec],
        out_specs=pl.BlockSpec((tm, K), lambda m: (m, 0)),
        out_shape=jax.ShapeDtypeStruct((M, K), out_dtype),
        compiler_params=_params(("parallel",)))(a1, w1, a2, w2)


def _mm_tn(a, b, name):
    M, K = a.shape
    N = b.shape[1]
    tm = _pick(M, (768, 512, 256, 128))
    tk = _col_tile(K, 1408)
    tn = _col_tile(N, 1408)

    def body(a_ref, b_ref, o_ref):
        @pl.when(pl.program_id(2) == 0)
        def _():
            o_ref[...] = jnp.zeros_like(o_ref)
        o_ref[...] += _dot_tn(a_ref[...].astype(BF16), b_ref[...].astype(BF16))

    return pl.pallas_call(
        body, name=name, grid=(K // tk, N // tn, M // tm),
        in_specs=[pl.BlockSpec((tm, tk), lambda k, n, m: (m, k)), pl.BlockSpec((tm, tn), lambda k, n, m: (m, n))],
        out_specs=pl.BlockSpec((tk, tn), lambda k, n, m: (k, n)),
        out_shape=jax.ShapeDtypeStruct((K, N), F32),
        compiler_params=_params(("parallel", "parallel", "arbitrary")))(a, b)


def _norm_fwd(h, gain, out_dtype, name):
    M, C = h.shape
    tm = _pick(M, (768, 512, 256, 128))

    def body(h_ref, g_ref, a_ref):
        x = h_ref[...]
        r = lax.rsqrt(jnp.mean(x * x, axis=-1, keepdims=True) + EPS)
        a_ref[...] = ((x * r) * g_ref[...]).astype(a_ref.dtype)

    return pl.pallas_call(
        body, name=name, grid=(M // tm,),
        in_specs=[pl.BlockSpec((tm, C), lambda m: (m, 0)), pl.BlockSpec((1, C), lambda m: (0, 0))],
        out_specs=pl.BlockSpec((tm, C), lambda m: (m, 0)),
        out_shape=jax.ShapeDtypeStruct((M, C), out_dtype),
        compiler_params=_params(("parallel",)))(h, gain)


def _norm_bwd(h, gain, da, dres, name, token_rows=False):
    M, C = h.shape
    tm = ROW0 if token_rows else _pick(M, (768, 512, 256, 128))

    def body(*refs):
        refs = list(refs)
        dx_ref = refs.pop() if token_rows else None
        if dres is None:
            h_ref, g_ref, da_ref, dh_ref, dg_ref = refs
        else:
            h_ref, g_ref, da_ref, dr_ref, dh_ref, dg_ref = refs
        x = h_ref[...]
        r = lax.rsqrt(jnp.mean(x * x, axis=-1, keepdims=True) + EPS)
        y = x * r
        dav = da_ref[...].astype(F32)
        dy = dav * g_ref[...]
        dh = r * (dy - y * jnp.mean(dy * y, axis=-1, keepdims=True))
        if dres is not None:
            dh = dh + dr_ref[...]
        dh_ref[...] = dh
        if token_rows:
            dx_ref[...] = dh

        @pl.when(pl.program_id(0) == 0)
        def _():
            dg_ref[...] = jnp.zeros_like(dg_ref)
        dg_ref[...] += jnp.sum(dav * y, axis=0, keepdims=True)

    row = pl.BlockSpec((tm, C), lambda m: (m, 0))
    vec = pl.BlockSpec((1, C), lambda m: (0, 0))
    in_specs = [row, vec, row] + ([row] if dres is not None else [])
    args = [h, gain, da] + ([dres] if dres is not None else [])
    out_specs = [row, vec]
    out_shape = [jax.ShapeDtypeStruct((M, C), F32), jax.ShapeDtypeStruct((1, C), F32)]
    if token_rows:
        out_specs.append(pl.BlockSpec((tm, C), lambda m: (jnp.maximum(m - 1, 0), 0)))
        out_shape.append(jax.ShapeDtypeStruct((M - ROW0, C), F32))
    return pl.pallas_call(
        body, name=name, grid=(M // tm,), in_specs=in_specs, out_specs=out_specs, out_shape=out_shape,
        compiler_params=_params(("arbitrary",)))(*args)


def _ffn_up(b, w_g, w_u, name):
    M, K = b.shape
    F = w_g.shape[1]
    tm = _pick(M, (768, 512, 256, 128))
    tn = _col_tile(F, 1408)
    nb = F // tn

    def body(b_ref, wg_ref, wu_ref, g_ref, u_ref, act_ref):
        x = b_ref[...]
        g = _dot(x, wg_ref[...])
        u = _dot(x, wu_ref[...])
        g_ref[...] = g.astype(g_ref.dtype)
        u_ref[...] = u.astype(u_ref.dtype)
        act_ref[...] = ((g * jax.nn.sigmoid(g)) * u).astype(act_ref.dtype)

    blk = pl.BlockSpec((tm, tn), lambda n, m: (m, n))
    return pl.pallas_call(
        body, name=name, grid=(nb, M // tm),
        in_specs=[pl.BlockSpec((tm, K), lambda n, m: (m, 0)),
                  pl.BlockSpec((K, tn), lambda n, m: (0, n)),
                  pl.BlockSpec((K, tn), lambda n, m: (0, n))],
        out_specs=[blk, blk, blk],
        out_shape=[jax.ShapeDtypeStruct((M, F), BF16), jax.ShapeDtypeStruct((M, F), BF16),
                   jax.ShapeDtypeStruct((M, F), BF16)],
        compiler_params=_params(("parallel", "parallel")))(b, w_g, w_u)


def _ffn_dact(dy, w_d, g, u, name):
    M, K = dy.shape
    F = w_d.shape[0]
    tm = _pick(M, (768, 512, 256, 128))
    tn = _col_tile(F, 1408)
    nb = F // tn

    def body(dy_ref, wd_ref, g_ref, u_ref, dg_ref, du_ref):
        dact = _dot_nt(dy_ref[...].astype(BF16), wd_ref[...])
        gv = g_ref[...].astype(F32)
        s = jax.nn.sigmoid(gv)
        silu = gv * s
        dg_ref[...] = (dact * u_ref[...].astype(F32) * (s * (1.0 + gv * (1.0 - s)))).astype(dg_ref.dtype)
        du_ref[...] = (dact * silu).astype(du_ref.dtype)

    blk = pl.BlockSpec((tm, tn), lambda n, m: (m, n))
    return pl.pallas_call(
        body, name=name, grid=(nb, M // tm),
        in_specs=[pl.BlockSpec((tm, K), lambda n, m: (m, 0)), pl.BlockSpec((tn, K), lambda n, m: (n, 0)), blk, blk],
        out_specs=[blk, blk],
        out_shape=[jax.ShapeDtypeStruct((M, F), BF16), jax.ShapeDtypeStruct((M, F), BF16)],
        compiler_params=_params(("parallel", "parallel")))(dy, w_d, g, u)


def _loss_head(h, gain, target, name):
    M, C = h.shape
    tm = ROW0
    assert M % tm == 0 and target.shape[0] == M - ROW0

    def body(h_ref, g_ref, t_ref, sq_ref, dh_ref, dg_ref):
        i = pl.program_id(0)

        @pl.when(i == 0)
        def _():
            sq_ref[...] = jnp.zeros_like(sq_ref)
            dg_ref[...] = jnp.zeros_like(dg_ref)
            dh_ref[...] = jnp.zeros_like(dh_ref)

        @pl.when(i > 0)
        def _():
            x = h_ref[...]
            r = lax.rsqrt(jnp.mean(x * x, axis=-1, keepdims=True) + EPS)
            y = x * r
            err = y * g_ref[...] - t_ref[...]
            sq_ref[...] += jnp.sum(err * err, axis=0, keepdims=True)
            da = err * (1.0 / C)
            dy = da * g_ref[...]
            dh_ref[...] = r * (dy - y * jnp.mean(dy * y, axis=-1, keepdims=True))
            dg_ref[...] += jnp.sum(da * y, axis=0, keepdims=True)

    row = pl.BlockSpec((tm, C), lambda m: (m, 0))
    vec = pl.BlockSpec((1, C), lambda m: (0, 0))
    return pl.pallas_call(
        body, name=name, grid=(M // tm,),
        in_specs=[row, vec, pl.BlockSpec((tm, C), lambda m: (jnp.maximum(m - 1, 0), 0))],
        out_specs=[vec, row, vec],
        out_shape=[jax.ShapeDtypeStruct((1, C), F32), jax.ShapeDtypeStruct((M, C), F32),
                   jax.ShapeDtypeStruct((1, C), F32)],
        compiler_params=_params(("arbitrary",)))(h, gain, target)


def _band_dot(band, x):
    hi = x.astype(BF16)
    rest = x - hi.astype(F32)
    mid = rest.astype(BF16)
    lo = (rest - mid.astype(F32)).astype(BF16)
    return _dot(band, hi) + _dot(band, mid) + _dot(band, lo)


def _pool_pos(row0, tm):
    return row0 + lax.broadcasted_iota(jnp.int32, (tm, 1), 0) - PAD


def _pool_fwd(h, a, w, scale, name):
    M, C = a.shape
    tm = 256
    hb = tm // HALO

    def body(h_ref, a_ref, halo_ref, w_ref, s_ref, o_ref, p_ref):
        i = pl.program_id(0)
        row0 = i * tm
        ext = jnp.concatenate([halo_ref[...], a_ref[...]], axis=0)
        src = row0 - HALO + lax.broadcasted_iota(jnp.int32, (tm + HALO, 1), 0)
        ext = jnp.where(src >= PAD, ext, 0.0)
        r = lax.broadcasted_iota(jnp.int32, (tm, tm + HALO), 0)
        c = lax.broadcasted_iota(jnp.int32, (tm, tm + HALO), 1)
        pos = _pool_pos(row0, tm)
        for g, win in enumerate(POOL_WINDOWS):
            band = ((c <= r + HALO) & (c > r + HALO - win)).astype(BF16)
            cols = slice(g * POOL_GROUP, (g + 1) * POOL_GROUP)
            xg = ext[:, cols]
            tot = _band_dot(band, xg)
            cnt = jnp.clip(pos + 1, 1, win).astype(F32)
            pooled = (tot / cnt - xg[HALO:]).astype(BF16)
            p_ref[:, cols] = pooled
            mixed = _dot(pooled, w_ref[g])
            o_ref[:, cols] = h_ref[:, cols] + mixed * s_ref[:, cols]

    row = pl.BlockSpec((tm, C), lambda m: (m, 0))
    return pl.pallas_call(
        body, name=name, grid=(M // tm,),
        in_specs=[row, row, pl.BlockSpec((HALO, C), lambda m: (jnp.maximum(m * hb - 1, 0), 0)),
                  pl.BlockSpec((4, POOL_GROUP, POOL_GROUP), lambda m: (0, 0, 0)),
                  pl.BlockSpec((1, C), lambda m: (0, 0))],
        out_specs=[row, row],
        out_shape=[jax.ShapeDtypeStruct((M, C), F32), jax.ShapeDtypeStruct((M, C), BF16)],
        compiler_params=_params(("parallel",)))(h, a, a, w, scale)


def _pool_bwd_mix(dout, pooled, w, scale, name):
    M, C = dout.shape
    tm = 256

    def body(do_ref, p_ref, w_ref, s_ref, dpc_ref, dw_ref, ds_ref):
        i = pl.program_id(0)

        @pl.when(i == 0)
        def _():
            dw_ref[...] = jnp.zeros_like(dw_ref)
            ds_ref[...] = jnp.zeros_like(ds_ref)

        pos = _pool_pos(i * tm, tm)
        for g, win in enumerate(POOL_WINDOWS):
            cols = slice(g * POOL_GROUP, (g + 1) * POOL_GROUP)
            do = do_ref[:, cols]
            pooled = p_ref[:, cols]
            mixed = _dot(pooled, w_ref[g])
            ds_ref[:, cols] += jnp.sum(do * mixed, axis=0, keepdims=True)
            dmix = (do * s_ref[:, cols]).astype(BF16)
            dw_ref[g] += _dot_tn(pooled, dmix)
            dp = _dot_nt(dmix, w_ref[g])
            cnt = jnp.clip(pos + 1, 1, win).astype(F32)
            dpc_ref[:, cols] = dp / cnt

    row = pl.BlockSpec((tm, C), lambda m: (m, 0))
    wspec = pl.BlockSpec((4, POOL_GROUP, POOL_GROUP), lambda m: (0, 0, 0))
    vec = pl.BlockSpec((1, C), lambda m: (0, 0))
    return pl.pallas_call(
        body, name=name, grid=(M // tm,),
        in_specs=[row, row, wspec, vec], out_specs=[row, wspec, vec],
        out_shape=[jax.ShapeDtypeStruct((M, C), F32), jax.ShapeDtypeStruct((4, POOL_GROUP, POOL_GROUP), F32),
                   jax.ShapeDtypeStruct((1, C), F32)],
        compiler_params=_params(("arbitrary",)))(dout, pooled, w, scale)


def _pool_bwd_window(dpc, name):
    M, C = dpc.shape
    tm = 256
    hb = tm // HALO
    last = M // HALO - 1

    def body(d_ref, halo_ref, da_ref):
        i = pl.program_id(0)
        row0 = i * tm
        ext = jnp.concatenate([d_ref[...], halo_ref[...]], axis=0)
        src = row0 + lax.broadcasted_iota(jnp.int32, (tm + HALO, 1), 0)
        ext = jnp.where(src < M, ext, 0.0)
        r = lax.broadcasted_iota(jnp.int32, (tm, tm + HALO), 0)
        c = lax.broadcasted_iota(jnp.int32, (tm, tm + HALO), 1)
        pos = _pool_pos(row0, tm)
        for g, win in enumerate(POOL_WINDOWS):
            band = ((c >= r) & (c < r + win)).astype(BF16)
            cols = slice(g * POOL_GROUP, (g + 1) * POOL_GROUP)
            xg = ext[:, cols]
            tot = _band_dot(band, xg)
            cnt = jnp.clip(pos + 1, 1, win).astype(F32)
            da_ref[:, cols] = jnp.where(pos >= 0, tot - xg[:tm] * cnt, 0.0)

    row = pl.BlockSpec((tm, C), lambda m: (m, 0))
    return pl.pallas_call(
        body, name=name, grid=(M // tm,),
        in_specs=[row, pl.BlockSpec((HALO, C), lambda m: (jnp.minimum((m + 1) * hb, last), 0))],
        out_specs=row, out_shape=jax.ShapeDtypeStruct((M, C), F32),
        compiler_params=_params(("parallel",)))(dpc, dpc)


def _head_masks():
    lane = lax.broadcasted_iota(jnp.int32, (1, 128), 1)
    return lane < HEAD_DIM, lane


def _split_heads(x, first):
    z = jnp.zeros_like(x)
    return jnp.where(first, x, z), jnp.where(first, z, x)


def _split_rope(x, lane):
    z = jnp.zeros_like(x)
    return jnp.where(lane < MLA_ROPE, x, z), jnp.where((lane >= MLA_ROPE) & (lane < 2 * MLA_ROPE), x, z)


def _walk_causal(i, step):
    def mid(kb, carry):
        step(kb, False)
        return carry

    step(0, True)
    lax.fori_loop(1, i, mid, 0)

    @pl.when(i > 0)
    def _():
        step(i, True)


def _mla_fwd(q_all, kv_all, qr, kr, scale, name):
    M = q_all.shape[0]
    t = ATTN_TILE

    def body(q_ref, k_ref, v_ref, qr_ref, kr_ref, o_ref, lse_ref, m_s, l_s, acc_s):
        i = pl.program_id(1)
        first, lane = _head_masks()
        qs = _split_heads(q_ref[...], first)
        qrs = _split_rope(qr_ref[...], lane)
        qcat = tuple(jnp.concatenate([qs[hh], qrs[hh]], axis=1) for hh in range(2))
        m_s[...] = jnp.full_like(m_s, NEG)
        l_s[...] = jnp.zeros_like(l_s)
        acc_s[...] = jnp.zeros_like(acc_s)
        qpos = i * t + lax.broadcasted_iota(jnp.int32, (t, t), 0)
        kidx = lax.broadcasted_iota(jnp.int32, (t, t), 1)

        def step(kb, masked):
            k0 = pl.multiple_of(kb * t, t)
            kcat = jnp.concatenate([k_ref[pl.ds(k0, t), :], kr_ref[pl.ds(k0, t), :]], axis=1)
            vs = _split_heads(v_ref[pl.ds(k0, t), :], first)
            if masked:
                kpos = k0 + kidx
                valid = (kpos <= qpos) & (kpos >= PAD)
            pv = None
            alphas = []
            for hh in range(2):
                s = _dot_nt(qcat[hh], kcat)
                if masked:
                    s = jnp.where(valid, s, NEG)
                m_old = m_s[hh]
                m_new = jnp.maximum(m_old, jnp.max(s, axis=1, keepdims=True))
                p = jnp.exp2((s - m_new) * (scale * LOG2E))
                alpha = jnp.exp2((m_old - m_new) * (scale * LOG2E))
                l_s[hh] = alpha * l_s[hh] + jnp.sum(p, axis=1, keepdims=True)
                m_s[hh] = m_new
                d = _dot(p.astype(BF16), vs[hh])
                pv = d if pv is None else pv + d
                alphas.append(alpha)
            acc_s[...] = acc_s[...] * jnp.where(first, alphas[0], alphas[1]) + pv

        _walk_causal(i, step)
        o_ref[...] = (acc_s[...] * jnp.where(first, 1.0 / l_s[0], 1.0 / l_s[1])).astype(o_ref.dtype)
        lse_ref[:, 0:1] = m_s[0] * scale + jnp.log(l_s[0])
        lse_ref[:, 1:2] = m_s[1] * scale + jnp.log(l_s[1])

    blk = pl.BlockSpec((t, 128), lambda j, i: (i, j))
    return pl.pallas_call(
        body, name=name, grid=(N_PAIRS, M // t),
        in_specs=[blk, pl.BlockSpec((M, 128), lambda j, i: (0, j)), pl.BlockSpec((M, 128), lambda j, i: (0, N_PAIRS + j)),
                  blk, pl.BlockSpec((M, 128), lambda j, i: (0, 0))],
        out_specs=[blk, pl.BlockSpec((None, t, 2), lambda j, i: (j, i, 0))],
        out_shape=[jax.ShapeDtypeStruct((M, N_PAIRS * 128), BF16), jax.ShapeDtypeStruct((N_PAIRS, M, 2), F32)],
        scratch_shapes=[pltpu.VMEM((2, t, 1), F32), pltpu.VMEM((2, t, 1), F32), pltpu.VMEM((t, 128), F32)],
        compiler_params=_params(("parallel", "arbitrary")))(q_all, kv_all, kv_all, qr, kr)


def _mla_bwd(q_all, kv_all, qr, kr, o, do, lse, scale, name):
    M = q_all.shape[0]
    t = ATTN_BWD_TILE

    def body(q_ref, kv_hbm, qr_ref, kr_hbm, o_ref, do_ref, lse_ref,
             dq_ref, dk_hbm, dv_hbm, dqr_ref, dkr_hbm,
             k_ref, v_ref, kr_ref, dk_ref, dv_ref, dkr_ref, dq_s, lse_s, delta_s):
        j = pl.program_id(0)
        i = pl.program_id(1)
        first, lane = _head_masks()
        every = pl.ds(0, M)
        kcols = pl.ds(pl.multiple_of(j * 128, 128), 128)
        vcols = pl.ds(pl.multiple_of((N_PAIRS + j) * 128, 128), 128)

        @pl.when(i == 0)
        def _():
            pltpu.sync_copy(kv_hbm.at[every, kcols], k_ref)
            pltpu.sync_copy(kv_hbm.at[every, vcols], v_ref)
            pltpu.sync_copy(kr_hbm, kr_ref)
            dk_ref[...] = jnp.zeros_like(dk_ref)
            dv_ref[...] = jnp.zeros_like(dv_ref)
            dkr_ref[...] = jnp.zeros_like(dkr_ref)

        qs = _split_heads(q_ref[...], first)
        qrs = _split_rope(qr_ref[...], lane)
        qcat = tuple(jnp.concatenate([qs[hh], qrs[hh]], axis=1) for hh in range(2))
        dov = do_ref[...]
        dos = _split_heads(dov, first)
        prod = dov.astype(F32) * o_ref[...].astype(F32)
        deltas = (jnp.sum(jnp.where(first, prod, 0.0), axis=1, keepdims=True),
                  jnp.sum(jnp.where(first, 0.0, prod), axis=1, keepdims=True))
        for hh in range(2):
            lse_s[hh] = jnp.broadcast_to(lse_ref[:, hh:hh + 1], (t, t))
            delta_s[hh] = jnp.broadcast_to(deltas[hh], (t, t))
        dq_s[...] = jnp.zeros_like(dq_s)
        qpos = i * t + lax.broadcasted_iota(jnp.int32, (t, t), 0)
        kidx = lax.broadcasted_iota(jnp.int32, (t, t), 1)

        def step(kb, masked):
            k0 = pl.multiple_of(kb * t, t)
            rows = pl.ds(k0, t)
            k = k_ref[rows, :]
            v = v_ref[rows, :]
            kr = kr_ref[rows, :]
            kcat = jnp.concatenate([k, kr], axis=1)
            ks = _split_heads(k, first)
            krs = _split_rope(kr, lane)
            if masked:
                kpos = k0 + kidx
                valid = (kpos <= qpos) & (kpos >= PAD)
            dq = dk = dv = None
            for hh in range(2):
                s = _dot_nt(qcat[hh], kcat) * scale
                if masked:
                    s = jnp.where(valid, s, NEG)
                p = jnp.exp(s - lse_s[hh])
                ds = p * (_dot_nt(dos[hh], v) - delta_s[hh])
                dsb = (ds * scale).astype(BF16)
                a = _dot(dsb, jnp.concatenate([ks[hh], krs[hh]], axis=1))
                b = _dot_tn(dsb, qcat[hh])
                c = _dot_tn(p.astype(BF16), dos[hh])
                dq = a if dq is None else dq + a
                dk = b if dk is None else dk + b
                dv = c if dv is None else dv + c
            dq_s[...] += dq
            dk_ref[rows, :] += dk[:, :128]
            dkr_ref[rows, :] += dk[:, 128:]
            dv_ref[rows, :] += dv

        _walk_causal(i, step)
        dq_ref[...] = dq_s[:, :128].astype(dq_ref.dtype)
        dqr_ref[...] = dq_s[:, 128:].astype(dqr_ref.dtype)

        @pl.when(i == M // t - 1)
        def _():
            pltpu.sync_copy(dk_ref, dk_hbm.at[every, kcols])
            pltpu.sync_copy(dv_ref, dv_hbm.at[every, kcols])
            pltpu.sync_copy(dkr_ref, dkr_hbm.at[j])

    blk = pl.BlockSpec((t, 128), lambda j, i: (i, j))
    whole = pl.BlockSpec(memory_space=pl.ANY)
    wide = jax.ShapeDtypeStruct((M, N_PAIRS * 128), F32)
    slab = lambda dtype: pltpu.VMEM((M, 128), dtype)
    return pl.pallas_call(
        body, name=name, grid=(N_PAIRS, M // t),
        in_specs=[blk, whole, blk, whole, blk, blk, pl.BlockSpec((None, t, 2), lambda j, i: (j, i, 0))],
        out_specs=[blk, whole, whole, blk, whole],
        out_shape=[jax.ShapeDtypeStruct((M, N_PAIRS * 128), BF16), wide, wide,
                   jax.ShapeDtypeStruct((M, N_PAIRS * 128), BF16), jax.ShapeDtypeStruct((N_PAIRS, M, 128), F32)],
        scratch_shapes=[slab(BF16), slab(BF16), slab(BF16), slab(F32), slab(F32), slab(F32),
                        pltpu.VMEM((t, 256), F32), pltpu.VMEM((2, t, t), F32), pltpu.VMEM((2, t, t), F32)],
        compiler_params=_params(("arbitrary", "arbitrary")))(q_all, kv_all, qr, kr, o, do, lse)


def _tri(t, rel):
    j = lax.broadcasted_iota(jnp.int32, (t, t), 0)
    k = lax.broadcasted_iota(jnp.int32, (t, t), 1)
    m = {"gt": j > k, "le": j <= k, "lt": j < k}[rel]
    return m.astype(BF16)


def _lane_cumsum(x, tri):
    hi = x.astype(BF16)
    lo = (x - hi.astype(F32)).astype(BF16)
    return _dot(hi, tri) + _dot(lo, tri)


def _log_sigmoids(z):
    sp = jnp.log(1.0 + jnp.exp(-jnp.abs(z)))
    return jnp.minimum(z, 0.0) - sp, jnp.minimum(-z, 0.0) - sp


def _log_sigmoids_fast(z):
    lk = -(jnp.maximum(z, 0.0) + jnp.log(1.0 + jnp.exp(-jnp.abs(z))))
    return lk + z, lk


def _sb_fwd(qkv, scale, name):
    M = qkv.shape[0]
    t = WALK_TILE
    ck, cv = N_PAIRS, 2 * N_PAIRS

    def body(q_ref, k_ref, v_ref, o_ref, tot_ref, c_s, acc_s):
        i = pl.program_id(1)
        first, _ = _head_masks()
        qs = _split_heads(q_ref[...], first)
        c_s[...] = jnp.zeros_like(c_s)
        acc_s[...] = jnp.zeros_like(acc_s)
        tri = _tri(t, "gt")
        qpos = i * t + lax.broadcasted_iota(jnp.int32, (t, t), 0)
        kidx = lax.broadcasted_iota(jnp.int32, (t, t), 1)

        def step(kb, masked):
            k0 = pl.multiple_of(kb * t, t)
            k = k_ref[pl.ds(k0, t), :]
            vs = _split_heads(v_ref[pl.ds(k0, t), :], first)
            if masked:
                kpos = k0 + kidx
                valid = (kpos < qpos) & (kpos >= PAD)
            pv = None
            for hh in range(2):
                z = _dot_nt(qs[hh], k) * scale
                lb, lk = _log_sigmoids_fast(z)
                if masked:
                    lk = jnp.where(valid, lk, 0.0)
                a = jnp.exp(lb + (c_s[hh] + _lane_cumsum(lk, tri)))
                if masked:
                    a = jnp.where(valid, a, 0.0)
                c_s[hh] = c_s[hh] + jnp.sum(lk, axis=1, keepdims=True)
                d = _dot(a.astype(BF16), vs[hh])
                pv = d if pv is None else pv + d
            acc_s[...] += pv

        def keep_going():
            return jnp.max(jnp.maximum(c_s[0], c_s[1])) > EXP_ZERO

        def cond(carry):
            kb, go, _ = carry
            return (kb >= 1) & go

        def walk(carry):
            kb, _, n = carry
            step(kb, False)
            return kb - 1, keep_going(), n + 1

        step(i, True)
        _, go, n = lax.while_loop(cond, walk, (i - 1, keep_going(), jnp.int32(1)))
        first_too = go & (i > 0)

        @pl.when(first_too)
        def _():
            step(0, True)

        walked = n + first_too.astype(jnp.int32)
        o_ref[...] = acc_s[...].astype(o_ref.dtype)
        tot_ref[:, 0:1] = c_s[0]
        tot_ref[:, 1:2] = c_s[1]
        tot_ref[:, 2:3] = jnp.full((t, 1), walked.astype(F32))

    whole = lambda c0: pl.BlockSpec((M, 128), lambda j, i: (0, c0 + j))
    return pl.pallas_call(
        body, name=name, grid=(N_PAIRS, M // t),
        in_specs=[pl.BlockSpec((t, 128), lambda j, i: (i, j)), whole(ck), whole(cv)],
        out_specs=[pl.BlockSpec((t, 128), lambda j, i: (i, j)), pl.BlockSpec((None, t, 3), lambda j, i: (j, i, 0))],
        out_shape=[jax.ShapeDtypeStruct((M, N_PAIRS * 128), BF16), jax.ShapeDtypeStruct((N_PAIRS, M, 3), F32)],
        scratch_shapes=[pltpu.VMEM((2, t, 1), F32), pltpu.VMEM((t, 128), F32)],
        compiler_params=_params(("parallel", "arbitrary")))(qkv, qkv, qkv)


def _sb_bwd(qkv, do, tot, scale, name):
    M = qkv.shape[0]
    t = WALK_TILE
    ck, cv = N_PAIRS, 2 * N_PAIRS

    def body(q_ref, k_ref, v_ref, do_ref, tot_ref, dq_ref, dk_ref, dv_ref, pc_s, dc_s, dq_s):
        i = pl.program_id(1)
        first, _ = _head_masks()

        @pl.when(i == 0)
        def _():
            dk_ref[...] = jnp.zeros_like(dk_ref)
            dv_ref[...] = jnp.zeros_like(dv_ref)

        qs = _split_heads(q_ref[...], first)
        dos = _split_heads(do_ref[...], first)
        pc_s[...] = jnp.zeros_like(pc_s)
        dc_s[...] = jnp.zeros_like(dc_s)
        dq_s[...] = jnp.zeros_like(dq_s)
        tri_le = _tri(t, "le")
        tri_lt = _tri(t, "lt")
        qpos = i * t + lax.broadcasted_iota(jnp.int32, (t, t), 0)
        kidx = lax.broadcasted_iota(jnp.int32, (t, t), 1)

        def step(kb, masked):
            k0 = pl.multiple_of(kb * t, t)
            rows = pl.ds(k0, t)
            k = k_ref[rows, :]
            v = v_ref[rows, :]
            ks = _split_heads(k, first)
            if masked:
                kpos = k0 + kidx
                valid = (kpos < qpos) & (kpos >= PAD)
            dq = dk = dv = None
            for hh in range(2):
                z = _dot_nt(qs[hh], k) * scale
                lb, lk = _log_sigmoids_fast(z)
                if masked:
                    lk = jnp.where(valid, lk, 0.0)
                later = tot_ref[:, hh:hh + 1] - (pc_s[hh] + _lane_cumsum(lk, tri_le))
                a = jnp.exp(lb + later)
                if masked:
                    a = jnp.where(valid, a, 0.0)
                dl = a * _dot_nt(dos[hh], v)
                early = dc_s[hh] + _lane_cumsum(dl, tri_lt)
                sg = jnp.exp(lb)
                dz = (dl * (1.0 - sg) - early * sg) * scale
                if masked:
                    dz = jnp.where(valid, dz, 0.0)
                pc_s[hh] = pc_s[hh] + jnp.sum(lk, axis=1, keepdims=True)
                dc_s[hh] = dc_s[hh] + jnp.sum(dl, axis=1, keepdims=True)
                dzb = dz.astype(BF16)
                x = _dot(dzb, ks[hh])
                y = _dot_tn(dzb, qs[hh])
                w = _dot_tn(a.astype(BF16), dos[hh])
                dq = x if dq is None else dq + x
                dk = y if dk is None else dk + y
                dv = w if dv is None else dv + w
            dq_s[...] += dq
            dk_ref[rows, :] += dk
            dv_ref[rows, :] += dv

        first_walked = i + 1 - jnp.max(tot_ref[:, 2:3]).astype(jnp.int32)

        def mid(kb, carry):
            step(kb, False)
            return carry

        @pl.when((first_walked == 0) & (i > 0))
        def _():
            step(0, True)

        lax.fori_loop(jnp.maximum(first_walked, 1), i, mid, 0)
        step(i, True)
        dq_ref[...] = dq_s[...].astype(dq_ref.dtype)

    whole = lambda c0: pl.BlockSpec((M, 128), lambda j, i: (0, c0 + j))
    blk = pl.BlockSpec((t, 128), lambda j, i: (i, j))
    col = pl.BlockSpec((M, 128), lambda j, i: (0, j))
    return pl.pallas_call(
        body, name=name, grid=(N_PAIRS, M // t),
        in_specs=[blk, whole(ck), whole(cv), blk, pl.BlockSpec((None, t, 3), lambda j, i: (j, i, 0))],
        out_specs=[blk, col, col],
        out_shape=[jax.ShapeDtypeStruct((M, N_PAIRS * 128), BF16), jax.ShapeDtypeStruct((M, N_PAIRS * 128), F32),
                   jax.ShapeDtypeStruct((M, N_PAIRS * 128), F32)],
        scratch_shapes=[pltpu.VMEM((2, t, 1), F32), pltpu.VMEM((2, t, 1), F32), pltpu.VMEM((t, 128), F32)],
        compiler_params=_params(("parallel", "arbitrary")))(qkv, qkv, qkv, do, tot)


def _rows_between(lo, hi):
    r = lax.broadcasted_iota(jnp.int32, (128, 1), 0)
    return (r >= lo) & (r < hi)


def _lanes_between(lo, hi):
    c = lax.broadcasted_iota(jnp.int32, (1, 128), 1)
    return (c >= lo) & (c < hi)


def _keep(x, mask):
    return jnp.where(mask, x, jnp.zeros_like(x))


def _valid_mask(i, kb, t):
    kpos = kb * t + lax.broadcasted_iota(jnp.int32, (t, t), 0)
    qpos = i * t + lax.broadcasted_iota(jnp.int32, (t, t), 1)
    return (kpos <= qpos) & (kpos >= PAD)


def _fox_fwd(qkv, qkv_t, f_rows, f_cols, scale, name):
    M = qkv.shape[0]
    t = FOX_TILE
    first_blk = PAD // t
    ck, cv = N_PAIRS, 2 * N_PAIRS

    def body(qt_ref, k_ref, vt_ref, fq_ref, fk_ref, o_ref, lse_ref, ox_ref, m_s, l_s, acc_s, accx_s, kmax_s):
        i = pl.program_id(1)

        @pl.when(i == 0)
        def _():
            first = _lanes_between(0, 64)

            def block_max(kb, carry):
                kk = k_ref[pl.ds(pl.multiple_of(kb * t, t), t), :].astype(F32)
                kk = kk * kk
                a = jnp.max(jnp.sum(jnp.where(first, kk, 0.0), axis=1, keepdims=True))
                b = jnp.max(jnp.sum(jnp.where(first, 0.0, kk), axis=1, keepdims=True))
                return jnp.maximum(carry[0], a), jnp.maximum(carry[1], b)

            a, b = lax.fori_loop(0, M // t, block_max, (jnp.float32(0.0), jnp.float32(0.0)))
            kmax_s[0] = a
            kmax_s[1] = b

        qt = qt_ref[...]
        qts = (_keep(qt, _rows_between(0, 64)), _keep(qt, _rows_between(64, 128)))
        qf = qt.astype(F32)
        qf = qf * qf
        qbound = tuple(
            (1.001 * scale) * jnp.sqrt(jnp.sum(qf[HEAD_DIM * hh:HEAD_DIM * (hh + 1)], axis=0, keepdims=True) * kmax_s[hh])
            for hh in range(2))
        m_s[...] = jnp.full_like(m_s, NEG)
        l_s[...] = jnp.zeros_like(l_s)
        acc_s[...] = jnp.zeros_like(acc_s)
        accx_s[...] = jnp.zeros_like(accx_s)

        def step(kb, masked):
            k0 = pl.multiple_of(kb * t, t)
            rows = pl.ds(k0, t)
            k = k_ref[rows, :]
            if masked:
                valid = _valid_mask(i, kb, t)
            for hh in range(2):
                s = _dot(k, qts[hh]) * scale + (fq_ref[hh:hh + 1, :] - fk_ref[rows, hh:hh + 1])
                if masked:
                    s = jnp.where(valid, s, NEG)
                m_old = m_s[hh]
                m_new = jnp.maximum(m_old, jnp.max(s, axis=0, keepdims=True))
                p = jnp.exp(s - m_new)
                alpha = jnp.exp(m_old - m_new)
                l_s[hh] = alpha * l_s[hh] + jnp.sum(p, axis=0, keepdims=True)
                m_s[hh] = m_new
                pb = p.astype(BF16)
                hr = slice(HEAD_DIM * hh, HEAD_DIM * (hh + 1))
                vt = vt_ref[hr, rows]
                acc_s[hr, :] = acc_s[hr, :] * alpha + _dot(vt, pb)
                accx_s[hr, :] = accx_s[hr, :] * alpha + _dot(vt, (p - pb.astype(F32)).astype(BF16))

        def keep_going(kb):
            k0 = pl.multiple_of(kb * t, t)
            worst = None
            for hh in range(2):
                f0 = jnp.max(fk_ref[pl.ds(k0, 8), hh:hh + 1])
                w = jnp.max(qbound[hh] + (fq_ref[hh:hh + 1, :] - f0) - m_s[hh])
                worst = w if worst is None else jnp.maximum(worst, w)
            return worst > EXP_ZERO

        def cond(carry):
            kb, go, _ = carry
            return (kb > first_blk) & go

        def walk(carry):
            kb, _, n = carry
            step(kb, False)
            return kb - 1, keep_going(kb), n + 1

        step(i, True)
        _, go, n = lax.while_loop(cond, walk, (i - 1, keep_going(i), jnp.int32(1)))
        first_too = go & (i > first_blk)

        @pl.when(first_too)
        def _():
            step(first_blk, True)

        walked = n + first_too.astype(jnp.int32)
        for hh in range(2):
            hr = slice(HEAD_DIM * hh, HEAD_DIM * (hh + 1))
            inv = 1.0 / l_s[hh]
            o_ref[hr, :] = (acc_s[hr, :] * inv).astype(o_ref.dtype)
            ox_ref[hr, :] = (acc_s[hr, :] + accx_s[hr, :]) * inv
            lse_ref[hh:hh + 1, :] = m_s[hh] + jnp.log(l_s[hh])
        lse_ref[2:3, :] = jnp.full((1, t), walked.astype(F32))

    blk = pl.BlockSpec((128, t), lambda j, i: (j, i))
    stat = pl.BlockSpec((None, 2, t), lambda j, i: (j, 0, i))
    return pl.pallas_call(
        body, name=name, grid=(N_PAIRS, M // t),
        in_specs=[blk, pl.BlockSpec((M, 128), lambda j, i: (0, ck + j)), pl.BlockSpec((128, M), lambda j, i: (cv + j, 0)),
                  stat, pl.BlockSpec((None, M, 2), lambda j, i: (j, 0, 0))],
        out_specs=[blk, pl.BlockSpec((None, 3, t), lambda j, i: (j, 0, i)), blk],
        out_shape=[jax.ShapeDtypeStruct((N_PAIRS * 128, M), BF16), jax.ShapeDtypeStruct((N_PAIRS, 3, M), F32),
                   jax.ShapeDtypeStruct((N_PAIRS * 128, M), F32)],
        scratch_shapes=[pltpu.VMEM((2, 1, t), F32), pltpu.VMEM((2, 1, t), F32), pltpu.VMEM((128, t), F32),
                        pltpu.VMEM((128, t), F32), pltpu.SMEM((2,), F32)],
        compiler_params=_params(("parallel", "arbitrary")))(qkv_t, qkv, qkv_t, f_rows, f_cols)


def _fox_bwd(qkv, qkv_t, o_t, do, do_t, lse, f_rows, f_cols, scale, name):
    M = qkv.shape[0]
    t = FOX_TILE
    first_blk = PAD // t
    ck, cv = N_PAIRS, 2 * N_PAIRS

    def body(q_ref, qt_ref, k_ref, kt_ref, v_ref, ot_ref, do_ref, dot_ref, lse_ref, fq_ref, fk_ref,
             dq_ref, dk_ref, dv_ref, cs_ref, dq_s):
        i = pl.program_id(1)

        @pl.when(i == 0)
        def _():
            dk_ref[...] = jnp.zeros_like(dk_ref)
            dv_ref[...] = jnp.zeros_like(dv_ref)
            cs_ref[...] = jnp.zeros_like(cs_ref)

        heads_l = (_lanes_between(0, 64), _lanes_between(64, 128))
        heads_r = (_rows_between(0, 64), _rows_between(64, 128))
        q = q_ref[...]
        qt = qt_ref[...]
        do = do_ref[...]
        dot = dot_ref[...]
        qs = tuple(_keep(q, m) for m in heads_l)
        qts = tuple(_keep(qt, m) for m in heads_r)
        dos = tuple(_keep(do, m) for m in heads_l)
        dots = tuple(_keep(dot, m) for m in heads_r)
        prod = dot.astype(F32) * ot_ref[...]
        deltas = tuple(jnp.sum(prod[HEAD_DIM * hh:HEAD_DIM * (hh + 1)], axis=0, keepdims=True) for hh in range(2))
        ones = tuple(m.astype(BF16) * jnp.ones((t, 128), BF16) for m in heads_l)
        dq_s[...] = jnp.zeros_like(dq_s)

        def step(kb, masked):
            k0 = pl.multiple_of(kb * t, t)
            rows = pl.ds(k0, t)
            k = k_ref[rows, :]
            v = v_ref[rows, :]
            if masked:
                valid = _valid_mask(i, kb, t)
            dk = dv = cs = None
            for hh in range(2):
                s = _dot(k, qts[hh]) * scale + (fq_ref[hh:hh + 1, :] - fk_ref[rows, hh:hh + 1])
                if masked:
                    s = jnp.where(valid, s, NEG)
                p = jnp.exp(s - lse_ref[hh:hh + 1, :])
                ds = p * (_dot(v, dots[hh]) - deltas[hh])
                hi = ds.astype(BF16)
                lo = (ds - hi.astype(F32)).astype(BF16)
                c = _dot(hi, ones[hh]) + _dot(lo, ones[hh])
                dsb = (ds * scale).astype(BF16)
                hr = slice(HEAD_DIM * hh, HEAD_DIM * (hh + 1))
                dq_s[hr, :] += _dot(kt_ref[hr, rows], dsb)
                a = _dot(dsb, qs[hh])
                b = _dot(p.astype(BF16), dos[hh])
                dk = a if dk is None else dk + a
                dv = b if dv is None else dv + b
                cs = c if cs is None else cs + c
            dk_ref[rows, :] += dk
            dv_ref[rows, :] += dv
            cs_ref[rows, :] += cs

        first_walked = i + 1 - jnp.max(lse_ref[2:3, :]).astype(jnp.int32)

        def mid(kb, carry):
            step(kb, False)
            return carry

        @pl.when((first_walked == first_blk) & (i > first_blk))
        def _():
            step(first_blk, True)

        lax.fori_loop(jnp.maximum(first_walked, first_blk + 1), i, mid, 0)
        step(i, True)
        dq_ref[...] = dq_s[...].astype(dq_ref.dtype)

    rblk = pl.BlockSpec((t, 128), lambda j, i: (i, j))
    tblk = pl.BlockSpec((128, t), lambda j, i: (j, i))
    stat = pl.BlockSpec((None, 2, t), lambda j, i: (j, 0, i))
    stat3 = pl.BlockSpec((None, 3, t), lambda j, i: (j, 0, i))
    col = pl.BlockSpec((M, 128), lambda j, i: (0, j))
    wide = jax.ShapeDtypeStruct((M, N_PAIRS * 128), F32)
    return pl.pallas_call(
        body, name=name, grid=(N_PAIRS, M // t),
        in_specs=[rblk, tblk, pl.BlockSpec((M, 128), lambda j, i: (0, ck + j)),
                  pl.BlockSpec((128, M), lambda j, i: (ck + j, 0)), pl.BlockSpec((M, 128), lambda j, i: (0, cv + j)),
                  tblk, rblk, tblk, stat3, stat, pl.BlockSpec((None, M, 2), lambda j, i: (j, 0, 0))],
        out_specs=[tblk, col, col, pl.BlockSpec((None, M, 128), lambda j, i: (j, 0, 0))],
        out_shape=[jax.ShapeDtypeStruct((N_PAIRS * 128, M), BF16), wide, wide,
                   jax.ShapeDtypeStruct((N_PAIRS, M, 128), F32)],
        scratch_shapes=[pltpu.VMEM((128, t), F32)],
        compiler_params=_params(("parallel", "arbitrary")))(qkv, qkv_t, qkv, qkv_t, qkv, o_t, do, do_t, lse,
                                                            f_rows, f_cols)


def _rope_tables(M):
    pos = (jnp.arange(M, dtype=jnp.int32) - PAD).astype(F32)
    inv = ROPE_THETA ** (-jnp.arange(0, MLA_ROPE, 2, dtype=F32) / MLA_ROPE)
    ang = pos[:, None] * inv[None, :]
    cos, sin = jnp.cos(ang), jnp.sin(ang)
    z = jnp.zeros((M, 64), F32)
    cos_t = jnp.concatenate([cos, cos, cos, cos, z], axis=1)
    sin_t = jnp.concatenate([-sin, sin, -sin, sin, z], axis=1)
    return cos_t, sin_t


def _rope(x, cos_t, sin_t, out_dtype, name, inverse=False, lead=0):
    M, C = x.shape
    tm = _pick(M, (768, 512, 256, 128))
    nblk = (C - lead) // 128
    sign = -1.0 if inverse else 1.0

    def body(x_ref, c_ref, s_ref, o_ref):
        lane = lax.broadcasted_iota(jnp.int32, (1, 128), 1)
        low = (lane % MLA_ROPE) < (MLA_ROPE // 2)
        cos = c_ref[...]
        sin = s_ref[...] * sign
        if lead:
            o_ref[:, :lead] = x_ref[:, :lead].astype(o_ref.dtype)
        for b in range(nblk):
            cols = slice(lead + b * 128, lead + (b + 1) * 128)
            v = x_ref[:, cols].astype(F32)
            up = pltpu.roll(v, 128 - MLA_ROPE // 2, 1)
            down = pltpu.roll(v, MLA_ROPE // 2, 1)
            o_ref[:, cols] = (v * cos + jnp.where(low, up, down) * sin).astype(o_ref.dtype)

    row = pl.BlockSpec((tm, C), lambda m: (m, 0))
    tab = pl.BlockSpec((tm, 128), lambda m: (m, 0))
    return pl.pallas_call(
        body, name=name, grid=(M // tm,), in_specs=[row, tab, tab], out_specs=row,
        out_shape=jax.ShapeDtypeStruct((M, C), out_dtype),
        compiler_params=_params(("parallel",)))(x, cos_t, sin_t)


def _forget_cumsum(f_logit, bias, name):
    M = f_logit.shape[0]
    tm = 256

    def body(f_ref, b_ref, o_ref, c_s):
        i = pl.program_id(0)

        @pl.when(i == 0)
        def _():
            c_s[...] = jnp.zeros_like(c_s)
        ls, _ = _log_sigmoids(f_ref[...] + b_ref[...])
        rows = i * tm + lax.broadcasted_iota(jnp.int32, (tm, 1), 0)
        ls = jnp.where(rows >= PAD, ls, 0.0)
        r = lax.broadcasted_iota(jnp.int32, (tm, tm), 0)
        c = lax.broadcasted_iota(jnp.int32, (tm, tm), 1)
        tri = (c <= r).astype(F32)
        cum = jnp.dot(tri, ls, precision=lax.Precision.HIGHEST, preferred_element_type=F32) + c_s[...]
        o_ref[...] = cum
        c_s[...] = cum[tm - 1:tm, :]

    row = pl.BlockSpec((tm, 128), lambda m: (m, 0))
    return pl.pallas_call(
        body, name=name, grid=(M // tm,),
        in_specs=[row, pl.BlockSpec((1, 128), lambda m: (0, 0))], out_specs=row,
        out_shape=jax.ShapeDtypeStruct((M, 128), F32), scratch_shapes=[pltpu.VMEM((1, 128), F32)],
        compiler_params=_params(("arbitrary",)))(f_logit, bias)


def _forget_cumsum_bwd(f_logit, bias, colsum, name):
    M = f_logit.shape[0]
    tm = 256
    nb = M // tm

    def body(f_ref, b_ref, cs_ref, o_ref, db_ref, c_s):
        i = pl.program_id(0)

        @pl.when(i == 0)
        def _():
            c_s[...] = jnp.zeros_like(c_s)
            db_ref[...] = jnp.zeros_like(db_ref)
        rr = lax.broadcasted_iota(jnp.int32, (128, 128), 0)
        cc = lax.broadcasted_iota(jnp.int32, (128, 128), 1)
        dF = None
        for j in range(N_PAIRS):
            sel = (((rr == 0) & (cc == 2 * j)) | ((rr == HEAD_DIM) & (cc == 2 * j + 1))).astype(F32)
            d = jnp.dot(cs_ref[j], sel, precision=lax.Precision.HIGHEST, preferred_element_type=F32)
            dF = d if dF is None else dF + d
        r = lax.broadcasted_iota(jnp.int32, (tm, tm), 0)
        c = lax.broadcasted_iota(jnp.int32, (tm, tm), 1)
        tri = (c >= r).astype(F32)
        cum = c_s[...] - jnp.dot(tri, dF, precision=lax.Precision.HIGHEST, preferred_element_type=F32)
        c_s[...] = cum[0:1, :]
        _, lsn = _log_sigmoids(f_ref[...] + b_ref[...])
        rows = (nb - 1 - i) * tm + lax.broadcasted_iota(jnp.int32, (tm, 1), 0)
        dl = jnp.where(rows >= PAD, cum * jnp.exp(lsn), 0.0)
        o_ref[...] = dl
        db_ref[...] += jnp.sum(dl, axis=0, keepdims=True)

    row = pl.BlockSpec((tm, 128), lambda m: (nb - 1 - m, 0))
    vec = pl.BlockSpec((1, 128), lambda m: (0, 0))
    return pl.pallas_call(
        body, name=name, grid=(nb,),
        in_specs=[row, vec, pl.BlockSpec((N_PAIRS, tm, 128), lambda m: (0, nb - 1 - m, 0))], out_specs=[row, vec],
        out_shape=[jax.ShapeDtypeStruct((M, 128), F32), jax.ShapeDtypeStruct((1, 128), F32)],
        scratch_shapes=[pltpu.VMEM((1, 128), F32)],
        compiler_params=_params(("arbitrary",)))(f_logit, bias, colsum)


def _adamw(w, parts, m, v, name):
    R, C = w.shape
    n_parts = parts.shape[0]
    tr = R
    for d in range(8, R, 8):
        if R % d == 0 and d * C <= ADAM_TILE_ELEMS:
            tr = d
    c1 = 1.0 - ADAM_B1 ** ADAM_STEP
    c2 = 1.0 - ADAM_B2 ** ADAM_STEP

    def body(w_ref, s_ref, m_ref, v_ref, g_ref, d_ref, mo_ref, vo_ref):
        g = s_ref[0].astype(F32)
        for k in range(1, n_parts):
            g = g + s_ref[k].astype(F32)
        mn = ADAM_B1 * m_ref[...] + (1.0 - ADAM_B1) * g
        vn = ADAM_B2 * v_ref[...] + (1.0 - ADAM_B2) * (g * g)
        m_hat = mn / c1
        v_hat = vn / c2
        g_ref[...] = g
        d_ref[...] = -ADAM_LR * (m_hat / (jnp.sqrt(v_hat) + ADAM_EPS) + ADAM_WD * w_ref[...])
        mo_ref[...] = mn
        vo_ref[...] = vn

    row = pl.BlockSpec((tr, C), lambda r: (r, 0))
    shp = jax.ShapeDtypeStruct((R, C), F32)
    return pl.pallas_call(
        body, name=name, grid=(R // tr,),
        in_specs=[row, pl.BlockSpec((n_parts, tr, C), lambda r: (0, r, 0)), row, row],
        out_specs=[row, row, row, row], out_shape=[shp, shp, shp, shp],
        compiler_params=_params(("parallel",)))(w, parts, m, v)


def _position():
    return lax.axis_index("x"), lax.axis_index("y"), lax.axis_index("c")


def _all_gather(blocks, name):
    n = len(blocks)

    def body(*refs):
        x_refs, out_refs = refs[:n], refs[n:2 * n]
        send_sems, recv_sems, local_sems = refs[2 * n:]
        x, y, c = _position()
        me, sibling = (x, y, c), (x, y, 1 - c)
        chips = [(1 - x, y), (x, 1 - y), (1 - x, 1 - y)]

        def copies(k, block, to, own=False):
            slot = 4 * block[0] + 2 * block[1] + block[2]
            return [pltpu.make_async_remote_copy(
                src_ref=x_refs[p] if own else out_refs[p].at[slot], dst_ref=out_refs[p].at[slot],
                send_sem=send_sems.at[k, p], recv_sem=recv_sems.at[k, p], device_id=to, device_id_type=MESH)
                for p in range(n)]

        mine = [pltpu.make_async_copy(x_refs[p], out_refs[p].at[4 * x + 2 * y + c], local_sems.at[p]) for p in range(n)]
        for cp in mine:
            cp.start()
        first = copies(0, me, sibling, own=True)
        for j, chip in enumerate(chips):
            first += copies(1 + j, me, (*chip, c), own=True)
        for cp in first:
            cp.start()
        passed = []
        for j, chip in enumerate(chips):
            for cp in copies(1 + j, (*chip, c), me):
                cp.wait_recv()
            onward = copies(4 + j, (*chip, c), sibling)
            for cp in onward:
                cp.start()
            passed += onward
        for cp in copies(0, sibling, me):
            cp.wait_recv()
        for j, chip in enumerate(chips):
            for cp in copies(4 + j, (*chip, 1 - c), me):
                cp.wait_recv()
        for cp in first + passed:
            cp.wait_send()
        for cp in mine:
            cp.wait()

    any_spec = pl.BlockSpec(memory_space=pl.ANY)
    return pl.pallas_call(
        body, name=name, out_shape=[jax.ShapeDtypeStruct((N_DEV,) + b.shape, b.dtype) for b in blocks],
        in_specs=[any_spec] * n, out_specs=[any_spec] * n,
        scratch_shapes=[pltpu.SemaphoreType.DMA((7, n)), pltpu.SemaphoreType.DMA((7, n)), pltpu.SemaphoreType.DMA((n,))],
    )(*blocks)


N_CHIPS = 4


def _exchange_siblings(parts, name):
    n = len(parts)

    def body(*refs):
        g_refs, land_refs = refs[:n], refs[n:2 * n]
        send_sems, recv_sems = refs[2 * n:]
        x, y, c = _position()
        sibling = (x, y, 1 - c)
        sends, recvs = [], []
        for q in range(N_CHIPS):
            for p in range(n):
                sends.append(pltpu.make_async_remote_copy(
                    src_ref=g_refs[p].at[2 * q + (1 - c)], dst_ref=land_refs[p].at[q], send_sem=send_sems.at[q, p],
                    recv_sem=recv_sems.at[q, p], device_id=sibling, device_id_type=MESH))
                recvs.append(pltpu.make_async_remote_copy(
                    src_ref=g_refs[p].at[2 * q + c], dst_ref=land_refs[p].at[q], send_sem=send_sems.at[q, p],
                    recv_sem=recv_sems.at[q, p], device_id=sibling, device_id_type=MESH))
        for cp in sends:
            cp.start()
        for cp in recvs:
            cp.wait_recv()
        for cp in sends:
            cp.wait_send()

    any_spec = pl.BlockSpec(memory_space=pl.ANY)
    return pl.pallas_call(
        body, name=name, out_shape=[jax.ShapeDtypeStruct((N_CHIPS,) + p.shape[1:], p.dtype) for p in parts],
        in_specs=[any_spec] * n, out_specs=[any_spec] * n,
        scratch_shapes=[pltpu.SemaphoreType.DMA((N_CHIPS, n)), pltpu.SemaphoreType.DMA((N_CHIPS, n))],
    )(*parts)


def _pair_sum(part, from_sibling, name):
    _, R, C = part.shape
    tr = R
    for d in range(8, R, 8):
        if R % d == 0 and d * C <= ADAM_TILE_ELEMS:
            tr = d

    def body(a_ref, b_ref, o_ref):
        c = lax.axis_index("c")
        for q in range(N_CHIPS):
            o_ref[q] = (a_ref[2 * q + c].astype(F32) + b_ref[q].astype(F32)).astype(o_ref.dtype)

    return pl.pallas_call(
        body, name=name, grid=(R // tr,),
        in_specs=[pl.BlockSpec((N_DEV, tr, C), lambda r: (0, r, 0)), pl.BlockSpec((N_CHIPS, tr, C), lambda r: (0, r, 0))],
        out_specs=pl.BlockSpec((N_CHIPS, tr, C), lambda r: (0, r, 0)),
        out_shape=jax.ShapeDtypeStruct((N_CHIPS, R, C), part.dtype),
        compiler_params=_params(("parallel",)))(part, from_sibling)


def _exchange_chips(sums, name):
    n = len(sums)

    def body(*refs):
        g_refs, land_refs = refs[:n], refs[n:2 * n]
        send_sems, recv_sems, local_sems = refs[2 * n:]
        x, y, c = _position()
        me = 2 * x + y
        mine = [pltpu.make_async_copy(g_refs[p].at[me], land_refs[p].at[me], local_sems.at[p]) for p in range(n)]
        for cp in mine:
            cp.start()
        sends, recvs = [], []
        for k in range(1, N_CHIPS):
            px = 1 - x if k & 2 else x
            py = 1 - y if k & 1 else y
            peer = 2 * px + py
            for p in range(n):
                sends.append(pltpu.make_async_remote_copy(
                    src_ref=g_refs[p].at[peer], dst_ref=land_refs[p].at[me], send_sem=send_sems.at[k - 1, p],
                    recv_sem=recv_sems.at[k - 1, p], device_id=(px, py, c), device_id_type=MESH))
                recvs.append(pltpu.make_async_remote_copy(
                    src_ref=g_refs[p].at[me], dst_ref=land_refs[p].at[peer], send_sem=send_sems.at[k - 1, p],
                    recv_sem=recv_sems.at[k - 1, p], device_id=(px, py, c), device_id_type=MESH))
        for cp in sends:
            cp.start()
        for cp in recvs:
            cp.wait_recv()
        for cp in sends:
            cp.wait_send()
        for cp in mine:
            cp.wait()

    any_spec = pl.BlockSpec(memory_space=pl.ANY)
    return pl.pallas_call(
        body, name=name, out_shape=[jax.ShapeDtypeStruct(p.shape, p.dtype) for p in sums],
        in_specs=[any_spec] * n, out_specs=[any_spec] * n,
        scratch_shapes=[pltpu.SemaphoreType.DMA((3, n)), pltpu.SemaphoreType.DMA((3, n)), pltpu.SemaphoreType.DMA((n,))],
    )(*sums)


SHARDED = (("sb_w_qkv", 2), ("sb_w_o", 1), ("mla_w_down", 1), ("mla_w_uq", 2), ("mla_w_ukv", 2), ("mla_w_o", 1),
           ("fox_w_qkvf", 2), ("fox_w_o", 1), ("ffn_w_gate", 2), ("ffn_w_up", 2), ("ffn_w_down", 1),
           ("pool_w", 2), ("meta", 1), ("mla_q_norm", 1), ("mla_kv_norm", 1))
KEPT_F32 = ("meta", "mla_q_norm", "mla_kv_norm")
REPLICATED = ("norm_mix", "norm_ffn", "pool_scale", "fox_b_f", "final_norm")
WEIGHT_NAMES = ("meta", "norm_mix", "norm_ffn", "pool_w", "pool_scale", "sb_w_qkv", "sb_w_o", "mla_w_down",
                "mla_q_norm", "mla_kv_norm", "mla_w_uq", "mla_w_ukv", "mla_w_o", "fox_w_qkvf", "fox_b_f",
                "fox_w_o", "ffn_w_gate", "ffn_w_up", "ffn_w_down", "final_norm")
LANES = 1024


def _pack_rows(arrays, names):
    parts = []
    for n in names:
        flat = arrays[n].reshape(-1).astype(F32)
        rows = -(-flat.shape[0] // LANES)
        parts.append(jnp.pad(flat, (0, rows * LANES - flat.shape[0])).reshape(rows, LANES))
    rows = sum(p.shape[0] for p in parts)
    parts.append(jnp.zeros((-(-rows // 8) * 8 - rows, LANES), F32))
    return jnp.concatenate(parts, axis=0)


def _unpack_rows(buf, shapes, names):
    out, row = {}, 0
    for n in names:
        size = int(np.prod(shapes[n]))
        rows = -(-size // LANES)
        out[n] = buf[row:row + rows].reshape(-1)[:size].reshape(shapes[n])
        row += rows
    return out


def _whole_from_gathered(g, axis):
    g = jnp.moveaxis(g, 0, axis)
    shp = g.shape
    return g.reshape(shp[:axis] + (shp[axis] * shp[axis + 1],) + shp[axis + 2:])


def _parts_from_whole(whole, axis):
    shp = whole.shape
    g = whole.reshape(shp[:axis] + (N_DEV, shp[axis] // N_DEV) + shp[axis + 1:])
    return jnp.moveaxis(g, axis, 0)


def _kernel_weights(full):
    W = {}
    W["pool_w"] = full["pool_w"][0]
    W["sb_w_qkv"] = full["sb_w_qkv"][0]
    W["sb_w_o"] = full["sb_w_o"][0]
    W["mla_w_down"] = full["mla_w_down"][0]
    uq = full["mla_w_uq"][0].reshape(MLA_Q_RANK, N_HEADS, MLA_NOPE + MLA_ROPE)
    nope = uq[:, :, :MLA_NOPE].reshape(MLA_Q_RANK, N_HEADS * MLA_NOPE)
    rope = uq[:, :, MLA_NOPE:].reshape(MLA_Q_RANK, N_PAIRS, 2 * MLA_ROPE)
    rope = jnp.pad(rope, ((0, 0), (0, 0), (0, 128 - 2 * MLA_ROPE))).reshape(MLA_Q_RANK, N_PAIRS * 128)
    W["mla_w_uq"] = jnp.concatenate([nope, rope], axis=1)
    ukv = full["mla_w_ukv"][0].reshape(MLA_KV_RANK, N_HEADS, 2, HEAD_DIM)
    W["mla_w_ukv"] = jnp.transpose(ukv, (0, 2, 1, 3)).reshape(MLA_KV_RANK, 2 * N_HEADS * HEAD_DIM)
    W["mla_w_o"] = full["mla_w_o"][0]
    qkvf = full["fox_w_qkvf"][0]
    n_qkv = 3 * N_HEADS * HEAD_DIM
    W["fox_w_qkv"] = qkvf[:, :n_qkv]
    W["fox_w_f"] = jnp.pad(qkvf[:, n_qkv:], ((0, 0), (0, 128 - N_HEADS)))
    W["fox_w_qkvf"] = jnp.concatenate([W["fox_w_qkv"], W["fox_w_f"]], axis=1)
    W["fox_w_o"] = full["fox_w_o"][0]
    W["ffn_w_gate"] = full["ffn_w_gate"]
    W["ffn_w_up"] = full["ffn_w_up"]
    W["ffn_w_down"] = full["ffn_w_down"]
    return W


def _reference_grads(G):
    out = {}
    out["pool_w"] = G["pool_w"][None]
    for n in ("sb_w_qkv", "sb_w_o", "mla_w_down", "mla_w_o", "fox_w_o"):
        out[n] = G[n][None]
    duq = G["mla_w_uq"]
    nope = duq[:, :N_HEADS * MLA_NOPE].reshape(MLA_Q_RANK, N_HEADS, MLA_NOPE)
    rope = duq[:, N_HEADS * MLA_NOPE:].reshape(MLA_Q_RANK, N_PAIRS, 128)[:, :, :2 * MLA_ROPE]
    rope = rope.reshape(MLA_Q_RANK, N_HEADS, MLA_ROPE)
    out["mla_w_uq"] = jnp.concatenate([nope, rope], axis=2).reshape(1, MLA_Q_RANK, -1)
    dukv = G["mla_w_ukv"].reshape(MLA_KV_RANK, 2, N_HEADS, HEAD_DIM)
    out["mla_w_ukv"] = jnp.transpose(dukv, (0, 2, 1, 3)).reshape(1, MLA_KV_RANK, -1)
    out["fox_w_qkvf"] = G["fox_w_qkvf"][None, :, :3 * N_HEADS * HEAD_DIM + N_HEADS]
    out["ffn_w_gate"] = G["ffn_w_gate"]
    out["ffn_w_up"] = G["ffn_w_up"]
    out["ffn_w_down"] = G["ffn_w_down"]
    out["mla_q_norm"] = G["mla_q_norm"]
    out["mla_kv_norm"] = G["mla_kv_norm"]
    return out


def _pairs_col(f16):
    M = f16.shape[0]
    return jnp.transpose(f16.reshape(M, N_PAIRS, 2), (1, 0, 2))


def _pairs_row(f16):
    M = f16.shape[0]
    return jnp.transpose(f16.reshape(M, N_PAIRS, 2), (1, 2, 0))


def _local_step(x, target, W, P):
    S = x.shape[0]
    M = S + ROW0
    G = {}
    gain = lambda name, i: P[name][i][None, :]
    h0 = jnp.concatenate([jnp.zeros((PAD, D_MODEL), F32), P["meta"], x], axis=0)

    def ffn_fwd(h1, i):
        b = _norm_fwd(h1, gain("norm_ffn", i), BF16, f"ffn{i}_norm")
        g, u, act = _ffn_up(b, W["ffn_w_gate"][i], W["ffn_w_up"][i], f"ffn{i}_up")
        h2 = _mm_nn(act, W["ffn_w_down"][i], F32, f"ffn{i}_down", res=h1)
        return h2, (h1, b, g, u, act)

    def ffn_bwd(dh2, saved, i):
        h1, b, g, u, act = saved
        dg, du = _ffn_dact(dh2, W["ffn_w_down"][i], g, u, f"ffn{i}_dact")
        G.setdefault("ffn_w_down", {})[i] = _mm_tn(act, dh2, f"ffn{i}_dwd")
        db = _mm_nt2(dg, W["ffn_w_gate"][i], du, W["ffn_w_up"][i], F32, f"ffn{i}_db")
        G.setdefault("ffn_w_gate", {})[i] = _mm_tn(b, dg, f"ffn{i}_dwg")
        G.setdefault("ffn_w_up", {})[i] = _mm_tn(b, du, f"ffn{i}_dwu")
        dh1, dgain = _norm_bwd(h1, gain("norm_ffn", i), db, dh2, f"ffn{i}_dnorm")
        G.setdefault("norm_ffn", {})[i] = dgain
        return dh1

    a0 = _norm_fwd(h0, gain("norm_mix", 0), F32, "mix0_norm")
    h1_0, pooled = _pool_fwd(h0, a0, W["pool_w"], P["pool_scale"], "pool_fwd")
    h_1, ffn0 = ffn_fwd(h1_0, 0)

    sb_scale = HEAD_DIM ** -0.5
    a1 = _norm_fwd(h_1, gain("norm_mix", 1), BF16, "mix1_norm")
    sb_qkv = _mm_nn(a1, W["sb_w_qkv"], BF16, "sb_qkv")
    sb_o, sb_tot = _sb_fwd(sb_qkv, sb_scale, "sb_fwd")
    h1_1 = _mm_nn(sb_o, W["sb_w_o"], F32, "sb_out", res=h_1)
    h_2, ffn1 = ffn_fwd(h1_1, 1)

    mla_scale = (MLA_NOPE + MLA_ROPE) ** -0.5
    cos_t, sin_t = _rope_tables(M)
    a2 = _norm_fwd(h_2, gain("norm_mix", 2), BF16, "mix2_norm")
    down = _mm_nn(a2, W["mla_w_down"], F32, "mla_down")
    dq_raw = down[:, :MLA_Q_RANK]
    dkv_raw = down[:, MLA_Q_RANK:MLA_Q_RANK + MLA_KV_RANK]
    kr_raw = down[:, MLA_Q_RANK + MLA_KV_RANK:]
    c_q = _norm_fwd(dq_raw, P["mla_q_norm"], BF16, "mla_qnorm")
    c_kv = _norm_fwd(dkv_raw, P["mla_kv_norm"], BF16, "mla_kvnorm")
    q_lin = _mm_nn(c_q, W["mla_w_uq"], F32, "mla_uq")
    q_all = _rope(q_lin, cos_t, sin_t, BF16, "mla_qrope", lead=D_MODEL)
    kv_all = _mm_nn(c_kv, W["mla_w_ukv"], BF16, "mla_ukv")
    kr_in = jnp.concatenate([kr_raw, kr_raw, jnp.zeros((M, 64), F32)], axis=1)
    kr = _rope(kr_in, cos_t, sin_t, BF16, "mla_krope")
    q_rope = q_all[:, D_MODEL:]
    mla_o, mla_lse = _mla_fwd(q_all, kv_all, q_rope, kr, mla_scale, "mla_fwd")
    h1_2 = _mm_nn(mla_o, W["mla_w_o"], F32, "mla_out", res=h_2)
    h_3, ffn2 = ffn_fwd(h1_2, 2)

    fox_scale = HEAD_DIM ** -0.5
    a3 = _norm_fwd(h_3, gain("norm_mix", 3), BF16, "mix3_norm")
    fox_qkv = _mm_nn(a3, W["fox_w_qkv"], BF16, "fox_qkv")
    f_logit = _mm_nn(a3, W["fox_w_f"], F32, "fox_f")
    b_f = jnp.pad(P["fox_b_f"], ((0, 0), (0, 128 - N_HEADS)))
    Fc = _forget_cumsum(f_logit, b_f, "fox_cumsum")
    f_rows, f_cols = _pairs_row(Fc[:, :N_HEADS]), _pairs_col(Fc[:, :N_HEADS])
    fox_qkv_t = fox_qkv.T
    fox_o_t, fox_lse, fox_ox_t = _fox_fwd(fox_qkv, fox_qkv_t, f_rows, f_cols, fox_scale, "fox_fwd")
    fox_o = fox_o_t.T
    h1_3 = _mm_nn(fox_o, W["fox_w_o"], F32, "fox_out", res=h_3)
    h_4, ffn3 = ffn_fwd(h1_3, 3)

    sq, dh, dgain = _loss_head(h_4, P["final_norm"][None, :], target, "loss_head")
    G["final_norm"] = dgain[0]

    dh = ffn_bwd(dh, ffn3, 3)
    do = _mm_nt(dh, W["fox_w_o"], BF16, "fox_do")
    G["fox_w_o"] = _mm_tn(fox_o, dh, "fox_dwo")
    dq_t, dk, dv, colsum = _fox_bwd(fox_qkv, fox_qkv_t, fox_ox_t, do, do.T, fox_lse, f_rows, f_cols, fox_scale,
                                    "fox_bwd")
    dlogit, db_f = _forget_cumsum_bwd(f_logit, b_f, colsum, "fox_dcumsum")
    G["fox_b_f"] = db_f[:, :N_HEADS]
    dproj = jnp.concatenate([dq_t.T, dk.astype(BF16), dv.astype(BF16), dlogit.astype(BF16)], axis=1)
    da = _mm_nt(dproj, W["fox_w_qkvf"], F32, "fox_da")
    G["fox_w_qkvf"] = _mm_tn(a3, dproj, "fox_dwqkvf")
    dh, dgain = _norm_bwd(h_3, gain("norm_mix", 3), da, dh, "mix3_dnorm")
    G.setdefault("norm_mix", {})[3] = dgain

    dh = ffn_bwd(dh, ffn2, 2)
    do = _mm_nt(dh, W["mla_w_o"], BF16, "mla_do")
    G["mla_w_o"] = _mm_tn(mla_o, dh, "mla_dwo")
    dq, dk, dv, dqr, dkr = _mla_bwd(q_all, kv_all, q_rope, kr, mla_o, do, mla_lse, mla_scale, "mla_bwd")
    dqr = _rope(dqr, cos_t, sin_t, BF16, "mla_dqrope", inverse=True)
    dq_all = jnp.concatenate([dq, dqr], axis=1)
    dkr_sum = _rope(jnp.sum(dkr, axis=0), cos_t, sin_t, F32, "mla_dkrope", inverse=True)
    dkr_raw = dkr_sum[:, :MLA_ROPE] + dkr_sum[:, MLA_ROPE:2 * MLA_ROPE]
    dkv_all = jnp.concatenate([dk.astype(BF16), dv.astype(BF16)], axis=1)
    dc_q = _mm_nt(dq_all, W["mla_w_uq"], F32, "mla_dcq")
    G["mla_w_uq"] = _mm_tn(c_q, dq_all, "mla_dwuq")
    dc_kv = _mm_nt(dkv_all, W["mla_w_ukv"], F32, "mla_dckv")
    G["mla_w_ukv"] = _mm_tn(c_kv, dkv_all, "mla_dwukv")
    ddq_raw, G["mla_q_norm"] = _norm_bwd(dq_raw, P["mla_q_norm"], dc_q, None, "mla_dqnorm")
    ddkv_raw, G["mla_kv_norm"] = _norm_bwd(dkv_raw, P["mla_kv_norm"], dc_kv, None, "mla_dkvnorm")
    ddown = jnp.concatenate([ddq_raw, ddkv_raw, dkr_raw], axis=1).astype(BF16)
    da = _mm_nt(ddown, W["mla_w_down"], F32, "mla_da")
    G["mla_w_down"] = _mm_tn(a2, ddown, "mla_dwdown")
    dh, dgain = _norm_bwd(h_2, gain("norm_mix", 2), da, dh, "mix2_dnorm")
    G["norm_mix"][2] = dgain

    dh = ffn_bwd(dh, ffn1, 1)
    do = _mm_nt(dh, W["sb_w_o"], BF16, "sb_do")
    G["sb_w_o"] = _mm_tn(sb_o, dh, "sb_dwo")
    dq, dk, dv = _sb_bwd(sb_qkv, do, sb_tot, sb_scale, "sb_bwd")
    dqkv = jnp.concatenate([dq, dk.astype(BF16), dv.astype(BF16)], axis=1)
    da = _mm_nt(dqkv, W["sb_w_qkv"], F32, "sb_da")
    G["sb_w_qkv"] = _mm_tn(a1, dqkv, "sb_dwqkv")
    dh, dgain = _norm_bwd(h_1, gain("norm_mix", 1), da, dh, "mix1_dnorm")
    G["norm_mix"][1] = dgain

    dh = ffn_bwd(dh, ffn0, 0)
    dpc, G["pool_w"], G["pool_scale"] = _pool_bwd_mix(dh, pooled, W["pool_w"], P["pool_scale"], "pool_dmix")
    da = _pool_bwd_window(dpc, "pool_dwindow")
    dh, dgain, dx = _norm_bwd(h0, gain("norm_mix", 0), da, dh, "mix0_dnorm", token_rows=True)
    G["norm_mix"][0] = dgain

    G["norm_mix"] = jnp.concatenate([G["norm_mix"][i] for i in range(DEPTH)], axis=0)
    G["norm_ffn"] = jnp.concatenate([G["norm_ffn"][i] for i in range(DEPTH)], axis=0)
    G["ffn_w_down"] = jnp.stack([G["ffn_w_down"][i] for i in range(DEPTH)])
    G["ffn_w_gate"] = jnp.stack([G["ffn_w_gate"][i] for i in range(DEPTH)])
    G["ffn_w_up"] = jnp.stack([G["ffn_w_up"][i] for i in range(DEPTH)])
    G["meta"] = dh[PAD:ROW0]
    return sq, dx, G


def kernel(x, meta, norm_mix, norm_ffn, pool_w, pool_scale, sb_w_qkv, sb_w_o, mla_w_down, mla_q_norm, mla_kv_norm, mla_w_uq, mla_w_ukv, mla_w_o, fox_w_qkvf, fox_b_f, fox_w_o, ffn_w_gate, ffn_w_up, ffn_w_down, final_norm, loss_target, m_meta, m_norm_mix, m_norm_ffn, m_pool_w, m_pool_scale, m_sb_w_qkv, m_sb_w_o, m_mla_w_down, m_mla_q_norm, m_mla_kv_norm, m_mla_w_uq, m_mla_w_ukv, m_mla_w_o, m_fox_w_qkvf, m_fox_b_f, m_fox_w_o, m_ffn_w_gate, m_ffn_w_up, m_ffn_w_down, m_final_norm, v_meta, v_norm_mix, v_norm_ffn, v_pool_w, v_pool_scale, v_sb_w_qkv, v_sb_w_o, v_mla_w_down, v_mla_q_norm, v_mla_kv_norm, v_mla_w_uq, v_mla_w_ukv, v_mla_w_o, v_fox_w_qkvf, v_fox_b_f, v_fox_w_o, v_ffn_w_gate, v_ffn_w_up, v_ffn_w_down, v_final_norm):
    w = dict(meta=meta, norm_mix=norm_mix, norm_ffn=norm_ffn, pool_w=pool_w, pool_scale=pool_scale,
             sb_w_qkv=sb_w_qkv, sb_w_o=sb_w_o, mla_w_down=mla_w_down, mla_q_norm=mla_q_norm,
             mla_kv_norm=mla_kv_norm, mla_w_uq=mla_w_uq, mla_w_ukv=mla_w_ukv, mla_w_o=mla_w_o,
             fox_w_qkvf=fox_w_qkvf, fox_b_f=fox_b_f, fox_w_o=fox_w_o, ffn_w_gate=ffn_w_gate, ffn_w_up=ffn_w_up,
             ffn_w_down=ffn_w_down, final_norm=final_norm)
    m = dict(meta=m_meta, norm_mix=m_norm_mix, norm_ffn=m_norm_ffn, pool_w=m_pool_w, pool_scale=m_pool_scale,
             sb_w_qkv=m_sb_w_qkv, sb_w_o=m_sb_w_o, mla_w_down=m_mla_w_down, mla_q_norm=m_mla_q_norm,
             mla_kv_norm=m_mla_kv_norm, mla_w_uq=m_mla_w_uq, mla_w_ukv=m_mla_w_ukv, mla_w_o=m_mla_w_o,
             fox_w_qkvf=m_fox_w_qkvf, fox_b_f=m_fox_b_f, fox_w_o=m_fox_w_o, ffn_w_gate=m_ffn_w_gate,
             ffn_w_up=m_ffn_w_up, ffn_w_down=m_ffn_w_down, final_norm=m_final_norm)
    v = dict(meta=v_meta, norm_mix=v_norm_mix, norm_ffn=v_norm_ffn, pool_w=v_pool_w, pool_scale=v_pool_scale,
             sb_w_qkv=v_sb_w_qkv, sb_w_o=v_sb_w_o, mla_w_down=v_mla_w_down, mla_q_norm=v_mla_q_norm,
             mla_kv_norm=v_mla_kv_norm, mla_w_uq=v_mla_w_uq, mla_w_ukv=v_mla_w_ukv, mla_w_o=v_mla_w_o,
             fox_w_qkvf=v_fox_w_qkvf, fox_b_f=v_fox_b_f, fox_w_o=v_fox_w_o, ffn_w_gate=v_ffn_w_gate,
             ffn_w_up=v_ffn_w_up, ffn_w_down=v_ffn_w_down, final_norm=v_final_norm)

    sh_names = tuple(n for n, _ in SHARDED)
    sh_axis = dict(SHARDED)
    shapes = {n: w[n].shape for n in WEIGHT_NAMES}
    wire = lambda n: F32 if n in KEPT_F32 else BF16

    gathered = _all_gather([w[n].astype(wire(n)) for n in sh_names], "gather_weights")
    full = {n: _whole_from_gathered(g, sh_axis[n]) for n, g in zip(sh_names, gathered)}
    W = _kernel_weights(full)
    P = dict(meta=full["meta"], mla_q_norm=full["mla_q_norm"], mla_kv_norm=full["mla_kv_norm"],
             norm_mix=norm_mix, norm_ffn=norm_ffn, pool_scale=pool_scale, fox_b_f=fox_b_f, final_norm=final_norm)

    sq, dx, G = _local_step(x[0], loss_target[0], W, P)
    loss = lax.psum(0.5 * jnp.sum(sq) / D_MODEL, ("x", "y", "c"))
    grad_x = dx[None]

    gw = _reference_grads(G)
    gw["meta"] = G["meta"]
    rc = {n: (int(np.prod(shapes[n][:-1])), shapes[n][-1]) for n in sh_names}
    parts = [_parts_from_whole(gw[n], sh_axis[n]).astype(wire(n)).reshape((N_DEV,) + rc[n]) for n in sh_names]
    from_sibling = _exchange_siblings(parts, "exchange_grads_d2d")
    sums = [_pair_sum(a, b, f"pair_sum_{n}") for n, a, b in zip(sh_names, parts, from_sibling)]
    landed = _exchange_chips(sums, "exchange_grads_ici")
    results = {}
    for n, got in zip(sh_names, landed):
        outs = _adamw(w[n].reshape(rc[n]), got, m[n].reshape(rc[n]), v[n].reshape(rc[n]), f"adamw_{n}")
        results[n] = [o.reshape(shapes[n]) for o in outs]

    rep_g = dict(norm_mix=G["norm_mix"], norm_ffn=G["norm_ffn"], pool_scale=G["pool_scale"], fox_b_f=G["fox_b_f"],
                 final_norm=G["final_norm"])
    (rep_all,) = _all_gather([_pack_rows(rep_g, REPLICATED)], "gather_replicated_grads")
    rep_out = _adamw(_pack_rows(w, REPLICATED), rep_all, _pack_rows(m, REPLICATED), _pack_rows(v, REPLICATED),
                     "adamw_replicated")
    rep = [_unpack_rows(o, shapes, REPLICATED) for o in rep_out]
    for n in REPLICATED:
        results[n] = [r[n] for r in rep]

    outs = [results[n][k] for k in range(4) for n in WEIGHT_NAMES]
    return (loss, grad_x, *outs)
```

```python
import numpy as np
import jax
import jax.numpy as jnp
from jax import lax
from jax.experimental import pallas as pl
from jax.experimental.pallas import tpu as pltpu

F32 = jnp.float32
BF16 = jnp.bfloat16

N_DEV = 8
D_MODEL = 1024
N_META = 16
PAD = 240
ROW0 = PAD + N_META
EPS = 1e-6
POOL_WINDOWS = (2, 4, 8, 16)
POOL_GROUP = 256
HALO = 128
N_HEADS = 16
HEAD_DIM = 64
N_PAIRS = N_HEADS // 2
MLA_Q_RANK = 384
MLA_KV_RANK = 256
MLA_NOPE = 64
MLA_ROPE = 32
ROPE_THETA = 10000.0
D_FF = 2816
DEPTH = 4
ATTN_TILE = 768
ATTN_BWD_TILE = 768
WALK_TILE = 256
FOX_TILE = 384
NEG = -1e30
LOG2E = 1.4426950408889634
EXP_ZERO = -110.0
VMEM_LIMIT = 56 * 2**20
ADAM_TILE_ELEMS = 192 * 1024

ADAM_LR = 0.001
ADAM_B1 = 0.9
ADAM_B2 = 0.999
ADAM_EPS = 1e-08
ADAM_WD = 0.01
ADAM_STEP = 10

MESH = pl.DeviceIdType.MESH


def _params(sem=None):
    return pltpu.CompilerParams(dimension_semantics=sem, vmem_limit_bytes=VMEM_LIMIT)


def _pick(n, cands):
    for c in cands:
        if n % c == 0:
            return c
    return n


def _col_tile(n, cap=1536):
    best = None
    for t in range(128, min(n, cap) + 1, 128):
        if n % t == 0:
            best = t
    return best if best is not None else n


def _dot(a, b):
    return jnp.dot(a, b, preferred_element_type=F32)


def _dot_nt(a, b):
    return lax.dot_general(a, b, (((1,), (1,)), ((), ())), preferred_element_type=F32)


def _dot_tn(a, b):
    return lax.dot_general(a, b, (((0,), (0,)), ((), ())), preferred_element_type=F32)


def _mm_nn(a, b, out_dtype, name, res=None):
    M, K = a.shape
    N = b.shape[1]
    tm = _pick(M, (768, 512, 256, 128))
    tn = _col_tile(N)

    def body(*refs):
        if res is None:
            a_ref, b_ref, o_ref = refs
        else:
            a_ref, b_ref, r_ref, o_ref = refs
        acc = _dot(a_ref[...].astype(BF16), b_ref[...])
        if res is not None:
            acc = acc + r_ref[...]
        o_ref[...] = acc.astype(o_ref.dtype)

    in_specs = [pl.BlockSpec((tm, K), lambda n, m: (m, 0)), pl.BlockSpec((K, tn), lambda n, m: (0, n))]
    args = [a, b]
    if res is not None:
        in_specs.append(pl.BlockSpec((tm, tn), lambda n, m: (m, n)))
        args.append(res)
    return pl.pallas_call(
        body, name=name, grid=(N // tn, M // tm), in_specs=in_specs,
        out_specs=pl.BlockSpec((tm, tn), lambda n, m: (m, n)),
        out_shape=jax.ShapeDtypeStruct((M, N), out_dtype),
        compiler_params=_params(("parallel", "parallel")))(*args)


def _mm_nt(a, w, out_dtype, name):
    M, N = a.shape
    K = w.shape[0]
    tm = _pick(M, (768, 512, 256, 128)) if N <= 3200 else _pick(M, (256, 128))
    tk = _col_tile(K, 1024)

    def body(a_ref, w_ref, o_ref):
        o_ref[...] = _dot_nt(a_ref[...].astype(BF16), w_ref[...]).astype(o_ref.dtype)

    return pl.pallas_call(
        body, name=name, grid=(K // tk, M // tm),
        in_specs=[pl.BlockSpec((tm, N), lambda k, m: (m, 0)), pl.BlockSpec((tk, N), lambda k, m: (k, 0))],
        out_specs=pl.BlockSpec((tm, tk), lambda k, m: (m, k)),
        out_shape=jax.ShapeDtypeStruct((M, K), out_dtype),
        compiler_params=_params(("parallel", "parallel")))(a, w)


def _mm_nt2(a1, w1, a2, w2, out_dtype, name):
    M, N = a1.shape
    K = w1.shape[0]
    tm = _pick(M, (384, 256, 128))

    def body(a1_ref, w1_ref, a2_ref, w2_ref, o_ref):
        o_ref[...] = (_dot_nt(a1_ref[...], w1_ref[...]) + _dot_nt(a2_ref[...], w2_ref[...])).astype(o_ref.dtype)

    a_spec = pl.BlockSpec((tm, N), lambda m: (m, 0))
    w_spec = pl.BlockSpec((K, N), lambda m: (0, 0))
    return pl.pallas_call(
        body, name=name, grid=(M // tm,), in_specs=[a_spec, w_spec, a_spec, w_spec],
        out_specs=pl.BlockSpec((tm, K), lambda m: (m, 0)),
        out_shape=jax.ShapeDtypeStruct((M, K), out_dtype),
        compiler_params=_params(("parallel",)))(a1, w1, a2, w2)


def _mm_tn(a, b, name):
    M, K = a.shape
    N = b.shape[1]
    tm = _pick(M, (768, 512, 256, 128))
    tk = _col_tile(K, 1408)
    tn = _col_tile(N, 1408)

    def body(a_ref, b_ref, o_ref):
        @pl.when(pl.program_id(2) == 0)
        def _():
            o_ref[...] = jnp.zeros_like(o_ref)
        o_ref[...] += _dot_tn(a_ref[...].astype(BF16), b_ref[...].astype(BF16))

    return pl.pallas_call(
        body, name=name, grid=(K // tk, N // tn, M // tm),
        in_specs=[pl.BlockSpec((tm, tk), lambda k, n, m: (m, k)), pl.BlockSpec((tm, tn), lambda k, n, m: (m, n))],
        out_specs=pl.BlockSpec((tk, tn), lambda k, n, m: (k, n)),
        out_shape=jax.ShapeDtypeStruct((K, N), F32),
        compiler_params=_params(("parallel", "parallel", "arbitrary")))(a, b)


def _norm_fwd(h, gain, out_dtype, name):
    M, C = h.shape
    tm = _pick(M, (768, 512, 256, 128))

    def body(h_ref, g_ref, a_ref):
        x = h_ref[...]
        r = lax.rsqrt(jnp.mean(x * x, axis=-1, keepdims=True) + EPS)
        a_ref[...] = ((x * r) * g_ref[...]).astype(a_ref.dtype)

    return pl.pallas_call(
        body, name=name, grid=(M // tm,),
        in_specs=[pl.BlockSpec((tm, C), lambda m: (m, 0)), pl.BlockSpec((1, C), lambda m: (0, 0))],
        out_specs=pl.BlockSpec((tm, C), lambda m: (m, 0)),
        out_shape=jax.ShapeDtypeStruct((M, C), out_dtype),
        compiler_params=_params(("parallel",)))(h, gain)


def _norm_bwd(h, gain, da, dres, name, token_rows=False):
    M, C = h.shape
    tm = ROW0 if token_rows else _pick(M, (768, 512, 256, 128))

    def body(*refs):
        refs = list(refs)
        dx_ref = refs.pop() if token_rows else None
        if dres is None:
            h_ref, g_ref, da_ref, dh_ref, dg_ref = refs
        else:
            h_ref, g_ref, da_ref, dr_ref, dh_ref, dg_ref = refs
        x = h_ref[...]
        r = lax.rsqrt(jnp.mean(x * x, axis=-1, keepdims=True) + EPS)
        y = x * r
        dav = da_ref[...].astype(F32)
        dy = dav * g_ref[...]
        dh = r * (dy - y * jnp.mean(dy * y, axis=-1, keepdims=True))
        if dres is not None:
            dh = dh + dr_ref[...]
        dh_ref[...] = dh
        if token_rows:
            dx_ref[...] = dh

        @pl.when(pl.program_id(0) == 0)
        def _():
            dg_ref[...] = jnp.zeros_like(dg_ref)
        dg_ref[...] += jnp.sum(dav * y, axis=0, keepdims=True)

    row = pl.BlockSpec((tm, C), lambda m: (m, 0))
    vec = pl.BlockSpec((1, C), lambda m: (0, 0))
    in_specs = [row, vec, row] + ([row] if dres is not None else [])
    args = [h, gain, da] + ([dres] if dres is not None else [])
    out_specs = [row, vec]
    out_shape = [jax.ShapeDtypeStruct((M, C), F32), jax.ShapeDtypeStruct((1, C), F32)]
    if token_rows:
        out_specs.append(pl.BlockSpec((tm, C), lambda m: (jnp.maximum(m - 1, 0), 0)))
        out_shape.append(jax.ShapeDtypeStruct((M - ROW0, C), F32))
    return pl.pallas_call(
        body, name=name, grid=(M // tm,), in_specs=in_specs, out_specs=out_specs, out_shape=out_shape,
        compiler_params=_params(("arbitrary",)))(*args)


def _ffn_up(b, w_g, w_u, name):
    M, K = b.shape
    F = w_g.shape[1]
    tm = _pick(M, (768, 512, 256, 128))
    tn = _col_tile(F, 1408)
    nb = F // tn

    def body(b_ref, wg_ref, wu_ref, g_ref, u_ref, act_ref):
        x = b_ref[...]
        g = _dot(x, wg_ref[...])
        u = _dot(x, wu_ref[...])
        g_ref[...] = g.astype(g_ref.dtype)
        u_ref[...] = u.astype(u_ref.dtype)
        act_ref[...] = ((g * jax.nn.sigmoid(g)) * u).astype(act_ref.dtype)

    blk = pl.BlockSpec((tm, tn), lambda n, m: (m, n))
    return pl.pallas_call(
        body, name=name, grid=(nb, M // tm),
        in_specs=[pl.BlockSpec((tm, K), lambda n, m: (m, 0)),
                  pl.BlockSpec((K, tn), lambda n, m: (0, n)),
                  pl.BlockSpec((K, tn), lambda n, m: (0, n))],
        out_specs=[blk, blk, blk],
        out_shape=[jax.ShapeDtypeStruct((M, F), BF16), jax.ShapeDtypeStruct((M, F), BF16),
                   jax.ShapeDtypeStruct((M, F), BF16)],
        compiler_params=_params(("parallel", "parallel")))(b, w_g, w_u)


def _ffn_dact(dy, w_d, g, u, name):
    M, K = dy.shape
    F = w_d.shape[0]
    tm = _pick(M, (768, 512, 256, 128))
    tn = _col_tile(F, 1408)
    nb = F // tn

    def body(dy_ref, wd_ref, g_ref, u_ref, dg_ref, du_ref):
        dact = _dot_nt(dy_ref[...].astype(BF16), wd_ref[...])
        gv = g_ref[...].astype(F32)
        s = jax.nn.sigmoid(gv)
        silu = gv * s
        dg_ref[...] = (dact * u_ref[...].astype(F32) * (s * (1.0 + gv * (1.0 - s)))).astype(dg_ref.dtype)
        du_ref[...] = (dact * silu).astype(du_ref.dtype)

    blk = pl.BlockSpec((tm, tn), lambda n, m: (m, n))
    return pl.pallas_call(
        body, name=name, grid=(nb, M // tm),
        in_specs=[pl.BlockSpec((tm, K), lambda n, m: (m, 0)), pl.BlockSpec((tn, K), lambda n, m: (n, 0)), blk, blk],
        out_specs=[blk, blk],
        out_shape=[jax.ShapeDtypeStruct((M, F), BF16), jax.ShapeDtypeStruct((M, F), BF16)],
        compiler_params=_params(("parallel", "parallel")))(dy, w_d, g, u)


def _loss_head(h, gain, target, name):
    M, C = h.shape
    tm = ROW0
    assert M % tm == 0 and target.shape[0] == M - ROW0

    def body(h_ref, g_ref, t_ref, sq_ref, dh_ref, dg_ref):
        i = pl.program_id(0)

        @pl.when(i == 0)
        def _():
            sq_ref[...] = jnp.zeros_like(sq_ref)
            dg_ref[...] = jnp.zeros_like(dg_ref)
            dh_ref[...] = jnp.zeros_like(dh_ref)

        @pl.when(i > 0)
        def _():
            x = h_ref[...]
            r = lax.rsqrt(jnp.mean(x * x, axis=-1, keepdims=True) + EPS)
            y = x * r
            err = y * g_ref[...] - t_ref[...]
            sq_ref[...] += jnp.sum(err * err, axis=0, keepdims=True)
            da = err * (1.0 / C)
            dy = da * g_ref[...]
            dh_ref[...] = r * (dy - y * jnp.mean(dy * y, axis=-1, keepdims=True))
            dg_ref[...] += jnp.sum(da * y, axis=0, keepdims=True)

    row = pl.BlockSpec((tm, C), lambda m: (m, 0))
    vec = pl.BlockSpec((1, C), lambda m: (0, 0))
    return pl.pallas_call(
        body, name=name, grid=(M // tm,),
        in_specs=[row, vec, pl.BlockSpec((tm, C), lambda m: (jnp.maximum(m - 1, 0), 0))],
        out_specs=[vec, row, vec],
        out_shape=[jax.ShapeDtypeStruct((1, C), F32), jax.ShapeDtypeStruct((M, C), F32),
                   jax.ShapeDtypeStruct((1, C), F32)],
        compiler_params=_params(("arbitrary",)))(h, gain, target)


def _band_dot(band, x):
    hi = x.astype(BF16)
    rest = x - hi.astype(F32)
    mid = rest.astype(BF16)
    lo = (rest - mid.astype(F32)).astype(BF16)
    return _dot(band, hi) + _dot(band, mid) + _dot(band, lo)


def _pool_pos(row0, tm):
    return row0 + lax.broadcasted_iota(jnp.int32, (tm, 1), 0) - PAD


def _pool_fwd(h, a, w, scale, name):
    M, C = a.shape
    tm = 256
    hb = tm // HALO

    def body(h_ref, a_ref, halo_ref, w_ref, s_ref, o_ref, p_ref):
        i = pl.program_id(0)
        row0 = i * tm
        ext = jnp.concatenate([halo_ref[...], a_ref[...]], axis=0)
        src = row0 - HALO + lax.broadcasted_iota(jnp.int32, (tm + HALO, 1), 0)
        ext = jnp.where(src >= PAD, ext, 0.0)
        r = lax.broadcasted_iota(jnp.int32, (tm, tm + HALO), 0)
        c = lax.broadcasted_iota(jnp.int32, (tm, tm + HALO), 1)
        pos = _pool_pos(row0, tm)
        for g, win in enumerate(POOL_WINDOWS):
            band = ((c <= r + HALO) & (c > r + HALO - win)).astype(BF16)
            cols = slice(g * POOL_GROUP, (g + 1) * POOL_GROUP)
            xg = ext[:, cols]
            tot = _band_dot(band, xg)
            cnt = jnp.clip(pos + 1, 1, win).astype(F32)
            pooled = (tot / cnt - xg[HALO:]).astype(BF16)
            p_ref[:, cols] = pooled
            mixed = _dot(pooled, w_ref[g])
            o_ref[:, cols] = h_ref[:, cols] + mixed * s_ref[:, cols]

    row = pl.BlockSpec((tm, C), lambda m: (m, 0))
    return pl.pallas_call(
        body, name=name, grid=(M // tm,),
        in_specs=[row, row, pl.BlockSpec((HALO, C), lambda m: (jnp.maximum(m * hb - 1, 0), 0)),
                  pl.BlockSpec((4, POOL_GROUP, POOL_GROUP), lambda m: (0, 0, 0)),
                  pl.BlockSpec((1, C), lambda m: (0, 0))],
        out_specs=[row, row],
        out_shape=[jax.ShapeDtypeStruct((M, C), F32), jax.ShapeDtypeStruct((M, C), BF16)],
        compiler_params=_params(("parallel",)))(h, a, a, w, scale)


def _pool_bwd_mix(dout, pooled, w, scale, name):
    M, C = dout.shape
    tm = 256

    def body(do_ref, p_ref, w_ref, s_ref, dpc_ref, dw_ref, ds_ref):
        i = pl.program_id(0)

        @pl.when(i == 0)
        def _():
            dw_ref[...] = jnp.zeros_like(dw_ref)
            ds_ref[...] = jnp.zeros_like(ds_ref)

        pos = _pool_pos(i * tm, tm)
        for g, win in enumerate(POOL_WINDOWS):
            cols = slice(g * POOL_GROUP, (g + 1) * POOL_GROUP)
            do = do_ref[:, cols]
            pooled = p_ref[:, cols]
            mixed = _dot(pooled, w_ref[g])
            ds_ref[:, cols] += jnp.sum(do * mixed, axis=0, keepdims=True)
            dmix = (do * s_ref[:, cols]).astype(BF16)
            dw_ref[g] += _dot_tn(pooled, dmix)
            dp = _dot_nt(dmix, w_ref[g])
            cnt = jnp.clip(pos + 1, 1, win).astype(F32)
            dpc_ref[:, cols] = dp / cnt

    row = pl.BlockSpec((tm, C), lambda m: (m, 0))
    wspec = pl.BlockSpec((4, POOL_GROUP, POOL_GROUP), lambda m: (0, 0, 0))
    vec = pl.BlockSpec((1, C), lambda m: (0, 0))
    return pl.pallas_call(
        body, name=name, grid=(M // tm,),
        in_specs=[row, row, wspec, vec], out_specs=[row, wspec, vec],
        out_shape=[jax.ShapeDtypeStruct((M, C), F32), jax.ShapeDtypeStruct((4, POOL_GROUP, POOL_GROUP), F32),
                   jax.ShapeDtypeStruct((1, C), F32)],
        compiler_params=_params(("arbitrary",)))(dout, pooled, w, scale)


def _pool_bwd_window(dpc, name):
    M, C = dpc.shape
    tm = 256
    hb = tm // HALO
    last = M // HALO - 1

    def body(d_ref, halo_ref, da_ref):
        i = pl.program_id(0)
        row0 = i * tm
        ext = jnp.concatenate([d_ref[...], halo_ref[...]], axis=0)
        src = row0 + lax.broadcasted_iota(jnp.int32, (tm + HALO, 1), 0)
        ext = jnp.where(src < M, ext, 0.0)
        r = lax.broadcasted_iota(jnp.int32, (tm, tm + HALO), 0)
        c = lax.broadcasted_iota(jnp.int32, (tm, tm + HALO), 1)
        pos = _pool_pos(row0, tm)
        for g, win in enumerate(POOL_WINDOWS):
            band = ((c >= r) & (c < r + win)).astype(BF16)
            cols = slice(g * POOL_GROUP, (g + 1) * POOL_GROUP)
            xg = ext[:, cols]
            tot = _band_dot(band, xg)
            cnt = jnp.clip(pos + 1, 1, win).astype(F32)
            da_ref[:, cols] = jnp.where(pos >= 0, tot - xg[:tm] * cnt, 0.0)

    row = pl.BlockSpec((tm, C), lambda m: (m, 0))
    return pl.pallas_call(
        body, name=name, grid=(M // tm,),
        in_specs=[row, pl.BlockSpec((HALO, C), lambda m: (jnp.minimum((m + 1) * hb, last), 0))],
        out_specs=row, out_shape=jax.ShapeDtypeStruct((M, C), F32),
        compiler_params=_params(("parallel",)))(dpc, dpc)


def _head_masks():
    lane = lax.broadcasted_iota(jnp.int32, (1, 128), 1)
    return lane < HEAD_DIM, lane


def _split_heads(x, first):
    z = jnp.zeros_like(x)
    return jnp.where(first, x, z), jnp.where(first, z, x)


def _split_rope(x, lane):
    z = jnp.zeros_like(x)
    return jnp.where(lane < MLA_ROPE, x, z), jnp.where((lane >= MLA_ROPE) & (lane < 2 * MLA_ROPE), x, z)


def _walk_causal(i, step):
    def mid(kb, carry):
        step(kb, False)
        return carry

    step(0, True)
    lax.fori_loop(1, i, mid, 0)

    @pl.when(i > 0)
    def _():
        step(i, True)


def _mla_fwd(q_all, kv_all, qr, kr, scale, name):
    M = q_all.shape[0]
    t = ATTN_TILE

    def body(q_ref, k_ref, v_ref, qr_ref, kr_ref, o_ref, lse_ref, m_s, l_s, acc_s, kmax_s):
        i = pl.program_id(1)
        first, lane = _head_masks()

        @pl.when(i == 0)
        def _():
            def block_max(kb, carry):
                rows = pl.ds(pl.multiple_of(kb * t, t), t)
                kk = k_ref[rows, :].astype(F32)
                kk = kk * kk
                rr = kr_ref[rows, :].astype(F32)
                rr = jnp.sum(jnp.where(lane < MLA_ROPE, rr * rr, 0.0), axis=1, keepdims=True)
                a = jnp.max(jnp.sum(jnp.where(first, kk, 0.0), axis=1, keepdims=True) + rr)
                b = jnp.max(jnp.sum(jnp.where(first, 0.0, kk), axis=1, keepdims=True) + rr)
                return jnp.maximum(carry[0], a), jnp.maximum(carry[1], b)

            a, b = lax.fori_loop(0, M // t, block_max, (jnp.float32(0.0), jnp.float32(0.0)))
            kmax_s[0] = a
            kmax_s[1] = b

        qs = _split_heads(q_ref[...], first)
        qrs = _split_rope(qr_ref[...], lane)
        qcat = tuple(jnp.concatenate([qs[hh], qrs[hh]], axis=1) for hh in range(2))
        qpos = i * t + lax.broadcasted_iota(jnp.int32, (t, t), 0)
        kidx = lax.broadcasted_iota(jnp.int32, (t, t), 1)
        c2 = scale * LOG2E

        def run(online):
            l_s[...] = jnp.zeros_like(l_s)
            acc_s[...] = jnp.zeros_like(acc_s)
            if online:
                m_s[...] = jnp.full_like(m_s, NEG)

            def step(kb, masked):
                k0 = pl.multiple_of(kb * t, t)
                kcat = jnp.concatenate([k_ref[pl.ds(k0, t), :], kr_ref[pl.ds(k0, t), :]], axis=1)
                vs = _split_heads(v_ref[pl.ds(k0, t), :], first)
                if masked:
                    kpos = k0 + kidx
                    valid = (kpos <= qpos) & (kpos >= PAD)
                pv = None
                alphas = []
                for hh in range(2):
                    s = _dot_nt(qcat[hh], kcat)
                    if online:
                        if masked:
                            s = jnp.where(valid, s, NEG)
                        m_old = m_s[hh]
                        m_new = jnp.maximum(m_old, jnp.max(s, axis=1, keepdims=True))
                        p = jnp.exp2((s - m_new) * c2)
                        alpha = jnp.exp2((m_old - m_new) * c2)
                        l_s[hh] = alpha * l_s[hh] + jnp.sum(p, axis=1, keepdims=True)
                        m_s[hh] = m_new
                        alphas.append(alpha)
                    else:
                        p = jnp.exp2(s * c2 - m_s[hh])
                        if masked:
                            p = jnp.where(valid, p, 0.0)
                        l_s[hh] = l_s[hh] + jnp.sum(p, axis=1, keepdims=True)
                    d = _dot(p.astype(BF16), vs[hh])
                    pv = d if pv is None else pv + d
                if online:
                    acc_s[...] = acc_s[...] * jnp.where(first, alphas[0], alphas[1]) + pv
                else:
                    acc_s[...] += pv

            _walk_causal(i, step)

        for hh in range(2):
            qf = qcat[hh].astype(F32)
            m_s[hh] = (1.001 * c2) * jnp.sqrt(jnp.sum(qf * qf, axis=1, keepdims=True) * kmax_s[hh])
        run(False)
        real = i * t + lax.broadcasted_iota(jnp.int32, (t, 1), 0) >= PAD
        underflow = jnp.max(jnp.where(real & (jnp.minimum(l_s[0], l_s[1]) < 1e-30), 1.0, 0.0)) > 0.0

        @pl.when(underflow)
        def _():
            run(True)
            m_s[...] = m_s[...] * c2

        ls = tuple(jnp.where(l_s[hh] > 0.0, l_s[hh], 1.0) for hh in range(2))
        o_ref[...] = (acc_s[...] * jnp.where(first, 1.0 / ls[0], 1.0 / ls[1])).astype(o_ref.dtype)
        lse_ref[:, 0:1] = m_s[0] * (1.0 / LOG2E) + jnp.log(ls[0])
        lse_ref[:, 1:2] = m_s[1] * (1.0 / LOG2E) + jnp.log(ls[1])

    blk = pl.BlockSpec((t, 128), lambda j, i: (i, j))
    return pl.pallas_call(
        body, name=name, grid=(N_PAIRS, M // t),
        in_specs=[blk, pl.BlockSpec((M, 128), lambda j, i: (0, j)), pl.BlockSpec((M, 128), lambda j, i: (0, N_PAIRS + j)),
                  blk, pl.BlockSpec((M, 128), lambda j, i: (0, 0))],
        out_specs=[blk, pl.BlockSpec((None, t, 2), lambda j, i: (j, i, 0))],
        out_shape=[jax.ShapeDtypeStruct((M, N_PAIRS * 128), BF16), jax.ShapeDtypeStruct((N_PAIRS, M, 2), F32)],
        scratch_shapes=[pltpu.VMEM((2, t, 1), F32), pltpu.VMEM((2, t, 1), F32), pltpu.VMEM((t, 128), F32),
                        pltpu.SMEM((2,), F32)],
        compiler_params=_params(("arbitrary", "arbitrary")))(q_all, kv_all, kv_all, qr, kr)


def _mla_bwd(q_all, kv_all, qr, kr, o, do, lse, scale, name):
    M = q_all.shape[0]
    t = ATTN_BWD_TILE

    def body(q_ref, kv_hbm, qr_ref, kr_hbm, o_ref, do_ref, lse_ref,
             dq_ref, dk_hbm, dv_hbm, dqr_ref, dkr_hbm,
             k_ref, v_ref, kr_ref, dk_ref, dv_ref, dkr_ref, dq_s, lse_s, delta_s):
        j = pl.program_id(0)
        i = pl.program_id(1)
        first, lane = _head_masks()
        every = pl.ds(0, M)
        kcols = pl.ds(pl.multiple_of(j * 128, 128), 128)
        vcols = pl.ds(pl.multiple_of((N_PAIRS + j) * 128, 128), 128)

        @pl.when(i == 0)
        def _():
            pltpu.sync_copy(kv_hbm.at[every, kcols], k_ref)
            pltpu.sync_copy(kv_hbm.at[every, vcols], v_ref)
            pltpu.sync_copy(kr_hbm, kr_ref)
            dk_ref[...] = jnp.zeros_like(dk_ref)
            dv_ref[...] = jnp.zeros_like(dv_ref)
            dkr_ref[...] = jnp.zeros_like(dkr_ref)

        qs = _split_heads(q_ref[...], first)
        qrs = _split_rope(qr_ref[...], lane)
        qcat = tuple(jnp.concatenate([qs[hh], qrs[hh]], axis=1) for hh in range(2))
        dov = do_ref[...]
        dos = _split_heads(dov, first)
        prod = dov.astype(F32) * o_ref[...].astype(F32)
        deltas = (jnp.sum(jnp.where(first, prod, 0.0), axis=1, keepdims=True),
                  jnp.sum(jnp.where(first, 0.0, prod), axis=1, keepdims=True))
        for hh in range(2):
            lse_s[hh] = jnp.broadcast_to(lse_ref[:, hh:hh + 1], (t, t))
            delta_s[hh] = jnp.broadcast_to(deltas[hh], (t, t))
        dq_s[...] = jnp.zeros_like(dq_s)
        qpos = i * t + lax.broadcasted_iota(jnp.int32, (t, t), 0)
        kidx = lax.broadcasted_iota(jnp.int32, (t, t), 1)

        def step(kb, masked):
            k0 = pl.multiple_of(kb * t, t)
            rows = pl.ds(k0, t)
            k = k_ref[rows, :]
            v = v_ref[rows, :]
            kr = kr_ref[rows, :]
            kcat = jnp.concatenate([k, kr], axis=1)
            ks = _split_heads(k, first)
            krs = _split_rope(kr, lane)
            if masked:
                kpos = k0 + kidx
                valid = (kpos <= qpos) & (kpos >= PAD)
            dq = dk = dv = None
            for hh in range(2):
                s = _dot_nt(qcat[hh], kcat) * scale
                if masked:
                    s = jnp.where(valid, s, NEG)
                p = jnp.exp(s - lse_s[hh])
                ds = p * (_dot_nt(dos[hh], v) - delta_s[hh])
                dsb = (ds * scale).astype(BF16)
                a = _dot(dsb, jnp.concatenate([ks[hh], krs[hh]], axis=1))
                b = _dot_tn(dsb, qcat[hh])
                c = _dot_tn(p.astype(BF16), dos[hh])
                dq = a if dq is None else dq + a
                dk = b if dk is None else dk + b
                dv = c if dv is None else dv + c
            dq_s[...] += dq
            dk_ref[rows, :] += dk[:, :128]
            dkr_ref[rows, :] += dk[:, 128:]
            dv_ref[rows, :] += dv

        _walk_causal(i, step)
        dq_ref[...] = dq_s[:, :128].astype(dq_ref.dtype)
        dqr_ref[...] = dq_s[:, 128:].astype(dqr_ref.dtype)

        @pl.when(i == M // t - 1)
        def _():
            pltpu.sync_copy(dk_ref, dk_hbm.at[every, kcols])
            pltpu.sync_copy(dv_ref, dv_hbm.at[every, kcols])
            pltpu.sync_copy(dkr_ref, dkr_hbm.at[j])

    blk = pl.BlockSpec((t, 128), lambda j, i: (i, j))
    whole = pl.BlockSpec(memory_space=pl.ANY)
    wide = jax.ShapeDtypeStruct((M, N_PAIRS * 128), F32)
    slab = lambda dtype: pltpu.VMEM((M, 128), dtype)
    return pl.pallas_call(
        body, name=name, grid=(N_PAIRS, M // t),
        in_specs=[blk, whole, blk, whole, blk, blk, pl.BlockSpec((None, t, 2), lambda j, i: (j, i, 0))],
        out_specs=[blk, whole, whole, blk, whole],
        out_shape=[jax.ShapeDtypeStruct((M, N_PAIRS * 128), BF16), wide, wide,
                   jax.ShapeDtypeStruct((M, N_PAIRS * 128), BF16), jax.ShapeDtypeStruct((N_PAIRS, M, 128), F32)],
        scratch_shapes=[slab(BF16), slab(BF16), slab(BF16), slab(F32), slab(F32), slab(F32),
                        pltpu.VMEM((t, 256), F32), pltpu.VMEM((2, t, t), F32), pltpu.VMEM((2, t, t), F32)],
        compiler_params=_params(("arbitrary", "arbitrary")))(q_all, kv_all, qr, kr, o, do, lse)


def _tri(t, rel):
    j = lax.broadcasted_iota(jnp.int32, (t, t), 0)
    k = lax.broadcasted_iota(jnp.int32, (t, t), 1)
    m = {"gt": j > k, "le": j <= k, "lt": j < k}[rel]
    return m.astype(BF16)


def _lane_cumsum(x, tri):
    hi = x.astype(BF16)
    lo = (x - hi.astype(F32)).astype(BF16)
    return _dot(hi, tri) + _dot(lo, tri)


def _log_sigmoids(z):
    sp = jnp.log(1.0 + jnp.exp(-jnp.abs(z)))
    return jnp.minimum(z, 0.0) - sp, jnp.minimum(-z, 0.0) - sp


def _log_sigmoids_fast(z):
    lk = -(jnp.maximum(z, 0.0) + jnp.log(1.0 + jnp.exp(-jnp.abs(z))))
    return lk + z, lk


def _sb_fwd(qkv, scale, name):
    M = qkv.shape[0]
    t = WALK_TILE
    ck, cv = N_PAIRS, 2 * N_PAIRS

    def body(q_ref, k_ref, v_ref, o_ref, tot_ref, c_s, acc_s):
        i = pl.program_id(1)
        first, _ = _head_masks()
        qs = _split_heads(q_ref[...], first)
        c_s[...] = jnp.zeros_like(c_s)
        acc_s[...] = jnp.zeros_like(acc_s)
        tri = _tri(t, "gt")
        qpos = i * t + lax.broadcasted_iota(jnp.int32, (t, t), 0)
        kidx = lax.broadcasted_iota(jnp.int32, (t, t), 1)

        def step(kb, masked):
            k0 = pl.multiple_of(kb * t, t)
            k = k_ref[pl.ds(k0, t), :]
            vs = _split_heads(v_ref[pl.ds(k0, t), :], first)
            if masked:
                kpos = k0 + kidx
                valid = (kpos < qpos) & (kpos >= PAD)
            pv = None
            for hh in range(2):
                z = _dot_nt(qs[hh], k) * scale
                lb, lk = _log_sigmoids_fast(z)
                if masked:
                    lk = jnp.where(valid, lk, 0.0)
                a = jnp.exp(lb + (c_s[hh] + _lane_cumsum(lk, tri)))
                if masked:
                    a = jnp.where(valid, a, 0.0)
                c_s[hh] = c_s[hh] + jnp.sum(lk, axis=1, keepdims=True)
                d = _dot(a.astype(BF16), vs[hh])
                pv = d if pv is None else pv + d
            acc_s[...] += pv

        def keep_going():
            return jnp.max(jnp.maximum(c_s[0], c_s[1])) > EXP_ZERO

        def cond(carry):
            kb, go, _ = carry
            return (kb >= 1) & go

        def walk(carry):
            kb, _, n = carry
            step(kb, False)
            return kb - 1, keep_going(), n + 1

        step(i, True)
        _, go, n = lax.while_loop(cond, walk, (i - 1, keep_going(), jnp.int32(1)))
        first_too = go & (i > 0)

        @pl.when(first_too)
        def _():
            step(0, True)

        walked = n + first_too.astype(jnp.int32)
        o_ref[...] = acc_s[...].astype(o_ref.dtype)
        tot_ref[:, 0:1] = c_s[0]
        tot_ref[:, 1:2] = c_s[1]
        tot_ref[:, 2:3] = jnp.full((t, 1), walked.astype(F32))

    whole = lambda c0: pl.BlockSpec((M, 128), lambda j, i: (0, c0 + j))
    return pl.pallas_call(
        body, name=name, grid=(N_PAIRS, M // t),
        in_specs=[pl.BlockSpec((t, 128), lambda j, i: (i, j)), whole(ck), whole(cv)],
        out_specs=[pl.BlockSpec((t, 128), lambda j, i: (i, j)), pl.BlockSpec((None, t, 3), lambda j, i: (j, i, 0))],
        out_shape=[jax.ShapeDtypeStruct((M, N_PAIRS * 128), BF16), jax.ShapeDtypeStruct((N_PAIRS, M, 3), F32)],
        scratch_shapes=[pltpu.VMEM((2, t, 1), F32), pltpu.VMEM((t, 128), F32)],
        compiler_params=_params(("parallel", "arbitrary")))(qkv, qkv, qkv)


def _sb_bwd(qkv, do, tot, scale, name):
    M = qkv.shape[0]
    t = WALK_TILE
    ck, cv = N_PAIRS, 2 * N_PAIRS

    def body(q_ref, k_ref, v_ref, do_ref, tot_ref, dq_ref, dk_ref, dv_ref, pc_s, dc_s, dq_s):
        i = pl.program_id(1)
        first, _ = _head_masks()

        @pl.when(i == 0)
        def _():
            dk_ref[...] = jnp.zeros_like(dk_ref)
            dv_ref[...] = jnp.zeros_like(dv_ref)

        qs = _split_heads(q_ref[...], first)
        dos = _split_heads(do_ref[...], first)
        pc_s[...] = jnp.zeros_like(pc_s)
        dc_s[...] = jnp.zeros_like(dc_s)
        dq_s[...] = jnp.zeros_like(dq_s)
        tri_le = _tri(t, "le")
        tri_lt = _tri(t, "lt")
        qpos = i * t + lax.broadcasted_iota(jnp.int32, (t, t), 0)
        kidx = lax.broadcasted_iota(jnp.int32, (t, t), 1)

        def step(kb, masked):
            k0 = pl.multiple_of(kb * t, t)
            rows = pl.ds(k0, t)
            k = k_ref[rows, :]
            v = v_ref[rows, :]
            ks = _split_heads(k, first)
            if masked:
                kpos = k0 + kidx
                valid = (kpos < qpos) & (kpos >= PAD)
            dq = dk = dv = None
            for hh in range(2):
                z = _dot_nt(qs[hh], k) * scale
                lb, lk = _log_sigmoids_fast(z)
                if masked:
                    lk = jnp.where(valid, lk, 0.0)
                later = tot_ref[:, hh:hh + 1] - (pc_s[hh] + _lane_cumsum(lk, tri_le))
                a = jnp.exp(lb + later)
                if masked:
                    a = jnp.where(valid, a, 0.0)
                dl = a * _dot_nt(dos[hh], v)
                early = dc_s[hh] + _lane_cumsum(dl, tri_lt)
                sg = jnp.exp(lb)
                dz = (dl * (1.0 - sg) - early * sg) * scale
                if masked:
                    dz = jnp.where(valid, dz, 0.0)
                pc_s[hh] = pc_s[hh] + jnp.sum(lk, axis=1, keepdims=True)
                dc_s[hh] = dc_s[hh] + jnp.sum(dl, axis=1, keepdims=True)
                dzb = dz.astype(BF16)
                x = _dot(dzb, ks[hh])
                y = _dot_tn(dzb, qs[hh])
                w = _dot_tn(a.astype(BF16), dos[hh])
                dq = x if dq is None else dq + x
                dk = y if dk is None else dk + y
                dv = w if dv is None else dv + w
            dq_s[...] += dq
            dk_ref[rows, :] += dk
            dv_ref[rows, :] += dv

        first_walked = i + 1 - jnp.max(tot_ref[:, 2:3]).astype(jnp.int32)

        def mid(kb, carry):
            step(kb, False)
            return carry

        @pl.when((first_walked == 0) & (i > 0))
        def _():
            step(0, True)

        lax.fori_loop(jnp.maximum(first_walked, 1), i, mid, 0)
        step(i, True)
        dq_ref[...] = dq_s[...].astype(dq_ref.dtype)

    whole = lambda c0: pl.BlockSpec((M, 128), lambda j, i: (0, c0 + j))
    blk = pl.BlockSpec((t, 128), lambda j, i: (i, j))
    col = pl.BlockSpec((M, 128), lambda j, i: (0, j))
    return pl.pallas_call(
        body, name=name, grid=(N_PAIRS, M // t),
        in_specs=[blk, whole(ck), whole(cv), blk, pl.BlockSpec((None, t, 3), lambda j, i: (j, i, 0))],
        out_specs=[blk, col, col],
        out_shape=[jax.ShapeDtypeStruct((M, N_PAIRS * 128), BF16), jax.ShapeDtypeStruct((M, N_PAIRS * 128), F32),
                   jax.ShapeDtypeStruct((M, N_PAIRS * 128), F32)],
        scratch_shapes=[pltpu.VMEM((2, t, 1), F32), pltpu.VMEM((2, t, 1), F32), pltpu.VMEM((t, 128), F32)],
        compiler_params=_params(("parallel", "arbitrary")))(qkv, qkv, qkv, do, tot)


def _rows_between(lo, hi):
    r = lax.broadcasted_iota(jnp.int32, (128, 1), 0)
    return (r >= lo) & (r < hi)


def _lanes_between(lo, hi):
    c = lax.broadcasted_iota(jnp.int32, (1, 128), 1)
    return (c >= lo) & (c < hi)


def _keep(x, mask):
    return jnp.where(mask, x, jnp.zeros_like(x))


def _valid_mask(i, kb, t):
    kpos = kb * t + lax.broadcasted_iota(jnp.int32, (t, t), 0)
    qpos = i * t + lax.broadcasted_iota(jnp.int32, (t, t), 1)
    return (kpos <= qpos) & (kpos >= PAD)


def _fox_fwd(qkv, qkv_t, f_rows, f_cols, scale, name):
    M = qkv.shape[0]
    t = FOX_TILE
    first_blk = PAD // t
    ck, cv = N_PAIRS, 2 * N_PAIRS

    def body(qt_ref, k_ref, vt_ref, fq_ref, fk_ref, o_ref, lse_ref, ox_ref, m_s, l_s, acc_s, accx_s, kmax_s):
        i = pl.program_id(1)

        @pl.when(i == 0)
        def _():
            first = _lanes_between(0, 64)

            def block_max(kb, carry):
                kk = k_ref[pl.ds(pl.multiple_of(kb * t, t), t), :].astype(F32)
                kk = kk * kk
                a = jnp.max(jnp.sum(jnp.where(first, kk, 0.0), axis=1, keepdims=True))
                b = jnp.max(jnp.sum(jnp.where(first, 0.0, kk), axis=1, keepdims=True))
                return jnp.maximum(carry[0], a), jnp.maximum(carry[1], b)

            a, b = lax.fori_loop(0, M // t, block_max, (jnp.float32(0.0), jnp.float32(0.0)))
            kmax_s[0] = a
            kmax_s[1] = b

        qt = qt_ref[...]
        qts = (_keep(qt, _rows_between(0, 64)), _keep(qt, _rows_between(64, 128)))
        qf = qt.astype(F32)
        qf = qf * qf
        qbound = tuple(
            (1.001 * scale) * jnp.sqrt(jnp.sum(qf[HEAD_DIM * hh:HEAD_DIM * (hh + 1)], axis=0, keepdims=True) * kmax_s[hh])
            for hh in range(2))
        m_s[...] = jnp.full_like(m_s, NEG)
        l_s[...] = jnp.zeros_like(l_s)
        acc_s[...] = jnp.zeros_like(acc_s)
        accx_s[...] = jnp.zeros_like(accx_s)

        def step(kb, masked):
            k0 = pl.multiple_of(kb * t, t)
            rows = pl.ds(k0, t)
            k = k_ref[rows, :]
            if masked:
                valid = _valid_mask(i, kb, t)
            for hh in range(2):
                s = _dot(k, qts[hh]) * scale + (fq_ref[hh:hh + 1, :] - fk_ref[rows, hh:hh + 1])
                if masked:
                    s = jnp.where(valid, s, NEG)
                m_old = m_s[hh]
                m_new = jnp.maximum(m_old, jnp.max(s, axis=0, keepdims=True))
                p = jnp.exp(s - m_new)
                alpha = jnp.exp(m_old - m_new)
                l_s[hh] = alpha * l_s[hh] + jnp.sum(p, axis=0, keepdims=True)
                m_s[hh] = m_new
                pb = p.astype(BF16)
                hr = slice(HEAD_DIM * hh, HEAD_DIM * (hh + 1))
                vt = vt_ref[hr, rows]
                acc_s[hr, :] = acc_s[hr, :] * alpha + _dot(vt, pb)
                accx_s[hr, :] = accx_s[hr, :] * alpha + _dot(vt, (p - pb.astype(F32)).astype(BF16))

        def keep_going(kb):
            k0 = pl.multiple_of(kb * t, t)
            worst = None
            for hh in range(2):
                f0 = jnp.max(fk_ref[pl.ds(k0, 8), hh:hh + 1])
                w = jnp.max(qbound[hh] + (fq_ref[hh:hh + 1, :] - f0) - m_s[hh])
                worst = w if worst is None else jnp.maximum(worst, w)
            return worst > EXP_ZERO

        def cond(carry):
            kb, go, _ = carry
            return (kb > first_blk) & go

        def walk(carry):
            kb, _, n = carry
            step(kb, False)
            return kb - 1, keep_going(kb), n + 1

        step(i, True)
        _, go, n = lax.while_loop(cond, walk, (i - 1, keep_going(i), jnp.int32(1)))
        first_too = go & (i > first_blk)

        @pl.when(first_too)
        def _():
            step(first_blk, True)

        walked = n + first_too.astype(jnp.int32)
        for hh in range(2):
            hr = slice(HEAD_DIM * hh, HEAD_DIM * (hh + 1))
            inv = 1.0 / l_s[hh]
            o_ref[hr, :] = (acc_s[hr, :] * inv).astype(o_ref.dtype)
            ox_ref[hr, :] = (acc_s[hr, :] + accx_s[hr, :]) * inv
            lse_ref[hh:hh + 1, :] = m_s[hh] + jnp.log(l_s[hh])
        lse_ref[2:3, :] = jnp.full((1, t), walked.astype(F32))

    blk = pl.BlockSpec((128, t), lambda j, i: (j, i))
    stat = pl.BlockSpec((None, 2, t), lambda j, i: (j, 0, i))
    return pl.pallas_call(
        body, name=name, grid=(N_PAIRS, M // t),
        in_specs=[blk, pl.BlockSpec((M, 128), lambda j, i: (0, ck + j)), pl.BlockSpec((128, M), lambda j, i: (cv + j, 0)),
                  stat, pl.BlockSpec((None, M, 2), lambda j, i: (j, 0, 0))],
        out_specs=[blk, pl.BlockSpec((None, 3, t), lambda j, i: (j, 0, i)), blk],
        out_shape=[jax.ShapeDtypeStruct((N_PAIRS * 128, M), BF16), jax.ShapeDtypeStruct((N_PAIRS, 3, M), F32),
                   jax.ShapeDtypeStruct((N_PAIRS * 128, M), F32)],
        scratch_shapes=[pltpu.VMEM((2, 1, t), F32), pltpu.VMEM((2, 1, t), F32), pltpu.VMEM((128, t), F32),
                        pltpu.VMEM((128, t), F32), pltpu.SMEM((2,), F32)],
        compiler_params=_params(("parallel", "arbitrary")))(qkv_t, qkv, qkv_t, f_rows, f_cols)


def _fox_bwd(qkv, qkv_t, o_t, do, do_t, lse, f_rows, f_cols, scale, name):
    M = qkv.shape[0]
    t = FOX_TILE
    first_blk = PAD // t
    ck, cv = N_PAIRS, 2 * N_PAIRS

    def body(q_ref, qt_ref, k_ref, kt_ref, v_ref, ot_ref, do_ref, dot_ref, lse_ref, fq_ref, fk_ref,
             dq_ref, dk_ref, dv_ref, cs_ref, dq_s):
        i = pl.program_id(1)

        @pl.when(i == 0)
        def _():
            dk_ref[...] = jnp.zeros_like(dk_ref)
            dv_ref[...] = jnp.zeros_like(dv_ref)
            cs_ref[...] = jnp.zeros_like(cs_ref)

        heads_l = (_lanes_between(0, 64), _lanes_between(64, 128))
        heads_r = (_rows_between(0, 64), _rows_between(64, 128))
        q = q_ref[...]
        qt = qt_ref[...]
        do = do_ref[...]
        dot = dot_ref[...]
        qs = tuple(_keep(q, m) for m in heads_l)
        qts = tuple(_keep(qt, m) for m in heads_r)
        dos = tuple(_keep(do, m) for m in heads_l)
        dots = tuple(_keep(dot, m) for m in heads_r)
        prod = dot.astype(F32) * ot_ref[...]
        deltas = tuple(jnp.sum(prod[HEAD_DIM * hh:HEAD_DIM * (hh + 1)], axis=0, keepdims=True) for hh in range(2))
        ones = tuple(m.astype(BF16) * jnp.ones((t, 128), BF16) for m in heads_l)
        dq_s[...] = jnp.zeros_like(dq_s)

        def step(kb, masked):
            k0 = pl.multiple_of(kb * t, t)
            rows = pl.ds(k0, t)
            k = k_ref[rows, :]
            v = v_ref[rows, :]
            if masked:
                valid = _valid_mask(i, kb, t)
            dk = dv = cs = None
            for hh in range(2):
                s = _dot(k, qts[hh]) * scale + (fq_ref[hh:hh + 1, :] - fk_ref[rows, hh:hh + 1])
                if masked:
                    s = jnp.where(valid, s, NEG)
                p = jnp.exp(s - lse_ref[hh:hh + 1, :])
                ds = p * (_dot(v, dots[hh]) - deltas[hh])
                hi = ds.astype(BF16)
                lo = (ds - hi.astype(F32)).astype(BF16)
                c = _dot(hi, ones[hh]) + _dot(lo, ones[hh])
                dsb = (ds * scale).astype(BF16)
                hr = slice(HEAD_DIM * hh, HEAD_DIM * (hh + 1))
                dq_s[hr, :] += _dot(kt_ref[hr, rows], dsb)
                a = _dot(dsb, qs[hh])
                b = _dot(p.astype(BF16), dos[hh])
                dk = a if dk is None else dk + a
                dv = b if dv is None else dv + b
                cs = c if cs is None else cs + c
            dk_ref[rows, :] += dk
            dv_ref[rows, :] += dv
            cs_ref[rows, :] += cs

        first_walked = i + 1 - jnp.max(lse_ref[2:3, :]).astype(jnp.int32)

        def mid(kb, carry):
            step(kb, False)
            return carry

        @pl.when((first_walked == first_blk) & (i > first_blk))
        def _():
            step(first_blk, True)

        lax.fori_loop(jnp.maximum(first_walked, first_blk + 1), i, mid, 0)
        step(i, True)
        dq_ref[...] = dq_s[...].astype(dq_ref.dtype)

    rblk = pl.BlockSpec((t, 128), lambda j, i: (i, j))
    tblk = pl.BlockSpec((128, t), lambda j, i: (j, i))
    stat = pl.BlockSpec((None, 2, t), lambda j, i: (j, 0, i))
    stat3 = pl.BlockSpec((None, 3, t), lambda j, i: (j, 0, i))
    col = pl.BlockSpec((M, 128), lambda j, i: (0, j))
    wide = jax.ShapeDtypeStruct((M, N_PAIRS * 128), F32)
    return pl.pallas_call(
        body, name=name, grid=(N_PAIRS, M // t),
        in_specs=[rblk, tblk, pl.BlockSpec((M, 128), lambda j, i: (0, ck + j)),
                  pl.BlockSpec((128, M), lambda j, i: (ck + j, 0)), pl.BlockSpec((M, 128), lambda j, i: (0, cv + j)),
                  tblk, rblk, tblk, stat3, stat, pl.BlockSpec((None, M, 2), lambda j, i: (j, 0, 0))],
        out_specs=[tblk, col, col, pl.BlockSpec((None, M, 128), lambda j, i: (j, 0, 0))],
        out_shape=[jax.ShapeDtypeStruct((N_PAIRS * 128, M), BF16), wide, wide,
                   jax.ShapeDtypeStruct((N_PAIRS, M, 128), F32)],
        scratch_shapes=[pltpu.VMEM((128, t), F32)],
        compiler_params=_params(("parallel", "arbitrary")))(qkv, qkv_t, qkv, qkv_t, qkv, o_t, do, do_t, lse,
                                                            f_rows, f_cols)


def _rope_tables(M):
    pos = (jnp.arange(M, dtype=jnp.int32) - PAD).astype(F32)
    inv = ROPE_THETA ** (-jnp.arange(0, MLA_ROPE, 2, dtype=F32) / MLA_ROPE)
    ang = pos[:, None] * inv[None, :]
    cos, sin = jnp.cos(ang), jnp.sin(ang)
    z = jnp.zeros((M, 64), F32)
    cos_t = jnp.concatenate([cos, cos, cos, cos, z], axis=1)
    sin_t = jnp.concatenate([-sin, sin, -sin, sin, z], axis=1)
    return cos_t, sin_t


def _rope(x, cos_t, sin_t, out_dtype, name, inverse=False, lead=0):
    M, C = x.shape
    tm = _pick(M, (768, 512, 256, 128))
    nblk = (C - lead) // 128
    sign = -1.0 if inverse else 1.0

    def body(x_ref, c_ref, s_ref, o_ref):
        lane = lax.broadcasted_iota(jnp.int32, (1, 128), 1)
        low = (lane % MLA_ROPE) < (MLA_ROPE // 2)
        cos = c_ref[...]
        sin = s_ref[...] * sign
        if lead:
            o_ref[:, :lead] = x_ref[:, :lead].astype(o_ref.dtype)
        for b in range(nblk):
            cols = slice(lead + b * 128, lead + (b + 1) * 128)
            v = x_ref[:, cols].astype(F32)
            up = pltpu.roll(v, 128 - MLA_ROPE // 2, 1)
            down = pltpu.roll(v, MLA_ROPE // 2, 1)
            o_ref[:, cols] = (v * cos + jnp.where(low, up, down) * sin).astype(o_ref.dtype)

    row = pl.BlockSpec((tm, C), lambda m: (m, 0))
    tab = pl.BlockSpec((tm, 128), lambda m: (m, 0))
    return pl.pallas_call(
        body, name=name, grid=(M // tm,), in_specs=[row, tab, tab], out_specs=row,
        out_shape=jax.ShapeDtypeStruct((M, C), out_dtype),
        compiler_params=_params(("parallel",)))(x, cos_t, sin_t)


def _forget_cumsum(f_logit, bias, name):
    M = f_logit.shape[0]
    tm = 256

    def body(f_ref, b_ref, o_ref, c_s):
        i = pl.program_id(0)

        @pl.when(i == 0)
        def _():
            c_s[...] = jnp.zeros_like(c_s)
        ls, _ = _log_sigmoids(f_ref[...] + b_ref[...])
        rows = i * tm + lax.broadcasted_iota(jnp.int32, (tm, 1), 0)
        ls = jnp.where(rows >= PAD, ls, 0.0)
        r = lax.broadcasted_iota(jnp.int32, (tm, tm), 0)
        c = lax.broadcasted_iota(jnp.int32, (tm, tm), 1)
        tri = (c <= r).astype(F32)
        cum = jnp.dot(tri, ls, precision=lax.Precision.HIGHEST, preferred_element_type=F32) + c_s[...]
        o_ref[...] = cum
        c_s[...] = cum[tm - 1:tm, :]

    row = pl.BlockSpec((tm, 128), lambda m: (m, 0))
    return pl.pallas_call(
        body, name=name, grid=(M // tm,),
        in_specs=[row, pl.BlockSpec((1, 128), lambda m: (0, 0))], out_specs=row,
        out_shape=jax.ShapeDtypeStruct((M, 128), F32), scratch_shapes=[pltpu.VMEM((1, 128), F32)],
        compiler_params=_params(("arbitrary",)))(f_logit, bias)


def _forget_cumsum_bwd(f_logit, bias, colsum, name):
    M = f_logit.shape[0]
    tm = 256
    nb = M // tm

    def body(f_ref, b_ref, cs_ref, o_ref, db_ref, c_s):
        i = pl.program_id(0)

        @pl.when(i == 0)
        def _():
            c_s[...] = jnp.zeros_like(c_s)
            db_ref[...] = jnp.zeros_like(db_ref)
        rr = lax.broadcasted_iota(jnp.int32, (128, 128), 0)
        cc = lax.broadcasted_iota(jnp.int32, (128, 128), 1)
        dF = None
        for j in range(N_PAIRS):
            sel = (((rr == 0) & (cc == 2 * j)) | ((rr == HEAD_DIM) & (cc == 2 * j + 1))).astype(F32)
            d = jnp.dot(cs_ref[j], sel, precision=lax.Precision.HIGHEST, preferred_element_type=F32)
            dF = d if dF is None else dF + d
        r = lax.broadcasted_iota(jnp.int32, (tm, tm), 0)
        c = lax.broadcasted_iota(jnp.int32, (tm, tm), 1)
        tri = (c >= r).astype(F32)
        cum = c_s[...] - jnp.dot(tri, dF, precision=lax.Precision.HIGHEST, preferred_element_type=F32)
        c_s[...] = cum[0:1, :]
        _, lsn = _log_sigmoids(f_ref[...] + b_ref[...])
        rows = (nb - 1 - i) * tm + lax.broadcasted_iota(jnp.int32, (tm, 1), 0)
        dl = jnp.where(rows >= PAD, cum * jnp.exp(lsn), 0.0)
        o_ref[...] = dl
        db_ref[...] += jnp.sum(dl, axis=0, keepdims=True)

    row = pl.BlockSpec((tm, 128), lambda m: (nb - 1 - m, 0))
    vec = pl.BlockSpec((1, 128), lambda m: (0, 0))
    return pl.pallas_call(
        body, name=name, grid=(nb,),
        in_specs=[row, vec, pl.BlockSpec((N_PAIRS, tm, 128), lambda m: (0, nb - 1 - m, 0))], out_specs=[row, vec],
        out_shape=[jax.ShapeDtypeStruct((M, 128), F32), jax.ShapeDtypeStruct((1, 128), F32)],
        scratch_shapes=[pltpu.VMEM((1, 128), F32)],
        compiler_params=_params(("arbitrary",)))(f_logit, bias, colsum)


def _adamw(w, parts, m, v, name):
    R, C = w.shape
    n_parts = parts.shape[0]
    tr = R
    for d in range(8, R, 8):
        if R % d == 0 and d * C <= ADAM_TILE_ELEMS:
            tr = d
    c1 = 1.0 - ADAM_B1 ** ADAM_STEP
    c2 = 1.0 - ADAM_B2 ** ADAM_STEP

    def body(w_ref, s_ref, m_ref, v_ref, g_ref, d_ref, mo_ref, vo_ref):
        g = s_ref[0].astype(F32)
        for k in range(1, n_parts):
            g = g + s_ref[k].astype(F32)
        mn = ADAM_B1 * m_ref[...] + (1.0 - ADAM_B1) * g
        vn = ADAM_B2 * v_ref[...] + (1.0 - ADAM_B2) * (g * g)
        m_hat = mn / c1
        v_hat = vn / c2
        g_ref[...] = g
        d_ref[...] = -ADAM_LR * (m_hat / (jnp.sqrt(v_hat) + ADAM_EPS) + ADAM_WD * w_ref[...])
        mo_ref[...] = mn
        vo_ref[...] = vn

    row = pl.BlockSpec((tr, C), lambda r: (r, 0))
    shp = jax.ShapeDtypeStruct((R, C), F32)
    return pl.pallas_call(
        body, name=name, grid=(R // tr,),
        in_specs=[row, pl.BlockSpec((n_parts, tr, C), lambda r: (0, r, 0)), row, row],
        out_specs=[row, row, row, row], out_shape=[shp, shp, shp, shp],
        compiler_params=_params(("parallel",)))(w, parts, m, v)


def _position():
    return lax.axis_index("x"), lax.axis_index("y"), lax.axis_index("c")


def _all_gather(blocks, name):
    n = len(blocks)

    def body(*refs):
        x_refs, out_refs = refs[:n], refs[n:2 * n]
        send_sems, recv_sems, local_sems = refs[2 * n:]
        x, y, c = _position()
        me, sibling = (x, y, c), (x, y, 1 - c)
        chips = [(1 - x, y), (x, 1 - y), (1 - x, 1 - y)]

        def copies(k, block, to, own=False):
            slot = 4 * block[0] + 2 * block[1] + block[2]
            return [pltpu.make_async_remote_copy(
                src_ref=x_refs[p] if own else out_refs[p].at[slot], dst_ref=out_refs[p].at[slot],
                send_sem=send_sems.at[k, p], recv_sem=recv_sems.at[k, p], device_id=to, device_id_type=MESH)
                for p in range(n)]

        mine = [pltpu.make_async_copy(x_refs[p], out_refs[p].at[4 * x + 2 * y + c], local_sems.at[p]) for p in range(n)]
        for cp in mine:
            cp.start()
        first = copies(0, me, sibling, own=True)
        for j, chip in enumerate(chips):
            first += copies(1 + j, me, (*chip, c), own=True)
        for cp in first:
            cp.start()
        passed = []
        for j, chip in enumerate(chips):
            for cp in copies(1 + j, (*chip, c), me):
                cp.wait_recv()
            onward = copies(4 + j, (*chip, c), sibling)
            for cp in onward:
                cp.start()
            passed += onward
        for cp in copies(0, sibling, me):
            cp.wait_recv()
        for j, chip in enumerate(chips):
            for cp in copies(4 + j, (*chip, 1 - c), me):
                cp.wait_recv()
        for cp in first + passed:
            cp.wait_send()
        for cp in mine:
            cp.wait()

    any_spec = pl.BlockSpec(memory_space=pl.ANY)
    return pl.pallas_call(
        body, name=name, out_shape=[jax.ShapeDtypeStruct((N_DEV,) + b.shape, b.dtype) for b in blocks],
        in_specs=[any_spec] * n, out_specs=[any_spec] * n,
        scratch_shapes=[pltpu.SemaphoreType.DMA((7, n)), pltpu.SemaphoreType.DMA((7, n)), pltpu.SemaphoreType.DMA((n,))],
    )(*blocks)


N_CHIPS = 4


def _exchange_siblings(parts, name):
    n = len(parts)

    def body(*refs):
        g_refs, land_refs = refs[:n], refs[n:2 * n]
        send_sems, recv_sems = refs[2 * n:]
        x, y, c = _position()
        sibling = (x, y, 1 - c)
        sends, recvs = [], []
        for q in range(N_CHIPS):
            for p in range(n):
                sends.append(pltpu.make_async_remote_copy(
                    src_ref=g_refs[p].at[2 * q + (1 - c)], dst_ref=land_refs[p].at[q], send_sem=send_sems.at[q, p],
                    recv_sem=recv_sems.at[q, p], device_id=sibling, device_id_type=MESH))
                recvs.append(pltpu.make_async_remote_copy(
                    src_ref=g_refs[p].at[2 * q + c], dst_ref=land_refs[p].at[q], send_sem=send_sems.at[q, p],
                    recv_sem=recv_sems.at[q, p], device_id=sibling, device_id_type=MESH))
        for cp in sends:
            cp.start()
        for cp in recvs:
            cp.wait_recv()
        for cp in sends:
            cp.wait_send()

    any_spec = pl.BlockSpec(memory_space=pl.ANY)
    return pl.pallas_call(
        body, name=name, out_shape=[jax.ShapeDtypeStruct((N_CHIPS,) + p.shape[1:], p.dtype) for p in parts],
        in_specs=[any_spec] * n, out_specs=[any_spec] * n,
        scratch_shapes=[pltpu.SemaphoreType.DMA((N_CHIPS, n)), pltpu.SemaphoreType.DMA((N_CHIPS, n))],
    )(*parts)


def _pair_sum(part, from_sibling, name):
    _, R, C = part.shape
    tr = R
    for d in range(8, R, 8):
        if R % d == 0 and d * C <= ADAM_TILE_ELEMS:
            tr = d

    def body(a_ref, b_ref, o_ref):
        c = lax.axis_index("c")
        for q in range(N_CHIPS):
            o_ref[q] = (a_ref[2 * q + c].astype(F32) + b_ref[q].astype(F32)).astype(o_ref.dtype)

    return pl.pallas_call(
        body, name=name, grid=(R // tr,),
        in_specs=[pl.BlockSpec((N_DEV, tr, C), lambda r: (0, r, 0)), pl.BlockSpec((N_CHIPS, tr, C), lambda r: (0, r, 0))],
        out_specs=pl.BlockSpec((N_CHIPS, tr, C), lambda r: (0, r, 0)),
        out_shape=jax.ShapeDtypeStruct((N_CHIPS, R, C), part.dtype),
        compiler_params=_params(("parallel",)))(part, from_sibling)


def _exchange_chips(sums, name):
    n = len(sums)

    def body(*refs):
        g_refs, land_refs = refs[:n], refs[n:2 * n]
        send_sems, recv_sems, local_sems = refs[2 * n:]
        x, y, c = _position()
        me = 2 * x + y
        mine = [pltpu.make_async_copy(g_refs[p].at[me], land_refs[p].at[me], local_sems.at[p]) for p in range(n)]
        for cp in mine:
            cp.start()
        sends, recvs = [], []
        for k in range(1, N_CHIPS):
            px = 1 - x if k & 2 else x
            py = 1 - y if k & 1 else y
            peer = 2 * px + py
            for p in range(n):
                sends.append(pltpu.make_async_remote_copy(
                    src_ref=g_refs[p].at[peer], dst_ref=land_refs[p].at[me], send_sem=send_sems.at[k - 1, p],
                    recv_sem=recv_sems.at[k - 1, p], device_id=(px, py, c), device_id_type=MESH))
                recvs.append(pltpu.make_async_remote_copy(
                    src_ref=g_refs[p].at[me], dst_ref=land_refs[p].at[peer], send_sem=send_sems.at[k - 1, p],
                    recv_sem=recv_sems.at[k - 1, p], device_id=(px, py, c), device_id_type=MESH))
        for cp in sends:
            cp.start()
        for cp in recvs:
            cp.wait_recv()
        for cp in sends:
            cp.wait_send()
        for cp in mine:
            cp.wait()

    any_spec = pl.BlockSpec(memory_space=pl.ANY)
    return pl.pallas_call(
        body, name=name, out_shape=[jax.ShapeDtypeStruct(p.shape, p.dtype) for p in sums],
        in_specs=[any_spec] * n, out_specs=[any_spec] * n,
        scratch_shapes=[pltpu.SemaphoreType.DMA((3, n)), pltpu.SemaphoreType.DMA((3, n)), pltpu.SemaphoreType.DMA((n,))],
    )(*sums)


SHARDED = (("sb_w_qkv", 2), ("sb_w_o", 1), ("mla_w_down", 1), ("mla_w_uq", 2), ("mla_w_ukv", 2), ("mla_w_o", 1),
           ("fox_w_qkvf", 2), ("fox_w_o", 1), ("ffn_w_gate", 2), ("ffn_w_up", 2), ("ffn_w_down", 1),
           ("pool_w", 2), ("meta", 1), ("mla_q_norm", 1), ("mla_kv_norm", 1))
KEPT_F32 = ("meta", "mla_q_norm", "mla_kv_norm")
REPLICATED = ("norm_mix", "norm_ffn", "pool_scale", "fox_b_f", "final_norm")
WEIGHT_NAMES = ("meta", "norm_mix", "norm_ffn", "pool_w", "pool_scale", "sb_w_qkv", "sb_w_o", "mla_w_down",
                "mla_q_norm", "mla_kv_norm", "mla_w_uq", "mla_w_ukv", "mla_w_o", "fox_w_qkvf", "fox_b_f",
                "fox_w_o", "ffn_w_gate", "ffn_w_up", "ffn_w_down", "final_norm")
LANES = 1024


def _pack_rows(arrays, names):
    parts = []
    for n in names:
        flat = arrays[n].reshape(-1).astype(F32)
        rows = -(-flat.shape[0] // LANES)
        parts.append(jnp.pad(flat, (0, rows * LANES - flat.shape[0])).reshape(rows, LANES))
    rows = sum(p.shape[0] for p in parts)
    parts.append(jnp.zeros((-(-rows // 8) * 8 - rows, LANES), F32))
    return jnp.concatenate(parts, axis=0)


def _unpack_rows(buf, shapes, names):
    out, row = {}, 0
    for n in names:
        size = int(np.prod(shapes[n]))
        rows = -(-size // LANES)
        out[n] = buf[row:row + rows].reshape(-1)[:size].reshape(shapes[n])
        row += rows
    return out


def _whole_from_gathered(g, axis):
    g = jnp.moveaxis(g, 0, axis)
    shp = g.shape
    return g.reshape(shp[:axis] + (shp[axis] * shp[axis + 1],) + shp[axis + 2:])


def _parts_from_whole(whole, axis):
    shp = whole.shape
    g = whole.reshape(shp[:axis] + (N_DEV, shp[axis] // N_DEV) + shp[axis + 1:])
    return jnp.moveaxis(g, axis, 0)


def _kernel_weights(full):
    W = {}
    W["pool_w"] = full["pool_w"][0]
    W["sb_w_qkv"] = full["sb_w_qkv"][0]
    W["sb_w_o"] = full["sb_w_o"][0]
    W["mla_w_down"] = full["mla_w_down"][0]
    uq = full["mla_w_uq"][0].reshape(MLA_Q_RANK, N_HEADS, MLA_NOPE + MLA_ROPE)
    nope = uq[:, :, :MLA_NOPE].reshape(MLA_Q_RANK, N_HEADS * MLA_NOPE)
    rope = uq[:, :, MLA_NOPE:].reshape(MLA_Q_RANK, N_PAIRS, 2 * MLA_ROPE)
    rope = jnp.pad(rope, ((0, 0), (0, 0), (0, 128 - 2 * MLA_ROPE))).reshape(MLA_Q_RANK, N_PAIRS * 128)
    W["mla_w_uq"] = jnp.concatenate([nope, rope], axis=1)
    ukv = full["mla_w_ukv"][0].reshape(MLA_KV_RANK, N_HEADS, 2, HEAD_DIM)
    W["mla_w_ukv"] = jnp.transpose(ukv, (0, 2, 1, 3)).reshape(MLA_KV_RANK, 2 * N_HEADS * HEAD_DIM)
    W["mla_w_o"] = full["mla_w_o"][0]
    qkvf = full["fox_w_qkvf"][0]
    n_qkv = 3 * N_HEADS * HEAD_DIM
    W["fox_w_qkv"] = qkvf[:, :n_qkv]
    W["fox_w_f"] = jnp.pad(qkvf[:, n_qkv:], ((0, 0), (0, 128 - N_HEADS)))
    W["fox_w_qkvf"] = jnp.concatenate([W["fox_w_qkv"], W["fox_w_f"]], axis=1)
    W["fox_w_o"] = full["fox_w_o"][0]
    W["ffn_w_gate"] = full["ffn_w_gate"]
    W["ffn_w_up"] = full["ffn_w_up"]
    W["ffn_w_down"] = full["ffn_w_down"]
    return W


def _reference_grads(G):
    out = {}
    out["pool_w"] = G["pool_w"][None]
    for n in ("sb_w_qkv", "sb_w_o", "mla_w_down", "mla_w_o", "fox_w_o"):
        out[n] = G[n][None]
    duq = G["mla_w_uq"]
    nope = duq[:, :N_HEADS * MLA_NOPE].reshape(MLA_Q_RANK, N_HEADS, MLA_NOPE)
    rope = duq[:, N_HEADS * MLA_NOPE:].reshape(MLA_Q_RANK, N_PAIRS, 128)[:, :, :2 * MLA_ROPE]
    rope = rope.reshape(MLA_Q_RANK, N_HEADS, MLA_ROPE)
    out["mla_w_uq"] = jnp.concatenate([nope, rope], axis=2).reshape(1, MLA_Q_RANK, -1)
    dukv = G["mla_w_ukv"].reshape(MLA_KV_RANK, 2, N_HEADS, HEAD_DIM)
    out["mla_w_ukv"] = jnp.transpose(dukv, (0, 2, 1, 3)).reshape(1, MLA_KV_RANK, -1)
    out["fox_w_qkvf"] = G["fox_w_qkvf"][None, :, :3 * N_HEADS * HEAD_DIM + N_HEADS]
    out["ffn_w_gate"] = G["ffn_w_gate"]
    out["ffn_w_up"] = G["ffn_w_up"]
    out["ffn_w_down"] = G["ffn_w_down"]
    out["mla_q_norm"] = G["mla_q_norm"]
    out["mla_kv_norm"] = G["mla_kv_norm"]
    return out


def _pairs_col(f16):
    M = f16.shape[0]
    return jnp.transpose(f16.reshape(M, N_PAIRS, 2), (1, 0, 2))


def _pairs_row(f16):
    M = f16.shape[0]
    return jnp.transpose(f16.reshape(M, N_PAIRS, 2), (1, 2, 0))


def _local_step(x, target, W, P):
    S = x.shape[0]
    M = S + ROW0
    G = {}
    gain = lambda name, i: P[name][i][None, :]
    h0 = jnp.concatenate([jnp.zeros((PAD, D_MODEL), F32), P["meta"], x], axis=0)

    def ffn_fwd(h1, i):
        b = _norm_fwd(h1, gain("norm_ffn", i), BF16, f"ffn{i}_norm")
        g, u, act = _ffn_up(b, W["ffn_w_gate"][i], W["ffn_w_up"][i], f"ffn{i}_up")
        h2 = _mm_nn(act, W["ffn_w_down"][i], F32, f"ffn{i}_down", res=h1)
        return h2, (h1, b, g, u, act)

    def ffn_bwd(dh2, saved, i):
        h1, b, g, u, act = saved
        dg, du = _ffn_dact(dh2, W["ffn_w_down"][i], g, u, f"ffn{i}_dact")
        G.setdefault("ffn_w_down", {})[i] = _mm_tn(act, dh2, f"ffn{i}_dwd")
        db = _mm_nt2(dg, W["ffn_w_gate"][i], du, W["ffn_w_up"][i], F32, f"ffn{i}_db")
        G.setdefault("ffn_w_gate", {})[i] = _mm_tn(b, dg, f"ffn{i}_dwg")
        G.setdefault("ffn_w_up", {})[i] = _mm_tn(b, du, f"ffn{i}_dwu")
        dh1, dgain = _norm_bwd(h1, gain("norm_ffn", i), db, dh2, f"ffn{i}_dnorm")
        G.setdefault("norm_ffn", {})[i] = dgain
        return dh1

    a0 = _norm_fwd(h0, gain("norm_mix", 0), F32, "mix0_norm")
    h1_0, pooled = _pool_fwd(h0, a0, W["pool_w"], P["pool_scale"], "pool_fwd")
    h_1, ffn0 = ffn_fwd(h1_0, 0)

    sb_scale = HEAD_DIM ** -0.5
    a1 = _norm_fwd(h_1, gain("norm_mix", 1), BF16, "mix1_norm")
    sb_qkv = _mm_nn(a1, W["sb_w_qkv"], BF16, "sb_qkv")
    sb_o, sb_tot = _sb_fwd(sb_qkv, sb_scale, "sb_fwd")
    h1_1 = _mm_nn(sb_o, W["sb_w_o"], F32, "sb_out", res=h_1)
    h_2, ffn1 = ffn_fwd(h1_1, 1)

    mla_scale = (MLA_NOPE + MLA_ROPE) ** -0.5
    cos_t, sin_t = _rope_tables(M)
    a2 = _norm_fwd(h_2, gain("norm_mix", 2), BF16, "mix2_norm")
    down = _mm_nn(a2, W["mla_w_down"], F32, "mla_down")
    dq_raw = down[:, :MLA_Q_RANK]
    dkv_raw = down[:, MLA_Q_RANK:MLA_Q_RANK + MLA_KV_RANK]
    kr_raw = down[:, MLA_Q_RANK + MLA_KV_RANK:]
    c_q = _norm_fwd(dq_raw, P["mla_q_norm"], BF16, "mla_qnorm")
    c_kv = _norm_fwd(dkv_raw, P["mla_kv_norm"], BF16, "mla_kvnorm")
    q_lin = _mm_nn(c_q, W["mla_w_uq"], F32, "mla_uq")
    q_all = _rope(q_lin, cos_t, sin_t, BF16, "mla_qrope", lead=D_MODEL)
    kv_all = _mm_nn(c_kv, W["mla_w_ukv"], BF16, "mla_ukv")
    kr_in = jnp.concatenate([kr_raw, kr_raw, jnp.zeros((M, 64), F32)], axis=1)
    kr = _rope(kr_in, cos_t, sin_t, BF16, "mla_krope")
    q_rope = q_all[:, D_MODEL:]
    mla_o, mla_lse = _mla_fwd(q_all, kv_all, q_rope, kr, mla_scale, "mla_fwd")
    h1_2 = _mm_nn(mla_o, W["mla_w_o"], F32, "mla_out", res=h_2)
    h_3, ffn2 = ffn_fwd(h1_2, 2)

    fox_scale = HEAD_DIM ** -0.5
    a3 = _norm_fwd(h_3, gain("norm_mix", 3), BF16, "mix3_norm")
    fox_qkv = _mm_nn(a3, W["fox_w_qkv"], BF16, "fox_qkv")
    f_logit = _mm_nn(a3, W["fox_w_f"], F32, "fox_f")
    b_f = jnp.pad(P["fox_b_f"], ((0, 0), (0, 128 - N_HEADS)))
    Fc = _forget_cumsum(f_logit, b_f, "fox_cumsum")
    f_rows, f_cols = _pairs_row(Fc[:, :N_HEADS]), _pairs_col(Fc[:, :N_HEADS])
    fox_qkv_t = fox_qkv.T
    fox_o_t, fox_lse, fox_ox_t = _fox_fwd(fox_qkv, fox_qkv_t, f_rows, f_cols, fox_scale, "fox_fwd")
    fox_o = fox_o_t.T
    h1_3 = _mm_nn(fox_o, W["fox_w_o"], F32, "fox_out", res=h_3)
    h_4, ffn3 = ffn_fwd(h1_3, 3)

    sq, dh, dgain = _loss_head(h_4, P["final_norm"][None, :], target, "loss_head")
    G["final_norm"] = dgain[0]

    dh = ffn_bwd(dh, ffn3, 3)
    do = _mm_nt(dh, W["fox_w_o"], BF16, "fox_do")
    G["fox_w_o"] = _mm_tn(fox_o, dh, "fox_dwo")
    dq_t, dk, dv, colsum = _fox_bwd(fox_qkv, fox_qkv_t, fox_ox_t, do, do.T, fox_lse, f_rows, f_cols, fox_scale,
                                    "fox_bwd")
    dlogit, db_f = _forget_cumsum_bwd(f_logit, b_f, colsum, "fox_dcumsum")
    G["fox_b_f"] = db_f[:, :N_HEADS]
    dproj = jnp.concatenate([dq_t.T, dk.astype(BF16), dv.astype(BF16), dlogit.astype(BF16)], axis=1)
    da = _mm_nt(dproj, W["fox_w_qkvf"], F32, "fox_da")
    G["fox_w_qkvf"] = _mm_tn(a3, dproj, "fox_dwqkvf")
    dh, dgain = _norm_bwd(h_3, gain("norm_mix", 3), da, dh, "mix3_dnorm")
    G.setdefault("norm_mix", {})[3] = dgain

    dh = ffn_bwd(dh, ffn2, 2)
    do = _mm_nt(dh, W["mla_w_o"], BF16, "mla_do")
    G["mla_w_o"] = _mm_tn(mla_o, dh, "mla_dwo")
    dq, dk, dv, dqr, dkr = _mla_bwd(q_all, kv_all, q_rope, kr, mla_o, do, mla_lse, mla_scale, "mla_bwd")
    dqr = _rope(dqr, cos_t, sin_t, BF16, "mla_dqrope", inverse=True)
    dq_all = jnp.concatenate([dq, dqr], axis=1)
    dkr_sum = _rope(jnp.sum(dkr, axis=0), cos_t, sin_t, F32, "mla_dkrope", inverse=True)
    dkr_raw = dkr_sum[:, :MLA_ROPE] + dkr_sum[:, MLA_ROPE:2 * MLA_ROPE]
    dkv_all = jnp.concatenate([dk.astype(BF16), dv.astype(BF16)], axis=1)
    dc_q = _mm_nt(dq_all, W["mla_w_uq"], F32, "mla_dcq")
    G["mla_w_uq"] = _mm_tn(c_q, dq_all, "mla_dwuq")
    dc_kv = _mm_nt(dkv_all, W["mla_w_ukv"], F32, "mla_dckv")
    G["mla_w_ukv"] = _mm_tn(c_kv, dkv_all, "mla_dwukv")
    ddq_raw, G["mla_q_norm"] = _norm_bwd(dq_raw, P["mla_q_norm"], dc_q, None, "mla_dqnorm")
    ddkv_raw, G["mla_kv_norm"] = _norm_bwd(dkv_raw, P["mla_kv_norm"], dc_kv, None, "mla_dkvnorm")
    ddown = jnp.concatenate([ddq_raw, ddkv_raw, dkr_raw], axis=1).astype(BF16)
    da = _mm_nt(ddown, W["mla_w_down"], F32, "mla_da")
    G["mla_w_down"] = _mm_tn(a2, ddown, "mla_dwdown")
    dh, dgain = _norm_bwd(h_2, gain("norm_mix", 2), da, dh, "mix2_dnorm")
    G["norm_mix"][2] = dgain

    dh = ffn_bwd(dh, ffn1, 1)
    do = _mm_nt(dh, W["sb_w_o"], BF16, "sb_do")
    G["sb_w_o"] = _mm_tn(sb_o, dh, "sb_dwo")
    dq, dk, dv = _sb_bwd(sb_qkv, do, sb_tot, sb_scale, "sb_bwd")
    dqkv = jnp.concatenate([dq, dk.astype(BF16), dv.astype(BF16)], axis=1)
    da = _mm_nt(dqkv, W["sb_w_qkv"], F32, "sb_da")
    G["sb_w_qkv"] = _mm_tn(a1, dqkv, "sb_dwqkv")
    dh, dgain = _norm_bwd(h_1, gain("norm_mix", 1), da, dh, "mix1_dnorm")
    G["norm_mix"][1] = dgain

    dh = ffn_bwd(dh, ffn0, 0)
    dpc, G["pool_w"], G["pool_scale"] = _pool_bwd_mix(dh, pooled, W["pool_w"], P["pool_scale"], "pool_dmix")
    da = _pool_bwd_window(dpc, "pool_dwindow")
    dh, dgain, dx = _norm_bwd(h0, gain("norm_mix", 0), da, dh, "mix0_dnorm", token_rows=True)
    G["norm_mix"][0] = dgain

    G["norm_mix"] = jnp.concatenate([G["norm_mix"][i] for i in range(DEPTH)], axis=0)
    G["norm_ffn"] = jnp.concatenate([G["norm_ffn"][i] for i in range(DEPTH)], axis=0)
    G["ffn_w_down"] = jnp.stack([G["ffn_w_down"][i] for i in range(DEPTH)])
    G["ffn_w_gate"] = jnp.stack([G["ffn_w_gate"][i] for i in range(DEPTH)])
    G["ffn_w_up"] = jnp.stack([G["ffn_w_up"][i] for i in range(DEPTH)])
    G["meta"] = dh[PAD:ROW0]
    return sq, dx, G


def kernel(x, meta, norm_mix, norm_ffn, pool_w, pool_scale, sb_w_qkv, sb_w_o, mla_w_down, mla_q_norm, mla_kv_norm, mla_w_uq, mla_w_ukv, mla_w_o, fox_w_qkvf, fox_b_f, fox_w_o, ffn_w_gate, ffn_w_up, ffn_w_down, final_norm, loss_target, m_meta, m_norm_mix, m_norm_ffn, m_pool_w, m_pool_scale, m_sb_w_qkv, m_sb_w_o, m_mla_w_down, m_mla_q_norm, m_mla_kv_norm, m_mla_w_uq, m_mla_w_ukv, m_mla_w_o, m_fox_w_qkvf, m_fox_b_f, m_fox_w_o, m_ffn_w_gate, m_ffn_w_up, m_ffn_w_down, m_final_norm, v_meta, v_norm_mix, v_norm_ffn, v_pool_w, v_pool_scale, v_sb_w_qkv, v_sb_w_o, v_mla_w_down, v_mla_q_norm, v_mla_kv_norm, v_mla_w_uq, v_mla_w_ukv, v_mla_w_o, v_fox_w_qkvf, v_fox_b_f, v_fox_w_o, v_ffn_w_gate, v_ffn_w_up, v_ffn_w_down, v_final_norm):
    w = dict(meta=meta, norm_mix=norm_mix, norm_ffn=norm_ffn, pool_w=pool_w, pool_scale=pool_scale,
             sb_w_qkv=sb_w_qkv, sb_w_o=sb_w_o, mla_w_down=mla_w_down, mla_q_norm=mla_q_norm,
             mla_kv_norm=mla_kv_norm, mla_w_uq=mla_w_uq, mla_w_ukv=mla_w_ukv, mla_w_o=mla_w_o,
             fox_w_qkvf=fox_w_qkvf, fox_b_f=fox_b_f, fox_w_o=fox_w_o, ffn_w_gate=ffn_w_gate, ffn_w_up=ffn_w_up,
             ffn_w_down=ffn_w_down, final_norm=final_norm)
    m = dict(meta=m_meta, norm_mix=m_norm_mix, norm_ffn=m_norm_ffn, pool_w=m_pool_w, pool_scale=m_pool_scale,
             sb_w_qkv=m_sb_w_qkv, sb_w_o=m_sb_w_o, mla_w_down=m_mla_w_down, mla_q_norm=m_mla_q_norm,
             mla_kv_norm=m_mla_kv_norm, mla_w_uq=m_mla_w_uq, mla_w_ukv=m_mla_w_ukv, mla_w_o=m_mla_w_o,
             fox_w_qkvf=m_fox_w_qkvf, fox_b_f=m_fox_b_f, fox_w_o=m_fox_w_o, ffn_w_gate=m_ffn_w_gate,
             ffn_w_up=m_ffn_w_up, ffn_w_down=m_ffn_w_down, final_norm=m_final_norm)
    v = dict(meta=v_meta, norm_mix=v_norm_mix, norm_ffn=v_norm_ffn, pool_w=v_pool_w, pool_scale=v_pool_scale,
             sb_w_qkv=v_sb_w_qkv, sb_w_o=v_sb_w_o, mla_w_down=v_mla_w_down, mla_q_norm=v_mla_q_norm,
             mla_kv_norm=v_mla_kv_norm, mla_w_uq=v_mla_w_uq, mla_w_ukv=v_mla_w_ukv, mla_w_o=v_mla_w_o,
             fox_w_qkvf=v_fox_w_qkvf, fox_b_f=v_fox_b_f, fox_w_o=v_fox_w_o, ffn_w_gate=v_ffn_w_gate,
             ffn_w_up=v_ffn_w_up, ffn_w_down=v_ffn_w_down, final_norm=v_final_norm)

    sh_names = tuple(n for n, _ in SHARDED)
    sh_axis = dict(SHARDED)
    shapes = {n: w[n].shape for n in WEIGHT_NAMES}
    wire = lambda n: F32 if n in KEPT_F32 else BF16

    gathered = _all_gather([w[n].astype(wire(n)) for n in sh_names], "gather_weights")
    full = {n: _whole_from_gathered(g, sh_axis[n]) for n, g in zip(sh_names, gathered)}
    W = _kernel_weights(full)
    P = dict(meta=full["meta"], mla_q_norm=full["mla_q_norm"], mla_kv_norm=full["mla_kv_norm"],
             norm_mix=norm_mix, norm_ffn=norm_ffn, pool_scale=pool_scale, fox_b_f=fox_b_f, final_norm=final_norm)

    sq, dx, G = _local_step(x[0], loss_target[0], W, P)
    loss = lax.psum(0.5 * jnp.sum(sq) / D_MODEL, ("x", "y", "c"))
    grad_x = dx[None]

    gw = _reference_grads(G)
    gw["meta"] = G["meta"]
    rc = {n: (int(np.prod(shapes[n][:-1])), shapes[n][-1]) for n in sh_names}
    parts = [_parts_from_whole(gw[n], sh_axis[n]).astype(wire(n)).reshape((N_DEV,) + rc[n]) for n in sh_names]
    from_sibling = _exchange_siblings(parts, "exchange_grads_d2d")
    sums = [_pair_sum(a, b, f"pair_sum_{n}") for n, a, b in zip(sh_names, parts, from_sibling)]
    landed = _exchange_chips(sums, "exchange_grads_ici")
    results = {}
    for n, got in zip(sh_names, landed):
        outs = _adamw(w[n].reshape(rc[n]), got, m[n].reshape(rc[n]), v[n].reshape(rc[n]), f"adamw_{n}")
        results[n] = [o.reshape(shapes[n]) for o in outs]

    rep_g = dict(norm_mix=G["norm_mix"], norm_ffn=G["norm_ffn"], pool_scale=G["pool_scale"], fox_b_f=G["fox_b_f"],
                 final_norm=G["final_norm"])
    (rep_all,) = _all_gather([_pack_rows(rep_g, REPLICATED)], "gather_replicated_grads")
    rep_out = _adamw(_pack_rows(w, REPLICATED), rep_all, _pack_rows(m, REPLICATED), _pack_rows(v, REPLICATED),
                     "adamw_replicated")
    rep = [_unpack_rows(o, shapes, REPLICATED) for o in rep_out]
    for n in REPLICATED:
        results[n] = [r[n] for r in rep]

    outs = [results[n][k] for k in range(4) for n in WEIGHT_NAMES]
    return (loss, grad_x, *outs)
```

```python
import numpy as np
import jax
import jax.numpy as jnp
from jax import lax
from jax.experimental import pallas as pl
from jax.experimental.pallas import tpu as pltpu

F32 = jnp.float32
BF16 = jnp.bfloat16

N_DEV = 8
D_MODEL = 1024
N_META = 16
PAD = 240
ROW0 = PAD + N_META
EPS = 1e-6
POOL_WINDOWS = (2, 4, 8, 16)
POOL_GROUP = 256
HALO = 128
N_HEADS = 16
HEAD_DIM = 64
N_PAIRS = N_HEADS // 2
MLA_Q_RANK = 384
MLA_KV_RANK = 256
MLA_NOPE = 64
MLA_ROPE = 32
ROPE_THETA = 10000.0
D_FF = 2816
DEPTH = 4
ATTN_TILE = 768
ATTN_BWD_TILE = 768
WALK_TILE = 256
FOX_TILE = 384
NEG = -1e30
LOG2E = 1.4426950408889634
EXP_ZERO = -110.0
VMEM_LIMIT = 56 * 2**20
ADAM_TILE_ELEMS = 192 * 1024

ADAM_LR = 0.001
ADAM_B1 = 0.9
ADAM_B2 = 0.999
ADAM_EPS = 1e-08
ADAM_WD = 0.01
ADAM_STEP = 10

MESH = pl.DeviceIdType.MESH


def _params(sem=None):
    return pltpu.CompilerParams(dimension_semantics=sem, vmem_limit_bytes=VMEM_LIMIT)


def _pick(n, cands):
    for c in cands:
        if n % c == 0:
            return c
    return n


def _col_tile(n, cap=1536):
    best = None
    for t in range(128, min(n, cap) + 1, 128):
        if n % t == 0:
            best = t
    return best if best is not None else n


def _dot(a, b):
    return jnp.dot(a, b, preferred_element_type=F32)


def _dot_nt(a, b):
    return lax.dot_general(a, b, (((1,), (1,)), ((), ())), preferred_element_type=F32)


def _dot_tn(a, b):
    return lax.dot_general(a, b, (((0,), (0,)), ((), ())), preferred_element_type=F32)


def _mm_nn(a, b, out_dtype, name, res=None):
    M, K = a.shape
    N = b.shape[1]
    tm = _pick(M, (768, 512, 256, 128))
    tn = _col_tile(N)

    def body(*refs):
        if res is None:
            a_ref, b_ref, o_ref = refs
        else:
            a_ref, b_ref, r_ref, o_ref = refs
        acc = _dot(a_ref[...].astype(BF16), b_ref[...])
        if res is not None:
            acc = acc + r_ref[...]
        o_ref[...] = acc.astype(o_ref.dtype)

    in_specs = [pl.BlockSpec((tm, K), lambda n, m: (m, 0)), pl.BlockSpec((K, tn), lambda n, m: (0, n))]
    args = [a, b]
    if res is not None:
        in_specs.append(pl.BlockSpec((tm, tn), lambda n, m: (m, n)))
        args.append(res)
    return pl.pallas_call(
        body, name=name, grid=(N // tn, M // tm), in_specs=in_specs,
        out_specs=pl.BlockSpec((tm, tn), lambda n, m: (m, n)),
        out_shape=jax.ShapeDtypeStruct((M, N), out_dtype),
        compiler_params=_params(("parallel", "parallel")))(*args)


def _mm_nt(a, w, out_dtype, name):
    M, N = a.shape
    K = w.shape[0]
    tm = _pick(M, (768, 512, 256, 128)) if N <= 3200 else _pick(M, (256, 128))
    tk = _col_tile(K, 1024)

    def body(a_ref, w_ref, o_ref):
        o_ref[...] = _dot_nt(a_ref[...].astype(BF16), w_ref[...]).astype(o_ref.dtype)

    return pl.pallas_call(
        body, name=name, grid=(K // tk, M // tm),
        in_specs=[pl.BlockSpec((tm, N), lambda k, m: (m, 0)), pl.BlockSpec((tk, N), lambda k, m: (k, 0))],
        out_specs=pl.BlockSpec((tm, tk), lambda k, m: (m, k)),
        out_shape=jax.ShapeDtypeStruct((M, K), out_dtype),
        compiler_params=_params(("parallel", "parallel")))(a, w)


def _mm_nt2(a1, w1, a2, w2, out_dtype, name):
    M, N = a1.shape
    K = w1.shape[0]
    tm = _pick(M, (384, 256, 128))

    def body(a1_ref, w1_ref, a2_ref, w2_ref, o_ref):
        o_ref[...] = (_dot_nt(a1_ref[...], w1_ref[...]) + _dot_nt(a2_ref[...], w2_ref[...])).astype(o_ref.dtype)

    a_spec = pl.BlockSpec((tm, N), lambda m: (m, 0))
    w_spec = pl.BlockSpec((K, N), lambda m: (0, 0))
    return pl.pallas_call(
        body, name=name, grid=(M // tm,), in_specs=[a_spec, w_spec, a_spec, w_spec],
        out_specs=pl.BlockSpec((tm, K), lambda m: (m, 0)),
        out_shape=jax.ShapeDtypeStruct((M, K), out_dtype),
        compiler_params=_params(("parallel",)))(a1, w1, a2, w2)


def _mm_tn(a, b, name):
    M, K = a.shape
    N = b.shape[1]
    tm = _pick(M, (768, 512, 256, 128))
    tk = _col_tile(K, 1408)
    tn = _col_tile(N, 1408)

    def body(a_ref, b_ref, o_ref):
        @pl.when(pl.program_id(2) == 0)
        def _():
            o_ref[...] = jnp.zeros_like(o_ref)
        o_ref[...] += _dot_tn(a_ref[...].astype(BF16), b_ref[...].astype(BF16))

    return pl.pallas_call(
        body, name=name, grid=(K // tk, N // tn, M // tm),
        in_specs=[pl.BlockSpec((tm, tk), lambda k, n, m: (m, k)), pl.BlockSpec((tm, tn), lambda k, n, m: (m, n))],
        out_specs=pl.BlockSpec((tk, tn), lambda k, n, m: (k, n)),
        out_shape=jax.ShapeDtypeStruct((K, N), F32),
        compiler_params=_params(("parallel", "parallel", "arbitrary")))(a, b)


def _norm_fwd(h, gain, out_dtype, name):
    M, C = h.shape
    tm = _pick(M, (768, 512, 256, 128))

    def body(h_ref, g_ref, a_ref):
        x = h_ref[...]
        r = lax.rsqrt(jnp.mean(x * x, axis=-1, keepdims=True) + EPS)
        a_ref[...] = ((x * r) * g_ref[...]).astype(a_ref.dtype)

    return pl.pallas_call(
        body, name=name, grid=(M // tm,),
        in_specs=[pl.BlockSpec((tm, C), lambda m: (m, 0)), pl.BlockSpec((1, C), lambda m: (0, 0))],
        out_specs=pl.BlockSpec((tm, C), lambda m: (m, 0)),
        out_shape=jax.ShapeDtypeStruct((M, C), out_dtype),
        compiler_params=_params(("parallel",)))(h, gain)


def _norm_bwd(h, gain, da, dres, name, token_rows=False):
    M, C = h.shape
    tm = ROW0 if token_rows else _pick(M, (768, 512, 256, 128))

    def body(*refs):
        refs = list(refs)
        dx_ref = refs.pop() if token_rows else None
        if dres is None:
            h_ref, g_ref, da_ref, dh_ref, dg_ref = refs
        else:
            h_ref, g_ref, da_ref, dr_ref, dh_ref, dg_ref = refs
        x = h_ref[...]
        r = lax.rsqrt(jnp.mean(x * x, axis=-1, keepdims=True) + EPS)
        y = x * r
        dav = da_ref[...].astype(F32)
        dy = dav * g_ref[...]
        dh = r * (dy - y * jnp.mean(dy * y, axis=-1, keepdims=True))
        if dres is not None:
            dh = dh + dr_ref[...]
        dh_ref[...] = dh
        if token_rows:
            dx_ref[...] = dh

        @pl.when(pl.program_id(0) == 0)
        def _():
            dg_ref[...] = jnp.zeros_like(dg_ref)
        dg_ref[...] += jnp.sum(dav * y, axis=0, keepdims=True)

    row = pl.BlockSpec((tm, C), lambda m: (m, 0))
    vec = pl.BlockSpec((1, C), lambda m: (0, 0))
    in_specs = [row, vec, row] + ([row] if dres is not None else [])
    args = [h, gain, da] + ([dres] if dres is not None else [])
    out_specs = [row, vec]
    out_shape = [jax.ShapeDtypeStruct((M, C), F32), jax.ShapeDtypeStruct((1, C), F32)]
    if token_rows:
        out_specs.append(pl.BlockSpec((tm, C), lambda m: (jnp.maximum(m - 1, 0), 0)))
        out_shape.append(jax.ShapeDtypeStruct((M - ROW0, C), F32))
    return pl.pallas_call(
        body, name=name, grid=(M // tm,), in_specs=in_specs, out_specs=out_specs, out_shape=out_shape,
        compiler_params=_params(("arbitrary",)))(*args)


def _ffn_up(b, w_g, w_u, name):
    M, K = b.shape
    F = w_g.shape[1]
    tm = _pick(M, (768, 512, 256, 128))
    tn = _col_tile(F, 1408)
    nb = F // tn

    def body(b_ref, wg_ref, wu_ref, g_ref, u_ref, act_ref):
        x = b_ref[...]
        g = _dot(x, wg_ref[...])
        u = _dot(x, wu_ref[...])
        g_ref[...] = g.astype(g_ref.dtype)
        u_ref[...] = u.astype(u_ref.dtype)
        act_ref[...] = ((g * jax.nn.sigmoid(g)) * u).astype(act_ref.dtype)

    blk = pl.BlockSpec((tm, tn), lambda n, m: (m, n))
    return pl.pallas_call(
        body, name=name, grid=(nb, M // tm),
        in_specs=[pl.BlockSpec((tm, K), lambda n, m: (m, 0)),
                  pl.BlockSpec((K, tn), lambda n, m: (0, n)),
                  pl.BlockSpec((K, tn), lambda n, m: (0, n))],
        out_specs=[blk, blk, blk],
        out_shape=[jax.ShapeDtypeStruct((M, F), BF16), jax.ShapeDtypeStruct((M, F), BF16),
                   jax.ShapeDtypeStruct((M, F), BF16)],
        compiler_params=_params(("parallel", "parallel")))(b, w_g, w_u)


def _ffn_dact(dy, w_d, g, u, name):
    M, K = dy.shape
    F = w_d.shape[0]
    tm = _pick(M, (768, 512, 256, 128))
    tn = _col_tile(F, 1408)
    nb = F // tn

    def body(dy_ref, wd_ref, g_ref, u_ref, dg_ref, du_ref):
        dact = _dot_nt(dy_ref[...].astype(BF16), wd_ref[...])
        gv = g_ref[...].astype(F32)
        s = jax.nn.sigmoid(gv)
        silu = gv * s
        dg_ref[...] = (dact * u_ref[...].astype(F32) * (s * (1.0 + gv * (1.0 - s)))).astype(dg_ref.dtype)
        du_ref[...] = (dact * silu).astype(du_ref.dtype)

    blk = pl.BlockSpec((tm, tn), lambda n, m: (m, n))
    return pl.pallas_call(
        body, name=name, grid=(nb, M // tm),
        in_specs=[pl.BlockSpec((tm, K), lambda n, m: (m, 0)), pl.BlockSpec((tn, K), lambda n, m: (n, 0)), blk, blk],
        out_specs=[blk, blk],
        out_shape=[jax.ShapeDtypeStruct((M, F), BF16), jax.ShapeDtypeStruct((M, F), BF16)],
        compiler_params=_params(("parallel", "parallel")))(dy, w_d, g, u)


def _loss_head(h, gain, target, name):
    M, C = h.shape
    tm = ROW0
    assert M % tm == 0 and target.shape[0] == M - ROW0

    def body(h_ref, g_ref, t_ref, sq_ref, dh_ref, dg_ref):
        i = pl.program_id(0)

        @pl.when(i == 0)
        def _():
            sq_ref[...] = jnp.zeros_like(sq_ref)
            dg_ref[...] = jnp.zeros_like(dg_ref)
            dh_ref[...] = jnp.zeros_like(dh_ref)

        @pl.when(i > 0)
        def _():
            x = h_ref[...]
            r = lax.rsqrt(jnp.mean(x * x, axis=-1, keepdims=True) + EPS)
            y = x * r
            err = y * g_ref[...] - t_ref[...]
            sq_ref[...] += jnp.sum(err * err, axis=0, keepdims=True)
            da = err * (1.0 / C)
            dy = da * g_ref[...]
            dh_ref[...] = r * (dy - y * jnp.mean(dy * y, axis=-1, keepdims=True))
            dg_ref[...] += jnp.sum(da * y, axis=0, keepdims=True)

    row = pl.BlockSpec((tm, C), lambda m: (m, 0))
    vec = pl.BlockSpec((1, C), lambda m: (0, 0))
    return pl.pallas_call(
        body, name=name, grid=(M // tm,),
        in_specs=[row, vec, pl.BlockSpec((tm, C), lambda m: (jnp.maximum(m - 1, 0), 0))],
        out_specs=[vec, row, vec],
        out_shape=[jax.ShapeDtypeStruct((1, C), F32), jax.ShapeDtypeStruct((M, C), F32),
                   jax.ShapeDtypeStruct((1, C), F32)],
        compiler_params=_params(("arbitrary",)))(h, gain, target)


def _band_dot(band, x):
    hi = x.astype(BF16)
    rest = x - hi.astype(F32)
    mid = rest.astype(BF16)
    lo = (rest - mid.astype(F32)).astype(BF16)
    return _dot(band, hi) + _dot(band, mid) + _dot(band, lo)


def _pool_pos(row0, tm):
    return row0 + lax.broadcasted_iota(jnp.int32, (tm, 1), 0) - PAD


def _pool_fwd(h, a, w, scale, name):
    M, C = a.shape
    tm = 256
    hb = tm // HALO

    def body(h_ref, a_ref, halo_ref, w_ref, s_ref, o_ref, p_ref):
        i = pl.program_id(0)
        row0 = i * tm
        ext = jnp.concatenate([halo_ref[...], a_ref[...]], axis=0)
        src = row0 - HALO + lax.broadcasted_iota(jnp.int32, (tm + HALO, 1), 0)
        ext = jnp.where(src >= PAD, ext, 0.0)
        r = lax.broadcasted_iota(jnp.int32, (tm, tm + HALO), 0)
        c = lax.broadcasted_iota(jnp.int32, (tm, tm + HALO), 1)
        pos = _pool_pos(row0, tm)
        for g, win in enumerate(POOL_WINDOWS):
            band = ((c <= r + HALO) & (c > r + HALO - win)).astype(BF16)
            cols = slice(g * POOL_GROUP, (g + 1) * POOL_GROUP)
            xg = ext[:, cols]
            tot = _band_dot(band, xg)
            cnt = jnp.clip(pos + 1, 1, win).astype(F32)
            pooled = (tot / cnt - xg[HALO:]).astype(BF16)
            p_ref[:, cols] = pooled
            mixed = _dot(pooled, w_ref[g])
            o_ref[:, cols] = h_ref[:, cols] + mixed * s_ref[:, cols]

    row = pl.BlockSpec((tm, C), lambda m: (m, 0))
    return pl.pallas_call(
        body, name=name, grid=(M // tm,),
        in_specs=[row, row, pl.BlockSpec((HALO, C), lambda m: (jnp.maximum(m * hb - 1, 0), 0)),
                  pl.BlockSpec((4, POOL_GROUP, POOL_GROUP), lambda m: (0, 0, 0)),
                  pl.BlockSpec((1, C), lambda m: (0, 0))],
        out_specs=[row, row],
        out_shape=[jax.ShapeDtypeStruct((M, C), F32), jax.ShapeDtypeStruct((M, C), BF16)],
        compiler_params=_params(("parallel",)))(h, a, a, w, scale)


def _pool_bwd_mix(dout, pooled, w, scale, name):
    M, C = dout.shape
    tm = 256

    def body(do_ref, p_ref, w_ref, s_ref, dpc_ref, dw_ref, ds_ref):
        i = pl.program_id(0)

        @pl.when(i == 0)
        def _():
            dw_ref[...] = jnp.zeros_like(dw_ref)
            ds_ref[...] = jnp.zeros_like(ds_ref)

        pos = _pool_pos(i * tm, tm)
        for g, win in enumerate(POOL_WINDOWS):
            cols = slice(g * POOL_GROUP, (g + 1) * POOL_GROUP)
            do = do_ref[:, cols]
            pooled = p_ref[:, cols]
            mixed = _dot(pooled, w_ref[g])
            ds_ref[:, cols] += jnp.sum(do * mixed, axis=0, keepdims=True)
            dmix = (do * s_ref[:, cols]).astype(BF16)
            dw_ref[g] += _dot_tn(pooled, dmix)
            dp = _dot_nt(dmix, w_ref[g])
            cnt = jnp.clip(pos + 1, 1, win).astype(F32)
            dpc_ref[:, cols] = dp / cnt

    row = pl.BlockSpec((tm, C), lambda m: (m, 0))
    wspec = pl.BlockSpec((4, POOL_GROUP, POOL_GROUP), lambda m: (0, 0, 0))
    vec = pl.BlockSpec((1, C), lambda m: (0, 0))
    return pl.pallas_call(
        body, name=name, grid=(M // tm,),
        in_specs=[row, row, wspec, vec], out_specs=[row, wspec, vec],
        out_shape=[jax.ShapeDtypeStruct((M, C), F32), jax.ShapeDtypeStruct((4, POOL_GROUP, POOL_GROUP), F32),
                   jax.ShapeDtypeStruct((1, C), F32)],
        compiler_params=_params(("arbitrary",)))(dout, pooled, w, scale)


def _pool_bwd_window(dpc, name):
    M, C = dpc.shape
    tm = 256
    hb = tm // HALO
    last = M // HALO - 1

    def body(d_ref, halo_ref, da_ref):
        i = pl.program_id(0)
        row0 = i * tm
        ext = jnp.concatenate([d_ref[...], halo_ref[...]], axis=0)
        src = row0 + lax.broadcasted_iota(jnp.int32, (tm + HALO, 1), 0)
        ext = jnp.where(src < M, ext, 0.0)
        r = lax.broadcasted_iota(jnp.int32, (tm, tm + HALO), 0)
        c = lax.broadcasted_iota(jnp.int32, (tm, tm + HALO), 1)
        pos = _pool_pos(row0, tm)
        for g, win in enumerate(POOL_WINDOWS):
            band = ((c >= r) & (c < r + win)).astype(BF16)
            cols = slice(g * POOL_GROUP, (g + 1) * POOL_GROUP)
            xg = ext[:, cols]
            tot = _band_dot(band, xg)
            cnt = jnp.clip(pos + 1, 1, win).astype(F32)
            da_ref[:, cols] = jnp.where(pos >= 0, tot - xg[:tm] * cnt, 0.0)

    row = pl.BlockSpec((tm, C), lambda m: (m, 0))
    return pl.pallas_call(
        body, name=name, grid=(M // tm,),
        in_specs=[row, pl.BlockSpec((HALO, C), lambda m: (jnp.minimum((m + 1) * hb, last), 0))],
        out_specs=row, out_shape=jax.ShapeDtypeStruct((M, C), F32),
        compiler_params=_params(("parallel",)))(dpc, dpc)


def _head_masks():
    lane = lax.broadcasted_iota(jnp.int32, (1, 128), 1)
    return lane < HEAD_DIM, lane


def _split_heads(x, first):
    z = jnp.zeros_like(x)
    return jnp.where(first, x, z), jnp.where(first, z, x)


def _split_rope(x, lane):
    z = jnp.zeros_like(x)
    return jnp.where(lane < MLA_ROPE, x, z), jnp.where((lane >= MLA_ROPE) & (lane < 2 * MLA_ROPE), x, z)


def _walk_causal(i, step):
    def mid(kb, carry):
        step(kb, False)
        return carry

    step(0, True)
    lax.fori_loop(1, i, mid, 0)

    @pl.when(i > 0)
    def _():
        step(i, True)


def _mla_fwd(q_all, kv_all, qr, kr, scale, name):
    M = q_all.shape[0]
    t = ATTN_TILE

    def body(q_ref, k_ref, v_ref, qr_ref, kr_ref, o_ref, lse_ref, m_s, l_s, acc_s, kmax_s):
        i = pl.program_id(1)
        first, lane = _head_masks()

        @pl.when(i == 0)
        def _():
            def block_max(kb, carry):
                rows = pl.ds(pl.multiple_of(kb * t, t), t)
                kk = k_ref[rows, :].astype(F32)
                kk = kk * kk
                rr = kr_ref[rows, :].astype(F32)
                rr = jnp.sum(jnp.where(lane < MLA_ROPE, rr * rr, 0.0), axis=1, keepdims=True)
                a = jnp.max(jnp.sum(jnp.where(first, kk, 0.0), axis=1, keepdims=True) + rr)
                b = jnp.max(jnp.sum(jnp.where(first, 0.0, kk), axis=1, keepdims=True) + rr)
                return jnp.maximum(carry[0], a), jnp.maximum(carry[1], b)

            a, b = lax.fori_loop(0, M // t, block_max, (jnp.float32(0.0), jnp.float32(0.0)))
            kmax_s[0] = a
            kmax_s[1] = b

        qs = _split_heads(q_ref[...], first)
        qrs = _split_rope(qr_ref[...], lane)
        qcat = tuple(jnp.concatenate([qs[hh], qrs[hh]], axis=1) for hh in range(2))
        qpos = i * t + lax.broadcasted_iota(jnp.int32, (t, t), 0)
        kidx = lax.broadcasted_iota(jnp.int32, (t, t), 1)
        c2 = scale * LOG2E

        def run(online):
            l_s[...] = jnp.zeros_like(l_s)
            acc_s[...] = jnp.zeros_like(acc_s)
            if online:
                m_s[...] = jnp.full_like(m_s, NEG)

            def step(kb, masked):
                k0 = pl.multiple_of(kb * t, t)
                kcat = jnp.concatenate([k_ref[pl.ds(k0, t), :], kr_ref[pl.ds(k0, t), :]], axis=1)
                vs = _split_heads(v_ref[pl.ds(k0, t), :], first)
                if masked:
                    kpos = k0 + kidx
                    valid = (kpos <= qpos) & (kpos >= PAD)
                pv = None
                alphas = []
                for hh in range(2):
                    s = _dot_nt(qcat[hh], kcat)
                    if online:
                        if masked:
                            s = jnp.where(valid, s, NEG)
                        m_old = m_s[hh]
                        m_new = jnp.maximum(m_old, jnp.max(s, axis=1, keepdims=True))
                        p = jnp.exp2((s - m_new) * c2)
                        alpha = jnp.exp2((m_old - m_new) * c2)
                        l_s[hh] = alpha * l_s[hh] + jnp.sum(p, axis=1, keepdims=True)
                        m_s[hh] = m_new
                        alphas.append(alpha)
                    else:
                        p = jnp.exp2(s * c2 - m_s[hh])
                        if masked:
                            p = jnp.where(valid, p, 0.0)
                        l_s[hh] = l_s[hh] + jnp.sum(p, axis=1, keepdims=True)
                    d = _dot(p.astype(BF16), vs[hh])
                    pv = d if pv is None else pv + d
                if online:
                    acc_s[...] = acc_s[...] * jnp.where(first, alphas[0], alphas[1]) + pv
                else:
                    acc_s[...] += pv

            _walk_causal(i, step)

        for hh in range(2):
            qf = qcat[hh].astype(F32)
            m_s[hh] = (1.001 * c2) * jnp.sqrt(jnp.sum(qf * qf, axis=1, keepdims=True) * kmax_s[hh])
        run(False)
        real = i * t + lax.broadcasted_iota(jnp.int32, (t, 1), 0) >= PAD
        underflow = jnp.max(jnp.where(real & (jnp.minimum(l_s[0], l_s[1]) < 1e-30), 1.0, 0.0)) > 0.0

        @pl.when(underflow)
        def _():
            run(True)
            m_s[...] = m_s[...] * c2

        ls = tuple(jnp.where(l_s[hh] > 0.0, l_s[hh], 1.0) for hh in range(2))
        o_ref[...] = (acc_s[...] * jnp.where(first, 1.0 / ls[0], 1.0 / ls[1])).astype(o_ref.dtype)
        lse_ref[:, 0:1] = m_s[0] * (1.0 / LOG2E) + jnp.log(ls[0])
        lse_ref[:, 1:2] = m_s[1] * (1.0 / LOG2E) + jnp.log(ls[1])

    blk = pl.BlockSpec((t, 128), lambda j, i: (i, j))
    return pl.pallas_call(
        body, name=name, grid=(N_PAIRS, M // t),
        in_specs=[blk, pl.BlockSpec((M, 128), lambda j, i: (0, j)), pl.BlockSpec((M, 128), lambda j, i: (0, N_PAIRS + j)),
                  blk, pl.BlockSpec((M, 128), lambda j, i: (0, 0))],
        out_specs=[blk, pl.BlockSpec((None, t, 2), lambda j, i: (j, i, 0))],
        out_shape=[jax.ShapeDtypeStruct((M, N_PAIRS * 128), BF16), jax.ShapeDtypeStruct((N_PAIRS, M, 2), F32)],
        scratch_shapes=[pltpu.VMEM((2, t, 1), F32), pltpu.VMEM((2, t, 1), F32), pltpu.VMEM((t, 128), F32),
                        pltpu.SMEM((2,), F32)],
        compiler_params=_params(("arbitrary", "arbitrary")))(q_all, kv_all, kv_all, qr, kr)


def _mla_bwd(q_all, kv_all, qr, kr, o, do, lse, scale, name):
    M = q_all.shape[0]
    t = ATTN_BWD_TILE

    def body(q_ref, kv_hbm, qr_ref, kr_hbm, o_ref, do_ref, lse_ref,
             dq_ref, dk_hbm, dv_hbm, dqr_ref, dkr_hbm,
             k_ref, v_ref, kr_ref, dk_ref, dv_ref, dkr_ref, dq_s, lse_s, delta_s):
        j = pl.program_id(0)
        i = pl.program_id(1)
        first, lane = _head_masks()
        every = pl.ds(0, M)
        kcols = pl.ds(pl.multiple_of(j * 128, 128), 128)
        vcols = pl.ds(pl.multiple_of((N_PAIRS + j) * 128, 128), 128)

        @pl.when(i == 0)
        def _():
            pltpu.sync_copy(kv_hbm.at[every, kcols], k_ref)
            pltpu.sync_copy(kv_hbm.at[every, vcols], v_ref)
            pltpu.sync_copy(kr_hbm, kr_ref)
            dk_ref[...] = jnp.zeros_like(dk_ref)
            dv_ref[...] = jnp.zeros_like(dv_ref)
            dkr_ref[...] = jnp.zeros_like(dkr_ref)

        qs = _split_heads(q_ref[...], first)
        qrs = _split_rope(qr_ref[...], lane)
        qcat = tuple(jnp.concatenate([qs[hh], qrs[hh]], axis=1) for hh in range(2))
        dov = do_ref[...]
        dos = _split_heads(dov, first)
        prod = dov.astype(F32) * o_ref[...].astype(F32)
        deltas = (jnp.sum(jnp.where(first, prod, 0.0), axis=1, keepdims=True),
                  jnp.sum(jnp.where(first, 0.0, prod), axis=1, keepdims=True))
        for hh in range(2):
            lse_s[hh] = jnp.broadcast_to(lse_ref[:, hh:hh + 1], (t, t))
            delta_s[hh] = jnp.broadcast_to(deltas[hh], (t, t))
        dq_s[...] = jnp.zeros_like(dq_s)
        qpos = i * t + lax.broadcasted_iota(jnp.int32, (t, t), 0)
        kidx = lax.broadcasted_iota(jnp.int32, (t, t), 1)

        def step(kb, masked):
            k0 = pl.multiple_of(kb * t, t)
            rows = pl.ds(k0, t)
            k = k_ref[rows, :]
            v = v_ref[rows, :]
            kr = kr_ref[rows, :]
            kcat = jnp.concatenate([k, kr], axis=1)
            ks = _split_heads(k, first)
            krs = _split_rope(kr, lane)
            if masked:
                kpos = k0 + kidx
                valid = (kpos <= qpos) & (kpos >= PAD)
            dq = dk = dv = None
            for hh in range(2):
                s = _dot_nt(qcat[hh], kcat) * scale
                if masked:
                    s = jnp.where(valid, s, NEG)
                p = jnp.exp(s - lse_s[hh])
                ds = p * (_dot_nt(dos[hh], v) - delta_s[hh])
                dsb = (ds * scale).astype(BF16)
                a = _dot(dsb, jnp.concatenate([ks[hh], krs[hh]], axis=1))
                b = _dot_tn(dsb, qcat[hh])
                c = _dot_tn(p.astype(BF16), dos[hh])
                dq = a if dq is None else dq + a
                dk = b if dk is None else dk + b
                dv = c if dv is None else dv + c
            dq_s[...] += dq
            dk_ref[rows, :] += dk[:, :128]
            dkr_ref[rows, :] += dk[:, 128:]
            dv_ref[rows, :] += dv

        _walk_causal(i, step)
        dq_ref[...] = dq_s[:, :128].astype(dq_ref.dtype)
        dqr_ref[...] = dq_s[:, 128:].astype(dqr_ref.dtype)

        @pl.when(i == M // t - 1)
        def _():
            pltpu.sync_copy(dk_ref, dk_hbm.at[every, kcols])
            pltpu.sync_copy(dv_ref, dv_hbm.at[every, kcols])
            pltpu.sync_copy(dkr_ref, dkr_hbm.at[j])

    blk = pl.BlockSpec((t, 128), lambda j, i: (i, j))
    whole = pl.BlockSpec(memory_space=pl.ANY)
    wide = jax.ShapeDtypeStruct((M, N_PAIRS * 128), F32)
    slab = lambda dtype: pltpu.VMEM((M, 128), dtype)
    return pl.pallas_call(
        body, name=name, grid=(N_PAIRS, M // t),
        in_specs=[blk, whole, blk, whole, blk, blk, pl.BlockSpec((None, t, 2), lambda j, i: (j, i, 0))],
        out_specs=[blk, whole, whole, blk, whole],
        out_shape=[jax.ShapeDtypeStruct((M, N_PAIRS * 128), BF16), wide, wide,
                   jax.ShapeDtypeStruct((M, N_PAIRS * 128), BF16), jax.ShapeDtypeStruct((N_PAIRS, M, 128), F32)],
        scratch_shapes=[slab(BF16), slab(BF16), slab(BF16), slab(F32), slab(F32), slab(F32),
                        pltpu.VMEM((t, 256), F32), pltpu.VMEM((2, t, t), F32), pltpu.VMEM((2, t, t), F32)],
        compiler_params=_params(("arbitrary", "arbitrary")))(q_all, kv_all, qr, kr, o, do, lse)


def _tri(t, rel):
    j = lax.broadcasted_iota(jnp.int32, (t, t), 0)
    k = lax.broadcasted_iota(jnp.int32, (t, t), 1)
    m = {"gt": j > k, "le": j <= k, "lt": j < k}[rel]
    return m.astype(BF16)


def _lane_cumsum(x, tri):
    hi = x.astype(BF16)
    lo = (x - hi.astype(F32)).astype(BF16)
    return _dot(hi, tri) + _dot(lo, tri)


def _log_sigmoids(z):
    sp = jnp.log(1.0 + jnp.exp(-jnp.abs(z)))
    return jnp.minimum(z, 0.0) - sp, jnp.minimum(-z, 0.0) - sp


def _log_sigmoids_fast(z):
    lk = -(jnp.maximum(z, 0.0) + jnp.log(1.0 + jnp.exp(-jnp.abs(z))))
    return lk + z, lk


def _sb_fwd(qkv, scale, name):
    M = qkv.shape[0]
    t = WALK_TILE
    ck, cv = N_PAIRS, 2 * N_PAIRS

    def body(q_ref, k_ref, v_ref, o_ref, tot_ref, c_s, acc_s):
        i = pl.program_id(1)
        first, _ = _head_masks()
        qs = _split_heads(q_ref[...], first)
        c_s[...] = jnp.zeros_like(c_s)
        acc_s[...] = jnp.zeros_like(acc_s)
        tri = _tri(t, "gt")
        qpos = i * t + lax.broadcasted_iota(jnp.int32, (t, t), 0)
        kidx = lax.broadcasted_iota(jnp.int32, (t, t), 1)

        def step(kb, masked):
            k0 = pl.multiple_of(kb * t, t)
            k = k_ref[pl.ds(k0, t), :]
            vs = _split_heads(v_ref[pl.ds(k0, t), :], first)
            if masked:
                kpos = k0 + kidx
                valid = (kpos < qpos) & (kpos >= PAD)
            pv = None
            for hh in range(2):
                z = _dot_nt(qs[hh], k) * scale
                lb, lk = _log_sigmoids_fast(z)
                if masked:
                    lk = jnp.where(valid, lk, 0.0)
                a = jnp.exp(lb + (c_s[hh] + _lane_cumsum(lk, tri)))
                if masked:
                    a = jnp.where(valid, a, 0.0)
                c_s[hh] = c_s[hh] + jnp.sum(lk, axis=1, keepdims=True)
                d = _dot(a.astype(BF16), vs[hh])
                pv = d if pv is None else pv + d
            acc_s[...] += pv

        def keep_going():
            return jnp.max(jnp.maximum(c_s[0], c_s[1])) > EXP_ZERO

        def cond(carry):
            kb, go, _ = carry
            return (kb >= 1) & go

        def walk(carry):
            kb, _, n = carry
            step(kb, False)
            return kb - 1, keep_going(), n + 1

        step(i, True)
        _, go, n = lax.while_loop(cond, walk, (i - 1, keep_going(), jnp.int32(1)))
        first_too = go & (i > 0)

        @pl.when(first_too)
        def _():
            step(0, True)

        walked = n + first_too.astype(jnp.int32)
        o_ref[...] = acc_s[...].astype(o_ref.dtype)
        tot_ref[:, 0:1] = c_s[0]
        tot_ref[:, 1:2] = c_s[1]
        tot_ref[:, 2:3] = jnp.full((t, 1), walked.astype(F32))

    whole = lambda c0: pl.BlockSpec((M, 128), lambda j, i: (0, c0 + j))
    return pl.pallas_call(
        body, name=name, grid=(N_PAIRS, M // t),
        in_specs=[pl.BlockSpec((t, 128), lambda j, i: (i, j)), whole(ck), whole(cv)],
        out_specs=[pl.BlockSpec((t, 128), lambda j, i: (i, j)), pl.BlockSpec((None, t, 3), lambda j, i: (j, i, 0))],
        out_shape=[jax.ShapeDtypeStruct((M, N_PAIRS * 128), BF16), jax.ShapeDtypeStruct((N_PAIRS, M, 3), F32)],
        scratch_shapes=[pltpu.VMEM((2, t, 1), F32), pltpu.VMEM((t, 128), F32)],
        compiler_params=_params(("parallel", "arbitrary")))(qkv, qkv, qkv)


def _sb_bwd(qkv, do, tot, scale, name):
    M = qkv.shape[0]
    t = WALK_TILE
    ck, cv = N_PAIRS, 2 * N_PAIRS

    def body(q_ref, k_ref, v_ref, do_ref, tot_ref, dq_ref, dk_ref, dv_ref, pc_s, dc_s, dq_s):
        i = pl.program_id(1)
        first, _ = _head_masks()

        @pl.when(i == 0)
        def _():
            dk_ref[...] = jnp.zeros_like(dk_ref)
            dv_ref[...] = jnp.zeros_like(dv_ref)

        qs = _split_heads(q_ref[...], first)
        dos = _split_heads(do_ref[...], first)
        pc_s[...] = jnp.zeros_like(pc_s)
        dc_s[...] = jnp.zeros_like(dc_s)
        dq_s[...] = jnp.zeros_like(dq_s)
        tri_le = _tri(t, "le")
        tri_lt = _tri(t, "lt")
        qpos = i * t + lax.broadcasted_iota(jnp.int32, (t, t), 0)
        kidx = lax.broadcasted_iota(jnp.int32, (t, t), 1)

        def step(kb, masked):
            k0 = pl.multiple_of(kb * t, t)
            rows = pl.ds(k0, t)
            k = k_ref[rows, :]
            v = v_ref[rows, :]
            ks = _split_heads(k, first)
            if masked:
                kpos = k0 + kidx
                valid = (kpos < qpos) & (kpos >= PAD)
            dq = dk = dv = None
            for hh in range(2):
                z = _dot_nt(qs[hh], k) * scale
                lb, lk = _log_sigmoids_fast(z)
                if masked:
                    lk = jnp.where(valid, lk, 0.0)
                later = tot_ref[:, hh:hh + 1] - (pc_s[hh] + _lane_cumsum(lk, tri_le))
                a = jnp.exp(lb + later)
                if masked:
                    a = jnp.where(valid, a, 0.0)
                dl = a * _dot_nt(dos[hh], v)
                early = dc_s[hh] + _lane_cumsum(dl, tri_lt)
                sg = jnp.exp(lb)
                dz = (dl * (1.0 - sg) - early * sg) * scale
                if masked:
                    dz = jnp.where(valid, dz, 0.0)
                pc_s[hh] = pc_s[hh] + jnp.sum(lk, axis=1, keepdims=True)
                dc_s[hh] = dc_s[hh] + jnp.sum(dl, axis=1, keepdims=True)
                dzb = dz.astype(BF16)
                x = _dot(dzb, ks[hh])
                y = _dot_tn(dzb, qs[hh])
                w = _dot_tn(a.astype(BF16), dos[hh])
                dq = x if dq is None else dq + x
                dk = y if dk is None else dk + y
                dv = w if dv is None else dv + w
            dq_s[...] += dq
            dk_ref[rows, :] += dk
            dv_ref[rows, :] += dv

        first_walked = i + 1 - jnp.max(tot_ref[:, 2:3]).astype(jnp.int32)

        def mid(kb, carry):
            step(kb, False)
            return carry

        @pl.when((first_walked == 0) & (i > 0))
        def _():
            step(0, True)

        lax.fori_loop(jnp.maximum(first_walked, 1), i, mid, 0)
        step(i, True)
        dq_ref[...] = dq_s[...].astype(dq_ref.dtype)

    whole = lambda c0: pl.BlockSpec((M, 128), lambda j, i: (0, c0 + j))
    blk = pl.BlockSpec((t, 128), lambda j, i: (i, j))
    col = pl.BlockSpec((M, 128), lambda j, i: (0, j))
    return pl.pallas_call(
        body, name=name, grid=(N_PAIRS, M // t),
        in_specs=[blk, whole(ck), whole(cv), blk, pl.BlockSpec((None, t, 3), lambda j, i: (j, i, 0))],
        out_specs=[blk, col, col],
        out_shape=[jax.ShapeDtypeStruct((M, N_PAIRS * 128), BF16), jax.ShapeDtypeStruct((M, N_PAIRS * 128), F32),
                   jax.ShapeDtypeStruct((M, N_PAIRS * 128), F32)],
        scratch_shapes=[pltpu.VMEM((2, t, 1), F32), pltpu.VMEM((2, t, 1), F32), pltpu.VMEM((t, 128), F32)],
        compiler_params=_params(("parallel", "arbitrary")))(qkv, qkv, qkv, do, tot)


def _rows_between(lo, hi):
    r = lax.broadcasted_iota(jnp.int32, (128, 1), 0)
    return (r >= lo) & (r < hi)


def _lanes_between(lo, hi):
    c = lax.broadcasted_iota(jnp.int32, (1, 128), 1)
    return (c >= lo) & (c < hi)


def _keep(x, mask):
    return jnp.where(mask, x, jnp.zeros_like(x))


def _valid_mask(i, kb, t):
    kpos = kb * t + lax.broadcasted_iota(jnp.int32, (t, t), 0)
    qpos = i * t + lax.broadcasted_iota(jnp.int32, (t, t), 1)
    return (kpos <= qpos) & (kpos >= PAD)


def _fox_fwd(qkv, qkv_t, f_rows, f_cols, scale, name):
    M = qkv.shape[0]
    t = FOX_TILE
    first_blk = PAD // t
    ck, cv = N_PAIRS, 2 * N_PAIRS

    def body(qt_ref, k_ref, vt_ref, fq_ref, fk_ref, o_ref, lse_ref, ox_ref, m_s, l_s, acc_s, accx_s, kmax_s, walked_s):
        i = pl.program_id(1)

        @pl.when(i == 0)
        def _():
            first = _lanes_between(0, 64)

            def block_max(kb, carry):
                kk = k_ref[pl.ds(pl.multiple_of(kb * t, t), t), :].astype(F32)
                kk = kk * kk
                a = jnp.max(jnp.sum(jnp.where(first, kk, 0.0), axis=1, keepdims=True))
                b = jnp.max(jnp.sum(jnp.where(first, 0.0, kk), axis=1, keepdims=True))
                return jnp.maximum(carry[0], a), jnp.maximum(carry[1], b)

            a, b = lax.fori_loop(0, M // t, block_max, (jnp.float32(0.0), jnp.float32(0.0)))
            kmax_s[0] = a
            kmax_s[1] = b

        qt = qt_ref[...]
        qts = (_keep(qt, _rows_between(0, 64)), _keep(qt, _rows_between(64, 128)))
        qf = qt.astype(F32)
        qf = qf * qf
        qbound = tuple(
            (1.001 * scale) * jnp.sqrt(jnp.sum(qf[HEAD_DIM * hh:HEAD_DIM * (hh + 1)], axis=0, keepdims=True) * kmax_s[hh])
            for hh in range(2))
        def run(online):
            l_s[...] = jnp.zeros_like(l_s)
            acc_s[...] = jnp.zeros_like(acc_s)
            accx_s[...] = jnp.zeros_like(accx_s)
            if online:
                m_s[...] = jnp.full_like(m_s, NEG)

            def step(kb, masked):
                k0 = pl.multiple_of(kb * t, t)
                rows = pl.ds(k0, t)
                k = k_ref[rows, :]
                if masked:
                    valid = _valid_mask(i, kb, t)
                for hh in range(2):
                    s = _dot(k, qts[hh]) * scale + (fq_ref[hh:hh + 1, :] - fk_ref[rows, hh:hh + 1])
                    hr = slice(HEAD_DIM * hh, HEAD_DIM * (hh + 1))
                    vt = vt_ref[hr, rows]
                    if online:
                        if masked:
                            s = jnp.where(valid, s, NEG)
                        m_old = m_s[hh]
                        m_new = jnp.maximum(m_old, jnp.max(s, axis=0, keepdims=True))
                        p = jnp.exp(s - m_new)
                        alpha = jnp.exp(m_old - m_new)
                        l_s[hh] = alpha * l_s[hh] + jnp.sum(p, axis=0, keepdims=True)
                        m_s[hh] = m_new
                        pb = p.astype(BF16)
                        acc_s[hr, :] = acc_s[hr, :] * alpha + _dot(vt, pb)
                        accx_s[hr, :] = accx_s[hr, :] * alpha + _dot(vt, (p - pb.astype(F32)).astype(BF16))
                    else:
                        p = jnp.exp(s - m_s[hh])
                        if masked:
                            p = jnp.where(valid, p, 0.0)
                        l_s[hh] = l_s[hh] + jnp.sum(p, axis=0, keepdims=True)
                        pb = p.astype(BF16)
                        acc_s[hr, :] += _dot(vt, pb)
                        accx_s[hr, :] += _dot(vt, (p - pb.astype(F32)).astype(BF16))

            def keep_going(kb):
                k0 = pl.multiple_of(kb * t, t)
                worst = None
                for hh in range(2):
                    f0 = jnp.max(fk_ref[pl.ds(k0, 8), hh:hh + 1])
                    decay = fq_ref[hh:hh + 1, :] - f0
                    if online:
                        w = jnp.max(qbound[hh] + decay - m_s[hh])
                    else:
                        w = jnp.max(decay - jnp.minimum(jnp.log(jnp.maximum(l_s[hh], 1e-37)), 0.0))
                    worst = w if worst is None else jnp.maximum(worst, w)
                return worst > EXP_ZERO

            def cond(carry):
                kb, go, _ = carry
                return (kb > first_blk) & go

            def walk(carry):
                kb, _, n = carry
                step(kb, False)
                return kb - 1, keep_going(kb), n + 1

            step(i, True)
            _, go, n = lax.while_loop(cond, walk, (i - 1, keep_going(i), jnp.int32(1)))
            first_too = go & (i > first_blk)

            @pl.when(first_too)
            def _():
                step(first_blk, True)

            walked_s[0] = n + first_too.astype(jnp.int32)

        for hh in range(2):
            m_s[hh] = qbound[hh]
        run(False)
        real = i * t + lax.broadcasted_iota(jnp.int32, (1, t), 1) >= PAD
        underflow = jnp.max(jnp.where(real & (jnp.minimum(l_s[0], l_s[1]) < 1e-30), 1.0, 0.0)) > 0.0

        @pl.when(underflow)
        def _():
            run(True)

        for hh in range(2):
            hr = slice(HEAD_DIM * hh, HEAD_DIM * (hh + 1))
            l = jnp.where(l_s[hh] > 0.0, l_s[hh], 1.0)
            inv = 1.0 / l
            o_ref[hr, :] = (acc_s[hr, :] * inv).astype(o_ref.dtype)
            ox_ref[hr, :] = (acc_s[hr, :] + accx_s[hr, :]) * inv
            lse_ref[hh:hh + 1, :] = m_s[hh] + jnp.log(l)
        lse_ref[2:3, :] = jnp.full((1, t), walked_s[0].astype(F32))

    blk = pl.BlockSpec((128, t), lambda j, i: (j, i))
    stat = pl.BlockSpec((None, 2, t), lambda j, i: (j, 0, i))
    return pl.pallas_call(
        body, name=name, grid=(N_PAIRS, M // t),
        in_specs=[blk, pl.BlockSpec((M, 128), lambda j, i: (0, ck + j)), pl.BlockSpec((128, M), lambda j, i: (cv + j, 0)),
                  stat, pl.BlockSpec((None, M, 2), lambda j, i: (j, 0, 0))],
        out_specs=[blk, pl.BlockSpec((None, 3, t), lambda j, i: (j, 0, i)), blk],
        out_shape=[jax.ShapeDtypeStruct((N_PAIRS * 128, M), BF16), jax.ShapeDtypeStruct((N_PAIRS, 3, M), F32),
                   jax.ShapeDtypeStruct((N_PAIRS * 128, M), F32)],
        scratch_shapes=[pltpu.VMEM((2, 1, t), F32), pltpu.VMEM((2, 1, t), F32), pltpu.VMEM((128, t), F32),
                        pltpu.VMEM((128, t), F32), pltpu.SMEM((2,), F32), pltpu.SMEM((1,), jnp.int32)],
        compiler_params=_params(("arbitrary", "arbitrary")))(qkv_t, qkv, qkv_t, f_rows, f_cols)


def _fox_bwd(qkv, qkv_t, o_t, do, do_t, lse, f_rows, f_cols, scale, name):
    M = qkv.shape[0]
    t = FOX_TILE
    first_blk = PAD // t
    ck, cv = N_PAIRS, 2 * N_PAIRS

    def body(q_ref, qt_ref, k_ref, kt_ref, v_ref, ot_ref, do_ref, dot_ref, lse_ref, fq_ref, fk_ref,
             dq_ref, dk_ref, dv_ref, cs_ref, dq_s):
        i = pl.program_id(1)

        @pl.when(i == 0)
        def _():
            dk_ref[...] = jnp.zeros_like(dk_ref)
            dv_ref[...] = jnp.zeros_like(dv_ref)
            cs_ref[...] = jnp.zeros_like(cs_ref)

        heads_l = (_lanes_between(0, 64), _lanes_between(64, 128))
        heads_r = (_rows_between(0, 64), _rows_between(64, 128))
        q = q_ref[...]
        qt = qt_ref[...]
        do = do_ref[...]
        dot = dot_ref[...]
        qs = tuple(_keep(q, m) for m in heads_l)
        qts = tuple(_keep(qt, m) for m in heads_r)
        dos = tuple(_keep(do, m) for m in heads_l)
        dots = tuple(_keep(dot, m) for m in heads_r)
        prod = dot.astype(F32) * ot_ref[...]
        deltas = tuple(jnp.sum(prod[HEAD_DIM * hh:HEAD_DIM * (hh + 1)], axis=0, keepdims=True) for hh in range(2))
        ones = tuple(m.astype(BF16) * jnp.ones((t, 128), BF16) for m in heads_l)
        dq_s[...] = jnp.zeros_like(dq_s)

        def step(kb, masked):
            k0 = pl.multiple_of(kb * t, t)
            rows = pl.ds(k0, t)
            k = k_ref[rows, :]
            v = v_ref[rows, :]
            if masked:
                valid = _valid_mask(i, kb, t)
            dk = dv = cs = None
            for hh in range(2):
                s = _dot(k, qts[hh]) * scale + (fq_ref[hh:hh + 1, :] - fk_ref[rows, hh:hh + 1])
                if masked:
                    s = jnp.where(valid, s, NEG)
                p = jnp.exp(s - lse_ref[hh:hh + 1, :])
                ds = p * (_dot(v, dots[hh]) - deltas[hh])
                hi = ds.astype(BF16)
                lo = (ds - hi.astype(F32)).astype(BF16)
                c = _dot(hi, ones[hh]) + _dot(lo, ones[hh])
                dsb = (ds * scale).astype(BF16)
                hr = slice(HEAD_DIM * hh, HEAD_DIM * (hh + 1))
                dq_s[hr, :] += _dot(kt_ref[hr, rows], dsb)
                a = _dot(dsb, qs[hh])
                b = _dot(p.astype(BF16), dos[hh])
                dk = a if dk is None else dk + a
                dv = b if dv is None else dv + b
                cs = c if cs is None else cs + c
            dk_ref[rows, :] += dk
            dv_ref[rows, :] += dv
            cs_ref[rows, :] += cs

        first_walked = i + 1 - jnp.max(lse_ref[2:3, :]).astype(jnp.int32)

        def mid(kb, carry):
            step(kb, False)
            return carry

        @pl.when((first_walked == first_blk) & (i > first_blk))
        def _():
            step(first_blk, True)

        lax.fori_loop(jnp.maximum(first_walked, first_blk + 1), i, mid, 0)
        step(i, True)
        dq_ref[...] = dq_s[...].astype(dq_ref.dtype)

    rblk = pl.BlockSpec((t, 128), lambda j, i: (i, j))
    tblk = pl.BlockSpec((128, t), lambda j, i: (j, i))
    stat = pl.BlockSpec((None, 2, t), lambda j, i: (j, 0, i))
    stat3 = pl.BlockSpec((None, 3, t), lambda j, i: (j, 0, i))
    col = pl.BlockSpec((M, 128), lambda j, i: (0, j))
    wide = jax.ShapeDtypeStruct((M, N_PAIRS * 128), F32)
    return pl.pallas_call(
        body, name=name, grid=(N_PAIRS, M // t),
        in_specs=[rblk, tblk, pl.BlockSpec((M, 128), lambda j, i: (0, ck + j)),
                  pl.BlockSpec((128, M), lambda j, i: (ck + j, 0)), pl.BlockSpec((M, 128), lambda j, i: (0, cv + j)),
                  tblk, rblk, tblk, stat3, stat, pl.BlockSpec((None, M, 2), lambda j, i: (j, 0, 0))],
        out_specs=[tblk, col, col, pl.BlockSpec((None, M, 128), lambda j, i: (j, 0, 0))],
        out_shape=[jax.ShapeDtypeStruct((N_PAIRS * 128, M), BF16), wide, wide,
                   jax.ShapeDtypeStruct((N_PAIRS, M, 128), F32)],
        scratch_shapes=[pltpu.VMEM((128, t), F32)],
        compiler_params=_params(("parallel", "arbitrary")))(qkv, qkv_t, qkv, qkv_t, qkv, o_t, do, do_t, lse,
                                                            f_rows, f_cols)


def _rope_tables(M):
    pos = (jnp.arange(M, dtype=jnp.int32) - PAD).astype(F32)
    inv = ROPE_THETA ** (-jnp.arange(0, MLA_ROPE, 2, dtype=F32) / MLA_ROPE)
    ang = pos[:, None] * inv[None, :]
    cos, sin = jnp.cos(ang), jnp.sin(ang)
    z = jnp.zeros((M, 64), F32)
    cos_t = jnp.concatenate([cos, cos, cos, cos, z], axis=1)
    sin_t = jnp.concatenate([-sin, sin, -sin, sin, z], axis=1)
    return cos_t, sin_t


def _rope(x, cos_t, sin_t, out_dtype, name, inverse=False, lead=0):
    M, C = x.shape
    tm = _pick(M, (768, 512, 256, 128))
    nblk = (C - lead) // 128
    sign = -1.0 if inverse else 1.0

    def body(x_ref, c_ref, s_ref, o_ref):
        lane = lax.broadcasted_iota(jnp.int32, (1, 128), 1)
        low = (lane % MLA_ROPE) < (MLA_ROPE // 2)
        cos = c_ref[...]
        sin = s_ref[...] * sign
        if lead:
            o_ref[:, :lead] = x_ref[:, :lead].astype(o_ref.dtype)
        for b in range(nblk):
            cols = slice(lead + b * 128, lead + (b + 1) * 128)
            v = x_ref[:, cols].astype(F32)
            up = pltpu.roll(v, 128 - MLA_ROPE // 2, 1)
            down = pltpu.roll(v, MLA_ROPE // 2, 1)
            o_ref[:, cols] = (v * cos + jnp.where(low, up, down) * sin).astype(o_ref.dtype)

    row = pl.BlockSpec((tm, C), lambda m: (m, 0))
    tab = pl.BlockSpec((tm, 128), lambda m: (m, 0))
    return pl.pallas_call(
        body, name=name, grid=(M // tm,), in_specs=[row, tab, tab], out_specs=row,
        out_shape=jax.ShapeDtypeStruct((M, C), out_dtype),
        compiler_params=_params(("parallel",)))(x, cos_t, sin_t)


def _forget_cumsum(f_logit, bias, name):
    M = f_logit.shape[0]
    tm = 256

    def body(f_ref, b_ref, o_ref, c_s):
        i = pl.program_id(0)

        @pl.when(i == 0)
        def _():
            c_s[...] = jnp.zeros_like(c_s)
        ls, _ = _log_sigmoids(f_ref[...] + b_ref[...])
        rows = i * tm + lax.broadcasted_iota(jnp.int32, (tm, 1), 0)
        ls = jnp.where(rows >= PAD, ls, 0.0)
        r = lax.broadcasted_iota(jnp.int32, (tm, tm), 0)
        c = lax.broadcasted_iota(jnp.int32, (tm, tm), 1)
        tri = (c <= r).astype(F32)
        cum = jnp.dot(tri, ls, precision=lax.Precision.HIGHEST, preferred_element_type=F32) + c_s[...]
        o_ref[...] = cum
        c_s[...] = cum[tm - 1:tm, :]

    row = pl.BlockSpec((tm, 128), lambda m: (m, 0))
    return pl.pallas_call(
        body, name=name, grid=(M // tm,),
        in_specs=[row, pl.BlockSpec((1, 128), lambda m: (0, 0))], out_specs=row,
        out_shape=jax.ShapeDtypeStruct((M, 128), F32), scratch_shapes=[pltpu.VMEM((1, 128), F32)],
        compiler_params=_params(("arbitrary",)))(f_logit, bias)


def _forget_cumsum_bwd(f_logit, bias, colsum, name):
    M = f_logit.shape[0]
    tm = 256
    nb = M // tm

    def body(f_ref, b_ref, cs_ref, o_ref, db_ref, c_s):
        i = pl.program_id(0)

        @pl.when(i == 0)
        def _():
            c_s[...] = jnp.zeros_like(c_s)
            db_ref[...] = jnp.zeros_like(db_ref)
        rr = lax.broadcasted_iota(jnp.int32, (128, 128), 0)
        cc = lax.broadcasted_iota(jnp.int32, (128, 128), 1)
        dF = None
        for j in range(N_PAIRS):
            sel = (((rr == 0) & (cc == 2 * j)) | ((rr == HEAD_DIM) & (cc == 2 * j + 1))).astype(F32)
            d = jnp.dot(cs_ref[j], sel, precision=lax.Precision.HIGHEST, preferred_element_type=F32)
            dF = d if dF is None else dF + d
        r = lax.broadcasted_iota(jnp.int32, (tm, tm), 0)
        c = lax.broadcasted_iota(jnp.int32, (tm, tm), 1)
        tri = (c >= r).astype(F32)
        cum = c_s[...] - jnp.dot(tri, dF, precision=lax.Precision.HIGHEST, preferred_element_type=F32)
        c_s[...] = cum[0:1, :]
        _, lsn = _log_sigmoids(f_ref[...] + b_ref[...])
        rows = (nb - 1 - i) * tm + lax.broadcasted_iota(jnp.int32, (tm, 1), 0)
        dl = jnp.where(rows >= PAD, cum * jnp.exp(lsn), 0.0)
        o_ref[...] = dl
        db_ref[...] += jnp.sum(dl, axis=0, keepdims=True)

    row = pl.BlockSpec((tm, 128), lambda m: (nb - 1 - m, 0))
    vec = pl.BlockSpec((1, 128), lambda m: (0, 0))
    return pl.pallas_call(
        body, name=name, grid=(nb,),
        in_specs=[row, vec, pl.BlockSpec((N_PAIRS, tm, 128), lambda m: (0, nb - 1 - m, 0))], out_specs=[row, vec],
        out_shape=[jax.ShapeDtypeStruct((M, 128), F32), jax.ShapeDtypeStruct((1, 128), F32)],
        scratch_shapes=[pltpu.VMEM((1, 128), F32)],
        compiler_params=_params(("arbitrary",)))(f_logit, bias, colsum)


def _adamw(w, parts, m, v, name):
    R, C = w.shape
    n_parts = parts.shape[0]
    tr = R
    for d in range(8, R, 8):
        if R % d == 0 and d * C <= ADAM_TILE_ELEMS:
            tr = d
    c1 = 1.0 - ADAM_B1 ** ADAM_STEP
    c2 = 1.0 - ADAM_B2 ** ADAM_STEP

    def body(w_ref, s_ref, m_ref, v_ref, g_ref, d_ref, mo_ref, vo_ref):
        g = s_ref[0].astype(F32)
        for k in range(1, n_parts):
            g = g + s_ref[k].astype(F32)
        mn = ADAM_B1 * m_ref[...] + (1.0 - ADAM_B1) * g
        vn = ADAM_B2 * v_ref[...] + (1.0 - ADAM_B2) * (g * g)
        m_hat = mn / c1
        v_hat = vn / c2
        g_ref[...] = g
        d_ref[...] = -ADAM_LR * (m_hat / (jnp.sqrt(v_hat) + ADAM_EPS) + ADAM_WD * w_ref[...])
        mo_ref[...] = mn
        vo_ref[...] = vn

    row = pl.BlockSpec((tr, C), lambda r: (r, 0))
    shp = jax.ShapeDtypeStruct((R, C), F32)
    return pl.pallas_call(
        body, name=name, grid=(R // tr,),
        in_specs=[row, pl.BlockSpec((n_parts, tr, C), lambda r: (0, r, 0)), row, row],
        out_specs=[row, row, row, row], out_shape=[shp, shp, shp, shp],
        compiler_params=_params(("parallel",)))(w, parts, m, v)


def _position():
    return lax.axis_index("x"), lax.axis_index("y"), lax.axis_index("c")


def _all_gather(blocks, name):
    n = len(blocks)

    def body(*refs):
        x_refs, out_refs = refs[:n], refs[n:2 * n]
        send_sems, recv_sems, local_sems = refs[2 * n:]
        x, y, c = _position()
        me, sibling = (x, y, c), (x, y, 1 - c)
        chips = [(1 - x, y), (x, 1 - y), (1 - x, 1 - y)]

        def copies(k, block, to, own=False):
            slot = 4 * block[0] + 2 * block[1] + block[2]
            return [pltpu.make_async_remote_copy(
                src_ref=x_refs[p] if own else out_refs[p].at[slot], dst_ref=out_refs[p].at[slot],
                send_sem=send_sems.at[k, p], recv_sem=recv_sems.at[k, p], device_id=to, device_id_type=MESH)
                for p in range(n)]

        mine = [pltpu.make_async_copy(x_refs[p], out_refs[p].at[4 * x + 2 * y + c], local_sems.at[p]) for p in range(n)]
        for cp in mine:
            cp.start()
        first = copies(0, me, sibling, own=True)
        for j, chip in enumerate(chips):
            first += copies(1 + j, me, (*chip, c), own=True)
        for cp in first:
            cp.start()
        passed = []
        for j, chip in enumerate(chips):
            for cp in copies(1 + j, (*chip, c), me):
                cp.wait_recv()
            onward = copies(4 + j, (*chip, c), sibling)
            for cp in onward:
                cp.start()
            passed += onward
        for cp in copies(0, sibling, me):
            cp.wait_recv()
        for j, chip in enumerate(chips):
            for cp in copies(4 + j, (*chip, 1 - c), me):
                cp.wait_recv()
        for cp in first + passed:
            cp.wait_send()
        for cp in mine:
            cp.wait()

    any_spec = pl.BlockSpec(memory_space=pl.ANY)
    return pl.pallas_call(
        body, name=name, out_shape=[jax.ShapeDtypeStruct((N_DEV,) + b.shape, b.dtype) for b in blocks],
        in_specs=[any_spec] * n, out_specs=[any_spec] * n,
        scratch_shapes=[pltpu.SemaphoreType.DMA((7, n)), pltpu.SemaphoreType.DMA((7, n)), pltpu.SemaphoreType.DMA((n,))],
    )(*blocks)


N_CHIPS = 4


def _exchange_siblings(parts, name):
    n = len(parts)

    def body(*refs):
        g_refs, land_refs = refs[:n], refs[n:2 * n]
        send_sems, recv_sems = refs[2 * n:]
        x, y, c = _position()
        sibling = (x, y, 1 - c)
        sends, recvs = [], []
        for q in range(N_CHIPS):
            for p in range(n):
                sends.append(pltpu.make_async_remote_copy(
                    src_ref=g_refs[p].at[2 * q + (1 - c)], dst_ref=land_refs[p].at[q], send_sem=send_sems.at[q, p],
                    recv_sem=recv_sems.at[q, p], device_id=sibling, device_id_type=MESH))
                recvs.append(pltpu.make_async_remote_copy(
                    src_ref=g_refs[p].at[2 * q + c], dst_ref=land_refs[p].at[q], send_sem=send_sems.at[q, p],
                    recv_sem=recv_sems.at[q, p], device_id=sibling, device_id_type=MESH))
        for cp in sends:
            cp.start()
        for cp in recvs:
            cp.wait_recv()
        for cp in sends:
            cp.wait_send()

    any_spec = pl.BlockSpec(memory_space=pl.ANY)
    return pl.pallas_call(
        body, name=name, out_shape=[jax.ShapeDtypeStruct((N_CHIPS,) + p.shape[1:], p.dtype) for p in parts],
        in_specs=[any_spec] * n, out_specs=[any_spec] * n,
        scratch_shapes=[pltpu.SemaphoreType.DMA((N_CHIPS, n)), pltpu.SemaphoreType.DMA((N_CHIPS, n))],
    )(*parts)


def _pair_sum(part, from_sibling, name):
    _, R, C = part.shape
    tr = R
    for d in range(8, R, 8):
        if R % d == 0 and d * C <= ADAM_TILE_ELEMS:
            tr = d

    def body(a_ref, b_ref, o_ref):
        c = lax.axis_index("c")
        for q in range(N_CHIPS):
            o_ref[q] = (a_ref[2 * q + c].astype(F32) + b_ref[q].astype(F32)).astype(o_ref.dtype)

    return pl.pallas_call(
        body, name=name, grid=(R // tr,),
        in_specs=[pl.BlockSpec((N_DEV, tr, C), lambda r: (0, r, 0)), pl.BlockSpec((N_CHIPS, tr, C), lambda r: (0, r, 0))],
        out_specs=pl.BlockSpec((N_CHIPS, tr, C), lambda r: (0, r, 0)),
        out_shape=jax.ShapeDtypeStruct((N_CHIPS, R, C), part.dtype),
        compiler_params=_params(("parallel",)))(part, from_sibling)


def _exchange_chips(sums, name):
    n = len(sums)

    def body(*refs):
        g_refs, land_refs = refs[:n], refs[n:2 * n]
        send_sems, recv_sems, local_sems = refs[2 * n:]
        x, y, c = _position()
        me = 2 * x + y
        mine = [pltpu.make_async_copy(g_refs[p].at[me], land_refs[p].at[me], local_sems.at[p]) for p in range(n)]
        for cp in mine:
            cp.start()
        sends, recvs = [], []
        for k in range(1, N_CHIPS):
            px = 1 - x if k & 2 else x
            py = 1 - y if k & 1 else y
            peer = 2 * px + py
            for p in range(n):
                sends.append(pltpu.make_async_remote_copy(
                    src_ref=g_refs[p].at[peer], dst_ref=land_refs[p].at[me], send_sem=send_sems.at[k - 1, p],
                    recv_sem=recv_sems.at[k - 1, p], device_id=(px, py, c), device_id_type=MESH))
                recvs.append(pltpu.make_async_remote_copy(
                    src_ref=g_refs[p].at[me], dst_ref=land_refs[p].at[peer], send_sem=send_sems.at[k - 1, p],
                    recv_sem=recv_sems.at[k - 1, p], device_id=(px, py, c), device_id_type=MESH))
        for cp in sends:
            cp.start()
        for cp in recvs:
            cp.wait_recv()
        for cp in sends:
            cp.wait_send()
        for cp in mine:
            cp.wait()

    any_spec = pl.BlockSpec(memory_space=pl.ANY)
    return pl.pallas_call(
        body, name=name, out_shape=[jax.ShapeDtypeStruct(p.shape, p.dtype) for p in sums],
        in_specs=[any_spec] * n, out_specs=[any_spec] * n,
        scratch_shapes=[pltpu.SemaphoreType.DMA((3, n)), pltpu.SemaphoreType.DMA((3, n)), pltpu.SemaphoreType.DMA((n,))],
    )(*sums)


SHARDED = (("sb_w_qkv", 2), ("sb_w_o", 1), ("mla_w_down", 1), ("mla_w_uq", 2), ("mla_w_ukv", 2), ("mla_w_o", 1),
           ("fox_w_qkvf", 2), ("fox_w_o", 1), ("ffn_w_gate", 2), ("ffn_w_up", 2), ("ffn_w_down", 1),
           ("pool_w", 2), ("meta", 1), ("mla_q_norm", 1), ("mla_kv_norm", 1))
KEPT_F32 = ("meta", "mla_q_norm", "mla_kv_norm")
REPLICATED = ("norm_mix", "norm_ffn", "pool_scale", "fox_b_f", "final_norm")
WEIGHT_NAMES = ("meta", "norm_mix", "norm_ffn", "pool_w", "pool_scale", "sb_w_qkv", "sb_w_o", "mla_w_down",
                "mla_q_norm", "mla_kv_norm", "mla_w_uq", "mla_w_ukv", "mla_w_o", "fox_w_qkvf", "fox_b_f",
                "fox_w_o", "ffn_w_gate", "ffn_w_up", "ffn_w_down", "final_norm")
LANES = 1024


def _pack_rows(arrays, names):
    parts = []
    for n in names:
        flat = arrays[n].reshape(-1).astype(F32)
        rows = -(-flat.shape[0] // LANES)
        parts.append(jnp.pad(flat, (0, rows * LANES - flat.shape[0])).reshape(rows, LANES))
    rows = sum(p.shape[0] for p in parts)
    parts.append(jnp.zeros((-(-rows // 8) * 8 - rows, LANES), F32))
    return jnp.concatenate(parts, axis=0)


def _unpack_rows(buf, shapes, names):
    out, row = {}, 0
    for n in names:
        size = int(np.prod(shapes[n]))
        rows = -(-size // LANES)
        out[n] = buf[row:row + rows].reshape(-1)[:size].reshape(shapes[n])
        row += rows
    return out


def _whole_from_gathered(g, axis):
    g = jnp.moveaxis(g, 0, axis)
    shp = g.shape
    return g.reshape(shp[:axis] + (shp[axis] * shp[axis + 1],) + shp[axis + 2:])


def _parts_from_whole(whole, axis):
    shp = whole.shape
    g = whole.reshape(shp[:axis] + (N_DEV, shp[axis] // N_DEV) + shp[axis + 1:])
    return jnp.moveaxis(g, axis, 0)


def _kernel_weights(full):
    W = {}
    W["pool_w"] = full["pool_w"][0]
    W["sb_w_qkv"] = full["sb_w_qkv"][0]
    W["sb_w_o"] = full["sb_w_o"][0]
    W["mla_w_down"] = full["mla_w_down"][0]
    uq = full["mla_w_uq"][0].reshape(MLA_Q_RANK, N_HEADS, MLA_NOPE + MLA_ROPE)
    nope = uq[:, :, :MLA_NOPE].reshape(MLA_Q_RANK, N_HEADS * MLA_NOPE)
    rope = uq[:, :, MLA_NOPE:].reshape(MLA_Q_RANK, N_PAIRS, 2 * MLA_ROPE)
    rope = jnp.pad(rope, ((0, 0), (0, 0), (0, 128 - 2 * MLA_ROPE))).reshape(MLA_Q_RANK, N_PAIRS * 128)
    W["mla_w_uq"] = jnp.concatenate([nope, rope], axis=1)
    ukv = full["mla_w_ukv"][0].reshape(MLA_KV_RANK, N_HEADS, 2, HEAD_DIM)
    W["mla_w_ukv"] = jnp.transpose(ukv, (0, 2, 1, 3)).reshape(MLA_KV_RANK, 2 * N_HEADS * HEAD_DIM)
    W["mla_w_o"] = full["mla_w_o"][0]
    qkvf = full["fox_w_qkvf"][0]
    n_qkv = 3 * N_HEADS * HEAD_DIM
    W["fox_w_qkv"] = qkvf[:, :n_qkv]
    W["fox_w_f"] = jnp.pad(qkvf[:, n_qkv:], ((0, 0), (0, 128 - N_HEADS)))
    W["fox_w_qkvf"] = jnp.concatenate([W["fox_w_qkv"], W["fox_w_f"]], axis=1)
    W["fox_w_o"] = full["fox_w_o"][0]
    W["ffn_w_gate"] = full["ffn_w_gate"]
    W["ffn_w_up"] = full["ffn_w_up"]
    W["ffn_w_down"] = full["ffn_w_down"]
    return W


def _reference_grads(G):
    out = {}
    out["pool_w"] = G["pool_w"][None]
    for n in ("sb_w_qkv", "sb_w_o", "mla_w_down", "mla_w_o", "fox_w_o"):
        out[n] = G[n][None]
    duq = G["mla_w_uq"]
    nope = duq[:, :N_HEADS * MLA_NOPE].reshape(MLA_Q_RANK, N_HEADS, MLA_NOPE)
    rope = duq[:, N_HEADS * MLA_NOPE:].reshape(MLA_Q_RANK, N_PAIRS, 128)[:, :, :2 * MLA_ROPE]
    rope = rope.reshape(MLA_Q_RANK, N_HEADS, MLA_ROPE)
    out["mla_w_uq"] = jnp.concatenate([nope, rope], axis=2).reshape(1, MLA_Q_RANK, -1)
    dukv = G["mla_w_ukv"].reshape(MLA_KV_RANK, 2, N_HEADS, HEAD_DIM)
    out["mla_w_ukv"] = jnp.transpose(dukv, (0, 2, 1, 3)).reshape(1, MLA_KV_RANK, -1)
    out["fox_w_qkvf"] = G["fox_w_qkvf"][None, :, :3 * N_HEADS * HEAD_DIM + N_HEADS]
    out["ffn_w_gate"] = G["ffn_w_gate"]
    out["ffn_w_up"] = G["ffn_w_up"]
    out["ffn_w_down"] = G["ffn_w_down"]
    out["mla_q_norm"] = G["mla_q_norm"]
    out["mla_kv_norm"] = G["mla_kv_norm"]
    return out


def _pairs_col(f16):
    M = f16.shape[0]
    return jnp.transpose(f16.reshape(M, N_PAIRS, 2), (1, 0, 2))


def _pairs_row(f16):
    M = f16.shape[0]
    return jnp.transpose(f16.reshape(M, N_PAIRS, 2), (1, 2, 0))


def _local_step(x, target, W, P):
    S = x.shape[0]
    M = S + ROW0
    G = {}
    gain = lambda name, i: P[name][i][None, :]
    h0 = jnp.concatenate([jnp.zeros((PAD, D_MODEL), F32), P["meta"], x], axis=0)

    def ffn_fwd(h1, i):
        b = _norm_fwd(h1, gain("norm_ffn", i), BF16, f"ffn{i}_norm")
        g, u, act = _ffn_up(b, W["ffn_w_gate"][i], W["ffn_w_up"][i], f"ffn{i}_up")
        h2 = _mm_nn(act, W["ffn_w_down"][i], F32, f"ffn{i}_down", res=h1)
        return h2, (h1, b, g, u, act)

    def ffn_bwd(dh2, saved, i):
        h1, b, g, u, act = saved
        dg, du = _ffn_dact(dh2, W["ffn_w_down"][i], g, u, f"ffn{i}_dact")
        G.setdefault("ffn_w_down", {})[i] = _mm_tn(act, dh2, f"ffn{i}_dwd")
        db = _mm_nt2(dg, W["ffn_w_gate"][i], du, W["ffn_w_up"][i], F32, f"ffn{i}_db")
        G.setdefault("ffn_w_gate", {})[i] = _mm_tn(b, dg, f"ffn{i}_dwg")
        G.setdefault("ffn_w_up", {})[i] = _mm_tn(b, du, f"ffn{i}_dwu")
        dh1, dgain = _norm_bwd(h1, gain("norm_ffn", i), db, dh2, f"ffn{i}_dnorm")
        G.setdefault("norm_ffn", {})[i] = dgain
        return dh1

    a0 = _norm_fwd(h0, gain("norm_mix", 0), F32, "mix0_norm")
    h1_0, pooled = _pool_fwd(h0, a0, W["pool_w"], P["pool_scale"], "pool_fwd")
    h_1, ffn0 = ffn_fwd(h1_0, 0)

    sb_scale = HEAD_DIM ** -0.5
    a1 = _norm_fwd(h_1, gain("norm_mix", 1), BF16, "mix1_norm")
    sb_qkv = _mm_nn(a1, W["sb_w_qkv"], BF16, "sb_qkv")
    sb_o, sb_tot = _sb_fwd(sb_qkv, sb_scale, "sb_fwd")
    h1_1 = _mm_nn(sb_o, W["sb_w_o"], F32, "sb_out", res=h_1)
    h_2, ffn1 = ffn_fwd(h1_1, 1)

    mla_scale = (MLA_NOPE + MLA_ROPE) ** -0.5
    cos_t, sin_t = _rope_tables(M)
    a2 = _norm_fwd(h_2, gain("norm_mix", 2), BF16, "mix2_norm")
    down = _mm_nn(a2, W["mla_w_down"], F32, "mla_down")
    dq_raw = down[:, :MLA_Q_RANK]
    dkv_raw = down[:, MLA_Q_RANK:MLA_Q_RANK + MLA_KV_RANK]
    kr_raw = down[:, MLA_Q_RANK + MLA_KV_RANK:]
    c_q = _norm_fwd(dq_raw, P["mla_q_norm"], BF16, "mla_qnorm")
    c_kv = _norm_fwd(dkv_raw, P["mla_kv_norm"], BF16, "mla_kvnorm")
    q_lin = _mm_nn(c_q, W["mla_w_uq"], F32, "mla_uq")
    q_all = _rope(q_lin, cos_t, sin_t, BF16, "mla_qrope", lead=D_MODEL)
    kv_all = _mm_nn(c_kv, W["mla_w_ukv"], BF16, "mla_ukv")
    kr_in = jnp.concatenate([kr_raw, kr_raw, jnp.zeros((M, 64), F32)], axis=1)
    kr = _rope(kr_in, cos_t, sin_t, BF16, "mla_krope")
    q_rope = q_all[:, D_MODEL:]
    mla_o, mla_lse = _mla_fwd(q_all, kv_all, q_rope, kr, mla_scale, "mla_fwd")
    h1_2 = _mm_nn(mla_o, W["mla_w_o"], F32, "mla_out", res=h_2)
    h_3, ffn2 = ffn_fwd(h1_2, 2)

    fox_scale = HEAD_DIM ** -0.5
    a3 = _norm_fwd(h_3, gain("norm_mix", 3), BF16, "mix3_norm")
    fox_qkv = _mm_nn(a3, W["fox_w_qkv"], BF16, "fox_qkv")
    f_logit = _mm_nn(a3, W["fox_w_f"], F32, "fox_f")
    b_f = jnp.pad(P["fox_b_f"], ((0, 0), (0, 128 - N_HEADS)))
    Fc = _forget_cumsum(f_logit, b_f, "fox_cumsum")
    f_rows, f_cols = _pairs_row(Fc[:, :N_HEADS]), _pairs_col(Fc[:, :N_HEADS])
    fox_qkv_t = fox_qkv.T
    fox_o_t, fox_lse, fox_ox_t = _fox_fwd(fox_qkv, fox_qkv_t, f_rows, f_cols, fox_scale, "fox_fwd")
    fox_o = fox_o_t.T
    h1_3 = _mm_nn(fox_o, W["fox_w_o"], F32, "fox_out", res=h_3)
    h_4, ffn3 = ffn_fwd(h1_3, 3)

    sq, dh, dgain = _loss_head(h_4, P["final_norm"][None, :], target, "loss_head")
    G["final_norm"] = dgain[0]

    dh = ffn_bwd(dh, ffn3, 3)
    do = _mm_nt(dh, W["fox_w_o"], BF16, "fox_do")
    G["fox_w_o"] = _mm_tn(fox_o, dh, "fox_dwo")
    dq_t, dk, dv, colsum = _fox_bwd(fox_qkv, fox_qkv_t, fox_ox_t, do, do.T, fox_lse, f_rows, f_cols, fox_scale,
                                    "fox_bwd")
    dlogit, db_f = _forget_cumsum_bwd(f_logit, b_f, colsum, "fox_dcumsum")
    G["fox_b_f"] = db_f[:, :N_HEADS]
    dproj = jnp.concatenate([dq_t.T, dk.astype(BF16), dv.astype(BF16), dlogit.astype(BF16)], axis=1)
    da = _mm_nt(dproj, W["fox_w_qkvf"], F32, "fox_da")
    G["fox_w_qkvf"] = _mm_tn(a3, dproj, "fox_dwqkvf")
    dh, dgain = _norm_bwd(h_3, gain("norm_mix", 3), da, dh, "mix3_dnorm")
    G.setdefault("norm_mix", {})[3] = dgain

    dh = ffn_bwd(dh, ffn2, 2)
    do = _mm_nt(dh, W["mla_w_o"], BF16, "mla_do")
    G["mla_w_o"] = _mm_tn(mla_o, dh, "mla_dwo")
    dq, dk, dv, dqr, dkr = _mla_bwd(q_all, kv_all, q_rope, kr, mla_o, do, mla_lse, mla_scale, "mla_bwd")
    dqr = _rope(dqr, cos_t, sin_t, BF16, "mla_dqrope", inverse=True)
    dq_all = jnp.concatenate([dq, dqr], axis=1)
    dkr_sum = _rope(jnp.sum(dkr, axis=0), cos_t, sin_t, F32, "mla_dkrope", inverse=True)
    dkr_raw = dkr_sum[:, :MLA_ROPE] + dkr_sum[:, MLA_ROPE:2 * MLA_ROPE]
    dkv_all = jnp.concatenate([dk.astype(BF16), dv.astype(BF16)], axis=1)
    dc_q = _mm_nt(dq_all, W["mla_w_uq"], F32, "mla_dcq")
    G["mla_w_uq"] = _mm_tn(c_q, dq_all, "mla_dwuq")
    dc_kv = _mm_nt(dkv_all, W["mla_w_ukv"], F32, "mla_dckv")
    G["mla_w_ukv"] = _mm_tn(c_kv, dkv_all, "mla_dwukv")
    ddq_raw, G["mla_q_norm"] = _norm_bwd(dq_raw, P["mla_q_norm"], dc_q, None, "mla_dqnorm")
    ddkv_raw, G["mla_kv_norm"] = _norm_bwd(dkv_raw, P["mla_kv_norm"], dc_kv, None, "mla_dkvnorm")
    ddown = jnp.concatenate([ddq_raw, ddkv_raw, dkr_raw], axis=1).astype(BF16)
    da = _mm_nt(ddown, W["mla_w_down"], F32, "mla_da")
    G["mla_w_down"] = _mm_tn(a2, ddown, "mla_dwdown")
    dh, dgain = _norm_bwd(h_2, gain("norm_mix", 2), da, dh, "mix2_dnorm")
    G["norm_mix"][2] = dgain

    dh = ffn_bwd(dh, ffn1, 1)
    do = _mm_nt(dh, W["sb_w_o"], BF16, "sb_do")
    G["sb_w_o"] = _mm_tn(sb_o, dh, "sb_dwo")
    dq, dk, dv = _sb_bwd(sb_qkv, do, sb_tot, sb_scale, "sb_bwd")
    dqkv = jnp.concatenate([dq, dk.astype(BF16), dv.astype(BF16)], axis=1)
    da = _mm_nt(dqkv, W["sb_w_qkv"], F32, "sb_da")
    G["sb_w_qkv"] = _mm_tn(a1, dqkv, "sb_dwqkv")
    dh, dgain = _norm_bwd(h_1, gain("norm_mix", 1), da, dh, "mix1_dnorm")
    G["norm_mix"][1] = dgain

    dh = ffn_bwd(dh, ffn0, 0)
    dpc, G["pool_w"], G["pool_scale"] = _pool_bwd_mix(dh, pooled, W["pool_w"], P["pool_scale"], "pool_dmix")
    da = _pool_bwd_window(dpc, "pool_dwindow")
    dh, dgain, dx = _norm_bwd(h0, gain("norm_mix", 0), da, dh, "mix0_dnorm", token_rows=True)
    G["norm_mix"][0] = dgain

    G["norm_mix"] = jnp.concatenate([G["norm_mix"][i] for i in range(DEPTH)], axis=0)
    G["norm_ffn"] = jnp.concatenate([G["norm_ffn"][i] for i in range(DEPTH)], axis=0)
    G["ffn_w_down"] = jnp.stack([G["ffn_w_down"][i] for i in range(DEPTH)])
    G["ffn_w_gate"] = jnp.stack([G["ffn_w_gate"][i] for i in range(DEPTH)])
    G["ffn_w_up"] = jnp.stack([G["ffn_w_up"][i] for i in range(DEPTH)])
    G["meta"] = dh[PAD:ROW0]
    return sq, dx, G


def kernel(x, meta, norm_mix, norm_ffn, pool_w, pool_scale, sb_w_qkv, sb_w_o, mla_w_down, mla_q_norm, mla_kv_norm, mla_w_uq, mla_w_ukv, mla_w_o, fox_w_qkvf, fox_b_f, fox_w_o, ffn_w_gate, ffn_w_up, ffn_w_down, final_norm, loss_target, m_meta, m_norm_mix, m_norm_ffn, m_pool_w, m_pool_scale, m_sb_w_qkv, m_sb_w_o, m_mla_w_down, m_mla_q_norm, m_mla_kv_norm, m_mla_w_uq, m_mla_w_ukv, m_mla_w_o, m_fox_w_qkvf, m_fox_b_f, m_fox_w_o, m_ffn_w_gate, m_ffn_w_up, m_ffn_w_down, m_final_norm, v_meta, v_norm_mix, v_norm_ffn, v_pool_w, v_pool_scale, v_sb_w_qkv, v_sb_w_o, v_mla_w_down, v_mla_q_norm, v_mla_kv_norm, v_mla_w_uq, v_mla_w_ukv, v_mla_w_o, v_fox_w_qkvf, v_fox_b_f, v_fox_w_o, v_ffn_w_gate, v_ffn_w_up, v_ffn_w_down, v_final_norm):
    w = dict(meta=meta, norm_mix=norm_mix, norm_ffn=norm_ffn, pool_w=pool_w, pool_scale=pool_scale,
             sb_w_qkv=sb_w_qkv, sb_w_o=sb_w_o, mla_w_down=mla_w_down, mla_q_norm=mla_q_norm,
             mla_kv_norm=mla_kv_norm, mla_w_uq=mla_w_uq, mla_w_ukv=mla_w_ukv, mla_w_o=mla_w_o,
             fox_w_qkvf=fox_w_qkvf, fox_b_f=fox_b_f, fox_w_o=fox_w_o, ffn_w_gate=ffn_w_gate, ffn_w_up=ffn_w_up,
             ffn_w_down=ffn_w_down, final_norm=final_norm)
    m = dict(meta=m_meta, norm_mix=m_norm_mix, norm_ffn=m_norm_ffn, pool_w=m_pool_w, pool_scale=m_pool_scale,
             sb_w_qkv=m_sb_w_qkv, sb_w_o=m_sb_w_o, mla_w_down=m_mla_w_down, mla_q_norm=m_mla_q_norm,
             mla_kv_norm=m_mla_kv_norm, mla_w_uq=m_mla_w_uq, mla_w_ukv=m_mla_w_ukv, mla_w_o=m_mla_w_o,
             fox_w_qkvf=m_fox_w_qkvf, fox_b_f=m_fox_b_f, fox_w_o=m_fox_w_o, ffn_w_gate=m_ffn_w_gate,
             ffn_w_up=m_ffn_w_up, ffn_w_down=m_ffn_w_down, final_norm=m_final_norm)
    v = dict(meta=v_meta, norm_mix=v_norm_mix, norm_ffn=v_norm_ffn, pool_w=v_pool_w, pool_scale=v_pool_scale,
             sb_w_qkv=v_sb_w_qkv, sb_w_o=v_sb_w_o, mla_w_down=v_mla_w_down, mla_q_norm=v_mla_q_norm,
             mla_kv_norm=v_mla_kv_norm, mla_w_uq=v_mla_w_uq, mla_w_ukv=v_mla_w_ukv, mla_w_o=v_mla_w_o,
             fox_w_qkvf=v_fox_w_qkvf, fox_b_f=v_fox_b_f, fox_w_o=v_fox_w_o, ffn_w_gate=v_ffn_w_gate,
             ffn_w_up=v_ffn_w_up, ffn_w_down=v_ffn_w_down, final_norm=v_final_norm)

    sh_names = tuple(n for n, _ in SHARDED)
    sh_axis = dict(SHARDED)
    shapes = {n: w[n].shape for n in WEIGHT_NAMES}
    wire = lambda n: F32 if n in KEPT_F32 else BF16

    gathered = _all_gather([w[n].astype(wire(n)) for n in sh_names], "gather_weights")
    full = {n: _whole_from_gathered(g, sh_axis[n]) for n, g in zip(sh_names, gathered)}
    W = _kernel_weights(full)
    P = dict(meta=full["meta"], mla_q_norm=full["mla_q_norm"], mla_kv_norm=full["mla_kv_norm"],
             norm_mix=norm_mix, norm_ffn=norm_ffn, pool_scale=pool_scale, fox_b_f=fox_b_f, final_norm=final_norm)

    sq, dx, G = _local_step(x[0], loss_target[0], W, P)
    loss = lax.psum(0.5 * jnp.sum(sq) / D_MODEL, ("x", "y", "c"))
    grad_x = dx[None]

    gw = _reference_grads(G)
    gw["meta"] = G["meta"]
    rc = {n: (int(np.prod(shapes[n][:-1])), shapes[n][-1]) for n in sh_names}
    parts = [_parts_from_whole(gw[n], sh_axis[n]).astype(wire(n)).reshape((N_DEV,) + rc[n]) for n in sh_names]
    from_sibling = _exchange_siblings(parts, "exchange_grads_d2d")
    sums = [_pair_sum(a, b, f"pair_sum_{n}") for n, a, b in zip(sh_names, parts, from_sibling)]
    landed = _exchange_chips(sums, "exchange_grads_ici")
    results = {}
    for n, got in zip(sh_names, landed):
        outs = _adamw(w[n].reshape(rc[n]), got, m[n].reshape(rc[n]), v[n].reshape(rc[n]), f"adamw_{n}")
        results[n] = [o.reshape(shapes[n]) for o in outs]

    rep_g = dict(norm_mix=G["norm_mix"], norm_ffn=G["norm_ffn"], pool_scale=G["pool_scale"], fox_b_f=G["fox_b_f"],
                 final_norm=G["final_norm"])
    (rep_all,) = _all_gather([_pack_rows(rep_g, REPLICATED)], "gather_replicated_grads")
    rep_out = _adamw(_pack_rows(w, REPLICATED), rep_all, _pack_rows(m, REPLICATED), _pack_rows(v, REPLICATED),
                     "adamw_replicated")
    rep = [_unpack_rows(o, shapes, REPLICATED) for o in rep_out]
    for n in REPLICATED:
        results[n] = [r[n] for r in rep]

    outs = [results[n][k] for k in range(4) for n in WEIGHT_NAMES]
    return (loss, grad_x, *outs)
```

```python
import numpy as np
import jax
import jax.numpy as jnp
from jax import lax
from jax.experimental import pallas as pl
from jax.experimental.pallas import tpu as pltpu

F32 = jnp.float32
BF16 = jnp.bfloat16

N_DEV = 8
D_MODEL = 1024
N_META = 16
PAD = 240
ROW0 = PAD + N_META
EPS = 1e-6
POOL_WINDOWS = (2, 4, 8, 16)
POOL_GROUP = 256
HALO = 128
N_HEADS = 16
HEAD_DIM = 64
N_PAIRS = N_HEADS // 2
MLA_Q_RANK = 384
MLA_KV_RANK = 256
MLA_NOPE = 64
MLA_ROPE = 32
ROPE_THETA = 10000.0
D_FF = 2816
DEPTH = 4
ATTN_TILE = 768
ATTN_BWD_TILE = 768
WALK_TILE = 256
FOX_TILE = 384
NEG = -1e30
LOG2E = 1.4426950408889634
EXP_ZERO = -110.0
VMEM_LIMIT = 56 * 2**20
ADAM_TILE_ELEMS = 192 * 1024

ADAM_LR = 0.001
ADAM_B1 = 0.9
ADAM_B2 = 0.999
ADAM_EPS = 1e-08
ADAM_WD = 0.01
ADAM_STEP = 10

MESH = pl.DeviceIdType.MESH


def _params(sem=None):
    return pltpu.CompilerParams(dimension_semantics=sem, vmem_limit_bytes=VMEM_LIMIT)


def _pick(n, cands):
    for c in cands:
        if n % c == 0:
            return c
    return n


def _col_tile(n, cap=1536):
    best = None
    for t in range(128, min(n, cap) + 1, 128):
        if n % t == 0:
            best = t
    return best if best is not None else n


def _dot(a, b):
    return jnp.dot(a, b, preferred_element_type=F32)


def _dot_nt(a, b):
    return lax.dot_general(a, b, (((1,), (1,)), ((), ())), preferred_element_type=F32)


def _dot_tn(a, b):
    return lax.dot_general(a, b, (((0,), (0,)), ((), ())), preferred_element_type=F32)


def _mm_nn(a, b, out_dtype, name, res=None):
    M, K = a.shape
    N = b.shape[1]
    tm = _pick(M, (768, 512, 256, 128))
    tn = _col_tile(N)

    def body(*refs):
        if res is None:
            a_ref, b_ref, o_ref = refs
        else:
            a_ref, b_ref, r_ref, o_ref = refs
        acc = _dot(a_ref[...].astype(BF16), b_ref[...])
        if res is not None:
            acc = acc + r_ref[...]
        o_ref[...] = acc.astype(o_ref.dtype)

    in_specs = [pl.BlockSpec((tm, K), lambda n, m: (m, 0)), pl.BlockSpec((K, tn), lambda n, m: (0, n))]
    args = [a, b]
    if res is not None:
        in_specs.append(pl.BlockSpec((tm, tn), lambda n, m: (m, n)))
        args.append(res)
    return pl.pallas_call(
        body, name=name, grid=(N // tn, M // tm), in_specs=in_specs,
        out_specs=pl.BlockSpec((tm, tn), lambda n, m: (m, n)),
        out_shape=jax.ShapeDtypeStruct((M, N), out_dtype),
        compiler_params=_params(("parallel", "parallel")))(*args)


def _mm_nt(a, w, out_dtype, name):
    M, N = a.shape
    K = w.shape[0]
    tm = _pick(M, (768, 512, 256, 128)) if N <= 3200 else _pick(M, (256, 128))
    tk = _col_tile(K, 1024)

    def body(a_ref, w_ref, o_ref):
        o_ref[...] = _dot_nt(a_ref[...].astype(BF16), w_ref[...]).astype(o_ref.dtype)

    return pl.pallas_call(
        body, name=name, grid=(K // tk, M // tm),
        in_specs=[pl.BlockSpec((tm, N), lambda k, m: (m, 0)), pl.BlockSpec((tk, N), lambda k, m: (k, 0))],
        out_specs=pl.BlockSpec((tm, tk), lambda k, m: (m, k)),
        out_shape=jax.ShapeDtypeStruct((M, K), out_dtype),
        compiler_params=_params(("parallel", "parallel")))(a, w)


def _mm_nt_dnorm(pairs, h, gain, dres, name):
    M, K = h.shape
    n = len(pairs)
    tm = _pick(M, (384, 256, 128))

    def body(*refs):
        refs = list(refs)
        ab, rest = refs[:2 * n], refs[2 * n:]
        h_ref, g_ref = rest[0], rest[1]
        dr_ref = rest[2] if dres is not None else None
        dh_ref, dg_ref = rest[-2], rest[-1]
        da = None
        for i in range(n):
            d = _dot_nt(ab[2 * i][...], ab[2 * i + 1][...])
            da = d if da is None else da + d
        x = h_ref[...]
        r = lax.rsqrt(jnp.mean(x * x, axis=-1, keepdims=True) + EPS)
        y = x * r
        dy = da * g_ref[...]
        dh = r * (dy - y * jnp.mean(dy * y, axis=-1, keepdims=True))
        if dr_ref is not None:
            dh = dh + dr_ref[...]
        dh_ref[...] = dh

        @pl.when(pl.program_id(0) == 0)
        def _():
            dg_ref[...] = jnp.zeros_like(dg_ref)
        dg_ref[...] += jnp.sum(da * y, axis=0, keepdims=True)

    row = pl.BlockSpec((tm, K), lambda m: (m, 0))
    vec = pl.BlockSpec((1, K), lambda m: (0, 0))
    in_specs, args = [], []
    for a, w in pairs:
        in_specs += [pl.BlockSpec((tm, a.shape[1]), lambda m: (m, 0)), pl.BlockSpec(w.shape, lambda m: (0, 0))]
        args += [a, w]
    in_specs += [row, vec] + ([row] if dres is not None else [])
    args += [h, gain] + ([dres] if dres is not None else [])
    return pl.pallas_call(
        body, name=name, grid=(M // tm,), in_specs=in_specs, out_specs=[row, vec],
        out_shape=[jax.ShapeDtypeStruct((M, K), F32), jax.ShapeDtypeStruct((1, K), F32)],
        compiler_params=_params(("arbitrary",)))(*args)


def _mm_tn(a, b, name):
    M, K = a.shape
    N = b.shape[1]
    tm = _pick(M, (768, 512, 256, 128))
    tk = _col_tile(K, 1408)
    tn = _col_tile(N, 1408)

    def body(a_ref, b_ref, o_ref):
        @pl.when(pl.program_id(2) == 0)
        def _():
            o_ref[...] = jnp.zeros_like(o_ref)
        o_ref[...] += _dot_tn(a_ref[...].astype(BF16), b_ref[...].astype(BF16))

    return pl.pallas_call(
        body, name=name, grid=(K // tk, N // tn, M // tm),
        in_specs=[pl.BlockSpec((tm, tk), lambda k, n, m: (m, k)), pl.BlockSpec((tm, tn), lambda k, n, m: (m, n))],
        out_specs=pl.BlockSpec((tk, tn), lambda k, n, m: (k, n)),
        out_shape=jax.ShapeDtypeStruct((K, N), F32),
        compiler_params=_params(("parallel", "parallel", "arbitrary")))(a, b)


def _norm_fwd(h, gain, out_dtype, name):
    M, C = h.shape
    tm = _pick(M, (768, 512, 256, 128))

    def body(h_ref, g_ref, a_ref):
        x = h_ref[...]
        r = lax.rsqrt(jnp.mean(x * x, axis=-1, keepdims=True) + EPS)
        a_ref[...] = ((x * r) * g_ref[...]).astype(a_ref.dtype)

    return pl.pallas_call(
        body, name=name, grid=(M // tm,),
        in_specs=[pl.BlockSpec((tm, C), lambda m: (m, 0)), pl.BlockSpec((1, C), lambda m: (0, 0))],
        out_specs=pl.BlockSpec((tm, C), lambda m: (m, 0)),
        out_shape=jax.ShapeDtypeStruct((M, C), out_dtype),
        compiler_params=_params(("parallel",)))(h, gain)


def _norm_bwd(h, gain, da, dres, name, token_rows=False):
    M, C = h.shape
    tm = ROW0 if token_rows else _pick(M, (768, 512, 256, 128))

    def body(*refs):
        refs = list(refs)
        dx_ref = refs.pop() if token_rows else None
        if dres is None:
            h_ref, g_ref, da_ref, dh_ref, dg_ref = refs
        else:
            h_ref, g_ref, da_ref, dr_ref, dh_ref, dg_ref = refs
        x = h_ref[...]
        r = lax.rsqrt(jnp.mean(x * x, axis=-1, keepdims=True) + EPS)
        y = x * r
        dav = da_ref[...].astype(F32)
        dy = dav * g_ref[...]
        dh = r * (dy - y * jnp.mean(dy * y, axis=-1, keepdims=True))
        if dres is not None:
            dh = dh + dr_ref[...]
        dh_ref[...] = dh
        if token_rows:
            dx_ref[...] = dh

        @pl.when(pl.program_id(0) == 0)
        def _():
            dg_ref[...] = jnp.zeros_like(dg_ref)
        dg_ref[...] += jnp.sum(dav * y, axis=0, keepdims=True)

    row = pl.BlockSpec((tm, C), lambda m: (m, 0))
    vec = pl.BlockSpec((1, C), lambda m: (0, 0))
    in_specs = [row, vec, row] + ([row] if dres is not None else [])
    args = [h, gain, da] + ([dres] if dres is not None else [])
    out_specs = [row, vec]
    out_shape = [jax.ShapeDtypeStruct((M, C), F32), jax.ShapeDtypeStruct((1, C), F32)]
    if token_rows:
        out_specs.append(pl.BlockSpec((tm, C), lambda m: (jnp.maximum(m - 1, 0), 0)))
        out_shape.append(jax.ShapeDtypeStruct((M - ROW0, C), F32))
    return pl.pallas_call(
        body, name=name, grid=(M // tm,), in_specs=in_specs, out_specs=out_specs, out_shape=out_shape,
        compiler_params=_params(("arbitrary",)))(*args)


def _ffn_up(b, w_g, w_u, name):
    M, K = b.shape
    F = w_g.shape[1]
    tm = _pick(M, (768, 512, 256, 128))
    tn = _col_tile(F, 1408)
    nb = F // tn

    def body(b_ref, wg_ref, wu_ref, g_ref, u_ref, act_ref):
        x = b_ref[...]
        g = _dot(x, wg_ref[...])
        u = _dot(x, wu_ref[...])
        g_ref[...] = g.astype(g_ref.dtype)
        u_ref[...] = u.astype(u_ref.dtype)
        act_ref[...] = ((g * jax.nn.sigmoid(g)) * u).astype(act_ref.dtype)

    blk = pl.BlockSpec((tm, tn), lambda n, m: (m, n))
    return pl.pallas_call(
        body, name=name, grid=(nb, M // tm),
        in_specs=[pl.BlockSpec((tm, K), lambda n, m: (m, 0)),
                  pl.BlockSpec((K, tn), lambda n, m: (0, n)),
                  pl.BlockSpec((K, tn), lambda n, m: (0, n))],
        out_specs=[blk, blk, blk],
        out_shape=[jax.ShapeDtypeStruct((M, F), BF16), jax.ShapeDtypeStruct((M, F), BF16),
                   jax.ShapeDtypeStruct((M, F), BF16)],
        compiler_params=_params(("parallel", "parallel")))(b, w_g, w_u)


def _ffn_dact(dy, w_d, g, u, name):
    M, K = dy.shape
    F = w_d.shape[0]
    tm = _pick(M, (768, 512, 256, 128))
    tn = _col_tile(F, 1408)
    nb = F // tn

    def body(dy_ref, wd_ref, g_ref, u_ref, dg_ref, du_ref):
        dact = _dot_nt(dy_ref[...].astype(BF16), wd_ref[...])
        gv = g_ref[...].astype(F32)
        s = jax.nn.sigmoid(gv)
        silu = gv * s
        dg_ref[...] = (dact * u_ref[...].astype(F32) * (s * (1.0 + gv * (1.0 - s)))).astype(dg_ref.dtype)
        du_ref[...] = (dact * silu).astype(du_ref.dtype)

    blk = pl.BlockSpec((tm, tn), lambda n, m: (m, n))
    return pl.pallas_call(
        body, name=name, grid=(nb, M // tm),
        in_specs=[pl.BlockSpec((tm, K), lambda n, m: (m, 0)), pl.BlockSpec((tn, K), lambda n, m: (n, 0)), blk, blk],
        out_specs=[blk, blk],
        out_shape=[jax.ShapeDtypeStruct((M, F), BF16), jax.ShapeDtypeStruct((M, F), BF16)],
        compiler_params=_params(("parallel", "parallel")))(dy, w_d, g, u)


def _loss_head(h, gain, target, name):
    M, C = h.shape
    tm = ROW0
    assert M % tm == 0 and target.shape[0] == M - ROW0

    def body(h_ref, g_ref, t_ref, sq_ref, dh_ref, dg_ref):
        i = pl.program_id(0)

        @pl.when(i == 0)
        def _():
            sq_ref[...] = jnp.zeros_like(sq_ref)
            dg_ref[...] = jnp.zeros_like(dg_ref)
            dh_ref[...] = jnp.zeros_like(dh_ref)

        @pl.when(i > 0)
        def _():
            x = h_ref[...]
            r = lax.rsqrt(jnp.mean(x * x, axis=-1, keepdims=True) + EPS)
            y = x * r
            err = y * g_ref[...] - t_ref[...]
            sq_ref[...] += jnp.sum(err * err, axis=0, keepdims=True)
            da = err * (1.0 / C)
            dy = da * g_ref[...]
            dh_ref[...] = r * (dy - y * jnp.mean(dy * y, axis=-1, keepdims=True))
            dg_ref[...] += jnp.sum(da * y, axis=0, keepdims=True)

    row = pl.BlockSpec((tm, C), lambda m: (m, 0))
    vec = pl.BlockSpec((1, C), lambda m: (0, 0))
    return pl.pallas_call(
        body, name=name, grid=(M // tm,),
        in_specs=[row, vec, pl.BlockSpec((tm, C), lambda m: (jnp.maximum(m - 1, 0), 0))],
        out_specs=[vec, row, vec],
        out_shape=[jax.ShapeDtypeStruct((1, C), F32), jax.ShapeDtypeStruct((M, C), F32),
                   jax.ShapeDtypeStruct((1, C), F32)],
        compiler_params=_params(("arbitrary",)))(h, gain, target)


def _band_dot(band, x):
    hi = x.astype(BF16)
    rest = x - hi.astype(F32)
    mid = rest.astype(BF16)
    lo = (rest - mid.astype(F32)).astype(BF16)
    return _dot(band, hi) + _dot(band, mid) + _dot(band, lo)


def _pool_pos(row0, tm):
    return row0 + lax.broadcasted_iota(jnp.int32, (tm, 1), 0) - PAD


def _pool_fwd(h, a, w, scale, name):
    M, C = a.shape
    tm = 256
    hb = tm // HALO

    def body(h_ref, a_ref, halo_ref, w_ref, s_ref, o_ref, p_ref):
        i = pl.program_id(0)
        row0 = i * tm
        ext = jnp.concatenate([halo_ref[...], a_ref[...]], axis=0)
        src = row0 - HALO + lax.broadcasted_iota(jnp.int32, (tm + HALO, 1), 0)
        ext = jnp.where(src >= PAD, ext, 0.0)
        r = lax.broadcasted_iota(jnp.int32, (tm, tm + HALO), 0)
        c = lax.broadcasted_iota(jnp.int32, (tm, tm + HALO), 1)
        pos = _pool_pos(row0, tm)
        for g, win in enumerate(POOL_WINDOWS):
            band = ((c <= r + HALO) & (c > r + HALO - win)).astype(BF16)
            cols = slice(g * POOL_GROUP, (g + 1) * POOL_GROUP)
            xg = ext[:, cols]
            tot = _band_dot(band, xg)
            cnt = jnp.clip(pos + 1, 1, win).astype(F32)
            pooled = (tot / cnt - xg[HALO:]).astype(BF16)
            p_ref[:, cols] = pooled
            mixed = _dot(pooled, w_ref[g])
            o_ref[:, cols] = h_ref[:, cols] + mixed * s_ref[:, cols]

    row = pl.BlockSpec((tm, C), lambda m: (m, 0))
    return pl.pallas_call(
        body, name=name, grid=(M // tm,),
        in_specs=[row, row, pl.BlockSpec((HALO, C), lambda m: (jnp.maximum(m * hb - 1, 0), 0)),
                  pl.BlockSpec((4, POOL_GROUP, POOL_GROUP), lambda m: (0, 0, 0)),
                  pl.BlockSpec((1, C), lambda m: (0, 0))],
        out_specs=[row, row],
        out_shape=[jax.ShapeDtypeStruct((M, C), F32), jax.ShapeDtypeStruct((M, C), BF16)],
        compiler_params=_params(("parallel",)))(h, a, a, w, scale)


def _pool_bwd_mix(dout, pooled, w, scale, name):
    M, C = dout.shape
    tm = 256

    def body(do_ref, p_ref, w_ref, s_ref, dpc_ref, dw_ref, ds_ref):
        i = pl.program_id(0)

        @pl.when(i == 0)
        def _():
            dw_ref[...] = jnp.zeros_like(dw_ref)
            ds_ref[...] = jnp.zeros_like(ds_ref)

        pos = _pool_pos(i * tm, tm)
        for g, win in enumerate(POOL_WINDOWS):
            cols = slice(g * POOL_GROUP, (g + 1) * POOL_GROUP)
            do = do_ref[:, cols]
            pooled = p_ref[:, cols]
            mixed = _dot(pooled, w_ref[g])
            ds_ref[:, cols] += jnp.sum(do * mixed, axis=0, keepdims=True)
            dmix = (do * s_ref[:, cols]).astype(BF16)
            dw_ref[g] += _dot_tn(pooled, dmix)
            dp = _dot_nt(dmix, w_ref[g])
            cnt = jnp.clip(pos + 1, 1, win).astype(F32)
            dpc_ref[:, cols] = dp / cnt

    row = pl.BlockSpec((tm, C), lambda m: (m, 0))
    wspec = pl.BlockSpec((4, POOL_GROUP, POOL_GROUP), lambda m: (0, 0, 0))
    vec = pl.BlockSpec((1, C), lambda m: (0, 0))
    return pl.pallas_call(
        body, name=name, grid=(M // tm,),
        in_specs=[row, row, wspec, vec], out_specs=[row, wspec, vec],
        out_shape=[jax.ShapeDtypeStruct((M, C), F32), jax.ShapeDtypeStruct((4, POOL_GROUP, POOL_GROUP), F32),
                   jax.ShapeDtypeStruct((1, C), F32)],
        compiler_params=_params(("arbitrary",)))(dout, pooled, w, scale)


def _pool_bwd_window(dpc, name):
    M, C = dpc.shape
    tm = 256
    hb = tm // HALO
    last = M // HALO - 1

    def body(d_ref, halo_ref, da_ref):
        i = pl.program_id(0)
        row0 = i * tm
        ext = jnp.concatenate([d_ref[...], halo_ref[...]], axis=0)
        src = row0 + lax.broadcasted_iota(jnp.int32, (tm + HALO, 1), 0)
        ext = jnp.where(src < M, ext, 0.0)
        r = lax.broadcasted_iota(jnp.int32, (tm, tm + HALO), 0)
        c = lax.broadcasted_iota(jnp.int32, (tm, tm + HALO), 1)
        pos = _pool_pos(row0, tm)
        for g, win in enumerate(POOL_WINDOWS):
            band = ((c >= r) & (c < r + win)).astype(BF16)
            cols = slice(g * POOL_GROUP, (g + 1) * POOL_GROUP)
            xg = ext[:, cols]
            tot = _band_dot(band, xg)
            cnt = jnp.clip(pos + 1, 1, win).astype(F32)
            da_ref[:, cols] = jnp.where(pos >= 0, tot - xg[:tm] * cnt, 0.0)

    row = pl.BlockSpec((tm, C), lambda m: (m, 0))
    return pl.pallas_call(
        body, name=name, grid=(M // tm,),
        in_specs=[row, pl.BlockSpec((HALO, C), lambda m: (jnp.minimum((m + 1) * hb, last), 0))],
        out_specs=row, out_shape=jax.ShapeDtypeStruct((M, C), F32),
        compiler_params=_params(("parallel",)))(dpc, dpc)


def _head_masks():
    lane = lax.broadcasted_iota(jnp.int32, (1, 128), 1)
    return lane < HEAD_DIM, lane


def _split_heads(x, first):
    z = jnp.zeros_like(x)
    return jnp.where(first, x, z), jnp.where(first, z, x)


def _split_rope(x, lane):
    z = jnp.zeros_like(x)
    return jnp.where(lane < MLA_ROPE, x, z), jnp.where((lane >= MLA_ROPE) & (lane < 2 * MLA_ROPE), x, z)


def _walk_causal(i, step):
    def mid(kb, carry):
        step(kb, False)
        return carry

    step(0, True)
    lax.fori_loop(1, i, mid, 0)

    @pl.when(i > 0)
    def _():
        step(i, True)


def _mla_fwd(q_all, kv_all, qr, kr, scale, name):
    M = q_all.shape[0]
    t = ATTN_TILE

    def body(q_ref, k_ref, v_ref, qr_ref, kr_ref, o_ref, lse_ref, m_s, l_s, acc_s, kmax_s):
        i = pl.program_id(1)
        first, lane = _head_masks()

        @pl.when(i == 0)
        def _():
            def block_max(kb, carry):
                rows = pl.ds(pl.multiple_of(kb * t, t), t)
                kk = k_ref[rows, :].astype(F32)
                kk = kk * kk
                rr = kr_ref[rows, :].astype(F32)
                rr = jnp.sum(jnp.where(lane < MLA_ROPE, rr * rr, 0.0), axis=1, keepdims=True)
                a = jnp.max(jnp.sum(jnp.where(first, kk, 0.0), axis=1, keepdims=True) + rr)
                b = jnp.max(jnp.sum(jnp.where(first, 0.0, kk), axis=1, keepdims=True) + rr)
                return jnp.maximum(carry[0], a), jnp.maximum(carry[1], b)

            a, b = lax.fori_loop(0, M // t, block_max, (jnp.float32(0.0), jnp.float32(0.0)))
            kmax_s[0] = a
            kmax_s[1] = b

        qs = _split_heads(q_ref[...], first)
        qrs = _split_rope(qr_ref[...], lane)
        qcat = tuple(jnp.concatenate([qs[hh], qrs[hh]], axis=1) for hh in range(2))
        qpos = i * t + lax.broadcasted_iota(jnp.int32, (t, t), 0)
        kidx = lax.broadcasted_iota(jnp.int32, (t, t), 1)
        c2 = scale * LOG2E

        def run(online):
            l_s[...] = jnp.zeros_like(l_s)
            acc_s[...] = jnp.zeros_like(acc_s)
            if online:
                m_s[...] = jnp.full_like(m_s, NEG)

            def step(kb, masked):
                k0 = pl.multiple_of(kb * t, t)
                kcat = jnp.concatenate([k_ref[pl.ds(k0, t), :], kr_ref[pl.ds(k0, t), :]], axis=1)
                vs = _split_heads(v_ref[pl.ds(k0, t), :], first)
                if masked:
                    kpos = k0 + kidx
                    valid = (kpos <= qpos) & (kpos >= PAD)
                pv = None
                alphas = []
                for hh in range(2):
                    s = _dot_nt(qcat[hh], kcat)
                    if online:
                        if masked:
                            s = jnp.where(valid, s, NEG)
                        m_old = m_s[hh]
                        m_new = jnp.maximum(m_old, jnp.max(s, axis=1, keepdims=True))
                        p = jnp.exp2((s - m_new) * c2)
                        alpha = jnp.exp2((m_old - m_new) * c2)
                        l_s[hh] = alpha * l_s[hh] + jnp.sum(p, axis=1, keepdims=True)
                        m_s[hh] = m_new
                        alphas.append(alpha)
                    else:
                        p = jnp.exp2(s * c2 - m_s[hh])
                        if masked:
                            p = jnp.where(valid, p, 0.0)
                        l_s[hh] = l_s[hh] + jnp.sum(p, axis=1, keepdims=True)
                    d = _dot(p.astype(BF16), vs[hh])
                    pv = d if pv is None else pv + d
                if online:
                    acc_s[...] = acc_s[...] * jnp.where(first, alphas[0], alphas[1]) + pv
                else:
                    acc_s[...] += pv

            _walk_causal(i, step)

        for hh in range(2):
            qf = qcat[hh].astype(F32)
            m_s[hh] = (1.001 * c2) * jnp.sqrt(jnp.sum(qf * qf, axis=1, keepdims=True) * kmax_s[hh])
        run(False)
        real = i * t + lax.broadcasted_iota(jnp.int32, (t, 1), 0) >= PAD
        underflow = jnp.max(jnp.where(real & (jnp.minimum(l_s[0], l_s[1]) < 1e-30), 1.0, 0.0)) > 0.0

        @pl.when(underflow)
        def _():
            run(True)
            m_s[...] = m_s[...] * c2

        ls = tuple(jnp.where(l_s[hh] > 0.0, l_s[hh], 1.0) for hh in range(2))
        o_ref[...] = (acc_s[...] * jnp.where(first, 1.0 / ls[0], 1.0 / ls[1])).astype(o_ref.dtype)
        lse_ref[:, 0:1] = m_s[0] * (1.0 / LOG2E) + jnp.log(ls[0])
        lse_ref[:, 1:2] = m_s[1] * (1.0 / LOG2E) + jnp.log(ls[1])

    blk = pl.BlockSpec((t, 128), lambda j, i: (i, j))
    return pl.pallas_call(
        body, name=name, grid=(N_PAIRS, M // t),
        in_specs=[blk, pl.BlockSpec((M, 128), lambda j, i: (0, j)), pl.BlockSpec((M, 128), lambda j, i: (0, N_PAIRS + j)),
                  blk, pl.BlockSpec((M, 128), lambda j, i: (0, 0))],
        out_specs=[blk, pl.BlockSpec((None, t, 2), lambda j, i: (j, i, 0))],
        out_shape=[jax.ShapeDtypeStruct((M, N_PAIRS * 128), BF16), jax.ShapeDtypeStruct((N_PAIRS, M, 2), F32)],
        scratch_shapes=[pltpu.VMEM((2, t, 1), F32), pltpu.VMEM((2, t, 1), F32), pltpu.VMEM((t, 128), F32),
                        pltpu.SMEM((2,), F32)],
        compiler_params=_params(("arbitrary", "arbitrary")))(q_all, kv_all, kv_all, qr, kr)


def _mla_bwd(q_all, kv_all, qr, kr, o, do, lse, scale, name):
    M = q_all.shape[0]
    t = ATTN_BWD_TILE

    def body(q_ref, kv_hbm, qr_ref, kr_hbm, o_ref, do_ref, lse_ref,
             dq_ref, dk_hbm, dv_hbm, dqr_ref, dkr_hbm,
             k_ref, v_ref, kr_ref, dk_ref, dv_ref, dkr_ref, dq_s, lse_s, delta_s):
        j = pl.program_id(0)
        i = pl.program_id(1)
        first, lane = _head_masks()
        every = pl.ds(0, M)
        kcols = pl.ds(pl.multiple_of(j * 128, 128), 128)
        vcols = pl.ds(pl.multiple_of((N_PAIRS + j) * 128, 128), 128)

        @pl.when(i == 0)
        def _():
            pltpu.sync_copy(kv_hbm.at[every, kcols], k_ref)
            pltpu.sync_copy(kv_hbm.at[every, vcols], v_ref)
            pltpu.sync_copy(kr_hbm, kr_ref)
            dk_ref[...] = jnp.zeros_like(dk_ref)
            dv_ref[...] = jnp.zeros_like(dv_ref)
            dkr_ref[...] = jnp.zeros_like(dkr_ref)

        qs = _split_heads(q_ref[...], first)
        qrs = _split_rope(qr_ref[...], lane)
        qcat = tuple(jnp.concatenate([qs[hh], qrs[hh]], axis=1) for hh in range(2))
        dov = do_ref[...]
        dos = _split_heads(dov, first)
        prod = dov.astype(F32) * o_ref[...].astype(F32)
        deltas = (jnp.sum(jnp.where(first, prod, 0.0), axis=1, keepdims=True),
                  jnp.sum(jnp.where(first, 0.0, prod), axis=1, keepdims=True))
        for hh in range(2):
            lse_s[hh] = jnp.broadcast_to(lse_ref[:, hh:hh + 1], (t, t))
            delta_s[hh] = jnp.broadcast_to(deltas[hh], (t, t))
        dq_s[...] = jnp.zeros_like(dq_s)
        qpos = i * t + lax.broadcasted_iota(jnp.int32, (t, t), 0)
        kidx = lax.broadcasted_iota(jnp.int32, (t, t), 1)

        def step(kb, masked):
            k0 = pl.multiple_of(kb * t, t)
            rows = pl.ds(k0, t)
            k = k_ref[rows, :]
            v = v_ref[rows, :]
            kr = kr_ref[rows, :]
            kcat = jnp.concatenate([k, kr], axis=1)
            ks = _split_heads(k, first)
            krs = _split_rope(kr, lane)
            if masked:
                kpos = k0 + kidx
                valid = (kpos <= qpos) & (kpos >= PAD)
            dq = dk = dv = None
            for hh in range(2):
                s = _dot_nt(qcat[hh], kcat) * scale
                if masked:
                    s = jnp.where(valid, s, NEG)
                p = jnp.exp(s - lse_s[hh])
                ds = p * (_dot_nt(dos[hh], v) - delta_s[hh])
                dsb = (ds * scale).astype(BF16)
                a = _dot(dsb, jnp.concatenate([ks[hh], krs[hh]], axis=1))
                b = _dot_tn(dsb, qcat[hh])
                c = _dot_tn(p.astype(BF16), dos[hh])
                dq = a if dq is None else dq + a
                dk = b if dk is None else dk + b
                dv = c if dv is None else dv + c
            dq_s[...] += dq
            dk_ref[rows, :] += dk[:, :128]
            dkr_ref[rows, :] += dk[:, 128:]
            dv_ref[rows, :] += dv

        _walk_causal(i, step)
        dq_ref[...] = dq_s[:, :128].astype(dq_ref.dtype)
        dqr_ref[...] = dq_s[:, 128:].astype(dqr_ref.dtype)

        @pl.when(i == M // t - 1)
        def _():
            pltpu.sync_copy(dk_ref, dk_hbm.at[every, kcols])
            pltpu.sync_copy(dv_ref, dv_hbm.at[every, kcols])
            pltpu.sync_copy(dkr_ref, dkr_hbm.at[j])

    blk = pl.BlockSpec((t, 128), lambda j, i: (i, j))
    whole = pl.BlockSpec(memory_space=pl.ANY)
    wide = jax.ShapeDtypeStruct((M, N_PAIRS * 128), F32)
    slab = lambda dtype: pltpu.VMEM((M, 128), dtype)
    return pl.pallas_call(
        body, name=name, grid=(N_PAIRS, M // t),
        in_specs=[blk, whole, blk, whole, blk, blk, pl.BlockSpec((None, t, 2), lambda j, i: (j, i, 0))],
        out_specs=[blk, whole, whole, blk, whole],
        out_shape=[jax.ShapeDtypeStruct((M, N_PAIRS * 128), BF16), wide, wide,
                   jax.ShapeDtypeStruct((M, N_PAIRS * 128), BF16), jax.ShapeDtypeStruct((N_PAIRS, M, 128), F32)],
        scratch_shapes=[slab(BF16), slab(BF16), slab(BF16), slab(F32), slab(F32), slab(F32),
                        pltpu.VMEM((t, 256), F32), pltpu.VMEM((2, t, t), F32), pltpu.VMEM((2, t, t), F32)],
        compiler_params=_params(("arbitrary", "arbitrary")))(q_all, kv_all, qr, kr, o, do, lse)


def _tri(t, rel):
    j = lax.broadcasted_iota(jnp.int32, (t, t), 0)
    k = lax.broadcasted_iota(jnp.int32, (t, t), 1)
    m = {"gt": j > k, "le": j <= k, "lt": j < k}[rel]
    return m.astype(BF16)


def _lane_cumsum(x, tri):
    hi = x.astype(BF16)
    lo = (x - hi.astype(F32)).astype(BF16)
    return _dot(hi, tri) + _dot(lo, tri)


def _log_sigmoids(z):
    sp = jnp.log(1.0 + jnp.exp(-jnp.abs(z)))
    return jnp.minimum(z, 0.0) - sp, jnp.minimum(-z, 0.0) - sp


def _log_sigmoids_fast(z):
    lk = -(jnp.maximum(z, 0.0) + jnp.log(1.0 + jnp.exp(-jnp.abs(z))))
    return lk + z, lk


def _sb_fwd(qkv, scale, name):
    M = qkv.shape[0]
    t = WALK_TILE
    ck, cv = N_PAIRS, 2 * N_PAIRS

    def body(q_ref, k_ref, v_ref, o_ref, tot_ref, c_s, acc_s):
        i = pl.program_id(1)
        first, _ = _head_masks()
        qs = _split_heads(q_ref[...], first)
        c_s[...] = jnp.zeros_like(c_s)
        acc_s[...] = jnp.zeros_like(acc_s)
        tri = _tri(t, "gt")
        qpos = i * t + lax.broadcasted_iota(jnp.int32, (t, t), 0)
        kidx = lax.broadcasted_iota(jnp.int32, (t, t), 1)

        def step(kb, masked):
            k0 = pl.multiple_of(kb * t, t)
            k = k_ref[pl.ds(k0, t), :]
            vs = _split_heads(v_ref[pl.ds(k0, t), :], first)
            if masked:
                kpos = k0 + kidx
                valid = (kpos < qpos) & (kpos >= PAD)
            pv = None
            for hh in range(2):
                z = _dot_nt(qs[hh], k) * scale
                lb, lk = _log_sigmoids_fast(z)
                if masked:
                    lk = jnp.where(valid, lk, 0.0)
                a = jnp.exp(lb + (c_s[hh] + _lane_cumsum(lk, tri)))
                if masked:
                    a = jnp.where(valid, a, 0.0)
                c_s[hh] = c_s[hh] + jnp.sum(lk, axis=1, keepdims=True)
                d = _dot(a.astype(BF16), vs[hh])
                pv = d if pv is None else pv + d
            acc_s[...] += pv

        def keep_going():
            return jnp.max(jnp.maximum(c_s[0], c_s[1])) > EXP_ZERO

        def cond(carry):
            kb, go, _ = carry
            return (kb >= 1) & go

        def walk(carry):
            kb, _, n = carry
            step(kb, False)
            return kb - 1, keep_going(), n + 1

        step(i, True)
        _, go, n = lax.while_loop(cond, walk, (i - 1, keep_going(), jnp.int32(1)))
        first_too = go & (i > 0)

        @pl.when(first_too)
        def _():
            step(0, True)

        walked = n + first_too.astype(jnp.int32)
        o_ref[...] = acc_s[...].astype(o_ref.dtype)
        tot_ref[:, 0:1] = c_s[0]
        tot_ref[:, 1:2] = c_s[1]
        tot_ref[:, 2:3] = jnp.full((t, 1), walked.astype(F32))

    whole = lambda c0: pl.BlockSpec((M, 128), lambda j, i: (0, c0 + j))
    return pl.pallas_call(
        body, name=name, grid=(N_PAIRS, M // t),
        in_specs=[pl.BlockSpec((t, 128), lambda j, i: (i, j)), whole(ck), whole(cv)],
        out_specs=[pl.BlockSpec((t, 128), lambda j, i: (i, j)), pl.BlockSpec((None, t, 3), lambda j, i: (j, i, 0))],
        out_shape=[jax.ShapeDtypeStruct((M, N_PAIRS * 128), BF16), jax.ShapeDtypeStruct((N_PAIRS, M, 3), F32)],
        scratch_shapes=[pltpu.VMEM((2, t, 1), F32), pltpu.VMEM((t, 128), F32)],
        compiler_params=_params(("parallel", "arbitrary")))(qkv, qkv, qkv)


def _sb_bwd(qkv, do, tot, scale, name):
    M = qkv.shape[0]
    t = WALK_TILE
    ck, cv = N_PAIRS, 2 * N_PAIRS

    def body(q_ref, k_ref, v_ref, do_ref, tot_ref, dq_ref, dk_ref, dv_ref, pc_s, dc_s, dq_s):
        i = pl.program_id(1)
        first, _ = _head_masks()

        @pl.when(i == 0)
        def _():
            dk_ref[...] = jnp.zeros_like(dk_ref)
            dv_ref[...] = jnp.zeros_like(dv_ref)

        qs = _split_heads(q_ref[...], first)
        dos = _split_heads(do_ref[...], first)
        pc_s[...] = jnp.zeros_like(pc_s)
        dc_s[...] = jnp.zeros_like(dc_s)
        dq_s[...] = jnp.zeros_like(dq_s)
        tri_le = _tri(t, "le")
        tri_lt = _tri(t, "lt")
        qpos = i * t + lax.broadcasted_iota(jnp.int32, (t, t), 0)
        kidx = lax.broadcasted_iota(jnp.int32, (t, t), 1)

        def step(kb, masked):
            k0 = pl.multiple_of(kb * t, t)
            rows = pl.ds(k0, t)
            k = k_ref[rows, :]
            v = v_ref[rows, :]
            ks = _split_heads(k, first)
            if masked:
                kpos = k0 + kidx
                valid = (kpos < qpos) & (kpos >= PAD)
            dq = dk = dv = None
            for hh in range(2):
                z = _dot_nt(qs[hh], k) * scale
                lb, lk = _log_sigmoids_fast(z)
                if masked:
                    lk = jnp.where(valid, lk, 0.0)
                later = tot_ref[:, hh:hh + 1] - (pc_s[hh] + _lane_cumsum(lk, tri_le))
                a = jnp.exp(lb + later)
                if masked:
                    a = jnp.where(valid, a, 0.0)
                dl = a * _dot_nt(dos[hh], v)
                early = dc_s[hh] + _lane_cumsum(dl, tri_lt)
                sg = jnp.exp(lb)
                dz = (dl * (1.0 - sg) - early * sg) * scale
                if masked:
                    dz = jnp.where(valid, dz, 0.0)
                pc_s[hh] = pc_s[hh] + jnp.sum(lk, axis=1, keepdims=True)
                dc_s[hh] = dc_s[hh] + jnp.sum(dl, axis=1, keepdims=True)
                dzb = dz.astype(BF16)
                x = _dot(dzb, ks[hh])
                y = _dot_tn(dzb, qs[hh])
                w = _dot_tn(a.astype(BF16), dos[hh])
                dq = x if dq is None else dq + x
                dk = y if dk is None else dk + y
                dv = w if dv is None else dv + w
            dq_s[...] += dq
            dk_ref[rows, :] += dk
            dv_ref[rows, :] += dv

        first_walked = i + 1 - jnp.max(tot_ref[:, 2:3]).astype(jnp.int32)

        def mid(kb, carry):
            step(kb, False)
            return carry

        @pl.when((first_walked == 0) & (i > 0))
        def _():
            step(0, True)

        lax.fori_loop(jnp.maximum(first_walked, 1), i, mid, 0)
        step(i, True)
        dq_ref[...] = dq_s[...].astype(dq_ref.dtype)

    whole = lambda c0: pl.BlockSpec((M, 128), lambda j, i: (0, c0 + j))
    blk = pl.BlockSpec((t, 128), lambda j, i: (i, j))
    col = pl.BlockSpec((M, 128), lambda j, i: (0, j))
    return pl.pallas_call(
        body, name=name, grid=(N_PAIRS, M // t),
        in_specs=[blk, whole(ck), whole(cv), blk, pl.BlockSpec((None, t, 3), lambda j, i: (j, i, 0))],
        out_specs=[blk, col, col],
        out_shape=[jax.ShapeDtypeStruct((M, N_PAIRS * 128), BF16), jax.ShapeDtypeStruct((M, N_PAIRS * 128), F32),
                   jax.ShapeDtypeStruct((M, N_PAIRS * 128), F32)],
        scratch_shapes=[pltpu.VMEM((2, t, 1), F32), pltpu.VMEM((2, t, 1), F32), pltpu.VMEM((t, 128), F32)],
        compiler_params=_params(("parallel", "arbitrary")))(qkv, qkv, qkv, do, tot)


def _rows_between(lo, hi):
    r = lax.broadcasted_iota(jnp.int32, (128, 1), 0)
    return (r >= lo) & (r < hi)


def _lanes_between(lo, hi):
    c = lax.broadcasted_iota(jnp.int32, (1, 128), 1)
    return (c >= lo) & (c < hi)


def _keep(x, mask):
    return jnp.where(mask, x, jnp.zeros_like(x))


def _valid_mask(i, kb, t):
    kpos = kb * t + lax.broadcasted_iota(jnp.int32, (t, t), 0)
    qpos = i * t + lax.broadcasted_iota(jnp.int32, (t, t), 1)
    return (kpos <= qpos) & (kpos >= PAD)


def _fox_fwd(qkv, qkv_t, f_rows, f_cols, scale, name):
    M = qkv.shape[0]
    t = FOX_TILE
    first_blk = PAD // t
    ck, cv = N_PAIRS, 2 * N_PAIRS

    def body(qt_ref, k_ref, vt_ref, fq_ref, fk_ref, o_ref, lse_ref, ox_ref, m_s, l_s, acc_s, accx_s, kmax_s, walked_s):
        i = pl.program_id(1)

        @pl.when(i == 0)
        def _():
            first = _lanes_between(0, 64)

            def block_max(kb, carry):
                kk = k_ref[pl.ds(pl.multiple_of(kb * t, t), t), :].astype(F32)
                kk = kk * kk
                a = jnp.max(jnp.sum(jnp.where(first, kk, 0.0), axis=1, keepdims=True))
                b = jnp.max(jnp.sum(jnp.where(first, 0.0, kk), axis=1, keepdims=True))
                return jnp.maximum(carry[0], a), jnp.maximum(carry[1], b)

            a, b = lax.fori_loop(0, M // t, block_max, (jnp.float32(0.0), jnp.float32(0.0)))
            kmax_s[0] = a
            kmax_s[1] = b

        qt = qt_ref[...]
        qts = (_keep(qt, _rows_between(0, 64)), _keep(qt, _rows_between(64, 128)))
        qf = qt.astype(F32)
        qf = qf * qf
        qbound = tuple(
            (1.001 * scale) * jnp.sqrt(jnp.sum(qf[HEAD_DIM * hh:HEAD_DIM * (hh + 1)], axis=0, keepdims=True) * kmax_s[hh])
            for hh in range(2))
        def run(online):
            l_s[...] = jnp.zeros_like(l_s)
            acc_s[...] = jnp.zeros_like(acc_s)
            accx_s[...] = jnp.zeros_like(accx_s)
            if online:
                m_s[...] = jnp.full_like(m_s, NEG)

            def step(kb, masked):
                k0 = pl.multiple_of(kb * t, t)
                rows = pl.ds(k0, t)
                k = k_ref[rows, :]
                if masked:
                    valid = _valid_mask(i, kb, t)
                for hh in range(2):
                    s = _dot(k, qts[hh]) * scale + (fq_ref[hh:hh + 1, :] - fk_ref[rows, hh:hh + 1])
                    hr = slice(HEAD_DIM * hh, HEAD_DIM * (hh + 1))
                    vt = vt_ref[hr, rows]
                    if online:
                        if masked:
                            s = jnp.where(valid, s, NEG)
                        m_old = m_s[hh]
                        m_new = jnp.maximum(m_old, jnp.max(s, axis=0, keepdims=True))
                        p = jnp.exp(s - m_new)
                        alpha = jnp.exp(m_old - m_new)
                        l_s[hh] = alpha * l_s[hh] + jnp.sum(p, axis=0, keepdims=True)
                        m_s[hh] = m_new
                        pb = p.astype(BF16)
                        acc_s[hr, :] = acc_s[hr, :] * alpha + _dot(vt, pb)
                        accx_s[hr, :] = accx_s[hr, :] * alpha + _dot(vt, (p - pb.astype(F32)).astype(BF16))
                    else:
                        p = jnp.exp(s - m_s[hh])
                        if masked:
                            p = jnp.where(valid, p, 0.0)
                        l_s[hh] = l_s[hh] + jnp.sum(p, axis=0, keepdims=True)
                        pb = p.astype(BF16)
                        acc_s[hr, :] += _dot(vt, pb)
                        accx_s[hr, :] += _dot(vt, (p - pb.astype(F32)).astype(BF16))

            def keep_going(kb):
                k0 = pl.multiple_of(kb * t, t)
                worst = None
                for hh in range(2):
                    f0 = jnp.max(fk_ref[pl.ds(k0, 8), hh:hh + 1])
                    decay = fq_ref[hh:hh + 1, :] - f0
                    if online:
                        w = jnp.max(qbound[hh] + decay - m_s[hh])
                    else:
                        w = jnp.max(decay - jnp.minimum(jnp.log(jnp.maximum(l_s[hh], 1e-37)), 0.0))
                    worst = w if worst is None else jnp.maximum(worst, w)
                return worst > EXP_ZERO

            def cond(carry):
                kb, go, _ = carry
                return (kb > first_blk) & go

            def walk(carry):
                kb, _, n = carry
                step(kb, False)
                return kb - 1, keep_going(kb), n + 1

            step(i, True)
            _, go, n = lax.while_loop(cond, walk, (i - 1, keep_going(i), jnp.int32(1)))
            first_too = go & (i > first_blk)

            @pl.when(first_too)
            def _():
                step(first_blk, True)

            walked_s[0] = n + first_too.astype(jnp.int32)

        for hh in range(2):
            m_s[hh] = qbound[hh]
        run(False)
        real = i * t + lax.broadcasted_iota(jnp.int32, (1, t), 1) >= PAD
        underflow = jnp.max(jnp.where(real & (jnp.minimum(l_s[0], l_s[1]) < 1e-30), 1.0, 0.0)) > 0.0

        @pl.when(underflow)
        def _():
            run(True)

        for hh in range(2):
            hr = slice(HEAD_DIM * hh, HEAD_DIM * (hh + 1))
            l = jnp.where(l_s[hh] > 0.0, l_s[hh], 1.0)
            inv = 1.0 / l
            o_ref[hr, :] = (acc_s[hr, :] * inv).astype(o_ref.dtype)
            ox_ref[hr, :] = (acc_s[hr, :] + accx_s[hr, :]) * inv
            lse_ref[hh:hh + 1, :] = m_s[hh] + jnp.log(l)
        lse_ref[2:3, :] = jnp.full((1, t), walked_s[0].astype(F32))

    blk = pl.BlockSpec((128, t), lambda j, i: (j, i))
    stat = pl.BlockSpec((None, 2, t), lambda j, i: (j, 0, i))
    return pl.pallas_call(
        body, name=name, grid=(N_PAIRS, M // t),
        in_specs=[blk, pl.BlockSpec((M, 128), lambda j, i: (0, ck + j)), pl.BlockSpec((128, M), lambda j, i: (cv + j, 0)),
                  stat, pl.BlockSpec((None, M, 2), lambda j, i: (j, 0, 0))],
        out_specs=[blk, pl.BlockSpec((None, 3, t), lambda j, i: (j, 0, i)), blk],
        out_shape=[jax.ShapeDtypeStruct((N_PAIRS * 128, M), BF16), jax.ShapeDtypeStruct((N_PAIRS, 3, M), F32),
                   jax.ShapeDtypeStruct((N_PAIRS * 128, M), F32)],
        scratch_shapes=[pltpu.VMEM((2, 1, t), F32), pltpu.VMEM((2, 1, t), F32), pltpu.VMEM((128, t), F32),
                        pltpu.VMEM((128, t), F32), pltpu.SMEM((2,), F32), pltpu.SMEM((1,), jnp.int32)],
        compiler_params=_params(("arbitrary", "arbitrary")))(qkv_t, qkv, qkv_t, f_rows, f_cols)


def _fox_bwd(qkv, qkv_t, o_t, do, do_t, lse, f_rows, f_cols, scale, name):
    M = qkv.shape[0]
    t = FOX_TILE
    first_blk = PAD // t
    ck, cv = N_PAIRS, 2 * N_PAIRS

    def body(q_ref, qt_ref, k_ref, kt_ref, v_ref, ot_ref, do_ref, dot_ref, lse_ref, fq_ref, fk_ref,
             dq_ref, dk_ref, dv_ref, cs_ref, dq_s):
        i = pl.program_id(1)

        @pl.when(i == 0)
        def _():
            dk_ref[...] = jnp.zeros_like(dk_ref)
            dv_ref[...] = jnp.zeros_like(dv_ref)
            cs_ref[...] = jnp.zeros_like(cs_ref)

        heads_l = (_lanes_between(0, 64), _lanes_between(64, 128))
        heads_r = (_rows_between(0, 64), _rows_between(64, 128))
        q = q_ref[...]
        qt = qt_ref[...]
        do = do_ref[...]
        dot = dot_ref[...]
        qs = tuple(_keep(q, m) for m in heads_l)
        qts = tuple(_keep(qt, m) for m in heads_r)
        dos = tuple(_keep(do, m) for m in heads_l)
        dots = tuple(_keep(dot, m) for m in heads_r)
        prod = dot.astype(F32) * ot_ref[...]
        deltas = tuple(jnp.sum(prod[HEAD_DIM * hh:HEAD_DIM * (hh + 1)], axis=0, keepdims=True) for hh in range(2))
        ones = tuple(m.astype(BF16) * jnp.ones((t, 128), BF16) for m in heads_l)
        dq_s[...] = jnp.zeros_like(dq_s)

        def step(kb, masked):
            k0 = pl.multiple_of(kb * t, t)
            rows = pl.ds(k0, t)
            k = k_ref[rows, :]
            v = v_ref[rows, :]
            if masked:
                valid = _valid_mask(i, kb, t)
            dk = dv = cs = None
            for hh in range(2):
                s = _dot(k, qts[hh]) * scale + (fq_ref[hh:hh + 1, :] - fk_ref[rows, hh:hh + 1])
                if masked:
                    s = jnp.where(valid, s, NEG)
                p = jnp.exp(s - lse_ref[hh:hh + 1, :])
                ds = p * (_dot(v, dots[hh]) - deltas[hh])
                hi = ds.astype(BF16)
                lo = (ds - hi.astype(F32)).astype(BF16)
                c = _dot(hi, ones[hh]) + _dot(lo, ones[hh])
                dsb = (ds * scale).astype(BF16)
                hr = slice(HEAD_DIM * hh, HEAD_DIM * (hh + 1))
                dq_s[hr, :] += _dot(kt_ref[hr, rows], dsb)
                a = _dot(dsb, qs[hh])
                b = _dot(p.astype(BF16), dos[hh])
                dk = a if dk is None else dk + a
                dv = b if dv is None else dv + b
                cs = c if cs is None else cs + c
            dk_ref[rows, :] += dk
            dv_ref[rows, :] += dv
            cs_ref[rows, :] += cs

        first_walked = i + 1 - jnp.max(lse_ref[2:3, :]).astype(jnp.int32)

        def mid(kb, carry):
            step(kb, False)
            return carry

        @pl.when((first_walked == first_blk) & (i > first_blk))
        def _():
            step(first_blk, True)

        lax.fori_loop(jnp.maximum(first_walked, first_blk + 1), i, mid, 0)
        step(i, True)
        dq_ref[...] = dq_s[...].astype(dq_ref.dtype)

    rblk = pl.BlockSpec((t, 128), lambda j, i: (i, j))
    tblk = pl.BlockSpec((128, t), lambda j, i: (j, i))
    stat = pl.BlockSpec((None, 2, t), lambda j, i: (j, 0, i))
    stat3 = pl.BlockSpec((None, 3, t), lambda j, i: (j, 0, i))
    col = pl.BlockSpec((M, 128), lambda j, i: (0, j))
    wide = jax.ShapeDtypeStruct((M, N_PAIRS * 128), F32)
    return pl.pallas_call(
        body, name=name, grid=(N_PAIRS, M // t),
        in_specs=[rblk, tblk, pl.BlockSpec((M, 128), lambda j, i: (0, ck + j)),
                  pl.BlockSpec((128, M), lambda j, i: (ck + j, 0)), pl.BlockSpec((M, 128), lambda j, i: (0, cv + j)),
                  tblk, rblk, tblk, stat3, stat, pl.BlockSpec((None, M, 2), lambda j, i: (j, 0, 0))],
        out_specs=[tblk, col, col, pl.BlockSpec((None, M, 128), lambda j, i: (j, 0, 0))],
        out_shape=[jax.ShapeDtypeStruct((N_PAIRS * 128, M), BF16), wide, wide,
                   jax.ShapeDtypeStruct((N_PAIRS, M, 128), F32)],
        scratch_shapes=[pltpu.VMEM((128, t), F32)],
        compiler_params=_params(("parallel", "arbitrary")))(qkv, qkv_t, qkv, qkv_t, qkv, o_t, do, do_t, lse,
                                                            f_rows, f_cols)


def _rope_tables(M):
    pos = (jnp.arange(M, dtype=jnp.int32) - PAD).astype(F32)
    inv = ROPE_THETA ** (-jnp.arange(0, MLA_ROPE, 2, dtype=F32) / MLA_ROPE)
    ang = pos[:, None] * inv[None, :]
    cos, sin = jnp.cos(ang), jnp.sin(ang)
    z = jnp.zeros((M, 64), F32)
    cos_t = jnp.concatenate([cos, cos, cos, cos, z], axis=1)
    sin_t = jnp.concatenate([-sin, sin, -sin, sin, z], axis=1)
    return cos_t, sin_t


def _rope(x, cos_t, sin_t, out_dtype, name, inverse=False, lead=0):
    M, C = x.shape
    tm = _pick(M, (768, 512, 256, 128))
    nblk = (C - lead) // 128
    sign = -1.0 if inverse else 1.0

    def body(x_ref, c_ref, s_ref, o_ref):
        lane = lax.broadcasted_iota(jnp.int32, (1, 128), 1)
        low = (lane % MLA_ROPE) < (MLA_ROPE // 2)
        cos = c_ref[...]
        sin = s_ref[...] * sign
        if lead:
            o_ref[:, :lead] = x_ref[:, :lead].astype(o_ref.dtype)
        for b in range(nblk):
            cols = slice(lead + b * 128, lead + (b + 1) * 128)
            v = x_ref[:, cols].astype(F32)
            up = pltpu.roll(v, 128 - MLA_ROPE // 2, 1)
            down = pltpu.roll(v, MLA_ROPE // 2, 1)
            o_ref[:, cols] = (v * cos + jnp.where(low, up, down) * sin).astype(o_ref.dtype)

    row = pl.BlockSpec((tm, C), lambda m: (m, 0))
    tab = pl.BlockSpec((tm, 128), lambda m: (m, 0))
    return pl.pallas_call(
        body, name=name, grid=(M // tm,), in_specs=[row, tab, tab], out_specs=row,
        out_shape=jax.ShapeDtypeStruct((M, C), out_dtype),
        compiler_params=_params(("parallel",)))(x, cos_t, sin_t)


def _forget_cumsum(f_logit, bias, name):
    M = f_logit.shape[0]
    tm = 256

    def body(f_ref, b_ref, o_ref, c_s):
        i = pl.program_id(0)

        @pl.when(i == 0)
        def _():
            c_s[...] = jnp.zeros_like(c_s)
        ls, _ = _log_sigmoids(f_ref[...] + b_ref[...])
        rows = i * tm + lax.broadcasted_iota(jnp.int32, (tm, 1), 0)
        ls = jnp.where(rows >= PAD, ls, 0.0)
        r = lax.broadcasted_iota(jnp.int32, (tm, tm), 0)
        c = lax.broadcasted_iota(jnp.int32, (tm, tm), 1)
        tri = (c <= r).astype(F32)
        cum = jnp.dot(tri, ls, precision=lax.Precision.HIGHEST, preferred_element_type=F32) + c_s[...]
        o_ref[...] = cum
        c_s[...] = cum[tm - 1:tm, :]

    row = pl.BlockSpec((tm, 128), lambda m: (m, 0))
    return pl.pallas_call(
        body, name=name, grid=(M // tm,),
        in_specs=[row, pl.BlockSpec((1, 128), lambda m: (0, 0))], out_specs=row,
        out_shape=jax.ShapeDtypeStruct((M, 128), F32), scratch_shapes=[pltpu.VMEM((1, 128), F32)],
        compiler_params=_params(("arbitrary",)))(f_logit, bias)


def _forget_cumsum_bwd(f_logit, bias, colsum, name):
    M = f_logit.shape[0]
    tm = 256
    nb = M // tm

    def body(f_ref, b_ref, cs_ref, o_ref, db_ref, c_s):
        i = pl.program_id(0)

        @pl.when(i == 0)
        def _():
            c_s[...] = jnp.zeros_like(c_s)
            db_ref[...] = jnp.zeros_like(db_ref)
        rr = lax.broadcasted_iota(jnp.int32, (128, 128), 0)
        cc = lax.broadcasted_iota(jnp.int32, (128, 128), 1)
        dF = None
        for j in range(N_PAIRS):
            sel = (((rr == 0) & (cc == 2 * j)) | ((rr == HEAD_DIM) & (cc == 2 * j + 1))).astype(F32)
            d = jnp.dot(cs_ref[j], sel, precision=lax.Precision.HIGHEST, preferred_element_type=F32)
            dF = d if dF is None else dF + d
        r = lax.broadcasted_iota(jnp.int32, (tm, tm), 0)
        c = lax.broadcasted_iota(jnp.int32, (tm, tm), 1)
        tri = (c >= r).astype(F32)
        cum = c_s[...] - jnp.dot(tri, dF, precision=lax.Precision.HIGHEST, preferred_element_type=F32)
        c_s[...] = cum[0:1, :]
        _, lsn = _log_sigmoids(f_ref[...] + b_ref[...])
        rows = (nb - 1 - i) * tm + lax.broadcasted_iota(jnp.int32, (tm, 1), 0)
        dl = jnp.where(rows >= PAD, cum * jnp.exp(lsn), 0.0)
        o_ref[...] = dl
        db_ref[...] += jnp.sum(dl, axis=0, keepdims=True)

    row = pl.BlockSpec((tm, 128), lambda m: (nb - 1 - m, 0))
    vec = pl.BlockSpec((1, 128), lambda m: (0, 0))
    return pl.pallas_call(
        body, name=name, grid=(nb,),
        in_specs=[row, vec, pl.BlockSpec((N_PAIRS, tm, 128), lambda m: (0, nb - 1 - m, 0))], out_specs=[row, vec],
        out_shape=[jax.ShapeDtypeStruct((M, 128), F32), jax.ShapeDtypeStruct((1, 128), F32)],
        scratch_shapes=[pltpu.VMEM((1, 128), F32)],
        compiler_params=_params(("arbitrary",)))(f_logit, bias, colsum)


def _adamw(w, parts, m, v, name):
    R, C = w.shape
    n_parts = parts.shape[0]
    tr = R
    for d in range(8, R, 8):
        if R % d == 0 and d * C <= ADAM_TILE_ELEMS:
            tr = d
    c1 = 1.0 - ADAM_B1 ** ADAM_STEP
    c2 = 1.0 - ADAM_B2 ** ADAM_STEP

    def body(w_ref, s_ref, m_ref, v_ref, g_ref, d_ref, mo_ref, vo_ref):
        g = s_ref[0].astype(F32)
        for k in range(1, n_parts):
            g = g + s_ref[k].astype(F32)
        mn = ADAM_B1 * m_ref[...] + (1.0 - ADAM_B1) * g
        vn = ADAM_B2 * v_ref[...] + (1.0 - ADAM_B2) * (g * g)
        m_hat = mn / c1
        v_hat = vn / c2
        g_ref[...] = g
        d_ref[...] = -ADAM_LR * (m_hat / (jnp.sqrt(v_hat) + ADAM_EPS) + ADAM_WD * w_ref[...])
        mo_ref[...] = mn
        vo_ref[...] = vn

    row = pl.BlockSpec((tr, C), lambda r: (r, 0))
    shp = jax.ShapeDtypeStruct((R, C), F32)
    return pl.pallas_call(
        body, name=name, grid=(R // tr,),
        in_specs=[row, pl.BlockSpec((n_parts, tr, C), lambda r: (0, r, 0)), row, row],
        out_specs=[row, row, row, row], out_shape=[shp, shp, shp, shp],
        compiler_params=_params(("parallel",)))(w, parts, m, v)


def _position():
    return lax.axis_index("x"), lax.axis_index("y"), lax.axis_index("c")


def _all_gather(blocks, name):
    n = len(blocks)

    def body(*refs):
        x_refs, out_refs = refs[:n], refs[n:2 * n]
        send_sems, recv_sems, local_sems = refs[2 * n:]
        x, y, c = _position()
        me, sibling = (x, y, c), (x, y, 1 - c)
        chips = [(1 - x, y), (x, 1 - y), (1 - x, 1 - y)]

        def copies(k, block, to, own=False):
            slot = 4 * block[0] + 2 * block[1] + block[2]
            return [pltpu.make_async_remote_copy(
                src_ref=x_refs[p] if own else out_refs[p].at[slot], dst_ref=out_refs[p].at[slot],
                send_sem=send_sems.at[k, p], recv_sem=recv_sems.at[k, p], device_id=to, device_id_type=MESH)
                for p in range(n)]

        mine = [pltpu.make_async_copy(x_refs[p], out_refs[p].at[4 * x + 2 * y + c], local_sems.at[p]) for p in range(n)]
        for cp in mine:
            cp.start()
        first = copies(0, me, sibling, own=True)
        for j, chip in enumerate(chips):
            first += copies(1 + j, me, (*chip, c), own=True)
        for cp in first:
            cp.start()
        passed = []
        for j, chip in enumerate(chips):
            for cp in copies(1 + j, (*chip, c), me):
                cp.wait_recv()
            onward = copies(4 + j, (*chip, c), sibling)
            for cp in onward:
                cp.start()
            passed += onward
        for cp in copies(0, sibling, me):
            cp.wait_recv()
        for j, chip in enumerate(chips):
            for cp in copies(4 + j, (*chip, 1 - c), me):
                cp.wait_recv()
        for cp in first + passed:
            cp.wait_send()
        for cp in mine:
            cp.wait()

    any_spec = pl.BlockSpec(memory_space=pl.ANY)
    return pl.pallas_call(
        body, name=name, out_shape=[jax.ShapeDtypeStruct((N_DEV,) + b.shape, b.dtype) for b in blocks],
        in_specs=[any_spec] * n, out_specs=[any_spec] * n,
        scratch_shapes=[pltpu.SemaphoreType.DMA((7, n)), pltpu.SemaphoreType.DMA((7, n)), pltpu.SemaphoreType.DMA((n,))],
    )(*blocks)


N_CHIPS = 4


def _exchange_siblings(parts, name):
    n = len(parts)

    def body(*refs):
        g_refs, land_refs = refs[:n], refs[n:2 * n]
        send_sems, recv_sems = refs[2 * n:]
        x, y, c = _position()
        sibling = (x, y, 1 - c)
        sends, recvs = [], []
        for q in range(N_CHIPS):
            for p in range(n):
                sends.append(pltpu.make_async_remote_copy(
                    src_ref=g_refs[p].at[2 * q + (1 - c)], dst_ref=land_refs[p].at[q], send_sem=send_sems.at[q, p],
                    recv_sem=recv_sems.at[q, p], device_id=sibling, device_id_type=MESH))
                recvs.append(pltpu.make_async_remote_copy(
                    src_ref=g_refs[p].at[2 * q + c], dst_ref=land_refs[p].at[q], send_sem=send_sems.at[q, p],
                    recv_sem=recv_sems.at[q, p], device_id=sibling, device_id_type=MESH))
        for cp in sends:
            cp.start()
        for cp in recvs:
            cp.wait_recv()
        for cp in sends:
            cp.wait_send()

    any_spec = pl.BlockSpec(memory_space=pl.ANY)
    return pl.pallas_call(
        body, name=name, out_shape=[jax.ShapeDtypeStruct((N_CHIPS,) + p.shape[1:], p.dtype) for p in parts],
        in_specs=[any_spec] * n, out_specs=[any_spec] * n,
        scratch_shapes=[pltpu.SemaphoreType.DMA((N_CHIPS, n)), pltpu.SemaphoreType.DMA((N_CHIPS, n))],
    )(*parts)


def _pair_sum(part, from_sibling, name):
    _, R, C = part.shape
    tr = R
    for d in range(8, R, 8):
        if R % d == 0 and d * C <= ADAM_TILE_ELEMS:
            tr = d

    def body(a_ref, b_ref, o_ref):
        c = lax.axis_index("c")
        for q in range(N_CHIPS):
            o_ref[q] = (a_ref[2 * q + c].astype(F32) + b_ref[q].astype(F32)).astype(o_ref.dtype)

    return pl.pallas_call(
        body, name=name, grid=(R // tr,),
        in_specs=[pl.BlockSpec((N_DEV, tr, C), lambda r: (0, r, 0)), pl.BlockSpec((N_CHIPS, tr, C), lambda r: (0, r, 0))],
        out_specs=pl.BlockSpec((N_CHIPS, tr, C), lambda r: (0, r, 0)),
        out_shape=jax.ShapeDtypeStruct((N_CHIPS, R, C), part.dtype),
        compiler_params=_params(("parallel",)))(part, from_sibling)


def _exchange_chips(sums, name):
    n = len(sums)

    def body(*refs):
        g_refs, land_refs = refs[:n], refs[n:2 * n]
        send_sems, recv_sems, local_sems = refs[2 * n:]
        x, y, c = _position()
        me = 2 * x + y
        mine = [pltpu.make_async_copy(g_refs[p].at[me], land_refs[p].at[me], local_sems.at[p]) for p in range(n)]
        for cp in mine:
            cp.start()
        sends, recvs = [], []
        for k in range(1, N_CHIPS):
            px = 1 - x if k & 2 else x
            py = 1 - y if k & 1 else y
            peer = 2 * px + py
            for p in range(n):
                sends.append(pltpu.make_async_remote_copy(
                    src_ref=g_refs[p].at[peer], dst_ref=land_refs[p].at[me], send_sem=send_sems.at[k - 1, p],
                    recv_sem=recv_sems.at[k - 1, p], device_id=(px, py, c), device_id_type=MESH))
                recvs.append(pltpu.make_async_remote_copy(
                    src_ref=g_refs[p].at[me], dst_ref=land_refs[p].at[peer], send_sem=send_sems.at[k - 1, p],
                    recv_sem=recv_sems.at[k - 1, p], device_id=(px, py, c), device_id_type=MESH))
        for cp in sends:
            cp.start()
        for cp in recvs:
            cp.wait_recv()
        for cp in sends:
            cp.wait_send()
        for cp in mine:
            cp.wait()

    any_spec = pl.BlockSpec(memory_space=pl.ANY)
    return pl.pallas_call(
        body, name=name, out_shape=[jax.ShapeDtypeStruct(p.shape, p.dtype) for p in sums],
        in_specs=[any_spec] * n, out_specs=[any_spec] * n,
        scratch_shapes=[pltpu.SemaphoreType.DMA((3, n)), pltpu.SemaphoreType.DMA((3, n)), pltpu.SemaphoreType.DMA((n,))],
    )(*sums)


SHARDED = (("sb_w_qkv", 2), ("sb_w_o", 1), ("mla_w_down", 1), ("mla_w_uq", 2), ("mla_w_ukv", 2), ("mla_w_o", 1),
           ("fox_w_qkvf", 2), ("fox_w_o", 1), ("ffn_w_gate", 2), ("ffn_w_up", 2), ("ffn_w_down", 1),
           ("pool_w", 2), ("meta", 1), ("mla_q_norm", 1), ("mla_kv_norm", 1))
KEPT_F32 = ("meta", "mla_q_norm", "mla_kv_norm")
REPLICATED = ("norm_mix", "norm_ffn", "pool_scale", "fox_b_f", "final_norm")
WEIGHT_NAMES = ("meta", "norm_mix", "norm_ffn", "pool_w", "pool_scale", "sb_w_qkv", "sb_w_o", "mla_w_down",
                "mla_q_norm", "mla_kv_norm", "mla_w_uq", "mla_w_ukv", "mla_w_o", "fox_w_qkvf", "fox_b_f",
                "fox_w_o", "ffn_w_gate", "ffn_w_up", "ffn_w_down", "final_norm")
LANES = 1024


def _pack_rows(arrays, names):
    parts = []
    for n in names:
        flat = arrays[n].reshape(-1).astype(F32)
        rows = -(-flat.shape[0] // LANES)
        parts.append(jnp.pad(flat, (0, rows * LANES - flat.shape[0])).reshape(rows, LANES))
    rows = sum(p.shape[0] for p in parts)
    parts.append(jnp.zeros((-(-rows // 8) * 8 - rows, LANES), F32))
    return jnp.concatenate(parts, axis=0)


def _unpack_rows(buf, shapes, names):
    out, row = {}, 0
    for n in names:
        size = int(np.prod(shapes[n]))
        rows = -(-size // LANES)
        out[n] = buf[row:row + rows].reshape(-1)[:size].reshape(shapes[n])
        row += rows
    return out


def _whole_from_gathered(g, axis):
    g = jnp.moveaxis(g, 0, axis)
    shp = g.shape
    return g.reshape(shp[:axis] + (shp[axis] * shp[axis + 1],) + shp[axis + 2:])


def _parts_from_whole(whole, axis):
    shp = whole.shape
    g = whole.reshape(shp[:axis] + (N_DEV, shp[axis] // N_DEV) + shp[axis + 1:])
    return jnp.moveaxis(g, axis, 0)


def _kernel_weights(full):
    W = {}
    W["pool_w"] = full["pool_w"][0]
    W["sb_w_qkv"] = full["sb_w_qkv"][0]
    W["sb_w_o"] = full["sb_w_o"][0]
    W["mla_w_down"] = full["mla_w_down"][0]
    uq = full["mla_w_uq"][0].reshape(MLA_Q_RANK, N_HEADS, MLA_NOPE + MLA_ROPE)
    nope = uq[:, :, :MLA_NOPE].reshape(MLA_Q_RANK, N_HEADS * MLA_NOPE)
    rope = uq[:, :, MLA_NOPE:].reshape(MLA_Q_RANK, N_PAIRS, 2 * MLA_ROPE)
    rope = jnp.pad(rope, ((0, 0), (0, 0), (0, 128 - 2 * MLA_ROPE))).reshape(MLA_Q_RANK, N_PAIRS * 128)
    W["mla_w_uq"] = jnp.concatenate([nope, rope], axis=1)
    ukv = full["mla_w_ukv"][0].reshape(MLA_KV_RANK, N_HEADS, 2, HEAD_DIM)
    W["mla_w_ukv"] = jnp.transpose(ukv, (0, 2, 1, 3)).reshape(MLA_KV_RANK, 2 * N_HEADS * HEAD_DIM)
    W["mla_w_o"] = full["mla_w_o"][0]
    qkvf = full["fox_w_qkvf"][0]
    n_qkv = 3 * N_HEADS * HEAD_DIM
    W["fox_w_qkv"] = qkvf[:, :n_qkv]
    W["fox_w_f"] = jnp.pad(qkvf[:, n_qkv:], ((0, 0), (0, 128 - N_HEADS)))
    W["fox_w_qkvf"] = jnp.concatenate([W["fox_w_qkv"], W["fox_w_f"]], axis=1)
    W["fox_w_o"] = full["fox_w_o"][0]
    W["ffn_w_gate"] = full["ffn_w_gate"]
    W["ffn_w_up"] = full["ffn_w_up"]
    W["ffn_w_down"] = full["ffn_w_down"]
    return W


def _reference_grads(G):
    out = {}
    out["pool_w"] = G["pool_w"][None]
    for n in ("sb_w_qkv", "sb_w_o", "mla_w_down", "mla_w_o", "fox_w_o"):
        out[n] = G[n][None]
    duq = G["mla_w_uq"]
    nope = duq[:, :N_HEADS * MLA_NOPE].reshape(MLA_Q_RANK, N_HEADS, MLA_NOPE)
    rope = duq[:, N_HEADS * MLA_NOPE:].reshape(MLA_Q_RANK, N_PAIRS, 128)[:, :, :2 * MLA_ROPE]
    rope = rope.reshape(MLA_Q_RANK, N_HEADS, MLA_ROPE)
    out["mla_w_uq"] = jnp.concatenate([nope, rope], axis=2).reshape(1, MLA_Q_RANK, -1)
    dukv = G["mla_w_ukv"].reshape(MLA_KV_RANK, 2, N_HEADS, HEAD_DIM)
    out["mla_w_ukv"] = jnp.transpose(dukv, (0, 2, 1, 3)).reshape(1, MLA_KV_RANK, -1)
    out["fox_w_qkvf"] = G["fox_w_qkvf"][None, :, :3 * N_HEADS * HEAD_DIM + N_HEADS]
    out["ffn_w_gate"] = G["ffn_w_gate"]
    out["ffn_w_up"] = G["ffn_w_up"]
    out["ffn_w_down"] = G["ffn_w_down"]
    out["mla_q_norm"] = G["mla_q_norm"]
    out["mla_kv_norm"] = G["mla_kv_norm"]
    return out


def _pairs_col(f16):
    M = f16.shape[0]
    return jnp.transpose(f16.reshape(M, N_PAIRS, 2), (1, 0, 2))


def _pairs_row(f16):
    M = f16.shape[0]
    return jnp.transpose(f16.reshape(M, N_PAIRS, 2), (1, 2, 0))


def _local_step(x, target, W, P):
    S = x.shape[0]
    M = S + ROW0
    G = {}
    gain = lambda name, i: P[name][i][None, :]
    h0 = jnp.concatenate([jnp.zeros((PAD, D_MODEL), F32), P["meta"], x], axis=0)

    def ffn_fwd(h1, i):
        b = _norm_fwd(h1, gain("norm_ffn", i), BF16, f"ffn{i}_norm")
        g, u, act = _ffn_up(b, W["ffn_w_gate"][i], W["ffn_w_up"][i], f"ffn{i}_up")
        h2 = _mm_nn(act, W["ffn_w_down"][i], F32, f"ffn{i}_down", res=h1)
        return h2, (h1, b, g, u, act)

    def ffn_bwd(dh2, saved, i):
        h1, b, g, u, act = saved
        dg, du = _ffn_dact(dh2, W["ffn_w_down"][i], g, u, f"ffn{i}_dact")
        G.setdefault("ffn_w_down", {})[i] = _mm_tn(act, dh2, f"ffn{i}_dwd")
        G.setdefault("ffn_w_gate", {})[i] = _mm_tn(b, dg, f"ffn{i}_dwg")
        G.setdefault("ffn_w_up", {})[i] = _mm_tn(b, du, f"ffn{i}_dwu")
        dh1, dgain = _mm_nt_dnorm([(dg, W["ffn_w_gate"][i]), (du, W["ffn_w_up"][i])], h1, gain("norm_ffn", i), dh2,
                                  f"ffn{i}_db_dnorm")
        G.setdefault("norm_ffn", {})[i] = dgain
        return dh1

    a0 = _norm_fwd(h0, gain("norm_mix", 0), F32, "mix0_norm")
    h1_0, pooled = _pool_fwd(h0, a0, W["pool_w"], P["pool_scale"], "pool_fwd")
    h_1, ffn0 = ffn_fwd(h1_0, 0)

    sb_scale = HEAD_DIM ** -0.5
    a1 = _norm_fwd(h_1, gain("norm_mix", 1), BF16, "mix1_norm")
    sb_qkv = _mm_nn(a1, W["sb_w_qkv"], BF16, "sb_qkv")
    sb_o, sb_tot = _sb_fwd(sb_qkv, sb_scale, "sb_fwd")
    h1_1 = _mm_nn(sb_o, W["sb_w_o"], F32, "sb_out", res=h_1)
    h_2, ffn1 = ffn_fwd(h1_1, 1)

    mla_scale = (MLA_NOPE + MLA_ROPE) ** -0.5
    cos_t, sin_t = _rope_tables(M)
    a2 = _norm_fwd(h_2, gain("norm_mix", 2), BF16, "mix2_norm")
    down = _mm_nn(a2, W["mla_w_down"], F32, "mla_down")
    dq_raw = down[:, :MLA_Q_RANK]
    dkv_raw = down[:, MLA_Q_RANK:MLA_Q_RANK + MLA_KV_RANK]
    kr_raw = down[:, MLA_Q_RANK + MLA_KV_RANK:]
    c_q = _norm_fwd(dq_raw, P["mla_q_norm"], BF16, "mla_qnorm")
    c_kv = _norm_fwd(dkv_raw, P["mla_kv_norm"], BF16, "mla_kvnorm")
    q_lin = _mm_nn(c_q, W["mla_w_uq"], F32, "mla_uq")
    q_all = _rope(q_lin, cos_t, sin_t, BF16, "mla_qrope", lead=D_MODEL)
    kv_all = _mm_nn(c_kv, W["mla_w_ukv"], BF16, "mla_ukv")
    kr_in = jnp.concatenate([kr_raw, kr_raw, jnp.zeros((M, 64), F32)], axis=1)
    kr = _rope(kr_in, cos_t, sin_t, BF16, "mla_krope")
    q_rope = q_all[:, D_MODEL:]
    mla_o, mla_lse = _mla_fwd(q_all, kv_all, q_rope, kr, mla_scale, "mla_fwd")
    h1_2 = _mm_nn(mla_o, W["mla_w_o"], F32, "mla_out", res=h_2)
    h_3, ffn2 = ffn_fwd(h1_2, 2)

    fox_scale = HEAD_DIM ** -0.5
    a3 = _norm_fwd(h_3, gain("norm_mix", 3), BF16, "mix3_norm")
    fox_qkv = _mm_nn(a3, W["fox_w_qkv"], BF16, "fox_qkv")
    f_logit = _mm_nn(a3, W["fox_w_f"], F32, "fox_f")
    b_f = jnp.pad(P["fox_b_f"], ((0, 0), (0, 128 - N_HEADS)))
    Fc = _forget_cumsum(f_logit, b_f, "fox_cumsum")
    f_rows, f_cols = _pairs_row(Fc[:, :N_HEADS]), _pairs_col(Fc[:, :N_HEADS])
    fox_qkv_t = fox_qkv.T
    fox_o_t, fox_lse, fox_ox_t = _fox_fwd(fox_qkv, fox_qkv_t, f_rows, f_cols, fox_scale, "fox_fwd")
    fox_o = fox_o_t.T
    h1_3 = _mm_nn(fox_o, W["fox_w_o"], F32, "fox_out", res=h_3)
    h_4, ffn3 = ffn_fwd(h1_3, 3)

    sq, dh, dgain = _loss_head(h_4, P["final_norm"][None, :], target, "loss_head")
    G["final_norm"] = dgain[0]

    dh = ffn_bwd(dh, ffn3, 3)
    do = _mm_nt(dh, W["fox_w_o"], BF16, "fox_do")
    G["fox_w_o"] = _mm_tn(fox_o, dh, "fox_dwo")
    dq_t, dk, dv, colsum = _fox_bwd(fox_qkv, fox_qkv_t, fox_ox_t, do, do.T, fox_lse, f_rows, f_cols, fox_scale,
                                    "fox_bwd")
    dlogit, db_f = _forget_cumsum_bwd(f_logit, b_f, colsum, "fox_dcumsum")
    G["fox_b_f"] = db_f[:, :N_HEADS]
    dproj = jnp.concatenate([dq_t.T, dk.astype(BF16), dv.astype(BF16), dlogit.astype(BF16)], axis=1)
    G["fox_w_qkvf"] = _mm_tn(a3, dproj, "fox_dwqkvf")
    dh, dgain = _mm_nt_dnorm([(dproj, W["fox_w_qkvf"])], h_3, gain("norm_mix", 3), dh, "fox_da_dnorm")
    G.setdefault("norm_mix", {})[3] = dgain

    dh = ffn_bwd(dh, ffn2, 2)
    do = _mm_nt(dh, W["mla_w_o"], BF16, "mla_do")
    G["mla_w_o"] = _mm_tn(mla_o, dh, "mla_dwo")
    dq, dk, dv, dqr, dkr = _mla_bwd(q_all, kv_all, q_rope, kr, mla_o, do, mla_lse, mla_scale, "mla_bwd")
    dqr = _rope(dqr, cos_t, sin_t, BF16, "mla_dqrope", inverse=True)
    dq_all = jnp.concatenate([dq, dqr], axis=1)
    dkr_sum = _rope(jnp.sum(dkr, axis=0), cos_t, sin_t, F32, "mla_dkrope", inverse=True)
    dkr_raw = dkr_sum[:, :MLA_ROPE] + dkr_sum[:, MLA_ROPE:2 * MLA_ROPE]
    dkv_all = jnp.concatenate([dk.astype(BF16), dv.astype(BF16)], axis=1)
    G["mla_w_uq"] = _mm_tn(c_q, dq_all, "mla_dwuq")
    G["mla_w_ukv"] = _mm_tn(c_kv, dkv_all, "mla_dwukv")
    ddq_raw, G["mla_q_norm"] = _mm_nt_dnorm([(dq_all, W["mla_w_uq"])], dq_raw, P["mla_q_norm"], None, "mla_dcq_dnorm")
    ddkv_raw, G["mla_kv_norm"] = _mm_nt_dnorm([(dkv_all, W["mla_w_ukv"])], dkv_raw, P["mla_kv_norm"], None,
                                              "mla_dckv_dnorm")
    ddown = jnp.concatenate([ddq_raw, ddkv_raw, dkr_raw], axis=1).astype(BF16)
    G["mla_w_down"] = _mm_tn(a2, ddown, "mla_dwdown")
    dh, dgain = _mm_nt_dnorm([(ddown, W["mla_w_down"])], h_2, gain("norm_mix", 2), dh, "mla_da_dnorm")
    G["norm_mix"][2] = dgain

    dh = ffn_bwd(dh, ffn1, 1)
    do = _mm_nt(dh, W["sb_w_o"], BF16, "sb_do")
    G["sb_w_o"] = _mm_tn(sb_o, dh, "sb_dwo")
    dq, dk, dv = _sb_bwd(sb_qkv, do, sb_tot, sb_scale, "sb_bwd")
    dqkv = jnp.concatenate([dq, dk.astype(BF16), dv.astype(BF16)], axis=1)
    G["sb_w_qkv"] = _mm_tn(a1, dqkv, "sb_dwqkv")
    dh, dgain = _mm_nt_dnorm([(dqkv, W["sb_w_qkv"])], h_1, gain("norm_mix", 1), dh, "sb_da_dnorm")
    G["norm_mix"][1] = dgain

    dh = ffn_bwd(dh, ffn0, 0)
    dpc, G["pool_w"], G["pool_scale"] = _pool_bwd_mix(dh, pooled, W["pool_w"], P["pool_scale"], "pool_dmix")
    da = _pool_bwd_window(dpc, "pool_dwindow")
    dh, dgain, dx = _norm_bwd(h0, gain("norm_mix", 0), da, dh, "mix0_dnorm", token_rows=True)
    G["norm_mix"][0] = dgain

    G["norm_mix"] = jnp.concatenate([G["norm_mix"][i] for i in range(DEPTH)], axis=0)
    G["norm_ffn"] = jnp.concatenate([G["norm_ffn"][i] for i in range(DEPTH)], axis=0)
    G["ffn_w_down"] = jnp.stack([G["ffn_w_down"][i] for i in range(DEPTH)])
    G["ffn_w_gate"] = jnp.stack([G["ffn_w_gate"][i] for i in range(DEPTH)])
    G["ffn_w_up"] = jnp.stack([G["ffn_w_up"][i] for i in range(DEPTH)])
    G["meta"] = dh[PAD:ROW0]
    return sq, dx, G


def kernel(x, meta, norm_mix, norm_ffn, pool_w, pool_scale, sb_w_qkv, sb_w_o, mla_w_down, mla_q_norm, mla_kv_norm, mla_w_uq, mla_w_ukv, mla_w_o, fox_w_qkvf, fox_b_f, fox_w_o, ffn_w_gate, ffn_w_up, ffn_w_down, final_norm, loss_target, m_meta, m_norm_mix, m_norm_ffn, m_pool_w, m_pool_scale, m_sb_w_qkv, m_sb_w_o, m_mla_w_down, m_mla_q_norm, m_mla_kv_norm, m_mla_w_uq, m_mla_w_ukv, m_mla_w_o, m_fox_w_qkvf, m_fox_b_f, m_fox_w_o, m_ffn_w_gate, m_ffn_w_up, m_ffn_w_down, m_final_norm, v_meta, v_norm_mix, v_norm_ffn, v_pool_w, v_pool_scale, v_sb_w_qkv, v_sb_w_o, v_mla_w_down, v_mla_q_norm, v_mla_kv_norm, v_mla_w_uq, v_mla_w_ukv, v_mla_w_o, v_fox_w_qkvf, v_fox_b_f, v_fox_w_o, v_ffn_w_gate, v_ffn_w_up, v_ffn_w_down, v_final_norm):
    w = dict(meta=meta, norm_mix=norm_mix, norm_ffn=norm_ffn, pool_w=pool_w, pool_scale=pool_scale,
             sb_w_qkv=sb_w_qkv, sb_w_o=sb_w_o, mla_w_down=mla_w_down, mla_q_norm=mla_q_norm,
             mla_kv_norm=mla_kv_norm, mla_w_uq=mla_w_uq, mla_w_ukv=mla_w_ukv, mla_w_o=mla_w_o,
             fox_w_qkvf=fox_w_qkvf, fox_b_f=fox_b_f, fox_w_o=fox_w_o, ffn_w_gate=ffn_w_gate, ffn_w_up=ffn_w_up,
             ffn_w_down=ffn_w_down, final_norm=final_norm)
    m = dict(meta=m_meta, norm_mix=m_norm_mix, norm_ffn=m_norm_ffn, pool_w=m_pool_w, pool_scale=m_pool_scale,
             sb_w_qkv=m_sb_w_qkv, sb_w_o=m_sb_w_o, mla_w_down=m_mla_w_down, mla_q_norm=m_mla_q_norm,
             mla_kv_norm=m_mla_kv_norm, mla_w_uq=m_mla_w_uq, mla_w_ukv=m_mla_w_ukv, mla_w_o=m_mla_w_o,
             fox_w_qkvf=m_fox_w_qkvf, fox_b_f=m_fox_b_f, fox_w_o=m_fox_w_o, ffn_w_gate=m_ffn_w_gate,
             ffn_w_up=m_ffn_w_up, ffn_w_down=m_ffn_w_down, final_norm=m_final_norm)
    v = dict(meta=v_meta, norm_mix=v_norm_mix, norm_ffn=v_norm_ffn, pool_w=v_pool_w, pool_scale=v_pool_scale,
             sb_w_qkv=v_sb_w_qkv, sb_w_o=v_sb_w_o, mla_w_down=v_mla_w_down, mla_q_norm=v_mla_q_norm,
             mla_kv_norm=v_mla_kv_norm, mla_w_uq=v_mla_w_uq, mla_w_ukv=v_mla_w_ukv, mla_w_o=v_mla_w_o,
             fox_w_qkvf=v_fox_w_qkvf, fox_b_f=v_fox_b_f, fox_w_o=v_fox_w_o, ffn_w_gate=v_ffn_w_gate,
             ffn_w_up=v_ffn_w_up, ffn_w_down=v_ffn_w_down, final_norm=v_final_norm)

    sh_names = tuple(n for n, _ in SHARDED)
    sh_axis = dict(SHARDED)
    shapes = {n: w[n].shape for n in WEIGHT_NAMES}
    wire = lambda n: F32 if n in KEPT_F32 else BF16

    gathered = _all_gather([w[n].astype(wire(n)) for n in sh_names], "gather_weights")
    full = {n: _whole_from_gathered(g, sh_axis[n]) for n, g in zip(sh_names, gathered)}
    W = _kernel_weights(full)
    P = dict(meta=full["meta"], mla_q_norm=full["mla_q_norm"], mla_kv_norm=full["mla_kv_norm"],
             norm_mix=norm_mix, norm_ffn=norm_ffn, pool_scale=pool_scale, fox_b_f=fox_b_f, final_norm=final_norm)

    sq, dx, G = _local_step(x[0], loss_target[0], W, P)
    loss = lax.psum(0.5 * jnp.sum(sq) / D_MODEL, ("x", "y", "c"))
    grad_x = dx[None]

    gw = _reference_grads(G)
    gw["meta"] = G["meta"]
    rc = {n: (int(np.prod(shapes[n][:-1])), shapes[n][-1]) for n in sh_names}
    parts = [_parts_from_whole(gw[n], sh_axis[n]).astype(wire(n)).reshape((N_DEV,) + rc[n]) for n in sh_names]
    from_sibling = _exchange_siblings(parts, "exchange_grads_d2d")
    sums = [_pair_sum(a, b, f"pair_sum_{n}") for n, a, b in zip(sh_names, parts, from_sibling)]
    landed = _exchange_chips(sums, "exchange_grads_ici")
    results = {}
    for n, got in zip(sh_names, landed):
        outs = _adamw(w[n].reshape(rc[n]), got, m[n].reshape(rc[n]), v[n].reshape(rc[n]), f"adamw_{n}")
        results[n] = [o.reshape(shapes[n]) for o in outs]

    rep_g = dict(norm_mix=G["norm_mix"], norm_ffn=G["norm_ffn"], pool_scale=G["pool_scale"], fox_b_f=G["fox_b_f"],
                 final_norm=G["final_norm"])
    (rep_all,) = _all_gather([_pack_rows(rep_g, REPLICATED)], "gather_replicated_grads")
    rep_out = _adamw(_pack_rows(w, REPLICATED), rep_all, _pack_rows(m, REPLICATED), _pack_rows(v, REPLICATED),
                     "adamw_replicated")
    rep = [_unpack_rows(o, shapes, REPLICATED) for o in rep_out]
    for n in REPLICATED:
        results[n] = [r[n] for r in rep]

    outs = [results[n][k] for k in range(4) for n in WEIGHT_NAMES]
    return (loss, grad_x, *outs)
```

```python
import numpy as np
import jax
import jax.numpy as jnp
from jax import lax
from jax.experimental import pallas as pl
from jax.experimental.pallas import tpu as pltpu

F32 = jnp.float32
BF16 = jnp.bfloat16

N_DEV = 8
D_MODEL = 1024
N_META = 16
PAD = 240
ROW0 = PAD + N_META
EPS = 1e-6
POOL_WINDOWS = (2, 4, 8, 16)
POOL_GROUP = 256
HALO = 128
N_HEADS = 16
HEAD_DIM = 64
N_PAIRS = N_HEADS // 2
MLA_Q_RANK = 384
MLA_KV_RANK = 256
MLA_NOPE = 64
MLA_ROPE = 32
ROPE_THETA = 10000.0
D_FF = 2816
DEPTH = 4
ATTN_TILE = 768
ATTN_BWD_TILE = 768
WALK_TILE = 256
FOX_TILE = 384
NEG = -1e30
LOG2E = 1.4426950408889634
EXP_ZERO = -110.0
VMEM_LIMIT = 56 * 2**20
ADAM_TILE_ELEMS = 192 * 1024

ADAM_LR = 0.001
ADAM_B1 = 0.9
ADAM_B2 = 0.999
ADAM_EPS = 1e-08
ADAM_WD = 0.01
ADAM_STEP = 10

MESH = pl.DeviceIdType.MESH


def _params(sem=None):
    return pltpu.CompilerParams(dimension_semantics=sem, vmem_limit_bytes=VMEM_LIMIT)


def _pick(n, cands):
    for c in cands:
        if n % c == 0:
            return c
    return n


def _col_tile(n, cap=1536):
    best = None
    for t in range(128, min(n, cap) + 1, 128):
        if n % t == 0:
            best = t
    return best if best is not None else n


def _dot(a, b):
    return jnp.dot(a, b, preferred_element_type=F32)


def _dot_nt(a, b):
    return lax.dot_general(a, b, (((1,), (1,)), ((), ())), preferred_element_type=F32)


def _dot_tn(a, b):
    return lax.dot_general(a, b, (((0,), (0,)), ((), ())), preferred_element_type=F32)


def _mm_nn(a, b, out_dtype, name, res=None):
    M, K = a.shape
    N = b.shape[1]
    tm = _pick(M, (768, 512, 256, 128))
    tn = _col_tile(N)

    def body(*refs):
        if res is None:
            a_ref, b_ref, o_ref = refs
        else:
            a_ref, b_ref, r_ref, o_ref = refs
        acc = _dot(a_ref[...].astype(BF16), b_ref[...])
        if res is not None:
            acc = acc + r_ref[...]
        o_ref[...] = acc.astype(o_ref.dtype)

    in_specs = [pl.BlockSpec((tm, K), lambda n, m: (m, 0)), pl.BlockSpec((K, tn), lambda n, m: (0, n))]
    args = [a, b]
    if res is not None:
        in_specs.append(pl.BlockSpec((tm, tn), lambda n, m: (m, n)))
        args.append(res)
    return pl.pallas_call(
        body, name=name, grid=(N // tn, M // tm), in_specs=in_specs,
        out_specs=pl.BlockSpec((tm, tn), lambda n, m: (m, n)),
        out_shape=jax.ShapeDtypeStruct((M, N), out_dtype),
        compiler_params=_params(("parallel", "parallel")))(*args)


def _norm_mm(h, gain, b, out_dtype, name):
    M, K = h.shape
    N = b.shape[1]
    tm = _pick(M, (384, 256, 128))

    def body(h_ref, g_ref, b_ref, o_ref, a_ref):
        x = h_ref[...]
        r = lax.rsqrt(jnp.mean(x * x, axis=-1, keepdims=True) + EPS)
        a = ((x * r) * g_ref[...]).astype(BF16)
        a_ref[...] = a
        o_ref[...] = _dot(a, b_ref[...]).astype(o_ref.dtype)

    return pl.pallas_call(
        body, name=name, grid=(M // tm,),
        in_specs=[pl.BlockSpec((tm, K), lambda m: (m, 0)), pl.BlockSpec((1, K), lambda m: (0, 0)),
                  pl.BlockSpec((K, N), lambda m: (0, 0))],
        out_specs=[pl.BlockSpec((tm, N), lambda m: (m, 0)), pl.BlockSpec((tm, K), lambda m: (m, 0))],
        out_shape=[jax.ShapeDtypeStruct((M, N), out_dtype), jax.ShapeDtypeStruct((M, K), BF16)],
        compiler_params=_params(("parallel",)))(h, gain, b)


def _mm_nt(a, w, out_dtype, name):
    M, N = a.shape
    K = w.shape[0]
    tm = _pick(M, (768, 512, 256, 128)) if N <= 3200 else _pick(M, (256, 128))
    tk = _col_tile(K, 1024)

    def body(a_ref, w_ref, o_ref):
        o_ref[...] = _dot_nt(a_ref[...].astype(BF16), w_ref[...]).astype(o_ref.dtype)

    return pl.pallas_call(
        body, name=name, grid=(K // tk, M // tm),
        in_specs=[pl.BlockSpec((tm, N), lambda k, m: (m, 0)), pl.BlockSpec((tk, N), lambda k, m: (k, 0))],
        out_specs=pl.BlockSpec((tm, tk), lambda k, m: (m, k)),
        out_shape=jax.ShapeDtypeStruct((M, K), out_dtype),
        compiler_params=_params(("parallel", "parallel")))(a, w)


def _mm_nt_dnorm(pairs, h, gain, dres, name):
    M, K = h.shape
    n = len(pairs)
    tm = _pick(M, (384, 256, 128))

    def body(*refs):
        refs = list(refs)
        ab, rest = refs[:2 * n], refs[2 * n:]
        h_ref, g_ref = rest[0], rest[1]
        dr_ref = rest[2] if dres is not None else None
        dh_ref, dg_ref = rest[-2], rest[-1]
        da = None
        for i in range(n):
            d = _dot_nt(ab[2 * i][...], ab[2 * i + 1][...])
            da = d if da is None else da + d
        x = h_ref[...]
        r = lax.rsqrt(jnp.mean(x * x, axis=-1, keepdims=True) + EPS)
        y = x * r
        dy = da * g_ref[...]
        dh = r * (dy - y * jnp.mean(dy * y, axis=-1, keepdims=True))
        if dr_ref is not None:
            dh = dh + dr_ref[...]
        dh_ref[...] = dh

        @pl.when(pl.program_id(0) == 0)
        def _():
            dg_ref[...] = jnp.zeros_like(dg_ref)
        dg_ref[...] += jnp.sum(da * y, axis=0, keepdims=True)

    row = pl.BlockSpec((tm, K), lambda m: (m, 0))
    vec = pl.BlockSpec((1, K), lambda m: (0, 0))
    in_specs, args = [], []
    for a, w in pairs:
        in_specs += [pl.BlockSpec((tm, a.shape[1]), lambda m: (m, 0)), pl.BlockSpec(w.shape, lambda m: (0, 0))]
        args += [a, w]
    in_specs += [row, vec] + ([row] if dres is not None else [])
    args += [h, gain] + ([dres] if dres is not None else [])
    return pl.pallas_call(
        body, name=name, grid=(M // tm,), in_specs=in_specs, out_specs=[row, vec],
        out_shape=[jax.ShapeDtypeStruct((M, K), F32), jax.ShapeDtypeStruct((1, K), F32)],
        compiler_params=_params(("arbitrary",)))(*args)


def _mm_tn(a, b, name):
    M, K = a.shape
    N = b.shape[1]
    tm = _pick(M, (768, 512, 256, 128))
    tk = _col_tile(K, 1408)
    tn = _col_tile(N, 1408)

    def body(a_ref, b_ref, o_ref):
        @pl.when(pl.program_id(2) == 0)
        def _():
            o_ref[...] = jnp.zeros_like(o_ref)
        o_ref[...] += _dot_tn(a_ref[...].astype(BF16), b_ref[...].astype(BF16))

    return pl.pallas_call(
        body, name=name, grid=(K // tk, N // tn, M // tm),
        in_specs=[pl.BlockSpec((tm, tk), lambda k, n, m: (m, k)), pl.BlockSpec((tm, tn), lambda k, n, m: (m, n))],
        out_specs=pl.BlockSpec((tk, tn), lambda k, n, m: (k, n)),
        out_shape=jax.ShapeDtypeStruct((K, N), F32),
        compiler_params=_params(("parallel", "parallel", "arbitrary")))(a, b)


def _norm_fwd(h, gain, out_dtype, name):
    M, C = h.shape
    tm = _pick(M, (768, 512, 256, 128))

    def body(h_ref, g_ref, a_ref):
        x = h_ref[...]
        r = lax.rsqrt(jnp.mean(x * x, axis=-1, keepdims=True) + EPS)
        a_ref[...] = ((x * r) * g_ref[...]).astype(a_ref.dtype)

    return pl.pallas_call(
        body, name=name, grid=(M // tm,),
        in_specs=[pl.BlockSpec((tm, C), lambda m: (m, 0)), pl.BlockSpec((1, C), lambda m: (0, 0))],
        out_specs=pl.BlockSpec((tm, C), lambda m: (m, 0)),
        out_shape=jax.ShapeDtypeStruct((M, C), out_dtype),
        compiler_params=_params(("parallel",)))(h, gain)


def _norm_bwd(h, gain, da, dres, name, token_rows=False):
    M, C = h.shape
    tm = ROW0 if token_rows else _pick(M, (768, 512, 256, 128))

    def body(*refs):
        refs = list(refs)
        dx_ref = refs.pop() if token_rows else None
        if dres is None:
            h_ref, g_ref, da_ref, dh_ref, dg_ref = refs
        else:
            h_ref, g_ref, da_ref, dr_ref, dh_ref, dg_ref = refs
        x = h_ref[...]
        r = lax.rsqrt(jnp.mean(x * x, axis=-1, keepdims=True) + EPS)
        y = x * r
        dav = da_ref[...].astype(F32)
        dy = dav * g_ref[...]
        dh = r * (dy - y * jnp.mean(dy * y, axis=-1, keepdims=True))
        if dres is not None:
            dh = dh + dr_ref[...]
        dh_ref[...] = dh
        if token_rows:
            dx_ref[...] = dh

        @pl.when(pl.program_id(0) == 0)
        def _():
            dg_ref[...] = jnp.zeros_like(dg_ref)
        dg_ref[...] += jnp.sum(dav * y, axis=0, keepdims=True)

    row = pl.BlockSpec((tm, C), lambda m: (m, 0))
    vec = pl.BlockSpec((1, C), lambda m: (0, 0))
    in_specs = [row, vec, row] + ([row] if dres is not None else [])
    args = [h, gain, da] + ([dres] if dres is not None else [])
    out_specs = [row, vec]
    out_shape = [jax.ShapeDtypeStruct((M, C), F32), jax.ShapeDtypeStruct((1, C), F32)]
    if token_rows:
        out_specs.append(pl.BlockSpec((tm, C), lambda m: (jnp.maximum(m - 1, 0), 0)))
        out_shape.append(jax.ShapeDtypeStruct((M - ROW0, C), F32))
    return pl.pallas_call(
        body, name=name, grid=(M // tm,), in_specs=in_specs, out_specs=out_specs, out_shape=out_shape,
        compiler_params=_params(("arbitrary",)))(*args)


def _ffn_up(h, gain, w_g, w_u, name):
    M, K = h.shape
    F = w_g.shape[1]
    tm = _pick(M, (384, 256, 128))

    def body(h_ref, gn_ref, wg_ref, wu_ref, g_ref, u_ref, act_ref, b_ref):
        x = h_ref[...]
        r = lax.rsqrt(jnp.mean(x * x, axis=-1, keepdims=True) + EPS)
        b = ((x * r) * gn_ref[...]).astype(BF16)
        b_ref[...] = b
        g = _dot(b, wg_ref[...])
        u = _dot(b, wu_ref[...])
        g_ref[...] = g.astype(g_ref.dtype)
        u_ref[...] = u.astype(u_ref.dtype)
        act_ref[...] = ((g * jax.nn.sigmoid(g)) * u).astype(act_ref.dtype)

    blk = pl.BlockSpec((tm, F), lambda m: (m, 0))
    wgt = pl.BlockSpec((K, F), lambda m: (0, 0))
    wide = jax.ShapeDtypeStruct((M, F), BF16)
    return pl.pallas_call(
        body, name=name, grid=(M // tm,),
        in_specs=[pl.BlockSpec((tm, K), lambda m: (m, 0)), pl.BlockSpec((1, K), lambda m: (0, 0)), wgt, wgt],
        out_specs=[blk, blk, blk, pl.BlockSpec((tm, K), lambda m: (m, 0))],
        out_shape=[wide, wide, wide, jax.ShapeDtypeStruct((M, K), BF16)],
        compiler_params=_params(("parallel",)))(h, gain, w_g, w_u)


def _ffn_dact(dy, w_d, g, u, name):
    M, K = dy.shape
    F = w_d.shape[0]
    tm = _pick(M, (768, 512, 256, 128))
    tn = _col_tile(F, 1408)
    nb = F // tn

    def body(dy_ref, wd_ref, g_ref, u_ref, dg_ref, du_ref):
        dact = _dot_nt(dy_ref[...].astype(BF16), wd_ref[...])
        gv = g_ref[...].astype(F32)
        s = jax.nn.sigmoid(gv)
        silu = gv * s
        dg_ref[...] = (dact * u_ref[...].astype(F32) * (s * (1.0 + gv * (1.0 - s)))).astype(dg_ref.dtype)
        du_ref[...] = (dact * silu).astype(du_ref.dtype)

    blk = pl.BlockSpec((tm, tn), lambda n, m: (m, n))
    return pl.pallas_call(
        body, name=name, grid=(nb, M // tm),
        in_specs=[pl.BlockSpec((tm, K), lambda n, m: (m, 0)), pl.BlockSpec((tn, K), lambda n, m: (n, 0)), blk, blk],
        out_specs=[blk, blk],
        out_shape=[jax.ShapeDtypeStruct((M, F), BF16), jax.ShapeDtypeStruct((M, F), BF16)],
        compiler_params=_params(("parallel", "parallel")))(dy, w_d, g, u)


def _loss_head(h, gain, target, name):
    M, C = h.shape
    tm = ROW0
    assert M % tm == 0 and target.shape[0] == M - ROW0

    def body(h_ref, g_ref, t_ref, sq_ref, dh_ref, dg_ref):
        i = pl.program_id(0)

        @pl.when(i == 0)
        def _():
            sq_ref[...] = jnp.zeros_like(sq_ref)
            dg_ref[...] = jnp.zeros_like(dg_ref)
            dh_ref[...] = jnp.zeros_like(dh_ref)

        @pl.when(i > 0)
        def _():
            x = h_ref[...]
            r = lax.rsqrt(jnp.mean(x * x, axis=-1, keepdims=True) + EPS)
            y = x * r
            err = y * g_ref[...] - t_ref[...]
            sq_ref[...] += jnp.sum(err * err, axis=0, keepdims=True)
            da = err * (1.0 / C)
            dy = da * g_ref[...]
            dh_ref[...] = r * (dy - y * jnp.mean(dy * y, axis=-1, keepdims=True))
            dg_ref[...] += jnp.sum(da * y, axis=0, keepdims=True)

    row = pl.BlockSpec((tm, C), lambda m: (m, 0))
    vec = pl.BlockSpec((1, C), lambda m: (0, 0))
    return pl.pallas_call(
        body, name=name, grid=(M // tm,),
        in_specs=[row, vec, pl.BlockSpec((tm, C), lambda m: (jnp.maximum(m - 1, 0), 0))],
        out_specs=[vec, row, vec],
        out_shape=[jax.ShapeDtypeStruct((1, C), F32), jax.ShapeDtypeStruct((M, C), F32),
                   jax.ShapeDtypeStruct((1, C), F32)],
        compiler_params=_params(("arbitrary",)))(h, gain, target)


def _band_dot(band, x):
    hi = x.astype(BF16)
    rest = x - hi.astype(F32)
    mid = rest.astype(BF16)
    lo = (rest - mid.astype(F32)).astype(BF16)
    return _dot(band, hi) + _dot(band, mid) + _dot(band, lo)


def _pool_pos(row0, tm):
    return row0 + lax.broadcasted_iota(jnp.int32, (tm, 1), 0) - PAD


def _pool_fwd(h, a, w, scale, name):
    M, C = a.shape
    tm = 256
    hb = tm // HALO

    def body(h_ref, a_ref, halo_ref, w_ref, s_ref, o_ref, p_ref):
        i = pl.program_id(0)
        row0 = i * tm
        ext = jnp.concatenate([halo_ref[...], a_ref[...]], axis=0)
        src = row0 - HALO + lax.broadcasted_iota(jnp.int32, (tm + HALO, 1), 0)
        ext = jnp.where(src >= PAD, ext, 0.0)
        r = lax.broadcasted_iota(jnp.int32, (tm, tm + HALO), 0)
        c = lax.broadcasted_iota(jnp.int32, (tm, tm + HALO), 1)
        pos = _pool_pos(row0, tm)
        for g, win in enumerate(POOL_WINDOWS):
            band = ((c <= r + HALO) & (c > r + HALO - win)).astype(BF16)
            cols = slice(g * POOL_GROUP, (g + 1) * POOL_GROUP)
            xg = ext[:, cols]
            tot = _band_dot(band, xg)
            cnt = jnp.clip(pos + 1, 1, win).astype(F32)
            pooled = (tot / cnt - xg[HALO:]).astype(BF16)
            p_ref[:, cols] = pooled
            mixed = _dot(pooled, w_ref[g])
            o_ref[:, cols] = h_ref[:, cols] + mixed * s_ref[:, cols]

    row = pl.BlockSpec((tm, C), lambda m: (m, 0))
    return pl.pallas_call(
        body, name=name, grid=(M // tm,),
        in_specs=[row, row, pl.BlockSpec((HALO, C), lambda m: (jnp.maximum(m * hb - 1, 0), 0)),
                  pl.BlockSpec((4, POOL_GROUP, POOL_GROUP), lambda m: (0, 0, 0)),
                  pl.BlockSpec((1, C), lambda m: (0, 0))],
        out_specs=[row, row],
        out_shape=[jax.ShapeDtypeStruct((M, C), F32), jax.ShapeDtypeStruct((M, C), BF16)],
        compiler_params=_params(("parallel",)))(h, a, a, w, scale)


def _pool_bwd_mix(dout, pooled, w, scale, name):
    M, C = dout.shape
    tm = 256

    def body(do_ref, p_ref, w_ref, s_ref, dpc_ref, dw_ref, ds_ref):
        i = pl.program_id(0)

        @pl.when(i == 0)
        def _():
            dw_ref[...] = jnp.zeros_like(dw_ref)
            ds_ref[...] = jnp.zeros_like(ds_ref)

        pos = _pool_pos(i * tm, tm)
        for g, win in enumerate(POOL_WINDOWS):
            cols = slice(g * POOL_GROUP, (g + 1) * POOL_GROUP)
            do = do_ref[:, cols]
            pooled = p_ref[:, cols]
            mixed = _dot(pooled, w_ref[g])
            ds_ref[:, cols] += jnp.sum(do * mixed, axis=0, keepdims=True)
            dmix = (do * s_ref[:, cols]).astype(BF16)
            dw_ref[g] += _dot_tn(pooled, dmix)
            dp = _dot_nt(dmix, w_ref[g])
            cnt = jnp.clip(pos + 1, 1, win).astype(F32)
            dpc_ref[:, cols] = dp / cnt

    row = pl.BlockSpec((tm, C), lambda m: (m, 0))
    wspec = pl.BlockSpec((4, POOL_GROUP, POOL_GROUP), lambda m: (0, 0, 0))
    vec = pl.BlockSpec((1, C), lambda m: (0, 0))
    return pl.pallas_call(
        body, name=name, grid=(M // tm,),
        in_specs=[row, row, wspec, vec], out_specs=[row, wspec, vec],
        out_shape=[jax.ShapeDtypeStruct((M, C), F32), jax.ShapeDtypeStruct((4, POOL_GROUP, POOL_GROUP), F32),
                   jax.ShapeDtypeStruct((1, C), F32)],
        compiler_params=_params(("arbitrary",)))(dout, pooled, w, scale)


def _pool_bwd_window(dpc, name):
    M, C = dpc.shape
    tm = 256
    hb = tm // HALO
    last = M // HALO - 1

    def body(d_ref, halo_ref, da_ref):
        i = pl.program_id(0)
        row0 = i * tm
        ext = jnp.concatenate([d_ref[...], halo_ref[...]], axis=0)
        src = row0 + lax.broadcasted_iota(jnp.int32, (tm + HALO, 1), 0)
        ext = jnp.where(src < M, ext, 0.0)
        r = lax.broadcasted_iota(jnp.int32, (tm, tm + HALO), 0)
        c = lax.broadcasted_iota(jnp.int32, (tm, tm + HALO), 1)
        pos = _pool_pos(row0, tm)
        for g, win in enumerate(POOL_WINDOWS):
            band = ((c >= r) & (c < r + win)).astype(BF16)
            cols = slice(g * POOL_GROUP, (g + 1) * POOL_GROUP)
            xg = ext[:, cols]
            tot = _band_dot(band, xg)
            cnt = jnp.clip(pos + 1, 1, win).astype(F32)
            da_ref[:, cols] = jnp.where(pos >= 0, tot - xg[:tm] * cnt, 0.0)

    row = pl.BlockSpec((tm, C), lambda m: (m, 0))
    return pl.pallas_call(
        body, name=name, grid=(M // tm,),
        in_specs=[row, pl.BlockSpec((HALO, C), lambda m: (jnp.minimum((m + 1) * hb, last), 0))],
        out_specs=row, out_shape=jax.ShapeDtypeStruct((M, C), F32),
        compiler_params=_params(("parallel",)))(dpc, dpc)


def _head_masks():
    lane = lax.broadcasted_iota(jnp.int32, (1, 128), 1)
    return lane < HEAD_DIM, lane


def _split_heads(x, first):
    z = jnp.zeros_like(x)
    return jnp.where(first, x, z), jnp.where(first, z, x)


def _split_rope(x, lane):
    z = jnp.zeros_like(x)
    return jnp.where(lane < MLA_ROPE, x, z), jnp.where((lane >= MLA_ROPE) & (lane < 2 * MLA_ROPE), x, z)


def _walk_causal(i, step):
    def mid(kb, carry):
        step(kb, False)
        return carry

    step(0, True)
    lax.fori_loop(1, i, mid, 0)

    @pl.when(i > 0)
    def _():
        step(i, True)


def _mla_fwd(q_all, kv_all, qr, kr, scale, name):
    M = q_all.shape[0]
    t = ATTN_TILE

    def body(q_ref, k_ref, v_ref, qr_ref, kr_ref, o_ref, lse_ref, m_s, l_s, acc_s, kmax_s):
        i = pl.program_id(1)
        first, lane = _head_masks()

        @pl.when(i == 0)
        def _():
            def block_max(kb, carry):
                rows = pl.ds(pl.multiple_of(kb * t, t), t)
                kk = k_ref[rows, :].astype(F32)
                kk = kk * kk
                rr = kr_ref[rows, :].astype(F32)
                rr = jnp.sum(jnp.where(lane < MLA_ROPE, rr * rr, 0.0), axis=1, keepdims=True)
                a = jnp.max(jnp.sum(jnp.where(first, kk, 0.0), axis=1, keepdims=True) + rr)
                b = jnp.max(jnp.sum(jnp.where(first, 0.0, kk), axis=1, keepdims=True) + rr)
                return jnp.maximum(carry[0], a), jnp.maximum(carry[1], b)

            a, b = lax.fori_loop(0, M // t, block_max, (jnp.float32(0.0), jnp.float32(0.0)))
            kmax_s[0] = a
            kmax_s[1] = b

        qs = _split_heads(q_ref[...], first)
        qrs = _split_rope(qr_ref[...], lane)
        qcat = tuple(jnp.concatenate([qs[hh], qrs[hh]], axis=1) for hh in range(2))
        qpos = i * t + lax.broadcasted_iota(jnp.int32, (t, t), 0)
        kidx = lax.broadcasted_iota(jnp.int32, (t, t), 1)
        c2 = scale * LOG2E

        def run(online):
            l_s[...] = jnp.zeros_like(l_s)
            acc_s[...] = jnp.zeros_like(acc_s)
            if online:
                m_s[...] = jnp.full_like(m_s, NEG)

            def step(kb, masked):
                k0 = pl.multiple_of(kb * t, t)
                kcat = jnp.concatenate([k_ref[pl.ds(k0, t), :], kr_ref[pl.ds(k0, t), :]], axis=1)
                vs = _split_heads(v_ref[pl.ds(k0, t), :], first)
                if masked:
                    kpos = k0 + kidx
                    valid = (kpos <= qpos) & (kpos >= PAD)
                pv = None
                alphas = []
                for hh in range(2):
                    s = _dot_nt(qcat[hh], kcat)
                    if online:
                        if masked:
                            s = jnp.where(valid, s, NEG)
                        m_old = m_s[hh]
                        m_new = jnp.maximum(m_old, jnp.max(s, axis=1, keepdims=True))
                        p = jnp.exp2((s - m_new) * c2)
                        alpha = jnp.exp2((m_old - m_new) * c2)
                        l_s[hh] = alpha * l_s[hh] + jnp.sum(p, axis=1, keepdims=True)
                        m_s[hh] = m_new
                        alphas.append(alpha)
                    else:
                        p = jnp.exp2(s * c2 - m_s[hh])
                        if masked:
                            p = jnp.where(valid, p, 0.0)
                        l_s[hh] = l_s[hh] + jnp.sum(p, axis=1, keepdims=True)
                    d = _dot(p.astype(BF16), vs[hh])
                    pv = d if pv is None else pv + d
                if online:
                    acc_s[...] = acc_s[...] * jnp.where(first, alphas[0], alphas[1]) + pv
                else:
                    acc_s[...] += pv

            _walk_causal(i, step)

        for hh in range(2):
            qf = qcat[hh].astype(F32)
            m_s[hh] = (1.001 * c2) * jnp.sqrt(jnp.sum(qf * qf, axis=1, keepdims=True) * kmax_s[hh])
        run(False)
        real = i * t + lax.broadcasted_iota(jnp.int32, (t, 1), 0) >= PAD
        underflow = jnp.max(jnp.where(real & (jnp.minimum(l_s[0], l_s[1]) < 1e-30), 1.0, 0.0)) > 0.0

        @pl.when(underflow)
        def _():
            run(True)
            m_s[...] = m_s[...] * c2

        ls = tuple(jnp.where(l_s[hh] > 0.0, l_s[hh], 1.0) for hh in range(2))
        o_ref[...] = (acc_s[...] * jnp.where(first, 1.0 / ls[0], 1.0 / ls[1])).astype(o_ref.dtype)
        lse_ref[:, 0:1] = m_s[0] * (1.0 / LOG2E) + jnp.log(ls[0])
        lse_ref[:, 1:2] = m_s[1] * (1.0 / LOG2E) + jnp.log(ls[1])

    blk = pl.BlockSpec((t, 128), lambda j, i: (i, j))
    return pl.pallas_call(
        body, name=name, grid=(N_PAIRS, M // t),
        in_specs=[blk, pl.BlockSpec((M, 128), lambda j, i: (0, j)), pl.BlockSpec((M, 128), lambda j, i: (0, N_PAIRS + j)),
                  blk, pl.BlockSpec((M, 128), lambda j, i: (0, 0))],
        out_specs=[blk, pl.BlockSpec((None, t, 2), lambda j, i: (j, i, 0))],
        out_shape=[jax.ShapeDtypeStruct((M, N_PAIRS * 128), BF16), jax.ShapeDtypeStruct((N_PAIRS, M, 2), F32)],
        scratch_shapes=[pltpu.VMEM((2, t, 1), F32), pltpu.VMEM((2, t, 1), F32), pltpu.VMEM((t, 128), F32),
                        pltpu.SMEM((2,), F32)],
        compiler_params=_params(("arbitrary", "arbitrary")))(q_all, kv_all, kv_all, qr, kr)


def _mla_bwd(q_all, kv_all, qr, kr, o, do, lse, scale, name):
    M = q_all.shape[0]
    t = ATTN_BWD_TILE

    def body(q_ref, kv_hbm, qr_ref, kr_hbm, o_ref, do_ref, lse_ref,
             dq_ref, dk_hbm, dv_hbm, dqr_ref, dkr_hbm,
             k_ref, v_ref, kr_ref, dk_ref, dv_ref, dkr_ref, dq_s, lse_s, delta_s):
        j = pl.program_id(0)
        i = pl.program_id(1)
        first, lane = _head_masks()
        every = pl.ds(0, M)
        kcols = pl.ds(pl.multiple_of(j * 128, 128), 128)
        vcols = pl.ds(pl.multiple_of((N_PAIRS + j) * 128, 128), 128)

        @pl.when(i == 0)
        def _():
            pltpu.sync_copy(kv_hbm.at[every, kcols], k_ref)
            pltpu.sync_copy(kv_hbm.at[every, vcols], v_ref)
            pltpu.sync_copy(kr_hbm, kr_ref)
            dk_ref[...] = jnp.zeros_like(dk_ref)
            dv_ref[...] = jnp.zeros_like(dv_ref)
            dkr_ref[...] = jnp.zeros_like(dkr_ref)

        qs = _split_heads(q_ref[...], first)
        qrs = _split_rope(qr_ref[...], lane)
        qcat = tuple(jnp.concatenate([qs[hh], qrs[hh]], axis=1) for hh in range(2))
        dov = do_ref[...]
        dos = _split_heads(dov, first)
        prod = dov.astype(F32) * o_ref[...].astype(F32)
        deltas = (jnp.sum(jnp.where(first, prod, 0.0), axis=1, keepdims=True),
                  jnp.sum(jnp.where(first, 0.0, prod), axis=1, keepdims=True))
        for hh in range(2):
            lse_s[hh] = jnp.broadcast_to(lse_ref[:, hh:hh + 1], (t, t))
            delta_s[hh] = jnp.broadcast_to(deltas[hh], (t, t))
        dq_s[...] = jnp.zeros_like(dq_s)
        qpos = i * t + lax.broadcasted_iota(jnp.int32, (t, t), 0)
        kidx = lax.broadcasted_iota(jnp.int32, (t, t), 1)

        def step(kb, masked):
            k0 = pl.multiple_of(kb * t, t)
            rows = pl.ds(k0, t)
            k = k_ref[rows, :]
            v = v_ref[rows, :]
            kr = kr_ref[rows, :]
            kcat = jnp.concatenate([k, kr], axis=1)
            ks = _split_heads(k, first)
            krs = _split_rope(kr, lane)
            if masked:
                kpos = k0 + kidx
                valid = (kpos <= qpos) & (kpos >= PAD)
            dq = dk = dv = None
            for hh in range(2):
                s = _dot_nt(qcat[hh], kcat) * scale
                if masked:
                    s = jnp.where(valid, s, NEG)
                p = jnp.exp(s - lse_s[hh])
                ds = p * (_dot_nt(dos[hh], v) - delta_s[hh])
                dsb = (ds * scale).astype(BF16)
                a = _dot(dsb, jnp.concatenate([ks[hh], krs[hh]], axis=1))
                b = _dot_tn(dsb, qcat[hh])
                c = _dot_tn(p.astype(BF16), dos[hh])
                dq = a if dq is None else dq + a
                dk = b if dk is None else dk + b
                dv = c if dv is None else dv + c
            dq_s[...] += dq
            dk_ref[rows, :] += dk[:, :128]
            dkr_ref[rows, :] += dk[:, 128:]
            dv_ref[rows, :] += dv

        _walk_causal(i, step)
        dq_ref[...] = dq_s[:, :128].astype(dq_ref.dtype)
        dqr_ref[...] = dq_s[:, 128:].astype(dqr_ref.dtype)

        @pl.when(i == M // t - 1)
        def _():
            pltpu.sync_copy(dk_ref, dk_hbm.at[every, kcols])
            pltpu.sync_copy(dv_ref, dv_hbm.at[every, kcols])
            pltpu.sync_copy(dkr_ref, dkr_hbm.at[j])

    blk = pl.BlockSpec((t, 128), lambda j, i: (i, j))
    whole = pl.BlockSpec(memory_space=pl.ANY)
    wide = jax.ShapeDtypeStruct((M, N_PAIRS * 128), F32)
    slab = lambda dtype: pltpu.VMEM((M, 128), dtype)
    return pl.pallas_call(
        body, name=name, grid=(N_PAIRS, M // t),
        in_specs=[blk, whole, blk, whole, blk, blk, pl.BlockSpec((None, t, 2), lambda j, i: (j, i, 0))],
        out_specs=[blk, whole, whole, blk, whole],
        out_shape=[jax.ShapeDtypeStruct((M, N_PAIRS * 128), BF16), wide, wide,
                   jax.ShapeDtypeStruct((M, N_PAIRS * 128), BF16), jax.ShapeDtypeStruct((N_PAIRS, M, 128), F32)],
        scratch_shapes=[slab(BF16), slab(BF16), slab(BF16), slab(F32), slab(F32), slab(F32),
                        pltpu.VMEM((t, 256), F32), pltpu.VMEM((2, t, t), F32), pltpu.VMEM((2, t, t), F32)],
        compiler_params=_params(("arbitrary", "arbitrary")))(q_all, kv_all, qr, kr, o, do, lse)


def _tri(t, rel):
    j = lax.broadcasted_iota(jnp.int32, (t, t), 0)
    k = lax.broadcasted_iota(jnp.int32, (t, t), 1)
    m = {"gt": j > k, "le": j <= k, "lt": j < k}[rel]
    return m.astype(BF16)


def _lane_cumsum(x, tri):
    hi = x.astype(BF16)
    lo = (x - hi.astype(F32)).astype(BF16)
    return _dot(hi, tri) + _dot(lo, tri)


def _log_sigmoids(z):
    sp = jnp.log(1.0 + jnp.exp(-jnp.abs(z)))
    return jnp.minimum(z, 0.0) - sp, jnp.minimum(-z, 0.0) - sp


def _log_sigmoids_fast(z):
    lk = -(jnp.maximum(z, 0.0) + jnp.log(1.0 + jnp.exp(-jnp.abs(z))))
    return lk + z, lk


def _sb_fwd(qkv, scale, name):
    M = qkv.shape[0]
    t = WALK_TILE
    ck, cv = N_PAIRS, 2 * N_PAIRS

    def body(q_ref, k_ref, v_ref, o_ref, tot_ref, c_s, acc_s):
        i = pl.program_id(1)
        first, _ = _head_masks()
        qs = _split_heads(q_ref[...], first)
        c_s[...] = jnp.zeros_like(c_s)
        acc_s[...] = jnp.zeros_like(acc_s)
        tri = _tri(t, "gt")
        qpos = i * t + lax.broadcasted_iota(jnp.int32, (t, t), 0)
        kidx = lax.broadcasted_iota(jnp.int32, (t, t), 1)

        def step(kb, masked):
            k0 = pl.multiple_of(kb * t, t)
            k = k_ref[pl.ds(k0, t), :]
            vs = _split_heads(v_ref[pl.ds(k0, t), :], first)
            if masked:
                kpos = k0 + kidx
                valid = (kpos < qpos) & (kpos >= PAD)
            pv = None
            for hh in range(2):
                z = _dot_nt(qs[hh], k) * scale
                lb, lk = _log_sigmoids_fast(z)
                if masked:
                    lk = jnp.where(valid, lk, 0.0)
                a = jnp.exp(lb + (c_s[hh] + _lane_cumsum(lk, tri)))
                if masked:
                    a = jnp.where(valid, a, 0.0)
                c_s[hh] = c_s[hh] + jnp.sum(lk, axis=1, keepdims=True)
                d = _dot(a.astype(BF16), vs[hh])
                pv = d if pv is None else pv + d
            acc_s[...] += pv

        def keep_going():
            return jnp.max(jnp.maximum(c_s[0], c_s[1])) > EXP_ZERO

        def cond(carry):
            kb, go, _ = carry
            return (kb >= 1) & go

        def walk(carry):
            kb, _, n = carry
            step(kb, False)
            return kb - 1, keep_going(), n + 1

        step(i, True)
        _, go, n = lax.while_loop(cond, walk, (i - 1, keep_going(), jnp.int32(1)))
        first_too = go & (i > 0)

        @pl.when(first_too)
        def _():
            step(0, True)

        walked = n + first_too.astype(jnp.int32)
        o_ref[...] = acc_s[...].astype(o_ref.dtype)
        tot_ref[:, 0:1] = c_s[0]
        tot_ref[:, 1:2] = c_s[1]
        tot_ref[:, 2:3] = jnp.full((t, 1), walked.astype(F32))

    whole = lambda c0: pl.BlockSpec((M, 128), lambda j, i: (0, c0 + j))
    return pl.pallas_call(
        body, name=name, grid=(N_PAIRS, M // t),
        in_specs=[pl.BlockSpec((t, 128), lambda j, i: (i, j)), whole(ck), whole(cv)],
        out_specs=[pl.BlockSpec((t, 128), lambda j, i: (i, j)), pl.BlockSpec((None, t, 3), lambda j, i: (j, i, 0))],
        out_shape=[jax.ShapeDtypeStruct((M, N_PAIRS * 128), BF16), jax.ShapeDtypeStruct((N_PAIRS, M, 3), F32)],
        scratch_shapes=[pltpu.VMEM((2, t, 1), F32), pltpu.VMEM((t, 128), F32)],
        compiler_params=_params(("parallel", "arbitrary")))(qkv, qkv, qkv)


def _sb_bwd(qkv, do, tot, scale, name):
    M = qkv.shape[0]
    t = WALK_TILE
    ck, cv = N_PAIRS, 2 * N_PAIRS

    def body(q_ref, k_ref, v_ref, do_ref, tot_ref, dq_ref, dk_ref, dv_ref, pc_s, dc_s, dq_s):
        i = pl.program_id(1)
        first, _ = _head_masks()

        @pl.when(i == 0)
        def _():
            dk_ref[...] = jnp.zeros_like(dk_ref)
            dv_ref[...] = jnp.zeros_like(dv_ref)

        qs = _split_heads(q_ref[...], first)
        dos = _split_heads(do_ref[...], first)
        pc_s[...] = jnp.zeros_like(pc_s)
        dc_s[...] = jnp.zeros_like(dc_s)
        dq_s[...] = jnp.zeros_like(dq_s)
        tri_le = _tri(t, "le")
        tri_lt = _tri(t, "lt")
        qpos = i * t + lax.broadcasted_iota(jnp.int32, (t, t), 0)
        kidx = lax.broadcasted_iota(jnp.int32, (t, t), 1)

        def step(kb, masked):
            k0 = pl.multiple_of(kb * t, t)
            rows = pl.ds(k0, t)
            k = k_ref[rows, :]
            v = v_ref[rows, :]
            ks = _split_heads(k, first)
            if masked:
                kpos = k0 + kidx
                valid = (kpos < qpos) & (kpos >= PAD)
            dq = dk = dv = None
            for hh in range(2):
                z = _dot_nt(qs[hh], k) * scale
                lb, lk = _log_sigmoids_fast(z)
                if masked:
                    lk = jnp.where(valid, lk, 0.0)
                later = tot_ref[:, hh:hh + 1] - (pc_s[hh] + _lane_cumsum(lk, tri_le))
                a = jnp.exp(lb + later)
                if masked:
                    a = jnp.where(valid, a, 0.0)
                dl = a * _dot_nt(dos[hh], v)
                early = dc_s[hh] + _lane_cumsum(dl, tri_lt)
                sg = jnp.exp(lb)
                dz = (dl * (1.0 - sg) - early * sg) * scale
                if masked:
                    dz = jnp.where(valid, dz, 0.0)
                pc_s[hh] = pc_s[hh] + jnp.sum(lk, axis=1, keepdims=True)
                dc_s[hh] = dc_s[hh] + jnp.sum(dl, axis=1, keepdims=True)
                dzb = dz.astype(BF16)
                x = _dot(dzb, ks[hh])
                y = _dot_tn(dzb, qs[hh])
                w = _dot_tn(a.astype(BF16), dos[hh])
                dq = x if dq is None else dq + x
                dk = y if dk is None else dk + y
                dv = w if dv is None else dv + w
            dq_s[...] += dq
            dk_ref[rows, :] += dk
            dv_ref[rows, :] += dv

        first_walked = i + 1 - jnp.max(tot_ref[:, 2:3]).astype(jnp.int32)

        def mid(kb, carry):
            step(kb, False)
            return carry

        @pl.when((first_walked == 0) & (i > 0))
        def _():
            step(0, True)

        lax.fori_loop(jnp.maximum(first_walked, 1), i, mid, 0)
        step(i, True)
        dq_ref[...] = dq_s[...].astype(dq_ref.dtype)

    whole = lambda c0: pl.BlockSpec((M, 128), lambda j, i: (0, c0 + j))
    blk = pl.BlockSpec((t, 128), lambda j, i: (i, j))
    col = pl.BlockSpec((M, 128), lambda j, i: (0, j))
    return pl.pallas_call(
        body, name=name, grid=(N_PAIRS, M // t),
        in_specs=[blk, whole(ck), whole(cv), blk, pl.BlockSpec((None, t, 3), lambda j, i: (j, i, 0))],
        out_specs=[blk, col, col],
        out_shape=[jax.ShapeDtypeStruct((M, N_PAIRS * 128), BF16), jax.ShapeDtypeStruct((M, N_PAIRS * 128), F32),
                   jax.ShapeDtypeStruct((M, N_PAIRS * 128), F32)],
        scratch_shapes=[pltpu.VMEM((2, t, 1), F32), pltpu.VMEM((2, t, 1), F32), pltpu.VMEM((t, 128), F32)],
        compiler_params=_params(("parallel", "arbitrary")))(qkv, qkv, qkv, do, tot)


def _rows_between(lo, hi):
    r = lax.broadcasted_iota(jnp.int32, (128, 1), 0)
    return (r >= lo) & (r < hi)


def _lanes_between(lo, hi):
    c = lax.broadcasted_iota(jnp.int32, (1, 128), 1)
    return (c >= lo) & (c < hi)


def _keep(x, mask):
    return jnp.where(mask, x, jnp.zeros_like(x))


def _valid_mask(i, kb, t):
    kpos = kb * t + lax.broadcasted_iota(jnp.int32, (t, t), 0)
    qpos = i * t + lax.broadcasted_iota(jnp.int32, (t, t), 1)
    return (kpos <= qpos) & (kpos >= PAD)


def _fox_fwd(qkv, qkv_t, f_rows, f_cols, scale, name):
    M = qkv.shape[0]
    t = FOX_TILE
    first_blk = PAD // t
    ck, cv = N_PAIRS, 2 * N_PAIRS

    def body(qt_ref, k_ref, vt_ref, fq_ref, fk_ref, o_ref, lse_ref, ox_ref, m_s, l_s, acc_s, accx_s, kmax_s, walked_s):
        i = pl.program_id(1)

        @pl.when(i == 0)
        def _():
            first = _lanes_between(0, 64)

            def block_max(kb, carry):
                kk = k_ref[pl.ds(pl.multiple_of(kb * t, t), t), :].astype(F32)
                kk = kk * kk
                a = jnp.max(jnp.sum(jnp.where(first, kk, 0.0), axis=1, keepdims=True))
                b = jnp.max(jnp.sum(jnp.where(first, 0.0, kk), axis=1, keepdims=True))
                return jnp.maximum(carry[0], a), jnp.maximum(carry[1], b)

            a, b = lax.fori_loop(0, M // t, block_max, (jnp.float32(0.0), jnp.float32(0.0)))
            kmax_s[0] = a
            kmax_s[1] = b

        qt = qt_ref[...]
        qts = (_keep(qt, _rows_between(0, 64)), _keep(qt, _rows_between(64, 128)))
        qf = qt.astype(F32)
        qf = qf * qf
        qbound = tuple(
            (1.001 * scale) * jnp.sqrt(jnp.sum(qf[HEAD_DIM * hh:HEAD_DIM * (hh + 1)], axis=0, keepdims=True) * kmax_s[hh])
            for hh in range(2))
        def run(online):
            l_s[...] = jnp.zeros_like(l_s)
            acc_s[...] = jnp.zeros_like(acc_s)
            accx_s[...] = jnp.zeros_like(accx_s)
            if online:
                m_s[...] = jnp.full_like(m_s, NEG)

            def step(kb, masked):
                k0 = pl.multiple_of(kb * t, t)
                rows = pl.ds(k0, t)
                k = k_ref[rows, :]
                if masked:
                    valid = _valid_mask(i, kb, t)
                for hh in range(2):
                    s = _dot(k, qts[hh]) * scale + (fq_ref[hh:hh + 1, :] - fk_ref[rows, hh:hh + 1])
                    hr = slice(HEAD_DIM * hh, HEAD_DIM * (hh + 1))
                    vt = vt_ref[hr, rows]
                    if online:
                        if masked:
                            s = jnp.where(valid, s, NEG)
                        m_old = m_s[hh]
                        m_new = jnp.maximum(m_old, jnp.max(s, axis=0, keepdims=True))
                        p = jnp.exp(s - m_new)
                        alpha = jnp.exp(m_old - m_new)
                        l_s[hh] = alpha * l_s[hh] + jnp.sum(p, axis=0, keepdims=True)
                        m_s[hh] = m_new
                        pb = p.astype(BF16)
                        acc_s[hr, :] = acc_s[hr, :] * alpha + _dot(vt, pb)
                        accx_s[hr, :] = accx_s[hr, :] * alpha + _dot(vt, (p - pb.astype(F32)).astype(BF16))
                    else:
                        p = jnp.exp(s - m_s[hh])
                        if masked:
                            p = jnp.where(valid, p, 0.0)
                        l_s[hh] = l_s[hh] + jnp.sum(p, axis=0, keepdims=True)
                        pb = p.astype(BF16)
                        acc_s[hr, :] += _dot(vt, pb)
                        accx_s[hr, :] += _dot(vt, (p - pb.astype(F32)).astype(BF16))

            def keep_going(kb):
                k0 = pl.multiple_of(kb * t, t)
                worst = None
                for hh in range(2):
                    f0 = jnp.max(fk_ref[pl.ds(k0, 8), hh:hh + 1])
                    decay = fq_ref[hh:hh + 1, :] - f0
                    if online:
                        w = jnp.max(qbound[hh] + decay - m_s[hh])
                    else:
                        w = jnp.max(decay - jnp.minimum(jnp.log(jnp.maximum(l_s[hh], 1e-37)), 0.0))
                    worst = w if worst is None else jnp.maximum(worst, w)
                return worst > EXP_ZERO

            def cond(carry):
                kb, go, _ = carry
                return (kb > first_blk) & go

            def walk(carry):
                kb, _, n = carry
                step(kb, False)
                return kb - 1, keep_going(kb), n + 1

            step(i, True)
            _, go, n = lax.while_loop(cond, walk, (i - 1, keep_going(i), jnp.int32(1)))
            first_too = go & (i > first_blk)

            @pl.when(first_too)
            def _():
                step(first_blk, True)

            walked_s[0] = n + first_too.astype(jnp.int32)

        for hh in range(2):
            m_s[hh] = qbound[hh]
        run(False)
        real = i * t + lax.broadcasted_iota(jnp.int32, (1, t), 1) >= PAD
        underflow = jnp.max(jnp.where(real & (jnp.minimum(l_s[0], l_s[1]) < 1e-30), 1.0, 0.0)) > 0.0

        @pl.when(underflow)
        def _():
            run(True)

        for hh in range(2):
            hr = slice(HEAD_DIM * hh, HEAD_DIM * (hh + 1))
            l = jnp.where(l_s[hh] > 0.0, l_s[hh], 1.0)
            inv = 1.0 / l
            o_ref[hr, :] = (acc_s[hr, :] * inv).astype(o_ref.dtype)
            ox_ref[hr, :] = (acc_s[hr, :] + accx_s[hr, :]) * inv
            lse_ref[hh:hh + 1, :] = m_s[hh] + jnp.log(l)
        lse_ref[2:3, :] = jnp.full((1, t), walked_s[0].astype(F32))

    blk = pl.BlockSpec((128, t), lambda j, i: (j, i))
    stat = pl.BlockSpec((None, 2, t), lambda j, i: (j, 0, i))
    return pl.pallas_call(
        body, name=name, grid=(N_PAIRS, M // t),
        in_specs=[blk, pl.BlockSpec((M, 128), lambda j, i: (0, ck + j)), pl.BlockSpec((128, M), lambda j, i: (cv + j, 0)),
                  stat, pl.BlockSpec((None, M, 2), lambda j, i: (j, 0, 0))],
        out_specs=[blk, pl.BlockSpec((None, 3, t), lambda j, i: (j, 0, i)), blk],
        out_shape=[jax.ShapeDtypeStruct((N_PAIRS * 128, M), BF16), jax.ShapeDtypeStruct((N_PAIRS, 3, M), F32),
                   jax.ShapeDtypeStruct((N_PAIRS * 128, M), F32)],
        scratch_shapes=[pltpu.VMEM((2, 1, t), F32), pltpu.VMEM((2, 1, t), F32), pltpu.VMEM((128, t), F32),
                        pltpu.VMEM((128, t), F32), pltpu.SMEM((2,), F32), pltpu.SMEM((1,), jnp.int32)],
        compiler_params=_params(("arbitrary", "arbitrary")))(qkv_t, qkv, qkv_t, f_rows, f_cols)


def _fox_bwd(qkv, qkv_t, o_t, do, do_t, lse, f_rows, f_cols, scale, name):
    M = qkv.shape[0]
    t = FOX_TILE
    first_blk = PAD // t
    ck, cv = N_PAIRS, 2 * N_PAIRS

    def body(q_ref, qt_ref, k_ref, kt_ref, v_ref, ot_ref, do_ref, dot_ref, lse_ref, fq_ref, fk_ref,
             dq_ref, dk_ref, dv_ref, cs_ref, dq_s):
        i = pl.program_id(1)

        @pl.when(i == 0)
        def _():
            dk_ref[...] = jnp.zeros_like(dk_ref)
            dv_ref[...] = jnp.zeros_like(dv_ref)
            cs_ref[...] = jnp.zeros_like(cs_ref)

        heads_l = (_lanes_between(0, 64), _lanes_between(64, 128))
        heads_r = (_rows_between(0, 64), _rows_between(64, 128))
        q = q_ref[...]
        qt = qt_ref[...]
        do = do_ref[...]
        dot = dot_ref[...]
        qs = tuple(_keep(q, m) for m in heads_l)
        qts = tuple(_keep(qt, m) for m in heads_r)
        dos = tuple(_keep(do, m) for m in heads_l)
        dots = tuple(_keep(dot, m) for m in heads_r)
        prod = dot.astype(F32) * ot_ref[...]
        deltas = tuple(jnp.sum(prod[HEAD_DIM * hh:HEAD_DIM * (hh + 1)], axis=0, keepdims=True) for hh in range(2))
        ones = tuple(m.astype(BF16) * jnp.ones((t, 128), BF16) for m in heads_l)
        dq_s[...] = jnp.zeros_like(dq_s)

        def step(kb, masked):
            k0 = pl.multiple_of(kb * t, t)
            rows = pl.ds(k0, t)
            k = k_ref[rows, :]
            v = v_ref[rows, :]
            if masked:
                valid = _valid_mask(i, kb, t)
            dk = dv = cs = None
            for hh in range(2):
                s = _dot(k, qts[hh]) * scale + (fq_ref[hh:hh + 1, :] - fk_ref[rows, hh:hh + 1])
                if masked:
                    s = jnp.where(valid, s, NEG)
                p = jnp.exp(s - lse_ref[hh:hh + 1, :])
                ds = p * (_dot(v, dots[hh]) - deltas[hh])
                hi = ds.astype(BF16)
                lo = (ds - hi.astype(F32)).astype(BF16)
                c = _dot(hi, ones[hh]) + _dot(lo, ones[hh])
                dsb = (ds * scale).astype(BF16)
                hr = slice(HEAD_DIM * hh, HEAD_DIM * (hh + 1))
                dq_s[hr, :] += _dot(kt_ref[hr, rows], dsb)
                a = _dot(dsb, qs[hh])
                b = _dot(p.astype(BF16), dos[hh])
                dk = a if dk is None else dk + a
                dv = b if dv is None else dv + b
                cs = c if cs is None else cs + c
            dk_ref[rows, :] += dk
            dv_ref[rows, :] += dv
            cs_ref[rows, :] += cs

        first_walked = i + 1 - jnp.max(lse_ref[2:3, :]).astype(jnp.int32)

        def mid(kb, carry):
            step(kb, False)
            return carry

        @pl.when((first_walked == first_blk) & (i > first_blk))
        def _():
            step(first_blk, True)

        lax.fori_loop(jnp.maximum(first_walked, first_blk + 1), i, mid, 0)
        step(i, True)
        dq_ref[...] = dq_s[...].astype(dq_ref.dtype)

    rblk = pl.BlockSpec((t, 128), lambda j, i: (i, j))
    tblk = pl.BlockSpec((128, t), lambda j, i: (j, i))
    stat = pl.BlockSpec((None, 2, t), lambda j, i: (j, 0, i))
    stat3 = pl.BlockSpec((None, 3, t), lambda j, i: (j, 0, i))
    col = pl.BlockSpec((M, 128), lambda j, i: (0, j))
    wide = jax.ShapeDtypeStruct((M, N_PAIRS * 128), F32)
    return pl.pallas_call(
        body, name=name, grid=(N_PAIRS, M // t),
        in_specs=[rblk, tblk, pl.BlockSpec((M, 128), lambda j, i: (0, ck + j)),
                  pl.BlockSpec((128, M), lambda j, i: (ck + j, 0)), pl.BlockSpec((M, 128), lambda j, i: (0, cv + j)),
                  tblk, rblk, tblk, stat3, stat, pl.BlockSpec((None, M, 2), lambda j, i: (j, 0, 0))],
        out_specs=[tblk, col, col, pl.BlockSpec((None, M, 128), lambda j, i: (j, 0, 0))],
        out_shape=[jax.ShapeDtypeStruct((N_PAIRS * 128, M), BF16), wide, wide,
                   jax.ShapeDtypeStruct((N_PAIRS, M, 128), F32)],
        scratch_shapes=[pltpu.VMEM((128, t), F32)],
        compiler_params=_params(("parallel", "arbitrary")))(qkv, qkv_t, qkv, qkv_t, qkv, o_t, do, do_t, lse,
                                                            f_rows, f_cols)


def _rope_tables(M):
    pos = (jnp.arange(M, dtype=jnp.int32) - PAD).astype(F32)
    inv = ROPE_THETA ** (-jnp.arange(0, MLA_ROPE, 2, dtype=F32) / MLA_ROPE)
    ang = pos[:, None] * inv[None, :]
    cos, sin = jnp.cos(ang), jnp.sin(ang)
    z = jnp.zeros((M, 64), F32)
    cos_t = jnp.concatenate([cos, cos, cos, cos, z], axis=1)
    sin_t = jnp.concatenate([-sin, sin, -sin, sin, z], axis=1)
    return cos_t, sin_t


def _rope(x, cos_t, sin_t, out_dtype, name, inverse=False, lead=0):
    M, C = x.shape
    tm = _pick(M, (768, 512, 256, 128))
    nblk = (C - lead) // 128
    sign = -1.0 if inverse else 1.0

    def body(x_ref, c_ref, s_ref, o_ref):
        lane = lax.broadcasted_iota(jnp.int32, (1, 128), 1)
        low = (lane % MLA_ROPE) < (MLA_ROPE // 2)
        cos = c_ref[...]
        sin = s_ref[...] * sign
        if lead:
            o_ref[:, :lead] = x_ref[:, :lead].astype(o_ref.dtype)
        for b in range(nblk):
            cols = slice(lead + b * 128, lead + (b + 1) * 128)
            v = x_ref[:, cols].astype(F32)
            up = pltpu.roll(v, 128 - MLA_ROPE // 2, 1)
            down = pltpu.roll(v, MLA_ROPE // 2, 1)
            o_ref[:, cols] = (v * cos + jnp.where(low, up, down) * sin).astype(o_ref.dtype)

    row = pl.BlockSpec((tm, C), lambda m: (m, 0))
    tab = pl.BlockSpec((tm, 128), lambda m: (m, 0))
    return pl.pallas_call(
        body, name=name, grid=(M // tm,), in_specs=[row, tab, tab], out_specs=row,
        out_shape=jax.ShapeDtypeStruct((M, C), out_dtype),
        compiler_params=_params(("parallel",)))(x, cos_t, sin_t)


def _forget_cumsum(f_logit, bias, name):
    M = f_logit.shape[0]
    tm = 256

    def body(f_ref, b_ref, o_ref, c_s):
        i = pl.program_id(0)

        @pl.when(i == 0)
        def _():
            c_s[...] = jnp.zeros_like(c_s)
        ls, _ = _log_sigmoids(f_ref[...] + b_ref[...])
        rows = i * tm + lax.broadcasted_iota(jnp.int32, (tm, 1), 0)
        ls = jnp.where(rows >= PAD, ls, 0.0)
        r = lax.broadcasted_iota(jnp.int32, (tm, tm), 0)
        c = lax.broadcasted_iota(jnp.int32, (tm, tm), 1)
        tri = (c <= r).astype(F32)
        cum = jnp.dot(tri, ls, precision=lax.Precision.HIGHEST, preferred_element_type=F32) + c_s[...]
        o_ref[...] = cum
        c_s[...] = cum[tm - 1:tm, :]

    row = pl.BlockSpec((tm, 128), lambda m: (m, 0))
    return pl.pallas_call(
        body, name=name, grid=(M // tm,),
        in_specs=[row, pl.BlockSpec((1, 128), lambda m: (0, 0))], out_specs=row,
        out_shape=jax.ShapeDtypeStruct((M, 128), F32), scratch_shapes=[pltpu.VMEM((1, 128), F32)],
        compiler_params=_params(("arbitrary",)))(f_logit, bias)


def _forget_cumsum_bwd(f_logit, bias, colsum, name):
    M = f_logit.shape[0]
    tm = 256
    nb = M // tm

    def body(f_ref, b_ref, cs_ref, o_ref, db_ref, c_s):
        i = pl.program_id(0)

        @pl.when(i == 0)
        def _():
            c_s[...] = jnp.zeros_like(c_s)
            db_ref[...] = jnp.zeros_like(db_ref)
        rr = lax.broadcasted_iota(jnp.int32, (128, 128), 0)
        cc = lax.broadcasted_iota(jnp.int32, (128, 128), 1)
        dF = None
        for j in range(N_PAIRS):
            sel = (((rr == 0) & (cc == 2 * j)) | ((rr == HEAD_DIM) & (cc == 2 * j + 1))).astype(F32)
            d = jnp.dot(cs_ref[j], sel, precision=lax.Precision.HIGHEST, preferred_element_type=F32)
            dF = d if dF is None else dF + d
        r = lax.broadcasted_iota(jnp.int32, (tm, tm), 0)
        c = lax.broadcasted_iota(jnp.int32, (tm, tm), 1)
        tri = (c >= r).astype(F32)
        cum = c_s[...] - jnp.dot(tri, dF, precision=lax.Precision.HIGHEST, preferred_element_type=F32)
        c_s[...] = cum[0:1, :]
        _, lsn = _log_sigmoids(f_ref[...] + b_ref[...])
        rows = (nb - 1 - i) * tm + lax.broadcasted_iota(jnp.int32, (tm, 1), 0)
        dl = jnp.where(rows >= PAD, cum * jnp.exp(lsn), 0.0)
        o_ref[...] = dl
        db_ref[...] += jnp.sum(dl, axis=0, keepdims=True)

    row = pl.BlockSpec((tm, 128), lambda m: (nb - 1 - m, 0))
    vec = pl.BlockSpec((1, 128), lambda m: (0, 0))
    return pl.pallas_call(
        body, name=name, grid=(nb,),
        in_specs=[row, vec, pl.BlockSpec((N_PAIRS, tm, 128), lambda m: (0, nb - 1 - m, 0))], out_specs=[row, vec],
        out_shape=[jax.ShapeDtypeStruct((M, 128), F32), jax.ShapeDtypeStruct((1, 128), F32)],
        scratch_shapes=[pltpu.VMEM((1, 128), F32)],
        compiler_params=_params(("arbitrary",)))(f_logit, bias, colsum)


def _adamw(w, parts, m, v, name):
    R, C = w.shape
    n_parts = parts.shape[0]
    tr = R
    for d in range(8, R, 8):
        if R % d == 0 and d * C <= ADAM_TILE_ELEMS:
            tr = d
    c1 = 1.0 - ADAM_B1 ** ADAM_STEP
    c2 = 1.0 - ADAM_B2 ** ADAM_STEP

    def body(w_ref, s_ref, m_ref, v_ref, g_ref, d_ref, mo_ref, vo_ref):
        g = s_ref[0].astype(F32)
        for k in range(1, n_parts):
            g = g + s_ref[k].astype(F32)
        mn = ADAM_B1 * m_ref[...] + (1.0 - ADAM_B1) * g
        vn = ADAM_B2 * v_ref[...] + (1.0 - ADAM_B2) * (g * g)
        m_hat = mn / c1
        v_hat = vn / c2
        g_ref[...] = g
        d_ref[...] = -ADAM_LR * (m_hat / (jnp.sqrt(v_hat) + ADAM_EPS) + ADAM_WD * w_ref[...])
        mo_ref[...] = mn
        vo_ref[...] = vn

    row = pl.BlockSpec((tr, C), lambda r: (r, 0))
    shp = jax.ShapeDtypeStruct((R, C), F32)
    return pl.pallas_call(
        body, name=name, grid=(R // tr,),
        in_specs=[row, pl.BlockSpec((n_parts, tr, C), lambda r: (0, r, 0)), row, row],
        out_specs=[row, row, row, row], out_shape=[shp, shp, shp, shp],
        compiler_params=_params(("parallel",)))(w, parts, m, v)


def _position():
    return lax.axis_index("x"), lax.axis_index("y"), lax.axis_index("c")


def _all_gather(blocks, name):
    n = len(blocks)

    def body(*refs):
        x_refs, out_refs = refs[:n], refs[n:2 * n]
        send_sems, recv_sems, local_sems = refs[2 * n:]
        x, y, c = _position()
        me, sibling = (x, y, c), (x, y, 1 - c)
        chips = [(1 - x, y), (x, 1 - y), (1 - x, 1 - y)]

        def copies(k, block, to, own=False):
            slot = 4 * block[0] + 2 * block[1] + block[2]
            return [pltpu.make_async_remote_copy(
                src_ref=x_refs[p] if own else out_refs[p].at[slot], dst_ref=out_refs[p].at[slot],
                send_sem=send_sems.at[k, p], recv_sem=recv_sems.at[k, p], device_id=to, device_id_type=MESH)
                for p in range(n)]

        mine = [pltpu.make_async_copy(x_refs[p], out_refs[p].at[4 * x + 2 * y + c], local_sems.at[p]) for p in range(n)]
        for cp in mine:
            cp.start()
        first = copies(0, me, sibling, own=True)
        for j, chip in enumerate(chips):
            first += copies(1 + j, me, (*chip, c), own=True)
        for cp in first:
            cp.start()
        passed = []
        for j, chip in enumerate(chips):
            for cp in copies(1 + j, (*chip, c), me):
                cp.wait_recv()
            onward = copies(4 + j, (*chip, c), sibling)
            for cp in onward:
                cp.start()
            passed += onward
        for cp in copies(0, sibling, me):
            cp.wait_recv()
        for j, chip in enumerate(chips):
            for cp in copies(4 + j, (*chip, 1 - c), me):
                cp.wait_recv()
        for cp in first + passed:
            cp.wait_send()
        for cp in mine:
            cp.wait()

    any_spec = pl.BlockSpec(memory_space=pl.ANY)
    return pl.pallas_call(
        body, name=name, out_shape=[jax.ShapeDtypeStruct((N_DEV,) + b.shape, b.dtype) for b in blocks],
        in_specs=[any_spec] * n, out_specs=[any_spec] * n,
        scratch_shapes=[pltpu.SemaphoreType.DMA((7, n)), pltpu.SemaphoreType.DMA((7, n)), pltpu.SemaphoreType.DMA((n,))],
    )(*blocks)


N_CHIPS = 4


def _exchange_siblings(parts, name):
    n = len(parts)

    def body(*refs):
        g_refs, land_refs = refs[:n], refs[n:2 * n]
        send_sems, recv_sems = refs[2 * n:]
        x, y, c = _position()
        sibling = (x, y, 1 - c)
        sends, recvs = [], []
        for q in range(N_CHIPS):
            for p in range(n):
                sends.append(pltpu.make_async_remote_copy(
                    src_ref=g_refs[p].at[2 * q + (1 - c)], dst_ref=land_refs[p].at[q], send_sem=send_sems.at[q, p],
                    recv_sem=recv_sems.at[q, p], device_id=sibling, device_id_type=MESH))
                recvs.append(pltpu.make_async_remote_copy(
                    src_ref=g_refs[p].at[2 * q + c], dst_ref=land_refs[p].at[q], send_sem=send_sems.at[q, p],
                    recv_sem=recv_sems.at[q, p], device_id=sibling, device_id_type=MESH))
        for cp in sends:
            cp.start()
        for cp in recvs:
            cp.wait_recv()
        for cp in sends:
            cp.wait_send()

    any_spec = pl.BlockSpec(memory_space=pl.ANY)
    return pl.pallas_call(
        body, name=name, out_shape=[jax.ShapeDtypeStruct((N_CHIPS,) + p.shape[1:], p.dtype) for p in parts],
        in_specs=[any_spec] * n, out_specs=[any_spec] * n,
        scratch_shapes=[pltpu.SemaphoreType.DMA((N_CHIPS, n)), pltpu.SemaphoreType.DMA((N_CHIPS, n))],
    )(*parts)


def _pair_sum(part, from_sibling, name):
    _, R, C = part.shape
    tr = R
    for d in range(8, R, 8):
        if R % d == 0 and d * C <= ADAM_TILE_ELEMS:
            tr = d

    def body(a_ref, b_ref, o_ref):
        c = lax.axis_index("c")
        for q in range(N_CHIPS):
            o_ref[q] = (a_ref[2 * q + c].astype(F32) + b_ref[q].astype(F32)).astype(o_ref.dtype)

    return pl.pallas_call(
        body, name=name, grid=(R // tr,),
        in_specs=[pl.BlockSpec((N_DEV, tr, C), lambda r: (0, r, 0)), pl.BlockSpec((N_CHIPS, tr, C), lambda r: (0, r, 0))],
        out_specs=pl.BlockSpec((N_CHIPS, tr, C), lambda r: (0, r, 0)),
        out_shape=jax.ShapeDtypeStruct((N_CHIPS, R, C), part.dtype),
        compiler_params=_params(("parallel",)))(part, from_sibling)


def _exchange_chips(sums, name):
    n = len(sums)

    def body(*refs):
        g_refs, land_refs = refs[:n], refs[n:2 * n]
        send_sems, recv_sems, local_sems = refs[2 * n:]
        x, y, c = _position()
        me = 2 * x + y
        mine = [pltpu.make_async_copy(g_refs[p].at[me], land_refs[p].at[me], local_sems.at[p]) for p in range(n)]
        for cp in mine:
            cp.start()
        sends, recvs = [], []
        for k in range(1, N_CHIPS):
            px = 1 - x if k & 2 else x
            py = 1 - y if k & 1 else y
            peer = 2 * px + py
            for p in range(n):
                sends.append(pltpu.make_async_remote_copy(
                    src_ref=g_refs[p].at[peer], dst_ref=land_refs[p].at[me], send_sem=send_sems.at[k - 1, p],
                    recv_sem=recv_sems.at[k - 1, p], device_id=(px, py, c), device_id_type=MESH))
                recvs.append(pltpu.make_async_remote_copy(
                    src_ref=g_refs[p].at[me], dst_ref=land_refs[p].at[peer], send_sem=send_sems.at[k - 1, p],
                    recv_sem=recv_sems.at[k - 1, p], device_id=(px, py, c), device_id_type=MESH))
        for cp in sends:
            cp.start()
        for cp in recvs:
            cp.wait_recv()
        for cp in sends:
            cp.wait_send()
        for cp in mine:
            cp.wait()

    any_spec = pl.BlockSpec(memory_space=pl.ANY)
    return pl.pallas_call(
        body, name=name, out_shape=[jax.ShapeDtypeStruct(p.shape, p.dtype) for p in sums],
        in_specs=[any_spec] * n, out_specs=[any_spec] * n,
        scratch_shapes=[pltpu.SemaphoreType.DMA((3, n)), pltpu.SemaphoreType.DMA((3, n)), pltpu.SemaphoreType.DMA((n,))],
    )(*sums)


SHARDED = (("sb_w_qkv", 2), ("sb_w_o", 1), ("mla_w_down", 1), ("mla_w_uq", 2), ("mla_w_ukv", 2), ("mla_w_o", 1),
           ("fox_w_qkvf", 2), ("fox_w_o", 1), ("ffn_w_gate", 2), ("ffn_w_up", 2), ("ffn_w_down", 1),
           ("pool_w", 2), ("meta", 1), ("mla_q_norm", 1), ("mla_kv_norm", 1))
KEPT_F32 = ("meta", "mla_q_norm", "mla_kv_norm")
REPLICATED = ("norm_mix", "norm_ffn", "pool_scale", "fox_b_f", "final_norm")
WEIGHT_NAMES = ("meta", "norm_mix", "norm_ffn", "pool_w", "pool_scale", "sb_w_qkv", "sb_w_o", "mla_w_down",
                "mla_q_norm", "mla_kv_norm", "mla_w_uq", "mla_w_ukv", "mla_w_o", "fox_w_qkvf", "fox_b_f",
                "fox_w_o", "ffn_w_gate", "ffn_w_up", "ffn_w_down", "final_norm")
LANES = 1024


def _pack_rows(arrays, names):
    parts = []
    for n in names:
        flat = arrays[n].reshape(-1).astype(F32)
        rows = -(-flat.shape[0] // LANES)
        parts.append(jnp.pad(flat, (0, rows * LANES - flat.shape[0])).reshape(rows, LANES))
    rows = sum(p.shape[0] for p in parts)
    parts.append(jnp.zeros((-(-rows // 8) * 8 - rows, LANES), F32))
    return jnp.concatenate(parts, axis=0)


def _unpack_rows(buf, shapes, names):
    out, row = {}, 0
    for n in names:
        size = int(np.prod(shapes[n]))
        rows = -(-size // LANES)
        out[n] = buf[row:row + rows].reshape(-1)[:size].reshape(shapes[n])
        row += rows
    return out


def _whole_from_gathered(g, axis):
    g = jnp.moveaxis(g, 0, axis)
    shp = g.shape
    return g.reshape(shp[:axis] + (shp[axis] * shp[axis + 1],) + shp[axis + 2:])


def _parts_from_whole(whole, axis):
    shp = whole.shape
    g = whole.reshape(shp[:axis] + (N_DEV, shp[axis] // N_DEV) + shp[axis + 1:])
    return jnp.moveaxis(g, axis, 0)


def _kernel_weights(full):
    W = {}
    W["pool_w"] = full["pool_w"][0]
    W["sb_w_qkv"] = full["sb_w_qkv"][0]
    W["sb_w_o"] = full["sb_w_o"][0]
    W["mla_w_down"] = full["mla_w_down"][0]
    uq = full["mla_w_uq"][0].reshape(MLA_Q_RANK, N_HEADS, MLA_NOPE + MLA_ROPE)
    nope = uq[:, :, :MLA_NOPE].reshape(MLA_Q_RANK, N_HEADS * MLA_NOPE)
    rope = uq[:, :, MLA_NOPE:].reshape(MLA_Q_RANK, N_PAIRS, 2 * MLA_ROPE)
    rope = jnp.pad(rope, ((0, 0), (0, 0), (0, 128 - 2 * MLA_ROPE))).reshape(MLA_Q_RANK, N_PAIRS * 128)
    W["mla_w_uq"] = jnp.concatenate([nope, rope], axis=1)
    ukv = full["mla_w_ukv"][0].reshape(MLA_KV_RANK, N_HEADS, 2, HEAD_DIM)
    W["mla_w_ukv"] = jnp.transpose(ukv, (0, 2, 1, 3)).reshape(MLA_KV_RANK, 2 * N_HEADS * HEAD_DIM)
    W["mla_w_o"] = full["mla_w_o"][0]
    qkvf = full["fox_w_qkvf"][0]
    n_qkv = 3 * N_HEADS * HEAD_DIM
    W["fox_w_qkv"] = qkvf[:, :n_qkv]
    W["fox_w_f"] = jnp.pad(qkvf[:, n_qkv:], ((0, 0), (0, 128 - N_HEADS)))
    W["fox_w_qkvf"] = jnp.concatenate([W["fox_w_qkv"], W["fox_w_f"]], axis=1)
    W["fox_w_o"] = full["fox_w_o"][0]
    W["ffn_w_gate"] = full["ffn_w_gate"]
    W["ffn_w_up"] = full["ffn_w_up"]
    W["ffn_w_down"] = full["ffn_w_down"]
    return W


def _reference_grads(G):
    out = {}
    out["pool_w"] = G["pool_w"][None]
    for n in ("sb_w_qkv", "sb_w_o", "mla_w_down", "mla_w_o", "fox_w_o"):
        out[n] = G[n][None]
    duq = G["mla_w_uq"]
    nope = duq[:, :N_HEADS * MLA_NOPE].reshape(MLA_Q_RANK, N_HEADS, MLA_NOPE)
    rope = duq[:, N_HEADS * MLA_NOPE:].reshape(MLA_Q_RANK, N_PAIRS, 128)[:, :, :2 * MLA_ROPE]
    rope = rope.reshape(MLA_Q_RANK, N_HEADS, MLA_ROPE)
    out["mla_w_uq"] = jnp.concatenate([nope, rope], axis=2).reshape(1, MLA_Q_RANK, -1)
    dukv = G["mla_w_ukv"].reshape(MLA_KV_RANK, 2, N_HEADS, HEAD_DIM)
    out["mla_w_ukv"] = jnp.transpose(dukv, (0, 2, 1, 3)).reshape(1, MLA_KV_RANK, -1)
    out["fox_w_qkvf"] = G["fox_w_qkvf"][None, :, :3 * N_HEADS * HEAD_DIM + N_HEADS]
    out["ffn_w_gate"] = G["ffn_w_gate"]
    out["ffn_w_up"] = G["ffn_w_up"]
    out["ffn_w_down"] = G["ffn_w_down"]
    out["mla_q_norm"] = G["mla_q_norm"]
    out["mla_kv_norm"] = G["mla_kv_norm"]
    return out


def _pairs_col(f16):
    M = f16.shape[0]
    return jnp.transpose(f16.reshape(M, N_PAIRS, 2), (1, 0, 2))


def _pairs_row(f16):
    M = f16.shape[0]
    return jnp.transpose(f16.reshape(M, N_PAIRS, 2), (1, 2, 0))


def _local_step(x, target, W, P):
    S = x.shape[0]
    M = S + ROW0
    G = {}
    gain = lambda name, i: P[name][i][None, :]
    h0 = jnp.concatenate([jnp.zeros((PAD, D_MODEL), F32), P["meta"], x], axis=0)

    def ffn_fwd(h1, i):
        g, u, act, b = _ffn_up(h1, gain("norm_ffn", i), W["ffn_w_gate"][i], W["ffn_w_up"][i], f"ffn{i}_up")
        h2 = _mm_nn(act, W["ffn_w_down"][i], F32, f"ffn{i}_down", res=h1)
        return h2, (h1, b, g, u, act)

    def ffn_bwd(dh2, saved, i):
        h1, b, g, u, act = saved
        dg, du = _ffn_dact(dh2, W["ffn_w_down"][i], g, u, f"ffn{i}_dact")
        G.setdefault("ffn_w_down", {})[i] = _mm_tn(act, dh2, f"ffn{i}_dwd")
        G.setdefault("ffn_w_gate", {})[i] = _mm_tn(b, dg, f"ffn{i}_dwg")
        G.setdefault("ffn_w_up", {})[i] = _mm_tn(b, du, f"ffn{i}_dwu")
        dh1, dgain = _mm_nt_dnorm([(dg, W["ffn_w_gate"][i]), (du, W["ffn_w_up"][i])], h1, gain("norm_ffn", i), dh2,
                                  f"ffn{i}_db_dnorm")
        G.setdefault("norm_ffn", {})[i] = dgain
        return dh1

    a0 = _norm_fwd(h0, gain("norm_mix", 0), F32, "mix0_norm")
    h1_0, pooled = _pool_fwd(h0, a0, W["pool_w"], P["pool_scale"], "pool_fwd")
    h_1, ffn0 = ffn_fwd(h1_0, 0)

    sb_scale = HEAD_DIM ** -0.5
    sb_qkv, a1 = _norm_mm(h_1, gain("norm_mix", 1), W["sb_w_qkv"], BF16, "sb_qkv")
    sb_o, sb_tot = _sb_fwd(sb_qkv, sb_scale, "sb_fwd")
    h1_1 = _mm_nn(sb_o, W["sb_w_o"], F32, "sb_out", res=h_1)
    h_2, ffn1 = ffn_fwd(h1_1, 1)

    mla_scale = (MLA_NOPE + MLA_ROPE) ** -0.5
    cos_t, sin_t = _rope_tables(M)
    down, a2 = _norm_mm(h_2, gain("norm_mix", 2), W["mla_w_down"], F32, "mla_down")
    dq_raw = down[:, :MLA_Q_RANK]
    dkv_raw = down[:, MLA_Q_RANK:MLA_Q_RANK + MLA_KV_RANK]
    kr_raw = down[:, MLA_Q_RANK + MLA_KV_RANK:]
    q_lin, c_q = _norm_mm(dq_raw, P["mla_q_norm"], W["mla_w_uq"], F32, "mla_uq")
    q_all = _rope(q_lin, cos_t, sin_t, BF16, "mla_qrope", lead=D_MODEL)
    kv_all, c_kv = _norm_mm(dkv_raw, P["mla_kv_norm"], W["mla_w_ukv"], BF16, "mla_ukv")
    kr_in = jnp.concatenate([kr_raw, kr_raw, jnp.zeros((M, 64), F32)], axis=1)
    kr = _rope(kr_in, cos_t, sin_t, BF16, "mla_krope")
    q_rope = q_all[:, D_MODEL:]
    mla_o, mla_lse = _mla_fwd(q_all, kv_all, q_rope, kr, mla_scale, "mla_fwd")
    h1_2 = _mm_nn(mla_o, W["mla_w_o"], F32, "mla_out", res=h_2)
    h_3, ffn2 = ffn_fwd(h1_2, 2)

    fox_scale = HEAD_DIM ** -0.5
    fox_qkv, a3 = _norm_mm(h_3, gain("norm_mix", 3), W["fox_w_qkv"], BF16, "fox_qkv")
    f_logit = _mm_nn(a3, W["fox_w_f"], F32, "fox_f")
    b_f = jnp.pad(P["fox_b_f"], ((0, 0), (0, 128 - N_HEADS)))
    Fc = _forget_cumsum(f_logit, b_f, "fox_cumsum")
    f_rows, f_cols = _pairs_row(Fc[:, :N_HEADS]), _pairs_col(Fc[:, :N_HEADS])
    fox_qkv_t = fox_qkv.T
    fox_o_t, fox_lse, fox_ox_t = _fox_fwd(fox_qkv, fox_qkv_t, f_rows, f_cols, fox_scale, "fox_fwd")
    fox_o = fox_o_t.T
    h1_3 = _mm_nn(fox_o, W["fox_w_o"], F32, "fox_out", res=h_3)
    h_4, ffn3 = ffn_fwd(h1_3, 3)

    sq, dh, dgain = _loss_head(h_4, P["final_norm"][None, :], target, "loss_head")
    G["final_norm"] = dgain[0]

    dh = ffn_bwd(dh, ffn3, 3)
    do = _mm_nt(dh, W["fox_w_o"], BF16, "fox_do")
    G["fox_w_o"] = _mm_tn(fox_o, dh, "fox_dwo")
    dq_t, dk, dv, colsum = _fox_bwd(fox_qkv, fox_qkv_t, fox_ox_t, do, do.T, fox_lse, f_rows, f_cols, fox_scale,
                                    "fox_bwd")
    dlogit, db_f = _forget_cumsum_bwd(f_logit, b_f, colsum, "fox_dcumsum")
    G["fox_b_f"] = db_f[:, :N_HEADS]
    dproj = jnp.concatenate([dq_t.T, dk.astype(BF16), dv.astype(BF16), dlogit.astype(BF16)], axis=1)
    G["fox_w_qkvf"] = _mm_tn(a3, dproj, "fox_dwqkvf")
    dh, dgain = _mm_nt_dnorm([(dproj, W["fox_w_qkvf"])], h_3, gain("norm_mix", 3), dh, "fox_da_dnorm")
    G.setdefault("norm_mix", {})[3] = dgain

    dh = ffn_bwd(dh, ffn2, 2)
    do = _mm_nt(dh, W["mla_w_o"], BF16, "mla_do")
    G["mla_w_o"] = _mm_tn(mla_o, dh, "mla_dwo")
    dq, dk, dv, dqr, dkr = _mla_bwd(q_all, kv_all, q_rope, kr, mla_o, do, mla_lse, mla_scale, "mla_bwd")
    dqr = _rope(dqr, cos_t, sin_t, BF16, "mla_dqrope", inverse=True)
    dq_all = jnp.concatenate([dq, dqr], axis=1)
    dkr_sum = _rope(jnp.sum(dkr, axis=0), cos_t, sin_t, F32, "mla_dkrope", inverse=True)
    dkr_raw = dkr_sum[:, :MLA_ROPE] + dkr_sum[:, MLA_ROPE:2 * MLA_ROPE]
    dkv_all = jnp.concatenate([dk.astype(BF16), dv.astype(BF16)], axis=1)
    G["mla_w_uq"] = _mm_tn(c_q, dq_all, "mla_dwuq")
    G["mla_w_ukv"] = _mm_tn(c_kv, dkv_all, "mla_dwukv")
    ddq_raw, G["mla_q_norm"] = _mm_nt_dnorm([(dq_all, W["mla_w_uq"])], dq_raw, P["mla_q_norm"], None, "mla_dcq_dnorm")
    ddkv_raw, G["mla_kv_norm"] = _mm_nt_dnorm([(dkv_all, W["mla_w_ukv"])], dkv_raw, P["mla_kv_norm"], None,
                                              "mla_dckv_dnorm")
    ddown = jnp.concatenate([ddq_raw, ddkv_raw, dkr_raw], axis=1).astype(BF16)
    G["mla_w_down"] = _mm_tn(a2, ddown, "mla_dwdown")
    dh, dgain = _mm_nt_dnorm([(ddown, W["mla_w_down"])], h_2, gain("norm_mix", 2), dh, "mla_da_dnorm")
    G["norm_mix"][2] = dgain

    dh = ffn_bwd(dh, ffn1, 1)
    do = _mm_nt(dh, W["sb_w_o"], BF16, "sb_do")
    G["sb_w_o"] = _mm_tn(sb_o, dh, "sb_dwo")
    dq, dk, dv = _sb_bwd(sb_qkv, do, sb_tot, sb_scale, "sb_bwd")
    dqkv = jnp.concatenate([dq, dk.astype(BF16), dv.astype(BF16)], axis=1)
    G["sb_w_qkv"] = _mm_tn(a1, dqkv, "sb_dwqkv")
    dh, dgain = _mm_nt_dnorm([(dqkv, W["sb_w_qkv"])], h_1, gain("norm_mix", 1), dh, "sb_da_dnorm")
    G["norm_mix"][1] = dgain

    dh = ffn_bwd(dh, ffn0, 0)
    dpc, G["pool_w"], G["pool_scale"] = _pool_bwd_mix(dh, pooled, W["pool_w"], P["pool_scale"], "pool_dmix")
    da = _pool_bwd_window(dpc, "pool_dwindow")
    dh, dgain, dx = _norm_bwd(h0, gain("norm_mix", 0), da, dh, "mix0_dnorm", token_rows=True)
    G["norm_mix"][0] = dgain

    G["norm_mix"] = jnp.concatenate([G["norm_mix"][i] for i in range(DEPTH)], axis=0)
    G["norm_ffn"] = jnp.concatenate([G["norm_ffn"][i] for i in range(DEPTH)], axis=0)
    G["ffn_w_down"] = jnp.stack([G["ffn_w_down"][i] for i in range(DEPTH)])
    G["ffn_w_gate"] = jnp.stack([G["ffn_w_gate"][i] for i in range(DEPTH)])
    G["ffn_w_up"] = jnp.stack([G["ffn_w_up"][i] for i in range(DEPTH)])
    G["meta"] = dh[PAD:ROW0]
    return sq, dx, G


def kernel(x, meta, norm_mix, norm_ffn, pool_w, pool_scale, sb_w_qkv, sb_w_o, mla_w_down, mla_q_norm, mla_kv_norm, mla_w_uq, mla_w_ukv, mla_w_o, fox_w_qkvf, fox_b_f, fox_w_o, ffn_w_gate, ffn_w_up, ffn_w_down, final_norm, loss_target, m_meta, m_norm_mix, m_norm_ffn, m_pool_w, m_pool_scale, m_sb_w_qkv, m_sb_w_o, m_mla_w_down, m_mla_q_norm, m_mla_kv_norm, m_mla_w_uq, m_mla_w_ukv, m_mla_w_o, m_fox_w_qkvf, m_fox_b_f, m_fox_w_o, m_ffn_w_gate, m_ffn_w_up, m_ffn_w_down, m_final_norm, v_meta, v_norm_mix, v_norm_ffn, v_pool_w, v_pool_scale, v_sb_w_qkv, v_sb_w_o, v_mla_w_down, v_mla_q_norm, v_mla_kv_norm, v_mla_w_uq, v_mla_w_ukv, v_mla_w_o, v_fox_w_qkvf, v_fox_b_f, v_fox_w_o, v_ffn_w_gate, v_ffn_w_up, v_ffn_w_down, v_final_norm):
    w = dict(meta=meta, norm_mix=norm_mix, norm_ffn=norm_ffn, pool_w=pool_w, pool_scale=pool_scale,
             sb_w_qkv=sb_w_qkv, sb_w_o=sb_w_o, mla_w_down=mla_w_down, mla_q_norm=mla_q_norm,
             mla_kv_norm=mla_kv_norm, mla_w_uq=mla_w_uq, mla_w_ukv=mla_w_ukv, mla_w_o=mla_w_o,
             fox_w_qkvf=fox_w_qkvf, fox_b_f=fox_b_f, fox_w_o=fox_w_o, ffn_w_gate=ffn_w_gate, ffn_w_up=ffn_w_up,
             ffn_w_down=ffn_w_down, final_norm=final_norm)
    m = dict(meta=m_meta, norm_mix=m_norm_mix, norm_ffn=m_norm_ffn, pool_w=m_pool_w, pool_scale=m_pool_scale,
             sb_w_qkv=m_sb_w_qkv, sb_w_o=m_sb_w_o, mla_w_down=m_mla_w_down, mla_q_norm=m_mla_q_norm,
             mla_kv_norm=m_mla_kv_norm, mla_w_uq=m_mla_w_uq, mla_w_ukv=m_mla_w_ukv, mla_w_o=m_mla_w_o,
             fox_w_qkvf=m_fox_w_qkvf, fox_b_f=m_fox_b_f, fox_w_o=m_fox_w_o, ffn_w_gate=m_ffn_w_gate,
             ffn_w_up=m_ffn_w_up, ffn_w_down=m_ffn_w_down, final_norm=m_final_norm)
    v = dict(meta=v_meta, norm_mix=v_norm_mix, norm_ffn=v_norm_ffn, pool_w=v_pool_w, pool_scale=v_pool_scale,
             sb_w_qkv=v_sb_w_qkv, sb_w_o=v_sb_w_o, mla_w_down=v_mla_w_down, mla_q_norm=v_mla_q_norm,
             mla_kv_norm=v_mla_kv_norm, mla_w_uq=v_mla_w_uq, mla_w_ukv=v_mla_w_ukv, mla_w_o=v_mla_w_o,
             fox_w_qkvf=v_fox_w_qkvf, fox_b_f=v_fox_b_f, fox_w_o=v_fox_w_o, ffn_w_gate=v_ffn_w_gate,
             ffn_w_up=v_ffn_w_up, ffn_w_down=v_ffn_w_down, final_norm=v_final_norm)

    sh_names = tuple(n for n, _ in SHARDED)
    sh_axis = dict(SHARDED)
    shapes = {n: w[n].shape for n in WEIGHT_NAMES}
    wire = lambda n: F32 if n in KEPT_F32 else BF16

    gathered = _all_gather([w[n].astype(wire(n)) for n in sh_names], "gather_weights")
    full = {n: _whole_from_gathered(g, sh_axis[n]) for n, g in zip(sh_names, gathered)}
    W = _kernel_weights(full)
    P = dict(meta=full["meta"], mla_q_norm=full["mla_q_norm"], mla_kv_norm=full["mla_kv_norm"],
             norm_mix=norm_mix, norm_ffn=norm_ffn, pool_scale=pool_scale, fox_b_f=fox_b_f, final_norm=final_norm)

    sq, dx, G = _local_step(x[0], loss_target[0], W, P)
    loss = lax.psum(0.5 * jnp.sum(sq) / D_MODEL, ("x", "y", "c"))
    grad_x = dx[None]

    gw = _reference_grads(G)
    gw["meta"] = G["meta"]
    rc = {n: (int(np.prod(shapes[n][:-1])), shapes[n][-1]) for n in sh_names}
    parts = [_parts_from_whole(gw[n], sh_axis[n]).astype(wire(n)).reshape((N_DEV,) + rc[n]) for n in sh_names]
    from_sibling = _exchange_siblings(parts, "exchange_grads_d2d")
    sums = [_pair_sum(a, b, f"pair_sum_{n}") for n, a, b in zip(sh_names, parts, from_sibling)]
    landed = _exchange_chips(sums, "exchange_grads_ici")
    results = {}
    for n, got in zip(sh_names, landed):
        outs = _adamw(w[n].reshape(rc[n]), got, m[n].reshape(rc[n]), v[n].reshape(rc[n]), f"adamw_{n}")
        results[n] = [o.reshape(shapes[n]) for o in outs]

    rep_g = dict(norm_mix=G["norm_mix"], norm_ffn=G["norm_ffn"], pool_scale=G["pool_scale"], fox_b_f=G["fox_b_f"],
                 final_norm=G["final_norm"])
    (rep_all,) = _all_gather([_pack_rows(rep_g, REPLICATED)], "gather_replicated_grads")
    rep_out = _adamw(_pack_rows(w, REPLICATED), rep_all, _pack_rows(m, REPLICATED), _pack_rows(v, REPLICATED),
                     "adamw_replicated")
    rep = [_unpack_rows(o, shapes, REPLICATED) for o in rep_out]
    for n in REPLICATED:
        results[n] = [r[n] for r in rep]

    outs = [results[n][k] for k in range(4) for n in WEIGHT_NAMES]
    return (loss, grad_x, *outs)
```

```python
import numpy as np
import jax
import jax.numpy as jnp
from jax import lax
from jax.experimental import pallas as pl
from jax.experimental.pallas import tpu as pltpu

F32 = jnp.float32
BF16 = jnp.bfloat16

N_DEV = 8
D_MODEL = 1024
N_META = 16
PAD = 240
ROW0 = PAD + N_META
EPS = 1e-6
POOL_WINDOWS = (2, 4, 8, 16)
POOL_GROUP = 256
HALO = 128
N_HEADS = 16
HEAD_DIM = 64
N_PAIRS = N_HEADS // 2
MLA_Q_RANK = 384
MLA_KV_RANK = 256
MLA_NOPE = 64
MLA_ROPE = 32
ROPE_THETA = 10000.0
D_FF = 2816
DEPTH = 4
ATTN_TILE = 768
ATTN_BWD_TILE = 768
WALK_TILE = 256
FOX_TILE = 384
NEG = -1e30
LOG2E = 1.4426950408889634
EXP_ZERO = -110.0
VMEM_LIMIT = 56 * 2**20
ADAM_TILE_ELEMS = 192 * 1024

ADAM_LR = 0.001
ADAM_B1 = 0.9
ADAM_B2 = 0.999
ADAM_EPS = 1e-08
ADAM_WD = 0.01
ADAM_STEP = 10

MESH = pl.DeviceIdType.MESH


def _params(sem=None):
    return pltpu.CompilerParams(dimension_semantics=sem, vmem_limit_bytes=VMEM_LIMIT)


def _pick(n, cands):
    for c in cands:
        if n % c == 0:
            return c
    return n


def _col_tile(n, cap=1536):
    best = None
    for t in range(128, min(n, cap) + 1, 128):
        if n % t == 0:
            best = t
    return best if best is not None else n


def _dot(a, b):
    return jnp.dot(a, b, preferred_element_type=F32)


def _dot_nt(a, b):
    return lax.dot_general(a, b, (((1,), (1,)), ((), ())), preferred_element_type=F32)


def _dot_tn(a, b):
    return lax.dot_general(a, b, (((0,), (0,)), ((), ())), preferred_element_type=F32)


def _mm_nn(a, b, out_dtype, name, res=None):
    M, K = a.shape
    N = b.shape[1]
    tm = _pick(M, (768, 512, 256, 128))
    tn = _col_tile(N)

    def body(*refs):
        if res is None:
            a_ref, b_ref, o_ref = refs
        else:
            a_ref, b_ref, r_ref, o_ref = refs
        acc = _dot(a_ref[...].astype(BF16), b_ref[...])
        if res is not None:
            acc = acc + r_ref[...]
        o_ref[...] = acc.astype(o_ref.dtype)

    in_specs = [pl.BlockSpec((tm, K), lambda n, m: (m, 0)), pl.BlockSpec((K, tn), lambda n, m: (0, n))]
    args = [a, b]
    if res is not None:
        in_specs.append(pl.BlockSpec((tm, tn), lambda n, m: (m, n)))
        args.append(res)
    return pl.pallas_call(
        body, name=name, grid=(N // tn, M // tm), in_specs=in_specs,
        out_specs=pl.BlockSpec((tm, tn), lambda n, m: (m, n)),
        out_shape=jax.ShapeDtypeStruct((M, N), out_dtype),
        compiler_params=_params(("parallel", "parallel")))(*args)


def _norm_mm(h, gain, b, out_dtype, name, transposed_too=False):
    M, K = h.shape
    N = b.shape[1]
    tm = _pick(M, (384, 256, 128))

    def body(h_ref, g_ref, b_ref, o_ref, a_ref, *ot_ref):
        x = h_ref[...]
        r = lax.rsqrt(jnp.mean(x * x, axis=-1, keepdims=True) + EPS)
        a = ((x * r) * g_ref[...]).astype(BF16)
        a_ref[...] = a
        acc = _dot(a, b_ref[...])
        o_ref[...] = acc.astype(o_ref.dtype)
        if transposed_too:
            ot_ref[0][...] = acc.T.astype(o_ref.dtype)

    out_specs = [pl.BlockSpec((tm, N), lambda m: (m, 0)), pl.BlockSpec((tm, K), lambda m: (m, 0))]
    out_shape = [jax.ShapeDtypeStruct((M, N), out_dtype), jax.ShapeDtypeStruct((M, K), BF16)]
    if transposed_too:
        out_specs.append(pl.BlockSpec((N, tm), lambda m: (0, m)))
        out_shape.append(jax.ShapeDtypeStruct((N, M), out_dtype))
    return pl.pallas_call(
        body, name=name, grid=(M // tm,),
        in_specs=[pl.BlockSpec((tm, K), lambda m: (m, 0)), pl.BlockSpec((1, K), lambda m: (0, 0)),
                  pl.BlockSpec((K, N), lambda m: (0, 0))],
        out_specs=out_specs, out_shape=out_shape,
        compiler_params=_params(("parallel",)))(h, gain, b)


def _mm_nt(a, w, out_dtype, name, transposed_too=False):
    M, N = a.shape
    K = w.shape[0]
    tm = _pick(M, (768, 512, 256, 128)) if N <= 3200 else _pick(M, (256, 128))
    tk = _col_tile(K, 1024)

    def body(a_ref, w_ref, o_ref, *ot_ref):
        acc = _dot_nt(a_ref[...].astype(BF16), w_ref[...])
        o_ref[...] = acc.astype(o_ref.dtype)
        if transposed_too:
            ot_ref[0][...] = acc.T.astype(o_ref.dtype)

    out_specs = [pl.BlockSpec((tm, tk), lambda k, m: (m, k))]
    out_shape = [jax.ShapeDtypeStruct((M, K), out_dtype)]
    if transposed_too:
        out_specs.append(pl.BlockSpec((tk, tm), lambda k, m: (k, m)))
        out_shape.append(jax.ShapeDtypeStruct((K, M), out_dtype))
    out = pl.pallas_call(
        body, name=name, grid=(K // tk, M // tm),
        in_specs=[pl.BlockSpec((tm, N), lambda k, m: (m, 0)), pl.BlockSpec((tk, N), lambda k, m: (k, 0))],
        out_specs=out_specs, out_shape=out_shape,
        compiler_params=_params(("parallel", "parallel")))(a, w)
    return out if transposed_too else out[0]


def _mm_nt_dnorm(pairs, h, gain, dres, name):
    M, K = h.shape
    n = len(pairs)
    tm = _pick(M, (384, 256, 128))

    def body(*refs):
        refs = list(refs)
        ab, rest = refs[:2 * n], refs[2 * n:]
        h_ref, g_ref = rest[0], rest[1]
        dr_ref = rest[2] if dres is not None else None
        dh_ref, dg_ref = rest[-2], rest[-1]
        da = None
        for i in range(n):
            d = _dot_nt(ab[2 * i][...], ab[2 * i + 1][...])
            da = d if da is None else da + d
        x = h_ref[...]
        r = lax.rsqrt(jnp.mean(x * x, axis=-1, keepdims=True) + EPS)
        y = x * r
        dy = da * g_ref[...]
        dh = r * (dy - y * jnp.mean(dy * y, axis=-1, keepdims=True))
        if dr_ref is not None:
            dh = dh + dr_ref[...]
        dh_ref[...] = dh

        @pl.when(pl.program_id(0) == 0)
        def _():
            dg_ref[...] = jnp.zeros_like(dg_ref)
        dg_ref[...] += jnp.sum(da * y, axis=0, keepdims=True)

    row = pl.BlockSpec((tm, K), lambda m: (m, 0))
    vec = pl.BlockSpec((1, K), lambda m: (0, 0))
    in_specs, args = [], []
    for a, w in pairs:
        in_specs += [pl.BlockSpec((tm, a.shape[1]), lambda m: (m, 0)), pl.BlockSpec(w.shape, lambda m: (0, 0))]
        args += [a, w]
    in_specs += [row, vec] + ([row] if dres is not None else [])
    args += [h, gain] + ([dres] if dres is not None else [])
    return pl.pallas_call(
        body, name=name, grid=(M // tm,), in_specs=in_specs, out_specs=[row, vec],
        out_shape=[jax.ShapeDtypeStruct((M, K), F32), jax.ShapeDtypeStruct((1, K), F32)],
        compiler_params=_params(("arbitrary",)))(*args)


def _mm_tn(a, b, name):
    M, K = a.shape
    N = b.shape[1]
    tm = _pick(M, (768, 512, 256, 128))
    tk = _col_tile(K, 1408)
    tn = _col_tile(N, 1408)

    def body(a_ref, b_ref, o_ref):
        @pl.when(pl.program_id(2) == 0)
        def _():
            o_ref[...] = jnp.zeros_like(o_ref)
        o_ref[...] += _dot_tn(a_ref[...].astype(BF16), b_ref[...].astype(BF16))

    return pl.pallas_call(
        body, name=name, grid=(K // tk, N // tn, M // tm),
        in_specs=[pl.BlockSpec((tm, tk), lambda k, n, m: (m, k)), pl.BlockSpec((tm, tn), lambda k, n, m: (m, n))],
        out_specs=pl.BlockSpec((tk, tn), lambda k, n, m: (k, n)),
        out_shape=jax.ShapeDtypeStruct((K, N), F32),
        compiler_params=_params(("parallel", "parallel", "arbitrary")))(a, b)


def _norm_fwd(h, gain, out_dtype, name):
    M, C = h.shape
    tm = _pick(M, (768, 512, 256, 128))

    def body(h_ref, g_ref, a_ref):
        x = h_ref[...]
        r = lax.rsqrt(jnp.mean(x * x, axis=-1, keepdims=True) + EPS)
        a_ref[...] = ((x * r) * g_ref[...]).astype(a_ref.dtype)

    return pl.pallas_call(
        body, name=name, grid=(M // tm,),
        in_specs=[pl.BlockSpec((tm, C), lambda m: (m, 0)), pl.BlockSpec((1, C), lambda m: (0, 0))],
        out_specs=pl.BlockSpec((tm, C), lambda m: (m, 0)),
        out_shape=jax.ShapeDtypeStruct((M, C), out_dtype),
        compiler_params=_params(("parallel",)))(h, gain)


def _norm_bwd(h, gain, da, dres, name, token_rows=False):
    M, C = h.shape
    tm = ROW0 if token_rows else _pick(M, (768, 512, 256, 128))

    def body(*refs):
        refs = list(refs)
        dx_ref = refs.pop() if token_rows else None
        if dres is None:
            h_ref, g_ref, da_ref, dh_ref, dg_ref = refs
        else:
            h_ref, g_ref, da_ref, dr_ref, dh_ref, dg_ref = refs
        x = h_ref[...]
        r = lax.rsqrt(jnp.mean(x * x, axis=-1, keepdims=True) + EPS)
        y = x * r
        dav = da_ref[...].astype(F32)
        dy = dav * g_ref[...]
        dh = r * (dy - y * jnp.mean(dy * y, axis=-1, keepdims=True))
        if dres is not None:
            dh = dh + dr_ref[...]
        dh_ref[...] = dh
        if token_rows:
            dx_ref[...] = dh

        @pl.when(pl.program_id(0) == 0)
        def _():
            dg_ref[...] = jnp.zeros_like(dg_ref)
        dg_ref[...] += jnp.sum(dav * y, axis=0, keepdims=True)

    row = pl.BlockSpec((tm, C), lambda m: (m, 0))
    vec = pl.BlockSpec((1, C), lambda m: (0, 0))
    in_specs = [row, vec, row] + ([row] if dres is not None else [])
    args = [h, gain, da] + ([dres] if dres is not None else [])
    out_specs = [row, vec]
    out_shape = [jax.ShapeDtypeStruct((M, C), F32), jax.ShapeDtypeStruct((1, C), F32)]
    if token_rows:
        out_specs.append(pl.BlockSpec((tm, C), lambda m: (jnp.maximum(m - 1, 0), 0)))
        out_shape.append(jax.ShapeDtypeStruct((M - ROW0, C), F32))
    return pl.pallas_call(
        body, name=name, grid=(M // tm,), in_specs=in_specs, out_specs=out_specs, out_shape=out_shape,
        compiler_params=_params(("arbitrary",)))(*args)


def _ffn_up(h, gain, w_g, w_u, name):
    M, K = h.shape
    F = w_g.shape[1]
    tm = _pick(M, (384, 256, 128))

    def body(h_ref, gn_ref, wg_ref, wu_ref, g_ref, u_ref, act_ref, b_ref):
        x = h_ref[...]
        r = lax.rsqrt(jnp.mean(x * x, axis=-1, keepdims=True) + EPS)
        b = ((x * r) * gn_ref[...]).astype(BF16)
        b_ref[...] = b
        g = _dot(b, wg_ref[...])
        u = _dot(b, wu_ref[...])
        g_ref[...] = g.astype(g_ref.dtype)
        u_ref[...] = u.astype(u_ref.dtype)
        act_ref[...] = ((g * jax.nn.sigmoid(g)) * u).astype(act_ref.dtype)

    blk = pl.BlockSpec((tm, F), lambda m: (m, 0))
    wgt = pl.BlockSpec((K, F), lambda m: (0, 0))
    wide = jax.ShapeDtypeStruct((M, F), BF16)
    return pl.pallas_call(
        body, name=name, grid=(M // tm,),
        in_specs=[pl.BlockSpec((tm, K), lambda m: (m, 0)), pl.BlockSpec((1, K), lambda m: (0, 0)), wgt, wgt],
        out_specs=[blk, blk, blk, pl.BlockSpec((tm, K), lambda m: (m, 0))],
        out_shape=[wide, wide, wide, jax.ShapeDtypeStruct((M, K), BF16)],
        compiler_params=_params(("parallel",)))(h, gain, w_g, w_u)


def _ffn_dact(dy, w_d, g, u, name):
    M, K = dy.shape
    F = w_d.shape[0]
    tm = _pick(M, (768, 512, 256, 128))
    tn = _col_tile(F, 1408)
    nb = F // tn

    def body(dy_ref, wd_ref, g_ref, u_ref, dg_ref, du_ref):
        dact = _dot_nt(dy_ref[...].astype(BF16), wd_ref[...])
        gv = g_ref[...].astype(F32)
        s = jax.nn.sigmoid(gv)
        silu = gv * s
        dg_ref[...] = (dact * u_ref[...].astype(F32) * (s * (1.0 + gv * (1.0 - s)))).astype(dg_ref.dtype)
        du_ref[...] = (dact * silu).astype(du_ref.dtype)

    blk = pl.BlockSpec((tm, tn), lambda n, m: (m, n))
    return pl.pallas_call(
        body, name=name, grid=(nb, M // tm),
        in_specs=[pl.BlockSpec((tm, K), lambda n, m: (m, 0)), pl.BlockSpec((tn, K), lambda n, m: (n, 0)), blk, blk],
        out_specs=[blk, blk],
        out_shape=[jax.ShapeDtypeStruct((M, F), BF16), jax.ShapeDtypeStruct((M, F), BF16)],
        compiler_params=_params(("parallel", "parallel")))(dy, w_d, g, u)


def _loss_head(h, gain, target, name):
    M, C = h.shape
    tm = ROW0
    assert M % tm == 0 and target.shape[0] == M - ROW0

    def body(h_ref, g_ref, t_ref, sq_ref, dh_ref, dg_ref):
        i = pl.program_id(0)

        @pl.when(i == 0)
        def _():
            sq_ref[...] = jnp.zeros_like(sq_ref)
            dg_ref[...] = jnp.zeros_like(dg_ref)
            dh_ref[...] = jnp.zeros_like(dh_ref)

        @pl.when(i > 0)
        def _():
            x = h_ref[...]
            r = lax.rsqrt(jnp.mean(x * x, axis=-1, keepdims=True) + EPS)
            y = x * r
            err = y * g_ref[...] - t_ref[...]
            sq_ref[...] += jnp.sum(err * err, axis=0, keepdims=True)
            da = err * (1.0 / C)
            dy = da * g_ref[...]
            dh_ref[...] = r * (dy - y * jnp.mean(dy * y, axis=-1, keepdims=True))
            dg_ref[...] += jnp.sum(da * y, axis=0, keepdims=True)

    row = pl.BlockSpec((tm, C), lambda m: (m, 0))
    vec = pl.BlockSpec((1, C), lambda m: (0, 0))
    return pl.pallas_call(
        body, name=name, grid=(M // tm,),
        in_specs=[row, vec, pl.BlockSpec((tm, C), lambda m: (jnp.maximum(m - 1, 0), 0))],
        out_specs=[vec, row, vec],
        out_shape=[jax.ShapeDtypeStruct((1, C), F32), jax.ShapeDtypeStruct((M, C), F32),
                   jax.ShapeDtypeStruct((1, C), F32)],
        compiler_params=_params(("arbitrary",)))(h, gain, target)


def _band_dot(band, x):
    hi = x.astype(BF16)
    rest = x - hi.astype(F32)
    mid = rest.astype(BF16)
    lo = (rest - mid.astype(F32)).astype(BF16)
    return _dot(band, hi) + _dot(band, mid) + _dot(band, lo)


def _pool_pos(row0, tm):
    return row0 + lax.broadcasted_iota(jnp.int32, (tm, 1), 0) - PAD


def _pool_fwd(h, a, w, scale, name):
    M, C = a.shape
    tm = 256
    hb = tm // HALO

    def body(h_ref, a_ref, halo_ref, w_ref, s_ref, o_ref, p_ref):
        i = pl.program_id(0)
        row0 = i * tm
        ext = jnp.concatenate([halo_ref[...], a_ref[...]], axis=0)
        src = row0 - HALO + lax.broadcasted_iota(jnp.int32, (tm + HALO, 1), 0)
        ext = jnp.where(src >= PAD, ext, 0.0)
        r = lax.broadcasted_iota(jnp.int32, (tm, tm + HALO), 0)
        c = lax.broadcasted_iota(jnp.int32, (tm, tm + HALO), 1)
        pos = _pool_pos(row0, tm)
        for g, win in enumerate(POOL_WINDOWS):
            band = ((c <= r + HALO) & (c > r + HALO - win)).astype(BF16)
            cols = slice(g * POOL_GROUP, (g + 1) * POOL_GROUP)
            xg = ext[:, cols]
            tot = _band_dot(band, xg)
            cnt = jnp.clip(pos + 1, 1, win).astype(F32)
            pooled = (tot / cnt - xg[HALO:]).astype(BF16)
            p_ref[:, cols] = pooled
            mixed = _dot(pooled, w_ref[g])
            o_ref[:, cols] = h_ref[:, cols] + mixed * s_ref[:, cols]

    row = pl.BlockSpec((tm, C), lambda m: (m, 0))
    return pl.pallas_call(
        body, name=name, grid=(M // tm,),
        in_specs=[row, row, pl.BlockSpec((HALO, C), lambda m: (jnp.maximum(m * hb - 1, 0), 0)),
                  pl.BlockSpec((4, POOL_GROUP, POOL_GROUP), lambda m: (0, 0, 0)),
                  pl.BlockSpec((1, C), lambda m: (0, 0))],
        out_specs=[row, row],
        out_shape=[jax.ShapeDtypeStruct((M, C), F32), jax.ShapeDtypeStruct((M, C), BF16)],
        compiler_params=_params(("parallel",)))(h, a, a, w, scale)


def _pool_bwd_mix(dout, pooled, w, scale, name):
    M, C = dout.shape
    tm = 256

    def body(do_ref, p_ref, w_ref, s_ref, dpc_ref, dw_ref, ds_ref):
        i = pl.program_id(0)

        @pl.when(i == 0)
        def _():
            dw_ref[...] = jnp.zeros_like(dw_ref)
            ds_ref[...] = jnp.zeros_like(ds_ref)

        pos = _pool_pos(i * tm, tm)
        for g, win in enumerate(POOL_WINDOWS):
            cols = slice(g * POOL_GROUP, (g + 1) * POOL_GROUP)
            do = do_ref[:, cols]
            pooled = p_ref[:, cols]
            mixed = _dot(pooled, w_ref[g])
            ds_ref[:, cols] += jnp.sum(do * mixed, axis=0, keepdims=True)
            dmix = (do * s_ref[:, cols]).astype(BF16)
            dw_ref[g] += _dot_tn(pooled, dmix)
            dp = _dot_nt(dmix, w_ref[g])
            cnt = jnp.clip(pos + 1, 1, win).astype(F32)
            dpc_ref[:, cols] = dp / cnt

    row = pl.BlockSpec((tm, C), lambda m: (m, 0))
    wspec = pl.BlockSpec((4, POOL_GROUP, POOL_GROUP), lambda m: (0, 0, 0))
    vec = pl.BlockSpec((1, C), lambda m: (0, 0))
    return pl.pallas_call(
        body, name=name, grid=(M // tm,),
        in_specs=[row, row, wspec, vec], out_specs=[row, wspec, vec],
        out_shape=[jax.ShapeDtypeStruct((M, C), F32), jax.ShapeDtypeStruct((4, POOL_GROUP, POOL_GROUP), F32),
                   jax.ShapeDtypeStruct((1, C), F32)],
        compiler_params=_params(("arbitrary",)))(dout, pooled, w, scale)


def _pool_bwd_window(dpc, name):
    M, C = dpc.shape
    tm = 256
    hb = tm // HALO
    last = M // HALO - 1

    def body(d_ref, halo_ref, da_ref):
        i = pl.program_id(0)
        row0 = i * tm
        ext = jnp.concatenate([d_ref[...], halo_ref[...]], axis=0)
        src = row0 + lax.broadcasted_iota(jnp.int32, (tm + HALO, 1), 0)
        ext = jnp.where(src < M, ext, 0.0)
        r = lax.broadcasted_iota(jnp.int32, (tm, tm + HALO), 0)
        c = lax.broadcasted_iota(jnp.int32, (tm, tm + HALO), 1)
        pos = _pool_pos(row0, tm)
        for g, win in enumerate(POOL_WINDOWS):
            band = ((c >= r) & (c < r + win)).astype(BF16)
            cols = slice(g * POOL_GROUP, (g + 1) * POOL_GROUP)
            xg = ext[:, cols]
            tot = _band_dot(band, xg)
            cnt = jnp.clip(pos + 1, 1, win).astype(F32)
            da_ref[:, cols] = jnp.where(pos >= 0, tot - xg[:tm] * cnt, 0.0)

    row = pl.BlockSpec((tm, C), lambda m: (m, 0))
    return pl.pallas_call(
        body, name=name, grid=(M // tm,),
        in_specs=[row, pl.BlockSpec((HALO, C), lambda m: (jnp.minimum((m + 1) * hb, last), 0))],
        out_specs=row, out_shape=jax.ShapeDtypeStruct((M, C), F32),
        compiler_params=_params(("parallel",)))(dpc, dpc)


def _head_masks():
    lane = lax.broadcasted_iota(jnp.int32, (1, 128), 1)
    return lane < HEAD_DIM, lane


def _split_heads(x, first):
    z = jnp.zeros_like(x)
    return jnp.where(first, x, z), jnp.where(first, z, x)


def _split_rope(x, lane):
    z = jnp.zeros_like(x)
    return jnp.where(lane < MLA_ROPE, x, z), jnp.where((lane >= MLA_ROPE) & (lane < 2 * MLA_ROPE), x, z)


def _walk_causal(i, step):
    def mid(kb, carry):
        step(kb, False)
        return carry

    step(0, True)
    lax.fori_loop(1, i, mid, 0)

    @pl.when(i > 0)
    def _():
        step(i, True)


def _mla_fwd(q_all, kv_all, qr, kr, scale, name):
    M = q_all.shape[0]
    t = ATTN_TILE

    def body(q_ref, k_ref, v_ref, qr_ref, kr_ref, o_ref, lse_ref, m_s, l_s, acc_s, kmax_s):
        i = pl.program_id(1)
        first, lane = _head_masks()

        @pl.when(i == 0)
        def _():
            def block_max(kb, carry):
                rows = pl.ds(pl.multiple_of(kb * t, t), t)
                kk = k_ref[rows, :].astype(F32)
                kk = kk * kk
                rr = kr_ref[rows, :].astype(F32)
                rr = jnp.sum(jnp.where(lane < MLA_ROPE, rr * rr, 0.0), axis=1, keepdims=True)
                a = jnp.max(jnp.sum(jnp.where(first, kk, 0.0), axis=1, keepdims=True) + rr)
                b = jnp.max(jnp.sum(jnp.where(first, 0.0, kk), axis=1, keepdims=True) + rr)
                return jnp.maximum(carry[0], a), jnp.maximum(carry[1], b)

            a, b = lax.fori_loop(0, M // t, block_max, (jnp.float32(0.0), jnp.float32(0.0)))
            kmax_s[0] = a
            kmax_s[1] = b

        qs = _split_heads(q_ref[...], first)
        qrs = _split_rope(qr_ref[...], lane)
        qcat = tuple(jnp.concatenate([qs[hh], qrs[hh]], axis=1) for hh in range(2))
        qpos = i * t + lax.broadcasted_iota(jnp.int32, (t, t), 0)
        kidx = lax.broadcasted_iota(jnp.int32, (t, t), 1)
        c2 = scale * LOG2E

        def run(online):
            l_s[...] = jnp.zeros_like(l_s)
            acc_s[...] = jnp.zeros_like(acc_s)
            if online:
                m_s[...] = jnp.full_like(m_s, NEG)

            def step(kb, masked):
                k0 = pl.multiple_of(kb * t, t)
                kcat = jnp.concatenate([k_ref[pl.ds(k0, t), :], kr_ref[pl.ds(k0, t), :]], axis=1)
                vs = _split_heads(v_ref[pl.ds(k0, t), :], first)
                if masked:
                    kpos = k0 + kidx
                    valid = (kpos <= qpos) & (kpos >= PAD)
                pv = None
                alphas = []
                for hh in range(2):
                    s = _dot_nt(qcat[hh], kcat)
                    if online:
                        if masked:
                            s = jnp.where(valid, s, NEG)
                        m_old = m_s[hh]
                        m_new = jnp.maximum(m_old, jnp.max(s, axis=1, keepdims=True))
                        p = jnp.exp2((s - m_new) * c2)
                        alpha = jnp.exp2((m_old - m_new) * c2)
                        l_s[hh] = alpha * l_s[hh] + jnp.sum(p, axis=1, keepdims=True)
                        m_s[hh] = m_new
                        alphas.append(alpha)
                    else:
                        p = jnp.exp2(s * c2 - m_s[hh])
                        if masked:
                            p = jnp.where(valid, p, 0.0)
                        l_s[hh] = l_s[hh] + jnp.sum(p, axis=1, keepdims=True)
                    d = _dot(p.astype(BF16), vs[hh])
                    pv = d if pv is None else pv + d
                if online:
                    acc_s[...] = acc_s[...] * jnp.where(first, alphas[0], alphas[1]) + pv
                else:
                    acc_s[...] += pv

            _walk_causal(i, step)

        for hh in range(2):
            qf = qcat[hh].astype(F32)
            m_s[hh] = (1.001 * c2) * jnp.sqrt(jnp.sum(qf * qf, axis=1, keepdims=True) * kmax_s[hh])
        run(False)
        real = i * t + lax.broadcasted_iota(jnp.int32, (t, 1), 0) >= PAD
        underflow = jnp.max(jnp.where(real & (jnp.minimum(l_s[0], l_s[1]) < 1e-30), 1.0, 0.0)) > 0.0

        @pl.when(underflow)
        def _():
            run(True)
            m_s[...] = m_s[...] * c2

        ls = tuple(jnp.where(l_s[hh] > 0.0, l_s[hh], 1.0) for hh in range(2))
        o_ref[...] = (acc_s[...] * jnp.where(first, 1.0 / ls[0], 1.0 / ls[1])).astype(o_ref.dtype)
        lse_ref[:, 0:1] = m_s[0] * (1.0 / LOG2E) + jnp.log(ls[0])
        lse_ref[:, 1:2] = m_s[1] * (1.0 / LOG2E) + jnp.log(ls[1])

    blk = pl.BlockSpec((t, 128), lambda j, i: (i, j))
    return pl.pallas_call(
        body, name=name, grid=(N_PAIRS, M // t),
        in_specs=[blk, pl.BlockSpec((M, 128), lambda j, i: (0, j)), pl.BlockSpec((M, 128), lambda j, i: (0, N_PAIRS + j)),
                  blk, pl.BlockSpec((M, 128), lambda j, i: (0, 0))],
        out_specs=[blk, pl.BlockSpec((None, t, 2), lambda j, i: (j, i, 0))],
        out_shape=[jax.ShapeDtypeStruct((M, N_PAIRS * 128), BF16), jax.ShapeDtypeStruct((N_PAIRS, M, 2), F32)],
        scratch_shapes=[pltpu.VMEM((2, t, 1), F32), pltpu.VMEM((2, t, 1), F32), pltpu.VMEM((t, 128), F32),
                        pltpu.SMEM((2,), F32)],
        compiler_params=_params(("arbitrary", "arbitrary")))(q_all, kv_all, kv_all, qr, kr)


def _mla_bwd(q_all, kv_all, qr, kr, o, do, lse, scale, name):
    M = q_all.shape[0]
    t = ATTN_BWD_TILE

    def body(q_ref, kv_hbm, qr_ref, kr_hbm, o_ref, do_ref, lse_ref,
             dq_ref, dk_hbm, dv_hbm, dqr_ref, dkr_hbm,
             k_ref, v_ref, kr_ref, dk_ref, dv_ref, dkr_ref, dq_s, lse_s, delta_s):
        j = pl.program_id(0)
        i = pl.program_id(1)
        first, lane = _head_masks()
        every = pl.ds(0, M)
        kcols = pl.ds(pl.multiple_of(j * 128, 128), 128)
        vcols = pl.ds(pl.multiple_of((N_PAIRS + j) * 128, 128), 128)

        @pl.when(i == 0)
        def _():
            pltpu.sync_copy(kv_hbm.at[every, kcols], k_ref)
            pltpu.sync_copy(kv_hbm.at[every, vcols], v_ref)
            pltpu.sync_copy(kr_hbm, kr_ref)
            dk_ref[...] = jnp.zeros_like(dk_ref)
            dv_ref[...] = jnp.zeros_like(dv_ref)
            dkr_ref[...] = jnp.zeros_like(dkr_ref)

        qs = _split_heads(q_ref[...], first)
        qrs = _split_rope(qr_ref[...], lane)
        qcat = tuple(jnp.concatenate([qs[hh], qrs[hh]], axis=1) for hh in range(2))
        dov = do_ref[...]
        dos = _split_heads(dov, first)
        prod = dov.astype(F32) * o_ref[...].astype(F32)
        deltas = (jnp.sum(jnp.where(first, prod, 0.0), axis=1, keepdims=True),
                  jnp.sum(jnp.where(first, 0.0, prod), axis=1, keepdims=True))
        for hh in range(2):
            lse_s[hh] = jnp.broadcast_to(lse_ref[:, hh:hh + 1], (t, t))
            delta_s[hh] = jnp.broadcast_to(deltas[hh], (t, t))
        dq_s[...] = jnp.zeros_like(dq_s)
        qpos = i * t + lax.broadcasted_iota(jnp.int32, (t, t), 0)
        kidx = lax.broadcasted_iota(jnp.int32, (t, t), 1)

        def step(kb, masked):
            k0 = pl.multiple_of(kb * t, t)
            rows = pl.ds(k0, t)
            k = k_ref[rows, :]
            v = v_ref[rows, :]
            kr = kr_ref[rows, :]
            kcat = jnp.concatenate([k, kr], axis=1)
            ks = _split_heads(k, first)
            krs = _split_rope(kr, lane)
            if masked:
                kpos = k0 + kidx
                valid = (kpos <= qpos) & (kpos >= PAD)
            dq = dk = dv = None
            for hh in range(2):
                s = _dot_nt(qcat[hh], kcat) * scale
                if masked:
                    s = jnp.where(valid, s, NEG)
                p = jnp.exp(s - lse_s[hh])
                ds = p * (_dot_nt(dos[hh], v) - delta_s[hh])
                dsb = (ds * scale).astype(BF16)
                a = _dot(dsb, jnp.concatenate([ks[hh], krs[hh]], axis=1))
                b = _dot_tn(dsb, qcat[hh])
                c = _dot_tn(p.astype(BF16), dos[hh])
                dq = a if dq is None else dq + a
                dk = b if dk is None else dk + b
                dv = c if dv is None else dv + c
            dq_s[...] += dq
            dk_ref[rows, :] += dk[:, :128]
            dkr_ref[rows, :] += dk[:, 128:]
            dv_ref[rows, :] += dv

        _walk_causal(i, step)
        dq_ref[...] = dq_s[:, :128].astype(dq_ref.dtype)
        dqr_ref[...] = dq_s[:, 128:].astype(dqr_ref.dtype)

        @pl.when(i == M // t - 1)
        def _():
            pltpu.sync_copy(dk_ref, dk_hbm.at[every, kcols])
            pltpu.sync_copy(dv_ref, dv_hbm.at[every, kcols])
            pltpu.sync_copy(dkr_ref, dkr_hbm.at[j])

    blk = pl.BlockSpec((t, 128), lambda j, i: (i, j))
    whole = pl.BlockSpec(memory_space=pl.ANY)
    wide = jax.ShapeDtypeStruct((M, N_PAIRS * 128), F32)
    slab = lambda dtype: pltpu.VMEM((M, 128), dtype)
    return pl.pallas_call(
        body, name=name, grid=(N_PAIRS, M // t),
        in_specs=[blk, whole, blk, whole, blk, blk, pl.BlockSpec((None, t, 2), lambda j, i: (j, i, 0))],
        out_specs=[blk, whole, whole, blk, whole],
        out_shape=[jax.ShapeDtypeStruct((M, N_PAIRS * 128), BF16), wide, wide,
                   jax.ShapeDtypeStruct((M, N_PAIRS * 128), BF16), jax.ShapeDtypeStruct((N_PAIRS, M, 128), F32)],
        scratch_shapes=[slab(BF16), slab(BF16), slab(BF16), slab(F32), slab(F32), slab(F32),
                        pltpu.VMEM((t, 256), F32), pltpu.VMEM((2, t, t), F32), pltpu.VMEM((2, t, t), F32)],
        compiler_params=_params(("arbitrary", "arbitrary")))(q_all, kv_all, qr, kr, o, do, lse)


def _tri(t, rel):
    j = lax.broadcasted_iota(jnp.int32, (t, t), 0)
    k = lax.broadcasted_iota(jnp.int32, (t, t), 1)
    m = {"gt": j > k, "le": j <= k, "lt": j < k}[rel]
    return m.astype(BF16)


def _lane_cumsum(x, tri):
    hi = x.astype(BF16)
    lo = (x - hi.astype(F32)).astype(BF16)
    return _dot(hi, tri) + _dot(lo, tri)


def _log_sigmoids(z):
    sp = jnp.log(1.0 + jnp.exp(-jnp.abs(z)))
    return jnp.minimum(z, 0.0) - sp, jnp.minimum(-z, 0.0) - sp


def _log_sigmoids_fast(z):
    lk = -(jnp.maximum(z, 0.0) + jnp.log(1.0 + jnp.exp(-jnp.abs(z))))
    return lk + z, lk


def _sb_fwd(qkv, scale, name):
    M = qkv.shape[0]
    t = WALK_TILE
    ck, cv = N_PAIRS, 2 * N_PAIRS

    def body(q_ref, k_ref, v_ref, o_ref, tot_ref, c_s, acc_s):
        i = pl.program_id(1)
        first, _ = _head_masks()
        qs = _split_heads(q_ref[...], first)
        c_s[...] = jnp.zeros_like(c_s)
        acc_s[...] = jnp.zeros_like(acc_s)
        tri = _tri(t, "gt")
        qpos = i * t + lax.broadcasted_iota(jnp.int32, (t, t), 0)
        kidx = lax.broadcasted_iota(jnp.int32, (t, t), 1)

        def step(kb, masked):
            k0 = pl.multiple_of(kb * t, t)
            k = k_ref[pl.ds(k0, t), :]
            vs = _split_heads(v_ref[pl.ds(k0, t), :], first)
            if masked:
                kpos = k0 + kidx
                valid = (kpos < qpos) & (kpos >= PAD)
            pv = None
            for hh in range(2):
                z = _dot_nt(qs[hh], k) * scale
                lb, lk = _log_sigmoids_fast(z)
                if masked:
                    lk = jnp.where(valid, lk, 0.0)
                a = jnp.exp(lb + (c_s[hh] + _lane_cumsum(lk, tri)))
                if masked:
                    a = jnp.where(valid, a, 0.0)
                c_s[hh] = c_s[hh] + jnp.sum(lk, axis=1, keepdims=True)
                d = _dot(a.astype(BF16), vs[hh])
                pv = d if pv is None else pv + d
            acc_s[...] += pv

        def keep_going():
            return jnp.max(jnp.maximum(c_s[0], c_s[1])) > EXP_ZERO

        def cond(carry):
            kb, go, _ = carry
            return (kb >= 1) & go

        def walk(carry):
            kb, _, n = carry
            step(kb, False)
            return kb - 1, keep_going(), n + 1

        step(i, True)
        _, go, n = lax.while_loop(cond, walk, (i - 1, keep_going(), jnp.int32(1)))
        first_too = go & (i > 0)

        @pl.when(first_too)
        def _():
            step(0, True)

        walked = n + first_too.astype(jnp.int32)
        o_ref[...] = acc_s[...].astype(o_ref.dtype)
        tot_ref[:, 0:1] = c_s[0]
        tot_ref[:, 1:2] = c_s[1]
        tot_ref[:, 2:3] = jnp.full((t, 1), walked.astype(F32))

    whole = lambda c0: pl.BlockSpec((M, 128), lambda j, i: (0, c0 + j))
    return pl.pallas_call(
        body, name=name, grid=(N_PAIRS, M // t),
        in_specs=[pl.BlockSpec((t, 128), lambda j, i: (i, j)), whole(ck), whole(cv)],
        out_specs=[pl.BlockSpec((t, 128), lambda j, i: (i, j)), pl.BlockSpec((None, t, 3), lambda j, i: (j, i, 0))],
        out_shape=[jax.ShapeDtypeStruct((M, N_PAIRS * 128), BF16), jax.ShapeDtypeStruct((N_PAIRS, M, 3), F32)],
        scratch_shapes=[pltpu.VMEM((2, t, 1), F32), pltpu.VMEM((t, 128), F32)],
        compiler_params=_params(("parallel", "arbitrary")))(qkv, qkv, qkv)


def _sb_bwd(qkv, do, tot, scale, name):
    M = qkv.shape[0]
    t = WALK_TILE
    ck, cv = N_PAIRS, 2 * N_PAIRS

    def body(q_ref, k_ref, v_ref, do_ref, tot_ref, dq_ref, dk_ref, dv_ref, pc_s, dc_s, dq_s):
        i = pl.program_id(1)
        first, _ = _head_masks()

        @pl.when(i == 0)
        def _():
            dk_ref[...] = jnp.zeros_like(dk_ref)
            dv_ref[...] = jnp.zeros_like(dv_ref)

        qs = _split_heads(q_ref[...], first)
        dos = _split_heads(do_ref[...], first)
        pc_s[...] = jnp.zeros_like(pc_s)
        dc_s[...] = jnp.zeros_like(dc_s)
        dq_s[...] = jnp.zeros_like(dq_s)
        tri_le = _tri(t, "le")
        tri_lt = _tri(t, "lt")
        qpos = i * t + lax.broadcasted_iota(jnp.int32, (t, t), 0)
        kidx = lax.broadcasted_iota(jnp.int32, (t, t), 1)

        def step(kb, masked):
            k0 = pl.multiple_of(kb * t, t)
            rows = pl.ds(k0, t)
            k = k_ref[rows, :]
            v = v_ref[rows, :]
            ks = _split_heads(k, first)
            if masked:
                kpos = k0 + kidx
                valid = (kpos < qpos) & (kpos >= PAD)
            dq = dk = dv = None
            for hh in range(2):
                z = _dot_nt(qs[hh], k) * scale
                lb, lk = _log_sigmoids_fast(z)
                if masked:
                    lk = jnp.where(valid, lk, 0.0)
                later = tot_ref[:, hh:hh + 1] - (pc_s[hh] + _lane_cumsum(lk, tri_le))
                a = jnp.exp(lb + later)
                if masked:
                    a = jnp.where(valid, a, 0.0)
                dl = a * _dot_nt(dos[hh], v)
                early = dc_s[hh] + _lane_cumsum(dl, tri_lt)
                sg = jnp.exp(lb)
                dz = (dl * (1.0 - sg) - early * sg) * scale
                if masked:
                    dz = jnp.where(valid, dz, 0.0)
                pc_s[hh] = pc_s[hh] + jnp.sum(lk, axis=1, keepdims=True)
                dc_s[hh] = dc_s[hh] + jnp.sum(dl, axis=1, keepdims=True)
                dzb = dz.astype(BF16)
                x = _dot(dzb, ks[hh])
                y = _dot_tn(dzb, qs[hh])
                w = _dot_tn(a.astype(BF16), dos[hh])
                dq = x if dq is None else dq + x
                dk = y if dk is None else dk + y
                dv = w if dv is None else dv + w
            dq_s[...] += dq
            dk_ref[rows, :] += dk
            dv_ref[rows, :] += dv

        first_walked = i + 1 - jnp.max(tot_ref[:, 2:3]).astype(jnp.int32)

        def mid(kb, carry):
            step(kb, False)
            return carry

        @pl.when((first_walked == 0) & (i > 0))
        def _():
            step(0, True)

        lax.fori_loop(jnp.maximum(first_walked, 1), i, mid, 0)
        step(i, True)
        dq_ref[...] = dq_s[...].astype(dq_ref.dtype)

    whole = lambda c0: pl.BlockSpec((M, 128), lambda j, i: (0, c0 + j))
    blk = pl.BlockSpec((t, 128), lambda j, i: (i, j))
    col = pl.BlockSpec((M, 128), lambda j, i: (0, j))
    return pl.pallas_call(
        body, name=name, grid=(N_PAIRS, M // t),
        in_specs=[blk, whole(ck), whole(cv), blk, pl.BlockSpec((None, t, 3), lambda j, i: (j, i, 0))],
        out_specs=[blk, col, col],
        out_shape=[jax.ShapeDtypeStruct((M, N_PAIRS * 128), BF16), jax.ShapeDtypeStruct((M, N_PAIRS * 128), F32),
                   jax.ShapeDtypeStruct((M, N_PAIRS * 128), F32)],
        scratch_shapes=[pltpu.VMEM((2, t, 1), F32), pltpu.VMEM((2, t, 1), F32), pltpu.VMEM((t, 128), F32)],
        compiler_params=_params(("parallel", "arbitrary")))(qkv, qkv, qkv, do, tot)


def _rows_between(lo, hi):
    r = lax.broadcasted_iota(jnp.int32, (128, 1), 0)
    return (r >= lo) & (r < hi)


def _lanes_between(lo, hi):
    c = lax.broadcasted_iota(jnp.int32, (1, 128), 1)
    return (c >= lo) & (c < hi)


def _keep(x, mask):
    return jnp.where(mask, x, jnp.zeros_like(x))


def _valid_mask(i, kb, t):
    kpos = kb * t + lax.broadcasted_iota(jnp.int32, (t, t), 0)
    qpos = i * t + lax.broadcasted_iota(jnp.int32, (t, t), 1)
    return (kpos <= qpos) & (kpos >= PAD)


def _fox_fwd(qkv, qkv_t, f_rows, f_cols, scale, name):
    M = qkv.shape[0]
    t = FOX_TILE
    first_blk = PAD // t
    ck, cv = N_PAIRS, 2 * N_PAIRS

    def body(qt_ref, k_ref, vt_ref, fq_ref, fk_ref, o_ref, lse_ref, ox_ref, m_s, l_s, acc_s, accx_s, kmax_s, walked_s):
        i = pl.program_id(1)

        @pl.when(i == 0)
        def _():
            first = _lanes_between(0, 64)

            def block_max(kb, carry):
                kk = k_ref[pl.ds(pl.multiple_of(kb * t, t), t), :].astype(F32)
                kk = kk * kk
                a = jnp.max(jnp.sum(jnp.where(first, kk, 0.0), axis=1, keepdims=True))
                b = jnp.max(jnp.sum(jnp.where(first, 0.0, kk), axis=1, keepdims=True))
                return jnp.maximum(carry[0], a), jnp.maximum(carry[1], b)

            a, b = lax.fori_loop(0, M // t, block_max, (jnp.float32(0.0), jnp.float32(0.0)))
            kmax_s[0] = a
            kmax_s[1] = b

        qt = qt_ref[...]
        qts = (_keep(qt, _rows_between(0, 64)), _keep(qt, _rows_between(64, 128)))
        qf = qt.astype(F32)
        qf = qf * qf
        qbound = tuple(
            (1.001 * scale) * jnp.sqrt(jnp.sum(qf[HEAD_DIM * hh:HEAD_DIM * (hh + 1)], axis=0, keepdims=True) * kmax_s[hh])
            for hh in range(2))
        def run(online):
            l_s[...] = jnp.zeros_like(l_s)
            acc_s[...] = jnp.zeros_like(acc_s)
            accx_s[...] = jnp.zeros_like(accx_s)
            if online:
                m_s[...] = jnp.full_like(m_s, NEG)

            def step(kb, masked):
                k0 = pl.multiple_of(kb * t, t)
                rows = pl.ds(k0, t)
                k = k_ref[rows, :]
                if masked:
                    valid = _valid_mask(i, kb, t)
                for hh in range(2):
                    s = _dot(k, qts[hh]) * scale + (fq_ref[hh:hh + 1, :] - fk_ref[rows, hh:hh + 1])
                    hr = slice(HEAD_DIM * hh, HEAD_DIM * (hh + 1))
                    vt = vt_ref[hr, rows]
                    if online:
                        if masked:
                            s = jnp.where(valid, s, NEG)
                        m_old = m_s[hh]
                        m_new = jnp.maximum(m_old, jnp.max(s, axis=0, keepdims=True))
                        p = jnp.exp(s - m_new)
                        alpha = jnp.exp(m_old - m_new)
                        l_s[hh] = alpha * l_s[hh] + jnp.sum(p, axis=0, keepdims=True)
                        m_s[hh] = m_new
                        pb = p.astype(BF16)
                        acc_s[hr, :] = acc_s[hr, :] * alpha + _dot(vt, pb)
                        accx_s[hr, :] = accx_s[hr, :] * alpha + _dot(vt, (p - pb.astype(F32)).astype(BF16))
                    else:
                        p = jnp.exp(s - m_s[hh])
                        if masked:
                            p = jnp.where(valid, p, 0.0)
                        l_s[hh] = l_s[hh] + jnp.sum(p, axis=0, keepdims=True)
                        pb = p.astype(BF16)
                        acc_s[hr, :] += _dot(vt, pb)
                        accx_s[hr, :] += _dot(vt, (p - pb.astype(F32)).astype(BF16))

            def keep_going(kb):
                k0 = pl.multiple_of(kb * t, t)
                worst = None
                for hh in range(2):
                    f0 = jnp.max(fk_ref[pl.ds(k0, 8), hh:hh + 1])
                    decay = fq_ref[hh:hh + 1, :] - f0
                    if online:
                        w = jnp.max(qbound[hh] + decay - m_s[hh])
                    else:
                        w = jnp.max(decay - jnp.minimum(jnp.log(jnp.maximum(l_s[hh], 1e-37)), 0.0))
                    worst = w if worst is None else jnp.maximum(worst, w)
                return worst > EXP_ZERO

            def cond(carry):
                kb, go, _ = carry
                return (kb > first_blk) & go

            def walk(carry):
                kb, _, n = carry
                step(kb, False)
                return kb - 1, keep_going(kb), n + 1

            step(i, True)
            _, go, n = lax.while_loop(cond, walk, (i - 1, keep_going(i), jnp.int32(1)))
            first_too = go & (i > first_blk)

            @pl.when(first_too)
            def _():
                step(first_blk, True)

            walked_s[0] = n + first_too.astype(jnp.int32)

        for hh in range(2):
            m_s[hh] = qbound[hh]
        run(False)
        real = i * t + lax.broadcasted_iota(jnp.int32, (1, t), 1) >= PAD
        underflow = jnp.max(jnp.where(real & (jnp.minimum(l_s[0], l_s[1]) < 1e-30), 1.0, 0.0)) > 0.0

        @pl.when(underflow)
        def _():
            run(True)

        outs = []
        for hh in range(2):
            hr = slice(HEAD_DIM * hh, HEAD_DIM * (hh + 1))
            l = jnp.where(l_s[hh] > 0.0, l_s[hh], 1.0)
            inv = 1.0 / l
            outs.append(acc_s[hr, :] * inv)
            ox_ref[hr, :] = (acc_s[hr, :] + accx_s[hr, :]) * inv
            lse_ref[hh:hh + 1, :] = m_s[hh] + jnp.log(l)
        o_ref[...] = jnp.concatenate(outs, axis=0).T.astype(o_ref.dtype)
        lse_ref[2:3, :] = jnp.full((1, t), walked_s[0].astype(F32))

    blk = pl.BlockSpec((128, t), lambda j, i: (j, i))
    stat = pl.BlockSpec((None, 2, t), lambda j, i: (j, 0, i))
    return pl.pallas_call(
        body, name=name, grid=(N_PAIRS, M // t),
        in_specs=[blk, pl.BlockSpec((M, 128), lambda j, i: (0, ck + j)), pl.BlockSpec((128, M), lambda j, i: (cv + j, 0)),
                  stat, pl.BlockSpec((None, M, 2), lambda j, i: (j, 0, 0))],
        out_specs=[pl.BlockSpec((t, 128), lambda j, i: (i, j)), pl.BlockSpec((None, 3, t), lambda j, i: (j, 0, i)), blk],
        out_shape=[jax.ShapeDtypeStruct((M, N_PAIRS * 128), BF16), jax.ShapeDtypeStruct((N_PAIRS, 3, M), F32),
                   jax.ShapeDtypeStruct((N_PAIRS * 128, M), F32)],
        scratch_shapes=[pltpu.VMEM((2, 1, t), F32), pltpu.VMEM((2, 1, t), F32), pltpu.VMEM((128, t), F32),
                        pltpu.VMEM((128, t), F32), pltpu.SMEM((2,), F32), pltpu.SMEM((1,), jnp.int32)],
        compiler_params=_params(("arbitrary", "arbitrary")))(qkv_t, qkv, qkv_t, f_rows, f_cols)


def _fox_bwd(qkv, qkv_t, o_t, do, do_t, lse, f_rows, f_cols, scale, name):
    M = qkv.shape[0]
    t = FOX_TILE
    first_blk = PAD // t
    ck, cv = N_PAIRS, 2 * N_PAIRS

    def body(q_ref, qt_ref, k_ref, kt_ref, v_ref, ot_ref, do_ref, dot_ref, lse_ref, fq_ref, fk_ref,
             dq_ref, dk_ref, dv_ref, cs_ref, dq_s):
        i = pl.program_id(1)

        @pl.when(i == 0)
        def _():
            dk_ref[...] = jnp.zeros_like(dk_ref)
            dv_ref[...] = jnp.zeros_like(dv_ref)
            cs_ref[...] = jnp.zeros_like(cs_ref)

        heads_l = (_lanes_between(0, 64), _lanes_between(64, 128))
        heads_r = (_rows_between(0, 64), _rows_between(64, 128))
        q = q_ref[...]
        qt = qt_ref[...]
        do = do_ref[...]
        dot = dot_ref[...]
        qs = tuple(_keep(q, m) for m in heads_l)
        qts = tuple(_keep(qt, m) for m in heads_r)
        dos = tuple(_keep(do, m) for m in heads_l)
        dots = tuple(_keep(dot, m) for m in heads_r)
        prod = dot.astype(F32) * ot_ref[...]
        deltas = tuple(jnp.sum(prod[HEAD_DIM * hh:HEAD_DIM * (hh + 1)], axis=0, keepdims=True) for hh in range(2))
        ones = tuple(m.astype(BF16) * jnp.ones((t, 128), BF16) for m in heads_l)
        dq_s[...] = jnp.zeros_like(dq_s)

        def step(kb, masked):
            k0 = pl.multiple_of(kb * t, t)
            rows = pl.ds(k0, t)
            k = k_ref[rows, :]
            v = v_ref[rows, :]
            if masked:
                valid = _valid_mask(i, kb, t)
            dk = dv = cs = None
            for hh in range(2):
                s = _dot(k, qts[hh]) * scale + (fq_ref[hh:hh + 1, :] - fk_ref[rows, hh:hh + 1])
                if masked:
                    s = jnp.where(valid, s, NEG)
                p = jnp.exp(s - lse_ref[hh:hh + 1, :])
                ds = p * (_dot(v, dots[hh]) - deltas[hh])
                hi = ds.astype(BF16)
                lo = (ds - hi.astype(F32)).astype(BF16)
                c = _dot(hi, ones[hh]) + _dot(lo, ones[hh])
                dsb = (ds * scale).astype(BF16)
                hr = slice(HEAD_DIM * hh, HEAD_DIM * (hh + 1))
                dq_s[hr, :] += _dot(kt_ref[hr, rows], dsb)
                a = _dot(dsb, qs[hh])
                b = _dot(p.astype(BF16), dos[hh])
                dk = a if dk is None else dk + a
                dv = b if dv is None else dv + b
                cs = c if cs is None else cs + c
            dk_ref[rows, :] += dk
            dv_ref[rows, :] += dv
            cs_ref[rows, :] += cs

        first_walked = i + 1 - jnp.max(lse_ref[2:3, :]).astype(jnp.int32)

        def mid(kb, carry):
            step(kb, False)
            return carry

        @pl.when((first_walked == first_blk) & (i > first_blk))
        def _():
            step(first_blk, True)

        lax.fori_loop(jnp.maximum(first_walked, first_blk + 1), i, mid, 0)
        step(i, True)
        dq_ref[...] = dq_s[...].T.astype(dq_ref.dtype)

    rblk = pl.BlockSpec((t, 128), lambda j, i: (i, j))
    tblk = pl.BlockSpec((128, t), lambda j, i: (j, i))
    stat = pl.BlockSpec((None, 2, t), lambda j, i: (j, 0, i))
    stat3 = pl.BlockSpec((None, 3, t), lambda j, i: (j, 0, i))
    col = pl.BlockSpec((M, 128), lambda j, i: (0, j))
    wide = jax.ShapeDtypeStruct((M, N_PAIRS * 128), F32)
    return pl.pallas_call(
        body, name=name, grid=(N_PAIRS, M // t),
        in_specs=[rblk, tblk, pl.BlockSpec((M, 128), lambda j, i: (0, ck + j)),
                  pl.BlockSpec((128, M), lambda j, i: (ck + j, 0)), pl.BlockSpec((M, 128), lambda j, i: (0, cv + j)),
                  tblk, rblk, tblk, stat3, stat, pl.BlockSpec((None, M, 2), lambda j, i: (j, 0, 0))],
        out_specs=[rblk, col, col, pl.BlockSpec((None, M, 128), lambda j, i: (j, 0, 0))],
        out_shape=[jax.ShapeDtypeStruct((M, N_PAIRS * 128), BF16), wide, wide,
                   jax.ShapeDtypeStruct((N_PAIRS, M, 128), F32)],
        scratch_shapes=[pltpu.VMEM((128, t), F32)],
        compiler_params=_params(("parallel", "arbitrary")))(qkv, qkv_t, qkv, qkv_t, qkv, o_t, do, do_t, lse,
                                                            f_rows, f_cols)


def _rope_tables(M):
    pos = (jnp.arange(M, dtype=jnp.int32) - PAD).astype(F32)
    inv = ROPE_THETA ** (-jnp.arange(0, MLA_ROPE, 2, dtype=F32) / MLA_ROPE)
    ang = pos[:, None] * inv[None, :]
    cos, sin = jnp.cos(ang), jnp.sin(ang)
    z = jnp.zeros((M, 64), F32)
    cos_t = jnp.concatenate([cos, cos, cos, cos, z], axis=1)
    sin_t = jnp.concatenate([-sin, sin, -sin, sin, z], axis=1)
    return cos_t, sin_t


def _rope(x, cos_t, sin_t, out_dtype, name, inverse=False, lead=0):
    M, C = x.shape
    tm = _pick(M, (768, 512, 256, 128))
    nblk = (C - lead) // 128
    sign = -1.0 if inverse else 1.0

    def body(x_ref, c_ref, s_ref, o_ref):
        lane = lax.broadcasted_iota(jnp.int32, (1, 128), 1)
        low = (lane % MLA_ROPE) < (MLA_ROPE // 2)
        cos = c_ref[...]
        sin = s_ref[...] * sign
        if lead:
            o_ref[:, :lead] = x_ref[:, :lead].astype(o_ref.dtype)
        for b in range(nblk):
            cols = slice(lead + b * 128, lead + (b + 1) * 128)
            v = x_ref[:, cols].astype(F32)
            up = pltpu.roll(v, 128 - MLA_ROPE // 2, 1)
            down = pltpu.roll(v, MLA_ROPE // 2, 1)
            o_ref[:, cols] = (v * cos + jnp.where(low, up, down) * sin).astype(o_ref.dtype)

    row = pl.BlockSpec((tm, C), lambda m: (m, 0))
    tab = pl.BlockSpec((tm, 128), lambda m: (m, 0))
    return pl.pallas_call(
        body, name=name, grid=(M // tm,), in_specs=[row, tab, tab], out_specs=row,
        out_shape=jax.ShapeDtypeStruct((M, C), out_dtype),
        compiler_params=_params(("parallel",)))(x, cos_t, sin_t)


def _forget_cumsum(f_logit, bias, name):
    M = f_logit.shape[0]
    tm = 256

    def body(f_ref, b_ref, o_ref, c_s):
        i = pl.program_id(0)

        @pl.when(i == 0)
        def _():
            c_s[...] = jnp.zeros_like(c_s)
        ls, _ = _log_sigmoids(f_ref[...] + b_ref[...])
        rows = i * tm + lax.broadcasted_iota(jnp.int32, (tm, 1), 0)
        ls = jnp.where(rows >= PAD, ls, 0.0)
        r = lax.broadcasted_iota(jnp.int32, (tm, tm), 0)
        c = lax.broadcasted_iota(jnp.int32, (tm, tm), 1)
        tri = (c <= r).astype(F32)
        cum = jnp.dot(tri, ls, precision=lax.Precision.HIGHEST, preferred_element_type=F32) + c_s[...]
        o_ref[...] = cum
        c_s[...] = cum[tm - 1:tm, :]

    row = pl.BlockSpec((tm, 128), lambda m: (m, 0))
    return pl.pallas_call(
        body, name=name, grid=(M // tm,),
        in_specs=[row, pl.BlockSpec((1, 128), lambda m: (0, 0))], out_specs=row,
        out_shape=jax.ShapeDtypeStruct((M, 128), F32), scratch_shapes=[pltpu.VMEM((1, 128), F32)],
        compiler_params=_params(("arbitrary",)))(f_logit, bias)


def _forget_cumsum_bwd(f_logit, bias, colsum, name):
    M = f_logit.shape[0]
    tm = 256
    nb = M // tm

    def body(f_ref, b_ref, cs_ref, o_ref, db_ref, c_s):
        i = pl.program_id(0)

        @pl.when(i == 0)
        def _():
            c_s[...] = jnp.zeros_like(c_s)
            db_ref[...] = jnp.zeros_like(db_ref)
        rr = lax.broadcasted_iota(jnp.int32, (128, 128), 0)
        cc = lax.broadcasted_iota(jnp.int32, (128, 128), 1)
        dF = None
        for j in range(N_PAIRS):
            sel = (((rr == 0) & (cc == 2 * j)) | ((rr == HEAD_DIM) & (cc == 2 * j + 1))).astype(F32)
            d = jnp.dot(cs_ref[j], sel, precision=lax.Precision.HIGHEST, preferred_element_type=F32)
            dF = d if dF is None else dF + d
        r = lax.broadcasted_iota(jnp.int32, (tm, tm), 0)
        c = lax.broadcasted_iota(jnp.int32, (tm, tm), 1)
        tri = (c >= r).astype(F32)
        cum = c_s[...] - jnp.dot(tri, dF, precision=lax.Precision.HIGHEST, preferred_element_type=F32)
        c_s[...] = cum[0:1, :]
        _, lsn = _log_sigmoids(f_ref[...] + b_ref[...])
        rows = (nb - 1 - i) * tm + lax.broadcasted_iota(jnp.int32, (tm, 1), 0)
        dl = jnp.where(rows >= PAD, cum * jnp.exp(lsn), 0.0)
        o_ref[...] = dl
        db_ref[...] += jnp.sum(dl, axis=0, keepdims=True)

    row = pl.BlockSpec((tm, 128), lambda m: (nb - 1 - m, 0))
    vec = pl.BlockSpec((1, 128), lambda m: (0, 0))
    return pl.pallas_call(
        body, name=name, grid=(nb,),
        in_specs=[row, vec, pl.BlockSpec((N_PAIRS, tm, 128), lambda m: (0, nb - 1 - m, 0))], out_specs=[row, vec],
        out_shape=[jax.ShapeDtypeStruct((M, 128), F32), jax.ShapeDtypeStruct((1, 128), F32)],
        scratch_shapes=[pltpu.VMEM((1, 128), F32)],
        compiler_params=_params(("arbitrary",)))(f_logit, bias, colsum)


def _adamw(w, parts, m, v, name):
    R, C = w.shape
    n_parts = parts.shape[0]
    tr = R
    for d in range(8, R, 8):
        if R % d == 0 and d * C <= ADAM_TILE_ELEMS:
            tr = d
    c1 = 1.0 - ADAM_B1 ** ADAM_STEP
    c2 = 1.0 - ADAM_B2 ** ADAM_STEP

    def body(w_ref, s_ref, m_ref, v_ref, g_ref, d_ref, mo_ref, vo_ref):
        g = s_ref[0].astype(F32)
        for k in range(1, n_parts):
            g = g + s_ref[k].astype(F32)
        mn = ADAM_B1 * m_ref[...] + (1.0 - ADAM_B1) * g
        vn = ADAM_B2 * v_ref[...] + (1.0 - ADAM_B2) * (g * g)
        m_hat = mn / c1
        v_hat = vn / c2
        g_ref[...] = g
        d_ref[...] = -ADAM_LR * (m_hat / (jnp.sqrt(v_hat) + ADAM_EPS) + ADAM_WD * w_ref[...])
        mo_ref[...] = mn
        vo_ref[...] = vn

    row = pl.BlockSpec((tr, C), lambda r: (r, 0))
    shp = jax.ShapeDtypeStruct((R, C), F32)
    return pl.pallas_call(
        body, name=name, grid=(R // tr,),
        in_specs=[row, pl.BlockSpec((n_parts, tr, C), lambda r: (0, r, 0)), row, row],
        out_specs=[row, row, row, row], out_shape=[shp, shp, shp, shp],
        compiler_params=_params(("parallel",)))(w, parts, m, v)


def _position():
    return lax.axis_index("x"), lax.axis_index("y"), lax.axis_index("c")


def _all_gather(blocks, name):
    n = len(blocks)

    def body(*refs):
        x_refs, out_refs = refs[:n], refs[n:2 * n]
        send_sems, recv_sems, local_sems = refs[2 * n:]
        x, y, c = _position()
        me, sibling = (x, y, c), (x, y, 1 - c)
        chips = [(1 - x, y), (x, 1 - y), (1 - x, 1 - y)]

        def copies(k, block, to, own=False):
            slot = 4 * block[0] + 2 * block[1] + block[2]
            return [pltpu.make_async_remote_copy(
                src_ref=x_refs[p] if own else out_refs[p].at[slot], dst_ref=out_refs[p].at[slot],
                send_sem=send_sems.at[k, p], recv_sem=recv_sems.at[k, p], device_id=to, device_id_type=MESH)
                for p in range(n)]

        mine = [pltpu.make_async_copy(x_refs[p], out_refs[p].at[4 * x + 2 * y + c], local_sems.at[p]) for p in range(n)]
        for cp in mine:
            cp.start()
        first = copies(0, me, sibling, own=True)
        for j, chip in enumerate(chips):
            first += copies(1 + j, me, (*chip, c), own=True)
        for cp in first:
            cp.start()
        passed = []
        for j, chip in enumerate(chips):
            for cp in copies(1 + j, (*chip, c), me):
                cp.wait_recv()
            onward = copies(4 + j, (*chip, c), sibling)
            for cp in onward:
                cp.start()
            passed += onward
        for cp in copies(0, sibling, me):
            cp.wait_recv()
        for j, chip in enumerate(chips):
            for cp in copies(4 + j, (*chip, 1 - c), me):
                cp.wait_recv()
        for cp in first + passed:
            cp.wait_send()
        for cp in mine:
            cp.wait()

    any_spec = pl.BlockSpec(memory_space=pl.ANY)
    return pl.pallas_call(
        body, name=name, out_shape=[jax.ShapeDtypeStruct((N_DEV,) + b.shape, b.dtype) for b in blocks],
        in_specs=[any_spec] * n, out_specs=[any_spec] * n,
        scratch_shapes=[pltpu.SemaphoreType.DMA((7, n)), pltpu.SemaphoreType.DMA((7, n)), pltpu.SemaphoreType.DMA((n,))],
    )(*blocks)


N_CHIPS = 4


def _exchange_siblings(parts, name):
    n = len(parts)

    def body(*refs):
        g_refs, land_refs = refs[:n], refs[n:2 * n]
        send_sems, recv_sems = refs[2 * n:]
        x, y, c = _position()
        sibling = (x, y, 1 - c)
        sends, recvs = [], []
        for q in range(N_CHIPS):
            for p in range(n):
                sends.append(pltpu.make_async_remote_copy(
                    src_ref=g_refs[p].at[2 * q + (1 - c)], dst_ref=land_refs[p].at[q], send_sem=send_sems.at[q, p],
                    recv_sem=recv_sems.at[q, p], device_id=sibling, device_id_type=MESH))
                recvs.append(pltpu.make_async_remote_copy(
                    src_ref=g_refs[p].at[2 * q + c], dst_ref=land_refs[p].at[q], send_sem=send_sems.at[q, p],
                    recv_sem=recv_sems.at[q, p], device_id=sibling, device_id_type=MESH))
        for cp in sends:
            cp.start()
        for cp in recvs:
            cp.wait_recv()
        for cp in sends:
            cp.wait_send()

    any_spec = pl.BlockSpec(memory_space=pl.ANY)
    return pl.pallas_call(
        body, name=name, out_shape=[jax.ShapeDtypeStruct((N_CHIPS,) + p.shape[1:], p.dtype) for p in parts],
        in_specs=[any_spec] * n, out_specs=[any_spec] * n,
        scratch_shapes=[pltpu.SemaphoreType.DMA((N_CHIPS, n)), pltpu.SemaphoreType.DMA((N_CHIPS, n))],
    )(*parts)


def _pair_sum(part, from_sibling, name):
    _, R, C = part.shape
    tr = R
    for d in range(8, R, 8):
        if R % d == 0 and d * C <= ADAM_TILE_ELEMS:
            tr = d

    def body(a_ref, b_ref, o_ref):
        c = lax.axis_index("c")
        for q in range(N_CHIPS):
            o_ref[q] = (a_ref[2 * q + c].astype(F32) + b_ref[q].astype(F32)).astype(o_ref.dtype)

    return pl.pallas_call(
        body, name=name, grid=(R // tr,),
        in_specs=[pl.BlockSpec((N_DEV, tr, C), lambda r: (0, r, 0)), pl.BlockSpec((N_CHIPS, tr, C), lambda r: (0, r, 0))],
        out_specs=pl.BlockSpec((N_CHIPS, tr, C), lambda r: (0, r, 0)),
        out_shape=jax.ShapeDtypeStruct((N_CHIPS, R, C), part.dtype),
        compiler_params=_params(("parallel",)))(part, from_sibling)


def _exchange_chips(sums, name):
    n = len(sums)

    def body(*refs):
        g_refs, land_refs = refs[:n], refs[n:2 * n]
        send_sems, recv_sems, local_sems = refs[2 * n:]
        x, y, c = _position()
        me = 2 * x + y
        mine = [pltpu.make_async_copy(g_refs[p].at[me], land_refs[p].at[me], local_sems.at[p]) for p in range(n)]
        for cp in mine:
            cp.start()
        sends, recvs = [], []
        for k in range(1, N_CHIPS):
            px = 1 - x if k & 2 else x
            py = 1 - y if k & 1 else y
            peer = 2 * px + py
            for p in range(n):
                sends.append(pltpu.make_async_remote_copy(
                    src_ref=g_refs[p].at[peer], dst_ref=land_refs[p].at[me], send_sem=send_sems.at[k - 1, p],
                    recv_sem=recv_sems.at[k - 1, p], device_id=(px, py, c), device_id_type=MESH))
                recvs.append(pltpu.make_async_remote_copy(
                    src_ref=g_refs[p].at[me], dst_ref=land_refs[p].at[peer], send_sem=send_sems.at[k - 1, p],
                    recv_sem=recv_sems.at[k - 1, p], device_id=(px, py, c), device_id_type=MESH))
        for cp in sends:
            cp.start()
        for cp in recvs:
            cp.wait_recv()
        for cp in sends:
            cp.wait_send()
        for cp in mine:
            cp.wait()

    any_spec = pl.BlockSpec(memory_space=pl.ANY)
    return pl.pallas_call(
        body, name=name, out_shape=[jax.ShapeDtypeStruct(p.shape, p.dtype) for p in sums],
        in_specs=[any_spec] * n, out_specs=[any_spec] * n,
        scratch_shapes=[pltpu.SemaphoreType.DMA((3, n)), pltpu.SemaphoreType.DMA((3, n)), pltpu.SemaphoreType.DMA((n,))],
    )(*sums)


SHARDED = (("sb_w_qkv", 2), ("sb_w_o", 1), ("mla_w_down", 1), ("mla_w_uq", 2), ("mla_w_ukv", 2), ("mla_w_o", 1),
           ("fox_w_qkvf", 2), ("fox_w_o", 1), ("ffn_w_gate", 2), ("ffn_w_up", 2), ("ffn_w_down", 1),
           ("pool_w", 2), ("meta", 1), ("mla_q_norm", 1), ("mla_kv_norm", 1))
KEPT_F32 = ("meta", "mla_q_norm", "mla_kv_norm")
REPLICATED = ("norm_mix", "norm_ffn", "pool_scale", "fox_b_f", "final_norm")
WEIGHT_NAMES = ("meta", "norm_mix", "norm_ffn", "pool_w", "pool_scale", "sb_w_qkv", "sb_w_o", "mla_w_down",
                "mla_q_norm", "mla_kv_norm", "mla_w_uq", "mla_w_ukv", "mla_w_o", "fox_w_qkvf", "fox_b_f",
                "fox_w_o", "ffn_w_gate", "ffn_w_up", "ffn_w_down", "final_norm")
LANES = 1024


def _pack_rows(arrays, names):
    parts = []
    for n in names:
        flat = arrays[n].reshape(-1).astype(F32)
        rows = -(-flat.shape[0] // LANES)
        parts.append(jnp.pad(flat, (0, rows * LANES - flat.shape[0])).reshape(rows, LANES))
    rows = sum(p.shape[0] for p in parts)
    parts.append(jnp.zeros((-(-rows // 8) * 8 - rows, LANES), F32))
    return jnp.concatenate(parts, axis=0)


def _unpack_rows(buf, shapes, names):
    out, row = {}, 0
    for n in names:
        size = int(np.prod(shapes[n]))
        rows = -(-size // LANES)
        out[n] = buf[row:row + rows].reshape(-1)[:size].reshape(shapes[n])
        row += rows
    return out


def _whole_from_gathered(g, axis):
    g = jnp.moveaxis(g, 0, axis)
    shp = g.shape
    return g.reshape(shp[:axis] + (shp[axis] * shp[axis + 1],) + shp[axis + 2:])


def _parts_from_whole(whole, axis):
    shp = whole.shape
    g = whole.reshape(shp[:axis] + (N_DEV, shp[axis] // N_DEV) + shp[axis + 1:])
    return jnp.moveaxis(g, axis, 0)


def _kernel_weights(full):
    W = {}
    W["pool_w"] = full["pool_w"][0]
    W["sb_w_qkv"] = full["sb_w_qkv"][0]
    W["sb_w_o"] = full["sb_w_o"][0]
    W["mla_w_down"] = full["mla_w_down"][0]
    uq = full["mla_w_uq"][0].reshape(MLA_Q_RANK, N_HEADS, MLA_NOPE + MLA_ROPE)
    nope = uq[:, :, :MLA_NOPE].reshape(MLA_Q_RANK, N_HEADS * MLA_NOPE)
    rope = uq[:, :, MLA_NOPE:].reshape(MLA_Q_RANK, N_PAIRS, 2 * MLA_ROPE)
    rope = jnp.pad(rope, ((0, 0), (0, 0), (0, 128 - 2 * MLA_ROPE))).reshape(MLA_Q_RANK, N_PAIRS * 128)
    W["mla_w_uq"] = jnp.concatenate([nope, rope], axis=1)
    ukv = full["mla_w_ukv"][0].reshape(MLA_KV_RANK, N_HEADS, 2, HEAD_DIM)
    W["mla_w_ukv"] = jnp.transpose(ukv, (0, 2, 1, 3)).reshape(MLA_KV_RANK, 2 * N_HEADS * HEAD_DIM)
    W["mla_w_o"] = full["mla_w_o"][0]
    qkvf = full["fox_w_qkvf"][0]
    n_qkv = 3 * N_HEADS * HEAD_DIM
    W["fox_w_qkv"] = qkvf[:, :n_qkv]
    W["fox_w_f"] = jnp.pad(qkvf[:, n_qkv:], ((0, 0), (0, 128 - N_HEADS)))
    W["fox_w_qkvf"] = jnp.concatenate([W["fox_w_qkv"], W["fox_w_f"]], axis=1)
    W["fox_w_o"] = full["fox_w_o"][0]
    W["ffn_w_gate"] = full["ffn_w_gate"]
    W["ffn_w_up"] = full["ffn_w_up"]
    W["ffn_w_down"] = full["ffn_w_down"]
    return W


def _reference_grads(G):
    out = {}
    out["pool_w"] = G["pool_w"][None]
    for n in ("sb_w_qkv", "sb_w_o", "mla_w_down", "mla_w_o", "fox_w_o"):
        out[n] = G[n][None]
    duq = G["mla_w_uq"]
    nope = duq[:, :N_HEADS * MLA_NOPE].reshape(MLA_Q_RANK, N_HEADS, MLA_NOPE)
    rope = duq[:, N_HEADS * MLA_NOPE:].reshape(MLA_Q_RANK, N_PAIRS, 128)[:, :, :2 * MLA_ROPE]
    rope = rope.reshape(MLA_Q_RANK, N_HEADS, MLA_ROPE)
    out["mla_w_uq"] = jnp.concatenate([nope, rope], axis=2).reshape(1, MLA_Q_RANK, -1)
    dukv = G["mla_w_ukv"].reshape(MLA_KV_RANK, 2, N_HEADS, HEAD_DIM)
    out["mla_w_ukv"] = jnp.transpose(dukv, (0, 2, 1, 3)).reshape(1, MLA_KV_RANK, -1)
    out["fox_w_qkvf"] = G["fox_w_qkvf"][None, :, :3 * N_HEADS * HEAD_DIM + N_HEADS]
    out["ffn_w_gate"] = G["ffn_w_gate"]
    out["ffn_w_up"] = G["ffn_w_up"]
    out["ffn_w_down"] = G["ffn_w_down"]
    out["mla_q_norm"] = G["mla_q_norm"]
    out["mla_kv_norm"] = G["mla_kv_norm"]
    return out


def _pairs_col(f16):
    M = f16.shape[0]
    return jnp.transpose(f16.reshape(M, N_PAIRS, 2), (1, 0, 2))


def _pairs_row(f16):
    M = f16.shape[0]
    return jnp.transpose(f16.reshape(M, N_PAIRS, 2), (1, 2, 0))


def _local_step(x, target, W, P):
    S = x.shape[0]
    M = S + ROW0
    G = {}
    gain = lambda name, i: P[name][i][None, :]
    h0 = jnp.concatenate([jnp.zeros((PAD, D_MODEL), F32), P["meta"], x], axis=0)

    def ffn_fwd(h1, i):
        g, u, act, b = _ffn_up(h1, gain("norm_ffn", i), W["ffn_w_gate"][i], W["ffn_w_up"][i], f"ffn{i}_up")
        h2 = _mm_nn(act, W["ffn_w_down"][i], F32, f"ffn{i}_down", res=h1)
        return h2, (h1, b, g, u, act)

    def ffn_bwd(dh2, saved, i):
        h1, b, g, u, act = saved
        dg, du = _ffn_dact(dh2, W["ffn_w_down"][i], g, u, f"ffn{i}_dact")
        G.setdefault("ffn_w_down", {})[i] = _mm_tn(act, dh2, f"ffn{i}_dwd")
        G.setdefault("ffn_w_gate", {})[i] = _mm_tn(b, dg, f"ffn{i}_dwg")
        G.setdefault("ffn_w_up", {})[i] = _mm_tn(b, du, f"ffn{i}_dwu")
        dh1, dgain = _mm_nt_dnorm([(dg, W["ffn_w_gate"][i]), (du, W["ffn_w_up"][i])], h1, gain("norm_ffn", i), dh2,
                                  f"ffn{i}_db_dnorm")
        G.setdefault("norm_ffn", {})[i] = dgain
        return dh1

    a0 = _norm_fwd(h0, gain("norm_mix", 0), F32, "mix0_norm")
    h1_0, pooled = _pool_fwd(h0, a0, W["pool_w"], P["pool_scale"], "pool_fwd")
    h_1, ffn0 = ffn_fwd(h1_0, 0)

    sb_scale = HEAD_DIM ** -0.5
    sb_qkv, a1 = _norm_mm(h_1, gain("norm_mix", 1), W["sb_w_qkv"], BF16, "sb_qkv")
    sb_o, sb_tot = _sb_fwd(sb_qkv, sb_scale, "sb_fwd")
    h1_1 = _mm_nn(sb_o, W["sb_w_o"], F32, "sb_out", res=h_1)
    h_2, ffn1 = ffn_fwd(h1_1, 1)

    mla_scale = (MLA_NOPE + MLA_ROPE) ** -0.5
    cos_t, sin_t = _rope_tables(M)
    down, a2 = _norm_mm(h_2, gain("norm_mix", 2), W["mla_w_down"], F32, "mla_down")
    dq_raw = down[:, :MLA_Q_RANK]
    dkv_raw = down[:, MLA_Q_RANK:MLA_Q_RANK + MLA_KV_RANK]
    kr_raw = down[:, MLA_Q_RANK + MLA_KV_RANK:]
    q_lin, c_q = _norm_mm(dq_raw, P["mla_q_norm"], W["mla_w_uq"], F32, "mla_uq")
    q_all = _rope(q_lin, cos_t, sin_t, BF16, "mla_qrope", lead=D_MODEL)
    kv_all, c_kv = _norm_mm(dkv_raw, P["mla_kv_norm"], W["mla_w_ukv"], BF16, "mla_ukv")
    kr_in = jnp.concatenate([kr_raw, kr_raw, jnp.zeros((M, 64), F32)], axis=1)
    kr = _rope(kr_in, cos_t, sin_t, BF16, "mla_krope")
    q_rope = q_all[:, D_MODEL:]
    mla_o, mla_lse = _mla_fwd(q_all, kv_all, q_rope, kr, mla_scale, "mla_fwd")
    h1_2 = _mm_nn(mla_o, W["mla_w_o"], F32, "mla_out", res=h_2)
    h_3, ffn2 = ffn_fwd(h1_2, 2)

    fox_scale = HEAD_DIM ** -0.5
    fox_qkv, a3, fox_qkv_t = _norm_mm(h_3, gain("norm_mix", 3), W["fox_w_qkv"], BF16, "fox_qkv", transposed_too=True)
    f_logit = _mm_nn(a3, W["fox_w_f"], F32, "fox_f")
    b_f = jnp.pad(P["fox_b_f"], ((0, 0), (0, 128 - N_HEADS)))
    Fc = _forget_cumsum(f_logit, b_f, "fox_cumsum")
    f_rows, f_cols = _pairs_row(Fc[:, :N_HEADS]), _pairs_col(Fc[:, :N_HEADS])
    fox_o, fox_lse, fox_ox_t = _fox_fwd(fox_qkv, fox_qkv_t, f_rows, f_cols, fox_scale, "fox_fwd")
    h1_3 = _mm_nn(fox_o, W["fox_w_o"], F32, "fox_out", res=h_3)
    h_4, ffn3 = ffn_fwd(h1_3, 3)

    sq, dh, dgain = _loss_head(h_4, P["final_norm"][None, :], target, "loss_head")
    G["final_norm"] = dgain[0]

    dh = ffn_bwd(dh, ffn3, 3)
    do, do_t = _mm_nt(dh, W["fox_w_o"], BF16, "fox_do", transposed_too=True)
    G["fox_w_o"] = _mm_tn(fox_o, dh, "fox_dwo")
    dq, dk, dv, colsum = _fox_bwd(fox_qkv, fox_qkv_t, fox_ox_t, do, do_t, fox_lse, f_rows, f_cols, fox_scale,
                                    "fox_bwd")
    dlogit, db_f = _forget_cumsum_bwd(f_logit, b_f, colsum, "fox_dcumsum")
    G["fox_b_f"] = db_f[:, :N_HEADS]
    dproj = jnp.concatenate([dq, dk.astype(BF16), dv.astype(BF16), dlogit.astype(BF16)], axis=1)
    G["fox_w_qkvf"] = _mm_tn(a3, dproj, "fox_dwqkvf")
    dh, dgain = _mm_nt_dnorm([(dproj, W["fox_w_qkvf"])], h_3, gain("norm_mix", 3), dh, "fox_da_dnorm")
    G.setdefault("norm_mix", {})[3] = dgain

    dh = ffn_bwd(dh, ffn2, 2)
    do = _mm_nt(dh, W["mla_w_o"], BF16, "mla_do")
    G["mla_w_o"] = _mm_tn(mla_o, dh, "mla_dwo")
    dq, dk, dv, dqr, dkr = _mla_bwd(q_all, kv_all, q_rope, kr, mla_o, do, mla_lse, mla_scale, "mla_bwd")
    dqr = _rope(dqr, cos_t, sin_t, BF16, "mla_dqrope", inverse=True)
    dq_all = jnp.concatenate([dq, dqr], axis=1)
    dkr_sum = _rope(jnp.sum(dkr, axis=0), cos_t, sin_t, F32, "mla_dkrope", inverse=True)
    dkr_raw = dkr_sum[:, :MLA_ROPE] + dkr_sum[:, MLA_ROPE:2 * MLA_ROPE]
    dkv_all = jnp.concatenate([dk.astype(BF16), dv.astype(BF16)], axis=1)
    G["mla_w_uq"] = _mm_tn(c_q, dq_all, "mla_dwuq")
    G["mla_w_ukv"] = _mm_tn(c_kv, dkv_all, "mla_dwukv")
    ddq_raw, G["mla_q_norm"] = _mm_nt_dnorm([(dq_all, W["mla_w_uq"])], dq_raw, P["mla_q_norm"], None, "mla_dcq_dnorm")
    ddkv_raw, G["mla_kv_norm"] = _mm_nt_dnorm([(dkv_all, W["mla_w_ukv"])], dkv_raw, P["mla_kv_norm"], None,
                                              "mla_dckv_dnorm")
    ddown = jnp.concatenate([ddq_raw, ddkv_raw, dkr_raw], axis=1).astype(BF16)
    G["mla_w_down"] = _mm_tn(a2, ddown, "mla_dwdown")
    dh, dgain = _mm_nt_dnorm([(ddown, W["mla_w_down"])], h_2, gain("norm_mix", 2), dh, "mla_da_dnorm")
    G["norm_mix"][2] = dgain

    dh = ffn_bwd(dh, ffn1, 1)
    do = _mm_nt(dh, W["sb_w_o"], BF16, "sb_do")
    G["sb_w_o"] = _mm_tn(sb_o, dh, "sb_dwo")
    dq, dk, dv = _sb_bwd(sb_qkv, do, sb_tot, sb_scale, "sb_bwd")
    dqkv = jnp.concatenate([dq, dk.astype(BF16), dv.astype(BF16)], axis=1)
    G["sb_w_qkv"] = _mm_tn(a1, dqkv, "sb_dwqkv")
    dh, dgain = _mm_nt_dnorm([(dqkv, W["sb_w_qkv"])], h_1, gain("norm_mix", 1), dh, "sb_da_dnorm")
    G["norm_mix"][1] = dgain

    dh = ffn_bwd(dh, ffn0, 0)
    dpc, G["pool_w"], G["pool_scale"] = _pool_bwd_mix(dh, pooled, W["pool_w"], P["pool_scale"], "pool_dmix")
    da = _pool_bwd_window(dpc, "pool_dwindow")
    dh, dgain, dx = _norm_bwd(h0, gain("norm_mix", 0), da, dh, "mix0_dnorm", token_rows=True)
    G["norm_mix"][0] = dgain

    G["norm_mix"] = jnp.concatenate([G["norm_mix"][i] for i in range(DEPTH)], axis=0)
    G["norm_ffn"] = jnp.concatenate([G["norm_ffn"][i] for i in range(DEPTH)], axis=0)
    G["ffn_w_down"] = jnp.stack([G["ffn_w_down"][i] for i in range(DEPTH)])
    G["ffn_w_gate"] = jnp.stack([G["ffn_w_gate"][i] for i in range(DEPTH)])
    G["ffn_w_up"] = jnp.stack([G["ffn_w_up"][i] for i in range(DEPTH)])
    G["meta"] = dh[PAD:ROW0]
    return sq, dx, G


def kernel(x, meta, norm_mix, norm_ffn, pool_w, pool_scale, sb_w_qkv, sb_w_o, mla_w_down, mla_q_norm, mla_kv_norm, mla_w_uq, mla_w_ukv, mla_w_o, fox_w_qkvf, fox_b_f, fox_w_o, ffn_w_gate, ffn_w_up, ffn_w_down, final_norm, loss_target, m_meta, m_norm_mix, m_norm_ffn, m_pool_w, m_pool_scale, m_sb_w_qkv, m_sb_w_o, m_mla_w_down, m_mla_q_norm, m_mla_kv_norm, m_mla_w_uq, m_mla_w_ukv, m_mla_w_o, m_fox_w_qkvf, m_fox_b_f, m_fox_w_o, m_ffn_w_gate, m_ffn_w_up, m_ffn_w_down, m_final_norm, v_meta, v_norm_mix, v_norm_ffn, v_pool_w, v_pool_scale, v_sb_w_qkv, v_sb_w_o, v_mla_w_down, v_mla_q_norm, v_mla_kv_norm, v_mla_w_uq, v_mla_w_ukv, v_mla_w_o, v_fox_w_qkvf, v_fox_b_f, v_fox_w_o, v_ffn_w_gate, v_ffn_w_up, v_ffn_w_down, v_final_norm):
    w = dict(meta=meta, norm_mix=norm_mix, norm_ffn=norm_ffn, pool_w=pool_w, pool_scale=pool_scale,
             sb_w_qkv=sb_w_qkv, sb_w_o=sb_w_o, mla_w_down=mla_w_down, mla_q_norm=mla_q_norm,
             mla_kv_norm=mla_kv_norm, mla_w_uq=mla_w_uq, mla_w_ukv=mla_w_ukv, mla_w_o=mla_w_o,
             fox_w_qkvf=fox_w_qkvf, fox_b_f=fox_b_f, fox_w_o=fox_w_o, ffn_w_gate=ffn_w_gate, ffn_w_up=ffn_w_up,
             ffn_w_down=ffn_w_down, final_norm=final_norm)
    m = dict(meta=m_meta, norm_mix=m_norm_mix, norm_ffn=m_norm_ffn, pool_w=m_pool_w, pool_scale=m_pool_scale,
             sb_w_qkv=m_sb_w_qkv, sb_w_o=m_sb_w_o, mla_w_down=m_mla_w_down, mla_q_norm=m_mla_q_norm,
             mla_kv_norm=m_mla_kv_norm, mla_w_uq=m_mla_w_uq, mla_w_ukv=m_mla_w_ukv, mla_w_o=m_mla_w_o,
             fox_w_qkvf=m_fox_w_qkvf, fox_b_f=m_fox_b_f, fox_w_o=m_fox_w_o, ffn_w_gate=m_ffn_w_gate,
             ffn_w_up=m_ffn_w_up, ffn_w_down=m_ffn_w_down, final_norm=m_final_norm)
    v = dict(meta=v_meta, norm_mix=v_norm_mix, norm_ffn=v_norm_ffn, pool_w=v_pool_w, pool_scale=v_pool_scale,
             sb_w_qkv=v_sb_w_qkv, sb_w_o=v_sb_w_o, mla_w_down=v_mla_w_down, mla_q_norm=v_mla_q_norm,
             mla_kv_norm=v_mla_kv_norm, mla_w_uq=v_mla_w_uq, mla_w_ukv=v_mla_w_ukv, mla_w_o=v_mla_w_o,
             fox_w_qkvf=v_fox_w_qkvf, fox_b_f=v_fox_b_f, fox_w_o=v_fox_w_o, ffn_w_gate=v_ffn_w_gate,
             ffn_w_up=v_ffn_w_up, ffn_w_down=v_ffn_w_down, final_norm=v_final_norm)

    sh_names = tuple(n for n, _ in SHARDED)
    sh_axis = dict(SHARDED)
    shapes = {n: w[n].shape for n in WEIGHT_NAMES}
    wire = lambda n: F32 if n in KEPT_F32 else BF16

    gathered = _all_gather([w[n].astype(wire(n)) for n in sh_names], "gather_weights")
    full = {n: _whole_from_gathered(g, sh_axis[n]) for n, g in zip(sh_names, gathered)}
    W = _kernel_weights(full)
    P = dict(meta=full["meta"], mla_q_norm=full["mla_q_norm"], mla_kv_norm=full["mla_kv_norm"],
             norm_mix=norm_mix, norm_ffn=norm_ffn, pool_scale=pool_scale, fox_b_f=fox_b_f, final_norm=final_norm)

    sq, dx, G = _local_step(x[0], loss_target[0], W, P)
    loss = lax.psum(0.5 * jnp.sum(sq) / D_MODEL, ("x", "y", "c"))
    grad_x = dx[None]

    gw = _reference_grads(G)
    gw["meta"] = G["meta"]
    rc = {n: (int(np.prod(shapes[n][:-1])), shapes[n][-1]) for n in sh_names}
    parts = [_parts_from_whole(gw[n], sh_axis[n]).astype(wire(n)).reshape((N_DEV,) + rc[n]) for n in sh_names]
    from_sibling = _exchange_siblings(parts, "exchange_grads_d2d")
    sums = [_pair_sum(a, b, f"pair_sum_{n}") for n, a, b in zip(sh_names, parts, from_sibling)]
    landed = _exchange_chips(sums, "exchange_grads_ici")
    results = {}
    for n, got in zip(sh_names, landed):
        outs = _adamw(w[n].reshape(rc[n]), got, m[n].reshape(rc[n]), v[n].reshape(rc[n]), f"adamw_{n}")
        results[n] = [o.reshape(shapes[n]) for o in outs]

    rep_g = dict(norm_mix=G["norm_mix"], norm_ffn=G["norm_ffn"], pool_scale=G["pool_scale"], fox_b_f=G["fox_b_f"],
                 final_norm=G["final_norm"])
    (rep_all,) = _all_gather([_pack_rows(rep_g, REPLICATED)], "gather_replicated_grads")
    rep_out = _adamw(_pack_rows(w, REPLICATED), rep_all, _pack_rows(m, REPLICATED), _pack_rows(v, REPLICATED),
                     "adamw_replicated")
    rep = [_unpack_rows(o, shapes, REPLICATED) for o in rep_out]
    for n in REPLICATED:
        results[n] = [r[n] for r in rep]

    outs = [results[n][k] for k in range(4) for n in WEIGHT_NAMES]
    return (loss, grad_x, *outs)
```

```python
import numpy as np
import jax
import jax.numpy as jnp
from jax import lax
from jax.experimental import pallas as pl
from jax.experimental.pallas import tpu as pltpu

F32 = jnp.float32
BF16 = jnp.bfloat16

N_DEV = 8
D_MODEL = 1024
N_META = 16
PAD = 240
ROW0 = PAD + N_META
EPS = 1e-6
POOL_WINDOWS = (2, 4, 8, 16)
POOL_GROUP = 256
HALO = 128
N_HEADS = 16
HEAD_DIM = 64
N_PAIRS = N_HEADS // 2
MLA_Q_RANK = 384
MLA_KV_RANK = 256
MLA_NOPE = 64
MLA_ROPE = 32
ROPE_THETA = 10000.0
D_FF = 2816
DEPTH = 4
ATTN_TILE = 768
ATTN_BWD_TILE = 768
WALK_TILE = 256
FOX_TILE = 384
NEG = -1e30
LOG2E = 1.4426950408889634
EXP_ZERO = -110.0
VMEM_LIMIT = 56 * 2**20
ADAM_TILE_ELEMS = 192 * 1024

ADAM_LR = 0.001
ADAM_B1 = 0.9
ADAM_B2 = 0.999
ADAM_EPS = 1e-08
ADAM_WD = 0.01
ADAM_STEP = 10

MESH = pl.DeviceIdType.MESH


def _params(sem=None):
    return pltpu.CompilerParams(dimension_semantics=sem, vmem_limit_bytes=VMEM_LIMIT)


def _pick(n, cands):
    for c in cands:
        if n % c == 0:
            return c
    return n


def _col_tile(n, cap=1536):
    best = None
    for t in range(128, min(n, cap) + 1, 128):
        if n % t == 0:
            best = t
    return best if best is not None else n


def _dot(a, b):
    return jnp.dot(a, b, preferred_element_type=F32)


def _dot_nt(a, b):
    return lax.dot_general(a, b, (((1,), (1,)), ((), ())), preferred_element_type=F32)


def _dot_tn(a, b):
    return lax.dot_general(a, b, (((0,), (0,)), ((), ())), preferred_element_type=F32)


def _mm_nn(a, b, out_dtype, name, res=None):
    M, K = a.shape
    N = b.shape[1]
    tm = _pick(M, (768, 512, 256, 128))
    tn = _col_tile(N)

    def body(*refs):
        if res is None:
            a_ref, b_ref, o_ref = refs
        else:
            a_ref, b_ref, r_ref, o_ref = refs
        acc = _dot(a_ref[...].astype(BF16), b_ref[...])
        if res is not None:
            acc = acc + r_ref[...]
        o_ref[...] = acc.astype(o_ref.dtype)

    in_specs = [pl.BlockSpec((tm, K), lambda n, m: (m, 0)), pl.BlockSpec((K, tn), lambda n, m: (0, n))]
    args = [a, b]
    if res is not None:
        in_specs.append(pl.BlockSpec((tm, tn), lambda n, m: (m, n)))
        args.append(res)
    return pl.pallas_call(
        body, name=name, grid=(N // tn, M // tm), in_specs=in_specs,
        out_specs=pl.BlockSpec((tm, tn), lambda n, m: (m, n)),
        out_shape=jax.ShapeDtypeStruct((M, N), out_dtype),
        compiler_params=_params(("parallel", "parallel")))(*args)


def _norm_mm(h, gain, b, out_dtype, name, transposed_too=False):
    M, K = h.shape
    N = b.shape[1]
    tm = _pick(M, (384, 256, 128))

    def body(h_ref, g_ref, b_ref, o_ref, a_ref, *ot_ref):
        x = h_ref[...]
        r = lax.rsqrt(jnp.mean(x * x, axis=-1, keepdims=True) + EPS)
        a = ((x * r) * g_ref[...]).astype(BF16)
        a_ref[...] = a
        acc = _dot(a, b_ref[...])
        o_ref[...] = acc.astype(o_ref.dtype)
        if transposed_too:
            ot_ref[0][...] = acc.T.astype(o_ref.dtype)

    out_specs = [pl.BlockSpec((tm, N), lambda m: (m, 0)), pl.BlockSpec((tm, K), lambda m: (m, 0))]
    out_shape = [jax.ShapeDtypeStruct((M, N), out_dtype), jax.ShapeDtypeStruct((M, K), BF16)]
    if transposed_too:
        out_specs.append(pl.BlockSpec((N, tm), lambda m: (0, m)))
        out_shape.append(jax.ShapeDtypeStruct((N, M), out_dtype))
    return pl.pallas_call(
        body, name=name, grid=(M // tm,),
        in_specs=[pl.BlockSpec((tm, K), lambda m: (m, 0)), pl.BlockSpec((1, K), lambda m: (0, 0)),
                  pl.BlockSpec((K, N), lambda m: (0, 0))],
        out_specs=out_specs, out_shape=out_shape,
        compiler_params=_params(("parallel",)))(h, gain, b)


def _mm_nt(a, w, out_dtype, name, transposed_too=False):
    M, N = a.shape
    K = w.shape[0]
    tm = _pick(M, (768, 512, 256, 128)) if N <= 3200 else _pick(M, (256, 128))
    tk = _col_tile(K, 1024)

    def body(a_ref, w_ref, o_ref, *ot_ref):
        acc = _dot_nt(a_ref[...].astype(BF16), w_ref[...])
        o_ref[...] = acc.astype(o_ref.dtype)
        if transposed_too:
            ot_ref[0][...] = acc.T.astype(o_ref.dtype)

    out_specs = [pl.BlockSpec((tm, tk), lambda k, m: (m, k))]
    out_shape = [jax.ShapeDtypeStruct((M, K), out_dtype)]
    if transposed_too:
        out_specs.append(pl.BlockSpec((tk, tm), lambda k, m: (k, m)))
        out_shape.append(jax.ShapeDtypeStruct((K, M), out_dtype))
    out = pl.pallas_call(
        body, name=name, grid=(K // tk, M // tm),
        in_specs=[pl.BlockSpec((tm, N), lambda k, m: (m, 0)), pl.BlockSpec((tk, N), lambda k, m: (k, 0))],
        out_specs=out_specs, out_shape=out_shape,
        compiler_params=_params(("parallel", "parallel")))(a, w)
    return out if transposed_too else out[0]


def _mm_nt_dnorm(pairs, h, gain, dres, name):
    M, K = h.shape
    n = len(pairs)
    tm = _pick(M, (384, 256, 128))

    def body(*refs):
        refs = list(refs)
        ab, rest = refs[:2 * n], refs[2 * n:]
        h_ref, g_ref = rest[0], rest[1]
        dr_ref = rest[2] if dres is not None else None
        dh_ref, dg_ref = rest[-2], rest[-1]
        da = None
        for i in range(n):
            d = _dot_nt(ab[2 * i][...], ab[2 * i + 1][...])
            da = d if da is None else da + d
        x = h_ref[...]
        r = lax.rsqrt(jnp.mean(x * x, axis=-1, keepdims=True) + EPS)
        y = x * r
        dy = da * g_ref[...]
        dh = r * (dy - y * jnp.mean(dy * y, axis=-1, keepdims=True))
        if dr_ref is not None:
            dh = dh + dr_ref[...]
        dh_ref[...] = dh

        @pl.when(pl.program_id(0) == 0)
        def _():
            dg_ref[...] = jnp.zeros_like(dg_ref)
        dg_ref[...] += jnp.sum(da * y, axis=0, keepdims=True)

    row = pl.BlockSpec((tm, K), lambda m: (m, 0))
    vec = pl.BlockSpec((1, K), lambda m: (0, 0))
    in_specs, args = [], []
    for a, w in pairs:
        in_specs += [pl.BlockSpec((tm, a.shape[1]), lambda m: (m, 0)), pl.BlockSpec(w.shape, lambda m: (0, 0))]
        args += [a, w]
    in_specs += [row, vec] + ([row] if dres is not None else [])
    args += [h, gain] + ([dres] if dres is not None else [])
    return pl.pallas_call(
        body, name=name, grid=(M // tm,), in_specs=in_specs, out_specs=[row, vec],
        out_shape=[jax.ShapeDtypeStruct((M, K), F32), jax.ShapeDtypeStruct((1, K), F32)],
        compiler_params=_params(("arbitrary",)))(*args)


def _mm_tn(a, b, name):
    M, K = a.shape
    N = b.shape[1]
    tm = _pick(M, (768, 512, 256, 128))
    tk = _col_tile(K, 1408)
    tn = _col_tile(N, 1408)

    def body(a_ref, b_ref, o_ref):
        @pl.when(pl.program_id(2) == 0)
        def _():
            o_ref[...] = jnp.zeros_like(o_ref)
        o_ref[...] += _dot_tn(a_ref[...].astype(BF16), b_ref[...].astype(BF16))

    return pl.pallas_call(
        body, name=name, grid=(K // tk, N // tn, M // tm),
        in_specs=[pl.BlockSpec((tm, tk), lambda k, n, m: (m, k)), pl.BlockSpec((tm, tn), lambda k, n, m: (m, n))],
        out_specs=pl.BlockSpec((tk, tn), lambda k, n, m: (k, n)),
        out_shape=jax.ShapeDtypeStruct((K, N), F32),
        compiler_params=_params(("parallel", "parallel", "arbitrary")))(a, b)


def _norm_fwd(h, gain, out_dtype, name):
    M, C = h.shape
    tm = _pick(M, (768, 512, 256, 128))

    def body(h_ref, g_ref, a_ref):
        x = h_ref[...]
        r = lax.rsqrt(jnp.mean(x * x, axis=-1, keepdims=True) + EPS)
        a_ref[...] = ((x * r) * g_ref[...]).astype(a_ref.dtype)

    return pl.pallas_call(
        body, name=name, grid=(M // tm,),
        in_specs=[pl.BlockSpec((tm, C), lambda m: (m, 0)), pl.BlockSpec((1, C), lambda m: (0, 0))],
        out_specs=pl.BlockSpec((tm, C), lambda m: (m, 0)),
        out_shape=jax.ShapeDtypeStruct((M, C), out_dtype),
        compiler_params=_params(("parallel",)))(h, gain)


def _norm_bwd(h, gain, da, dres, name, token_rows=False):
    M, C = h.shape
    tm = ROW0 if token_rows else _pick(M, (768, 512, 256, 128))

    def body(*refs):
        refs = list(refs)
        dx_ref = refs.pop() if token_rows else None
        if dres is None:
            h_ref, g_ref, da_ref, dh_ref, dg_ref = refs
        else:
            h_ref, g_ref, da_ref, dr_ref, dh_ref, dg_ref = refs
        x = h_ref[...]
        r = lax.rsqrt(jnp.mean(x * x, axis=-1, keepdims=True) + EPS)
        y = x * r
        dav = da_ref[...].astype(F32)
        dy = dav * g_ref[...]
        dh = r * (dy - y * jnp.mean(dy * y, axis=-1, keepdims=True))
        if dres is not None:
            dh = dh + dr_ref[...]
        dh_ref[...] = dh
        if token_rows:
            dx_ref[...] = dh

        @pl.when(pl.program_id(0) == 0)
        def _():
            dg_ref[...] = jnp.zeros_like(dg_ref)
        dg_ref[...] += jnp.sum(dav * y, axis=0, keepdims=True)

    row = pl.BlockSpec((tm, C), lambda m: (m, 0))
    vec = pl.BlockSpec((1, C), lambda m: (0, 0))
    in_specs = [row, vec, row] + ([row] if dres is not None else [])
    args = [h, gain, da] + ([dres] if dres is not None else [])
    out_specs = [row, vec]
    out_shape = [jax.ShapeDtypeStruct((M, C), F32), jax.ShapeDtypeStruct((1, C), F32)]
    if token_rows:
        out_specs.append(pl.BlockSpec((tm, C), lambda m: (jnp.maximum(m - 1, 0), 0)))
        out_shape.append(jax.ShapeDtypeStruct((M - ROW0, C), F32))
    return pl.pallas_call(
        body, name=name, grid=(M // tm,), in_specs=in_specs, out_specs=out_specs, out_shape=out_shape,
        compiler_params=_params(("arbitrary",)))(*args)


def _ffn_up(h, gain, w_g, w_u, name):
    M, K = h.shape
    F = w_g.shape[1]
    tm = _pick(M, (384, 256, 128))

    def body(h_ref, gn_ref, wg_ref, wu_ref, g_ref, u_ref, act_ref, b_ref):
        x = h_ref[...]
        r = lax.rsqrt(jnp.mean(x * x, axis=-1, keepdims=True) + EPS)
        b = ((x * r) * gn_ref[...]).astype(BF16)
        b_ref[...] = b
        g = _dot(b, wg_ref[...])
        u = _dot(b, wu_ref[...])
        g_ref[...] = g.astype(g_ref.dtype)
        u_ref[...] = u.astype(u_ref.dtype)
        act_ref[...] = ((g * jax.nn.sigmoid(g)) * u).astype(act_ref.dtype)

    blk = pl.BlockSpec((tm, F), lambda m: (m, 0))
    wgt = pl.BlockSpec((K, F), lambda m: (0, 0))
    wide = jax.ShapeDtypeStruct((M, F), BF16)
    return pl.pallas_call(
        body, name=name, grid=(M // tm,),
        in_specs=[pl.BlockSpec((tm, K), lambda m: (m, 0)), pl.BlockSpec((1, K), lambda m: (0, 0)), wgt, wgt],
        out_specs=[blk, blk, blk, pl.BlockSpec((tm, K), lambda m: (m, 0))],
        out_shape=[wide, wide, wide, jax.ShapeDtypeStruct((M, K), BF16)],
        compiler_params=_params(("parallel",)))(h, gain, w_g, w_u)


def _ffn_dact(dy, w_d, g, u, name):
    M, K = dy.shape
    F = w_d.shape[0]
    tm = _pick(M, (768, 512, 256, 128))
    tn = _col_tile(F, 1408)
    nb = F // tn

    def body(dy_ref, wd_ref, g_ref, u_ref, dg_ref, du_ref):
        dact = _dot_nt(dy_ref[...].astype(BF16), wd_ref[...])
        gv = g_ref[...].astype(F32)
        s = jax.nn.sigmoid(gv)
        silu = gv * s
        dg_ref[...] = (dact * u_ref[...].astype(F32) * (s * (1.0 + gv * (1.0 - s)))).astype(dg_ref.dtype)
        du_ref[...] = (dact * silu).astype(du_ref.dtype)

    blk = pl.BlockSpec((tm, tn), lambda n, m: (m, n))
    return pl.pallas_call(
        body, name=name, grid=(nb, M // tm),
        in_specs=[pl.BlockSpec((tm, K), lambda n, m: (m, 0)), pl.BlockSpec((tn, K), lambda n, m: (n, 0)), blk, blk],
        out_specs=[blk, blk],
        out_shape=[jax.ShapeDtypeStruct((M, F), BF16), jax.ShapeDtypeStruct((M, F), BF16)],
        compiler_params=_params(("parallel", "parallel")))(dy, w_d, g, u)


def _loss_head(h, gain, target, name):
    M, C = h.shape
    tm = ROW0
    assert M % tm == 0 and target.shape[0] == M - ROW0

    def body(h_ref, g_ref, t_ref, sq_ref, dh_ref, dg_ref):
        i = pl.program_id(0)

        @pl.when(i == 0)
        def _():
            sq_ref[...] = jnp.zeros_like(sq_ref)
            dg_ref[...] = jnp.zeros_like(dg_ref)
            dh_ref[...] = jnp.zeros_like(dh_ref)

        @pl.when(i > 0)
        def _():
            x = h_ref[...]
            r = lax.rsqrt(jnp.mean(x * x, axis=-1, keepdims=True) + EPS)
            y = x * r
            err = y * g_ref[...] - t_ref[...]
            sq_ref[...] += jnp.sum(err * err, axis=0, keepdims=True)
            da = err * (1.0 / C)
            dy = da * g_ref[...]
            dh_ref[...] = r * (dy - y * jnp.mean(dy * y, axis=-1, keepdims=True))
            dg_ref[...] += jnp.sum(da * y, axis=0, keepdims=True)

    row = pl.BlockSpec((tm, C), lambda m: (m, 0))
    vec = pl.BlockSpec((1, C), lambda m: (0, 0))
    return pl.pallas_call(
        body, name=name, grid=(M // tm,),
        in_specs=[row, vec, pl.BlockSpec((tm, C), lambda m: (jnp.maximum(m - 1, 0), 0))],
        out_specs=[vec, row, vec],
        out_shape=[jax.ShapeDtypeStruct((1, C), F32), jax.ShapeDtypeStruct((M, C), F32),
                   jax.ShapeDtypeStruct((1, C), F32)],
        compiler_params=_params(("arbitrary",)))(h, gain, target)


def _band_dot(band, x):
    hi = x.astype(BF16)
    rest = x - hi.astype(F32)
    mid = rest.astype(BF16)
    lo = (rest - mid.astype(F32)).astype(BF16)
    return _dot(band, hi) + _dot(band, mid) + _dot(band, lo)


def _pool_pos(row0, tm):
    return row0 + lax.broadcasted_iota(jnp.int32, (tm, 1), 0) - PAD


def _pool_fwd(h, a, w, scale, name):
    M, C = a.shape
    tm = 256
    hb = tm // HALO

    def body(h_ref, a_ref, halo_ref, w_ref, s_ref, o_ref, p_ref):
        i = pl.program_id(0)
        row0 = i * tm
        ext = jnp.concatenate([halo_ref[...], a_ref[...]], axis=0)
        src = row0 - HALO + lax.broadcasted_iota(jnp.int32, (tm + HALO, 1), 0)
        ext = jnp.where(src >= PAD, ext, 0.0)
        r = lax.broadcasted_iota(jnp.int32, (tm, tm + HALO), 0)
        c = lax.broadcasted_iota(jnp.int32, (tm, tm + HALO), 1)
        pos = _pool_pos(row0, tm)
        for g, win in enumerate(POOL_WINDOWS):
            band = ((c <= r + HALO) & (c > r + HALO - win)).astype(BF16)
            cols = slice(g * POOL_GROUP, (g + 1) * POOL_GROUP)
            xg = ext[:, cols]
            tot = _band_dot(band, xg)
            cnt = jnp.clip(pos + 1, 1, win).astype(F32)
            pooled = (tot / cnt - xg[HALO:]).astype(BF16)
            p_ref[:, cols] = pooled
            mixed = _dot(pooled, w_ref[g])
            o_ref[:, cols] = h_ref[:, cols] + mixed * s_ref[:, cols]

    row = pl.BlockSpec((tm, C), lambda m: (m, 0))
    return pl.pallas_call(
        body, name=name, grid=(M // tm,),
        in_specs=[row, row, pl.BlockSpec((HALO, C), lambda m: (jnp.maximum(m * hb - 1, 0), 0)),
                  pl.BlockSpec((4, POOL_GROUP, POOL_GROUP), lambda m: (0, 0, 0)),
                  pl.BlockSpec((1, C), lambda m: (0, 0))],
        out_specs=[row, row],
        out_shape=[jax.ShapeDtypeStruct((M, C), F32), jax.ShapeDtypeStruct((M, C), BF16)],
        compiler_params=_params(("parallel",)))(h, a, a, w, scale)


def _pool_bwd_mix(dout, pooled, w, scale, name):
    M, C = dout.shape
    tm = 256

    def body(do_ref, p_ref, w_ref, s_ref, dpc_ref, dw_ref, ds_ref):
        i = pl.program_id(0)

        @pl.when(i == 0)
        def _():
            dw_ref[...] = jnp.zeros_like(dw_ref)
            ds_ref[...] = jnp.zeros_like(ds_ref)

        pos = _pool_pos(i * tm, tm)
        for g, win in enumerate(POOL_WINDOWS):
            cols = slice(g * POOL_GROUP, (g + 1) * POOL_GROUP)
            do = do_ref[:, cols]
            pooled = p_ref[:, cols]
            mixed = _dot(pooled, w_ref[g])
            ds_ref[:, cols] += jnp.sum(do * mixed, axis=0, keepdims=True)
            dmix = (do * s_ref[:, cols]).astype(BF16)
            dw_ref[g] += _dot_tn(pooled, dmix)
            dp = _dot_nt(dmix, w_ref[g])
            cnt = jnp.clip(pos + 1, 1, win).astype(F32)
            dpc_ref[:, cols] = dp / cnt

    row = pl.BlockSpec((tm, C), lambda m: (m, 0))
    wspec = pl.BlockSpec((4, POOL_GROUP, POOL_GROUP), lambda m: (0, 0, 0))
    vec = pl.BlockSpec((1, C), lambda m: (0, 0))
    return pl.pallas_call(
        body, name=name, grid=(M // tm,),
        in_specs=[row, row, wspec, vec], out_specs=[row, wspec, vec],
        out_shape=[jax.ShapeDtypeStruct((M, C), F32), jax.ShapeDtypeStruct((4, POOL_GROUP, POOL_GROUP), F32),
                   jax.ShapeDtypeStruct((1, C), F32)],
        compiler_params=_params(("arbitrary",)))(dout, pooled, w, scale)


def _pool_bwd_window(dpc, name):
    M, C = dpc.shape
    tm = 256
    hb = tm // HALO
    last = M // HALO - 1

    def body(d_ref, halo_ref, da_ref):
        i = pl.program_id(0)
        row0 = i * tm
        ext = jnp.concatenate([d_ref[...], halo_ref[...]], axis=0)
        src = row0 + lax.broadcasted_iota(jnp.int32, (tm + HALO, 1), 0)
        ext = jnp.where(src < M, ext, 0.0)
        r = lax.broadcasted_iota(jnp.int32, (tm, tm + HALO), 0)
        c = lax.broadcasted_iota(jnp.int32, (tm, tm + HALO), 1)
        pos = _pool_pos(row0, tm)
        for g, win in enumerate(POOL_WINDOWS):
            band = ((c >= r) & (c < r + win)).astype(BF16)
            cols = slice(g * POOL_GROUP, (g + 1) * POOL_GROUP)
            xg = ext[:, cols]
            tot = _band_dot(band, xg)
            cnt = jnp.clip(pos + 1, 1, win).astype(F32)
            da_ref[:, cols] = jnp.where(pos >= 0, tot - xg[:tm] * cnt, 0.0)

    row = pl.BlockSpec((tm, C), lambda m: (m, 0))
    return pl.pallas_call(
        body, name=name, grid=(M // tm,),
        in_specs=[row, pl.BlockSpec((HALO, C), lambda m: (jnp.minimum((m + 1) * hb, last), 0))],
        out_specs=row, out_shape=jax.ShapeDtypeStruct((M, C), F32),
        compiler_params=_params(("parallel",)))(dpc, dpc)


def _head_masks():
    lane = lax.broadcasted_iota(jnp.int32, (1, 128), 1)
    return lane < HEAD_DIM, lane


def _split_heads(x, first):
    z = jnp.zeros_like(x)
    return jnp.where(first, x, z), jnp.where(first, z, x)


def _split_rope(x, lane):
    z = jnp.zeros_like(x)
    return jnp.where(lane < MLA_ROPE, x, z), jnp.where((lane >= MLA_ROPE) & (lane < 2 * MLA_ROPE), x, z)


def _walk_causal(i, step):
    def mid(kb, carry):
        step(kb, False)
        return carry

    step(0, True)
    lax.fori_loop(1, i, mid, 0)

    @pl.when(i > 0)
    def _():
        step(i, True)


def _mla_fwd(q_all, kv_all, qr, kr, scale, name):
    M = q_all.shape[0]
    t = ATTN_TILE

    def body(q_ref, k_ref, v_ref, qr_ref, kr_ref, o_ref, lse_ref, m_s, l_s, acc_s, kmax_s):
        i = pl.program_id(1)
        first, lane = _head_masks()

        @pl.when(i == 0)
        def _():
            def block_max(kb, carry):
                rows = pl.ds(pl.multiple_of(kb * t, t), t)
                kk = k_ref[rows, :].astype(F32)
                kk = kk * kk
                rr = kr_ref[rows, :].astype(F32)
                rr = jnp.sum(jnp.where(lane < MLA_ROPE, rr * rr, 0.0), axis=1, keepdims=True)
                a = jnp.max(jnp.sum(jnp.where(first, kk, 0.0), axis=1, keepdims=True) + rr)
                b = jnp.max(jnp.sum(jnp.where(first, 0.0, kk), axis=1, keepdims=True) + rr)
                return jnp.maximum(carry[0], a), jnp.maximum(carry[1], b)

            a, b = lax.fori_loop(0, M // t, block_max, (jnp.float32(0.0), jnp.float32(0.0)))
            kmax_s[0] = a
            kmax_s[1] = b

        qs = _split_heads(q_ref[...], first)
        qrs = _split_rope(qr_ref[...], lane)
        qcat = tuple(jnp.concatenate([qs[hh], qrs[hh]], axis=1) for hh in range(2))
        qpos = i * t + lax.broadcasted_iota(jnp.int32, (t, t), 0)
        kidx = lax.broadcasted_iota(jnp.int32, (t, t), 1)
        c2 = scale * LOG2E

        def run(online):
            l_s[...] = jnp.zeros_like(l_s)
            acc_s[...] = jnp.zeros_like(acc_s)
            if online:
                m_s[...] = jnp.full_like(m_s, NEG)

            def step(kb, masked):
                k0 = pl.multiple_of(kb * t, t)
                kcat = jnp.concatenate([k_ref[pl.ds(k0, t), :], kr_ref[pl.ds(k0, t), :]], axis=1)
                vs = _split_heads(v_ref[pl.ds(k0, t), :], first)
                if masked:
                    kpos = k0 + kidx
                    valid = (kpos <= qpos) & (kpos >= PAD)
                pv = None
                alphas = []
                for hh in range(2):
                    s = _dot_nt(qcat[hh], kcat)
                    if online:
                        if masked:
                            s = jnp.where(valid, s, NEG)
                        m_old = m_s[hh]
                        m_new = jnp.maximum(m_old, jnp.max(s, axis=1, keepdims=True))
                        p = jnp.exp2((s - m_new) * c2)
                        alpha = jnp.exp2((m_old - m_new) * c2)
                        l_s[hh] = alpha * l_s[hh] + jnp.sum(p, axis=1, keepdims=True)
                        m_s[hh] = m_new
                        alphas.append(alpha)
                    else:
                        p = jnp.exp2(s * c2 - m_s[hh])
                        if masked:
                            p = jnp.where(valid, p, 0.0)
                        l_s[hh] = l_s[hh] + jnp.sum(p, axis=1, keepdims=True)
                    d = _dot(p.astype(BF16), vs[hh])
                    pv = d if pv is None else pv + d
                if online:
                    acc_s[...] = acc_s[...] * jnp.where(first, alphas[0], alphas[1]) + pv
                else:
                    acc_s[...] += pv

            _walk_causal(i, step)

        for hh in range(2):
            qf = qcat[hh].astype(F32)
            m_s[hh] = (1.001 * c2) * jnp.sqrt(jnp.sum(qf * qf, axis=1, keepdims=True) * kmax_s[hh])
        run(False)
        real = i * t + lax.broadcasted_iota(jnp.int32, (t, 1), 0) >= PAD
        underflow = jnp.max(jnp.where(real & (jnp.minimum(l_s[0], l_s[1]) < 1e-30), 1.0, 0.0)) > 0.0

        @pl.when(underflow)
        def _():
            run(True)
            m_s[...] = m_s[...] * c2

        ls = tuple(jnp.where(l_s[hh] > 0.0, l_s[hh], 1.0) for hh in range(2))
        o_ref[...] = (acc_s[...] * jnp.where(first, 1.0 / ls[0], 1.0 / ls[1])).astype(o_ref.dtype)
        lse_ref[:, 0:1] = m_s[0] * (1.0 / LOG2E) + jnp.log(ls[0])
        lse_ref[:, 1:2] = m_s[1] * (1.0 / LOG2E) + jnp.log(ls[1])

    blk = pl.BlockSpec((t, 128), lambda j, i: (i, j))
    return pl.pallas_call(
        body, name=name, grid=(N_PAIRS, M // t),
        in_specs=[blk, pl.BlockSpec((M, 128), lambda j, i: (0, j)), pl.BlockSpec((M, 128), lambda j, i: (0, N_PAIRS + j)),
                  blk, pl.BlockSpec((M, 128), lambda j, i: (0, 0))],
        out_specs=[blk, pl.BlockSpec((None, t, 2), lambda j, i: (j, i, 0))],
        out_shape=[jax.ShapeDtypeStruct((M, N_PAIRS * 128), BF16), jax.ShapeDtypeStruct((N_PAIRS, M, 2), F32)],
        scratch_shapes=[pltpu.VMEM((2, t, 1), F32), pltpu.VMEM((2, t, 1), F32), pltpu.VMEM((t, 128), F32),
                        pltpu.SMEM((2,), F32)],
        compiler_params=_params(("arbitrary", "arbitrary")))(q_all, kv_all, kv_all, qr, kr)


def _mla_bwd(q_all, kv_all, qr, kr, o, do, lse, scale, name):
    M = q_all.shape[0]
    t = ATTN_BWD_TILE

    def body(q_ref, kv_hbm, qr_ref, kr_hbm, o_ref, do_ref, lse_ref,
             dq_ref, dk_hbm, dv_hbm, dqr_ref, dkr_hbm,
             k_ref, v_ref, kr_ref, dk_ref, dv_ref, dkr_ref, dq_s, lse_s, delta_s):
        j = pl.program_id(0)
        i = pl.program_id(1)
        first, lane = _head_masks()
        every = pl.ds(0, M)
        kcols = pl.ds(pl.multiple_of(j * 128, 128), 128)
        vcols = pl.ds(pl.multiple_of((N_PAIRS + j) * 128, 128), 128)

        @pl.when(i == 0)
        def _():
            pltpu.sync_copy(kv_hbm.at[every, kcols], k_ref)
            pltpu.sync_copy(kv_hbm.at[every, vcols], v_ref)
            pltpu.sync_copy(kr_hbm, kr_ref)
            dk_ref[...] = jnp.zeros_like(dk_ref)
            dv_ref[...] = jnp.zeros_like(dv_ref)
            dkr_ref[...] = jnp.zeros_like(dkr_ref)

        qs = _split_heads(q_ref[...], first)
        qrs = _split_rope(qr_ref[...], lane)
        qcat = tuple(jnp.concatenate([qs[hh], qrs[hh]], axis=1) for hh in range(2))
        dov = do_ref[...]
        dos = _split_heads(dov, first)
        prod = dov.astype(F32) * o_ref[...].astype(F32)
        deltas = (jnp.sum(jnp.where(first, prod, 0.0), axis=1, keepdims=True),
                  jnp.sum(jnp.where(first, 0.0, prod), axis=1, keepdims=True))
        for hh in range(2):
            lse_s[hh] = jnp.broadcast_to(lse_ref[:, hh:hh + 1], (t, t))
            delta_s[hh] = jnp.broadcast_to(deltas[hh], (t, t))
        dq_s[...] = jnp.zeros_like(dq_s)
        qpos = i * t + lax.broadcasted_iota(jnp.int32, (t, t), 0)
        kidx = lax.broadcasted_iota(jnp.int32, (t, t), 1)

        def step(kb, masked):
            k0 = pl.multiple_of(kb * t, t)
            rows = pl.ds(k0, t)
            k = k_ref[rows, :]
            v = v_ref[rows, :]
            kr = kr_ref[rows, :]
            kcat = jnp.concatenate([k, kr], axis=1)
            ks = _split_heads(k, first)
            krs = _split_rope(kr, lane)
            if masked:
                kpos = k0 + kidx
                valid = (kpos <= qpos) & (kpos >= PAD)
            dq = dk = dv = None
            for hh in range(2):
                s = _dot_nt(qcat[hh], kcat) * scale
                if masked:
                    s = jnp.where(valid, s, NEG)
                p = jnp.exp(s - lse_s[hh])
                ds = p * (_dot_nt(dos[hh], v) - delta_s[hh])
                dsb = (ds * scale).astype(BF16)
                a = _dot(dsb, jnp.concatenate([ks[hh], krs[hh]], axis=1))
                b = _dot_tn(dsb, qcat[hh])
                c = _dot_tn(p.astype(BF16), dos[hh])
                dq = a if dq is None else dq + a
                dk = b if dk is None else dk + b
                dv = c if dv is None else dv + c
            dq_s[...] += dq
            dk_ref[rows, :] += dk[:, :128]
            dkr_ref[rows, :] += dk[:, 128:]
            dv_ref[rows, :] += dv

        _walk_causal(i, step)
        dq_ref[...] = dq_s[:, :128].astype(dq_ref.dtype)
        dqr_ref[...] = dq_s[:, 128:].astype(dqr_ref.dtype)

        @pl.when(i == M // t - 1)
        def _():
            pltpu.sync_copy(dk_ref, dk_hbm.at[every, kcols])
            pltpu.sync_copy(dv_ref, dv_hbm.at[every, kcols])
            pltpu.sync_copy(dkr_ref, dkr_hbm.at[j])

    blk = pl.BlockSpec((t, 128), lambda j, i: (i, j))
    whole = pl.BlockSpec(memory_space=pl.ANY)
    wide = jax.ShapeDtypeStruct((M, N_PAIRS * 128), F32)
    slab = lambda dtype: pltpu.VMEM((M, 128), dtype)
    return pl.pallas_call(
        body, name=name, grid=(N_PAIRS, M // t),
        in_specs=[blk, whole, blk, whole, blk, blk, pl.BlockSpec((None, t, 2), lambda j, i: (j, i, 0))],
        out_specs=[blk, whole, whole, blk, whole],
        out_shape=[jax.ShapeDtypeStruct((M, N_PAIRS * 128), BF16), wide, wide,
                   jax.ShapeDtypeStruct((M, N_PAIRS * 128), BF16), jax.ShapeDtypeStruct((N_PAIRS, M, 128), F32)],
        scratch_shapes=[slab(BF16), slab(BF16), slab(BF16), slab(F32), slab(F32), slab(F32),
                        pltpu.VMEM((t, 256), F32), pltpu.VMEM((2, t, t), F32), pltpu.VMEM((2, t, t), F32)],
        compiler_params=_params(("arbitrary", "arbitrary")))(q_all, kv_all, qr, kr, o, do, lse)


def _tri(t, rel):
    j = lax.broadcasted_iota(jnp.int32, (t, t), 0)
    k = lax.broadcasted_iota(jnp.int32, (t, t), 1)
    m = {"gt": j > k, "le": j <= k, "lt": j < k}[rel]
    return m.astype(BF16)


def _lane_cumsum(x, tri):
    hi = x.astype(BF16)
    lo = (x - hi.astype(F32)).astype(BF16)
    return _dot(hi, tri) + _dot(lo, tri)


def _log_sigmoids(z):
    sp = jnp.log(1.0 + jnp.exp(-jnp.abs(z)))
    return jnp.minimum(z, 0.0) - sp, jnp.minimum(-z, 0.0) - sp


def _log_sigmoids_fast(z):
    lk = -(jnp.maximum(z, 0.0) + jnp.log(1.0 + jnp.exp(-jnp.abs(z))))
    return lk + z, lk


def _sb_fwd(qkv, scale, name):
    M = qkv.shape[0]
    t = WALK_TILE
    ck, cv = N_PAIRS, 2 * N_PAIRS

    def body(q_ref, k_ref, v_ref, o_ref, tot_ref, c_s, acc_s):
        i = pl.program_id(1)
        first, _ = _head_masks()
        qs = _split_heads(q_ref[...], first)
        c_s[...] = jnp.zeros_like(c_s)
        acc_s[...] = jnp.zeros_like(acc_s)
        tri = _tri(t, "gt")
        qpos = i * t + lax.broadcasted_iota(jnp.int32, (t, t), 0)
        kidx = lax.broadcasted_iota(jnp.int32, (t, t), 1)

        def step(kb, masked):
            k0 = pl.multiple_of(kb * t, t)
            k = k_ref[pl.ds(k0, t), :]
            vs = _split_heads(v_ref[pl.ds(k0, t), :], first)
            if masked:
                kpos = k0 + kidx
                valid = (kpos < qpos) & (kpos >= PAD)
            pv = None
            for hh in range(2):
                z = _dot_nt(qs[hh], k) * scale
                lb, lk = _log_sigmoids_fast(z)
                if masked:
                    lk = jnp.where(valid, lk, 0.0)
                a = jnp.exp(lb + (c_s[hh] + _lane_cumsum(lk, tri)))
                if masked:
                    a = jnp.where(valid, a, 0.0)
                c_s[hh] = c_s[hh] + jnp.sum(lk, axis=1, keepdims=True)
                d = _dot(a.astype(BF16), vs[hh])
                pv = d if pv is None else pv + d
            acc_s[...] += pv

        def keep_going():
            return jnp.max(jnp.maximum(c_s[0], c_s[1])) > EXP_ZERO

        def cond(carry):
            kb, go, _ = carry
            return (kb >= 1) & go

        def walk(carry):
            kb, _, n = carry
            step(kb, False)
            return kb - 1, keep_going(), n + 1

        step(i, True)
        _, go, n = lax.while_loop(cond, walk, (i - 1, keep_going(), jnp.int32(1)))
        first_too = go & (i > 0)

        @pl.when(first_too)
        def _():
            step(0, True)

        walked = n + first_too.astype(jnp.int32)
        o_ref[...] = acc_s[...].astype(o_ref.dtype)
        tot_ref[:, 0:1] = c_s[0]
        tot_ref[:, 1:2] = c_s[1]
        tot_ref[:, 2:3] = jnp.full((t, 1), walked.astype(F32))

    whole = lambda c0: pl.BlockSpec((M, 128), lambda j, i: (0, c0 + j))
    return pl.pallas_call(
        body, name=name, grid=(N_PAIRS, M // t),
        in_specs=[pl.BlockSpec((t, 128), lambda j, i: (i, j)), whole(ck), whole(cv)],
        out_specs=[pl.BlockSpec((t, 128), lambda j, i: (i, j)), pl.BlockSpec((None, t, 3), lambda j, i: (j, i, 0))],
        out_shape=[jax.ShapeDtypeStruct((M, N_PAIRS * 128), BF16), jax.ShapeDtypeStruct((N_PAIRS, M, 3), F32)],
        scratch_shapes=[pltpu.VMEM((2, t, 1), F32), pltpu.VMEM((t, 128), F32)],
        compiler_params=_params(("parallel", "arbitrary")))(qkv, qkv, qkv)


def _sb_bwd(qkv, do, tot, scale, name):
    M = qkv.shape[0]
    t = WALK_TILE
    ck, cv = N_PAIRS, 2 * N_PAIRS

    def body(q_ref, k_ref, v_ref, do_ref, tot_ref, dq_ref, dk_ref, dv_ref, pc_s, dc_s, dq_s):
        i = pl.program_id(1)
        first, _ = _head_masks()

        @pl.when(i == 0)
        def _():
            dk_ref[...] = jnp.zeros_like(dk_ref)
            dv_ref[...] = jnp.zeros_like(dv_ref)

        qs = _split_heads(q_ref[...], first)
        dos = _split_heads(do_ref[...], first)
        pc_s[...] = jnp.zeros_like(pc_s)
        dc_s[...] = jnp.zeros_like(dc_s)
        dq_s[...] = jnp.zeros_like(dq_s)
        tri_le = _tri(t, "le")
        tri_lt = _tri(t, "lt")
        qpos = i * t + lax.broadcasted_iota(jnp.int32, (t, t), 0)
        kidx = lax.broadcasted_iota(jnp.int32, (t, t), 1)

        def step(kb, masked):
            k0 = pl.multiple_of(kb * t, t)
            rows = pl.ds(k0, t)
            k = k_ref[rows, :]
            v = v_ref[rows, :]
            ks = _split_heads(k, first)
            if masked:
                kpos = k0 + kidx
                valid = (kpos < qpos) & (kpos >= PAD)
            dq = dk = dv = None
            for hh in range(2):
                z = _dot_nt(qs[hh], k) * scale
                lb, lk = _log_sigmoids_fast(z)
                if masked:
                    lk = jnp.where(valid, lk, 0.0)
                later = tot_ref[:, hh:hh + 1] - (pc_s[hh] + _lane_cumsum(lk, tri_le))
                a = jnp.exp(lb + later)
                if masked:
                    a = jnp.where(valid, a, 0.0)
                dl = a * _dot_nt(dos[hh], v)
                early = dc_s[hh] + _lane_cumsum(dl, tri_lt)
                sg = jnp.exp(lb)
                dz = (dl * (1.0 - sg) - early * sg) * scale
                if masked:
                    dz = jnp.where(valid, dz, 0.0)
                pc_s[hh] = pc_s[hh] + jnp.sum(lk, axis=1, keepdims=True)
                dc_s[hh] = dc_s[hh] + jnp.sum(dl, axis=1, keepdims=True)
                dzb = dz.astype(BF16)
                x = _dot(dzb, ks[hh])
                y = _dot_tn(dzb, qs[hh])
                w = _dot_tn(a.astype(BF16), dos[hh])
                dq = x if dq is None else dq + x
                dk = y if dk is None else dk + y
                dv = w if dv is None else dv + w
            dq_s[...] += dq
            dk_ref[rows, :] += dk
            dv_ref[rows, :] += dv

        first_walked = i + 1 - jnp.max(tot_ref[:, 2:3]).astype(jnp.int32)

        def mid(kb, carry):
            step(kb, False)
            return carry

        @pl.when((first_walked == 0) & (i > 0))
        def _():
            step(0, True)

        lax.fori_loop(jnp.maximum(first_walked, 1), i, mid, 0)
        step(i, True)
        dq_ref[...] = dq_s[...].astype(dq_ref.dtype)

    whole = lambda c0: pl.BlockSpec((M, 128), lambda j, i: (0, c0 + j))
    blk = pl.BlockSpec((t, 128), lambda j, i: (i, j))
    col = pl.BlockSpec((M, 128), lambda j, i: (0, j))
    return pl.pallas_call(
        body, name=name, grid=(N_PAIRS, M // t),
        in_specs=[blk, whole(ck), whole(cv), blk, pl.BlockSpec((None, t, 3), lambda j, i: (j, i, 0))],
        out_specs=[blk, col, col],
        out_shape=[jax.ShapeDtypeStruct((M, N_PAIRS * 128), BF16), jax.ShapeDtypeStruct((M, N_PAIRS * 128), F32),
                   jax.ShapeDtypeStruct((M, N_PAIRS * 128), F32)],
        scratch_shapes=[pltpu.VMEM((2, t, 1), F32), pltpu.VMEM((2, t, 1), F32), pltpu.VMEM((t, 128), F32)],
        compiler_params=_params(("parallel", "arbitrary")))(qkv, qkv, qkv, do, tot)


def _rows_between(lo, hi):
    r = lax.broadcasted_iota(jnp.int32, (128, 1), 0)
    return (r >= lo) & (r < hi)


def _lanes_between(lo, hi):
    c = lax.broadcasted_iota(jnp.int32, (1, 128), 1)
    return (c >= lo) & (c < hi)


def _keep(x, mask):
    return jnp.where(mask, x, jnp.zeros_like(x))


def _valid_mask(i, kb, t):
    kpos = kb * t + lax.broadcasted_iota(jnp.int32, (t, t), 0)
    qpos = i * t + lax.broadcasted_iota(jnp.int32, (t, t), 1)
    return (kpos <= qpos) & (kpos >= PAD)


def _fox_fwd(qkv, qkv_t, f_rows, f_cols, scale, name):
    M = qkv.shape[0]
    t = FOX_TILE
    first_blk = PAD // t
    ck, cv = N_PAIRS, 2 * N_PAIRS

    def body(qt_ref, k_ref, vt_ref, fq_ref, fk_ref, o_ref, lse_ref, ox_ref, m_s, l_s, acc_s, accx_s, kmax_s, walked_s):
        i = pl.program_id(1)

        @pl.when(i == 0)
        def _():
            first = _lanes_between(0, 64)

            def block_max(kb, carry):
                kk = k_ref[pl.ds(pl.multiple_of(kb * t, t), t), :].astype(F32)
                kk = kk * kk
                a = jnp.max(jnp.sum(jnp.where(first, kk, 0.0), axis=1, keepdims=True))
                b = jnp.max(jnp.sum(jnp.where(first, 0.0, kk), axis=1, keepdims=True))
                return jnp.maximum(carry[0], a), jnp.maximum(carry[1], b)

            a, b = lax.fori_loop(0, M // t, block_max, (jnp.float32(0.0), jnp.float32(0.0)))
            kmax_s[0] = a
            kmax_s[1] = b

        qt = qt_ref[...]
        qts = (_keep(qt, _rows_between(0, 64)), _keep(qt, _rows_between(64, 128)))
        qf = qt.astype(F32)
        qf = qf * qf
        qbound = tuple(
            (1.001 * scale) * jnp.sqrt(jnp.sum(qf[HEAD_DIM * hh:HEAD_DIM * (hh + 1)], axis=0, keepdims=True) * kmax_s[hh])
            for hh in range(2))
        def run(online):
            l_s[...] = jnp.zeros_like(l_s)
            acc_s[...] = jnp.zeros_like(acc_s)
            accx_s[...] = jnp.zeros_like(accx_s)
            if online:
                m_s[...] = jnp.full_like(m_s, NEG)

            def step(kb, masked):
                k0 = pl.multiple_of(kb * t, t)
                rows = pl.ds(k0, t)
                k = k_ref[rows, :]
                if masked:
                    valid = _valid_mask(i, kb, t)
                for hh in range(2):
                    s = _dot(k, qts[hh]) * scale + (fq_ref[hh:hh + 1, :] - fk_ref[rows, hh:hh + 1])
                    hr = slice(HEAD_DIM * hh, HEAD_DIM * (hh + 1))
                    vt = vt_ref[hr, rows]
                    if online:
                        if masked:
                            s = jnp.where(valid, s, NEG)
                        m_old = m_s[hh]
                        m_new = jnp.maximum(m_old, jnp.max(s, axis=0, keepdims=True))
                        p = jnp.exp(s - m_new)
                        alpha = jnp.exp(m_old - m_new)
                        l_s[hh] = alpha * l_s[hh] + jnp.sum(p, axis=0, keepdims=True)
                        m_s[hh] = m_new
                        pb = p.astype(BF16)
                        acc_s[hr, :] = acc_s[hr, :] * alpha + _dot(vt, pb)
                        accx_s[hr, :] = accx_s[hr, :] * alpha + _dot(vt, (p - pb.astype(F32)).astype(BF16))
                    else:
                        p = jnp.exp(s - m_s[hh])
                        if masked:
                            p = jnp.where(valid, p, 0.0)
                        l_s[hh] = l_s[hh] + jnp.sum(p, axis=0, keepdims=True)
                        pb = p.astype(BF16)
                        acc_s[hr, :] += _dot(vt, pb)
                        accx_s[hr, :] += _dot(vt, (p - pb.astype(F32)).astype(BF16))

            def keep_going(kb):
                k0 = pl.multiple_of(kb * t, t)
                worst = None
                for hh in range(2):
                    f0 = jnp.max(fk_ref[pl.ds(k0, 8), hh:hh + 1])
                    decay = fq_ref[hh:hh + 1, :] - f0
                    if online:
                        w = jnp.max(qbound[hh] + decay - m_s[hh])
                    else:
                        w = jnp.max(decay - jnp.minimum(jnp.log(jnp.maximum(l_s[hh], 1e-37)), 0.0))
                    worst = w if worst is None else jnp.maximum(worst, w)
                return worst > EXP_ZERO

            def cond(carry):
                kb, go, _ = carry
                return (kb > first_blk) & go

            def walk(carry):
                kb, _, n = carry
                step(kb, False)
                return kb - 1, keep_going(kb), n + 1

            step(i, True)
            _, go, n = lax.while_loop(cond, walk, (i - 1, keep_going(i), jnp.int32(1)))
            first_too = go & (i > first_blk)

            @pl.when(first_too)
            def _():
                step(first_blk, True)

            walked_s[0] = n + first_too.astype(jnp.int32)

        for hh in range(2):
            m_s[hh] = qbound[hh]
        run(False)
        real = i * t + lax.broadcasted_iota(jnp.int32, (1, t), 1) >= PAD
        underflow = jnp.max(jnp.where(real & (jnp.minimum(l_s[0], l_s[1]) < 1e-30), 1.0, 0.0)) > 0.0

        @pl.when(underflow)
        def _():
            run(True)

        outs = []
        for hh in range(2):
            hr = slice(HEAD_DIM * hh, HEAD_DIM * (hh + 1))
            l = jnp.where(l_s[hh] > 0.0, l_s[hh], 1.0)
            inv = 1.0 / l
            outs.append(acc_s[hr, :] * inv)
            ox_ref[hr, :] = (acc_s[hr, :] + accx_s[hr, :]) * inv
            lse_ref[hh:hh + 1, :] = m_s[hh] + jnp.log(l)
        o_ref[...] = jnp.concatenate(outs, axis=0).T.astype(o_ref.dtype)
        lse_ref[2:3, :] = jnp.full((1, t), walked_s[0].astype(F32))

    blk = pl.BlockSpec((128, t), lambda j, i: (j, i))
    stat = pl.BlockSpec((None, 2, t), lambda j, i: (j, 0, i))
    return pl.pallas_call(
        body, name=name, grid=(N_PAIRS, M // t),
        in_specs=[blk, pl.BlockSpec((M, 128), lambda j, i: (0, ck + j)), pl.BlockSpec((128, M), lambda j, i: (cv + j, 0)),
                  stat, pl.BlockSpec((None, M, 2), lambda j, i: (j, 0, 0))],
        out_specs=[pl.BlockSpec((t, 128), lambda j, i: (i, j)), pl.BlockSpec((None, 3, t), lambda j, i: (j, 0, i)), blk],
        out_shape=[jax.ShapeDtypeStruct((M, N_PAIRS * 128), BF16), jax.ShapeDtypeStruct((N_PAIRS, 3, M), F32),
                   jax.ShapeDtypeStruct((N_PAIRS * 128, M), F32)],
        scratch_shapes=[pltpu.VMEM((2, 1, t), F32), pltpu.VMEM((2, 1, t), F32), pltpu.VMEM((128, t), F32),
                        pltpu.VMEM((128, t), F32), pltpu.SMEM((2,), F32), pltpu.SMEM((1,), jnp.int32)],
        compiler_params=_params(("arbitrary", "arbitrary")))(qkv_t, qkv, qkv_t, f_rows, f_cols)


def _fox_bwd(qkv, qkv_t, o_t, do, do_t, lse, f_rows, f_cols, scale, name):
    assert np.frexp(scale)[0] == 0.5, "the key sums ride in the dK matmul: the scale must be a power of two"
    M = qkv.shape[0]
    t = FOX_TILE
    first_blk = PAD // t
    ck, cv = N_PAIRS, 2 * N_PAIRS

    def body(q_ref, qt_ref, k_ref, kt_ref, v_ref, ot_ref, do_ref, dot_ref, lse_ref, fq_ref, fk_ref,
             dq_ref, dk_ref, dv_ref, cs_ref, dq_s):
        i = pl.program_id(1)

        @pl.when(i == 0)
        def _():
            dk_ref[...] = jnp.zeros_like(dk_ref)
            dv_ref[...] = jnp.zeros_like(dv_ref)
            cs_ref[...] = jnp.zeros_like(cs_ref)

        heads_l = (_lanes_between(0, 64), _lanes_between(64, 128))
        heads_r = (_rows_between(0, 64), _rows_between(64, 128))
        q = q_ref[...]
        qt = qt_ref[...]
        do = do_ref[...]
        dot = dot_ref[...]
        q_ones = tuple(jnp.where(m, q, jnp.ones_like(q)) for m in heads_l)
        qts = tuple(_keep(qt, m) for m in heads_r)
        dos = tuple(_keep(do, m) for m in heads_l)
        dots = tuple(_keep(dot, m) for m in heads_r)
        prod = dot.astype(F32) * ot_ref[...]
        deltas = tuple(jnp.sum(prod[HEAD_DIM * hh:HEAD_DIM * (hh + 1)], axis=0, keepdims=True) for hh in range(2))
        ones = tuple(m.astype(BF16) * jnp.ones((t, 128), BF16) for m in heads_l)
        dq_s[...] = jnp.zeros_like(dq_s)

        def step(kb, masked):
            k0 = pl.multiple_of(kb * t, t)
            rows = pl.ds(k0, t)
            k = k_ref[rows, :]
            v = v_ref[rows, :]
            if masked:
                valid = _valid_mask(i, kb, t)
            dk = dv = cs = None
            for hh in range(2):
                s = _dot(k, qts[hh]) * scale + (fq_ref[hh:hh + 1, :] - fk_ref[rows, hh:hh + 1])
                if masked:
                    s = jnp.where(valid, s, NEG)
                p = jnp.exp(s - lse_ref[hh:hh + 1, :])
                ds = p * (_dot(v, dots[hh]) - deltas[hh])
                hi = ds.astype(BF16)
                lo = (ds - hi.astype(F32)).astype(BF16)
                dsb = (ds * scale).astype(BF16)
                hr = slice(HEAD_DIM * hh, HEAD_DIM * (hh + 1))
                dq_s[hr, :] += _dot(kt_ref[hr, rows], dsb)
                a = _dot(dsb, q_ones[hh])
                c = jnp.where(heads_l[hh], pltpu.roll(a, HEAD_DIM, 1) * (1.0 / scale), 0.0) + _dot(lo, ones[hh])
                a = jnp.where(heads_l[hh], a, 0.0)
                b = _dot(p.astype(BF16), dos[hh])
                dk = a if dk is None else dk + a
                dv = b if dv is None else dv + b
                cs = c if cs is None else cs + c
            dk_ref[rows, :] += dk
            dv_ref[rows, :] += dv
            cs_ref[rows, :] += cs

        first_walked = i + 1 - jnp.max(lse_ref[2:3, :]).astype(jnp.int32)

        def mid(kb, carry):
            step(kb, False)
            return carry

        @pl.when((first_walked == first_blk) & (i > first_blk))
        def _():
            step(first_blk, True)

        lax.fori_loop(jnp.maximum(first_walked, first_blk + 1), i, mid, 0)
        step(i, True)
        dq_ref[...] = dq_s[...].T.astype(dq_ref.dtype)

    rblk = pl.BlockSpec((t, 128), lambda j, i: (i, j))
    tblk = pl.BlockSpec((128, t), lambda j, i: (j, i))
    stat = pl.BlockSpec((None, 2, t), lambda j, i: (j, 0, i))
    stat3 = pl.BlockSpec((None, 3, t), lambda j, i: (j, 0, i))
    col = pl.BlockSpec((M, 128), lambda j, i: (0, j))
    wide = jax.ShapeDtypeStruct((M, N_PAIRS * 128), F32)
    return pl.pallas_call(
        body, name=name, grid=(N_PAIRS, M // t),
        in_specs=[rblk, tblk, pl.BlockSpec((M, 128), lambda j, i: (0, ck + j)),
                  pl.BlockSpec((128, M), lambda j, i: (ck + j, 0)), pl.BlockSpec((M, 128), lambda j, i: (0, cv + j)),
                  tblk, rblk, tblk, stat3, stat, pl.BlockSpec((None, M, 2), lambda j, i: (j, 0, 0))],
        out_specs=[rblk, col, col, pl.BlockSpec((None, M, 128), lambda j, i: (j, 0, 0))],
        out_shape=[jax.ShapeDtypeStruct((M, N_PAIRS * 128), BF16), wide, wide,
                   jax.ShapeDtypeStruct((N_PAIRS, M, 128), F32)],
        scratch_shapes=[pltpu.VMEM((128, t), F32)],
        compiler_params=_params(("parallel", "arbitrary")))(qkv, qkv_t, qkv, qkv_t, qkv, o_t, do, do_t, lse,
                                                            f_rows, f_cols)


def _rope_tables(M):
    pos = (jnp.arange(M, dtype=jnp.int32) - PAD).astype(F32)
    inv = ROPE_THETA ** (-jnp.arange(0, MLA_ROPE, 2, dtype=F32) / MLA_ROPE)
    ang = pos[:, None] * inv[None, :]
    cos, sin = jnp.cos(ang), jnp.sin(ang)
    z = jnp.zeros((M, 64), F32)
    cos_t = jnp.concatenate([cos, cos, cos, cos, z], axis=1)
    sin_t = jnp.concatenate([-sin, sin, -sin, sin, z], axis=1)
    return cos_t, sin_t


def _rope(x, cos_t, sin_t, out_dtype, name, inverse=False, lead=0):
    M, C = x.shape
    tm = _pick(M, (768, 512, 256, 128))
    nblk = (C - lead) // 128
    sign = -1.0 if inverse else 1.0

    def body(x_ref, c_ref, s_ref, o_ref):
        lane = lax.broadcasted_iota(jnp.int32, (1, 128), 1)
        low = (lane % MLA_ROPE) < (MLA_ROPE // 2)
        cos = c_ref[...]
        sin = s_ref[...] * sign
        if lead:
            o_ref[:, :lead] = x_ref[:, :lead].astype(o_ref.dtype)
        for b in range(nblk):
            cols = slice(lead + b * 128, lead + (b + 1) * 128)
            v = x_ref[:, cols].astype(F32)
            up = pltpu.roll(v, 128 - MLA_ROPE // 2, 1)
            down = pltpu.roll(v, MLA_ROPE // 2, 1)
            o_ref[:, cols] = (v * cos + jnp.where(low, up, down) * sin).astype(o_ref.dtype)

    row = pl.BlockSpec((tm, C), lambda m: (m, 0))
    tab = pl.BlockSpec((tm, 128), lambda m: (m, 0))
    return pl.pallas_call(
        body, name=name, grid=(M // tm,), in_specs=[row, tab, tab], out_specs=row,
        out_shape=jax.ShapeDtypeStruct((M, C), out_dtype),
        compiler_params=_params(("parallel",)))(x, cos_t, sin_t)


def _forget_cumsum(f_logit, bias, name):
    M = f_logit.shape[0]
    tm = 256

    def body(f_ref, b_ref, o_ref, c_s):
        i = pl.program_id(0)

        @pl.when(i == 0)
        def _():
            c_s[...] = jnp.zeros_like(c_s)
        ls, _ = _log_sigmoids(f_ref[...] + b_ref[...])
        rows = i * tm + lax.broadcasted_iota(jnp.int32, (tm, 1), 0)
        ls = jnp.where(rows >= PAD, ls, 0.0)
        r = lax.broadcasted_iota(jnp.int32, (tm, tm), 0)
        c = lax.broadcasted_iota(jnp.int32, (tm, tm), 1)
        tri = (c <= r).astype(F32)
        cum = jnp.dot(tri, ls, precision=lax.Precision.HIGHEST, preferred_element_type=F32) + c_s[...]
        o_ref[...] = cum
        c_s[...] = cum[tm - 1:tm, :]

    row = pl.BlockSpec((tm, 128), lambda m: (m, 0))
    return pl.pallas_call(
        body, name=name, grid=(M // tm,),
        in_specs=[row, pl.BlockSpec((1, 128), lambda m: (0, 0))], out_specs=row,
        out_shape=jax.ShapeDtypeStruct((M, 128), F32), scratch_shapes=[pltpu.VMEM((1, 128), F32)],
        compiler_params=_params(("arbitrary",)))(f_logit, bias)


def _forget_cumsum_bwd(f_logit, bias, colsum, name):
    M = f_logit.shape[0]
    tm = 256
    nb = M // tm

    def body(f_ref, b_ref, cs_ref, o_ref, db_ref, c_s):
        i = pl.program_id(0)

        @pl.when(i == 0)
        def _():
            c_s[...] = jnp.zeros_like(c_s)
            db_ref[...] = jnp.zeros_like(db_ref)
        rr = lax.broadcasted_iota(jnp.int32, (128, 128), 0)
        cc = lax.broadcasted_iota(jnp.int32, (128, 128), 1)
        dF = None
        for j in range(N_PAIRS):
            sel = (((rr == 0) & (cc == 2 * j)) | ((rr == HEAD_DIM) & (cc == 2 * j + 1))).astype(F32)
            d = jnp.dot(cs_ref[j], sel, precision=lax.Precision.HIGHEST, preferred_element_type=F32)
            dF = d if dF is None else dF + d
        r = lax.broadcasted_iota(jnp.int32, (tm, tm), 0)
        c = lax.broadcasted_iota(jnp.int32, (tm, tm), 1)
        tri = (c >= r).astype(F32)
        cum = c_s[...] - jnp.dot(tri, dF, precision=lax.Precision.HIGHEST, preferred_element_type=F32)
        c_s[...] = cum[0:1, :]
        _, lsn = _log_sigmoids(f_ref[...] + b_ref[...])
        rows = (nb - 1 - i) * tm + lax.broadcasted_iota(jnp.int32, (tm, 1), 0)
        dl = jnp.where(rows >= PAD, cum * jnp.exp(lsn), 0.0)
        o_ref[...] = dl
        db_ref[...] += jnp.sum(dl, axis=0, keepdims=True)

    row = pl.BlockSpec((tm, 128), lambda m: (nb - 1 - m, 0))
    vec = pl.BlockSpec((1, 128), lambda m: (0, 0))
    return pl.pallas_call(
        body, name=name, grid=(nb,),
        in_specs=[row, vec, pl.BlockSpec((N_PAIRS, tm, 128), lambda m: (0, nb - 1 - m, 0))], out_specs=[row, vec],
        out_shape=[jax.ShapeDtypeStruct((M, 128), F32), jax.ShapeDtypeStruct((1, 128), F32)],
        scratch_shapes=[pltpu.VMEM((1, 128), F32)],
        compiler_params=_params(("arbitrary",)))(f_logit, bias, colsum)


def _adamw(w, parts, m, v, name):
    R, C = w.shape
    n_parts = parts.shape[0]
    tr = R
    for d in range(8, R, 8):
        if R % d == 0 and d * C <= ADAM_TILE_ELEMS:
            tr = d
    c1 = 1.0 - ADAM_B1 ** ADAM_STEP
    c2 = 1.0 - ADAM_B2 ** ADAM_STEP

    def body(w_ref, s_ref, m_ref, v_ref, g_ref, d_ref, mo_ref, vo_ref):
        g = s_ref[0].astype(F32)
        for k in range(1, n_parts):
            g = g + s_ref[k].astype(F32)
        mn = ADAM_B1 * m_ref[...] + (1.0 - ADAM_B1) * g
        vn = ADAM_B2 * v_ref[...] + (1.0 - ADAM_B2) * (g * g)
        m_hat = mn / c1
        v_hat = vn / c2
        g_ref[...] = g
        d_ref[...] = -ADAM_LR * (m_hat / (jnp.sqrt(v_hat) + ADAM_EPS) + ADAM_WD * w_ref[...])
        mo_ref[...] = mn
        vo_ref[...] = vn

    row = pl.BlockSpec((tr, C), lambda r: (r, 0))
    shp = jax.ShapeDtypeStruct((R, C), F32)
    return pl.pallas_call(
        body, name=name, grid=(R // tr,),
        in_specs=[row, pl.BlockSpec((n_parts, tr, C), lambda r: (0, r, 0)), row, row],
        out_specs=[row, row, row, row], out_shape=[shp, shp, shp, shp],
        compiler_params=_params(("parallel",)))(w, parts, m, v)


def _position():
    return lax.axis_index("x"), lax.axis_index("y"), lax.axis_index("c")


def _all_gather(blocks, name):
    n = len(blocks)

    def body(*refs):
        x_refs, out_refs = refs[:n], refs[n:2 * n]
        send_sems, recv_sems, local_sems = refs[2 * n:]
        x, y, c = _position()
        me, sibling = (x, y, c), (x, y, 1 - c)
        chips = [(1 - x, y), (x, 1 - y), (1 - x, 1 - y)]

        def copies(k, block, to, own=False):
            slot = 4 * block[0] + 2 * block[1] + block[2]
            return [pltpu.make_async_remote_copy(
                src_ref=x_refs[p] if own else out_refs[p].at[slot], dst_ref=out_refs[p].at[slot],
                send_sem=send_sems.at[k, p], recv_sem=recv_sems.at[k, p], device_id=to, device_id_type=MESH)
                for p in range(n)]

        mine = [pltpu.make_async_copy(x_refs[p], out_refs[p].at[4 * x + 2 * y + c], local_sems.at[p]) for p in range(n)]
        for cp in mine:
            cp.start()
        first = copies(0, me, sibling, own=True)
        for j, chip in enumerate(chips):
            first += copies(1 + j, me, (*chip, c), own=True)
        for cp in first:
            cp.start()
        passed = []
        for j, chip in enumerate(chips):
            for cp in copies(1 + j, (*chip, c), me):
                cp.wait_recv()
            onward = copies(4 + j, (*chip, c), sibling)
            for cp in onward:
                cp.start()
            passed += onward
        for cp in copies(0, sibling, me):
            cp.wait_recv()
        for j, chip in enumerate(chips):
            for cp in copies(4 + j, (*chip, 1 - c), me):
                cp.wait_recv()
        for cp in first + passed:
            cp.wait_send()
        for cp in mine:
            cp.wait()

    any_spec = pl.BlockSpec(memory_space=pl.ANY)
    return pl.pallas_call(
        body, name=name, out_shape=[jax.ShapeDtypeStruct((N_DEV,) + b.shape, b.dtype) for b in blocks],
        in_specs=[any_spec] * n, out_specs=[any_spec] * n,
        scratch_shapes=[pltpu.SemaphoreType.DMA((7, n)), pltpu.SemaphoreType.DMA((7, n)), pltpu.SemaphoreType.DMA((n,))],
    )(*blocks)


N_CHIPS = 4


def _exchange_siblings(parts, name):
    n = len(parts)

    def body(*refs):
        g_refs, land_refs = refs[:n], refs[n:2 * n]
        send_sems, recv_sems = refs[2 * n:]
        x, y, c = _position()
        sibling = (x, y, 1 - c)
        sends, recvs = [], []
        for q in range(N_CHIPS):
            for p in range(n):
                sends.append(pltpu.make_async_remote_copy(
                    src_ref=g_refs[p].at[2 * q + (1 - c)], dst_ref=land_refs[p].at[q], send_sem=send_sems.at[q, p],
                    recv_sem=recv_sems.at[q, p], device_id=sibling, device_id_type=MESH))
                recvs.append(pltpu.make_async_remote_copy(
                    src_ref=g_refs[p].at[2 * q + c], dst_ref=land_refs[p].at[q], send_sem=send_sems.at[q, p],
                    recv_sem=recv_sems.at[q, p], device_id=sibling, device_id_type=MESH))
        for cp in sends:
            cp.start()
        for cp in recvs:
            cp.wait_recv()
        for cp in sends:
            cp.wait_send()

    any_spec = pl.BlockSpec(memory_space=pl.ANY)
    return pl.pallas_call(
        body, name=name, out_shape=[jax.ShapeDtypeStruct((N_CHIPS,) + p.shape[1:], p.dtype) for p in parts],
        in_specs=[any_spec] * n, out_specs=[any_spec] * n,
        scratch_shapes=[pltpu.SemaphoreType.DMA((N_CHIPS, n)), pltpu.SemaphoreType.DMA((N_CHIPS, n))],
    )(*parts)


def _pair_sum(part, from_sibling, name):
    _, R, C = part.shape
    tr = R
    for d in range(8, R, 8):
        if R % d == 0 and d * C <= ADAM_TILE_ELEMS:
            tr = d

    def body(a_ref, b_ref, o_ref):
        c = lax.axis_index("c")
        for q in range(N_CHIPS):
            o_ref[q] = (a_ref[2 * q + c].astype(F32) + b_ref[q].astype(F32)).astype(o_ref.dtype)

    return pl.pallas_call(
        body, name=name, grid=(R // tr,),
        in_specs=[pl.BlockSpec((N_DEV, tr, C), lambda r: (0, r, 0)), pl.BlockSpec((N_CHIPS, tr, C), lambda r: (0, r, 0))],
        out_specs=pl.BlockSpec((N_CHIPS, tr, C), lambda r: (0, r, 0)),
        out_shape=jax.ShapeDtypeStruct((N_CHIPS, R, C), part.dtype),
        compiler_params=_params(("parallel",)))(part, from_sibling)


def _exchange_chips(sums, name):
    n = len(sums)

    def body(*refs):
        g_refs, land_refs = refs[:n], refs[n:2 * n]
        send_sems, recv_sems, local_sems = refs[2 * n:]
        x, y, c = _position()
        me = 2 * x + y
        mine = [pltpu.make_async_copy(g_refs[p].at[me], land_refs[p].at[me], local_sems.at[p]) for p in range(n)]
        for cp in mine:
            cp.start()
        sends, recvs = [], []
        for k in range(1, N_CHIPS):
            px = 1 - x if k & 2 else x
            py = 1 - y if k & 1 else y
            peer = 2 * px + py
            for p in range(n):
                sends.append(pltpu.make_async_remote_copy(
                    src_ref=g_refs[p].at[peer], dst_ref=land_refs[p].at[me], send_sem=send_sems.at[k - 1, p],
                    recv_sem=recv_sems.at[k - 1, p], device_id=(px, py, c), device_id_type=MESH))
                recvs.append(pltpu.make_async_remote_copy(
                    src_ref=g_refs[p].at[me], dst_ref=land_refs[p].at[peer], send_sem=send_sems.at[k - 1, p],
                    recv_sem=recv_sems.at[k - 1, p], device_id=(px, py, c), device_id_type=MESH))
        for cp in sends:
            cp.start()
        for cp in recvs:
            cp.wait_recv()
        for cp in sends:
            cp.wait_send()
        for cp in mine:
            cp.wait()

    any_spec = pl.BlockSpec(memory_space=pl.ANY)
    return pl.pallas_call(
        body, name=name, out_shape=[jax.ShapeDtypeStruct(p.shape, p.dtype) for p in sums],
        in_specs=[any_spec] * n, out_specs=[any_spec] * n,
        scratch_shapes=[pltpu.SemaphoreType.DMA((3, n)), pltpu.SemaphoreType.DMA((3, n)), pltpu.SemaphoreType.DMA((n,))],
    )(*sums)


SHARDED = (("sb_w_qkv", 2), ("sb_w_o", 1), ("mla_w_down", 1), ("mla_w_uq", 2), ("mla_w_ukv", 2), ("mla_w_o", 1),
           ("fox_w_qkvf", 2), ("fox_w_o", 1), ("ffn_w_gate", 2), ("ffn_w_up", 2), ("ffn_w_down", 1),
           ("pool_w", 2), ("meta", 1), ("mla_q_norm", 1), ("mla_kv_norm", 1))
KEPT_F32 = ("meta", "mla_q_norm", "mla_kv_norm")
REPLICATED = ("norm_mix", "norm_ffn", "pool_scale", "fox_b_f", "final_norm")
WEIGHT_NAMES = ("meta", "norm_mix", "norm_ffn", "pool_w", "pool_scale", "sb_w_qkv", "sb_w_o", "mla_w_down",
                "mla_q_norm", "mla_kv_norm", "mla_w_uq", "mla_w_ukv", "mla_w_o", "fox_w_qkvf", "fox_b_f",
                "fox_w_o", "ffn_w_gate", "ffn_w_up", "ffn_w_down", "final_norm")
LANES = 1024


def _pack_rows(arrays, names):
    parts = []
    for n in names:
        flat = arrays[n].reshape(-1).astype(F32)
        rows = -(-flat.shape[0] // LANES)
        parts.append(jnp.pad(flat, (0, rows * LANES - flat.shape[0])).reshape(rows, LANES))
    rows = sum(p.shape[0] for p in parts)
    parts.append(jnp.zeros((-(-rows // 8) * 8 - rows, LANES), F32))
    return jnp.concatenate(parts, axis=0)


def _unpack_rows(buf, shapes, names):
    out, row = {}, 0
    for n in names:
        size = int(np.prod(shapes[n]))
        rows = -(-size // LANES)
        out[n] = buf[row:row + rows].reshape(-1)[:size].reshape(shapes[n])
        row += rows
    return out


def _whole_from_gathered(g, axis):
    g = jnp.moveaxis(g, 0, axis)
    shp = g.shape
    return g.reshape(shp[:axis] + (shp[axis] * shp[axis + 1],) + shp[axis + 2:])


def _parts_from_whole(whole, axis):
    shp = whole.shape
    g = whole.reshape(shp[:axis] + (N_DEV, shp[axis] // N_DEV) + shp[axis + 1:])
    return jnp.moveaxis(g, axis, 0)


def _kernel_weights(full):
    W = {}
    W["pool_w"] = full["pool_w"][0]
    W["sb_w_qkv"] = full["sb_w_qkv"][0]
    W["sb_w_o"] = full["sb_w_o"][0]
    W["mla_w_down"] = full["mla_w_down"][0]
    uq = full["mla_w_uq"][0].reshape(MLA_Q_RANK, N_HEADS, MLA_NOPE + MLA_ROPE)
    nope = uq[:, :, :MLA_NOPE].reshape(MLA_Q_RANK, N_HEADS * MLA_NOPE)
    rope = uq[:, :, MLA_NOPE:].reshape(MLA_Q_RANK, N_PAIRS, 2 * MLA_ROPE)
    rope = jnp.pad(rope, ((0, 0), (0, 0), (0, 128 - 2 * MLA_ROPE))).reshape(MLA_Q_RANK, N_PAIRS * 128)
    W["mla_w_uq"] = jnp.concatenate([nope, rope], axis=1)
    ukv = full["mla_w_ukv"][0].reshape(MLA_KV_RANK, N_HEADS, 2, HEAD_DIM)
    W["mla_w_ukv"] = jnp.transpose(ukv, (0, 2, 1, 3)).reshape(MLA_KV_RANK, 2 * N_HEADS * HEAD_DIM)
    W["mla_w_o"] = full["mla_w_o"][0]
    qkvf = full["fox_w_qkvf"][0]
    n_qkv = 3 * N_HEADS * HEAD_DIM
    W["fox_w_qkv"] = qkvf[:, :n_qkv]
    W["fox_w_f"] = jnp.pad(qkvf[:, n_qkv:], ((0, 0), (0, 128 - N_HEADS)))
    W["fox_w_qkvf"] = jnp.concatenate([W["fox_w_qkv"], W["fox_w_f"]], axis=1)
    W["fox_w_o"] = full["fox_w_o"][0]
    W["ffn_w_gate"] = full["ffn_w_gate"]
    W["ffn_w_up"] = full["ffn_w_up"]
    W["ffn_w_down"] = full["ffn_w_down"]
    return W


def _reference_grads(G):
    out = {}
    out["pool_w"] = G["pool_w"][None]
    for n in ("sb_w_qkv", "sb_w_o", "mla_w_down", "mla_w_o", "fox_w_o"):
        out[n] = G[n][None]
    duq = G["mla_w_uq"]
    nope = duq[:, :N_HEADS * MLA_NOPE].reshape(MLA_Q_RANK, N_HEADS, MLA_NOPE)
    rope = duq[:, N_HEADS * MLA_NOPE:].reshape(MLA_Q_RANK, N_PAIRS, 128)[:, :, :2 * MLA_ROPE]
    rope = rope.reshape(MLA_Q_RANK, N_HEADS, MLA_ROPE)
    out["mla_w_uq"] = jnp.concatenate([nope, rope], axis=2).reshape(1, MLA_Q_RANK, -1)
    dukv = G["mla_w_ukv"].reshape(MLA_KV_RANK, 2, N_HEADS, HEAD_DIM)
    out["mla_w_ukv"] = jnp.transpose(dukv, (0, 2, 1, 3)).reshape(1, MLA_KV_RANK, -1)
    out["fox_w_qkvf"] = G["fox_w_qkvf"][None, :, :3 * N_HEADS * HEAD_DIM + N_HEADS]
    out["ffn_w_gate"] = G["ffn_w_gate"]
    out["ffn_w_up"] = G["ffn_w_up"]
    out["ffn_w_down"] = G["ffn_w_down"]
    out["mla_q_norm"] = G["mla_q_norm"]
    out["mla_kv_norm"] = G["mla_kv_norm"]
    return out


def _pairs_col(f16):
    M = f16.shape[0]
    return jnp.transpose(f16.reshape(M, N_PAIRS, 2), (1, 0, 2))


def _pairs_row(f16):
    M = f16.shape[0]
    return jnp.transpose(f16.reshape(M, N_PAIRS, 2), (1, 2, 0))


def _local_step(x, target, W, P):
    S = x.shape[0]
    M = S + ROW0
    G = {}
    gain = lambda name, i: P[name][i][None, :]
    h0 = jnp.concatenate([jnp.zeros((PAD, D_MODEL), F32), P["meta"], x], axis=0)

    def ffn_fwd(h1, i):
        g, u, act, b = _ffn_up(h1, gain("norm_ffn", i), W["ffn_w_gate"][i], W["ffn_w_up"][i], f"ffn{i}_up")
        h2 = _mm_nn(act, W["ffn_w_down"][i], F32, f"ffn{i}_down", res=h1)
        return h2, (h1, b, g, u, act)

    def ffn_bwd(dh2, saved, i):
        h1, b, g, u, act = saved
        dg, du = _ffn_dact(dh2, W["ffn_w_down"][i], g, u, f"ffn{i}_dact")
        G.setdefault("ffn_w_down", {})[i] = _mm_tn(act, dh2, f"ffn{i}_dwd")
        G.setdefault("ffn_w_gate", {})[i] = _mm_tn(b, dg, f"ffn{i}_dwg")
        G.setdefault("ffn_w_up", {})[i] = _mm_tn(b, du, f"ffn{i}_dwu")
        dh1, dgain = _mm_nt_dnorm([(dg, W["ffn_w_gate"][i]), (du, W["ffn_w_up"][i])], h1, gain("norm_ffn", i), dh2,
                                  f"ffn{i}_db_dnorm")
        G.setdefault("norm_ffn", {})[i] = dgain
        return dh1

    a0 = _norm_fwd(h0, gain("norm_mix", 0), F32, "mix0_norm")
    h1_0, pooled = _pool_fwd(h0, a0, W["pool_w"], P["pool_scale"], "pool_fwd")
    h_1, ffn0 = ffn_fwd(h1_0, 0)

    sb_scale = HEAD_DIM ** -0.5
    sb_qkv, a1 = _norm_mm(h_1, gain("norm_mix", 1), W["sb_w_qkv"], BF16, "sb_qkv")
    sb_o, sb_tot = _sb_fwd(sb_qkv, sb_scale, "sb_fwd")
    h1_1 = _mm_nn(sb_o, W["sb_w_o"], F32, "sb_out", res=h_1)
    h_2, ffn1 = ffn_fwd(h1_1, 1)

    mla_scale = (MLA_NOPE + MLA_ROPE) ** -0.5
    cos_t, sin_t = _rope_tables(M)
    down, a2 = _norm_mm(h_2, gain("norm_mix", 2), W["mla_w_down"], F32, "mla_down")
    dq_raw = down[:, :MLA_Q_RANK]
    dkv_raw = down[:, MLA_Q_RANK:MLA_Q_RANK + MLA_KV_RANK]
    kr_raw = down[:, MLA_Q_RANK + MLA_KV_RANK:]
    q_lin, c_q = _norm_mm(dq_raw, P["mla_q_norm"], W["mla_w_uq"], F32, "mla_uq")
    q_all = _rope(q_lin, cos_t, sin_t, BF16, "mla_qrope", lead=D_MODEL)
    kv_all, c_kv = _norm_mm(dkv_raw, P["mla_kv_norm"], W["mla_w_ukv"], BF16, "mla_ukv")
    kr_in = jnp.concatenate([kr_raw, kr_raw, jnp.zeros((M, 64), F32)], axis=1)
    kr = _rope(kr_in, cos_t, sin_t, BF16, "mla_krope")
    q_rope = q_all[:, D_MODEL:]
    mla_o, mla_lse = _mla_fwd(q_all, kv_all, q_rope, kr, mla_scale, "mla_fwd")
    h1_2 = _mm_nn(mla_o, W["mla_w_o"], F32, "mla_out", res=h_2)
    h_3, ffn2 = ffn_fwd(h1_2, 2)

    fox_scale = HEAD_DIM ** -0.5
    fox_qkv, a3, fox_qkv_t = _norm_mm(h_3, gain("norm_mix", 3), W["fox_w_qkv"], BF16, "fox_qkv", transposed_too=True)
    f_logit = _mm_nn(a3, W["fox_w_f"], F32, "fox_f")
    b_f = jnp.pad(P["fox_b_f"], ((0, 0), (0, 128 - N_HEADS)))
    Fc = _forget_cumsum(f_logit, b_f, "fox_cumsum")
    f_rows, f_cols = _pairs_row(Fc[:, :N_HEADS]), _pairs_col(Fc[:, :N_HEADS])
    fox_o, fox_lse, fox_ox_t = _fox_fwd(fox_qkv, fox_qkv_t, f_rows, f_cols, fox_scale, "fox_fwd")
    h1_3 = _mm_nn(fox_o, W["fox_w_o"], F32, "fox_out", res=h_3)
    h_4, ffn3 = ffn_fwd(h1_3, 3)

    sq, dh, dgain = _loss_head(h_4, P["final_norm"][None, :], target, "loss_head")
    G["final_norm"] = dgain[0]

    dh = ffn_bwd(dh, ffn3, 3)
    do, do_t = _mm_nt(dh, W["fox_w_o"], BF16, "fox_do", transposed_too=True)
    G["fox_w_o"] = _mm_tn(fox_o, dh, "fox_dwo")
    dq, dk, dv, colsum = _fox_bwd(fox_qkv, fox_qkv_t, fox_ox_t, do, do_t, fox_lse, f_rows, f_cols, fox_scale,
                                    "fox_bwd")
    dlogit, db_f = _forget_cumsum_bwd(f_logit, b_f, colsum, "fox_dcumsum")
    G["fox_b_f"] = db_f[:, :N_HEADS]
    dproj = jnp.concatenate([dq, dk.astype(BF16), dv.astype(BF16), dlogit.astype(BF16)], axis=1)
    G["fox_w_qkvf"] = _mm_tn(a3, dproj, "fox_dwqkvf")
    dh, dgain = _mm_nt_dnorm([(dproj, W["fox_w_qkvf"])], h_3, gain("norm_mix", 3), dh, "fox_da_dnorm")
    G.setdefault("norm_mix", {})[3] = dgain

    dh = ffn_bwd(dh, ffn2, 2)
    do = _mm_nt(dh, W["mla_w_o"], BF16, "mla_do")
    G["mla_w_o"] = _mm_tn(mla_o, dh, "mla_dwo")
    dq, dk, dv, dqr, dkr = _mla_bwd(q_all, kv_all, q_rope, kr, mla_o, do, mla_lse, mla_scale, "mla_bwd")
    dqr = _rope(dqr, cos_t, sin_t, BF16, "mla_dqrope", inverse=True)
    dq_all = jnp.concatenate([dq, dqr], axis=1)
    dkr_sum = _rope(jnp.sum(dkr, axis=0), cos_t, sin_t, F32, "mla_dkrope", inverse=True)
    dkr_raw = dkr_sum[:, :MLA_ROPE] + dkr_sum[:, MLA_ROPE:2 * MLA_ROPE]
    dkv_all = jnp.concatenate([dk.astype(BF16), dv.astype(BF16)], axis=1)
    G["mla_w_uq"] = _mm_tn(c_q, dq_all, "mla_dwuq")
    G["mla_w_ukv"] = _mm_tn(c_kv, dkv_all, "mla_dwukv")
    ddq_raw, G["mla_q_norm"] = _mm_nt_dnorm([(dq_all, W["mla_w_uq"])], dq_raw, P["mla_q_norm"], None, "mla_dcq_dnorm")
    ddkv_raw, G["mla_kv_norm"] = _mm_nt_dnorm([(dkv_all, W["mla_w_ukv"])], dkv_raw, P["mla_kv_norm"], None,
                                              "mla_dckv_dnorm")
    ddown = jnp.concatenate([ddq_raw, ddkv_raw, dkr_raw], axis=1).astype(BF16)
    G["mla_w_down"] = _mm_tn(a2, ddown, "mla_dwdown")
    dh, dgain = _mm_nt_dnorm([(ddown, W["mla_w_down"])], h_2, gain("norm_mix", 2), dh, "mla_da_dnorm")
    G["norm_mix"][2] = dgain

    dh = ffn_bwd(dh, ffn1, 1)
    do = _mm_nt(dh, W["sb_w_o"], BF16, "sb_do")
    G["sb_w_o"] = _mm_tn(sb_o, dh, "sb_dwo")
    dq, dk, dv = _sb_bwd(sb_qkv, do, sb_tot, sb_scale, "sb_bwd")
    dqkv = jnp.concatenate([dq, dk.astype(BF16), dv.astype(BF16)], axis=1)
    G["sb_w_qkv"] = _mm_tn(a1, dqkv, "sb_dwqkv")
    dh, dgain = _mm_nt_dnorm([(dqkv, W["sb_w_qkv"])], h_1, gain("norm_mix", 1), dh, "sb_da_dnorm")
    G["norm_mix"][1] = dgain

    dh = ffn_bwd(dh, ffn0, 0)
    dpc, G["pool_w"], G["pool_scale"] = _pool_bwd_mix(dh, pooled, W["pool_w"], P["pool_scale"], "pool_dmix")
    da = _pool_bwd_window(dpc, "pool_dwindow")
    dh, dgain, dx = _norm_bwd(h0, gain("norm_mix", 0), da, dh, "mix0_dnorm", token_rows=True)
    G["norm_mix"][0] = dgain

    G["norm_mix"] = jnp.concatenate([G["norm_mix"][i] for i in range(DEPTH)], axis=0)
    G["norm_ffn"] = jnp.concatenate([G["norm_ffn"][i] for i in range(DEPTH)], axis=0)
    G["ffn_w_down"] = jnp.stack([G["ffn_w_down"][i] for i in range(DEPTH)])
    G["ffn_w_gate"] = jnp.stack([G["ffn_w_gate"][i] for i in range(DEPTH)])
    G["ffn_w_up"] = jnp.stack([G["ffn_w_up"][i] for i in range(DEPTH)])
    G["meta"] = dh[PAD:ROW0]
    return sq, dx, G


def kernel(x, meta, norm_mix, norm_ffn, pool_w, pool_scale, sb_w_qkv, sb_w_o, mla_w_down, mla_q_norm, mla_kv_norm, mla_w_uq, mla_w_ukv, mla_w_o, fox_w_qkvf, fox_b_f, fox_w_o, ffn_w_gate, ffn_w_up, ffn_w_down, final_norm, loss_target, m_meta, m_norm_mix, m_norm_ffn, m_pool_w, m_pool_scale, m_sb_w_qkv, m_sb_w_o, m_mla_w_down, m_mla_q_norm, m_mla_kv_norm, m_mla_w_uq, m_mla_w_ukv, m_mla_w_o, m_fox_w_qkvf, m_fox_b_f, m_fox_w_o, m_ffn_w_gate, m_ffn_w_up, m_ffn_w_down, m_final_norm, v_meta, v_norm_mix, v_norm_ffn, v_pool_w, v_pool_scale, v_sb_w_qkv, v_sb_w_o, v_mla_w_down, v_mla_q_norm, v_mla_kv_norm, v_mla_w_uq, v_mla_w_ukv, v_mla_w_o, v_fox_w_qkvf, v_fox_b_f, v_fox_w_o, v_ffn_w_gate, v_ffn_w_up, v_ffn_w_down, v_final_norm):
    w = dict(meta=meta, norm_mix=norm_mix, norm_ffn=norm_ffn, pool_w=pool_w, pool_scale=pool_scale,
             sb_w_qkv=sb_w_qkv, sb_w_o=sb_w_o, mla_w_down=mla_w_down, mla_q_norm=mla_q_norm,
             mla_kv_norm=mla_kv_norm, mla_w_uq=mla_w_uq, mla_w_ukv=mla_w_ukv, mla_w_o=mla_w_o,
             fox_w_qkvf=fox_w_qkvf, fox_b_f=fox_b_f, fox_w_o=fox_w_o, ffn_w_gate=ffn_w_gate, ffn_w_up=ffn_w_up,
             ffn_w_down=ffn_w_down, final_norm=final_norm)
    m = dict(meta=m_meta, norm_mix=m_norm_mix, norm_ffn=m_norm_ffn, pool_w=m_pool_w, pool_scale=m_pool_scale,
             sb_w_qkv=m_sb_w_qkv, sb_w_o=m_sb_w_o, mla_w_down=m_mla_w_down, mla_q_norm=m_mla_q_norm,
             mla_kv_norm=m_mla_kv_norm, mla_w_uq=m_mla_w_uq, mla_w_ukv=m_mla_w_ukv, mla_w_o=m_mla_w_o,
             fox_w_qkvf=m_fox_w_qkvf, fox_b_f=m_fox_b_f, fox_w_o=m_fox_w_o, ffn_w_gate=m_ffn_w_gate,
             ffn_w_up=m_ffn_w_up, ffn_w_down=m_ffn_w_down, final_norm=m_final_norm)
    v = dict(meta=v_meta, norm_mix=v_norm_mix, norm_ffn=v_norm_ffn, pool_w=v_pool_w, pool_scale=v_pool_scale,
             sb_w_qkv=v_sb_w_qkv, sb_w_o=v_sb_w_o, mla_w_down=v_mla_w_down, mla_q_norm=v_mla_q_norm,
             mla_kv_norm=v_mla_kv_norm, mla_w_uq=v_mla_w_uq, mla_w_ukv=v_mla_w_ukv, mla_w_o=v_mla_w_o,
             fox_w_qkvf=v_fox_w_qkvf, fox_b_f=v_fox_b_f, fox_w_o=v_fox_w_o, ffn_w_gate=v_ffn_w_gate,
             ffn_w_up=v_ffn_w_up, ffn_w_down=v_ffn_w_down, final_norm=v_final_norm)

    sh_names = tuple(n for n, _ in SHARDED)
    sh_axis = dict(SHARDED)
    shapes = {n: w[n].shape for n in WEIGHT_NAMES}
    wire = lambda n: F32 if n in KEPT_F32 else BF16

    gathered = _all_gather([w[n].astype(wire(n)) for n in sh_names], "gather_weights")
    full = {n: _whole_from_gathered(g, sh_axis[n]) for n, g in zip(sh_names, gathered)}
    W = _kernel_weights(full)
    P = dict(meta=full["meta"], mla_q_norm=full["mla_q_norm"], mla_kv_norm=full["mla_kv_norm"],
             norm_mix=norm_mix, norm_ffn=norm_ffn, pool_scale=pool_scale, fox_b_f=fox_b_f, final_norm=final_norm)

    sq, dx, G = _local_step(x[0], loss_target[0], W, P)
    loss = lax.psum(0.5 * jnp.sum(sq) / D_MODEL, ("x", "y", "c"))
    grad_x = dx[None]

    gw = _reference_grads(G)
    gw["meta"] = G["meta"]
    rc = {n: (int(np.prod(shapes[n][:-1])), shapes[n][-1]) for n in sh_names}
    parts = [_parts_from_whole(gw[n], sh_axis[n]).astype(wire(n)).reshape((N_DEV,) + rc[n]) for n in sh_names]
    from_sibling = _exchange_siblings(parts, "exchange_grads_d2d")
    sums = [_pair_sum(a, b, f"pair_sum_{n}") for n, a, b in zip(sh_names, parts, from_sibling)]
    landed = _exchange_chips(sums, "exchange_grads_ici")
    results = {}
    for n, got in zip(sh_names, landed):
        outs = _adamw(w[n].reshape(rc[n]), got, m[n].reshape(rc[n]), v[n].reshape(rc[n]), f"adamw_{n}")
        results[n] = [o.reshape(shapes[n]) for o in outs]

    rep_g = dict(norm_mix=G["norm_mix"], norm_ffn=G["norm_ffn"], pool_scale=G["pool_scale"], fox_b_f=G["fox_b_f"],
                 final_norm=G["final_norm"])
    (rep_all,) = _all_gather([_pack_rows(rep_g, REPLICATED)], "gather_replicated_grads")
    rep_out = _adamw(_pack_rows(w, REPLICATED), rep_all, _pack_rows(m, REPLICATED), _pack_rows(v, REPLICATED),
                     "adamw_replicated")
    rep = [_unpack_rows(o, shapes, REPLICATED) for o in rep_out]
    for n in REPLICATED:
        results[n] = [r[n] for r in rep]

    outs = [results[n][k] for k in range(4) for n in WEIGHT_NAMES]
    return (loss, grad_x, *outs)
```

```python
import numpy as np
import jax
import jax.numpy as jnp
from jax import lax
from jax.experimental import pallas as pl
from jax.experimental.pallas import tpu as pltpu

F32 = jnp.float32
BF16 = jnp.bfloat16

N_DEV = 8
D_MODEL = 1024
N_META = 16
PAD = 240
ROW0 = PAD + N_META
EPS = 1e-6
POOL_WINDOWS = (2, 4, 8, 16)
POOL_GROUP = 256
HALO = 128
N_HEADS = 16
HEAD_DIM = 64
N_PAIRS = N_HEADS // 2
MLA_Q_RANK = 384
MLA_KV_RANK = 256
MLA_NOPE = 64
MLA_ROPE = 32
ROPE_THETA = 10000.0
D_FF = 2816
DEPTH = 4
ATTN_TILE = 768
ATTN_BWD_TILE = 768
WALK_TILE = 256
FOX_TILE = 384
NEG = -1e30
LOG2E = 1.4426950408889634
EXP_ZERO = -110.0
VMEM_LIMIT = 56 * 2**20
ADAM_TILE_ELEMS = 192 * 1024

ADAM_LR = 0.001
ADAM_B1 = 0.9
ADAM_B2 = 0.999
ADAM_EPS = 1e-08
ADAM_WD = 0.01
ADAM_STEP = 10

MESH = pl.DeviceIdType.MESH


def _params(sem=None):
    return pltpu.CompilerParams(dimension_semantics=sem, vmem_limit_bytes=VMEM_LIMIT)


def _pick(n, cands):
    for c in cands:
        if n % c == 0:
            return c
    return n


def _col_tile(n, cap=1536):
    best = None
    for t in range(128, min(n, cap) + 1, 128):
        if n % t == 0:
            best = t
    return best if best is not None else n


def _dot(a, b):
    return jnp.dot(a, b, preferred_element_type=F32)


def _dot_nt(a, b):
    return lax.dot_general(a, b, (((1,), (1,)), ((), ())), preferred_element_type=F32)


def _dot_tn(a, b):
    return lax.dot_general(a, b, (((0,), (0,)), ((), ())), preferred_element_type=F32)


def _mm_nn(a, b, out_dtype, name, res=None):
    M, K = a.shape
    N = b.shape[1]
    tm = _pick(M, (768, 512, 256, 128))
    tn = _col_tile(N)

    def body(*refs):
        if res is None:
            a_ref, b_ref, o_ref = refs
        else:
            a_ref, b_ref, r_ref, o_ref = refs
        acc = _dot(a_ref[...].astype(BF16), b_ref[...])
        if res is not None:
            acc = acc + r_ref[...]
        o_ref[...] = acc.astype(o_ref.dtype)

    in_specs = [pl.BlockSpec((tm, K), lambda n, m: (m, 0)), pl.BlockSpec((K, tn), lambda n, m: (0, n))]
    args = [a, b]
    if res is not None:
        in_specs.append(pl.BlockSpec((tm, tn), lambda n, m: (m, n)))
        args.append(res)
    return pl.pallas_call(
        body, name=name, grid=(N // tn, M // tm), in_specs=in_specs,
        out_specs=pl.BlockSpec((tm, tn), lambda n, m: (m, n)),
        out_shape=jax.ShapeDtypeStruct((M, N), out_dtype),
        compiler_params=_params(("parallel", "parallel")))(*args)


def _norm_mm(h, gain, b, out_dtype, name, transposed_too=False):
    M, K = h.shape
    N = b.shape[1]
    tm = _pick(M, (384, 256, 128))

    def body(h_ref, g_ref, b_ref, o_ref, a_ref, *ot_ref):
        x = h_ref[...]
        r = lax.rsqrt(jnp.mean(x * x, axis=-1, keepdims=True) + EPS)
        a = ((x * r) * g_ref[...]).astype(BF16)
        a_ref[...] = a
        acc = _dot(a, b_ref[...])
        o_ref[...] = acc.astype(o_ref.dtype)
        if transposed_too:
            ot_ref[0][...] = acc.T.astype(o_ref.dtype)

    out_specs = [pl.BlockSpec((tm, N), lambda m: (m, 0)), pl.BlockSpec((tm, K), lambda m: (m, 0))]
    out_shape = [jax.ShapeDtypeStruct((M, N), out_dtype), jax.ShapeDtypeStruct((M, K), BF16)]
    if transposed_too:
        out_specs.append(pl.BlockSpec((N, tm), lambda m: (0, m)))
        out_shape.append(jax.ShapeDtypeStruct((N, M), out_dtype))
    return pl.pallas_call(
        body, name=name, grid=(M // tm,),
        in_specs=[pl.BlockSpec((tm, K), lambda m: (m, 0)), pl.BlockSpec((1, K), lambda m: (0, 0)),
                  pl.BlockSpec((K, N), lambda m: (0, 0))],
        out_specs=out_specs, out_shape=out_shape,
        compiler_params=_params(("parallel",)))(h, gain, b)


def _mm_nt(a, w, out_dtype, name, transposed_too=False):
    M, N = a.shape
    K = w.shape[0]
    tm = _pick(M, (768, 512, 256, 128)) if N <= 3200 else _pick(M, (256, 128))
    tk = _col_tile(K, 1024)

    def body(a_ref, w_ref, o_ref, *ot_ref):
        acc = _dot_nt(a_ref[...].astype(BF16), w_ref[...])
        o_ref[...] = acc.astype(o_ref.dtype)
        if transposed_too:
            ot_ref[0][...] = acc.T.astype(o_ref.dtype)

    out_specs = [pl.BlockSpec((tm, tk), lambda k, m: (m, k))]
    out_shape = [jax.ShapeDtypeStruct((M, K), out_dtype)]
    if transposed_too:
        out_specs.append(pl.BlockSpec((tk, tm), lambda k, m: (k, m)))
        out_shape.append(jax.ShapeDtypeStruct((K, M), out_dtype))
    out = pl.pallas_call(
        body, name=name, grid=(K // tk, M // tm),
        in_specs=[pl.BlockSpec((tm, N), lambda k, m: (m, 0)), pl.BlockSpec((tk, N), lambda k, m: (k, 0))],
        out_specs=out_specs, out_shape=out_shape,
        compiler_params=_params(("parallel", "parallel")))(a, w)
    return out if transposed_too else out[0]


def _mm_nt_dnorm(pairs, h, gain, dres, name):
    M, K = h.shape
    n = len(pairs)
    tm = _pick(M, (384, 256, 128))

    def body(*refs):
        refs = list(refs)
        ab, rest = refs[:2 * n], refs[2 * n:]
        h_ref, g_ref = rest[0], rest[1]
        dr_ref = rest[2] if dres is not None else None
        dh_ref, dg_ref = rest[-2], rest[-1]
        da = None
        for i in range(n):
            d = _dot_nt(ab[2 * i][...], ab[2 * i + 1][...])
            da = d if da is None else da + d
        x = h_ref[...]
        r = lax.rsqrt(jnp.mean(x * x, axis=-1, keepdims=True) + EPS)
        y = x * r
        dy = da * g_ref[...]
        dh = r * (dy - y * jnp.mean(dy * y, axis=-1, keepdims=True))
        if dr_ref is not None:
            dh = dh + dr_ref[...]
        dh_ref[...] = dh

        @pl.when(pl.program_id(0) == 0)
        def _():
            dg_ref[...] = jnp.zeros_like(dg_ref)
        dg_ref[...] += jnp.sum(da * y, axis=0, keepdims=True)

    row = pl.BlockSpec((tm, K), lambda m: (m, 0))
    vec = pl.BlockSpec((1, K), lambda m: (0, 0))
    in_specs, args = [], []
    for a, w in pairs:
        in_specs += [pl.BlockSpec((tm, a.shape[1]), lambda m: (m, 0)), pl.BlockSpec(w.shape, lambda m: (0, 0))]
        args += [a, w]
    in_specs += [row, vec] + ([row] if dres is not None else [])
    args += [h, gain] + ([dres] if dres is not None else [])
    return pl.pallas_call(
        body, name=name, grid=(M // tm,), in_specs=in_specs, out_specs=[row, vec],
        out_shape=[jax.ShapeDtypeStruct((M, K), F32), jax.ShapeDtypeStruct((1, K), F32)],
        compiler_params=_params(("arbitrary",)))(*args)


def _mm_tn(a, b, name):
    M, K = a.shape
    N = b.shape[1]
    tm = _pick(M, (768, 512, 256, 128))
    tk = _col_tile(K, 1408)
    tn = _col_tile(N, 1408)

    def body(a_ref, b_ref, o_ref):
        @pl.when(pl.program_id(2) == 0)
        def _():
            o_ref[...] = jnp.zeros_like(o_ref)
        o_ref[...] += _dot_tn(a_ref[...].astype(BF16), b_ref[...].astype(BF16))

    return pl.pallas_call(
        body, name=name, grid=(K // tk, N // tn, M // tm),
        in_specs=[pl.BlockSpec((tm, tk), lambda k, n, m: (m, k)), pl.BlockSpec((tm, tn), lambda k, n, m: (m, n))],
        out_specs=pl.BlockSpec((tk, tn), lambda k, n, m: (k, n)),
        out_shape=jax.ShapeDtypeStruct((K, N), F32),
        compiler_params=_params(("parallel", "parallel", "arbitrary")))(a, b)


def _norm_fwd(h, gain, out_dtype, name):
    M, C = h.shape
    tm = _pick(M, (768, 512, 256, 128))

    def body(h_ref, g_ref, a_ref):
        x = h_ref[...]
        r = lax.rsqrt(jnp.mean(x * x, axis=-1, keepdims=True) + EPS)
        a_ref[...] = ((x * r) * g_ref[...]).astype(a_ref.dtype)

    return pl.pallas_call(
        body, name=name, grid=(M // tm,),
        in_specs=[pl.BlockSpec((tm, C), lambda m: (m, 0)), pl.BlockSpec((1, C), lambda m: (0, 0))],
        out_specs=pl.BlockSpec((tm, C), lambda m: (m, 0)),
        out_shape=jax.ShapeDtypeStruct((M, C), out_dtype),
        compiler_params=_params(("parallel",)))(h, gain)


def _norm_bwd(h, gain, da, dres, name, token_rows=False):
    M, C = h.shape
    tm = ROW0 if token_rows else _pick(M, (768, 512, 256, 128))

    def body(*refs):
        refs = list(refs)
        dx_ref = refs.pop() if token_rows else None
        if dres is None:
            h_ref, g_ref, da_ref, dh_ref, dg_ref = refs
        else:
            h_ref, g_ref, da_ref, dr_ref, dh_ref, dg_ref = refs
        x = h_ref[...]
        r = lax.rsqrt(jnp.mean(x * x, axis=-1, keepdims=True) + EPS)
        y = x * r
        dav = da_ref[...].astype(F32)
        dy = dav * g_ref[...]
        dh = r * (dy - y * jnp.mean(dy * y, axis=-1, keepdims=True))
        if dres is not None:
            dh = dh + dr_ref[...]
        dh_ref[...] = dh
        if token_rows:
            dx_ref[...] = dh

        @pl.when(pl.program_id(0) == 0)
        def _():
            dg_ref[...] = jnp.zeros_like(dg_ref)
        dg_ref[...] += jnp.sum(dav * y, axis=0, keepdims=True)

    row = pl.BlockSpec((tm, C), lambda m: (m, 0))
    vec = pl.BlockSpec((1, C), lambda m: (0, 0))
    in_specs = [row, vec, row] + ([row] if dres is not None else [])
    args = [h, gain, da] + ([dres] if dres is not None else [])
    out_specs = [row, vec]
    out_shape = [jax.ShapeDtypeStruct((M, C), F32), jax.ShapeDtypeStruct((1, C), F32)]
    if token_rows:
        out_specs.append(pl.BlockSpec((tm, C), lambda m: (jnp.maximum(m - 1, 0), 0)))
        out_shape.append(jax.ShapeDtypeStruct((M - ROW0, C), F32))
    return pl.pallas_call(
        body, name=name, grid=(M // tm,), in_specs=in_specs, out_specs=out_specs, out_shape=out_shape,
        compiler_params=_params(("arbitrary",)))(*args)


def _ffn_up(h, gain, w_g, w_u, name):
    M, K = h.shape
    F = w_g.shape[1]
    tm = _pick(M, (384, 256, 128))

    def body(h_ref, gn_ref, wg_ref, wu_ref, g_ref, u_ref, act_ref, b_ref):
        x = h_ref[...]
        r = lax.rsqrt(jnp.mean(x * x, axis=-1, keepdims=True) + EPS)
        b = ((x * r) * gn_ref[...]).astype(BF16)
        b_ref[...] = b
        g = _dot(b, wg_ref[...])
        u = _dot(b, wu_ref[...])
        g_ref[...] = g.astype(g_ref.dtype)
        u_ref[...] = u.astype(u_ref.dtype)
        act_ref[...] = ((g * jax.nn.sigmoid(g)) * u).astype(act_ref.dtype)

    blk = pl.BlockSpec((tm, F), lambda m: (m, 0))
    wgt = pl.BlockSpec((K, F), lambda m: (0, 0))
    wide = jax.ShapeDtypeStruct((M, F), BF16)
    return pl.pallas_call(
        body, name=name, grid=(M // tm,),
        in_specs=[pl.BlockSpec((tm, K), lambda m: (m, 0)), pl.BlockSpec((1, K), lambda m: (0, 0)), wgt, wgt],
        out_specs=[blk, blk, blk, pl.BlockSpec((tm, K), lambda m: (m, 0))],
        out_shape=[wide, wide, wide, jax.ShapeDtypeStruct((M, K), BF16)],
        compiler_params=_params(("parallel",)))(h, gain, w_g, w_u)


def _ffn_dact(dy, w_d, g, u, name):
    M, K = dy.shape
    F = w_d.shape[0]
    tm = _pick(M, (384, 256, 128))

    def body(dy_ref, wd_ref, g_ref, u_ref, dg_ref, du_ref):
        dact = _dot_nt(dy_ref[...].astype(BF16), wd_ref[...])
        gv = g_ref[...].astype(F32)
        s = jax.nn.sigmoid(gv)
        silu = gv * s
        dg_ref[...] = (dact * u_ref[...].astype(F32) * (s * (1.0 + gv * (1.0 - s)))).astype(dg_ref.dtype)
        du_ref[...] = (dact * silu).astype(du_ref.dtype)

    blk = pl.BlockSpec((tm, F), lambda m: (m, 0))
    return pl.pallas_call(
        body, name=name, grid=(M // tm,),
        in_specs=[pl.BlockSpec((tm, K), lambda m: (m, 0)), pl.BlockSpec((F, K), lambda m: (0, 0)), blk, blk],
        out_specs=[blk, blk],
        out_shape=[jax.ShapeDtypeStruct((M, F), BF16), jax.ShapeDtypeStruct((M, F), BF16)],
        compiler_params=_params(("parallel",)))(dy, w_d, g, u)


def _loss_head(h, gain, target, name):
    M, C = h.shape
    tm = ROW0
    assert M % tm == 0 and target.shape[0] == M - ROW0

    def body(h_ref, g_ref, t_ref, sq_ref, dh_ref, dg_ref):
        i = pl.program_id(0)

        @pl.when(i == 0)
        def _():
            sq_ref[...] = jnp.zeros_like(sq_ref)
            dg_ref[...] = jnp.zeros_like(dg_ref)
            dh_ref[...] = jnp.zeros_like(dh_ref)

        @pl.when(i > 0)
        def _():
            x = h_ref[...]
            r = lax.rsqrt(jnp.mean(x * x, axis=-1, keepdims=True) + EPS)
            y = x * r
            err = y * g_ref[...] - t_ref[...]
            sq_ref[...] += jnp.sum(err * err, axis=0, keepdims=True)
            da = err * (1.0 / C)
            dy = da * g_ref[...]
            dh_ref[...] = r * (dy - y * jnp.mean(dy * y, axis=-1, keepdims=True))
            dg_ref[...] += jnp.sum(da * y, axis=0, keepdims=True)

    row = pl.BlockSpec((tm, C), lambda m: (m, 0))
    vec = pl.BlockSpec((1, C), lambda m: (0, 0))
    return pl.pallas_call(
        body, name=name, grid=(M // tm,),
        in_specs=[row, vec, pl.BlockSpec((tm, C), lambda m: (jnp.maximum(m - 1, 0), 0))],
        out_specs=[vec, row, vec],
        out_shape=[jax.ShapeDtypeStruct((1, C), F32), jax.ShapeDtypeStruct((M, C), F32),
                   jax.ShapeDtypeStruct((1, C), F32)],
        compiler_params=_params(("arbitrary",)))(h, gain, target)


def _split_dot(x, sel):
    hi = x.astype(BF16)
    rest = x - hi.astype(F32)
    mid = rest.astype(BF16)
    lo = (rest - mid.astype(F32)).astype(BF16)
    return _dot(hi, sel) + _dot(mid, sel) + _dot(lo, sel)


def _band_dot(band, x):
    hi = x.astype(BF16)
    rest = x - hi.astype(F32)
    mid = rest.astype(BF16)
    lo = (rest - mid.astype(F32)).astype(BF16)
    return _dot(band, hi) + _dot(band, mid) + _dot(band, lo)


def _pool_pos(row0, tm):
    return row0 + lax.broadcasted_iota(jnp.int32, (tm, 1), 0) - PAD


def _pool_fwd(h, a, w, scale, name):
    M, C = a.shape
    tm = 256
    hb = tm // HALO

    def body(h_ref, a_ref, halo_ref, w_ref, s_ref, o_ref, p_ref):
        i = pl.program_id(0)
        row0 = i * tm
        ext = jnp.concatenate([halo_ref[...], a_ref[...]], axis=0)
        src = row0 - HALO + lax.broadcasted_iota(jnp.int32, (tm + HALO, 1), 0)
        ext = jnp.where(src >= PAD, ext, 0.0)
        r = lax.broadcasted_iota(jnp.int32, (tm, tm + HALO), 0)
        c = lax.broadcasted_iota(jnp.int32, (tm, tm + HALO), 1)
        pos = _pool_pos(row0, tm)
        for g, win in enumerate(POOL_WINDOWS):
            band = ((c <= r + HALO) & (c > r + HALO - win)).astype(BF16)
            cols = slice(g * POOL_GROUP, (g + 1) * POOL_GROUP)
            xg = ext[:, cols]
            tot = _band_dot(band, xg)
            cnt = jnp.clip(pos + 1, 1, win).astype(F32)
            pooled = (tot / cnt - xg[HALO:]).astype(BF16)
            p_ref[:, cols] = pooled
            mixed = _dot(pooled, w_ref[g])
            o_ref[:, cols] = h_ref[:, cols] + mixed * s_ref[:, cols]

    row = pl.BlockSpec((tm, C), lambda m: (m, 0))
    return pl.pallas_call(
        body, name=name, grid=(M // tm,),
        in_specs=[row, row, pl.BlockSpec((HALO, C), lambda m: (jnp.maximum(m * hb - 1, 0), 0)),
                  pl.BlockSpec((4, POOL_GROUP, POOL_GROUP), lambda m: (0, 0, 0)),
                  pl.BlockSpec((1, C), lambda m: (0, 0))],
        out_specs=[row, row],
        out_shape=[jax.ShapeDtypeStruct((M, C), F32), jax.ShapeDtypeStruct((M, C), BF16)],
        compiler_params=_params(("parallel",)))(h, a, a, w, scale)


def _pool_bwd_mix(dout, pooled, w, scale, name):
    M, C = dout.shape
    tm = 256

    def body(do_ref, p_ref, w_ref, s_ref, dpc_ref, dw_ref, ds_ref):
        i = pl.program_id(0)

        @pl.when(i == 0)
        def _():
            dw_ref[...] = jnp.zeros_like(dw_ref)
            ds_ref[...] = jnp.zeros_like(ds_ref)

        pos = _pool_pos(i * tm, tm)
        for g, win in enumerate(POOL_WINDOWS):
            cols = slice(g * POOL_GROUP, (g + 1) * POOL_GROUP)
            do = do_ref[:, cols]
            pooled = p_ref[:, cols]
            mixed = _dot(pooled, w_ref[g])
            ds_ref[:, cols] += jnp.sum(do * mixed, axis=0, keepdims=True)
            dmix = (do * s_ref[:, cols]).astype(BF16)
            dw_ref[g] += _dot_tn(pooled, dmix)
            dp = _dot_nt(dmix, w_ref[g])
            cnt = jnp.clip(pos + 1, 1, win).astype(F32)
            dpc_ref[:, cols] = dp / cnt

    row = pl.BlockSpec((tm, C), lambda m: (m, 0))
    wspec = pl.BlockSpec((4, POOL_GROUP, POOL_GROUP), lambda m: (0, 0, 0))
    vec = pl.BlockSpec((1, C), lambda m: (0, 0))
    return pl.pallas_call(
        body, name=name, grid=(M // tm,),
        in_specs=[row, row, wspec, vec], out_specs=[row, wspec, vec],
        out_shape=[jax.ShapeDtypeStruct((M, C), F32), jax.ShapeDtypeStruct((4, POOL_GROUP, POOL_GROUP), F32),
                   jax.ShapeDtypeStruct((1, C), F32)],
        compiler_params=_params(("arbitrary",)))(dout, pooled, w, scale)


def _pool_bwd_window(dpc, name):
    M, C = dpc.shape
    tm = 256
    hb = tm // HALO
    last = M // HALO - 1

    def body(d_ref, halo_ref, da_ref):
        i = pl.program_id(0)
        row0 = i * tm
        ext = jnp.concatenate([d_ref[...], halo_ref[...]], axis=0)
        src = row0 + lax.broadcasted_iota(jnp.int32, (tm + HALO, 1), 0)
        ext = jnp.where(src < M, ext, 0.0)
        r = lax.broadcasted_iota(jnp.int32, (tm, tm + HALO), 0)
        c = lax.broadcasted_iota(jnp.int32, (tm, tm + HALO), 1)
        pos = _pool_pos(row0, tm)
        for g, win in enumerate(POOL_WINDOWS):
            band = ((c >= r) & (c < r + win)).astype(BF16)
            cols = slice(g * POOL_GROUP, (g + 1) * POOL_GROUP)
            xg = ext[:, cols]
            tot = _band_dot(band, xg)
            cnt = jnp.clip(pos + 1, 1, win).astype(F32)
            da_ref[:, cols] = jnp.where(pos >= 0, tot - xg[:tm] * cnt, 0.0)

    row = pl.BlockSpec((tm, C), lambda m: (m, 0))
    return pl.pallas_call(
        body, name=name, grid=(M // tm,),
        in_specs=[row, pl.BlockSpec((HALO, C), lambda m: (jnp.minimum((m + 1) * hb, last), 0))],
        out_specs=row, out_shape=jax.ShapeDtypeStruct((M, C), F32),
        compiler_params=_params(("parallel",)))(dpc, dpc)


def _head_masks():
    lane = lax.broadcasted_iota(jnp.int32, (1, 128), 1)
    return lane < HEAD_DIM, lane


def _split_heads(x, first):
    z = jnp.zeros_like(x)
    return jnp.where(first, x, z), jnp.where(first, z, x)


def _split_rope(x, lane):
    z = jnp.zeros_like(x)
    return jnp.where(lane < MLA_ROPE, x, z), jnp.where((lane >= MLA_ROPE) & (lane < 2 * MLA_ROPE), x, z)


def _walk_causal(i, step):
    def mid(kb, carry):
        step(kb, False)
        return carry

    step(0, True)
    lax.fori_loop(1, i, mid, 0)

    @pl.when(i > 0)
    def _():
        step(i, True)


def _mla_fwd(q_all, kv_all, qr, kr, scale, name):
    M = q_all.shape[0]
    t = ATTN_TILE

    def body(q_ref, k_ref, v_ref, qr_ref, kr_ref, o_ref, lse_ref, m_s, l_s, acc_s, kmax_s):
        i = pl.program_id(1)
        first, lane = _head_masks()

        @pl.when(i == 0)
        def _():
            def block_max(kb, carry):
                rows = pl.ds(pl.multiple_of(kb * t, t), t)
                kk = k_ref[rows, :].astype(F32)
                kk = kk * kk
                rr = kr_ref[rows, :].astype(F32)
                rr = jnp.sum(jnp.where(lane < MLA_ROPE, rr * rr, 0.0), axis=1, keepdims=True)
                a = jnp.max(jnp.sum(jnp.where(first, kk, 0.0), axis=1, keepdims=True) + rr)
                b = jnp.max(jnp.sum(jnp.where(first, 0.0, kk), axis=1, keepdims=True) + rr)
                return jnp.maximum(carry[0], a), jnp.maximum(carry[1], b)

            a, b = lax.fori_loop(0, M // t, block_max, (jnp.float32(0.0), jnp.float32(0.0)))
            kmax_s[0] = a
            kmax_s[1] = b

        qs = _split_heads(q_ref[...], first)
        qrs = _split_rope(qr_ref[...], lane)
        qcat = tuple(jnp.concatenate([qs[hh], qrs[hh]], axis=1) for hh in range(2))
        qpos = i * t + lax.broadcasted_iota(jnp.int32, (t, t), 0)
        kidx = lax.broadcasted_iota(jnp.int32, (t, t), 1)
        c2 = scale * LOG2E

        def run(online):
            l_s[...] = jnp.zeros_like(l_s)
            acc_s[...] = jnp.zeros_like(acc_s)
            if online:
                m_s[...] = jnp.full_like(m_s, NEG)

            def step(kb, masked):
                k0 = pl.multiple_of(kb * t, t)
                kcat = jnp.concatenate([k_ref[pl.ds(k0, t), :], kr_ref[pl.ds(k0, t), :]], axis=1)
                vs = _split_heads(v_ref[pl.ds(k0, t), :], first)
                if masked:
                    kpos = k0 + kidx
                    valid = (kpos <= qpos) & (kpos >= PAD)
                pv = None
                alphas = []
                for hh in range(2):
                    s = _dot_nt(qcat[hh], kcat)
                    if online:
                        if masked:
                            s = jnp.where(valid, s, NEG)
                        m_old = m_s[hh]
                        m_new = jnp.maximum(m_old, jnp.max(s, axis=1, keepdims=True))
                        p = jnp.exp2((s - m_new) * c2)
                        alpha = jnp.exp2((m_old - m_new) * c2)
                        l_s[hh] = alpha * l_s[hh] + jnp.sum(p, axis=1, keepdims=True)
                        m_s[hh] = m_new
                        alphas.append(alpha)
                    else:
                        p = jnp.exp2(s * c2 - m_s[hh])
                        if masked:
                            p = jnp.where(valid, p, 0.0)
                        l_s[hh] = l_s[hh] + jnp.sum(p, axis=1, keepdims=True)
                    d = _dot(p.astype(BF16), vs[hh])
                    pv = d if pv is None else pv + d
                if online:
                    acc_s[...] = acc_s[...] * jnp.where(first, alphas[0], alphas[1]) + pv
                else:
                    acc_s[...] += pv

            _walk_causal(i, step)

        for hh in range(2):
            qf = qcat[hh].astype(F32)
            m_s[hh] = (1.001 * c2) * jnp.sqrt(jnp.sum(qf * qf, axis=1, keepdims=True) * kmax_s[hh])
        run(False)
        real = i * t + lax.broadcasted_iota(jnp.int32, (t, 1), 0) >= PAD
        underflow = jnp.max(jnp.where(real & (jnp.minimum(l_s[0], l_s[1]) < 1e-30), 1.0, 0.0)) > 0.0

        @pl.when(underflow)
        def _():
            run(True)
            m_s[...] = m_s[...] * c2

        ls = tuple(jnp.where(l_s[hh] > 0.0, l_s[hh], 1.0) for hh in range(2))
        o_ref[...] = (acc_s[...] * jnp.where(first, 1.0 / ls[0], 1.0 / ls[1])).astype(o_ref.dtype)
        lse_ref[:, 0:1] = m_s[0] * (1.0 / LOG2E) + jnp.log(ls[0])
        lse_ref[:, 1:2] = m_s[1] * (1.0 / LOG2E) + jnp.log(ls[1])

    blk = pl.BlockSpec((t, 128), lambda j, i: (i, j))
    return pl.pallas_call(
        body, name=name, grid=(N_PAIRS, M // t),
        in_specs=[blk, pl.BlockSpec((M, 128), lambda j, i: (0, j)), pl.BlockSpec((M, 128), lambda j, i: (0, N_PAIRS + j)),
                  blk, pl.BlockSpec((M, 128), lambda j, i: (0, 0))],
        out_specs=[blk, pl.BlockSpec((None, t, 2), lambda j, i: (j, i, 0))],
        out_shape=[jax.ShapeDtypeStruct((M, N_PAIRS * 128), BF16), jax.ShapeDtypeStruct((N_PAIRS, M, 2), F32)],
        scratch_shapes=[pltpu.VMEM((2, t, 1), F32), pltpu.VMEM((2, t, 1), F32), pltpu.VMEM((t, 128), F32),
                        pltpu.SMEM((2,), F32)],
        compiler_params=_params(("arbitrary", "arbitrary")))(q_all, kv_all, kv_all, qr, kr)


def _mla_bwd(q_all, kv_all, qr, kr, o, do, lse, scale, name):
    M = q_all.shape[0]
    t = ATTN_BWD_TILE

    def body(q_ref, kv_hbm, qr_ref, kr_hbm, o_ref, do_ref, lse_ref,
             dq_ref, dk_hbm, dv_hbm, dqr_ref, dkr_hbm,
             k_ref, v_ref, kr_ref, dk_ref, dv_ref, dkr_ref, dq_s, lse_s, delta_s):
        j = pl.program_id(0)
        i = pl.program_id(1)
        first, lane = _head_masks()
        every = pl.ds(0, M)
        kcols = pl.ds(pl.multiple_of(j * 128, 128), 128)
        vcols = pl.ds(pl.multiple_of((N_PAIRS + j) * 128, 128), 128)

        @pl.when(i == 0)
        def _():
            pltpu.sync_copy(kv_hbm.at[every, kcols], k_ref)
            pltpu.sync_copy(kv_hbm.at[every, vcols], v_ref)
            pltpu.sync_copy(kr_hbm, kr_ref)
            dk_ref[...] = jnp.zeros_like(dk_ref)
            dv_ref[...] = jnp.zeros_like(dv_ref)
            dkr_ref[...] = jnp.zeros_like(dkr_ref)

        qs = _split_heads(q_ref[...], first)
        qrs = _split_rope(qr_ref[...], lane)
        qcat = tuple(jnp.concatenate([qs[hh], qrs[hh]], axis=1) for hh in range(2))
        dov = do_ref[...]
        dos = _split_heads(dov, first)
        prod = dov.astype(F32) * o_ref[...].astype(F32)
        deltas = (jnp.sum(jnp.where(first, prod, 0.0), axis=1, keepdims=True),
                  jnp.sum(jnp.where(first, 0.0, prod), axis=1, keepdims=True))
        for hh in range(2):
            lse_s[hh] = jnp.broadcast_to(lse_ref[:, hh:hh + 1], (t, t))
            delta_s[hh] = jnp.broadcast_to(deltas[hh], (t, t))
        dq_s[...] = jnp.zeros_like(dq_s)
        qpos = i * t + lax.broadcasted_iota(jnp.int32, (t, t), 0)
        kidx = lax.broadcasted_iota(jnp.int32, (t, t), 1)

        def step(kb, masked):
            k0 = pl.multiple_of(kb * t, t)
            rows = pl.ds(k0, t)
            k = k_ref[rows, :]
            v = v_ref[rows, :]
            kr = kr_ref[rows, :]
            kcat = jnp.concatenate([k, kr], axis=1)
            ks = _split_heads(k, first)
            krs = _split_rope(kr, lane)
            if masked:
                kpos = k0 + kidx
                valid = (kpos <= qpos) & (kpos >= PAD)
            dq = dk = dv = None
            for hh in range(2):
                s = _dot_nt(qcat[hh], kcat) * scale
                if masked:
                    s = jnp.where(valid, s, NEG)
                p = jnp.exp(s - lse_s[hh])
                ds = p * (_dot_nt(dos[hh], v) - delta_s[hh])
                dsb = (ds * scale).astype(BF16)
                a = _dot(dsb, jnp.concatenate([ks[hh], krs[hh]], axis=1))
                b = _dot_tn(dsb, qcat[hh])
                c = _dot_tn(p.astype(BF16), dos[hh])
                dq = a if dq is None else dq + a
                dk = b if dk is None else dk + b
                dv = c if dv is None else dv + c
            dq_s[...] += dq
            dk_ref[rows, :] += dk[:, :128]
            dkr_ref[rows, :] += dk[:, 128:]
            dv_ref[rows, :] += dv

        _walk_causal(i, step)
        dq_ref[...] = dq_s[:, :128].astype(dq_ref.dtype)
        dqr_ref[...] = dq_s[:, 128:].astype(dqr_ref.dtype)

        @pl.when(i == M // t - 1)
        def _():
            pltpu.sync_copy(dk_ref, dk_hbm.at[every, kcols])
            pltpu.sync_copy(dv_ref, dv_hbm.at[every, kcols])
            pltpu.sync_copy(dkr_ref, dkr_hbm.at[j])

    blk = pl.BlockSpec((t, 128), lambda j, i: (i, j))
    whole = pl.BlockSpec(memory_space=pl.ANY)
    wide = jax.ShapeDtypeStruct((M, N_PAIRS * 128), F32)
    slab = lambda dtype: pltpu.VMEM((M, 128), dtype)
    return pl.pallas_call(
        body, name=name, grid=(N_PAIRS, M // t),
        in_specs=[blk, whole, blk, whole, blk, blk, pl.BlockSpec((None, t, 2), lambda j, i: (j, i, 0))],
        out_specs=[blk, whole, whole, blk, whole],
        out_shape=[jax.ShapeDtypeStruct((M, N_PAIRS * 128), BF16), wide, wide,
                   jax.ShapeDtypeStruct((M, N_PAIRS * 128), BF16), jax.ShapeDtypeStruct((N_PAIRS, M, 128), F32)],
        scratch_shapes=[slab(BF16), slab(BF16), slab(BF16), slab(F32), slab(F32), slab(F32),
                        pltpu.VMEM((t, 256), F32), pltpu.VMEM((2, t, t), F32), pltpu.VMEM((2, t, t), F32)],
        compiler_params=_params(("arbitrary", "arbitrary")))(q_all, kv_all, qr, kr, o, do, lse)


def _tri(t, rel):
    j = lax.broadcasted_iota(jnp.int32, (t, t), 0)
    k = lax.broadcasted_iota(jnp.int32, (t, t), 1)
    m = {"gt": j > k, "le": j <= k, "lt": j < k}[rel]
    return m.astype(BF16)


def _lane_cumsum(x, tri):
    hi = x.astype(BF16)
    lo = (x - hi.astype(F32)).astype(BF16)
    return _dot(hi, tri) + _dot(lo, tri)


def _log_sigmoids(z):
    sp = jnp.log(1.0 + jnp.exp(-jnp.abs(z)))
    return jnp.minimum(z, 0.0) - sp, jnp.minimum(-z, 0.0) - sp


def _log_sigmoids_fast(z):
    lk = -(jnp.maximum(z, 0.0) + jnp.log(1.0 + jnp.exp(-jnp.abs(z))))
    return lk + z, lk


def _sb_fwd(qkv, scale, name):
    M = qkv.shape[0]
    t = WALK_TILE
    ck, cv = N_PAIRS, 2 * N_PAIRS

    def body(q_ref, k_ref, v_ref, o_ref, tot_ref, c_s, acc_s):
        i = pl.program_id(1)
        first, _ = _head_masks()
        qs = _split_heads(q_ref[...], first)
        c_s[...] = jnp.zeros_like(c_s)
        acc_s[...] = jnp.zeros_like(acc_s)
        tri = _tri(t, "gt")
        qpos = i * t + lax.broadcasted_iota(jnp.int32, (t, t), 0)
        kidx = lax.broadcasted_iota(jnp.int32, (t, t), 1)

        def step(kb, masked):
            k0 = pl.multiple_of(kb * t, t)
            k = k_ref[pl.ds(k0, t), :]
            vs = _split_heads(v_ref[pl.ds(k0, t), :], first)
            if masked:
                kpos = k0 + kidx
                valid = (kpos < qpos) & (kpos >= PAD)
            pv = None
            for hh in range(2):
                z = _dot_nt(qs[hh], k) * scale
                lb, lk = _log_sigmoids_fast(z)
                if masked:
                    lk = jnp.where(valid, lk, 0.0)
                a = jnp.exp(lb + (c_s[hh] + _lane_cumsum(lk, tri)))
                if masked:
                    a = jnp.where(valid, a, 0.0)
                c_s[hh] = c_s[hh] + jnp.sum(lk, axis=1, keepdims=True)
                d = _dot(a.astype(BF16), vs[hh])
                pv = d if pv is None else pv + d
            acc_s[...] += pv

        def keep_going():
            return jnp.max(jnp.maximum(c_s[0], c_s[1])) > EXP_ZERO

        def cond(carry):
            kb, go, _ = carry
            return (kb >= 1) & go

        def walk(carry):
            kb, _, n = carry
            step(kb, False)
            return kb - 1, keep_going(), n + 1

        step(i, True)
        _, go, n = lax.while_loop(cond, walk, (i - 1, keep_going(), jnp.int32(1)))
        first_too = go & (i > 0)

        @pl.when(first_too)
        def _():
            step(0, True)

        walked = n + first_too.astype(jnp.int32)
        o_ref[...] = acc_s[...].astype(o_ref.dtype)
        tot_ref[:, 0:1] = c_s[0]
        tot_ref[:, 1:2] = c_s[1]
        tot_ref[:, 2:3] = jnp.full((t, 1), walked.astype(F32))

    whole = lambda c0: pl.BlockSpec((M, 128), lambda j, i: (0, c0 + j))
    return pl.pallas_call(
        body, name=name, grid=(N_PAIRS, M // t),
        in_specs=[pl.BlockSpec((t, 128), lambda j, i: (i, j)), whole(ck), whole(cv)],
        out_specs=[pl.BlockSpec((t, 128), lambda j, i: (i, j)), pl.BlockSpec((None, t, 3), lambda j, i: (j, i, 0))],
        out_shape=[jax.ShapeDtypeStruct((M, N_PAIRS * 128), BF16), jax.ShapeDtypeStruct((N_PAIRS, M, 3), F32)],
        scratch_shapes=[pltpu.VMEM((2, t, 1), F32), pltpu.VMEM((t, 128), F32)],
        compiler_params=_params(("parallel", "arbitrary")))(qkv, qkv, qkv)


def _sb_bwd(qkv, do, tot, scale, name):
    M = qkv.shape[0]
    t = WALK_TILE
    ck, cv = N_PAIRS, 2 * N_PAIRS

    def body(q_ref, k_ref, v_ref, do_ref, tot_ref, dq_ref, dk_ref, dv_ref, pc_s, dc_s, dq_s):
        i = pl.program_id(1)
        first, _ = _head_masks()

        @pl.when(i == 0)
        def _():
            dk_ref[...] = jnp.zeros_like(dk_ref)
            dv_ref[...] = jnp.zeros_like(dv_ref)

        qs = _split_heads(q_ref[...], first)
        dos = _split_heads(do_ref[...], first)
        pc_s[...] = jnp.zeros_like(pc_s)
        dc_s[...] = jnp.zeros_like(dc_s)
        dq_s[...] = jnp.zeros_like(dq_s)
        tri_le = _tri(t, "le")
        tri_lt = _tri(t, "lt")
        qpos = i * t + lax.broadcasted_iota(jnp.int32, (t, t), 0)
        kidx = lax.broadcasted_iota(jnp.int32, (t, t), 1)

        def step(kb, masked):
            k0 = pl.multiple_of(kb * t, t)
            rows = pl.ds(k0, t)
            k = k_ref[rows, :]
            v = v_ref[rows, :]
            ks = _split_heads(k, first)
            if masked:
                kpos = k0 + kidx
                valid = (kpos < qpos) & (kpos >= PAD)
            dq = dk = dv = None
            for hh in range(2):
                z = _dot_nt(qs[hh], k) * scale
                lb, lk = _log_sigmoids_fast(z)
                if masked:
                    lk = jnp.where(valid, lk, 0.0)
                later = tot_ref[:, hh:hh + 1] - (pc_s[hh] + _lane_cumsum(lk, tri_le))
                a = jnp.exp(lb + later)
                if masked:
                    a = jnp.where(valid, a, 0.0)
                dl = a * _dot_nt(dos[hh], v)
                early = dc_s[hh] + _lane_cumsum(dl, tri_lt)
                sg = jnp.exp(lb)
                dz = (dl * (1.0 - sg) - early * sg) * scale
                if masked:
                    dz = jnp.where(valid, dz, 0.0)
                pc_s[hh] = pc_s[hh] + jnp.sum(lk, axis=1, keepdims=True)
                dc_s[hh] = dc_s[hh] + jnp.sum(dl, axis=1, keepdims=True)
                dzb = dz.astype(BF16)
                x = _dot(dzb, ks[hh])
                y = _dot_tn(dzb, qs[hh])
                w = _dot_tn(a.astype(BF16), dos[hh])
                dq = x if dq is None else dq + x
                dk = y if dk is None else dk + y
                dv = w if dv is None else dv + w
            dq_s[...] += dq
            dk_ref[rows, :] += dk
            dv_ref[rows, :] += dv

        first_walked = i + 1 - jnp.max(tot_ref[:, 2:3]).astype(jnp.int32)

        def mid(kb, carry):
            step(kb, False)
            return carry

        @pl.when((first_walked == 0) & (i > 0))
        def _():
            step(0, True)

        lax.fori_loop(jnp.maximum(first_walked, 1), i, mid, 0)
        step(i, True)
        dq_ref[...] = dq_s[...].astype(dq_ref.dtype)

    whole = lambda c0: pl.BlockSpec((M, 128), lambda j, i: (0, c0 + j))
    blk = pl.BlockSpec((t, 128), lambda j, i: (i, j))
    col = pl.BlockSpec((M, 128), lambda j, i: (0, j))
    return pl.pallas_call(
        body, name=name, grid=(N_PAIRS, M // t),
        in_specs=[blk, whole(ck), whole(cv), blk, pl.BlockSpec((None, t, 3), lambda j, i: (j, i, 0))],
        out_specs=[blk, col, col],
        out_shape=[jax.ShapeDtypeStruct((M, N_PAIRS * 128), BF16), jax.ShapeDtypeStruct((M, N_PAIRS * 128), F32),
                   jax.ShapeDtypeStruct((M, N_PAIRS * 128), F32)],
        scratch_shapes=[pltpu.VMEM((2, t, 1), F32), pltpu.VMEM((2, t, 1), F32), pltpu.VMEM((t, 128), F32)],
        compiler_params=_params(("parallel", "arbitrary")))(qkv, qkv, qkv, do, tot)


def _rows_between(lo, hi):
    r = lax.broadcasted_iota(jnp.int32, (128, 1), 0)
    return (r >= lo) & (r < hi)


def _lanes_between(lo, hi):
    c = lax.broadcasted_iota(jnp.int32, (1, 128), 1)
    return (c >= lo) & (c < hi)


def _keep(x, mask):
    return jnp.where(mask, x, jnp.zeros_like(x))


def _valid_mask(i, kb, t):
    kpos = kb * t + lax.broadcasted_iota(jnp.int32, (t, t), 0)
    qpos = i * t + lax.broadcasted_iota(jnp.int32, (t, t), 1)
    return (kpos <= qpos) & (kpos >= PAD)


def _fox_fwd(qkv, qkv_t, f_rows, f_cols, scale, name):
    M = qkv.shape[0]
    t = FOX_TILE
    first_blk = PAD // t
    ck, cv = N_PAIRS, 2 * N_PAIRS

    def body(qt_ref, k_ref, vt_ref, fq_ref, fk_ref, o_ref, lse_ref, ox_ref, m_s, l_s, acc_s, accx_s, kmax_s, walked_s):
        i = pl.program_id(1)

        @pl.when(i == 0)
        def _():
            first = _lanes_between(0, 64)

            def block_max(kb, carry):
                kk = k_ref[pl.ds(pl.multiple_of(kb * t, t), t), :].astype(F32)
                kk = kk * kk
                a = jnp.max(jnp.sum(jnp.where(first, kk, 0.0), axis=1, keepdims=True))
                b = jnp.max(jnp.sum(jnp.where(first, 0.0, kk), axis=1, keepdims=True))
                return jnp.maximum(carry[0], a), jnp.maximum(carry[1], b)

            a, b = lax.fori_loop(0, M // t, block_max, (jnp.float32(0.0), jnp.float32(0.0)))
            kmax_s[0] = a
            kmax_s[1] = b

        qt = qt_ref[...]
        qts = (_keep(qt, _rows_between(0, 64)), _keep(qt, _rows_between(64, 128)))
        qf = qt.astype(F32)
        qf = qf * qf
        qbound = tuple(
            (1.001 * scale) * jnp.sqrt(jnp.sum(qf[HEAD_DIM * hh:HEAD_DIM * (hh + 1)], axis=0, keepdims=True) * kmax_s[hh])
            for hh in range(2))
        def run(online):
            l_s[...] = jnp.zeros_like(l_s)
            acc_s[...] = jnp.zeros_like(acc_s)
            accx_s[...] = jnp.zeros_like(accx_s)
            if online:
                m_s[...] = jnp.full_like(m_s, NEG)

            def step(kb, masked):
                k0 = pl.multiple_of(kb * t, t)
                rows = pl.ds(k0, t)
                k = k_ref[rows, :]
                if masked:
                    valid = _valid_mask(i, kb, t)
                for hh in range(2):
                    s = _dot(k, qts[hh]) * scale + (fq_ref[hh:hh + 1, :] - fk_ref[rows, hh:hh + 1])
                    hr = slice(HEAD_DIM * hh, HEAD_DIM * (hh + 1))
                    vt = vt_ref[hr, rows]
                    if online:
                        if masked:
                            s = jnp.where(valid, s, NEG)
                        m_old = m_s[hh]
                        m_new = jnp.maximum(m_old, jnp.max(s, axis=0, keepdims=True))
                        p = jnp.exp(s - m_new)
                        alpha = jnp.exp(m_old - m_new)
                        l_s[hh] = alpha * l_s[hh] + jnp.sum(p, axis=0, keepdims=True)
                        m_s[hh] = m_new
                        pb = p.astype(BF16)
                        acc_s[hr, :] = acc_s[hr, :] * alpha + _dot(vt, pb)
                        accx_s[hr, :] = accx_s[hr, :] * alpha + _dot(vt, (p - pb.astype(F32)).astype(BF16))
                    else:
                        p = jnp.exp(s - m_s[hh])
                        if masked:
                            p = jnp.where(valid, p, 0.0)
                        l_s[hh] = l_s[hh] + jnp.sum(p, axis=0, keepdims=True)
                        pb = p.astype(BF16)
                        acc_s[hr, :] += _dot(vt, pb)
                        accx_s[hr, :] += _dot(vt, (p - pb.astype(F32)).astype(BF16))

            def keep_going(kb):
                k0 = pl.multiple_of(kb * t, t)
                worst = None
                for hh in range(2):
                    f0 = jnp.max(fk_ref[pl.ds(k0, 8), hh:hh + 1])
                    decay = fq_ref[hh:hh + 1, :] - f0
                    if online:
                        w = jnp.max(qbound[hh] + decay - m_s[hh])
                    else:
                        w = jnp.max(decay - jnp.minimum(jnp.log(jnp.maximum(l_s[hh], 1e-37)), 0.0))
                    worst = w if worst is None else jnp.maximum(worst, w)
                return worst > EXP_ZERO

            def cond(carry):
                kb, go, _ = carry
                return (kb > first_blk) & go

            def walk(carry):
                kb, _, n = carry
                step(kb, False)
                return kb - 1, keep_going(kb), n + 1

            step(i, True)
            _, go, n = lax.while_loop(cond, walk, (i - 1, keep_going(i), jnp.int32(1)))
            first_too = go & (i > first_blk)

            @pl.when(first_too)
            def _():
                step(first_blk, True)

            walked_s[0] = n + first_too.astype(jnp.int32)

        for hh in range(2):
            m_s[hh] = qbound[hh]
        run(False)
        real = i * t + lax.broadcasted_iota(jnp.int32, (1, t), 1) >= PAD
        underflow = jnp.max(jnp.where(real & (jnp.minimum(l_s[0], l_s[1]) < 1e-30), 1.0, 0.0)) > 0.0

        @pl.when(underflow)
        def _():
            run(True)

        outs = []
        for hh in range(2):
            hr = slice(HEAD_DIM * hh, HEAD_DIM * (hh + 1))
            l = jnp.where(l_s[hh] > 0.0, l_s[hh], 1.0)
            inv = 1.0 / l
            outs.append(acc_s[hr, :] * inv)
            ox_ref[hr, :] = (acc_s[hr, :] + accx_s[hr, :]) * inv
            lse_ref[hh:hh + 1, :] = m_s[hh] + jnp.log(l)
        o_ref[...] = jnp.concatenate(outs, axis=0).T.astype(o_ref.dtype)
        lse_ref[2:3, :] = jnp.full((1, t), walked_s[0].astype(F32))

    blk = pl.BlockSpec((128, t), lambda j, i: (j, i))
    stat = pl.BlockSpec((None, 2, t), lambda j, i: (j, 0, i))
    return pl.pallas_call(
        body, name=name, grid=(N_PAIRS, M // t),
        in_specs=[blk, pl.BlockSpec((M, 128), lambda j, i: (0, ck + j)), pl.BlockSpec((128, M), lambda j, i: (cv + j, 0)),
                  stat, pl.BlockSpec((None, M, 2), lambda j, i: (j, 0, 0))],
        out_specs=[pl.BlockSpec((t, 128), lambda j, i: (i, j)), pl.BlockSpec((None, 3, t), lambda j, i: (j, 0, i)), blk],
        out_shape=[jax.ShapeDtypeStruct((M, N_PAIRS * 128), BF16), jax.ShapeDtypeStruct((N_PAIRS, 3, M), F32),
                   jax.ShapeDtypeStruct((N_PAIRS * 128, M), F32)],
        scratch_shapes=[pltpu.VMEM((2, 1, t), F32), pltpu.VMEM((2, 1, t), F32), pltpu.VMEM((128, t), F32),
                        pltpu.VMEM((128, t), F32), pltpu.SMEM((2,), F32), pltpu.SMEM((1,), jnp.int32)],
        compiler_params=_params(("arbitrary", "arbitrary")))(qkv_t, qkv, qkv_t, f_rows, f_cols)


def _fox_bwd(qkv, qkv_t, o_t, do, do_t, lse, f_rows, f_cols, scale, name):
    assert np.frexp(scale)[0] == 0.5, "the key sums ride in the dK matmul: the scale must be a power of two"
    M = qkv.shape[0]
    t = FOX_TILE
    first_blk = PAD // t
    ck, cv = N_PAIRS, 2 * N_PAIRS

    def body(q_ref, qt_ref, k_ref, kt_ref, v_ref, ot_ref, do_ref, dot_ref, lse_ref, fq_ref, fk_ref,
             dq_ref, dk_ref, dv_ref, cs_ref, dq_s):
        i = pl.program_id(1)

        @pl.when(i == 0)
        def _():
            dk_ref[...] = jnp.zeros_like(dk_ref)
            dv_ref[...] = jnp.zeros_like(dv_ref)
            cs_ref[...] = jnp.zeros_like(cs_ref)

        heads_l = (_lanes_between(0, 64), _lanes_between(64, 128))
        heads_r = (_rows_between(0, 64), _rows_between(64, 128))
        q = q_ref[...]
        qt = qt_ref[...]
        do = do_ref[...]
        dot = dot_ref[...]
        q_ones = tuple(jnp.where(m, q, jnp.ones_like(q)) for m in heads_l)
        qts = tuple(_keep(qt, m) for m in heads_r)
        dos = tuple(_keep(do, m) for m in heads_l)
        dots = tuple(_keep(dot, m) for m in heads_r)
        prod = dot.astype(F32) * ot_ref[...]
        deltas = tuple(jnp.sum(prod[HEAD_DIM * hh:HEAD_DIM * (hh + 1)], axis=0, keepdims=True) for hh in range(2))
        ones = tuple(m.astype(BF16) * jnp.ones((t, 128), BF16) for m in heads_l)
        dq_s[...] = jnp.zeros_like(dq_s)

        def step(kb, masked):
            k0 = pl.multiple_of(kb * t, t)
            rows = pl.ds(k0, t)
            k = k_ref[rows, :]
            v = v_ref[rows, :]
            if masked:
                valid = _valid_mask(i, kb, t)
            dk = dv = cs = None
            for hh in range(2):
                s = _dot(k, qts[hh]) * scale + (fq_ref[hh:hh + 1, :] - fk_ref[rows, hh:hh + 1])
                if masked:
                    s = jnp.where(valid, s, NEG)
                p = jnp.exp(s - lse_ref[hh:hh + 1, :])
                ds = p * (_dot(v, dots[hh]) - deltas[hh])
                hi = ds.astype(BF16)
                lo = (ds - hi.astype(F32)).astype(BF16)
                dsb = (ds * scale).astype(BF16)
                hr = slice(HEAD_DIM * hh, HEAD_DIM * (hh + 1))
                dq_s[hr, :] += _dot(kt_ref[hr, rows], dsb)
                a = _dot(dsb, q_ones[hh])
                c = jnp.where(heads_l[hh], pltpu.roll(a, HEAD_DIM, 1) * (1.0 / scale), 0.0) + _dot(lo, ones[hh])
                a = jnp.where(heads_l[hh], a, 0.0)
                b = _dot(p.astype(BF16), dos[hh])
                dk = a if dk is None else dk + a
                dv = b if dv is None else dv + b
                cs = c if cs is None else cs + c
            dk_ref[rows, :] += dk
            dv_ref[rows, :] += dv
            cs_ref[rows, :] += cs

        first_walked = i + 1 - jnp.max(lse_ref[2:3, :]).astype(jnp.int32)

        def mid(kb, carry):
            step(kb, False)
            return carry

        @pl.when((first_walked == first_blk) & (i > first_blk))
        def _():
            step(first_blk, True)

        lax.fori_loop(jnp.maximum(first_walked, first_blk + 1), i, mid, 0)
        step(i, True)
        dq_ref[...] = dq_s[...].T.astype(dq_ref.dtype)

    rblk = pl.BlockSpec((t, 128), lambda j, i: (i, j))
    tblk = pl.BlockSpec((128, t), lambda j, i: (j, i))
    stat = pl.BlockSpec((None, 2, t), lambda j, i: (j, 0, i))
    stat3 = pl.BlockSpec((None, 3, t), lambda j, i: (j, 0, i))
    col = pl.BlockSpec((M, 128), lambda j, i: (0, j))
    wide = jax.ShapeDtypeStruct((M, N_PAIRS * 128), F32)
    return pl.pallas_call(
        body, name=name, grid=(N_PAIRS, M // t),
        in_specs=[rblk, tblk, pl.BlockSpec((M, 128), lambda j, i: (0, ck + j)),
                  pl.BlockSpec((128, M), lambda j, i: (ck + j, 0)), pl.BlockSpec((M, 128), lambda j, i: (0, cv + j)),
                  tblk, rblk, tblk, stat3, stat, pl.BlockSpec((None, M, 2), lambda j, i: (j, 0, 0))],
        out_specs=[rblk, col, col, pl.BlockSpec((None, M, 128), lambda j, i: (j, 0, 0))],
        out_shape=[jax.ShapeDtypeStruct((M, N_PAIRS * 128), BF16), wide, wide,
                   jax.ShapeDtypeStruct((N_PAIRS, M, 128), F32)],
        scratch_shapes=[pltpu.VMEM((128, t), F32)],
        compiler_params=_params(("parallel", "arbitrary")))(qkv, qkv_t, qkv, qkv_t, qkv, o_t, do, do_t, lse,
                                                            f_rows, f_cols)


def _rope_tables(M):
    pos = (jnp.arange(M, dtype=jnp.int32) - PAD).astype(F32)
    inv = ROPE_THETA ** (-jnp.arange(0, MLA_ROPE, 2, dtype=F32) / MLA_ROPE)
    ang = pos[:, None] * inv[None, :]
    cos, sin = jnp.cos(ang), jnp.sin(ang)
    z = jnp.zeros((M, 64), F32)
    cos_t = jnp.concatenate([cos, cos, cos, cos, z], axis=1)
    sin_t = jnp.concatenate([-sin, sin, -sin, sin, z], axis=1)
    return cos_t, sin_t


def _rope(x, cos_t, sin_t, out_dtype, name, inverse=False, lead=0):
    M, C = x.shape
    tm = _pick(M, (768, 512, 256, 128))
    nblk = (C - lead) // 128
    sign = -1.0 if inverse else 1.0

    def body(x_ref, c_ref, s_ref, o_ref):
        lane = lax.broadcasted_iota(jnp.int32, (1, 128), 1)
        low = (lane % MLA_ROPE) < (MLA_ROPE // 2)
        cos = c_ref[...]
        sin = s_ref[...] * sign
        if lead:
            o_ref[:, :lead] = x_ref[:, :lead].astype(o_ref.dtype)
        for b in range(nblk):
            cols = slice(lead + b * 128, lead + (b + 1) * 128)
            v = x_ref[:, cols].astype(F32)
            up = pltpu.roll(v, 128 - MLA_ROPE // 2, 1)
            down = pltpu.roll(v, MLA_ROPE // 2, 1)
            o_ref[:, cols] = (v * cos + jnp.where(low, up, down) * sin).astype(o_ref.dtype)

    row = pl.BlockSpec((tm, C), lambda m: (m, 0))
    tab = pl.BlockSpec((tm, 128), lambda m: (m, 0))
    return pl.pallas_call(
        body, name=name, grid=(M // tm,), in_specs=[row, tab, tab], out_specs=row,
        out_shape=jax.ShapeDtypeStruct((M, C), out_dtype),
        compiler_params=_params(("parallel",)))(x, cos_t, sin_t)


def _forget_cumsum(f_logit, bias, name):
    M = f_logit.shape[0]
    tm = 256

    def body(f_ref, b_ref, o_ref, c_s):
        i = pl.program_id(0)

        @pl.when(i == 0)
        def _():
            c_s[...] = jnp.zeros_like(c_s)
        ls, _ = _log_sigmoids(f_ref[...] + b_ref[...])
        rows = i * tm + lax.broadcasted_iota(jnp.int32, (tm, 1), 0)
        ls = jnp.where(rows >= PAD, ls, 0.0)
        r = lax.broadcasted_iota(jnp.int32, (tm, tm), 0)
        c = lax.broadcasted_iota(jnp.int32, (tm, tm), 1)
        tri = (c <= r).astype(F32)
        cum = jnp.dot(tri, ls, precision=lax.Precision.HIGHEST, preferred_element_type=F32) + c_s[...]
        o_ref[...] = cum
        c_s[...] = cum[tm - 1:tm, :]

    row = pl.BlockSpec((tm, 128), lambda m: (m, 0))
    return pl.pallas_call(
        body, name=name, grid=(M // tm,),
        in_specs=[row, pl.BlockSpec((1, 128), lambda m: (0, 0))], out_specs=row,
        out_shape=jax.ShapeDtypeStruct((M, 128), F32), scratch_shapes=[pltpu.VMEM((1, 128), F32)],
        compiler_params=_params(("arbitrary",)))(f_logit, bias)


def _forget_cumsum_bwd(f_logit, bias, colsum, name):
    M = f_logit.shape[0]
    tm = 256
    nb = M // tm

    def body(f_ref, b_ref, cs_ref, o_ref, db_ref, c_s):
        i = pl.program_id(0)

        @pl.when(i == 0)
        def _():
            c_s[...] = jnp.zeros_like(c_s)
            db_ref[...] = jnp.zeros_like(db_ref)
        rr = lax.broadcasted_iota(jnp.int32, (128, 128), 0)
        cc = lax.broadcasted_iota(jnp.int32, (128, 128), 1)
        dF = None
        for j in range(N_PAIRS):
            sel = (((rr == 0) & (cc == 2 * j)) | ((rr == HEAD_DIM) & (cc == 2 * j + 1))).astype(BF16)
            d = _split_dot(cs_ref[j], sel)
            dF = d if dF is None else dF + d
        r = lax.broadcasted_iota(jnp.int32, (tm, tm), 0)
        c = lax.broadcasted_iota(jnp.int32, (tm, tm), 1)
        tri = (c >= r).astype(F32)
        cum = c_s[...] - jnp.dot(tri, dF, precision=lax.Precision.HIGHEST, preferred_element_type=F32)
        c_s[...] = cum[0:1, :]
        _, lsn = _log_sigmoids(f_ref[...] + b_ref[...])
        rows = (nb - 1 - i) * tm + lax.broadcasted_iota(jnp.int32, (tm, 1), 0)
        dl = jnp.where(rows >= PAD, cum * jnp.exp(lsn), 0.0)
        o_ref[...] = dl
        db_ref[...] += jnp.sum(dl, axis=0, keepdims=True)

    row = pl.BlockSpec((tm, 128), lambda m: (nb - 1 - m, 0))
    vec = pl.BlockSpec((1, 128), lambda m: (0, 0))
    return pl.pallas_call(
        body, name=name, grid=(nb,),
        in_specs=[row, vec, pl.BlockSpec((N_PAIRS, tm, 128), lambda m: (0, nb - 1 - m, 0))], out_specs=[row, vec],
        out_shape=[jax.ShapeDtypeStruct((M, 128), F32), jax.ShapeDtypeStruct((1, 128), F32)],
        scratch_shapes=[pltpu.VMEM((1, 128), F32)],
        compiler_params=_params(("arbitrary",)))(f_logit, bias, colsum)


def _adamw(w, parts, m, v, name):
    R, C = w.shape
    n_parts = parts.shape[0]
    tr = R
    for d in range(8, R, 8):
        if R % d == 0 and d * C <= ADAM_TILE_ELEMS:
            tr = d
    c1 = 1.0 - ADAM_B1 ** ADAM_STEP
    c2 = 1.0 - ADAM_B2 ** ADAM_STEP

    def body(w_ref, s_ref, m_ref, v_ref, g_ref, d_ref, mo_ref, vo_ref):
        g = s_ref[0].astype(F32)
        for k in range(1, n_parts):
            g = g + s_ref[k].astype(F32)
        mn = ADAM_B1 * m_ref[...] + (1.0 - ADAM_B1) * g
        vn = ADAM_B2 * v_ref[...] + (1.0 - ADAM_B2) * (g * g)
        m_hat = mn / c1
        v_hat = vn / c2
        g_ref[...] = g
        d_ref[...] = -ADAM_LR * (m_hat / (jnp.sqrt(v_hat) + ADAM_EPS) + ADAM_WD * w_ref[...])
        mo_ref[...] = mn
        vo_ref[...] = vn

    row = pl.BlockSpec((tr, C), lambda r: (r, 0))
    shp = jax.ShapeDtypeStruct((R, C), F32)
    return pl.pallas_call(
        body, name=name, grid=(R // tr,),
        in_specs=[row, pl.BlockSpec((n_parts, tr, C), lambda r: (0, r, 0)), row, row],
        out_specs=[row, row, row, row], out_shape=[shp, shp, shp, shp],
        compiler_params=_params(("parallel",)))(w, parts, m, v)


def _position():
    return lax.axis_index("x"), lax.axis_index("y"), lax.axis_index("c")


def _all_gather(blocks, name):
    n = len(blocks)

    def body(*refs):
        x_refs, out_refs = refs[:n], refs[n:2 * n]
        send_sems, recv_sems, local_sems = refs[2 * n:]
        x, y, c = _position()
        me, sibling = (x, y, c), (x, y, 1 - c)
        chips = [(1 - x, y), (x, 1 - y), (1 - x, 1 - y)]

        def copies(k, block, to, own=False):
            slot = 4 * block[0] + 2 * block[1] + block[2]
            return [pltpu.make_async_remote_copy(
                src_ref=x_refs[p] if own else out_refs[p].at[slot], dst_ref=out_refs[p].at[slot],
                send_sem=send_sems.at[k, p], recv_sem=recv_sems.at[k, p], device_id=to, device_id_type=MESH)
                for p in range(n)]

        mine = [pltpu.make_async_copy(x_refs[p], out_refs[p].at[4 * x + 2 * y + c], local_sems.at[p]) for p in range(n)]
        for cp in mine:
            cp.start()
        first = copies(0, me, sibling, own=True)
        for j, chip in enumerate(chips):
            first += copies(1 + j, me, (*chip, c), own=True)
        for cp in first:
            cp.start()
        passed = []
        for j, chip in enumerate(chips):
            for cp in copies(1 + j, (*chip, c), me):
                cp.wait_recv()
            onward = copies(4 + j, (*chip, c), sibling)
            for cp in onward:
                cp.start()
            passed += onward
        for cp in copies(0, sibling, me):
            cp.wait_recv()
        for j, chip in enumerate(chips):
            for cp in copies(4 + j, (*chip, 1 - c), me):
                cp.wait_recv()
        for cp in first + passed:
            cp.wait_send()
        for cp in mine:
            cp.wait()

    any_spec = pl.BlockSpec(memory_space=pl.ANY)
    return pl.pallas_call(
        body, name=name, out_shape=[jax.ShapeDtypeStruct((N_DEV,) + b.shape, b.dtype) for b in blocks],
        in_specs=[any_spec] * n, out_specs=[any_spec] * n,
        scratch_shapes=[pltpu.SemaphoreType.DMA((7, n)), pltpu.SemaphoreType.DMA((7, n)), pltpu.SemaphoreType.DMA((n,))],
    )(*blocks)


N_CHIPS = 4


def _exchange_siblings(parts, name):
    n = len(parts)

    def body(*refs):
        g_refs, land_refs = refs[:n], refs[n:2 * n]
        send_sems, recv_sems = refs[2 * n:]
        x, y, c = _position()
        sibling = (x, y, 1 - c)
        sends, recvs = [], []
        for q in range(N_CHIPS):
            for p in range(n):
                sends.append(pltpu.make_async_remote_copy(
                    src_ref=g_refs[p].at[2 * q + (1 - c)], dst_ref=land_refs[p].at[q], send_sem=send_sems.at[q, p],
                    recv_sem=recv_sems.at[q, p], device_id=sibling, device_id_type=MESH))
                recvs.append(pltpu.make_async_remote_copy(
                    src_ref=g_refs[p].at[2 * q + c], dst_ref=land_refs[p].at[q], send_sem=send_sems.at[q, p],
                    recv_sem=recv_sems.at[q, p], device_id=sibling, device_id_type=MESH))
        for cp in sends:
            cp.start()
        for cp in recvs:
            cp.wait_recv()
        for cp in sends:
            cp.wait_send()

    any_spec = pl.BlockSpec(memory_space=pl.ANY)
    return pl.pallas_call(
        body, name=name, out_shape=[jax.ShapeDtypeStruct((N_CHIPS,) + p.shape[1:], p.dtype) for p in parts],
        in_specs=[any_spec] * n, out_specs=[any_spec] * n,
        scratch_shapes=[pltpu.SemaphoreType.DMA((N_CHIPS, n)), pltpu.SemaphoreType.DMA((N_CHIPS, n))],
    )(*parts)


def _pair_sum(part, from_sibling, name):
    _, R, C = part.shape
    tr = R
    for d in range(8, R, 8):
        if R % d == 0 and d * C <= ADAM_TILE_ELEMS:
            tr = d

    def body(a_ref, b_ref, o_ref):
        c = lax.axis_index("c")
        for q in range(N_CHIPS):
            o_ref[q] = (a_ref[2 * q + c].astype(F32) + b_ref[q].astype(F32)).astype(o_ref.dtype)

    return pl.pallas_call(
        body, name=name, grid=(R // tr,),
        in_specs=[pl.BlockSpec((N_DEV, tr, C), lambda r: (0, r, 0)), pl.BlockSpec((N_CHIPS, tr, C), lambda r: (0, r, 0))],
        out_specs=pl.BlockSpec((N_CHIPS, tr, C), lambda r: (0, r, 0)),
        out_shape=jax.ShapeDtypeStruct((N_CHIPS, R, C), part.dtype),
        compiler_params=_params(("parallel",)))(part, from_sibling)


def _exchange_chips(sums, name):
    n = len(sums)

    def body(*refs):
        g_refs, land_refs = refs[:n], refs[n:2 * n]
        send_sems, recv_sems, local_sems = refs[2 * n:]
        x, y, c = _position()
        me = 2 * x + y
        mine = [pltpu.make_async_copy(g_refs[p].at[me], land_refs[p].at[me], local_sems.at[p]) for p in range(n)]
        for cp in mine:
            cp.start()
        sends, recvs = [], []
        for k in range(1, N_CHIPS):
            px = 1 - x if k & 2 else x
            py = 1 - y if k & 1 else y
            peer = 2 * px + py
            for p in range(n):
                sends.append(pltpu.make_async_remote_copy(
                    src_ref=g_refs[p].at[peer], dst_ref=land_refs[p].at[me], send_sem=send_sems.at[k - 1, p],
                    recv_sem=recv_sems.at[k - 1, p], device_id=(px, py, c), device_id_type=MESH))
                recvs.append(pltpu.make_async_remote_copy(
                    src_ref=g_refs[p].at[me], dst_ref=land_refs[p].at[peer], send_sem=send_sems.at[k - 1, p],
                    recv_sem=recv_sems.at[k - 1, p], device_id=(px, py, c), device_id_type=MESH))
        for cp in sends:
            cp.start()
        for cp in recvs:
            cp.wait_recv()
        for cp in sends:
            cp.wait_send()
        for cp in mine:
            cp.wait()

    any_spec = pl.BlockSpec(memory_space=pl.ANY)
    return pl.pallas_call(
        body, name=name, out_shape=[jax.ShapeDtypeStruct(p.shape, p.dtype) for p in sums],
        in_specs=[any_spec] * n, out_specs=[any_spec] * n,
        scratch_shapes=[pltpu.SemaphoreType.DMA((3, n)), pltpu.SemaphoreType.DMA((3, n)), pltpu.SemaphoreType.DMA((n,))],
    )(*sums)


SHARDED = (("sb_w_qkv", 2), ("sb_w_o", 1), ("mla_w_down", 1), ("mla_w_uq", 2), ("mla_w_ukv", 2), ("mla_w_o", 1),
           ("fox_w_qkvf", 2), ("fox_w_o", 1), ("ffn_w_gate", 2), ("ffn_w_up", 2), ("ffn_w_down", 1),
           ("pool_w", 2), ("meta", 1), ("mla_q_norm", 1), ("mla_kv_norm", 1))
KEPT_F32 = ("meta", "mla_q_norm", "mla_kv_norm")
REPLICATED = ("norm_mix", "norm_ffn", "pool_scale", "fox_b_f", "final_norm")
WEIGHT_NAMES = ("meta", "norm_mix", "norm_ffn", "pool_w", "pool_scale", "sb_w_qkv", "sb_w_o", "mla_w_down",
                "mla_q_norm", "mla_kv_norm", "mla_w_uq", "mla_w_ukv", "mla_w_o", "fox_w_qkvf", "fox_b_f",
                "fox_w_o", "ffn_w_gate", "ffn_w_up", "ffn_w_down", "final_norm")
LANES = 1024


def _pack_rows(arrays, names):
    parts = []
    for n in names:
        flat = arrays[n].reshape(-1).astype(F32)
        rows = -(-flat.shape[0] // LANES)
        parts.append(jnp.pad(flat, (0, rows * LANES - flat.shape[0])).reshape(rows, LANES))
    rows = sum(p.shape[0] for p in parts)
    parts.append(jnp.zeros((-(-rows // 8) * 8 - rows, LANES), F32))
    return jnp.concatenate(parts, axis=0)


def _unpack_rows(buf, shapes, names):
    out, row = {}, 0
    for n in names:
        size = int(np.prod(shapes[n]))
        rows = -(-size // LANES)
        out[n] = buf[row:row + rows].reshape(-1)[:size].reshape(shapes[n])
        row += rows
    return out


def _whole_from_gathered(g, axis):
    g = jnp.moveaxis(g, 0, axis)
    shp = g.shape
    return g.reshape(shp[:axis] + (shp[axis] * shp[axis + 1],) + shp[axis + 2:])


def _parts_from_whole(whole, axis):
    shp = whole.shape
    g = whole.reshape(shp[:axis] + (N_DEV, shp[axis] // N_DEV) + shp[axis + 1:])
    return jnp.moveaxis(g, axis, 0)


def _kernel_weights(full):
    W = {}
    W["pool_w"] = full["pool_w"][0]
    W["sb_w_qkv"] = full["sb_w_qkv"][0]
    W["sb_w_o"] = full["sb_w_o"][0]
    W["mla_w_down"] = full["mla_w_down"][0]
    uq = full["mla_w_uq"][0].reshape(MLA_Q_RANK, N_HEADS, MLA_NOPE + MLA_ROPE)
    nope = uq[:, :, :MLA_NOPE].reshape(MLA_Q_RANK, N_HEADS * MLA_NOPE)
    rope = uq[:, :, MLA_NOPE:].reshape(MLA_Q_RANK, N_PAIRS, 2 * MLA_ROPE)
    rope = jnp.pad(rope, ((0, 0), (0, 0), (0, 128 - 2 * MLA_ROPE))).reshape(MLA_Q_RANK, N_PAIRS * 128)
    W["mla_w_uq"] = jnp.concatenate([nope, rope], axis=1)
    ukv = full["mla_w_ukv"][0].reshape(MLA_KV_RANK, N_HEADS, 2, HEAD_DIM)
    W["mla_w_ukv"] = jnp.transpose(ukv, (0, 2, 1, 3)).reshape(MLA_KV_RANK, 2 * N_HEADS * HEAD_DIM)
    W["mla_w_o"] = full["mla_w_o"][0]
    qkvf = full["fox_w_qkvf"][0]
    n_qkv = 3 * N_HEADS * HEAD_DIM
    W["fox_w_qkv"] = qkvf[:, :n_qkv]
    W["fox_w_f"] = jnp.pad(qkvf[:, n_qkv:], ((0, 0), (0, 128 - N_HEADS)))
    W["fox_w_qkvf"] = jnp.concatenate([W["fox_w_qkv"], W["fox_w_f"]], axis=1)
    W["fox_w_o"] = full["fox_w_o"][0]
    W["ffn_w_gate"] = full["ffn_w_gate"]
    W["ffn_w_up"] = full["ffn_w_up"]
    W["ffn_w_down"] = full["ffn_w_down"]
    return W


def _reference_grads(G):
    out = {}
    out["pool_w"] = G["pool_w"][None]
    for n in ("sb_w_qkv", "sb_w_o", "mla_w_down", "mla_w_o", "fox_w_o"):
        out[n] = G[n][None]
    duq = G["mla_w_uq"]
    nope = duq[:, :N_HEADS * MLA_NOPE].reshape(MLA_Q_RANK, N_HEADS, MLA_NOPE)
    rope = duq[:, N_HEADS * MLA_NOPE:].reshape(MLA_Q_RANK, N_PAIRS, 128)[:, :, :2 * MLA_ROPE]
    rope = rope.reshape(MLA_Q_RANK, N_HEADS, MLA_ROPE)
    out["mla_w_uq"] = jnp.concatenate([nope, rope], axis=2).reshape(1, MLA_Q_RANK, -1)
    dukv = G["mla_w_ukv"].reshape(MLA_KV_RANK, 2, N_HEADS, HEAD_DIM)
    out["mla_w_ukv"] = jnp.transpose(dukv, (0, 2, 1, 3)).reshape(1, MLA_KV_RANK, -1)
    out["fox_w_qkvf"] = G["fox_w_qkvf"][None, :, :3 * N_HEADS * HEAD_DIM + N_HEADS]
    out["ffn_w_gate"] = G["ffn_w_gate"]
    out["ffn_w_up"] = G["ffn_w_up"]
    out["ffn_w_down"] = G["ffn_w_down"]
    out["mla_q_norm"] = G["mla_q_norm"]
    out["mla_kv_norm"] = G["mla_kv_norm"]
    return out


def _pairs_col(f16):
    M = f16.shape[0]
    return jnp.transpose(f16.reshape(M, N_PAIRS, 2), (1, 0, 2))


def _pairs_row(f16):
    M = f16.shape[0]
    return jnp.transpose(f16.reshape(M, N_PAIRS, 2), (1, 2, 0))


def _local_step(x, target, W, P):
    S = x.shape[0]
    M = S + ROW0
    G = {}
    gain = lambda name, i: P[name][i][None, :]
    h0 = jnp.concatenate([jnp.zeros((PAD, D_MODEL), F32), P["meta"], x], axis=0)

    def ffn_fwd(h1, i):
        g, u, act, b = _ffn_up(h1, gain("norm_ffn", i), W["ffn_w_gate"][i], W["ffn_w_up"][i], f"ffn{i}_up")
        h2 = _mm_nn(act, W["ffn_w_down"][i], F32, f"ffn{i}_down", res=h1)
        return h2, (h1, b, g, u, act)

    def ffn_bwd(dh2, saved, i):
        h1, b, g, u, act = saved
        dg, du = _ffn_dact(dh2, W["ffn_w_down"][i], g, u, f"ffn{i}_dact")
        G.setdefault("ffn_w_down", {})[i] = _mm_tn(act, dh2, f"ffn{i}_dwd")
        G.setdefault("ffn_w_gate", {})[i] = _mm_tn(b, dg, f"ffn{i}_dwg")
        G.setdefault("ffn_w_up", {})[i] = _mm_tn(b, du, f"ffn{i}_dwu")
        dh1, dgain = _mm_nt_dnorm([(dg, W["ffn_w_gate"][i]), (du, W["ffn_w_up"][i])], h1, gain("norm_ffn", i), dh2,
                                  f"ffn{i}_db_dnorm")
        G.setdefault("norm_ffn", {})[i] = dgain
        return dh1

    a0 = _norm_fwd(h0, gain("norm_mix", 0), F32, "mix0_norm")
    h1_0, pooled = _pool_fwd(h0, a0, W["pool_w"], P["pool_scale"], "pool_fwd")
    h_1, ffn0 = ffn_fwd(h1_0, 0)

    sb_scale = HEAD_DIM ** -0.5
    sb_qkv, a1 = _norm_mm(h_1, gain("norm_mix", 1), W["sb_w_qkv"], BF16, "sb_qkv")
    sb_o, sb_tot = _sb_fwd(sb_qkv, sb_scale, "sb_fwd")
    h1_1 = _mm_nn(sb_o, W["sb_w_o"], F32, "sb_out", res=h_1)
    h_2, ffn1 = ffn_fwd(h1_1, 1)

    mla_scale = (MLA_NOPE + MLA_ROPE) ** -0.5
    cos_t, sin_t = _rope_tables(M)
    down, a2 = _norm_mm(h_2, gain("norm_mix", 2), W["mla_w_down"], F32, "mla_down")
    dq_raw = down[:, :MLA_Q_RANK]
    dkv_raw = down[:, MLA_Q_RANK:MLA_Q_RANK + MLA_KV_RANK]
    kr_raw = down[:, MLA_Q_RANK + MLA_KV_RANK:]
    q_lin, c_q = _norm_mm(dq_raw, P["mla_q_norm"], W["mla_w_uq"], F32, "mla_uq")
    q_all = _rope(q_lin, cos_t, sin_t, BF16, "mla_qrope", lead=D_MODEL)
    kv_all, c_kv = _norm_mm(dkv_raw, P["mla_kv_norm"], W["mla_w_ukv"], BF16, "mla_ukv")
    kr_in = jnp.concatenate([kr_raw, kr_raw, jnp.zeros((M, 64), F32)], axis=1)
    kr = _rope(kr_in, cos_t, sin_t, BF16, "mla_krope")
    q_rope = q_all[:, D_MODEL:]
    mla_o, mla_lse = _mla_fwd(q_all, kv_all, q_rope, kr, mla_scale, "mla_fwd")
    h1_2 = _mm_nn(mla_o, W["mla_w_o"], F32, "mla_out", res=h_2)
    h_3, ffn2 = ffn_fwd(h1_2, 2)

    fox_scale = HEAD_DIM ** -0.5
    fox_qkv, a3, fox_qkv_t = _norm_mm(h_3, gain("norm_mix", 3), W["fox_w_qkv"], BF16, "fox_qkv", transposed_too=True)
    f_logit = _mm_nn(a3, W["fox_w_f"], F32, "fox_f")
    b_f = jnp.pad(P["fox_b_f"], ((0, 0), (0, 128 - N_HEADS)))
    Fc = _forget_cumsum(f_logit, b_f, "fox_cumsum")
    f_rows, f_cols = _pairs_row(Fc[:, :N_HEADS]), _pairs_col(Fc[:, :N_HEADS])
    fox_o, fox_lse, fox_ox_t = _fox_fwd(fox_qkv, fox_qkv_t, f_rows, f_cols, fox_scale, "fox_fwd")
    h1_3 = _mm_nn(fox_o, W["fox_w_o"], F32, "fox_out", res=h_3)
    h_4, ffn3 = ffn_fwd(h1_3, 3)

    sq, dh, dgain = _loss_head(h_4, P["final_norm"][None, :], target, "loss_head")
    G["final_norm"] = dgain[0]

    dh = ffn_bwd(dh, ffn3, 3)
    do, do_t = _mm_nt(dh, W["fox_w_o"], BF16, "fox_do", transposed_too=True)
    G["fox_w_o"] = _mm_tn(fox_o, dh, "fox_dwo")
    dq, dk, dv, colsum = _fox_bwd(fox_qkv, fox_qkv_t, fox_ox_t, do, do_t, fox_lse, f_rows, f_cols, fox_scale,
                                    "fox_bwd")
    dlogit, db_f = _forget_cumsum_bwd(f_logit, b_f, colsum, "fox_dcumsum")
    G["fox_b_f"] = db_f[:, :N_HEADS]
    dproj = jnp.concatenate([dq, dk.astype(BF16), dv.astype(BF16), dlogit.astype(BF16)], axis=1)
    G["fox_w_qkvf"] = _mm_tn(a3, dproj, "fox_dwqkvf")
    dh, dgain = _mm_nt_dnorm([(dproj, W["fox_w_qkvf"])], h_3, gain("norm_mix", 3), dh, "fox_da_dnorm")
    G.setdefault("norm_mix", {})[3] = dgain

    dh = ffn_bwd(dh, ffn2, 2)
    do = _mm_nt(dh, W["mla_w_o"], BF16, "mla_do")
    G["mla_w_o"] = _mm_tn(mla_o, dh, "mla_dwo")
    dq, dk, dv, dqr, dkr = _mla_bwd(q_all, kv_all, q_rope, kr, mla_o, do, mla_lse, mla_scale, "mla_bwd")
    dqr = _rope(dqr, cos_t, sin_t, BF16, "mla_dqrope", inverse=True)
    dq_all = jnp.concatenate([dq, dqr], axis=1)
    dkr_sum = _rope(jnp.sum(dkr, axis=0), cos_t, sin_t, F32, "mla_dkrope", inverse=True)
    dkr_raw = dkr_sum[:, :MLA_ROPE] + dkr_sum[:, MLA_ROPE:2 * MLA_ROPE]
    dkv_all = jnp.concatenate([dk.astype(BF16), dv.astype(BF16)], axis=1)
    G["mla_w_uq"] = _mm_tn(c_q, dq_all, "mla_dwuq")
    G["mla_w_ukv"] = _mm_tn(c_kv, dkv_all, "mla_dwukv")
    ddq_raw, G["mla_q_norm"] = _mm_nt_dnorm([(dq_all, W["mla_w_uq"])], dq_raw, P["mla_q_norm"], None, "mla_dcq_dnorm")
    ddkv_raw, G["mla_kv_norm"] = _mm_nt_dnorm([(dkv_all, W["mla_w_ukv"])], dkv_raw, P["mla_kv_norm"], None,
                                              "mla_dckv_dnorm")
    ddown = jnp.concatenate([ddq_raw, ddkv_raw, dkr_raw], axis=1).astype(BF16)
    G["mla_w_down"] = _mm_tn(a2, ddown, "mla_dwdown")
    dh, dgain = _mm_nt_dnorm([(ddown, W["mla_w_down"])], h_2, gain("norm_mix", 2), dh, "mla_da_dnorm")
    G["norm_mix"][2] = dgain

    dh = ffn_bwd(dh, ffn1, 1)
    do = _mm_nt(dh, W["sb_w_o"], BF16, "sb_do")
    G["sb_w_o"] = _mm_tn(sb_o, dh, "sb_dwo")
    dq, dk, dv = _sb_bwd(sb_qkv, do, sb_tot, sb_scale, "sb_bwd")
    dqkv = jnp.concatenate([dq, dk.astype(BF16), dv.astype(BF16)], axis=1)
    G["sb_w_qkv"] = _mm_tn(a1, dqkv, "sb_dwqkv")
    dh, dgain = _mm_nt_dnorm([(dqkv, W["sb_w_qkv"])], h_1, gain("norm_mix", 1), dh, "sb_da_dnorm")
    G["norm_mix"][1] = dgain

    dh = ffn_bwd(dh, ffn0, 0)
    dpc, G["pool_w"], G["pool_scale"] = _pool_bwd_mix(dh, pooled, W["pool_w"], P["pool_scale"], "pool_dmix")
    da = _pool_bwd_window(dpc, "pool_dwindow")
    dh, dgain, dx = _norm_bwd(h0, gain("norm_mix", 0), da, dh, "mix0_dnorm", token_rows=True)
    G["norm_mix"][0] = dgain

    G["norm_mix"] = jnp.concatenate([G["norm_mix"][i] for i in range(DEPTH)], axis=0)
    G["norm_ffn"] = jnp.concatenate([G["norm_ffn"][i] for i in range(DEPTH)], axis=0)
    G["ffn_w_down"] = jnp.stack([G["ffn_w_down"][i] for i in range(DEPTH)])
    G["ffn_w_gate"] = jnp.stack([G["ffn_w_gate"][i] for i in range(DEPTH)])
    G["ffn_w_up"] = jnp.stack([G["ffn_w_up"][i] for i in range(DEPTH)])
    G["meta"] = dh[PAD:ROW0]
    return sq, dx, G


def kernel(x, meta, norm_mix, norm_ffn, pool_w, pool_scale, sb_w_qkv, sb_w_o, mla_w_down, mla_q_norm, mla_kv_norm, mla_w_uq, mla_w_ukv, mla_w_o, fox_w_qkvf, fox_b_f, fox_w_o, ffn_w_gate, ffn_w_up, ffn_w_down, final_norm, loss_target, m_meta, m_norm_mix, m_norm_ffn, m_pool_w, m_pool_scale, m_sb_w_qkv, m_sb_w_o, m_mla_w_down, m_mla_q_norm, m_mla_kv_norm, m_mla_w_uq, m_mla_w_ukv, m_mla_w_o, m_fox_w_qkvf, m_fox_b_f, m_fox_w_o, m_ffn_w_gate, m_ffn_w_up, m_ffn_w_down, m_final_norm, v_meta, v_norm_mix, v_norm_ffn, v_pool_w, v_pool_scale, v_sb_w_qkv, v_sb_w_o, v_mla_w_down, v_mla_q_norm, v_mla_kv_norm, v_mla_w_uq, v_mla_w_ukv, v_mla_w_o, v_fox_w_qkvf, v_fox_b_f, v_fox_w_o, v_ffn_w_gate, v_ffn_w_up, v_ffn_w_down, v_final_norm):
    w = dict(meta=meta, norm_mix=norm_mix, norm_ffn=norm_ffn, pool_w=pool_w, pool_scale=pool_scale,
             sb_w_qkv=sb_w_qkv, sb_w_o=sb_w_o, mla_w_down=mla_w_down, mla_q_norm=mla_q_norm,
             mla_kv_norm=mla_kv_norm, mla_w_uq=mla_w_uq, mla_w_ukv=mla_w_ukv, mla_w_o=mla_w_o,
             fox_w_qkvf=fox_w_qkvf, fox_b_f=fox_b_f, fox_w_o=fox_w_o, ffn_w_gate=ffn_w_gate, ffn_w_up=ffn_w_up,
             ffn_w_down=ffn_w_down, final_norm=final_norm)
    m = dict(meta=m_meta, norm_mix=m_norm_mix, norm_ffn=m_norm_ffn, pool_w=m_pool_w, pool_scale=m_pool_scale,
             sb_w_qkv=m_sb_w_qkv, sb_w_o=m_sb_w_o, mla_w_down=m_mla_w_down, mla_q_norm=m_mla_q_norm,
             mla_kv_norm=m_mla_kv_norm, mla_w_uq=m_mla_w_uq, mla_w_ukv=m_mla_w_ukv, mla_w_o=m_mla_w_o,
             fox_w_qkvf=m_fox_w_qkvf, fox_b_f=m_fox_b_f, fox_w_o=m_fox_w_o, ffn_w_gate=m_ffn_w_gate,
             ffn_w_up=m_ffn_w_up, ffn_w_down=m_ffn_w_down, final_norm=m_final_norm)
    v = dict(meta=v_meta, norm_mix=v_norm_mix, norm_ffn=v_norm_ffn, pool_w=v_pool_w, pool_scale=v_pool_scale,
             sb_w_qkv=v_sb_w_qkv, sb_w_o=v_sb_w_o, mla_w_down=v_mla_w_down, mla_q_norm=v_mla_q_norm,
             mla_kv_norm=v_mla_kv_norm, mla_w_uq=v_mla_w_uq, mla_w_ukv=v_mla_w_ukv, mla_w_o=v_mla_w_o,
             fox_w_qkvf=v_fox_w_qkvf, fox_b_f=v_fox_b_f, fox_w_o=v_fox_w_o, ffn_w_gate=v_ffn_w_gate,
             ffn_w_up=v_ffn_w_up, ffn_w_down=v_ffn_w_down, final_norm=v_final_norm)

    sh_names = tuple(n for n, _ in SHARDED)
    sh_axis = dict(SHARDED)
    shapes = {n: w[n].shape for n in WEIGHT_NAMES}
    wire = lambda n: F32 if n in KEPT_F32 else BF16

    gathered = _all_gather([w[n].astype(wire(n)) for n in sh_names], "gather_weights")
    full = {n: _whole_from_gathered(g, sh_axis[n]) for n, g in zip(sh_names, gathered)}
    W = _kernel_weights(full)
    P = dict(meta=full["meta"], mla_q_norm=full["mla_q_norm"], mla_kv_norm=full["mla_kv_norm"],
             norm_mix=norm_mix, norm_ffn=norm_ffn, pool_scale=pool_scale, fox_b_f=fox_b_f, final_norm=final_norm)

    sq, dx, G = _local_step(x[0], loss_target[0], W, P)
    loss = lax.psum(0.5 * jnp.sum(sq) / D_MODEL, ("x", "y", "c"))
    grad_x = dx[None]

    gw = _reference_grads(G)
    gw["meta"] = G["meta"]
    rc = {n: (int(np.prod(shapes[n][:-1])), shapes[n][-1]) for n in sh_names}
    parts = [_parts_from_whole(gw[n], sh_axis[n]).astype(wire(n)).reshape((N_DEV,) + rc[n]) for n in sh_names]
    from_sibling = _exchange_siblings(parts, "exchange_grads_d2d")
    sums = [_pair_sum(a, b, f"pair_sum_{n}") for n, a, b in zip(sh_names, parts, from_sibling)]
    landed = _exchange_chips(sums, "exchange_grads_ici")
    results = {}
    for n, got in zip(sh_names, landed):
        outs = _adamw(w[n].reshape(rc[n]), got, m[n].reshape(rc[n]), v[n].reshape(rc[n]), f"adamw_{n}")
        results[n] = [o.reshape(shapes[n]) for o in outs]

    rep_g = dict(norm_mix=G["norm_mix"], norm_ffn=G["norm_ffn"], pool_scale=G["pool_scale"], fox_b_f=G["fox_b_f"],
                 final_norm=G["final_norm"])
    (rep_all,) = _all_gather([_pack_rows(rep_g, REPLICATED)], "gather_replicated_grads")
    rep_out = _adamw(_pack_rows(w, REPLICATED), rep_all, _pack_rows(m, REPLICATED), _pack_rows(v, REPLICATED),
                     "adamw_replicated")
    rep = [_unpack_rows(o, shapes, REPLICATED) for o in rep_out]
    for n in REPLICATED:
        results[n] = [r[n] for r in rep]

    outs = [results[n][k] for k in range(4) for n in WEIGHT_NAMES]
    return (loss, grad_x, *outs)
```
